```python
import jax, jax.numpy as jnp
from jax import lax
import numpy as np

D_MODEL = 1024
BATCH = 8
SEQ = 4096
DEPTH = 1

N_META = 16
EPS = 1e-6
D_FF = 2816
CHUNK = 64
A_DK = 128
A_DV = 128
A_HEADS = D_MODEL // A_DV
A_CONV = 4
A_WK = A_HEADS * A_DK
A_WV = A_HEADS * A_DV
B_N = 64
B_HEADS = D_MODEL // B_N
B_W = B_HEADS * B_N
W_LORA = 64
AA_LORA = 64
G_LORA = 160
B_GN_EPS = B_N * 1e-5
B_COLS = 3 * B_W + W_LORA + AA_LORA + G_LORA
IN_SIZES = (A_WK, A_WK, A_WV, A_WV, A_HEADS, A_HEADS, B_COLS, D_MODEL, D_MODEL)
IN_TOTAL = sum(IN_SIZES)

kernel_name = "meta_macaron_deltanet_rwkv7_hybrid"


def _offsets(sizes):
    out, acc = [], 0
    for s in sizes[:-1]:
        acc += s
        out.append(acc)
    return out


def _rmsnorm(x, gain):
    xf = x.astype(jnp.float32)
    y = xf * lax.rsqrt(jnp.mean(xf * xf, axis=-1, keepdims=True) + EPS)
    return (y * gain.astype(jnp.float32)).astype(x.dtype)


def _l2norm(x):
    xf = x.astype(jnp.float32)
    return xf * lax.rsqrt(jnp.sum(xf * xf, axis=-1, keepdims=True) + 1e-6)


def _swiglu(x, w_gu, w_down):
    gate, up = jnp.split(x @ w_gu, 2, axis=-1)
    return (jax.nn.silu(gate) * up) @ w_down


def _causal_dwconv(x, w):
    k = w.shape[0]
    return lax.conv_general_dilated(x, w[:, None, :].astype(x.dtype), window_strides=(1,),
                                    padding=[(k - 1, 0)], dimension_numbers=('NWC', 'WIO', 'NWC'),
                                    feature_group_count=x.shape[-1])


def _gated_delta_chunked(q, k, v, beta, g):
    b, h, t, dk = q.shape
    dv = v.shape[-1]
    n = t // CHUNK

    def ch(z):
        return z.reshape((b, h, n, CHUNK) + z.shape[3:])

    q, k, v, beta, g = ch(q), ch(k), ch(v), ch(beta), ch(g)
    g = jnp.cumsum(g, axis=-1)
    kb = k * beta[..., None]
    vb = v * beta[..., None]
    idx = jnp.arange(CHUNK)
    incl = idx[:, None] >= idx[None, :]
    strict = idx[:, None] > idx[None, :]
    diff = g[..., :, None] - g[..., None, :]
    decay = jnp.where(incl, jnp.exp(jnp.where(incl, diff, 0.0)), 0.0)
    m = jnp.where(strict, jnp.einsum('bhncd,bhnsd->bhncs', kb, k) * decay, 0.0)
    eye = jnp.eye(CHUNK, dtype=m.dtype)
    tinv = lax.linalg.triangular_solve(m + eye, jnp.broadcast_to(eye, m.shape), left_side=True,
                                       lower=True, unit_diagonal=True)
    u = jnp.einsum('bhncs,bhnsd->bhncd', tinv, vb)
    wk = jnp.einsum('bhncs,bhnsd->bhncd', tinv, kb * jnp.exp(g)[..., None])
    attn = jnp.einsum('bhncd,bhnsd->bhncs', q, k) * decay
    qg = q * jnp.exp(g)[..., None]
    g_last = g[..., -1]
    k_tail = k * jnp.exp(g_last[..., None] - g)[..., None]

    def step(state, inp):
        u_i, w_i, attn_i, qg_i, kt_i, gl_i = inp
        v_new = u_i - jnp.einsum('bhcd,bhde->bhce', w_i, state)
        o = jnp.einsum('bhcd,bhde->bhce', qg_i, state) + jnp.einsum('bhcs,bhse->bhce', attn_i, v_new)
        state = state * jnp.exp(gl_i)[..., None, None] + jnp.einsum('bhcd,bhce->bhde', kt_i, v_new)
        return state, o

    xs = (jnp.moveaxis(u, 2, 0), jnp.moveaxis(wk, 2, 0), jnp.moveaxis(attn, 2, 0),
          jnp.moveaxis(qg, 2, 0), jnp.moveaxis(k_tail, 2, 0), jnp.moveaxis(g_last, 2, 0))
    s0 = jnp.zeros((b, h, dk, dv), jnp.float32)
    _, o = lax.scan(step, s0, xs)
    return jnp.moveaxis(o, 0, 2).reshape(b, h, t, dv)


def _deltanet_branch(q, k, v, z, beta_pre, alpha_pre, conv_w, log_rate, dt_bias, out_gain):
    bsz, t, _ = q.shape
    qkv = jax.nn.silu(_causal_dwconv(jnp.concatenate([q, k, v], axis=-1), conv_w))
    q, k, v = jnp.split(qkv, [A_WK, 2 * A_WK], axis=-1)
    q = _l2norm(q.reshape(bsz, t, A_HEADS, A_DK)) * (A_DK ** -0.5)
    k = _l2norm(k.reshape(bsz, t, A_HEADS, A_DK))
    v = v.reshape(bsz, t, A_HEADS, A_DV).astype(jnp.float32)
    beta = jax.nn.sigmoid(beta_pre.astype(jnp.float32))
    g = -jnp.exp(log_rate.astype(jnp.float32)) * jax.nn.softplus(
        alpha_pre.astype(jnp.float32) + dt_bias.astype(jnp.float32))
    pad = CHUNK - N_META

    def prep(a):
        a = jnp.pad(a, ((0, 0), (pad, 0)) + ((0, 0),) * (a.ndim - 2))
        return jnp.moveaxis(a, 2, 1)

    o = _gated_delta_chunked(prep(q), prep(k), prep(v), prep(beta), prep(g))
    o = jnp.moveaxis(o, 1, 2)[:, pad:]
    o = o * lax.rsqrt(jnp.mean(o * o, axis=-1, keepdims=True) + EPS) * out_gain.astype(jnp.float32)
    o = o.reshape(bsz, t, A_WV) * jax.nn.silu(z.astype(jnp.float32))
    return o.astype(z.dtype)


def _rwkv7_scan(r, w, k, v, a_vec, b_vec):
    def step(state, inp):
        r_t, w_t, k_t, v_t, a_t, b_t = inp
        sa = jnp.einsum('bhvk,bhk->bhv', state, a_t)
        state = state * w_t[:, :, None, :] + sa[..., None] * b_t[:, :, None, :] \
            + v_t[..., None] * k_t[:, :, None, :]
        return state, jnp.einsum('bhvk,bhk->bhv', state, r_t)

    bsz, t, h, n = r.shape
    xs = tuple(jnp.moveaxis(a, 1, 0) for a in (r, w, k, v, a_vec, b_vec))
    s0 = jnp.zeros((bsz, h, n, n), jnp.float32)
    _, y = lax.scan(step, s0, xs)
    return jnp.moveaxis(y, 0, 1)


def _rwkv7_branch(zb, mu, w0, w_up, a0, a_up, g_up, k_k, k_a, r_k, ln_gain, ln_bias):
    bsz, t, _ = zb.shape
    f32 = jnp.float32
    zf = zb.astype(f32)
    prev = jnp.pad(zf, ((0, 0), (1, 0), (0, 0)))[:, :-1]
    zf = zf + (prev - zf) * mu.astype(f32)
    r, k, v, wd, ad, gd = jnp.split(
        zf, [B_W, 2 * B_W, 3 * B_W, 3 * B_W + W_LORA, 3 * B_W + W_LORA + AA_LORA], axis=-1)
    w_log = -jax.nn.softplus(-(w0.astype(f32) + jnp.tanh(wd) @ w_up.astype(f32))) - 0.5
    decay = jnp.exp(-jnp.exp(w_log))
    a = jax.nn.sigmoid(a0.astype(f32) + ad @ a_up.astype(f32))
    gate = jax.nn.sigmoid(gd) @ g_up.astype(f32)
    hs = (bsz, t, B_HEADS, B_N)
    kk = _l2norm((k * k_k.astype(f32)).reshape(hs))
    k = k * (1.0 + (a - 1.0) * k_a.astype(f32))
    r, k, v, decay, a = r.reshape(hs), k.reshape(hs), v.reshape(hs), decay.reshape(hs), a.reshape(hs)
    y = _rwkv7_scan(r, decay, k, v, -kk, kk * a)
    mean = jnp.mean(y, axis=-1, keepdims=True)
    var = jnp.mean(jnp.square(y - mean), axis=-1, keepdims=True)
    y = (y - mean) * lax.rsqrt(var + B_GN_EPS) * ln_gain.astype(f32).reshape(B_HEADS, B_N) \
        + ln_bias.astype(f32).reshape(B_HEADS, B_N)
    y = y + jnp.sum(r * k * r_k.astype(f32), axis=-1, keepdims=True) * v
    return (y.reshape(bsz, t, B_W) * gate).astype(zb.dtype)


def _fwd_setup_inputs(seed: int = 0) -> dict:
    key = jax.random.key(seed)
    ks = jax.random.split(key, 32)
    f32 = jnp.float32

    def nrm(k, shape, scale):
        return jax.random.normal(k, shape, f32) * scale

    def gain(k, shape):
        return 1.0 + 0.02 * jax.random.normal(k, shape, f32)

    dt = jnp.exp(jax.random.uniform(ks[8], (DEPTH, A_HEADS), f32, np.log(1e-3), np.log(1e-1)))
    return {
        "x": nrm(ks[0], (BATCH, SEQ, D_MODEL), 1.0),
        "meta_tokens": nrm(ks[1], (N_META, D_MODEL), 1.0),
        "ffn1_norm": gain(ks[2], (DEPTH, D_MODEL)),
        "ffn1_w_gu": nrm(ks[3], (DEPTH, D_MODEL, 2 * D_FF), D_MODEL ** -0.5),
        "ffn1_w_down": nrm(ks[4], (DEPTH, D_FF, D_MODEL), D_FF ** -0.5),
        "mix_norm": gain(ks[5], (DEPTH, D_MODEL)),
        "w_in": nrm(ks[6], (DEPTH, D_MODEL, IN_TOTAL), D_MODEL ** -0.5),
        "a_conv_w": nrm(ks[7], (DEPTH, A_CONV, 2 * A_WK + A_WV), A_CONV ** -0.5),
        "a_log_rate": jnp.log(jax.random.uniform(ks[9], (DEPTH, A_HEADS), f32, 1.0, 16.0)),
        "a_dt_bias": dt + jnp.log(-jnp.expm1(-dt)),
        "a_out_norm": gain(ks[10], (DEPTH, A_DV)),
        "b_shift_mu": jax.random.uniform(ks[11], (DEPTH, B_COLS), f32, 0.0, 1.0),
        "b_w0": jax.random.uniform(ks[12], (DEPTH, B_W), f32, -6.5, -1.5),
        "b_w_up": nrm(ks[13], (DEPTH, W_LORA, B_W), 0.5 * W_LORA ** -0.5),
        "b_a0": nrm(ks[14], (DEPTH, B_W), 0.1),
        "b_a_up": nrm(ks[15], (DEPTH, AA_LORA, B_W), AA_LORA ** -0.5),
        "b_g_up": nrm(ks[16], (DEPTH, G_LORA, B_W), G_LORA ** -0.5),
        "b_k_k": 0.85 + 0.05 * jax.random.normal(ks[17], (DEPTH, B_W), f32),
        "b_k_a": 1.0 + 0.05 * jax.random.normal(ks[18], (DEPTH, B_W), f32),
        "b_r_k": nrm(ks[19], (DEPTH, B_HEADS, B_N), 0.1),
        "b_ln_gain": gain(ks[20], (DEPTH, B_W)),
        "b_ln_bias": nrm(ks[21], (DEPTH, B_W), 0.02),
        "w_out": nrm(ks[22], (DEPTH, D_MODEL, D_MODEL), D_MODEL ** -0.5),
        "ffn2_norm": gain(ks[23], (DEPTH, D_MODEL)),
        "ffn2_w_gu": nrm(ks[24], (DEPTH, D_MODEL, 2 * D_FF), D_MODEL ** -0.5),
        "ffn2_w_down": nrm(ks[25], (DEPTH, D_FF, D_MODEL), D_FF ** -0.5),
        "final_norm": gain(ks[26], (D_MODEL,)),
    }


def _fwd_reference(x, meta_tokens, ffn1_norm, ffn1_w_gu, ffn1_w_down, mix_norm, w_in, a_conv_w,
              a_log_rate, a_dt_bias, a_out_norm, b_shift_mu, b_w0, b_w_up, b_a0, b_a_up, b_g_up,
              b_k_k, b_k_a, b_r_k, b_ln_gain, b_ln_bias, w_out, ffn2_norm, ffn2_w_gu, ffn2_w_down,
              final_norm):
    bsz = x.shape[0]
    meta = jnp.broadcast_to(meta_tokens[None].astype(x.dtype), (bsz, N_META, D_MODEL))
    h = jnp.concatenate([meta, x], axis=1)
    for l in range(DEPTH):
        h = h + 0.5 * _swiglu(_rmsnorm(h, ffn1_norm[l]), ffn1_w_gu[l], ffn1_w_down[l])
        u = _rmsnorm(h, mix_norm[l])
        aq, ak, av, az, abeta, aalpha, bcols, ga, gb = jnp.split(u @ w_in[l], _offsets(IN_SIZES), axis=-1)
        o_a = _deltanet_branch(aq, ak, av, az, abeta, aalpha, a_conv_w[l], a_log_rate[l],
                               a_dt_bias[l], a_out_norm[l])
        o_b = _rwkv7_branch(bcols, b_shift_mu[l], b_w0[l], b_w_up[l], b_a0[l], b_a_up[l], b_g_up[l],
                            b_k_k[l], b_k_a[l], b_r_k[l], b_ln_gain[l], b_ln_bias[l])
        merged = jax.nn.sigmoid(ga) * o_a + jax.nn.sigmoid(gb) * o_b
        h = h + merged @ w_out[l]
        h = h + 0.5 * _swiglu(_rmsnorm(h, ffn2_norm[l]), ffn2_w_gu[l], ffn2_w_down[l])
    return _rmsnorm(h, final_norm)[:, N_META:]


import jax as _jax
import jax.numpy as _jnp

TWIN_FORMAT = 'train_step'
FWD_PARAMS = ['x', 'meta_tokens', 'ffn1_norm', 'ffn1_w_gu', 'ffn1_w_down', 'mix_norm', 'w_in', 'a_conv_w', 'a_log_rate', 'a_dt_bias', 'a_out_norm', 'b_shift_mu', 'b_w0', 'b_w_up', 'b_a0', 'b_a_up', 'b_g_up', 'b_k_k', 'b_k_a', 'b_r_k', 'b_ln_gain', 'b_ln_bias', 'w_out', 'ffn2_norm', 'ffn2_w_gu', 'ffn2_w_down', 'final_norm']
TWIN_WEIGHTS = ['meta_tokens', 'ffn1_norm', 'ffn1_w_gu', 'ffn1_w_down', 'mix_norm', 'w_in', 'a_conv_w', 'a_log_rate', 'a_dt_bias', 'a_out_norm', 'b_shift_mu', 'b_w0', 'b_w_up', 'b_a0', 'b_a_up', 'b_g_up', 'b_k_k', 'b_k_a', 'b_r_k', 'b_ln_gain', 'b_ln_bias', 'w_out', 'ffn2_norm', 'ffn2_w_gu', 'ffn2_w_down', 'final_norm']
TWIN_DIFF_INPUT = 'x'
TWIN_INPUTS = ['x', 'meta_tokens', 'ffn1_norm', 'ffn1_w_gu', 'ffn1_w_down', 'mix_norm', 'w_in', 'a_conv_w', 'a_log_rate', 'a_dt_bias', 'a_out_norm', 'b_shift_mu', 'b_w0', 'b_w_up', 'b_a0', 'b_a_up', 'b_g_up', 'b_k_k', 'b_k_a', 'b_r_k', 'b_ln_gain', 'b_ln_bias', 'w_out', 'ffn2_norm', 'ffn2_w_gu', 'ffn2_w_down', 'final_norm', 'loss_target', 'm_meta_tokens', 'm_ffn1_norm', 'm_ffn1_w_gu', 'm_ffn1_w_down', 'm_mix_norm', 'm_w_in', 'm_a_conv_w', 'm_a_log_rate', 'm_a_dt_bias', 'm_a_out_norm', 'm_b_shift_mu', 'm_b_w0', 'm_b_w_up', 'm_b_a0', 'm_b_a_up', 'm_b_g_up', 'm_b_k_k', 'm_b_k_a', 'm_b_r_k', 'm_b_ln_gain', 'm_b_ln_bias', 'm_w_out', 'm_ffn2_norm', 'm_ffn2_w_gu', 'm_ffn2_w_down', 'm_final_norm', 'v_meta_tokens', 'v_ffn1_norm', 'v_ffn1_w_gu', 'v_ffn1_w_down', 'v_mix_norm', 'v_w_in', 'v_a_conv_w', 'v_a_log_rate', 'v_a_dt_bias', 'v_a_out_norm', 'v_b_shift_mu', 'v_b_w0', 'v_b_w_up', 'v_b_a0', 'v_b_a_up', 'v_b_g_up', 'v_b_k_k', 'v_b_k_a', 'v_b_r_k', 'v_b_ln_gain', 'v_b_ln_bias', 'v_w_out', 'v_ffn2_norm', 'v_ffn2_w_gu', 'v_ffn2_w_down', 'v_final_norm']
TWIN_OUTPUTS = ['loss', 'grad_x', 'grad_meta_tokens', 'grad_ffn1_norm', 'grad_ffn1_w_gu', 'grad_ffn1_w_down', 'grad_mix_norm', 'grad_w_in', 'grad_a_conv_w', 'grad_a_log_rate', 'grad_a_dt_bias', 'grad_a_out_norm', 'grad_b_shift_mu', 'grad_b_w0', 'grad_b_w_up', 'grad_b_a0', 'grad_b_a_up', 'grad_b_g_up', 'grad_b_k_k', 'grad_b_k_a', 'grad_b_r_k', 'grad_b_ln_gain', 'grad_b_ln_bias', 'grad_w_out', 'grad_ffn2_norm', 'grad_ffn2_w_gu', 'grad_ffn2_w_down', 'grad_final_norm', 'delta_meta_tokens', 'delta_ffn1_norm', 'delta_ffn1_w_gu', 'delta_ffn1_w_down', 'delta_mix_norm', 'delta_w_in', 'delta_a_conv_w', 'delta_a_log_rate', 'delta_a_dt_bias', 'delta_a_out_norm', 'delta_b_shift_mu', 'delta_b_w0', 'delta_b_w_up', 'delta_b_a0', 'delta_b_a_up', 'delta_b_g_up', 'delta_b_k_k', 'delta_b_k_a', 'delta_b_r_k', 'delta_b_ln_gain', 'delta_b_ln_bias', 'delta_w_out', 'delta_ffn2_norm', 'delta_ffn2_w_gu', 'delta_ffn2_w_down', 'delta_final_norm', 'new_m_meta_tokens', 'new_m_ffn1_norm', 'new_m_ffn1_w_gu', 'new_m_ffn1_w_down', 'new_m_mix_norm', 'new_m_w_in', 'new_m_a_conv_w', 'new_m_a_log_rate', 'new_m_a_dt_bias', 'new_m_a_out_norm', 'new_m_b_shift_mu', 'new_m_b_w0', 'new_m_b_w_up', 'new_m_b_a0', 'new_m_b_a_up', 'new_m_b_g_up', 'new_m_b_k_k', 'new_m_b_k_a', 'new_m_b_r_k', 'new_m_b_ln_gain', 'new_m_b_ln_bias', 'new_m_w_out', 'new_m_ffn2_norm', 'new_m_ffn2_w_gu', 'new_m_ffn2_w_down', 'new_m_final_norm', 'new_v_meta_tokens', 'new_v_ffn1_norm', 'new_v_ffn1_w_gu', 'new_v_ffn1_w_down', 'new_v_mix_norm', 'new_v_w_in', 'new_v_a_conv_w', 'new_v_a_log_rate', 'new_v_a_dt_bias', 'new_v_a_out_norm', 'new_v_b_shift_mu', 'new_v_b_w0', 'new_v_b_w_up', 'new_v_b_a0', 'new_v_b_a_up', 'new_v_b_g_up', 'new_v_b_k_k', 'new_v_b_k_a', 'new_v_b_r_k', 'new_v_b_ln_gain', 'new_v_b_ln_bias', 'new_v_w_out', 'new_v_ffn2_norm', 'new_v_ffn2_w_gu', 'new_v_ffn2_w_down', 'new_v_final_norm']
TWIN_LEAF_KINDS = {'loss': 'loss', 'grad_x': 'grad_x', 'grad_meta_tokens': 'grad_w', 'grad_ffn1_norm': 'grad_w', 'grad_ffn1_w_gu': 'grad_w', 'grad_ffn1_w_down': 'grad_w', 'grad_mix_norm': 'grad_w', 'grad_w_in': 'grad_w', 'grad_a_conv_w': 'grad_w', 'grad_a_log_rate': 'grad_w', 'grad_a_dt_bias': 'grad_w', 'grad_a_out_norm': 'grad_w', 'grad_b_shift_mu': 'grad_w', 'grad_b_w0': 'grad_w', 'grad_b_w_up': 'grad_w', 'grad_b_a0': 'grad_w', 'grad_b_a_up': 'grad_w', 'grad_b_g_up': 'grad_w', 'grad_b_k_k': 'grad_w', 'grad_b_k_a': 'grad_w', 'grad_b_r_k': 'grad_w', 'grad_b_ln_gain': 'grad_w', 'grad_b_ln_bias': 'grad_w', 'grad_w_out': 'grad_w', 'grad_ffn2_norm': 'grad_w', 'grad_ffn2_w_gu': 'grad_w', 'grad_ffn2_w_down': 'grad_w', 'grad_final_norm': 'grad_w', 'delta_meta_tokens': 'delta_w', 'delta_ffn1_norm': 'delta_w', 'delta_ffn1_w_gu': 'delta_w', 'delta_ffn1_w_down': 'delta_w', 'delta_mix_norm': 'delta_w', 'delta_w_in': 'delta_w', 'delta_a_conv_w': 'delta_w', 'delta_a_log_rate': 'delta_w', 'delta_a_dt_bias': 'delta_w', 'delta_a_out_norm': 'delta_w', 'delta_b_shift_mu': 'delta_w', 'delta_b_w0': 'delta_w', 'delta_b_w_up': 'delta_w', 'delta_b_a0': 'delta_w', 'delta_b_a_up': 'delta_w', 'delta_b_g_up': 'delta_w', 'delta_b_k_k': 'delta_w', 'delta_b_k_a': 'delta_w', 'delta_b_r_k': 'delta_w', 'delta_b_ln_gain': 'delta_w', 'delta_b_ln_bias': 'delta_w', 'delta_w_out': 'delta_w', 'delta_ffn2_norm': 'delta_w', 'delta_ffn2_w_gu': 'delta_w', 'delta_ffn2_w_down': 'delta_w', 'delta_final_norm': 'delta_w', 'new_m_meta_tokens': 'new_m', 'new_m_ffn1_norm': 'new_m', 'new_m_ffn1_w_gu': 'new_m', 'new_m_ffn1_w_down': 'new_m', 'new_m_mix_norm': 'new_m', 'new_m_w_in': 'new_m', 'new_m_a_conv_w': 'new_m', 'new_m_a_log_rate': 'new_m', 'new_m_a_dt_bias': 'new_m', 'new_m_a_out_norm': 'new_m', 'new_m_b_shift_mu': 'new_m', 'new_m_b_w0': 'new_m', 'new_m_b_w_up': 'new_m', 'new_m_b_a0': 'new_m', 'new_m_b_a_up': 'new_m', 'new_m_b_g_up': 'new_m', 'new_m_b_k_k': 'new_m', 'new_m_b_k_a': 'new_m', 'new_m_b_r_k': 'new_m', 'new_m_b_ln_gain': 'new_m', 'new_m_b_ln_bias': 'new_m', 'new_m_w_out': 'new_m', 'new_m_ffn2_norm': 'new_m', 'new_m_ffn2_w_gu': 'new_m', 'new_m_ffn2_w_down': 'new_m', 'new_m_final_norm': 'new_m', 'new_v_meta_tokens': 'new_v', 'new_v_ffn1_norm': 'new_v', 'new_v_ffn1_w_gu': 'new_v', 'new_v_ffn1_w_down': 'new_v', 'new_v_mix_norm': 'new_v', 'new_v_w_in': 'new_v', 'new_v_a_conv_w': 'new_v', 'new_v_a_log_rate': 'new_v', 'new_v_a_dt_bias': 'new_v', 'new_v_a_out_norm': 'new_v', 'new_v_b_shift_mu': 'new_v', 'new_v_b_w0': 'new_v', 'new_v_b_w_up': 'new_v', 'new_v_b_a0': 'new_v', 'new_v_b_a_up': 'new_v', 'new_v_b_g_up': 'new_v', 'new_v_b_k_k': 'new_v', 'new_v_b_k_a': 'new_v', 'new_v_b_r_k': 'new_v', 'new_v_b_ln_gain': 'new_v', 'new_v_b_ln_bias': 'new_v', 'new_v_w_out': 'new_v', 'new_v_ffn2_norm': 'new_v', 'new_v_ffn2_w_gu': 'new_v', 'new_v_ffn2_w_down': 'new_v', 'new_v_final_norm': 'new_v'}


def _forward(args):
    return _fwd_reference(*[args[k] for k in FWD_PARAMS])


def _output_shape():
    def fwd():
        inp = _fwd_setup_inputs(0)
        return _fwd_reference(*[inp[k] for k in FWD_PARAMS])
    out = _jax.eval_shape(fwd)
    return out.shape, out.dtype

N_MICROBATCH = 1
ADAM_LR = 0.001
ADAM_B1 = 0.9
ADAM_B2 = 0.999
ADAM_EPS = 1e-08
ADAM_WD = 0.01
ADAM_STEP = 10
PER_EXAMPLE_BATCH_AXIS = {'x': 0, 'loss_target': 0}
SHARED_INPUTS = []
_WEIGHT_DTYPES = {'meta_tokens': _jnp.float32, 'ffn1_norm': _jnp.float32, 'ffn1_w_gu': _jnp.float32, 'ffn1_w_down': _jnp.float32, 'mix_norm': _jnp.float32, 'w_in': _jnp.float32, 'a_conv_w': _jnp.float32, 'a_log_rate': _jnp.float32, 'a_dt_bias': _jnp.float32, 'a_out_norm': _jnp.float32, 'b_shift_mu': _jnp.float32, 'b_w0': _jnp.float32, 'b_w_up': _jnp.float32, 'b_a0': _jnp.float32, 'b_a_up': _jnp.float32, 'b_g_up': _jnp.float32, 'b_k_k': _jnp.float32, 'b_k_a': _jnp.float32, 'b_r_k': _jnp.float32, 'b_ln_gain': _jnp.float32, 'b_ln_bias': _jnp.float32, 'w_out': _jnp.float32, 'ffn2_norm': _jnp.float32, 'ffn2_w_gu': _jnp.float32, 'ffn2_w_down': _jnp.float32, 'final_norm': _jnp.float32}
MOMENT_SCALE = {'meta_tokens': 5.923607e-03, 'ffn1_norm': 8.992477e-02, 'ffn1_w_gu': 3.695607e-02, 'ffn1_w_down': 6.041355e-02, 'mix_norm': 1.304664e-01, 'w_in': 4.158593e-02, 'a_conv_w': 4.161515e-02, 'a_log_rate': 3.091384e-01, 'a_dt_bias': 3.043839e-01, 'a_out_norm': 1.600121e-01, 'b_shift_mu': 7.852156e-02, 'b_w0': 1.670260e-02, 'b_w_up': 1.913956e-03, 'b_a0': 1.716846e-02, 'b_a_up': 1.627843e-02, 'b_g_up': 4.792364e-02, 'b_k_k': 5.934765e-02, 'b_k_a': 4.929459e-02, 'b_r_k': 1.030358e-01, 'b_ln_gain': 4.777268e-02, 'b_ln_bias': 4.789160e-02, 'w_out': 6.957530e-02, 'ffn2_norm': 6.640641e-02, 'ffn2_w_gu': 2.779067e-02, 'ffn2_w_down': 4.538335e-02, 'final_norm': 3.199392e+01}


def _to_microbatches(a, axis):
    t = _jnp.moveaxis(a, axis, 0)
    t = t.reshape((N_MICROBATCH, t.shape[0] // N_MICROBATCH) + t.shape[1:])
    return _jnp.moveaxis(t, 1, axis + 1)


def setup_inputs(seed: int = 0) -> dict:
    inp = _fwd_setup_inputs(seed)
    key = _jax.random.fold_in(_jax.random.key(seed), 7919)
    shape, _ = _output_shape()
    out = dict(inp)
    out["loss_target"] = _jax.random.normal(_jax.random.fold_in(key, 0), shape, _jnp.float32)
    for i, name in enumerate(TWIN_WEIGHTS):
        w = inp[name].astype(_jnp.float32)
        if MOMENT_SCALE is None:
            s = _jnp.sqrt(_jnp.mean(_jnp.square(w)) + 1e-30)
        else:
            s = MOMENT_SCALE[name]
        km, kv = _jax.random.split(_jax.random.fold_in(key, i + 1))
        out[name] = w
        out["m_" + name] = s * _jax.random.normal(km, w.shape, _jnp.float32)
        out["v_" + name] = (s * s) * _jax.random.uniform(kv, w.shape, _jnp.float32, 0.5, 1.5)
    if N_MICROBATCH > 1:
        for name, axis in PER_EXAMPLE_BATCH_AXIS.items():
            out[name] = _to_microbatches(out[name], axis)
    return {'x': out['x'], 'meta_tokens': out['meta_tokens'], 'ffn1_norm': out['ffn1_norm'], 'ffn1_w_gu': out['ffn1_w_gu'], 'ffn1_w_down': out['ffn1_w_down'], 'mix_norm': out['mix_norm'], 'w_in': out['w_in'], 'a_conv_w': out['a_conv_w'], 'a_log_rate': out['a_log_rate'], 'a_dt_bias': out['a_dt_bias'], 'a_out_norm': out['a_out_norm'], 'b_shift_mu': out['b_shift_mu'], 'b_w0': out['b_w0'], 'b_w_up': out['b_w_up'], 'b_a0': out['b_a0'], 'b_a_up': out['b_a_up'], 'b_g_up': out['b_g_up'], 'b_k_k': out['b_k_k'], 'b_k_a': out['b_k_a'], 'b_r_k': out['b_r_k'], 'b_ln_gain': out['b_ln_gain'], 'b_ln_bias': out['b_ln_bias'], 'w_out': out['w_out'], 'ffn2_norm': out['ffn2_norm'], 'ffn2_w_gu': out['ffn2_w_gu'], 'ffn2_w_down': out['ffn2_w_down'], 'final_norm': out['final_norm'], 'loss_target': out['loss_target'], 'm_meta_tokens': out['m_meta_tokens'], 'm_ffn1_norm': out['m_ffn1_norm'], 'm_ffn1_w_gu': out['m_ffn1_w_gu'], 'm_ffn1_w_down': out['m_ffn1_w_down'], 'm_mix_norm': out['m_mix_norm'], 'm_w_in': out['m_w_in'], 'm_a_conv_w': out['m_a_conv_w'], 'm_a_log_rate': out['m_a_log_rate'], 'm_a_dt_bias': out['m_a_dt_bias'], 'm_a_out_norm': out['m_a_out_norm'], 'm_b_shift_mu': out['m_b_shift_mu'], 'm_b_w0': out['m_b_w0'], 'm_b_w_up': out['m_b_w_up'], 'm_b_a0': out['m_b_a0'], 'm_b_a_up': out['m_b_a_up'], 'm_b_g_up': out['m_b_g_up'], 'm_b_k_k': out['m_b_k_k'], 'm_b_k_a': out['m_b_k_a'], 'm_b_r_k': out['m_b_r_k'], 'm_b_ln_gain': out['m_b_ln_gain'], 'm_b_ln_bias': out['m_b_ln_bias'], 'm_w_out': out['m_w_out'], 'm_ffn2_norm': out['m_ffn2_norm'], 'm_ffn2_w_gu': out['m_ffn2_w_gu'], 'm_ffn2_w_down': out['m_ffn2_w_down'], 'm_final_norm': out['m_final_norm'], 'v_meta_tokens': out['v_meta_tokens'], 'v_ffn1_norm': out['v_ffn1_norm'], 'v_ffn1_w_gu': out['v_ffn1_w_gu'], 'v_ffn1_w_down': out['v_ffn1_w_down'], 'v_mix_norm': out['v_mix_norm'], 'v_w_in': out['v_w_in'], 'v_a_conv_w': out['v_a_conv_w'], 'v_a_log_rate': out['v_a_log_rate'], 'v_a_dt_bias': out['v_a_dt_bias'], 'v_a_out_norm': out['v_a_out_norm'], 'v_b_shift_mu': out['v_b_shift_mu'], 'v_b_w0': out['v_b_w0'], 'v_b_w_up': out['v_b_w_up'], 'v_b_a0': out['v_b_a0'], 'v_b_a_up': out['v_b_a_up'], 'v_b_g_up': out['v_b_g_up'], 'v_b_k_k': out['v_b_k_k'], 'v_b_k_a': out['v_b_k_a'], 'v_b_r_k': out['v_b_r_k'], 'v_b_ln_gain': out['v_b_ln_gain'], 'v_b_ln_bias': out['v_b_ln_bias'], 'v_w_out': out['v_w_out'], 'v_ffn2_norm': out['v_ffn2_norm'], 'v_ffn2_w_gu': out['v_ffn2_w_gu'], 'v_ffn2_w_down': out['v_ffn2_w_down'], 'v_final_norm': out['v_final_norm']}


def _loss(weights, diff, rest, loss_target):
    with _jax.named_scope("forward"):
        args = {**rest, TWIN_DIFF_INPUT: diff, **{k: w.astype(_WEIGHT_DTYPES[k]) for k, w in weights.items()}}
        y = _forward(args)
    with _jax.named_scope("loss_head"):
        err = _jnp.square(y.astype(_jnp.float32) - loss_target)
        return 0.5 * _jnp.sum(_jnp.mean(err, axis=-1)) if err.ndim else 0.5 * err


def _adamw(w, g, m, v):
    m = ADAM_B1 * m + (1.0 - ADAM_B1) * g
    v = ADAM_B2 * v + (1.0 - ADAM_B2) * _jnp.square(g)
    m_hat = m / (1.0 - ADAM_B1 ** ADAM_STEP)
    v_hat = v / (1.0 - ADAM_B2 ** ADAM_STEP)
    delta = -ADAM_LR * (m_hat / (_jnp.sqrt(v_hat) + ADAM_EPS) + ADAM_WD * w)
    return delta, m, v


def reference(x, meta_tokens, ffn1_norm, ffn1_w_gu, ffn1_w_down, mix_norm, w_in, a_conv_w, a_log_rate, a_dt_bias, a_out_norm, b_shift_mu, b_w0, b_w_up, b_a0, b_a_up, b_g_up, b_k_k, b_k_a, b_r_k, b_ln_gain, b_ln_bias, w_out, ffn2_norm, ffn2_w_gu, ffn2_w_down, final_norm, loss_target, m_meta_tokens, m_ffn1_norm, m_ffn1_w_gu, m_ffn1_w_down, m_mix_norm, m_w_in, m_a_conv_w, m_a_log_rate, m_a_dt_bias, m_a_out_norm, m_b_shift_mu, m_b_w0, m_b_w_up, m_b_a0, m_b_a_up, m_b_g_up, m_b_k_k, m_b_k_a, m_b_r_k, m_b_ln_gain, m_b_ln_bias, m_w_out, m_ffn2_norm, m_ffn2_w_gu, m_ffn2_w_down, m_final_norm, v_meta_tokens, v_ffn1_norm, v_ffn1_w_gu, v_ffn1_w_down, v_mix_norm, v_w_in, v_a_conv_w, v_a_log_rate, v_a_dt_bias, v_a_out_norm, v_b_shift_mu, v_b_w0, v_b_w_up, v_b_a0, v_b_a_up, v_b_g_up, v_b_k_k, v_b_k_a, v_b_r_k, v_b_ln_gain, v_b_ln_bias, v_w_out, v_ffn2_norm, v_ffn2_w_gu, v_ffn2_w_down, v_final_norm):
    given = dict(x=x, meta_tokens=meta_tokens, ffn1_norm=ffn1_norm, ffn1_w_gu=ffn1_w_gu, ffn1_w_down=ffn1_w_down, mix_norm=mix_norm, w_in=w_in, a_conv_w=a_conv_w, a_log_rate=a_log_rate, a_dt_bias=a_dt_bias, a_out_norm=a_out_norm, b_shift_mu=b_shift_mu, b_w0=b_w0, b_w_up=b_w_up, b_a0=b_a0, b_a_up=b_a_up, b_g_up=b_g_up, b_k_k=b_k_k, b_k_a=b_k_a, b_r_k=b_r_k, b_ln_gain=b_ln_gain, b_ln_bias=b_ln_bias, w_out=w_out, ffn2_norm=ffn2_norm, ffn2_w_gu=ffn2_w_gu, ffn2_w_down=ffn2_w_down, final_norm=final_norm, loss_target=loss_target, m_meta_tokens=m_meta_tokens, m_ffn1_norm=m_ffn1_norm, m_ffn1_w_gu=m_ffn1_w_gu, m_ffn1_w_down=m_ffn1_w_down, m_mix_norm=m_mix_norm, m_w_in=m_w_in, m_a_conv_w=m_a_conv_w, m_a_log_rate=m_a_log_rate, m_a_dt_bias=m_a_dt_bias, m_a_out_norm=m_a_out_norm, m_b_shift_mu=m_b_shift_mu, m_b_w0=m_b_w0, m_b_w_up=m_b_w_up, m_b_a0=m_b_a0, m_b_a_up=m_b_a_up, m_b_g_up=m_b_g_up, m_b_k_k=m_b_k_k, m_b_k_a=m_b_k_a, m_b_r_k=m_b_r_k, m_b_ln_gain=m_b_ln_gain, m_b_ln_bias=m_b_ln_bias, m_w_out=m_w_out, m_ffn2_norm=m_ffn2_norm, m_ffn2_w_gu=m_ffn2_w_gu, m_ffn2_w_down=m_ffn2_w_down, m_final_norm=m_final_norm, v_meta_tokens=v_meta_tokens, v_ffn1_norm=v_ffn1_norm, v_ffn1_w_gu=v_ffn1_w_gu, v_ffn1_w_down=v_ffn1_w_down, v_mix_norm=v_mix_norm, v_w_in=v_w_in, v_a_conv_w=v_a_conv_w, v_a_log_rate=v_a_log_rate, v_a_dt_bias=v_a_dt_bias, v_a_out_norm=v_a_out_norm, v_b_shift_mu=v_b_shift_mu, v_b_w0=v_b_w0, v_b_w_up=v_b_w_up, v_b_a0=v_b_a0, v_b_a_up=v_b_a_up, v_b_g_up=v_b_g_up, v_b_k_k=v_b_k_k, v_b_k_a=v_b_k_a, v_b_r_k=v_b_r_k, v_b_ln_gain=v_b_ln_gain, v_b_ln_bias=v_b_ln_bias, v_w_out=v_w_out, v_ffn2_norm=v_ffn2_norm, v_ffn2_w_gu=v_ffn2_w_gu, v_ffn2_w_down=v_ffn2_w_down, v_final_norm=v_final_norm)
    weights = {n: given[n] for n in TWIN_WEIGHTS}
    shared = {n: given[n] for n in SHARED_INPUTS}
    per_example = {n: given[n] for n in ['x']}
    grad_fn = _jax.value_and_grad(_loss, argnums=(0, 1))

    def one_microbatch(ex, loss_target):
        ex = dict(ex)
        diff = ex.pop(TWIN_DIFF_INPUT)
        return grad_fn(weights, diff, {**shared, **ex}, loss_target)

    if N_MICROBATCH == 1:
        loss, (grad_w, grad_x) = one_microbatch(per_example, given["loss_target"])
    else:
        def body(carry, xs):
            loss_sum, grad_sum = carry
            l_k, (gw_k, gx_k) = one_microbatch(xs[0], xs[1])
            with _jax.named_scope("update"):
                return (loss_sum + l_k, _jax.tree.map(_jnp.add, grad_sum, gw_k)), gx_k

        init = (_jnp.zeros((), _jnp.float32), _jax.tree.map(_jnp.zeros_like, weights))
        (loss, grad_w), grad_x = _jax.lax.scan(body, init, (per_example, given["loss_target"]))
    with _jax.named_scope("update"):
        delta_w, new_m, new_v = {}, {}, {}
        for n in TWIN_WEIGHTS:
            delta_w[n], new_m[n], new_v[n] = _adamw(weights[n], grad_w[n], given["m_" + n], given["v_" + n])
    return (loss, grad_x, *[grad_w[n] for n in TWIN_WEIGHTS], *[delta_w[n] for n in TWIN_WEIGHTS],
            *[new_m[n] for n in TWIN_WEIGHTS], *[new_v[n] for n in TWIN_WEIGHTS])
```

```python
import functools

import jax
import jax.numpy as jnp
from jax import lax
from jax.experimental import pallas as pl
from jax.experimental.pallas import tpu as pltpu

F32 = jnp.float32
BF16 = jnp.bfloat16
N_DEV = 8
N_META = 16
CHUNK = 64
PAD = CHUNK - N_META
HEAD_A = 128
HEAD_B = 64
LORA_W, LORA_A, LORA_G = 64, 64, 160
LORA_PAD = 384
EPS = 1e-6
GN_EPS = HEAD_B * 1e-5
ADAM_LR, ADAM_B1, ADAM_B2, ADAM_EPS, ADAM_WD, ADAM_STEP = 0.001, 0.9, 0.999, 1e-08, 0.01, 10
SCAN_STEPS = 16
MXU_WIDTH = 256
VMEM_LIMIT = 56 * 1024 * 1024
HIGHEST = lax.Precision.HIGHEST
MESH_ID = pl.DeviceIdType.MESH
ANY = pl.BlockSpec(memory_space=pl.ANY)


def _cp(*sem):
    return pltpu.CompilerParams(dimension_semantics=sem, vmem_limit_bytes=VMEM_LIMIT)


def _tb(t, target):
    best = 16
    for d in range(16, target + 1, 16):
        if t % d == 0:
            best = d
    return best


def _sigmoid(x):
    return 1.0 / (1.0 + jnp.exp(-x))


def _silu(x):
    return x * _sigmoid(x)


def _softplus(x):
    return jnp.maximum(x, 0.0) + jnp.log(1.0 + jnp.exp(-jnp.abs(x)))


def _dot_nt(a, b, precision=None):
    return lax.dot_general(a, b, (((1,), (1,)), ((), ())), preferred_element_type=F32, precision=precision)


def _dot_tn(a, b, precision=None):
    return lax.dot_general(a, b, (((0,), (0,)), ((), ())), preferred_element_type=F32, precision=precision)


def _dot(a, b, precision=None):
    return jnp.dot(a, b, preferred_element_type=F32, precision=precision)


def _block_diag_ones():
    i = lax.broadcasted_iota(jnp.int32, (MXU_WIDTH, MXU_WIDTH), 0) // HEAD_B
    j = lax.broadcasted_iota(jnp.int32, (MXU_WIDTH, MXU_WIDTH), 1) // HEAD_B
    return (i == j).astype(BF16)


def _segsum_many(xs, bd):
    rows = xs[0].shape[0]
    parts = []
    for x in xs:
        hi = x.astype(BF16)
        parts += [hi, (x - hi.astype(F32)).astype(BF16)]
    stacked = jnp.concatenate(parts, axis=0)
    out = jnp.concatenate([_dot(stacked[:, s:s + MXU_WIDTH], bd) for s in range(0, stacked.shape[1], MXU_WIDTH)], axis=1)
    return [out[2 * i * rows:(2 * i + 1) * rows] + out[(2 * i + 1) * rows:(2 * i + 2) * rows] for i in range(len(xs))]


def _segsum_impl(x):
    return _segsum_many([x], _block_diag_ones())[0]


@jax.custom_vjp
def _segsum64(x):
    return _segsum_impl(x)


_segsum64.defvjp(lambda x: (_segsum_impl(x), None), lambda _, ct: (_segsum_impl(ct),))


def _tok(t):
    return t if isinstance(t, tuple) else (t, t.shape[1], 0)


def _tok_spec(tb, width, colblk):
    return pl.BlockSpec((tb, width), lambda i: (i, colblk))


def _par_spec(p):
    return pl.BlockSpec(p.shape, lambda i: (0, 0))


def _tok_fwd(name, fn, toks, pars, outs, tb):
    toks = [_tok(t) for t in toks]
    rows = toks[0][0].shape[0]
    n_in = len(toks) + len(pars)

    def body(*refs):
        row0 = pl.program_id(0) * tb
        res = fn(row0, *[r[...] for r in refs[:n_in]])
        for r, o in zip(refs[n_in:], res):
            r[...] = o.astype(r.dtype)

    return pl.pallas_call(
        body, name=name, grid=(rows // tb,),
        in_specs=[_tok_spec(tb, w, c) for _, w, c in toks] + [_par_spec(p) for p in pars],
        out_specs=[_tok_spec(tb, w, 0) for w, _ in outs],
        out_shape=[jax.ShapeDtypeStruct((rows, w), dt) for w, dt in outs],
        compiler_params=_cp("parallel"),
    )(*[a for a, _, _ in toks], *pars)


def _tok_bwd(name, fn, toks, pars, cts, want, tb):
    toks = [_tok(t) for t in toks]
    cts = [[_tok(c) for c in group] for group in cts]
    flat_cts = [c for group in cts for c in group]
    rows = toks[0][0].shape[0]
    n_tok, n_par, n_ct = len(toks), len(pars), len(flat_cts)

    def body(*refs):
        i = pl.program_id(0)
        row0 = i * tb
        prim = [r[...].astype(F32) for r in refs[:n_tok + n_par]]
        ct_refs = list(refs[n_tok + n_par:n_tok + n_par + n_ct])
        out_refs = refs[n_tok + n_par + n_ct:]
        res, vjp = jax.vjp(lambda *a: fn(row0, *a), *prim)
        ct = []
        for group, o in zip(cts, res):
            acc = None
            for _ in group:
                v = ct_refs.pop(0)[...].astype(F32)
                acc = v if acc is None else acc + v
            ct.append(acc.astype(o.dtype))
        grads = vjp(tuple(ct))
        for r, k in zip(out_refs[:len(want)], want):
            r[...] = grads[k]

        @pl.when(i == 0)
        def _():
            for r in out_refs[len(want):]:
                r[...] = jnp.zeros_like(r)

        for r, g in zip(out_refs[len(want):], grads[n_tok:]):
            r[...] += g

    return pl.pallas_call(
        body, name=name, grid=(rows // tb,),
        in_specs=[_tok_spec(tb, w, c) for _, w, c in toks] + [_par_spec(p) for p in pars]
        + [_tok_spec(tb, w, c) for _, w, c in flat_cts],
        out_specs=[_tok_spec(tb, toks[k][1], 0) for k in want] + [_par_spec(p) for p in pars],
        out_shape=[jax.ShapeDtypeStruct((rows, toks[k][1]), F32) for k in want]
        + [jax.ShapeDtypeStruct(p.shape, F32) for p in pars],
        compiler_params=_cp("arbitrary"),
    )(*[a for a, _, _ in toks], *pars, *[a for a, _, _ in flat_cts])


def _mm(name, a, b, *, trans_b=False, add=None, tb, tn):
    rows, k = a.shape
    n = b.shape[0] if trans_b else b.shape[1]

    def body(*refs):
        a_ref, b_ref = refs[:2]
        o_ref = refs[-1]
        acc = _dot_nt(a_ref[...], b_ref[...]) if trans_b else _dot(a_ref[...], b_ref[...])
        if add is not None:
            acc = acc + refs[2][...]
        o_ref[...] = acc

    in_specs = [pl.BlockSpec((tb, k), lambda i, j: (i, 0)),
                pl.BlockSpec((tn, k), lambda i, j: (j, 0)) if trans_b else pl.BlockSpec((k, tn), lambda i, j: (0, j))]
    args = [a, b]
    if add is not None:
        in_specs.append(pl.BlockSpec((tb, tn), lambda i, j: (i, j)))
        args.append(add)
    return pl.pallas_call(
        body, name=name, grid=(rows // tb, n // tn), in_specs=in_specs,
        out_specs=pl.BlockSpec((tb, tn), lambda i, j: (i, j)),
        out_shape=jax.ShapeDtypeStruct((rows, n), F32),
        compiler_params=_cp("parallel", "parallel"),
    )(*args)


def _mm_tn(name, a, b, *, tm, tn, tk):
    rows, m = a.shape
    n = b.shape[1]

    def body(a_ref, b_ref, o_ref):
        @pl.when(pl.program_id(2) == 0)
        def _():
            o_ref[...] = jnp.zeros_like(o_ref)

        o_ref[...] += _dot_tn(a_ref[...], b_ref[...])

    return pl.pallas_call(
        body, name=name, grid=(m // tm, n // tn, rows // tk),
        in_specs=[pl.BlockSpec((tk, tm), lambda i, j, k: (k, i)), pl.BlockSpec((tk, tn), lambda i, j, k: (k, j))],
        out_specs=pl.BlockSpec((tm, tn), lambda i, j, k: (i, j)),
        out_shape=jax.ShapeDtypeStruct((m, n), F32),
        compiler_params=_cp("parallel", "parallel", "arbitrary"),
    )(a, b)


def _col_tile(n, target):
    if n <= target:
        return n
    best = 128
    for d in range(128, target + 1, 128):
        if n % d == 0:
            best = d
    return best


def _rms(x, gain):
    return x * lax.rsqrt(jnp.mean(x * x, axis=-1, keepdims=True) + EPS) * gain


def _ffn_fwd(name, h, gain, wgu, wd, tb):
    rows, d = h.shape
    ff = wd.shape[0]
    nj = 2
    fc = ff // nj

    def body(h_ref, g_ref, wg_ref, wu_ref, wd_ref, o_ref, xn_s, acc_s):
        j = pl.program_id(1)

        @pl.when(j == 0)
        def _():
            xn_s[...] = _rms(h_ref[...], g_ref[...]).astype(BF16)
            acc_s[...] = jnp.zeros_like(acc_s)

        xn = xn_s[...]
        gate = _dot(xn, wg_ref[...])
        up = _dot(xn, wu_ref[...])
        acc_s[...] += _dot((_silu(gate) * up).astype(BF16), wd_ref[...])

        @pl.when(j == nj - 1)
        def _():
            o_ref[...] = h_ref[...] + 0.5 * acc_s[...]

    return pl.pallas_call(
        body, name=name, grid=(rows // tb, nj),
        in_specs=[pl.BlockSpec((tb, d), lambda i, j: (i, 0)), pl.BlockSpec((1, d), lambda i, j: (0, 0)),
                  pl.BlockSpec((d, fc), lambda i, j: (0, j)), pl.BlockSpec((d, fc), lambda i, j: (0, nj + j)),
                  pl.BlockSpec((fc, d), lambda i, j: (j, 0))],
        out_specs=pl.BlockSpec((tb, d), lambda i, j: (i, 0)),
        out_shape=jax.ShapeDtypeStruct((rows, d), F32),
        scratch_shapes=[pltpu.VMEM((tb, d), BF16), pltpu.VMEM((tb, d), F32)],
        compiler_params=_cp("parallel", "arbitrary"),
    )(h, gain, wgu, wgu, wd)


def _ffn_bwd(name, h, gain, dout, wgu, wd, tb):
    rows, d = h.shape
    ff = wd.shape[0]
    nj = 2
    fc = ff // nj

    def body(h_ref, g_ref, do_ref, wg_ref, wu_ref, wd_ref,
             dh_ref, dg_ref, xn_ref, act_ref, dgate_ref, dup_ref, dhalf_ref, dxn_s):
        i, j = pl.program_id(0), pl.program_id(1)

        @pl.when(j == 0)
        def _():
            xn_ref[...] = _rms(h_ref[...], g_ref[...]).astype(BF16)
            dhalf_ref[...] = (0.5 * do_ref[...]).astype(BF16)
            dxn_s[...] = jnp.zeros_like(dxn_s)

        xn = xn_ref[...]
        gate = _dot(xn, wg_ref[...])
        up = _dot(xn, wu_ref[...])
        sg = _sigmoid(gate)
        dact = _dot_nt(dhalf_ref[...], wd_ref[...])
        act_ref[...] = (gate * sg * up).astype(BF16)
        dgate = (dact * up * (sg * (1.0 + gate * (1.0 - sg)))).astype(BF16)
        dup = (dact * gate * sg).astype(BF16)
        dgate_ref[...] = dgate
        dup_ref[...] = dup
        dxn_s[...] += _dot_nt(dgate, wg_ref[...]) + _dot_nt(dup, wu_ref[...])

        @pl.when((i == 0) & (j == 0))
        def _():
            dg_ref[...] = jnp.zeros_like(dg_ref)

        @pl.when(j == nj - 1)
        def _():
            x = h_ref[...]
            r = lax.rsqrt(jnp.mean(x * x, axis=-1, keepdims=True) + EPS)
            dxn = dxn_s[...]
            dyg = dxn * g_ref[...]
            dh_ref[...] = do_ref[...] + r * dyg - x * (r * r * r) * jnp.mean(dyg * x, axis=-1, keepdims=True)
            dg_ref[...] += jnp.sum(dxn * x * r, axis=0, keepdims=True)

    row_d = pl.BlockSpec((tb, d), lambda i, j: (i, 0))
    return pl.pallas_call(
        body, name=name, grid=(rows // tb, nj),
        in_specs=[row_d, pl.BlockSpec((1, d), lambda i, j: (0, 0)), row_d,
                  pl.BlockSpec((d, fc), lambda i, j: (0, j)), pl.BlockSpec((d, fc), lambda i, j: (0, nj + j)),
                  pl.BlockSpec((fc, d), lambda i, j: (j, 0))],
        out_specs=[row_d, pl.BlockSpec((1, d), lambda i, j: (0, 0)), row_d,
                   pl.BlockSpec((tb, fc), lambda i, j: (i, j)),
                   pl.BlockSpec((tb, fc), lambda i, j: (i, j)), pl.BlockSpec((tb, fc), lambda i, j: (i, j)),
                   row_d],
        out_shape=[jax.ShapeDtypeStruct((rows, d), F32), jax.ShapeDtypeStruct((1, d), F32),
                   jax.ShapeDtypeStruct((rows, d), BF16), jax.ShapeDtypeStruct((rows, ff), BF16),
                   jax.ShapeDtypeStruct((rows, ff), BF16), jax.ShapeDtypeStruct((rows, ff), BF16),
                   jax.ShapeDtypeStruct((rows, d), BF16)],
        scratch_shapes=[pltpu.VMEM((tb, d), F32)],
        compiler_params=_cp("arbitrary", "arbitrary"),
    )(h, gain, dout, wgu, wgu, wd)


def _shift_rows(x, s):
    return pltpu.roll(x, s % x.shape[0], 0)


def _a_post(c, which):
    s = _silu(c)
    n = s * lax.rsqrt(jnp.sum(s * s, axis=-1, keepdims=True) + 1e-6)
    scale = jnp.where(which == 0, HEAD_A ** -0.5, 1.0)
    return jnp.where(which == 2, s, n * scale)


def _conv(x, w):
    return x * w[3:4] + _shift_rows(x, 1) * w[2:3] + _shift_rows(x, 2) * w[1:2] + _shift_rows(x, 3) * w[0:1]


def _a_pre_fwd(zqkv, conv_w):
    rows, width = zqkv.shape
    heads = width // (3 * HEAD_A)

    def body(x_ref, w_ref, o_ref):
        which = pl.program_id(0) // heads
        live = lax.broadcasted_iota(jnp.int32, (rows, HEAD_A), 0) >= PAD
        o_ref[...] = jnp.where(live, _a_post(_conv(x_ref[...], w_ref[...]), which), 0.0)

    return pl.pallas_call(
        body, name="a_pre_fwd", grid=(width // HEAD_A,),
        in_specs=[pl.BlockSpec((rows, HEAD_A), lambda c: (0, c)), pl.BlockSpec((4, HEAD_A), lambda c: (0, c))],
        out_specs=pl.BlockSpec((rows, HEAD_A), lambda c: (0, c)),
        out_shape=jax.ShapeDtypeStruct((rows, width), F32),
        compiler_params=_cp("parallel"),
    )(zqkv, conv_w)


def _a_pre_bwd(zqkv, conv_w, dqkv):
    rows, width = zqkv.shape
    heads = width // (3 * HEAD_A)

    def body(x_ref, w_ref, ct_ref, dx_ref, dw_ref):
        which = pl.program_id(0) // heads
        live = lax.broadcasted_iota(jnp.int32, (rows, HEAD_A), 0) >= PAD
        x, w = x_ref[...], w_ref[...]
        _, vjp = jax.vjp(lambda c: _a_post(c, which), _conv(x, w))
        (dc,) = vjp(jnp.where(live, ct_ref[...], 0.0))
        dc = jnp.where(live, dc, 0.0)
        dx_ref[...] = (dc * w[3:4] + _shift_rows(dc, -1) * w[2:3] + _shift_rows(dc, -2) * w[1:2]
                       + _shift_rows(dc, -3) * w[0:1])
        dw_ref[...] = jnp.concatenate(
            [jnp.sum(dc * (_shift_rows(x, 3 - j) if j < 3 else x), axis=0, keepdims=True) for j in range(4)], axis=0)

    col = pl.BlockSpec((rows, HEAD_A), lambda c: (0, c))
    wsp = pl.BlockSpec((4, HEAD_A), lambda c: (0, c))
    return pl.pallas_call(
        body, name="a_pre_bwd", grid=(width // HEAD_A,),
        in_specs=[col, wsp, col], out_specs=[col, wsp],
        out_shape=[jax.ShapeDtypeStruct((rows, width), F32), jax.ShapeDtypeStruct((4, width), F32)],
        compiler_params=_cp("parallel"),
    )(zqkv, conv_w, dqkv)


SHIFT_TILE = 384


def _shift_fwd(zb, mu):
    rows, width = zb.shape

    def body(x_ref, mu_ref, o_ref):
        x = x_ref[...]
        first = lax.broadcasted_iota(jnp.int32, x.shape, 0) == 0
        prev = jnp.where(first, 0.0, _shift_rows(x, 1))
        o_ref[...] = x + (prev - x) * mu_ref[...]

    col = pl.BlockSpec((rows, SHIFT_TILE), lambda c: (0, c))
    return pl.pallas_call(
        body, name="shift_fwd", grid=(width // SHIFT_TILE,),
        in_specs=[col, pl.BlockSpec((1, SHIFT_TILE), lambda c: (0, c))], out_specs=col,
        out_shape=jax.ShapeDtypeStruct((rows, width), F32), compiler_params=_cp("parallel"),
    )(zb, mu)


def _shift_bwd(zb, mu, dzf):
    rows, width = zb.shape

    def body(x_ref, mu_ref, ct_ref, dx_ref, dmu_ref):
        x, ct, mu_v = x_ref[...], ct_ref[...], mu_ref[...]
        row = lax.broadcasted_iota(jnp.int32, x.shape, 0)
        prev = jnp.where(row == 0, 0.0, _shift_rows(x, 1))
        nxt = jnp.where(row == rows - 1, 0.0, _shift_rows(ct, -1))
        dx_ref[...] = ct * (1.0 - mu_v) + nxt * mu_v
        dmu_ref[...] = jnp.sum(ct * (prev - x), axis=0, keepdims=True)

    col = pl.BlockSpec((rows, SHIFT_TILE), lambda c: (0, c))
    msp = pl.BlockSpec((1, SHIFT_TILE), lambda c: (0, c))
    return pl.pallas_call(
        body, name="shift_bwd", grid=(width // SHIFT_TILE,),
        in_specs=[col, msp, col], out_specs=[col, msp],
        out_shape=[jax.ShapeDtypeStruct((rows, width), F32), jax.ShapeDtypeStruct((1, width), F32)],
        compiler_params=_cp("parallel"),
    )(zb, mu, dzf)


def _dn_chunk(q, k, v, beta, g, state):
    ri = lax.broadcasted_iota(jnp.int32, (CHUNK, CHUNK), 0)
    ci = lax.broadcasted_iota(jnp.int32, (CHUNK, CHUNK), 1)
    eye = (ri == ci).astype(F32)
    incl = ri >= ci
    g_row = jnp.sum(g * eye, axis=0, keepdims=True)
    gc = jnp.sum(jnp.where(incl, g_row, 0.0), axis=1, keepdims=True)
    gc_row = jnp.sum(gc * eye, axis=0, keepdims=True)
    decay = jnp.where(incl, jnp.exp(jnp.where(incl, gc - gc_row, 0.0)), 0.0)
    kb = k * beta
    vb = v * beta
    m = jnp.where(ri > ci, _dot_nt(kb, k, HIGHEST) * decay, 0.0)
    p = -m
    tinv = eye + p
    for _ in range(5):
        p = _dot(p, p, HIGHEST)
        tinv = tinv + _dot(tinv, p, HIGHEST)
    eg = jnp.exp(gc)
    u = _dot(tinv, vb, HIGHEST)
    wk = _dot(tinv, kb * eg, HIGHEST)
    attn = _dot_nt(q, k, HIGHEST) * decay
    g_last = jnp.sum(jnp.where(lax.broadcasted_iota(jnp.int32, (CHUNK, 1), 0) == CHUNK - 1, gc, 0.0),
                     axis=0, keepdims=True)
    k_tail = k * jnp.exp(g_last - gc)
    v_new = u - _dot(wk, state, HIGHEST)
    o = _dot(q * eg, state, HIGHEST) + _dot(attn, v_new, HIGHEST)
    return o, state * jnp.exp(g_last) + _dot_tn(k_tail, v_new, HIGHEST)


def _bg_cols(bg, h, heads):
    lane = lax.broadcasted_iota(jnp.int32, bg.shape, 1)
    beta = jnp.sum(jnp.where(lane == h, bg, 0.0), axis=1, keepdims=True)
    g = jnp.sum(jnp.where(lane == heads + h, bg, 0.0), axis=1, keepdims=True)
    return beta, g


def _dn_fwd(qkv, bg):
    rows = qkv.shape[0]
    heads = qkv.shape[1] // (3 * HEAD_A)
    n = rows // CHUNK

    def body(q_ref, k_ref, v_ref, bg_ref, o_ref, hist_ref, s_ref):
        c, h = pl.program_id(0), pl.program_id(1)

        @pl.when(c == 0)
        def _():
            s_ref[h] = jnp.zeros((HEAD_A, HEAD_A), F32)

        state = s_ref[h]
        hist_ref[0, 0] = state
        beta, g = _bg_cols(bg_ref[...], h, heads)
        o, new = _dn_chunk(q_ref[...], k_ref[...], v_ref[...], beta, g, state)
        o_ref[...] = o
        s_ref[h] = new

    def part(p):
        return pl.BlockSpec((CHUNK, HEAD_A), lambda c, h: (c, p * heads + h))

    return pl.pallas_call(
        body, name="deltanet_fwd", grid=(n, heads),
        in_specs=[part(0), part(1), part(2), pl.BlockSpec((CHUNK, 128), lambda c, h: (c, 0))],
        out_specs=[pl.BlockSpec((CHUNK, HEAD_A), lambda c, h: (c, h)),
                   pl.BlockSpec((1, 1, HEAD_A, HEAD_A), lambda c, h: (c, h, 0, 0))],
        out_shape=[jax.ShapeDtypeStruct((rows, heads * HEAD_A), F32),
                   jax.ShapeDtypeStruct((n, heads, HEAD_A, HEAD_A), F32)],
        scratch_shapes=[pltpu.VMEM((heads, HEAD_A, HEAD_A), F32)],
        compiler_params=_cp("arbitrary", "arbitrary"),
    )(qkv, qkv, qkv, bg)


def _dn_bwd(qkv, bg, hist, do):
    rows = qkv.shape[0]
    heads = qkv.shape[1] // (3 * HEAD_A)
    n = rows // CHUNK

    def body(q_ref, k_ref, v_ref, bg_ref, hist_ref, do_ref, dq_ref, dk_ref, dv_ref, dbg_ref, ds_ref):
        c, h = pl.program_id(0), pl.program_id(1)

        @pl.when(c == 0)
        def _():
            ds_ref[h] = jnp.zeros((HEAD_A, HEAD_A), F32)

        beta, g = _bg_cols(bg_ref[...], h, heads)
        _, vjp = jax.vjp(_dn_chunk, q_ref[...], k_ref[...], v_ref[...], beta, g, hist_ref[0, 0])
        dq, dk, dv, dbeta, dg, ds = vjp((do_ref[...], ds_ref[h]))
        dq_ref[...] = dq
        dk_ref[...] = dk
        dv_ref[...] = dv
        ds_ref[h] = ds
        lane = lax.broadcasted_iota(jnp.int32, (CHUNK, 128), 1)

        @pl.when(h == 0)
        def _():
            dbg_ref[...] = jnp.zeros_like(dbg_ref)

        dbg_ref[...] += jnp.where(lane == h, dbeta, 0.0) + jnp.where(lane == heads + h, dg, 0.0)

    def part(p):
        return pl.BlockSpec((CHUNK, HEAD_A), lambda c, h: (n - 1 - c, p * heads + h))

    return pl.pallas_call(
        body, name="deltanet_bwd", grid=(n, heads),
        in_specs=[part(0), part(1), part(2), pl.BlockSpec((CHUNK, 128), lambda c, h: (n - 1 - c, 0)),
                  pl.BlockSpec((1, 1, HEAD_A, HEAD_A), lambda c, h: (n - 1 - c, h, 0, 0)),
                  pl.BlockSpec((CHUNK, HEAD_A), lambda c, h: (n - 1 - c, h))],
        out_specs=[pl.BlockSpec((CHUNK, HEAD_A), lambda c, h: (n - 1 - c, h))] * 3
        + [pl.BlockSpec((CHUNK, 128), lambda c, h: (n - 1 - c, 0))],
        out_shape=[jax.ShapeDtypeStruct((rows, heads * HEAD_A), F32)] * 3 + [jax.ShapeDtypeStruct((rows, 128), F32)],
        scratch_shapes=[pltpu.VMEM((heads, HEAD_A, HEAD_A), F32)],
        compiler_params=_cp("arbitrary", "arbitrary"),
    )(qkv, qkv, qkv, bg, hist, do)


def _diag_mask(width):
    return (lax.broadcasted_iota(jnp.int32, (HEAD_B, width), 0)
            == lax.broadcasted_iota(jnp.int32, (HEAD_B, width), 1) % HEAD_B).astype(F32)


def _rwkv_fwd(r, w, k, v, a, b):
    rows, width = r.shape
    ts = SCAN_STEPS

    def body(r_ref, w_ref, k_ref, v_ref, a_ref, b_ref, y_ref, hist_ref, s_ref):
        @pl.when(pl.program_id(0) == 0)
        def _():
            s_ref[...] = jnp.zeros_like(s_ref)

        eye = _diag_mask(width)
        bd = _block_diag_ones()

        for j in range(ts):
            row = pl.ds(j, 1)
            s = s_ref[...]
            hist_ref[j] = s
            sa, vb = _segsum_many([s * a_ref[row, :], eye * v_ref[row, :]], bd)
            s = s * w_ref[row, :] + sa * b_ref[row, :] + vb * k_ref[row, :]
            (yb,) = _segsum_many([s * r_ref[row, :]], bd)
            y_ref[row, :] = jnp.sum(yb * eye, axis=0, keepdims=True)
            s_ref[...] = s

    blk = pl.BlockSpec((ts, width), lambda i: (i, 0))
    return pl.pallas_call(
        body, name="rwkv_fwd", grid=(rows // ts,),
        in_specs=[blk] * 6,
        out_specs=[blk, pl.BlockSpec((ts, HEAD_B, width), lambda i: (i, 0, 0))],
        out_shape=[jax.ShapeDtypeStruct((rows, width), F32), jax.ShapeDtypeStruct((rows, HEAD_B, width), F32)],
        scratch_shapes=[pltpu.VMEM((HEAD_B, width), F32)],
        compiler_params=_cp("arbitrary"),
    )(r, w, k, v, a, b)


def _rwkv_bwd(r, w, k, v, a, b, hist, dy):
    rows, width = r.shape
    ts = SCAN_STEPS
    nb = rows // ts

    def body(r_ref, w_ref, k_ref, v_ref, a_ref, b_ref, hist_ref, dy_ref,
             dr_ref, dw_ref, dk_ref, dv_ref, da_ref, db_ref, g_ref):
        @pl.when(pl.program_id(0) == 0)
        def _():
            g_ref[...] = jnp.zeros_like(g_ref)

        eye = _diag_mask(width)
        bd = _block_diag_ones()

        def colsum(x):
            return jnp.sum(x, axis=0, keepdims=True)

        for j in reversed(range(ts)):
            row = pl.ds(j, 1)
            sp = hist_ref[j]
            a_t, b_t, k_t, w_t, r_t = a_ref[row, :], b_ref[row, :], k_ref[row, :], w_ref[row, :], r_ref[row, :]
            sa, vb, dyb = _segsum_many([sp * a_t, eye * v_ref[row, :], eye * dy_ref[row, :]], bd)
            st = sp * w_t + sa * b_t + vb * k_t
            g = g_ref[...] + dyb * r_t
            dr_ref[row, :] = colsum(st * dyb)
            dw_ref[row, :] = colsum(g * sp)
            dk_ref[row, :] = colsum(g * vb)
            db_ref[row, :] = colsum(g * sa)
            dsa, dvb = _segsum_many([g * b_t, g * k_t], bd)
            dv_ref[row, :] = colsum(dvb * eye)
            da_ref[row, :] = colsum(sp * dsa)
            g_ref[...] = g * w_t + dsa * a_t

    blk = pl.BlockSpec((ts, width), lambda i: (nb - 1 - i, 0))
    return pl.pallas_call(
        body, name="rwkv_bwd", grid=(nb,),
        in_specs=[blk] * 6 + [pl.BlockSpec((ts, HEAD_B, width), lambda i: (nb - 1 - i, 0, 0)), blk],
        out_specs=[blk] * 6,
        out_shape=[jax.ShapeDtypeStruct((rows, width), F32)] * 6,
        scratch_shapes=[pltpu.VMEM((HEAD_B, width), F32)],
        compiler_params=_cp("arbitrary"),
    )(r, w, k, v, a, b, hist, dy)


def _live(row0, shape):
    return (row0 + lax.broadcasted_iota(jnp.int32, shape, 0)) >= PAD


def _norm_fn(row0, h, gain):
    return (_rms(h, gain),)


def _norm_res_fn(row0, h, gain):
    return _rms(h, gain), h


def _make_bg_fn(heads):
    def fn(row0, x, log_rate, dt_bias):
        lane = lax.broadcasted_iota(jnp.int32, x.shape, 1)
        beta = _sigmoid(x)
        g = -jnp.exp(log_rate) * _softplus(x + dt_bias)
        out = jnp.where(lane < heads, beta, jnp.where(lane < 2 * heads, g, 0.0))
        return (jnp.where(_live(row0, x.shape), out, 0.0),)
    return fn


def _b_pre_fn(row0, zf, w0, w_up, a0, a_up, g_up, k_k, k_a):
    d = w0.shape[1]
    r, k, v = zf[:, :d], zf[:, d:2 * d], zf[:, 2 * d:3 * d]
    lo = zf[:, 3 * d:3 * d + 128]
    lg = zf[:, 3 * d + 128:3 * d + LORA_PAD]
    lane = lax.broadcasted_iota(jnp.int32, lo.shape, 1)
    lw = _dot(jnp.where(lane < LORA_W, jnp.tanh(lo), 0.0), w_up)
    la = _dot(jnp.where(lane >= LORA_W, lo, 0.0), a_up)
    lane_g = lax.broadcasted_iota(jnp.int32, lg.shape, 1)
    gate = _dot(jnp.where(lane_g < LORA_G, _sigmoid(lg), 0.0), g_up)
    decay = jnp.exp(-jnp.exp(-_softplus(-(w0 + lw)) - 0.5))
    a = _sigmoid(a0 + la)
    kx = k * k_k
    kk = kx * lax.rsqrt(_segsum64(kx * kx) + 1e-6)
    k2 = k * (1.0 + (a - 1.0) * k_a)
    return r, decay, k2, v, -kk, kk * a, gate


def _post_fn(row0, o, az, ga, gb, y, r, k2, v, gate, out_gain, r_k, ln_g, ln_b):
    heads = o.shape[1] // HEAD_A
    parts = []
    for h in range(heads):
        oh = o[:, h * HEAD_A:(h + 1) * HEAD_A]
        parts.append(oh * lax.rsqrt(jnp.mean(oh * oh, axis=-1, keepdims=True) + EPS) * out_gain)
    o_a = jnp.concatenate(parts, axis=1) * _silu(az)
    mean = _segsum64(y) * (1.0 / HEAD_B)
    yc = y - mean
    var = _segsum64(yc * yc) * (1.0 / HEAD_B)
    yn = yc * lax.rsqrt(var + GN_EPS) * ln_g + ln_b
    o_b = (yn + _segsum64(r * k2 * r_k) * v) * gate
    return (_sigmoid(ga) * o_a + _sigmoid(gb) * o_b,)


def _loss(h3, target, gain, tb):
    rows, d = h3.shape

    def body(h_ref, t_ref, g_ref, dh_ref, dg_ref, l_ref):
        i = pl.program_id(0)
        live = (i * tb + lax.broadcasted_iota(jnp.int32, (tb, 1), 0)) >= CHUNK
        tgt = t_ref[...]

        def f(h, g):
            err = _rms(h, g) - tgt
            return 0.5 * jnp.sum(jnp.where(live, jnp.mean(err * err, axis=-1, keepdims=True), 0.0))

        val, vjp = jax.vjp(f, h_ref[...], g_ref[...])
        dh, dg = vjp(jnp.ones((), F32))
        dh_ref[...] = dh

        @pl.when(i == 0)
        def _():
            dg_ref[...] = jnp.zeros_like(dg_ref)
            l_ref[...] = jnp.zeros_like(l_ref)

        dg_ref[...] += dg
        l_ref[...] += jnp.full((1, 128), val, F32)

    blk = pl.BlockSpec((tb, d), lambda i: (i, 0))
    return pl.pallas_call(
        body, name="loss", grid=(rows // tb,),
        in_specs=[blk, blk, pl.BlockSpec((1, d), lambda i: (0, 0))],
        out_specs=[blk, pl.BlockSpec((1, d), lambda i: (0, 0)), pl.BlockSpec((1, 128), lambda i: (0, 0))],
        out_shape=[jax.ShapeDtypeStruct((rows, d), F32), jax.ShapeDtypeStruct((1, d), F32),
                   jax.ShapeDtypeStruct((1, 128), F32)],
        compiler_params=_cp("arbitrary"),
    )(h3, target, gain)


def _adamw_math(w, g, m, v):
    m2 = ADAM_B1 * m + (1.0 - ADAM_B1) * g
    v2 = ADAM_B2 * v + (1.0 - ADAM_B2) * (g * g)
    m_hat = m2 / (1.0 - ADAM_B1 ** ADAM_STEP)
    v_hat = v2 / (1.0 - ADAM_B2 ** ADAM_STEP)
    return -ADAM_LR * (m_hat / (jnp.sqrt(v_hat) + ADAM_EPS) + ADAM_WD * w), m2, v2


def _adamw(name, slabs, w, m, v, rb):
    rows, cols = w.shape

    def body(s_ref, w_ref, m_ref, v_ref, g_ref, d_ref, m2_ref, v2_ref):
        g = s_ref[0]
        for dev in range(1, N_DEV):
            g = g + s_ref[dev]
        g_ref[...] = g
        d_ref[...], m2_ref[...], v2_ref[...] = _adamw_math(w_ref[...], g, m_ref[...], v_ref[...])

    blk = pl.BlockSpec((rb, cols), lambda i: (i, 0))
    return pl.pallas_call(
        body, name=name, grid=(rows // rb,),
        in_specs=[pl.BlockSpec((N_DEV, rb, cols), lambda i: (0, i, 0)), blk, blk, blk],
        out_specs=[blk] * 4, out_shape=[jax.ShapeDtypeStruct((rows, cols), F32)] * 4,
        compiler_params=_cp("parallel"),
    )(slabs, w, m, v)


def _sum_slabs(name, slabs, rb):
    _, rows, cols = slabs.shape

    def body(s_ref, o_ref):
        g = s_ref[0]
        for dev in range(1, N_DEV):
            g = g + s_ref[dev]
        o_ref[...] = g

    return pl.pallas_call(
        body, name=name, grid=(rows // rb,),
        in_specs=[pl.BlockSpec((N_DEV, rb, cols), lambda i: (0, i, 0))],
        out_specs=pl.BlockSpec((rb, cols), lambda i: (i, 0)),
        out_shape=jax.ShapeDtypeStruct((rows, cols), F32), compiler_params=_cp("parallel"),
    )(slabs)


def _adamw_small(w, g, m, v):
    def body(w_ref, g_ref, m_ref, v_ref, d_ref, m2_ref, v2_ref):
        d_ref[...], m2_ref[...], v2_ref[...] = _adamw_math(w_ref[...], g_ref[...], m_ref[...], v_ref[...])

    return pl.pallas_call(body, name="adamw_small", out_shape=[jax.ShapeDtypeStruct(w.shape, F32)] * 3)(w, g, m, v)


def _place():
    return lax.axis_index("x"), lax.axis_index("y"), lax.axis_index("c")


def _index(p):
    return 4 * p[0] + 2 * p[1] + p[2]


def _all_gather(name, xs):
    n = len(xs)

    def body(*refs):
        x_refs, o_refs = refs[:n], refs[n:2 * n]
        send_sems, recv_sems, local_sems = refs[2 * n:]
        x, y, c = _place()
        me, sibling = (x, y, c), (x, y, 1 - c)
        chips = [(1 - x, y), (x, 1 - y), (1 - x, 1 - y)]

        def copy(i, k, block, to, src=None):
            dst = o_refs[i].at[_index(block)]
            return pltpu.make_async_remote_copy(src_ref=dst if src is None else src, dst_ref=dst,
                                                send_sem=send_sems.at[i, k], recv_sem=recv_sems.at[i, k],
                                                device_id=to, device_id_type=MESH_ID)

        mine = [pltpu.make_async_copy(x_refs[i], o_refs[i].at[_index(me)], local_sems.at[i]) for i in range(n)]
        for cp in mine:
            cp.start()
        first = []
        for i in range(n):
            first.append(copy(i, 0, me, sibling, src=x_refs[i]))
            first += [copy(i, 1 + j, me, (*chip, c), src=x_refs[i]) for j, chip in enumerate(chips)]
        for cp in first:
            cp.start()
        passed = []
        for j, chip in enumerate(chips):
            for i in range(n):
                copy(i, 1 + j, (*chip, c), me).wait_recv()
                cp = copy(i, 4 + j, (*chip, c), sibling)
                cp.start()
                passed.append(cp)
        for i in range(n):
            copy(i, 0, sibling, me).wait_recv()
            for j, chip in enumerate(chips):
                copy(i, 4 + j, (*chip, 1 - c), me).wait_recv()
        for cp in first + passed:
            cp.wait_send()
        for cp in mine:
            cp.wait()

    return pl.pallas_call(
        body, name=name, in_specs=[ANY] * n, out_specs=[ANY] * n,
        out_shape=[jax.ShapeDtypeStruct((N_DEV,) + x.shape, x.dtype) for x in xs],
        scratch_shapes=[pltpu.SemaphoreType.DMA((n, 7)), pltpu.SemaphoreType.DMA((n, 7)), pltpu.SemaphoreType.DMA((n,))],
    )(*xs)


def _exchange(name, xs):
    n = len(xs)

    def body(*refs):
        x_refs, o_refs = refs[:n], refs[n:2 * n]
        send_sems, recv_sems, local_sems = refs[2 * n:]
        x, y, c = _place()
        me = (x, y, c)
        peers = [((1 - x) if k & 4 else x, (1 - y) if k & 2 else y, (1 - c) if k & 1 else c) for k in range(1, N_DEV)]

        def copy(i, k, peer):
            return pltpu.make_async_remote_copy(src_ref=x_refs[i].at[_index(peer)], dst_ref=o_refs[i].at[_index(me)],
                                                send_sem=send_sems.at[i, k], recv_sem=recv_sems.at[i, k],
                                                device_id=peer, device_id_type=MESH_ID)

        def arrival(i, k, peer):
            return pltpu.make_async_remote_copy(src_ref=x_refs[i].at[_index(peer)], dst_ref=o_refs[i].at[_index(peer)],
                                                send_sem=send_sems.at[i, k], recv_sem=recv_sems.at[i, k],
                                                device_id=peer, device_id_type=MESH_ID)

        mine = [pltpu.make_async_copy(x_refs[i].at[_index(me)], o_refs[i].at[_index(me)], local_sems.at[i])
                for i in range(n)]
        for cp in mine:
            cp.start()
        sent = [copy(i, k, peer) for k, peer in enumerate(peers) for i in range(n)]
        for cp in sent:
            cp.start()
        for k, peer in enumerate(peers):
            for i in range(n):
                arrival(i, k, peer).wait_recv()
        for cp in sent:
            cp.wait_send()
        for cp in mine:
            cp.wait()

    return pl.pallas_call(
        body, name=name, in_specs=[ANY] * n, out_specs=[ANY] * n,
        out_shape=[jax.ShapeDtypeStruct(x.shape, x.dtype) for x in xs],
        scratch_shapes=[pltpu.SemaphoreType.DMA((n, 7)), pltpu.SemaphoreType.DMA((n, 7)), pltpu.SemaphoreType.DMA((n,))],
    )(*xs)


def _pack(arrays):
    flat = jnp.concatenate([a.reshape(-1) for a in arrays])
    pad = (-flat.shape[0]) % 1024
    return jnp.pad(flat, (0, pad)).reshape(-1, 128)


def _unpack(packed, shapes):
    flat = packed.reshape(-1)
    out, pos = [], 0
    for s in shapes:
        size = 1
        for dim in s:
            size *= dim
        out.append(flat[pos:pos + size].reshape(s))
        pos += size
    return out


def _cols_from_slabs(stack):
    return jnp.transpose(stack, (1, 0, 2)).reshape(stack.shape[1], -1)


def _cols_to_slabs(full):
    return jnp.transpose(full.reshape(full.shape[0], N_DEV, -1), (1, 0, 2))


def kernel(x, meta_tokens, ffn1_norm, ffn1_w_gu, ffn1_w_down, mix_norm, w_in, a_conv_w, a_log_rate, a_dt_bias, a_out_norm, b_shift_mu, b_w0, b_w_up, b_a0, b_a_up, b_g_up, b_k_k, b_k_a, b_r_k, b_ln_gain, b_ln_bias, w_out, ffn2_norm, ffn2_w_gu, ffn2_w_down, final_norm, loss_target, m_meta_tokens, m_ffn1_norm, m_ffn1_w_gu, m_ffn1_w_down, m_mix_norm, m_w_in, m_a_conv_w, m_a_log_rate, m_a_dt_bias, m_a_out_norm, m_b_shift_mu, m_b_w0, m_b_w_up, m_b_a0, m_b_a_up, m_b_g_up, m_b_k_k, m_b_k_a, m_b_r_k, m_b_ln_gain, m_b_ln_bias, m_w_out, m_ffn2_norm, m_ffn2_w_gu, m_ffn2_w_down, m_final_norm, v_meta_tokens, v_ffn1_norm, v_ffn1_w_gu, v_ffn1_w_down, v_mix_norm, v_w_in, v_a_conv_w, v_a_log_rate, v_a_dt_bias, v_a_out_norm, v_b_shift_mu, v_b_w0, v_b_w_up, v_b_a0, v_b_a_up, v_b_g_up, v_b_k_k, v_b_k_a, v_b_r_k, v_b_ln_gain, v_b_ln_bias, v_w_out, v_ffn2_norm, v_ffn2_w_gu, v_ffn2_w_down, v_final_norm):
    names = ['meta_tokens', 'ffn1_norm', 'ffn1_w_gu', 'ffn1_w_down', 'mix_norm', 'w_in', 'a_conv_w', 'a_log_rate',
             'a_dt_bias', 'a_out_norm', 'b_shift_mu', 'b_w0', 'b_w_up', 'b_a0', 'b_a_up', 'b_g_up', 'b_k_k', 'b_k_a',
             'b_r_k', 'b_ln_gain', 'b_ln_bias', 'w_out', 'ffn2_norm', 'ffn2_w_gu', 'ffn2_w_down', 'final_norm']
    env = dict(locals())
    wts = {k: env[k] for k in names}
    mom_m = {k: env['m_' + k] for k in names}
    mom_v = {k: env['v_' + k] for k in names}
    big = ['ffn1_w_gu', 'ffn1_w_down', 'w_in', 'w_out', 'ffn2_w_gu', 'ffn2_w_down']
    col_sharded = {'ffn1_w_gu', 'w_in', 'ffn2_w_gu'}
    small_sharded = ['meta_tokens', 'a_conv_w', 'b_w_up', 'b_a_up', 'b_g_up']
    replicated = [k for k in names if k not in big and k not in small_sharded]

    seq, d = x.shape[1], x.shape[2]
    rows = PAD + N_META + seq
    heads_a = d // HEAD_A
    tb_mm = _tb(rows, 416)
    tb_vjp = _tb(rows, 208)
    me = _index(_place())

    big_local = [wts[k][0].astype(BF16) for k in big]
    small_local = [wts['meta_tokens']] + [wts[k][0] for k in small_sharded[1:]]
    gathered = _all_gather("gather_weights", big_local + small_local)
    gw = dict(zip(big + small_sharded, gathered))
    full = {k: (_cols_from_slabs(gw[k]) if k in col_sharded else gw[k].reshape(-1, gw[k].shape[-1])) for k in big}
    for k in small_sharded:
        full[k] = _cols_from_slabs(gw[k])
    for k in replicated:
        full[k] = wts[k].reshape(1, -1)

    win = full['w_in']
    n_b = 3 * d + LORA_W + LORA_A + LORA_G
    off_beta, off_b = 4 * d, 4 * d + 2 * heads_a
    off_ga = off_b + n_b
    b_width = 3 * d + LORA_PAD
    zcols = lambda r, c: jnp.zeros((r, c), BF16)
    w_qkv = win[:, :3 * d]
    w_zg = jnp.concatenate([win[:, 3 * d:4 * d], win[:, off_ga:off_ga + 2 * d]], axis=1)
    w_b = jnp.concatenate([win[:, off_b:off_b + n_b], zcols(d, b_width - n_b)], axis=1)
    w_bg = jnp.concatenate([win[:, off_beta:off_beta + 2 * heads_a], zcols(d, 128 - 2 * heads_a)], axis=1)

    def lanes(vec, start, width):
        return jnp.pad(vec.reshape(1, -1), ((0, 0), (start, width - start - vec.size)))

    log_rate = lanes(wts['a_log_rate'], heads_a, 128)
    dt_bias = lanes(wts['a_dt_bias'], heads_a, 128)
    mu = lanes(wts['b_shift_mu'], 0, b_width)
    w_up = jnp.pad(full['b_w_up'], ((0, 128 - LORA_W), (0, 0)))
    a_up = jnp.pad(full['b_a_up'], ((LORA_W, 0), (0, 0)))
    g_up = jnp.pad(full['b_g_up'], ((0, 256 - LORA_G), (0, 0)))
    b_pars = [full['b_w0'], w_up, full['b_a0'], a_up, g_up, full['b_k_k'], full['b_k_a']]
    post_pars = [full['a_out_norm'], full['b_r_k'], full['b_ln_gain'], full['b_ln_bias']]
    bg_fn = _make_bg_fn(heads_a)

    h0 = jnp.concatenate([jnp.zeros((PAD, d), F32), full['meta_tokens'], x[0]], axis=0)
    h1 = _ffn_fwd("ffn1_fwd", h0, full['ffn1_norm'], full['ffn1_w_gu'], full['ffn1_w_down'], tb_mm)
    (u,) = _tok_fwd("mix_norm_fwd", _norm_fn, [h1], [full['mix_norm']], [(d, BF16)], tb_mm)
    z_qkv = _mm("in_qkv", u, w_qkv, tb=tb_mm, tn=_col_tile(3 * d, 1536))
    z_zg = _mm("in_zg", u, w_zg, tb=tb_mm, tn=_col_tile(3 * d, 1536))
    z_b = _mm("in_b", u, w_b, tb=tb_mm, tn=_col_tile(b_width, 1536))
    z_bg = _mm("in_bg", u, w_bg, tb=tb_mm, tn=128)
    qkv = _a_pre_fwd(z_qkv, full['a_conv_w'])
    (bg,) = _tok_fwd("bg_fwd", bg_fn, [z_bg], [log_rate, dt_bias], [(128, F32)], tb_mm)
    o_dn, dn_hist = _dn_fwd(qkv, bg)
    zf = _shift_fwd(z_b, mu)
    rr, ww, kk2, vv, av, bv, gate = _tok_fwd("b_pre_fwd", _b_pre_fn, [zf], b_pars, [(d, F32)] * 7, tb_vjp)
    y_b, b_hist = _rwkv_fwd(rr, ww, kk2, vv, av, bv)
    post_toks = [o_dn, (z_zg, d, 0), (z_zg, d, 1), (z_zg, d, 2), y_b, rr, kk2, vv, gate]
    (merged,) = _tok_fwd("post_fwd", _post_fn, post_toks, post_pars, [(d, BF16)], tb_vjp)
    h2 = _mm("out_proj", merged, full['w_out'], add=h1, tb=tb_mm, tn=d)
    h3 = _ffn_fwd("ffn2_fwd", h2, full['ffn2_norm'], full['ffn2_w_gu'], full['ffn2_w_down'], tb_mm)

    target = jnp.pad(loss_target[0], ((CHUNK, 0), (0, 0)))
    dh3, g_final, loss_part = _loss(h3, target, full['final_norm'].reshape(1, d), tb_vjp)

    def ffn_backward(tag, h, dout, key_norm, key_gu, key_down):
        dh, dgain, xn, act, dgate, dup, dhalf = _ffn_bwd(tag + "_bwd", h, full[key_norm], dout, full[key_gu],
                                                         full[key_down], tb_mm)
        ff = full[key_down].shape[0]
        d_gu = jnp.concatenate([_mm_tn(tag + "_dw_gate", xn, dgate, tm=d, tn=_col_tile(ff, 1408), tk=tb_mm),
                                _mm_tn(tag + "_dw_up", xn, dup, tm=d, tn=_col_tile(ff, 1408), tk=tb_mm)], axis=1)
        d_down = _mm_tn(tag + "_dw_down", act, dhalf, tm=_col_tile(ff, 1408), tn=d, tk=tb_mm)
        return dh, dgain, d_gu, d_down

    dh2, g_ffn2_norm, g_ffn2_gu, g_ffn2_down = ffn_backward("ffn2", h2, dh3, 'ffn2_norm', 'ffn2_w_gu', 'ffn2_w_down')
    dh2_bf = dh2.astype(BF16)
    g_w_out = _mm_tn("dw_out", merged, dh2_bf, tm=d, tn=d, tk=tb_mm)
    dmerged = _mm("d_merged", dh2_bf, full['w_out'], trans_b=True, tb=tb_mm, tn=d)
    post_grads = _tok_bwd("post_bwd", _post_fn, post_toks, post_pars, [[dmerged]], list(range(9)), tb_vjp)
    do_dn, daz, dga, dgb, dy_b, dr1, dk1, dv1, dgate = post_grads[:9]
    g_out_norm, g_r_k, g_ln_g, g_ln_b = post_grads[9:]
    dr2, dw2, dk2, dv2, da2, db2 = _rwkv_bwd(rr, ww, kk2, vv, av, bv, b_hist, dy_b)
    b_grads = _tok_bwd("b_pre_bwd", _b_pre_fn, [zf], b_pars,
                       [[dr1, dr2], [dw2], [dk1, dk2], [dv1, dv2], [da2], [db2], [dgate]], [0], tb_vjp)
    dzf = b_grads[0]
    g_w0, g_w_up, g_a0, g_a_up, g_g_up, g_k_k, g_k_a = b_grads[1:]
    dz_b, g_mu = _shift_bwd(z_b, mu, dzf)
    dq, dk, dv, dbg = _dn_bwd(qkv, bg, dn_hist, do_dn)
    dqkv = jnp.concatenate([dq, dk, dv], axis=1)
    dz_qkv, g_conv = _a_pre_bwd(z_qkv, full['a_conv_w'], dqkv)
    dz_bg, g_log_rate, g_dt_bias = _tok_bwd("bg_bwd", bg_fn, [z_bg], [log_rate, dt_bias], [[dbg]], [0], tb_mm)
    dz_zg = jnp.concatenate([daz, dga, dgb], axis=1)

    du = None
    g_w_in_parts = []
    for tag, dz, wpiece in (("qkv", dz_qkv, w_qkv), ("zg", dz_zg, w_zg), ("b", dz_b, w_b), ("bg", dz_bg, w_bg)):
        dz_bf = dz.astype(BF16)
        du = _mm("du_" + tag, dz_bf, wpiece, trans_b=True, add=du, tb=tb_mm, tn=d)
        g_w_in_parts.append(_mm_tn("dw_in_" + tag, u, dz_bf, tm=d, tn=_col_tile(dz.shape[1], 1536), tk=tb_mm))
    dh1, g_mix_norm = _tok_bwd("mix_norm_bwd", _norm_res_fn, [h1], [full['mix_norm']], [[du], [dh2]], [0], tb_vjp)
    dh0, g_ffn1_norm, g_ffn1_gu, g_ffn1_down = ffn_backward("ffn1", h0, dh1, 'ffn1_norm', 'ffn1_w_gu', 'ffn1_w_down')

    gp_qkv, gp_zg, gp_b, gp_bg = g_w_in_parts
    g_w_in = jnp.concatenate([gp_qkv, gp_zg[:, :d], gp_bg[:, :2 * heads_a], gp_b[:, :n_b], gp_zg[:, d:]], axis=1)

    big_grads = {'ffn1_w_gu': g_ffn1_gu, 'ffn1_w_down': g_ffn1_down, 'w_in': g_w_in, 'w_out': g_w_out,
                 'ffn2_w_gu': g_ffn2_gu, 'ffn2_w_down': g_ffn2_down}
    slabs = [(_cols_to_slabs(big_grads[k]) if k in col_sharded else big_grads[k].reshape(N_DEV, -1, d)) for k in big]
    received = dict(zip(big, _exchange("exchange_grads", slabs)))

    small_full = {
        'meta_tokens': dh0[PAD:CHUNK], 'ffn1_norm': g_ffn1_norm, 'mix_norm': g_mix_norm, 'a_conv_w': g_conv,
        'a_log_rate': g_log_rate[:, heads_a:2 * heads_a], 'a_dt_bias': g_dt_bias[:, heads_a:2 * heads_a],
        'a_out_norm': g_out_norm, 'b_shift_mu': g_mu[:, :n_b], 'b_w0': g_w0, 'b_w_up': g_w_up[:LORA_W],
        'b_a0': g_a0, 'b_a_up': g_a_up[LORA_W:], 'b_g_up': g_g_up[:LORA_G], 'b_k_k': g_k_k, 'b_k_a': g_k_a,
        'b_r_k': g_r_k, 'b_ln_gain': g_ln_g, 'b_ln_bias': g_ln_b, 'ffn2_norm': g_ffn2_norm, 'final_norm': g_final,
    }
    small_names = list(small_full)
    packed = _pack([small_full[k] for k in small_names] + [loss_part[:, :1]])
    (all_parts,) = _all_gather("gather_small_grads", [packed])
    summed = _sum_slabs("sum_small_grads", all_parts, packed.shape[0])
    pieces = _unpack(summed, [small_full[k].shape for k in small_names] + [(1, 1)])
    small_grad = dict(zip(small_names, pieces[:-1]))
    loss = pieces[-1].reshape(())

    grads, deltas, new_m, new_v = {}, {}, {}, {}
    for k in big:
        shard = wts[k][0]
        rb = _tb(shard.shape[0], 128) if shard.shape[0] % 16 == 0 else shard.shape[0]
        g, dl, m2, v2 = _adamw("adamw_" + k, received[k], shard, mom_m[k][0], mom_v[k][0], rb)
        grads[k], deltas[k], new_m[k], new_v[k] = g[None], dl[None], m2[None], v2[None]

    local_small = {}
    for k in small_names:
        g = small_grad[k]
        if k in small_sharded:
            width = wts[k].shape[-1]
            g = lax.dynamic_slice_in_dim(g, me * width, width, axis=1)
        local_small[k] = g.reshape(wts[k].shape)
    pk = lambda tree: _pack([tree[k] for k in small_names])
    dl_s, m_s, v_s = _adamw_small(pk(wts), pk(local_small), pk(mom_m), pk(mom_v))
    shapes = [wts[k].shape for k in small_names]
    for k, dl, m2, v2 in zip(small_names, _unpack(dl_s, shapes), _unpack(m_s, shapes), _unpack(v_s, shapes)):
        grads[k], deltas[k], new_m[k], new_v[k] = local_small[k], dl, m2, v2

    grad_x = dh0[CHUNK:][None]
    return (loss, grad_x, *[grads[k] for k in names], *[deltas[k] for k in names],
            *[new_m[k] for k in names], *[new_v[k] for k in names])
```

```python
import functools

import jax
import jax.numpy as jnp
from jax import lax
from jax.experimental import pallas as pl
from jax.experimental.pallas import tpu as pltpu

F32 = jnp.float32
BF16 = jnp.bfloat16
N_DEV = 8
N_META = 16
CHUNK = 64
PAD = CHUNK - N_META
HEAD_A = 128
HEAD_B = 64
LORA_W, LORA_A, LORA_G = 64, 64, 160
LORA_PAD = 384
EPS = 1e-6
GN_EPS = HEAD_B * 1e-5
ADAM_LR, ADAM_B1, ADAM_B2, ADAM_EPS, ADAM_WD, ADAM_STEP = 0.001, 0.9, 0.999, 1e-08, 0.01, 10
SCAN_STEPS = 16
DN_HEADS_PER_STEP = 8
MXU_WIDTH = 256
VMEM_LIMIT = 56 * 1024 * 1024
HIGHEST = lax.Precision.HIGHEST
MESH_ID = pl.DeviceIdType.MESH
ANY = pl.BlockSpec(memory_space=pl.ANY)


def _cp(*sem):
    return pltpu.CompilerParams(dimension_semantics=sem, vmem_limit_bytes=VMEM_LIMIT)


def _tb(t, target):
    best = 16
    for d in range(16, target + 1, 16):
        if t % d == 0:
            best = d
    return best


def _sigmoid(x):
    return 1.0 / (1.0 + jnp.exp(-x))


def _silu(x):
    return x * _sigmoid(x)


def _softplus(x):
    return jnp.maximum(x, 0.0) + jnp.log(1.0 + jnp.exp(-jnp.abs(x)))


def _dot_nt(a, b, precision=None):
    return lax.dot_general(a, b, (((1,), (1,)), ((), ())), preferred_element_type=F32, precision=precision)


def _dot_tn(a, b, precision=None):
    return lax.dot_general(a, b, (((0,), (0,)), ((), ())), preferred_element_type=F32, precision=precision)


def _dot(a, b, precision=None):
    return jnp.dot(a, b, preferred_element_type=F32, precision=precision)


def _block_diag_ones():
    i = lax.broadcasted_iota(jnp.int32, (MXU_WIDTH, MXU_WIDTH), 0) // HEAD_B
    j = lax.broadcasted_iota(jnp.int32, (MXU_WIDTH, MXU_WIDTH), 1) // HEAD_B
    return (i == j).astype(BF16)


def _hi_lo(x):
    hi = x.astype(BF16)
    return hi, (x - hi.astype(F32)).astype(BF16)


def _segsum_many(xs, bd):
    groups = [x if isinstance(x, tuple) else _hi_lo(x) for x in xs]
    rows = groups[0][0].shape[0]
    stacked = jnp.concatenate([p for grp in groups for p in grp], axis=0)
    out = jnp.concatenate([_dot(stacked[:, s:s + MXU_WIDTH], bd) for s in range(0, stacked.shape[1], MXU_WIDTH)], axis=1)
    res, pos = [], 0
    for grp in groups:
        acc = out[pos:pos + rows]
        for j in range(1, len(grp)):
            acc = acc + out[pos + j * rows:pos + (j + 1) * rows]
        res.append(acc)
        pos += len(grp) * rows
    return res


def _segsum_impl(x):
    return _segsum_many([x], _block_diag_ones())[0]


@jax.custom_vjp
def _segsum64(x):
    return _segsum_impl(x)


_segsum64.defvjp(lambda x: (_segsum_impl(x), None), lambda _, ct: (_segsum_impl(ct),))


def _tok(t):
    return t if isinstance(t, tuple) else (t, t.shape[1], 0)


def _tok_spec(tb, width, colblk):
    return pl.BlockSpec((tb, width), lambda i: (i, colblk))


def _par_spec(p):
    return pl.BlockSpec(p.shape, lambda i: (0, 0))


def _tok_fwd(name, fn, toks, pars, outs, tb):
    toks = [_tok(t) for t in toks]
    rows = toks[0][0].shape[0]
    n_in = len(toks) + len(pars)

    def body(*refs):
        row0 = pl.program_id(0) * tb
        res = fn(row0, *[r[...] for r in refs[:n_in]])
        for r, o in zip(refs[n_in:], res):
            r[...] = o.astype(r.dtype)

    return pl.pallas_call(
        body, name=name, grid=(rows // tb,),
        in_specs=[_tok_spec(tb, w, c) for _, w, c in toks] + [_par_spec(p) for p in pars],
        out_specs=[_tok_spec(tb, w, 0) for w, _ in outs],
        out_shape=[jax.ShapeDtypeStruct((rows, w), dt) for w, dt in outs],
        compiler_params=_cp("parallel"),
    )(*[a for a, _, _ in toks], *pars)


def _tok_bwd(name, fn, toks, pars, cts, want, tb):
    toks = [_tok(t) for t in toks]
    cts = [[_tok(c) for c in group] for group in cts]
    flat_cts = [c for group in cts for c in group]
    rows = toks[0][0].shape[0]
    n_tok, n_par, n_ct = len(toks), len(pars), len(flat_cts)

    def body(*refs):
        i = pl.program_id(0)
        row0 = i * tb
        prim = [r[...].astype(F32) for r in refs[:n_tok + n_par]]
        ct_refs = list(refs[n_tok + n_par:n_tok + n_par + n_ct])
        out_refs = refs[n_tok + n_par + n_ct:]
        res, vjp = jax.vjp(lambda *a: fn(row0, *a), *prim)
        ct = []
        for group, o in zip(cts, res):
            acc = None
            for _ in group:
                v = ct_refs.pop(0)[...].astype(F32)
                acc = v if acc is None else acc + v
            ct.append(acc.astype(o.dtype))
        grads = vjp(tuple(ct))
        for r, k in zip(out_refs[:len(want)], want):
            r[...] = grads[k]

        @pl.when(i == 0)
        def _():
            for r in out_refs[len(want):]:
                r[...] = jnp.zeros_like(r)

        for r, g in zip(out_refs[len(want):], grads[n_tok:]):
            r[...] += g

    return pl.pallas_call(
        body, name=name, grid=(rows // tb,),
        in_specs=[_tok_spec(tb, w, c) for _, w, c in toks] + [_par_spec(p) for p in pars]
        + [_tok_spec(tb, w, c) for _, w, c in flat_cts],
        out_specs=[_tok_spec(tb, toks[k][1], 0) for k in want] + [_par_spec(p) for p in pars],
        out_shape=[jax.ShapeDtypeStruct((rows, toks[k][1]), F32) for k in want]
        + [jax.ShapeDtypeStruct(p.shape, F32) for p in pars],
        compiler_params=_cp("arbitrary"),
    )(*[a for a, _, _ in toks], *pars, *[a for a, _, _ in flat_cts])


def _mm(name, a, b, *, trans_b=False, add=None, tb, tn):
    rows, k = a.shape
    n = b.shape[0] if trans_b else b.shape[1]

    def body(*refs):
        a_ref, b_ref = refs[:2]
        o_ref = refs[-1]
        acc = _dot_nt(a_ref[...], b_ref[...]) if trans_b else _dot(a_ref[...], b_ref[...])
        if add is not None:
            acc = acc + refs[2][...]
        o_ref[...] = acc

    in_specs = [pl.BlockSpec((tb, k), lambda i, j: (i, 0)),
                pl.BlockSpec((tn, k), lambda i, j: (j, 0)) if trans_b else pl.BlockSpec((k, tn), lambda i, j: (0, j))]
    args = [a, b]
    if add is not None:
        in_specs.append(pl.BlockSpec((tb, tn), lambda i, j: (i, j)))
        args.append(add)
    return pl.pallas_call(
        body, name=name, grid=(rows // tb, n // tn), in_specs=in_specs,
        out_specs=pl.BlockSpec((tb, tn), lambda i, j: (i, j)),
        out_shape=jax.ShapeDtypeStruct((rows, n), F32),
        compiler_params=_cp("parallel", "parallel"),
    )(*args)


def _mm_tn(name, a, b, *, tm, tn, tk):
    rows, m = a.shape
    n = b.shape[1]

    def body(a_ref, b_ref, o_ref):
        @pl.when(pl.program_id(2) == 0)
        def _():
            o_ref[...] = jnp.zeros_like(o_ref)

        o_ref[...] += _dot_tn(a_ref[...], b_ref[...])

    return pl.pallas_call(
        body, name=name, grid=(m // tm, n // tn, rows // tk),
        in_specs=[pl.BlockSpec((tk, tm), lambda i, j, k: (k, i)), pl.BlockSpec((tk, tn), lambda i, j, k: (k, j))],
        out_specs=pl.BlockSpec((tm, tn), lambda i, j, k: (i, j)),
        out_shape=jax.ShapeDtypeStruct((m, n), F32),
        compiler_params=_cp("parallel", "parallel", "arbitrary"),
    )(a, b)


def _col_tile(n, target):
    if n <= target:
        return n
    best = 128
    for d in range(128, target + 1, 128):
        if n % d == 0:
            best = d
    return best


def _rms(x, gain):
    return x * lax.rsqrt(jnp.mean(x * x, axis=-1, keepdims=True) + EPS) * gain


def _ffn_fwd(name, h, gain, wgu, wd, tb):
    rows, d = h.shape
    ff = wd.shape[0]
    nj = 2
    fc = ff // nj

    def body(h_ref, g_ref, wg_ref, wu_ref, wd_ref, o_ref, xn_s, acc_s):
        j = pl.program_id(1)

        @pl.when(j == 0)
        def _():
            xn_s[...] = _rms(h_ref[...], g_ref[...]).astype(BF16)
            acc_s[...] = jnp.zeros_like(acc_s)

        xn = xn_s[...]
        gate = _dot(xn, wg_ref[...])
        up = _dot(xn, wu_ref[...])
        acc_s[...] += _dot((_silu(gate) * up).astype(BF16), wd_ref[...])

        @pl.when(j == nj - 1)
        def _():
            o_ref[...] = h_ref[...] + 0.5 * acc_s[...]

    return pl.pallas_call(
        body, name=name, grid=(rows // tb, nj),
        in_specs=[pl.BlockSpec((tb, d), lambda i, j: (i, 0)), pl.BlockSpec((1, d), lambda i, j: (0, 0)),
                  pl.BlockSpec((d, fc), lambda i, j: (0, j)), pl.BlockSpec((d, fc), lambda i, j: (0, nj + j)),
                  pl.BlockSpec((fc, d), lambda i, j: (j, 0))],
        out_specs=pl.BlockSpec((tb, d), lambda i, j: (i, 0)),
        out_shape=jax.ShapeDtypeStruct((rows, d), F32),
        scratch_shapes=[pltpu.VMEM((tb, d), BF16), pltpu.VMEM((tb, d), F32)],
        compiler_params=_cp("parallel", "arbitrary"),
    )(h, gain, wgu, wgu, wd)


def _ffn_bwd(name, h, gain, dout, wgu, wd, tb):
    rows, d = h.shape
    ff = wd.shape[0]
    nj = 2
    fc = ff // nj

    def body(h_ref, g_ref, do_ref, wg_ref, wu_ref, wd_ref,
             dh_ref, dg_ref, xn_ref, act_ref, dgate_ref, dup_ref, dhalf_ref, dxn_s):
        i, j = pl.program_id(0), pl.program_id(1)

        @pl.when(j == 0)
        def _():
            xn_ref[...] = _rms(h_ref[...], g_ref[...]).astype(BF16)
            dhalf_ref[...] = (0.5 * do_ref[...]).astype(BF16)
            dxn_s[...] = jnp.zeros_like(dxn_s)

        xn = xn_ref[...]
        gate = _dot(xn, wg_ref[...])
        up = _dot(xn, wu_ref[...])
        sg = _sigmoid(gate)
        dact = _dot_nt(dhalf_ref[...], wd_ref[...])
        act_ref[...] = (gate * sg * up).astype(BF16)
        dgate = (dact * up * (sg * (1.0 + gate * (1.0 - sg)))).astype(BF16)
        dup = (dact * gate * sg).astype(BF16)
        dgate_ref[...] = dgate
        dup_ref[...] = dup
        dxn_s[...] += _dot_nt(dgate, wg_ref[...]) + _dot_nt(dup, wu_ref[...])

        @pl.when((i == 0) & (j == 0))
        def _():
            dg_ref[...] = jnp.zeros_like(dg_ref)

        @pl.when(j == nj - 1)
        def _():
            x = h_ref[...]
            r = lax.rsqrt(jnp.mean(x * x, axis=-1, keepdims=True) + EPS)
            dxn = dxn_s[...]
            dyg = dxn * g_ref[...]
            dh_ref[...] = do_ref[...] + r * dyg - x * (r * r * r) * jnp.mean(dyg * x, axis=-1, keepdims=True)
            dg_ref[...] += jnp.sum(dxn * x * r, axis=0, keepdims=True)

    row_d = pl.BlockSpec((tb, d), lambda i, j: (i, 0))
    return pl.pallas_call(
        body, name=name, grid=(rows // tb, nj),
        in_specs=[row_d, pl.BlockSpec((1, d), lambda i, j: (0, 0)), row_d,
                  pl.BlockSpec((d, fc), lambda i, j: (0, j)), pl.BlockSpec((d, fc), lambda i, j: (0, nj + j)),
                  pl.BlockSpec((fc, d), lambda i, j: (j, 0))],
        out_specs=[row_d, pl.BlockSpec((1, d), lambda i, j: (0, 0)), row_d,
                   pl.BlockSpec((tb, fc), lambda i, j: (i, j)),
                   pl.BlockSpec((tb, fc), lambda i, j: (i, j)), pl.BlockSpec((tb, fc), lambda i, j: (i, j)),
                   row_d],
        out_shape=[jax.ShapeDtypeStruct((rows, d), F32), jax.ShapeDtypeStruct((1, d), F32),
                   jax.ShapeDtypeStruct((rows, d), BF16), jax.ShapeDtypeStruct((rows, ff), BF16),
                   jax.ShapeDtypeStruct((rows, ff), BF16), jax.ShapeDtypeStruct((rows, ff), BF16),
                   jax.ShapeDtypeStruct((rows, d), BF16)],
        scratch_shapes=[pltpu.VMEM((tb, d), F32)],
        compiler_params=_cp("arbitrary", "arbitrary"),
    )(h, gain, dout, wgu, wgu, wd)


def _shift_rows(x, s):
    return pltpu.roll(x, s % x.shape[0], 0)


def _a_post(c, which):
    s = _silu(c)
    n = s * lax.rsqrt(jnp.sum(s * s, axis=-1, keepdims=True) + 1e-6)
    scale = jnp.where(which == 0, HEAD_A ** -0.5, 1.0)
    return jnp.where(which == 2, s, n * scale)


def _conv(x, w):
    return x * w[3:4] + _shift_rows(x, 1) * w[2:3] + _shift_rows(x, 2) * w[1:2] + _shift_rows(x, 3) * w[0:1]


def _a_pre_fwd(zqkv, conv_w):
    rows, width = zqkv.shape
    heads = width // (3 * HEAD_A)

    def body(x_ref, w_ref, o_ref):
        which = pl.program_id(0) // heads
        live = lax.broadcasted_iota(jnp.int32, (rows, HEAD_A), 0) >= PAD
        o_ref[...] = jnp.where(live, _a_post(_conv(x_ref[...], w_ref[...]), which), 0.0)

    return pl.pallas_call(
        body, name="a_pre_fwd", grid=(width // HEAD_A,),
        in_specs=[pl.BlockSpec((rows, HEAD_A), lambda c: (0, c)), pl.BlockSpec((4, HEAD_A), lambda c: (0, c))],
        out_specs=pl.BlockSpec((rows, HEAD_A), lambda c: (0, c)),
        out_shape=jax.ShapeDtypeStruct((rows, width), F32),
        compiler_params=_cp("parallel"),
    )(zqkv, conv_w)


def _a_pre_bwd(zqkv, conv_w, dqkv):
    rows, width = zqkv.shape
    heads = width // (3 * HEAD_A)

    def body(x_ref, w_ref, ct_ref, dx_ref, dw_ref):
        which = pl.program_id(0) // heads
        live = lax.broadcasted_iota(jnp.int32, (rows, HEAD_A), 0) >= PAD
        x, w = x_ref[...], w_ref[...]
        _, vjp = jax.vjp(lambda c: _a_post(c, which), _conv(x, w))
        (dc,) = vjp(jnp.where(live, ct_ref[...], 0.0))
        dc = jnp.where(live, dc, 0.0)
        dx_ref[...] = (dc * w[3:4] + _shift_rows(dc, -1) * w[2:3] + _shift_rows(dc, -2) * w[1:2]
                       + _shift_rows(dc, -3) * w[0:1])
        dw_ref[...] = jnp.concatenate(
            [jnp.sum(dc * (_shift_rows(x, 3 - j) if j < 3 else x), axis=0, keepdims=True) for j in range(4)], axis=0)

    col = pl.BlockSpec((rows, HEAD_A), lambda c: (0, c))
    wsp = pl.BlockSpec((4, HEAD_A), lambda c: (0, c))
    return pl.pallas_call(
        body, name="a_pre_bwd", grid=(width // HEAD_A,),
        in_specs=[col, wsp, col], out_specs=[col, wsp],
        out_shape=[jax.ShapeDtypeStruct((rows, width), F32), jax.ShapeDtypeStruct((4, width), F32)],
        compiler_params=_cp("parallel"),
    )(zqkv, conv_w, dqkv)


SHIFT_TILE = 384


def _shift_fwd(zb, mu):
    rows, width = zb.shape

    def body(x_ref, mu_ref, o_ref):
        x = x_ref[...]
        first = lax.broadcasted_iota(jnp.int32, x.shape, 0) == 0
        prev = jnp.where(first, 0.0, _shift_rows(x, 1))
        o_ref[...] = x + (prev - x) * mu_ref[...]

    col = pl.BlockSpec((rows, SHIFT_TILE), lambda c: (0, c))
    return pl.pallas_call(
        body, name="shift_fwd", grid=(width // SHIFT_TILE,),
        in_specs=[col, pl.BlockSpec((1, SHIFT_TILE), lambda c: (0, c))], out_specs=col,
        out_shape=jax.ShapeDtypeStruct((rows, width), F32), compiler_params=_cp("parallel"),
    )(zb, mu)


def _shift_bwd(zb, mu, dzf):
    rows, width = zb.shape

    def body(x_ref, mu_ref, ct_ref, dx_ref, dmu_ref):
        x, ct, mu_v = x_ref[...], ct_ref[...], mu_ref[...]
        row = lax.broadcasted_iota(jnp.int32, x.shape, 0)
        prev = jnp.where(row == 0, 0.0, _shift_rows(x, 1))
        nxt = jnp.where(row == rows - 1, 0.0, _shift_rows(ct, -1))
        dx_ref[...] = ct * (1.0 - mu_v) + nxt * mu_v
        dmu_ref[...] = jnp.sum(ct * (prev - x), axis=0, keepdims=True)

    col = pl.BlockSpec((rows, SHIFT_TILE), lambda c: (0, c))
    msp = pl.BlockSpec((1, SHIFT_TILE), lambda c: (0, c))
    return pl.pallas_call(
        body, name="shift_bwd", grid=(width // SHIFT_TILE,),
        in_specs=[col, msp, col], out_specs=[col, msp],
        out_shape=[jax.ShapeDtypeStruct((rows, width), F32), jax.ShapeDtypeStruct((1, width), F32)],
        compiler_params=_cp("parallel"),
    )(zb, mu, dzf)


def _dn_chunk(q, k, v, beta, g, state):
    heads = range(len(q))
    ri = lax.broadcasted_iota(jnp.int32, (CHUNK, CHUNK), 0)
    ci = lax.broadcasted_iota(jnp.int32, (CHUNK, CHUNK), 1)
    eye = (ri == ci).astype(F32)
    incl = ri >= ci
    last = lax.broadcasted_iota(jnp.int32, (CHUNK, 1), 0) == CHUNK - 1
    g_row = [jnp.sum(g[h] * eye, axis=0, keepdims=True) for h in heads]
    gc = [jnp.sum(jnp.where(incl, g_row[h], 0.0), axis=1, keepdims=True) for h in heads]
    gc_row = [jnp.sum(gc[h] * eye, axis=0, keepdims=True) for h in heads]
    decay = [jnp.where(incl, jnp.exp(jnp.where(incl, gc[h] - gc_row[h], 0.0)), 0.0) for h in heads]
    kb = [k[h] * beta[h] for h in heads]
    vb = [v[h] * beta[h] for h in heads]
    p = [-jnp.where(ri > ci, _dot_nt(kb[h], k[h], HIGHEST) * decay[h], 0.0) for h in heads]
    tinv = [eye + p[h] for h in heads]
    for _ in range(5):
        p = [_dot(p[h], p[h], HIGHEST) for h in heads]
        tinv = [tinv[h] + _dot(tinv[h], p[h], HIGHEST) for h in heads]
    eg = [jnp.exp(gc[h]) for h in heads]
    u = [_dot(tinv[h], vb[h], HIGHEST) for h in heads]
    wk = [_dot(tinv[h], kb[h] * eg[h], HIGHEST) for h in heads]
    attn = [_dot_nt(q[h], k[h], HIGHEST) * decay[h] for h in heads]
    g_last = [jnp.sum(jnp.where(last, gc[h], 0.0), axis=0, keepdims=True) for h in heads]
    k_tail = [k[h] * jnp.exp(g_last[h] - gc[h]) for h in heads]
    v_new = [u[h] - _dot(wk[h], state[h], HIGHEST) for h in heads]
    o = [_dot(q[h] * eg[h], state[h], HIGHEST) + _dot(attn[h], v_new[h], HIGHEST) for h in heads]
    new = [state[h] * jnp.exp(g_last[h]) + _dot_tn(k_tail[h], v_new[h], HIGHEST) for h in heads]
    return o, new


def _bg_cols(bg, h, heads):
    lane = lax.broadcasted_iota(jnp.int32, bg.shape, 1)
    beta = jnp.sum(jnp.where(lane == h, bg, 0.0), axis=1, keepdims=True)
    g = jnp.sum(jnp.where(lane == heads + h, bg, 0.0), axis=1, keepdims=True)
    return beta, g


def _dn_fwd(qkv, bg):
    rows = qkv.shape[0]
    heads = qkv.shape[1] // (3 * HEAD_A)
    n = rows // CHUNK
    hp = min(heads, DN_HEADS_PER_STEP)
    groups = heads // hp

    def body(q_ref, k_ref, v_ref, bg_ref, o_ref, hist_ref, s_ref):
        c, grp = pl.program_id(0), pl.program_id(1)

        @pl.when(c == 0)
        def _():
            for i in range(hp):
                s_ref[grp * hp + i] = jnp.zeros((HEAD_A, HEAD_A), F32)

        bg_v = bg_ref[...]
        cols = [slice(i * HEAD_A, (i + 1) * HEAD_A) for i in range(hp)]
        state = [s_ref[grp * hp + i] for i in range(hp)]
        beta_g = [_bg_cols(bg_v, grp * hp + i, heads) for i in range(hp)]
        o, new = _dn_chunk([q_ref[:, c_] for c_ in cols], [k_ref[:, c_] for c_ in cols], [v_ref[:, c_] for c_ in cols],
                           [b for b, _ in beta_g], [g for _, g in beta_g], state)
        for i in range(hp):
            hist_ref[0, i] = state[i]
            o_ref[:, cols[i]] = o[i]
            s_ref[grp * hp + i] = new[i]

    def part(p):
        return pl.BlockSpec((CHUNK, hp * HEAD_A), lambda c, grp: (c, p * groups + grp))

    return pl.pallas_call(
        body, name="deltanet_fwd", grid=(n, groups),
        in_specs=[part(0), part(1), part(2), pl.BlockSpec((CHUNK, 128), lambda c, grp: (c, 0))],
        out_specs=[part(0), pl.BlockSpec((1, hp, HEAD_A, HEAD_A), lambda c, grp: (c, grp, 0, 0))],
        out_shape=[jax.ShapeDtypeStruct((rows, heads * HEAD_A), F32),
                   jax.ShapeDtypeStruct((n, heads, HEAD_A, HEAD_A), F32)],
        scratch_shapes=[pltpu.VMEM((heads, HEAD_A, HEAD_A), F32)],
        compiler_params=_cp("arbitrary", "arbitrary"),
    )(qkv, qkv, qkv, bg)


def _dn_bwd(qkv, bg, hist, do):
    rows = qkv.shape[0]
    heads = qkv.shape[1] // (3 * HEAD_A)
    n = rows // CHUNK
    hp = min(heads, DN_HEADS_PER_STEP)
    groups = heads // hp

    def body(q_ref, k_ref, v_ref, bg_ref, hist_ref, do_ref, dq_ref, dk_ref, dv_ref, dbg_ref, ds_ref):
        c, grp = pl.program_id(0), pl.program_id(1)

        @pl.when(c == 0)
        def _():
            for i in range(hp):
                ds_ref[grp * hp + i] = jnp.zeros((HEAD_A, HEAD_A), F32)

        bg_v = bg_ref[...]
        lane = lax.broadcasted_iota(jnp.int32, (CHUNK, 128), 1)
        cols = [slice(i * HEAD_A, (i + 1) * HEAD_A) for i in range(hp)]
        beta_g = [_bg_cols(bg_v, grp * hp + i, heads) for i in range(hp)]
        _, vjp = jax.vjp(_dn_chunk, [q_ref[:, c_] for c_ in cols], [k_ref[:, c_] for c_ in cols],
                         [v_ref[:, c_] for c_ in cols], [b for b, _ in beta_g], [g for _, g in beta_g],
                         [hist_ref[0, i] for i in range(hp)])
        dq, dk, dv, dbeta, dg, ds = vjp(([do_ref[:, c_] for c_ in cols], [ds_ref[grp * hp + i] for i in range(hp)]))
        dbg = jnp.zeros((CHUNK, 128), F32)
        for i in range(hp):
            h = grp * hp + i
            dq_ref[:, cols[i]] = dq[i]
            dk_ref[:, cols[i]] = dk[i]
            dv_ref[:, cols[i]] = dv[i]
            ds_ref[h] = ds[i]
            dbg = dbg + jnp.where(lane == h, dbeta[i], 0.0) + jnp.where(lane == heads + h, dg[i], 0.0)

        @pl.when(grp == 0)
        def _():
            dbg_ref[...] = jnp.zeros_like(dbg_ref)

        dbg_ref[...] += dbg

    def part(p):
        return pl.BlockSpec((CHUNK, hp * HEAD_A), lambda c, grp: (n - 1 - c, p * groups + grp))

    return pl.pallas_call(
        body, name="deltanet_bwd", grid=(n, groups),
        in_specs=[part(0), part(1), part(2), pl.BlockSpec((CHUNK, 128), lambda c, grp: (n - 1 - c, 0)),
                  pl.BlockSpec((1, hp, HEAD_A, HEAD_A), lambda c, grp: (n - 1 - c, grp, 0, 0)), part(0)],
        out_specs=[part(0)] * 3 + [pl.BlockSpec((CHUNK, 128), lambda c, grp: (n - 1 - c, 0))],
        out_shape=[jax.ShapeDtypeStruct((rows, heads * HEAD_A), F32)] * 3 + [jax.ShapeDtypeStruct((rows, 128), F32)],
        scratch_shapes=[pltpu.VMEM((heads, HEAD_A, HEAD_A), F32)],
        compiler_params=_cp("arbitrary", "arbitrary"),
    )(qkv, qkv, qkv, bg, hist, do)


def _diag_mask(width):
    return (lax.broadcasted_iota(jnp.int32, (HEAD_B, width), 0)
            == lax.broadcasted_iota(jnp.int32, (HEAD_B, width), 1) % HEAD_B).astype(F32)


def _rwkv_fwd(r, w, k, v, a, b):
    rows, width = r.shape
    ts = SCAN_STEPS

    def body(r_ref, w_ref, k_ref, v_ref, a_ref, b_ref, y_ref, hist_ref, s_ref):
        @pl.when(pl.program_id(0) == 0)
        def _():
            s_ref[...] = jnp.zeros_like(s_ref)

        eye = _diag_mask(width)
        bd = _block_diag_ones()

        for j in range(ts):
            row = pl.ds(j, 1)
            s = s_ref[...]
            hist_ref[j] = s
            v_t = v_ref[row, :]
            v_hi = v_t.astype(BF16).astype(F32)
            spread_v = ((eye * v_hi).astype(BF16), (eye * (v_t - v_hi)).astype(BF16))
            sa, vb = _segsum_many([s * a_ref[row, :], spread_v], bd)
            s = s * w_ref[row, :] + sa * b_ref[row, :] + vb * k_ref[row, :]
            (yb,) = _segsum_many([((s * r_ref[row, :]).astype(BF16),)], bd)
            y_ref[row, :] = jnp.sum(yb * eye, axis=0, keepdims=True)
            s_ref[...] = s

    blk = pl.BlockSpec((ts, width), lambda i: (i, 0))
    return pl.pallas_call(
        body, name="rwkv_fwd", grid=(rows // ts,),
        in_specs=[blk] * 6,
        out_specs=[blk, pl.BlockSpec((ts, HEAD_B, width), lambda i: (i, 0, 0))],
        out_shape=[jax.ShapeDtypeStruct((rows, width), F32), jax.ShapeDtypeStruct((rows, HEAD_B, width), F32)],
        scratch_shapes=[pltpu.VMEM((HEAD_B, width), F32)],
        compiler_params=_cp("arbitrary"),
    )(r, w, k, v, a, b)


def _rwkv_bwd(r, w, k, v, a, b, hist, dy):
    rows, width = r.shape
    ts = SCAN_STEPS
    nb = rows // ts

    def body(r_ref, w_ref, k_ref, v_ref, a_ref, b_ref, hist_ref, dy_ref,
             dr_ref, dw_ref, dk_ref, dv_ref, da_ref, db_ref, g_ref):
        @pl.when(pl.program_id(0) == 0)
        def _():
            g_ref[...] = jnp.zeros_like(g_ref)

        eye = _diag_mask(width)
        bd = _block_diag_ones()

        def colsum(x):
            return jnp.sum(x, axis=0, keepdims=True)

        for j in reversed(range(ts)):
            row = pl.ds(j, 1)
            sp = hist_ref[j]
            a_t, b_t, k_t, w_t, r_t = a_ref[row, :], b_ref[row, :], k_ref[row, :], w_ref[row, :], r_ref[row, :]
            sa, vb, dyb = _segsum_many([((sp * a_t).astype(BF16),), ((eye * v_ref[row, :]).astype(BF16),),
                                        ((eye * dy_ref[row, :]).astype(BF16),)], bd)
            st = sp * w_t + sa * b_t + vb * k_t
            g = g_ref[...] + dyb * r_t
            dr_ref[row, :] = colsum(st * dyb)
            dw_ref[row, :] = colsum(g * sp)
            dk_ref[row, :] = colsum(g * vb)
            db_ref[row, :] = colsum(g * sa)
            dsa, dvb = _segsum_many([g * b_t, ((g * k_t).astype(BF16),)], bd)
            dv_ref[row, :] = colsum(dvb * eye)
            da_ref[row, :] = colsum(sp * dsa)
            g_ref[...] = g * w_t + dsa * a_t

    blk = pl.BlockSpec((ts, width), lambda i: (nb - 1 - i, 0))
    return pl.pallas_call(
        body, name="rwkv_bwd", grid=(nb,),
        in_specs=[blk] * 6 + [pl.BlockSpec((ts, HEAD_B, width), lambda i: (nb - 1 - i, 0, 0)), blk],
        out_specs=[blk] * 6,
        out_shape=[jax.ShapeDtypeStruct((rows, width), F32)] * 6,
        scratch_shapes=[pltpu.VMEM((HEAD_B, width), F32)],
        compiler_params=_cp("arbitrary"),
    )(r, w, k, v, a, b, hist, dy)


def _live(row0, shape):
    return (row0 + lax.broadcasted_iota(jnp.int32, shape, 0)) >= PAD


def _norm_fn(row0, h, gain):
    return (_rms(h, gain),)


def _norm_res_fn(row0, h, gain):
    return _rms(h, gain), h


def _make_bg_fn(heads):
    def fn(row0, x, log_rate, dt_bias):
        lane = lax.broadcasted_iota(jnp.int32, x.shape, 1)
        beta = _sigmoid(x)
        g = -jnp.exp(log_rate) * _softplus(x + dt_bias)
        out = jnp.where(lane < heads, beta, jnp.where(lane < 2 * heads, g, 0.0))
        return (jnp.where(_live(row0, x.shape), out, 0.0),)
    return fn


def _b_pre_fn(row0, zf, w0, w_up, a0, a_up, g_up, k_k, k_a):
    d = w0.shape[1]
    r, k, v = zf[:, :d], zf[:, d:2 * d], zf[:, 2 * d:3 * d]
    lo = zf[:, 3 * d:3 * d + 128]
    lg = zf[:, 3 * d + 128:3 * d + LORA_PAD]
    lane = lax.broadcasted_iota(jnp.int32, lo.shape, 1)
    lw = _dot(jnp.where(lane < LORA_W, jnp.tanh(lo), 0.0), w_up)
    la = _dot(jnp.where(lane >= LORA_W, lo, 0.0), a_up)
    lane_g = lax.broadcasted_iota(jnp.int32, lg.shape, 1)
    gate = _dot(jnp.where(lane_g < LORA_G, _sigmoid(lg), 0.0), g_up)
    decay = jnp.exp(-jnp.exp(-_softplus(-(w0 + lw)) - 0.5))
    a = _sigmoid(a0 + la)
    kx = k * k_k
    kk = kx * lax.rsqrt(_segsum64(kx * kx) + 1e-6)
    k2 = k * (1.0 + (a - 1.0) * k_a)
    return r, decay, k2, v, -kk, kk * a, gate


def _post_fn(row0, o, az, ga, gb, y, r, k2, v, gate, out_gain, r_k, ln_g, ln_b):
    heads = o.shape[1] // HEAD_A
    parts = []
    for h in range(heads):
        oh = o[:, h * HEAD_A:(h + 1) * HEAD_A]
        parts.append(oh * lax.rsqrt(jnp.mean(oh * oh, axis=-1, keepdims=True) + EPS) * out_gain)
    o_a = jnp.concatenate(parts, axis=1) * _silu(az)
    mean = _segsum64(y) * (1.0 / HEAD_B)
    yc = y - mean
    var = _segsum64(yc * yc) * (1.0 / HEAD_B)
    yn = yc * lax.rsqrt(var + GN_EPS) * ln_g + ln_b
    o_b = (yn + _segsum64(r * k2 * r_k) * v) * gate
    return (_sigmoid(ga) * o_a + _sigmoid(gb) * o_b,)


def _loss(h3, target, gain, tb):
    rows, d = h3.shape

    def body(h_ref, t_ref, g_ref, dh_ref, dg_ref, l_ref):
        i = pl.program_id(0)
        live = (i * tb + lax.broadcasted_iota(jnp.int32, (tb, 1), 0)) >= CHUNK
        tgt = t_ref[...]

        def f(h, g):
            err = _rms(h, g) - tgt
            return 0.5 * jnp.sum(jnp.where(live, jnp.mean(err * err, axis=-1, keepdims=True), 0.0))

        val, vjp = jax.vjp(f, h_ref[...], g_ref[...])
        dh, dg = vjp(jnp.ones((), F32))
        dh_ref[...] = dh

        @pl.when(i == 0)
        def _():
            dg_ref[...] = jnp.zeros_like(dg_ref)
            l_ref[...] = jnp.zeros_like(l_ref)

        dg_ref[...] += dg
        l_ref[...] += jnp.full((1, 128), val, F32)

    blk = pl.BlockSpec((tb, d), lambda i: (i, 0))
    return pl.pallas_call(
        body, name="loss", grid=(rows // tb,),
        in_specs=[blk, blk, pl.BlockSpec((1, d), lambda i: (0, 0))],
        out_specs=[blk, pl.BlockSpec((1, d), lambda i: (0, 0)), pl.BlockSpec((1, 128), lambda i: (0, 0))],
        out_shape=[jax.ShapeDtypeStruct((rows, d), F32), jax.ShapeDtypeStruct((1, d), F32),
                   jax.ShapeDtypeStruct((1, 128), F32)],
        compiler_params=_cp("arbitrary"),
    )(h3, target, gain)


def _adamw_math(w, g, m, v):
    m2 = ADAM_B1 * m + (1.0 - ADAM_B1) * g
    v2 = ADAM_B2 * v + (1.0 - ADAM_B2) * (g * g)
    m_hat = m2 / (1.0 - ADAM_B1 ** ADAM_STEP)
    v_hat = v2 / (1.0 - ADAM_B2 ** ADAM_STEP)
    return -ADAM_LR * (m_hat / (jnp.sqrt(v_hat) + ADAM_EPS) + ADAM_WD * w), m2, v2


def _adamw(name, slabs, w, m, v, rb):
    rows, cols = w.shape

    def body(s_ref, w_ref, m_ref, v_ref, g_ref, d_ref, m2_ref, v2_ref):
        g = s_ref[0].astype(F32)
        for dev in range(1, N_DEV):
            g = g + s_ref[dev].astype(F32)
        g_ref[...] = g
        d_ref[...], m2_ref[...], v2_ref[...] = _adamw_math(w_ref[...], g, m_ref[...], v_ref[...])

    blk = pl.BlockSpec((rb, cols), lambda i: (i, 0))
    return pl.pallas_call(
        body, name=name, grid=(rows // rb,),
        in_specs=[pl.BlockSpec((N_DEV, rb, cols), lambda i: (0, i, 0)), blk, blk, blk],
        out_specs=[blk] * 4, out_shape=[jax.ShapeDtypeStruct((rows, cols), F32)] * 4,
        compiler_params=_cp("parallel"),
    )(slabs, w, m, v)


def _sum_slabs(name, slabs, rb):
    _, rows, cols = slabs.shape

    def body(s_ref, o_ref):
        g = s_ref[0]
        for dev in range(1, N_DEV):
            g = g + s_ref[dev]
        o_ref[...] = g

    return pl.pallas_call(
        body, name=name, grid=(rows // rb,),
        in_specs=[pl.BlockSpec((N_DEV, rb, cols), lambda i: (0, i, 0))],
        out_specs=pl.BlockSpec((rb, cols), lambda i: (i, 0)),
        out_shape=jax.ShapeDtypeStruct((rows, cols), F32), compiler_params=_cp("parallel"),
    )(slabs)


def _adamw_small(w, g, m, v):
    def body(w_ref, g_ref, m_ref, v_ref, d_ref, m2_ref, v2_ref):
        d_ref[...], m2_ref[...], v2_ref[...] = _adamw_math(w_ref[...], g_ref[...], m_ref[...], v_ref[...])

    return pl.pallas_call(body, name="adamw_small", out_shape=[jax.ShapeDtypeStruct(w.shape, F32)] * 3)(w, g, m, v)


def _place():
    return lax.axis_index("x"), lax.axis_index("y"), lax.axis_index("c")


def _index(p):
    return 4 * p[0] + 2 * p[1] + p[2]


def _all_gather(name, xs):
    n = len(xs)

    def body(*refs):
        x_refs, o_refs = refs[:n], refs[n:2 * n]
        send_sems, recv_sems, local_sems = refs[2 * n:]
        x, y, c = _place()
        me, sibling = (x, y, c), (x, y, 1 - c)
        chips = [(1 - x, y), (x, 1 - y), (1 - x, 1 - y)]

        def copy(i, k, block, to, src=None):
            dst = o_refs[i].at[_index(block)]
            return pltpu.make_async_remote_copy(src_ref=dst if src is None else src, dst_ref=dst,
                                                send_sem=send_sems.at[i, k], recv_sem=recv_sems.at[i, k],
                                                device_id=to, device_id_type=MESH_ID)

        mine = [pltpu.make_async_copy(x_refs[i], o_refs[i].at[_index(me)], local_sems.at[i]) for i in range(n)]
        for cp in mine:
            cp.start()
        first = []
        for i in range(n):
            first.append(copy(i, 0, me, sibling, src=x_refs[i]))
            first += [copy(i, 1 + j, me, (*chip, c), src=x_refs[i]) for j, chip in enumerate(chips)]
        for cp in first:
            cp.start()
        passed = []
        for j, chip in enumerate(chips):
            for i in range(n):
                copy(i, 1 + j, (*chip, c), me).wait_recv()
                cp = copy(i, 4 + j, (*chip, c), sibling)
                cp.start()
                passed.append(cp)
        for i in range(n):
            copy(i, 0, sibling, me).wait_recv()
            for j, chip in enumerate(chips):
                copy(i, 4 + j, (*chip, 1 - c), me).wait_recv()
        for cp in first + passed:
            cp.wait_send()
        for cp in mine:
            cp.wait()

    return pl.pallas_call(
        body, name=name, in_specs=[ANY] * n, out_specs=[ANY] * n,
        out_shape=[jax.ShapeDtypeStruct((N_DEV,) + x.shape, x.dtype) for x in xs],
        scratch_shapes=[pltpu.SemaphoreType.DMA((n, 7)), pltpu.SemaphoreType.DMA((n, 7)), pltpu.SemaphoreType.DMA((n,))],
    )(*xs)


def _exchange(name, xs):
    n = len(xs)

    def body(*refs):
        x_refs, o_refs = refs[:n], refs[n:2 * n]
        send_sems, recv_sems, local_sems = refs[2 * n:]
        x, y, c = _place()
        me = (x, y, c)
        peers = [((1 - x) if k & 4 else x, (1 - y) if k & 2 else y, (1 - c) if k & 1 else c) for k in range(1, N_DEV)]

        def copy(i, k, peer):
            return pltpu.make_async_remote_copy(src_ref=x_refs[i].at[_index(peer)], dst_ref=o_refs[i].at[_index(me)],
                                                send_sem=send_sems.at[i, k], recv_sem=recv_sems.at[i, k],
                                                device_id=peer, device_id_type=MESH_ID)

        def arrival(i, k, peer):
            return pltpu.make_async_remote_copy(src_ref=x_refs[i].at[_index(peer)], dst_ref=o_refs[i].at[_index(peer)],
                                                send_sem=send_sems.at[i, k], recv_sem=recv_sems.at[i, k],
                                                device_id=peer, device_id_type=MESH_ID)

        mine = [pltpu.make_async_copy(x_refs[i].at[_index(me)], o_refs[i].at[_index(me)], local_sems.at[i])
                for i in range(n)]
        for cp in mine:
            cp.start()
        sent = [copy(i, k, peer) for k, peer in enumerate(peers) for i in range(n)]
        for cp in sent:
            cp.start()
        for k, peer in enumerate(peers):
            for i in range(n):
                arrival(i, k, peer).wait_recv()
        for cp in sent:
            cp.wait_send()
        for cp in mine:
            cp.wait()

    return pl.pallas_call(
        body, name=name, in_specs=[ANY] * n, out_specs=[ANY] * n,
        out_shape=[jax.ShapeDtypeStruct(x.shape, x.dtype) for x in xs],
        scratch_shapes=[pltpu.SemaphoreType.DMA((n, 7)), pltpu.SemaphoreType.DMA((n, 7)), pltpu.SemaphoreType.DMA((n,))],
    )(*xs)


def _pack(arrays):
    flat = jnp.concatenate([a.reshape(-1) for a in arrays])
    pad = (-flat.shape[0]) % 1024
    return jnp.pad(flat, (0, pad)).reshape(-1, 128)


def _unpack(packed, shapes):
    flat = packed.reshape(-1)
    out, pos = [], 0
    for s in shapes:
        size = 1
        for dim in s:
            size *= dim
        out.append(flat[pos:pos + size].reshape(s))
        pos += size
    return out


def _cols_from_slabs(stack):
    return jnp.transpose(stack, (1, 0, 2)).reshape(stack.shape[1], -1)


def _cols_to_slabs(full):
    return jnp.transpose(full.reshape(full.shape[0], N_DEV, -1), (1, 0, 2))


def kernel(x, meta_tokens, ffn1_norm, ffn1_w_gu, ffn1_w_down, mix_norm, w_in, a_conv_w, a_log_rate, a_dt_bias, a_out_norm, b_shift_mu, b_w0, b_w_up, b_a0, b_a_up, b_g_up, b_k_k, b_k_a, b_r_k, b_ln_gain, b_ln_bias, w_out, ffn2_norm, ffn2_w_gu, ffn2_w_down, final_norm, loss_target, m_meta_tokens, m_ffn1_norm, m_ffn1_w_gu, m_ffn1_w_down, m_mix_norm, m_w_in, m_a_conv_w, m_a_log_rate, m_a_dt_bias, m_a_out_norm, m_b_shift_mu, m_b_w0, m_b_w_up, m_b_a0, m_b_a_up, m_b_g_up, m_b_k_k, m_b_k_a, m_b_r_k, m_b_ln_gain, m_b_ln_bias, m_w_out, m_ffn2_norm, m_ffn2_w_gu, m_ffn2_w_down, m_final_norm, v_meta_tokens, v_ffn1_norm, v_ffn1_w_gu, v_ffn1_w_down, v_mix_norm, v_w_in, v_a_conv_w, v_a_log_rate, v_a_dt_bias, v_a_out_norm, v_b_shift_mu, v_b_w0, v_b_w_up, v_b_a0, v_b_a_up, v_b_g_up, v_b_k_k, v_b_k_a, v_b_r_k, v_b_ln_gain, v_b_ln_bias, v_w_out, v_ffn2_norm, v_ffn2_w_gu, v_ffn2_w_down, v_final_norm):
    names = ['meta_tokens', 'ffn1_norm', 'ffn1_w_gu', 'ffn1_w_down', 'mix_norm', 'w_in', 'a_conv_w', 'a_log_rate',
             'a_dt_bias', 'a_out_norm', 'b_shift_mu', 'b_w0', 'b_w_up', 'b_a0', 'b_a_up', 'b_g_up', 'b_k_k', 'b_k_a',
             'b_r_k', 'b_ln_gain', 'b_ln_bias', 'w_out', 'ffn2_norm', 'ffn2_w_gu', 'ffn2_w_down', 'final_norm']
    env = dict(locals())
    wts = {k: env[k] for k in names}
    mom_m = {k: env['m_' + k] for k in names}
    mom_v = {k: env['v_' + k] for k in names}
    big = ['ffn1_w_gu', 'ffn1_w_down', 'w_in', 'w_out', 'ffn2_w_gu', 'ffn2_w_down']
    col_sharded = {'ffn1_w_gu', 'w_in', 'ffn2_w_gu'}
    small_sharded = ['meta_tokens', 'a_conv_w', 'b_w_up', 'b_a_up', 'b_g_up']
    replicated = [k for k in names if k not in big and k not in small_sharded]

    seq, d = x.shape[1], x.shape[2]
    rows = PAD + N_META + seq
    heads_a = d // HEAD_A
    tb_mm = _tb(rows, 416)
    tb_vjp = _tb(rows, 208)
    me = _index(_place())

    big_local = [wts[k][0].astype(BF16) for k in big]
    small_local = [wts['meta_tokens']] + [wts[k][0] for k in small_sharded[1:]]
    gathered = _all_gather("gather_weights", big_local + small_local)
    gw = dict(zip(big + small_sharded, gathered))
    full = {k: (_cols_from_slabs(gw[k]) if k in col_sharded else gw[k].reshape(-1, gw[k].shape[-1])) for k in big}
    for k in small_sharded:
        full[k] = _cols_from_slabs(gw[k])
    for k in replicated:
        full[k] = wts[k].reshape(1, -1)

    win = full['w_in']
    n_b = 3 * d + LORA_W + LORA_A + LORA_G
    off_beta, off_b = 4 * d, 4 * d + 2 * heads_a
    off_ga = off_b + n_b
    b_width = 3 * d + LORA_PAD
    zcols = lambda r, c: jnp.zeros((r, c), BF16)
    w_qkv = win[:, :3 * d]
    w_zg = jnp.concatenate([win[:, 3 * d:4 * d], win[:, off_ga:off_ga + 2 * d]], axis=1)
    w_b = jnp.concatenate([win[:, off_b:off_b + n_b], zcols(d, b_width - n_b)], axis=1)
    w_bg = jnp.concatenate([win[:, off_beta:off_beta + 2 * heads_a], zcols(d, 128 - 2 * heads_a)], axis=1)

    def lanes(vec, start, width):
        return jnp.pad(vec.reshape(1, -1), ((0, 0), (start, width - start - vec.size)))

    log_rate = lanes(wts['a_log_rate'], heads_a, 128)
    dt_bias = lanes(wts['a_dt_bias'], heads_a, 128)
    mu = lanes(wts['b_shift_mu'], 0, b_width)
    w_up = jnp.pad(full['b_w_up'], ((0, 128 - LORA_W), (0, 0)))
    a_up = jnp.pad(full['b_a_up'], ((LORA_W, 0), (0, 0)))
    g_up = jnp.pad(full['b_g_up'], ((0, 256 - LORA_G), (0, 0)))
    b_pars = [full['b_w0'], w_up, full['b_a0'], a_up, g_up, full['b_k_k'], full['b_k_a']]
    post_pars = [full['a_out_norm'], full['b_r_k'], full['b_ln_gain'], full['b_ln_bias']]
    bg_fn = _make_bg_fn(heads_a)

    h0 = jnp.concatenate([jnp.zeros((PAD, d), F32), full['meta_tokens'], x[0]], axis=0)
    h1 = _ffn_fwd("ffn1_fwd", h0, full['ffn1_norm'], full['ffn1_w_gu'], full['ffn1_w_down'], tb_mm)
    (u,) = _tok_fwd("mix_norm_fwd", _norm_fn, [h1], [full['mix_norm']], [(d, BF16)], tb_mm)
    z_qkv = _mm("in_qkv", u, w_qkv, tb=tb_mm, tn=_col_tile(3 * d, 1536))
    z_zg = _mm("in_zg", u, w_zg, tb=tb_mm, tn=_col_tile(3 * d, 1536))
    z_b = _mm("in_b", u, w_b, tb=tb_mm, tn=_col_tile(b_width, 1536))
    z_bg = _mm("in_bg", u, w_bg, tb=tb_mm, tn=128)
    qkv = _a_pre_fwd(z_qkv, full['a_conv_w'])
    (bg,) = _tok_fwd("bg_fwd", bg_fn, [z_bg], [log_rate, dt_bias], [(128, F32)], tb_mm)
    o_dn, dn_hist = _dn_fwd(qkv, bg)
    zf = _shift_fwd(z_b, mu)
    rr, ww, kk2, vv, av, bv, gate = _tok_fwd("b_pre_fwd", _b_pre_fn, [zf], b_pars, [(d, F32)] * 7, tb_vjp)
    y_b, b_hist = _rwkv_fwd(rr, ww, kk2, vv, av, bv)
    post_toks = [o_dn, (z_zg, d, 0), (z_zg, d, 1), (z_zg, d, 2), y_b, rr, kk2, vv, gate]
    (merged,) = _tok_fwd("post_fwd", _post_fn, post_toks, post_pars, [(d, BF16)], tb_vjp)
    h2 = _mm("out_proj", merged, full['w_out'], add=h1, tb=tb_mm, tn=d)
    h3 = _ffn_fwd("ffn2_fwd", h2, full['ffn2_norm'], full['ffn2_w_gu'], full['ffn2_w_down'], tb_mm)

    target = jnp.pad(loss_target[0], ((CHUNK, 0), (0, 0)))
    dh3, g_final, loss_part = _loss(h3, target, full['final_norm'].reshape(1, d), tb_vjp)

    def ffn_backward(tag, h, dout, key_norm, key_gu, key_down):
        dh, dgain, xn, act, dgate, dup, dhalf = _ffn_bwd(tag + "_bwd", h, full[key_norm], dout, full[key_gu],
                                                         full[key_down], tb_mm)
        ff = full[key_down].shape[0]
        d_gu = jnp.concatenate([_mm_tn(tag + "_dw_gate", xn, dgate, tm=d, tn=_col_tile(ff, 1408), tk=tb_mm),
                                _mm_tn(tag + "_dw_up", xn, dup, tm=d, tn=_col_tile(ff, 1408), tk=tb_mm)], axis=1)
        d_down = _mm_tn(tag + "_dw_down", act, dhalf, tm=_col_tile(ff, 1408), tn=d, tk=tb_mm)
        return dh, dgain, d_gu, d_down

    dh2, g_ffn2_norm, g_ffn2_gu, g_ffn2_down = ffn_backward("ffn2", h2, dh3, 'ffn2_norm', 'ffn2_w_gu', 'ffn2_w_down')
    dh2_bf = dh2.astype(BF16)
    g_w_out = _mm_tn("dw_out", merged, dh2_bf, tm=d, tn=d, tk=tb_mm)
    dmerged = _mm("d_merged", dh2_bf, full['w_out'], trans_b=True, tb=tb_mm, tn=d)
    post_grads = _tok_bwd("post_bwd", _post_fn, post_toks, post_pars, [[dmerged]], list(range(9)), tb_vjp)
    do_dn, daz, dga, dgb, dy_b, dr1, dk1, dv1, dgate = post_grads[:9]
    g_out_norm, g_r_k, g_ln_g, g_ln_b = post_grads[9:]
    dr2, dw2, dk2, dv2, da2, db2 = _rwkv_bwd(rr, ww, kk2, vv, av, bv, b_hist, dy_b)
    b_grads = _tok_bwd("b_pre_bwd", _b_pre_fn, [zf], b_pars,
                       [[dr1, dr2], [dw2], [dk1, dk2], [dv1, dv2], [da2], [db2], [dgate]], [0], tb_vjp)
    dzf = b_grads[0]
    g_w0, g_w_up, g_a0, g_a_up, g_g_up, g_k_k, g_k_a = b_grads[1:]
    dz_b, g_mu = _shift_bwd(z_b, mu, dzf)
    dq, dk, dv, dbg = _dn_bwd(qkv, bg, dn_hist, do_dn)
    dqkv = jnp.concatenate([dq, dk, dv], axis=1)
    dz_qkv, g_conv = _a_pre_bwd(z_qkv, full['a_conv_w'], dqkv)
    dz_bg, g_log_rate, g_dt_bias = _tok_bwd("bg_bwd", bg_fn, [z_bg], [log_rate, dt_bias], [[dbg]], [0], tb_mm)
    dz_zg = jnp.concatenate([daz, dga, dgb], axis=1)

    du = None
    g_w_in_parts = []
    for tag, dz, wpiece in (("qkv", dz_qkv, w_qkv), ("zg", dz_zg, w_zg), ("b", dz_b, w_b), ("bg", dz_bg, w_bg)):
        dz_bf = dz.astype(BF16)
        du = _mm("du_" + tag, dz_bf, wpiece, trans_b=True, add=du, tb=tb_mm, tn=d)
        g_w_in_parts.append(_mm_tn("dw_in_" + tag, u, dz_bf, tm=d, tn=_col_tile(dz.shape[1], 1536), tk=tb_mm))
    dh1, g_mix_norm = _tok_bwd("mix_norm_bwd", _norm_res_fn, [h1], [full['mix_norm']], [[du], [dh2]], [0], tb_vjp)
    dh0, g_ffn1_norm, g_ffn1_gu, g_ffn1_down = ffn_backward("ffn1", h0, dh1, 'ffn1_norm', 'ffn1_w_gu', 'ffn1_w_down')

    gp_qkv, gp_zg, gp_b, gp_bg = g_w_in_parts
    g_w_in = jnp.concatenate([gp_qkv, gp_zg[:, :d], gp_bg[:, :2 * heads_a], gp_b[:, :n_b], gp_zg[:, d:]], axis=1)

    big_grads = {'ffn1_w_gu': g_ffn1_gu, 'ffn1_w_down': g_ffn1_down, 'w_in': g_w_in, 'w_out': g_w_out,
                 'ffn2_w_gu': g_ffn2_gu, 'ffn2_w_down': g_ffn2_down}
    slabs = [(_cols_to_slabs(big_grads[k]) if k in col_sharded else big_grads[k].reshape(N_DEV, -1, d)).astype(BF16)
             for k in big]
    received = dict(zip(big, _exchange("exchange_grads", slabs)))

    small_full = {
        'meta_tokens': dh0[PAD:CHUNK], 'ffn1_norm': g_ffn1_norm, 'mix_norm': g_mix_norm, 'a_conv_w': g_conv,
        'a_log_rate': g_log_rate[:, heads_a:2 * heads_a], 'a_dt_bias': g_dt_bias[:, heads_a:2 * heads_a],
        'a_out_norm': g_out_norm, 'b_shift_mu': g_mu[:, :n_b], 'b_w0': g_w0, 'b_w_up': g_w_up[:LORA_W],
        'b_a0': g_a0, 'b_a_up': g_a_up[LORA_W:], 'b_g_up': g_g_up[:LORA_G], 'b_k_k': g_k_k, 'b_k_a': g_k_a,
        'b_r_k': g_r_k, 'b_ln_gain': g_ln_g, 'b_ln_bias': g_ln_b, 'ffn2_norm': g_ffn2_norm, 'final_norm': g_final,
    }
    small_names = list(small_full)
    packed = _pack([small_full[k] for k in small_names] + [loss_part[:, :1]])
    (all_parts,) = _all_gather("gather_small_grads", [packed])
    summed = _sum_slabs("sum_small_grads", all_parts, packed.shape[0])
    pieces = _unpack(summed, [small_full[k].shape for k in small_names] + [(1, 1)])
    small_grad = dict(zip(small_names, pieces[:-1]))
    loss = pieces[-1].reshape(())

    grads, deltas, new_m, new_v = {}, {}, {}, {}
    for k in big:
        shard = wts[k][0]
        rb = _tb(shard.shape[0], 128) if shard.shape[0] % 16 == 0 else shard.shape[0]
        g, dl, m2, v2 = _adamw("adamw_" + k, received[k], shard, mom_m[k][0], mom_v[k][0], rb)
        grads[k], deltas[k], new_m[k], new_v[k] = g[None], dl[None], m2[None], v2[None]

    local_small = {}
    for k in small_names:
        g = small_grad[k]
        if k in small_sharded:
            width = wts[k].shape[-1]
            g = lax.dynamic_slice_in_dim(g, me * width, width, axis=1)
        local_small[k] = g.reshape(wts[k].shape)
    pk = lambda tree: _pack([tree[k] for k in small_names])
    dl_s, m_s, v_s = _adamw_small(pk(wts), pk(local_small), pk(mom_m), pk(mom_v))
    shapes = [wts[k].shape for k in small_names]
    for k, dl, m2, v2 in zip(small_names, _unpack(dl_s, shapes), _unpack(m_s, shapes), _unpack(v_s, shapes)):
        grads[k], deltas[k], new_m[k], new_v[k] = local_small[k], dl, m2, v2

    grad_x = dh0[CHUNK:][None]
    return (loss, grad_x, *[grads[k] for k in names], *[deltas[k] for k in names],
            *[new_m[k] for k in names], *[new_v[k] for k in names])
```

```python
import functools

import jax
import jax.numpy as jnp
from jax import lax
from jax.experimental import pallas as pl
from jax.experimental.pallas import tpu as pltpu

F32 = jnp.float32
BF16 = jnp.bfloat16
N_DEV = 8
N_META = 16
CHUNK = 64
PAD = CHUNK - N_META
HEAD_A = 128
HEAD_B = 64
LORA_W, LORA_A, LORA_G = 64, 64, 160
LORA_PAD = 384
EPS = 1e-6
GN_EPS = HEAD_B * 1e-5
ADAM_LR, ADAM_B1, ADAM_B2, ADAM_EPS, ADAM_WD, ADAM_STEP = 0.001, 0.9, 0.999, 1e-08, 0.01, 10
SCAN_STEPS = 16
MXU_WIDTH = 256
VMEM_LIMIT = 56 * 1024 * 1024
DN_PRECISION = lax.Precision.HIGH
MESH_ID = pl.DeviceIdType.MESH
ANY = pl.BlockSpec(memory_space=pl.ANY)


def _cp(*sem):
    return pltpu.CompilerParams(dimension_semantics=sem, vmem_limit_bytes=VMEM_LIMIT)


def _tb(t, target):
    best = 16
    for d in range(16, target + 1, 16):
        if t % d == 0:
            best = d
    return best


def _sigmoid(x):
    return 1.0 / (1.0 + jnp.exp(-x))


def _silu(x):
    return x * _sigmoid(x)


def _softplus(x):
    return jnp.maximum(x, 0.0) + jnp.log(1.0 + jnp.exp(-jnp.abs(x)))


def _dot_nt(a, b, precision=None):
    return lax.dot_general(a, b, (((1,), (1,)), ((), ())), preferred_element_type=F32, precision=precision)


def _dot_tn(a, b, precision=None):
    return lax.dot_general(a, b, (((0,), (0,)), ((), ())), preferred_element_type=F32, precision=precision)


def _dot(a, b, precision=None):
    return jnp.dot(a, b, preferred_element_type=F32, precision=precision)


def _block_diag_ones():
    i = lax.broadcasted_iota(jnp.int32, (MXU_WIDTH, MXU_WIDTH), 0) // HEAD_B
    j = lax.broadcasted_iota(jnp.int32, (MXU_WIDTH, MXU_WIDTH), 1) // HEAD_B
    return (i == j).astype(BF16)


def _hi_lo(x):
    hi = x.astype(BF16)
    return hi, (x - hi.astype(F32)).astype(BF16)


def _segsum_many(xs, bd):
    groups = [x if isinstance(x, tuple) else _hi_lo(x) for x in xs]
    rows = groups[0][0].shape[0]
    stacked = jnp.concatenate([p for grp in groups for p in grp], axis=0)
    out = jnp.concatenate([_dot(stacked[:, s:s + MXU_WIDTH], bd) for s in range(0, stacked.shape[1], MXU_WIDTH)], axis=1)
    res, pos = [], 0
    for grp in groups:
        acc = out[pos:pos + rows]
        for j in range(1, len(grp)):
            acc = acc + out[pos + j * rows:pos + (j + 1) * rows]
        res.append(acc)
        pos += len(grp) * rows
    return res


def _segsum_impl(x):
    return _segsum_many([x], _block_diag_ones())[0]


@jax.custom_vjp
def _segsum64(x):
    return _segsum_impl(x)


_segsum64.defvjp(lambda x: (_segsum_impl(x), None), lambda _, ct: (_segsum_impl(ct),))


def _tok(t):
    return t if isinstance(t, tuple) else (t, t.shape[1], 0)


def _tok_spec(tb, width, colblk):
    return pl.BlockSpec((tb, width), lambda i: (i, colblk))


def _par_spec(p):
    return pl.BlockSpec(p.shape, lambda i: (0, 0))


def _tok_fwd(name, fn, toks, pars, outs, tb):
    toks = [_tok(t) for t in toks]
    rows = toks[0][0].shape[0]
    n_in = len(toks) + len(pars)

    def body(*refs):
        row0 = pl.program_id(0) * tb
        res = fn(row0, *[r[...] for r in refs[:n_in]])
        for r, o in zip(refs[n_in:], res):
            r[...] = o.astype(r.dtype)

    return pl.pallas_call(
        body, name=name, grid=(rows // tb,),
        in_specs=[_tok_spec(tb, w, c) for _, w, c in toks] + [_par_spec(p) for p in pars],
        out_specs=[_tok_spec(tb, w, 0) for w, _ in outs],
        out_shape=[jax.ShapeDtypeStruct((rows, w), dt) for w, dt in outs],
        compiler_params=_cp("parallel"),
    )(*[a for a, _, _ in toks], *pars)


def _tok_bwd(name, fn, toks, pars, cts, want, tb, want_dtypes=None):
    toks = [_tok(t) for t in toks]
    want_dtypes = want_dtypes or [F32] * len(want)
    cts = [[_tok(c) for c in group] for group in cts]
    flat_cts = [c for group in cts for c in group]
    rows = toks[0][0].shape[0]
    n_tok, n_par, n_ct = len(toks), len(pars), len(flat_cts)

    def body(*refs):
        i = pl.program_id(0)
        row0 = i * tb
        prim = [r[...].astype(F32) for r in refs[:n_tok + n_par]]
        ct_refs = list(refs[n_tok + n_par:n_tok + n_par + n_ct])
        out_refs = refs[n_tok + n_par + n_ct:]
        res, vjp = jax.vjp(lambda *a: fn(row0, *a), *prim)
        ct = []
        for group, o in zip(cts, res):
            acc = None
            for _ in group:
                v = ct_refs.pop(0)[...].astype(F32)
                acc = v if acc is None else acc + v
            ct.append(acc.astype(o.dtype))
        grads = vjp(tuple(ct))
        for r, k in zip(out_refs[:len(want)], want):
            r[...] = grads[k].astype(r.dtype)

        @pl.when(i == 0)
        def _():
            for r in out_refs[len(want):]:
                r[...] = jnp.zeros_like(r)

        for r, g in zip(out_refs[len(want):], grads[n_tok:]):
            r[...] += g

    return pl.pallas_call(
        body, name=name, grid=(rows // tb,),
        in_specs=[_tok_spec(tb, w, c) for _, w, c in toks] + [_par_spec(p) for p in pars]
        + [_tok_spec(tb, w, c) for _, w, c in flat_cts],
        out_specs=[_tok_spec(tb, toks[k][1], 0) for k in want] + [_par_spec(p) for p in pars],
        out_shape=[jax.ShapeDtypeStruct((rows, toks[k][1]), dt) for k, dt in zip(want, want_dtypes)]
        + [jax.ShapeDtypeStruct(p.shape, F32) for p in pars],
        compiler_params=_cp("arbitrary"),
    )(*[a for a, _, _ in toks], *pars, *[a for a, _, _ in flat_cts])


def _mm(name, a, b, *, trans_b=False, add=None, tb, tn):
    rows, k = a.shape
    n = b.shape[0] if trans_b else b.shape[1]

    def body(*refs):
        a_ref, b_ref = refs[:2]
        o_ref = refs[-1]
        acc = _dot_nt(a_ref[...], b_ref[...]) if trans_b else _dot(a_ref[...], b_ref[...])
        if add is not None:
            acc = acc + refs[2][...]
        o_ref[...] = acc

    in_specs = [pl.BlockSpec((tb, k), lambda i, j: (i, 0)),
                pl.BlockSpec((tn, k), lambda i, j: (j, 0)) if trans_b else pl.BlockSpec((k, tn), lambda i, j: (0, j))]
    args = [a, b]
    if add is not None:
        in_specs.append(pl.BlockSpec((tb, tn), lambda i, j: (i, j)))
        args.append(add)
    return pl.pallas_call(
        body, name=name, grid=(rows // tb, n // tn), in_specs=in_specs,
        out_specs=pl.BlockSpec((tb, tn), lambda i, j: (i, j)),
        out_shape=jax.ShapeDtypeStruct((rows, n), F32),
        compiler_params=_cp("parallel", "parallel"),
    )(*args)


def _mm_tn_call(name, grid, a, b, a_spec, b_spec, o_spec, acc_shape, out_shape):
    last = len(grid) - 1

    def body(a_ref, b_ref, o_ref, acc_ref):
        k = pl.program_id(last)

        @pl.when(k == 0)
        def _():
            acc_ref[...] = jnp.zeros_like(acc_ref)

        a_blk = a_ref[0] if len(a_ref.shape) == 3 else a_ref[...]
        b_blk = b_ref[0] if len(b_ref.shape) == 3 else b_ref[...]
        acc_ref[...] += _dot_tn(a_blk, b_blk)

        @pl.when(k == grid[last] - 1)
        def _():
            if len(o_ref.shape) == 3:
                o_ref[0] = acc_ref[...].astype(o_ref.dtype)
            else:
                o_ref[...] = acc_ref[...].astype(o_ref.dtype)

    return pl.pallas_call(
        body, name=name, grid=grid, in_specs=[a_spec, b_spec], out_specs=o_spec,
        out_shape=jax.ShapeDtypeStruct(out_shape, BF16), scratch_shapes=[pltpu.VMEM(acc_shape, F32)],
        compiler_params=_cp(*(["parallel"] * last + ["arbitrary"])),
    )(a, b)


def _mm_tn(name, a, b, *, tm, tn, tk):
    rows, m = a.shape
    n = b.shape[1]
    return _mm_tn_call(name, (m // tm, n // tn, rows // tk), a, b,
                       pl.BlockSpec((tk, tm), lambda i, j, k: (k, i)), pl.BlockSpec((tk, tn), lambda i, j, k: (k, j)),
                       pl.BlockSpec((tm, tn), lambda i, j, k: (i, j)), (tm, tn), (m, n))


def _mm_tn_to_slabs(name, a, b3, *, tk):
    rows, m = a.shape
    s, _, c = b3.shape
    return _mm_tn_call(name, (s, rows // tk), a, b3,
                       pl.BlockSpec((tk, m), lambda i, k: (k, 0)), pl.BlockSpec((1, tk, c), lambda i, k: (i, k, 0)),
                       pl.BlockSpec((1, m, c), lambda i, k: (i, 0, 0)), (m, c), (s, m, c))


def _mm_tn_from_slabs(name, a3, b, *, tk):
    s, rows, c = a3.shape
    n = b.shape[1]
    return _mm_tn_call(name, (s, rows // tk), a3, b,
                       pl.BlockSpec((1, tk, c), lambda i, k: (i, k, 0)), pl.BlockSpec((tk, n), lambda i, k: (k, 0)),
                       pl.BlockSpec((c, n), lambda i, k: (i, 0)), (c, n), (s * c, n))


def _col_tile(n, target):
    if n <= target:
        return n
    best = 128
    for d in range(128, target + 1, 128):
        if n % d == 0:
            best = d
    return best


def _rms(x, gain):
    return x * lax.rsqrt(jnp.mean(x * x, axis=-1, keepdims=True) + EPS) * gain


def _ffn_specs(d, fc, nj):
    return [pl.BlockSpec((1, d, fc), lambda i, j: (j, 0, 0)), pl.BlockSpec((1, d, fc), lambda i, j: (nj + j, 0, 0)),
            pl.BlockSpec((fc, d), lambda i, j: (j, 0))]


def _ffn_fwd(name, h, gain, wgu, wd, tb):
    rows, d = h.shape
    nj = wgu.shape[0] // 2
    fc = wgu.shape[2]

    def body(h_ref, g_ref, wg_ref, wu_ref, wd_ref, o_ref, xn_s, acc_s):
        j = pl.program_id(1)

        @pl.when(j == 0)
        def _():
            xn_s[...] = _rms(h_ref[...], g_ref[...]).astype(BF16)
            acc_s[...] = jnp.zeros_like(acc_s)

        xn = xn_s[...]
        gate = _dot(xn, wg_ref[0])
        up = _dot(xn, wu_ref[0])
        acc_s[...] += _dot((_silu(gate) * up).astype(BF16), wd_ref[...])

        @pl.when(j == nj - 1)
        def _():
            o_ref[...] = h_ref[...] + 0.5 * acc_s[...]

    return pl.pallas_call(
        body, name=name, grid=(rows // tb, nj),
        in_specs=[pl.BlockSpec((tb, d), lambda i, j: (i, 0)), pl.BlockSpec((1, d), lambda i, j: (0, 0))]
        + _ffn_specs(d, fc, nj),
        out_specs=pl.BlockSpec((tb, d), lambda i, j: (i, 0)),
        out_shape=jax.ShapeDtypeStruct((rows, d), F32),
        scratch_shapes=[pltpu.VMEM((tb, d), BF16), pltpu.VMEM((tb, d), F32)],
        compiler_params=_cp("parallel", "arbitrary"),
    )(h, gain, wgu, wgu, wd)


def _ffn_bwd(name, h, gain, dout, wgu, wd, tb):
    rows, d = h.shape
    nj = wgu.shape[0] // 2
    fc = wgu.shape[2]

    def body(h_ref, g_ref, do_ref, wg_ref, wu_ref, wd_ref,
             dh_ref, dhb_ref, dg_ref, xn_ref, act_ref, dgate_ref, dup_ref, dhalf_ref, dxn_s):
        i, j = pl.program_id(0), pl.program_id(1)

        @pl.when(j == 0)
        def _():
            xn_ref[...] = _rms(h_ref[...], g_ref[...]).astype(BF16)
            dhalf_ref[...] = (0.5 * do_ref[...]).astype(BF16)
            dxn_s[...] = jnp.zeros_like(dxn_s)

        xn = xn_ref[...]
        wg, wu = wg_ref[0], wu_ref[0]
        gate = _dot(xn, wg)
        up = _dot(xn, wu)
        sg = _sigmoid(gate)
        dact = _dot_nt(dhalf_ref[...], wd_ref[...])
        act_ref[0] = (gate * sg * up).astype(BF16)
        dgate = (dact * up * (sg * (1.0 + gate * (1.0 - sg)))).astype(BF16)
        dup = (dact * gate * sg).astype(BF16)
        dgate_ref[0] = dgate
        dup_ref[0] = dup
        dxn_s[...] += _dot_nt(dgate, wg) + _dot_nt(dup, wu)

        @pl.when((i == 0) & (j == 0))
        def _():
            dg_ref[...] = jnp.zeros_like(dg_ref)

        @pl.when(j == nj - 1)
        def _():
            x = h_ref[...]
            r = lax.rsqrt(jnp.mean(x * x, axis=-1, keepdims=True) + EPS)
            dxn = dxn_s[...]
            dyg = dxn * g_ref[...]
            dh = do_ref[...] + r * dyg - x * (r * r * r) * jnp.mean(dyg * x, axis=-1, keepdims=True)
            dh_ref[...] = dh
            dhb_ref[...] = dh.astype(BF16)
            dg_ref[...] += jnp.sum(dxn * x * r, axis=0, keepdims=True)

    row_d = pl.BlockSpec((tb, d), lambda i, j: (i, 0))
    slab = pl.BlockSpec((1, tb, fc), lambda i, j: (j, i, 0))
    hidden = jax.ShapeDtypeStruct((nj, rows, fc), BF16)
    return pl.pallas_call(
        body, name=name, grid=(rows // tb, nj),
        in_specs=[row_d, pl.BlockSpec((1, d), lambda i, j: (0, 0)), row_d] + _ffn_specs(d, fc, nj),
        out_specs=[row_d, row_d, pl.BlockSpec((1, d), lambda i, j: (0, 0)), row_d, slab, slab, slab, row_d],
        out_shape=[jax.ShapeDtypeStruct((rows, d), F32), jax.ShapeDtypeStruct((rows, d), BF16),
                   jax.ShapeDtypeStruct((1, d), F32), jax.ShapeDtypeStruct((rows, d), BF16),
                   hidden, hidden, hidden, jax.ShapeDtypeStruct((rows, d), BF16)],
        scratch_shapes=[pltpu.VMEM((tb, d), F32)],
        compiler_params=_cp("arbitrary", "arbitrary"),
    )(h, gain, dout, wgu, wgu, wd)


def _shift_rows(x, s):
    return pltpu.roll(x, s % x.shape[0], 0)


def _a_post(c, which):
    s = _silu(c)
    n = s * lax.rsqrt(jnp.sum(s * s, axis=-1, keepdims=True) + 1e-6)
    scale = jnp.where(which == 0, HEAD_A ** -0.5, 1.0)
    return jnp.where(which == 2, s, n * scale)


def _conv(x, w):
    return x * w[3:4] + _shift_rows(x, 1) * w[2:3] + _shift_rows(x, 2) * w[1:2] + _shift_rows(x, 3) * w[0:1]


def _a_pre_fwd(zqkv, conv_w):
    rows, width = zqkv.shape
    heads = width // (3 * HEAD_A)

    def body(x_ref, w_ref, o_ref):
        which = pl.program_id(0) // heads
        live = lax.broadcasted_iota(jnp.int32, (rows, HEAD_A), 0) >= PAD
        o_ref[...] = jnp.where(live, _a_post(_conv(x_ref[...], w_ref[...]), which), 0.0)

    return pl.pallas_call(
        body, name="a_pre_fwd", grid=(width // HEAD_A,),
        in_specs=[pl.BlockSpec((rows, HEAD_A), lambda c: (0, c)), pl.BlockSpec((4, HEAD_A), lambda c: (0, c))],
        out_specs=pl.BlockSpec((rows, HEAD_A), lambda c: (0, c)),
        out_shape=jax.ShapeDtypeStruct((rows, width), F32),
        compiler_params=_cp("parallel"),
    )(zqkv, conv_w)


def _a_pre_bwd(zqkv, conv_w, dqkv):
    rows, width = zqkv.shape
    heads = width // (3 * HEAD_A)

    def body(x_ref, w_ref, ct_ref, dx_ref, dw_ref):
        which = pl.program_id(0) // heads
        live = lax.broadcasted_iota(jnp.int32, (rows, HEAD_A), 0) >= PAD
        x, w = x_ref[...], w_ref[...]
        _, vjp = jax.vjp(lambda c: _a_post(c, which), _conv(x, w))
        (dc,) = vjp(jnp.where(live, ct_ref[...], 0.0))
        dc = jnp.where(live, dc, 0.0)
        dx_ref[...] = (dc * w[3:4] + _shift_rows(dc, -1) * w[2:3] + _shift_rows(dc, -2) * w[1:2]
                       + _shift_rows(dc, -3) * w[0:1]).astype(BF16)
        dw_ref[...] = jnp.concatenate(
            [jnp.sum(dc * (_shift_rows(x, 3 - j) if j < 3 else x), axis=0, keepdims=True) for j in range(4)], axis=0)

    col = pl.BlockSpec((rows, HEAD_A), lambda c: (0, c))
    wsp = pl.BlockSpec((4, HEAD_A), lambda c: (0, c))
    return pl.pallas_call(
        body, name="a_pre_bwd", grid=(width // HEAD_A,),
        in_specs=[col, wsp, col], out_specs=[col, wsp],
        out_shape=[jax.ShapeDtypeStruct((rows, width), BF16), jax.ShapeDtypeStruct((4, width), F32)],
        compiler_params=_cp("parallel"),
    )(zqkv, conv_w, dqkv)


SHIFT_TILE = 384


def _shift_fwd(zb, mu):
    rows, width = zb.shape

    def body(x_ref, mu_ref, o_ref):
        x = x_ref[...]
        first = lax.broadcasted_iota(jnp.int32, x.shape, 0) == 0
        prev = jnp.where(first, 0.0, _shift_rows(x, 1))
        o_ref[...] = x + (prev - x) * mu_ref[...]

    col = pl.BlockSpec((rows, SHIFT_TILE), lambda c: (0, c))
    return pl.pallas_call(
        body, name="shift_fwd", grid=(width // SHIFT_TILE,),
        in_specs=[col, pl.BlockSpec((1, SHIFT_TILE), lambda c: (0, c))], out_specs=col,
        out_shape=jax.ShapeDtypeStruct((rows, width), F32), compiler_params=_cp("parallel"),
    )(zb, mu)


def _shift_bwd(zb, mu, dzf):
    rows, width = zb.shape

    def body(x_ref, mu_ref, ct_ref, dx_ref, dmu_ref):
        x, ct, mu_v = x_ref[...], ct_ref[...], mu_ref[...]
        row = lax.broadcasted_iota(jnp.int32, x.shape, 0)
        prev = jnp.where(row == 0, 0.0, _shift_rows(x, 1))
        nxt = jnp.where(row == rows - 1, 0.0, _shift_rows(ct, -1))
        dx_ref[...] = (ct * (1.0 - mu_v) + nxt * mu_v).astype(BF16)
        dmu_ref[...] = jnp.sum(ct * (prev - x), axis=0, keepdims=True)

    col = pl.BlockSpec((rows, SHIFT_TILE), lambda c: (0, c))
    msp = pl.BlockSpec((1, SHIFT_TILE), lambda c: (0, c))
    return pl.pallas_call(
        body, name="shift_bwd", grid=(width // SHIFT_TILE,),
        in_specs=[col, msp, col], out_specs=[col, msp],
        out_shape=[jax.ShapeDtypeStruct((rows, width), BF16), jax.ShapeDtypeStruct((1, width), F32)],
        compiler_params=_cp("parallel"),
    )(zb, mu, dzf)


def _dn_chunk(q, k, v, beta, g, state):
    heads = range(len(q))
    ri = lax.broadcasted_iota(jnp.int32, (CHUNK, CHUNK), 0)
    ci = lax.broadcasted_iota(jnp.int32, (CHUNK, CHUNK), 1)
    eye = (ri == ci).astype(F32)
    incl = ri >= ci
    last = lax.broadcasted_iota(jnp.int32, (CHUNK, 1), 0) == CHUNK - 1
    g_row = [jnp.sum(g[h] * eye, axis=0, keepdims=True) for h in heads]
    gc = [jnp.sum(jnp.where(incl, g_row[h], 0.0), axis=1, keepdims=True) for h in heads]
    gc_row = [jnp.sum(gc[h] * eye, axis=0, keepdims=True) for h in heads]
    decay = [jnp.where(incl, jnp.exp(jnp.where(incl, gc[h] - gc_row[h], 0.0)), 0.0) for h in heads]
    kb = [k[h] * beta[h] for h in heads]
    vb = [v[h] * beta[h] for h in heads]
    p = [-jnp.where(ri > ci, _dot_nt(kb[h], k[h], DN_PRECISION) * decay[h], 0.0) for h in heads]
    tinv = [eye + p[h] for h in heads]
    for _ in range(5):
        p = [_dot(p[h], p[h], DN_PRECISION) for h in heads]
        tinv = [tinv[h] + _dot(tinv[h], p[h], DN_PRECISION) for h in heads]
    eg = [jnp.exp(gc[h]) for h in heads]
    u = [_dot(tinv[h], vb[h], DN_PRECISION) for h in heads]
    wk = [_dot(tinv[h], kb[h] * eg[h], DN_PRECISION) for h in heads]
    attn = [_dot_nt(q[h], k[h], DN_PRECISION) * decay[h] for h in heads]
    g_last = [jnp.sum(jnp.where(last, gc[h], 0.0), axis=0, keepdims=True) for h in heads]
    k_tail = [k[h] * jnp.exp(g_last[h] - gc[h]) for h in heads]
    v_new = [u[h] - _dot(wk[h], state[h], DN_PRECISION) for h in heads]
    o = [_dot(q[h] * eg[h], state[h], DN_PRECISION) + _dot(attn[h], v_new[h], DN_PRECISION) for h in heads]
    new = [state[h] * jnp.exp(g_last[h]) + _dot_tn(k_tail[h], v_new[h], DN_PRECISION) for h in heads]
    return o, new


def _bg_cols(bg, h, heads):
    lane = lax.broadcasted_iota(jnp.int32, bg.shape, 1)
    beta = jnp.sum(jnp.where(lane == h, bg, 0.0), axis=1, keepdims=True)
    g = jnp.sum(jnp.where(lane == heads + h, bg, 0.0), axis=1, keepdims=True)
    return beta, g


def _dn_fwd(qkv, bg):
    rows = qkv.shape[0]
    heads = qkv.shape[1] // (3 * HEAD_A)
    n = rows // CHUNK
    hp, groups = heads, 1

    def body(q_ref, k_ref, v_ref, bg_ref, o_ref, hist_ref, s_ref):
        c, grp = pl.program_id(0), pl.program_id(1)

        @pl.when(c == 0)
        def _():
            for i in range(hp):
                s_ref[grp * hp + i] = jnp.zeros((HEAD_A, HEAD_A), F32)

        bg_v = bg_ref[...]
        cols = [slice(i * HEAD_A, (i + 1) * HEAD_A) for i in range(hp)]
        state = [s_ref[grp * hp + i] for i in range(hp)]
        beta_g = [_bg_cols(bg_v, grp * hp + i, heads) for i in range(hp)]
        o, new = _dn_chunk([q_ref[:, c_] for c_ in cols], [k_ref[:, c_] for c_ in cols], [v_ref[:, c_] for c_ in cols],
                           [b for b, _ in beta_g], [g for _, g in beta_g], state)
        for i in range(hp):
            hist_ref[0, i] = state[i]
            o_ref[:, cols[i]] = o[i]
            s_ref[grp * hp + i] = new[i]

    def part(p):
        return pl.BlockSpec((CHUNK, hp * HEAD_A), lambda c, grp: (c, p * groups + grp))

    return pl.pallas_call(
        body, name="deltanet_fwd", grid=(n, groups),
        in_specs=[part(0), part(1), part(2), pl.BlockSpec((CHUNK, 128), lambda c, grp: (c, 0))],
        out_specs=[part(0), pl.BlockSpec((1, hp, HEAD_A, HEAD_A), lambda c, grp: (c, grp, 0, 0))],
        out_shape=[jax.ShapeDtypeStruct((rows, heads * HEAD_A), F32),
                   jax.ShapeDtypeStruct((n, heads, HEAD_A, HEAD_A), F32)],
        scratch_shapes=[pltpu.VMEM((heads, HEAD_A, HEAD_A), F32)],
        compiler_params=_cp("arbitrary", "arbitrary"),
    )(qkv, qkv, qkv, bg)


def _dn_bwd(qkv, bg, hist, do):
    rows = qkv.shape[0]
    heads = qkv.shape[1] // (3 * HEAD_A)
    n = rows // CHUNK
    hp, groups = heads, 1

    def body(q_ref, k_ref, v_ref, bg_ref, hist_ref, do_ref, dqkv_ref, dbg_ref, ds_ref):
        c, grp = pl.program_id(0), pl.program_id(1)

        @pl.when(c == 0)
        def _():
            for i in range(hp):
                ds_ref[grp * hp + i] = jnp.zeros((HEAD_A, HEAD_A), F32)

        bg_v = bg_ref[...]
        lane = lax.broadcasted_iota(jnp.int32, (CHUNK, 128), 1)
        cols = [slice(i * HEAD_A, (i + 1) * HEAD_A) for i in range(hp)]
        beta_g = [_bg_cols(bg_v, grp * hp + i, heads) for i in range(hp)]
        _, vjp = jax.vjp(_dn_chunk, [q_ref[:, c_] for c_ in cols], [k_ref[:, c_] for c_ in cols],
                         [v_ref[:, c_] for c_ in cols], [b for b, _ in beta_g], [g for _, g in beta_g],
                         [hist_ref[0, i] for i in range(hp)])
        dq, dk, dv, dbeta, dg, ds = vjp(([do_ref[:, c_] for c_ in cols], [ds_ref[grp * hp + i] for i in range(hp)]))
        dbg = jnp.zeros((CHUNK, 128), F32)
        for i in range(hp):
            h = grp * hp + i
            for p, part_grad in enumerate((dq, dk, dv)):
                dqkv_ref[:, pl.ds((p * heads + i) * HEAD_A, HEAD_A)] = part_grad[i]
            ds_ref[h] = ds[i]
            dbg = dbg + jnp.where(lane == h, dbeta[i], 0.0) + jnp.where(lane == heads + h, dg[i], 0.0)

        @pl.when(grp == 0)
        def _():
            dbg_ref[...] = jnp.zeros_like(dbg_ref)

        dbg_ref[...] += dbg

    def part(p):
        return pl.BlockSpec((CHUNK, hp * HEAD_A), lambda c, grp: (n - 1 - c, p * groups + grp))

    return pl.pallas_call(
        body, name="deltanet_bwd", grid=(n, groups),
        in_specs=[part(0), part(1), part(2), pl.BlockSpec((CHUNK, 128), lambda c, grp: (n - 1 - c, 0)),
                  pl.BlockSpec((1, hp, HEAD_A, HEAD_A), lambda c, grp: (n - 1 - c, grp, 0, 0)), part(0)],
        out_specs=[pl.BlockSpec((CHUNK, 3 * heads * HEAD_A), lambda c, grp: (n - 1 - c, 0)),
                   pl.BlockSpec((CHUNK, 128), lambda c, grp: (n - 1 - c, 0))],
        out_shape=[jax.ShapeDtypeStruct(qkv.shape, F32), jax.ShapeDtypeStruct((rows, 128), F32)],
        scratch_shapes=[pltpu.VMEM((heads, HEAD_A, HEAD_A), F32)],
        compiler_params=_cp("arbitrary", "arbitrary"),
    )(qkv, qkv, qkv, bg, hist, do)


def _diag_mask(width):
    return (lax.broadcasted_iota(jnp.int32, (HEAD_B, width), 0)
            == lax.broadcasted_iota(jnp.int32, (HEAD_B, width), 1) % HEAD_B).astype(F32)


def _rwkv_fwd(r, w, k, v, a, b):
    rows, width = r.shape
    ts = SCAN_STEPS

    def body(r_ref, w_ref, k_ref, v_ref, a_ref, b_ref, y_ref, hist_ref, s_ref):
        @pl.when(pl.program_id(0) == 0)
        def _():
            s_ref[...] = jnp.zeros_like(s_ref)

        eye = _diag_mask(width)
        bd = _block_diag_ones()

        for j in range(ts):
            row = pl.ds(j, 1)
            s = s_ref[...]
            hist_ref[j] = s
            v_t = v_ref[row, :]
            v_hi = v_t.astype(BF16).astype(F32)
            spread_v = ((eye * v_hi).astype(BF16), (eye * (v_t - v_hi)).astype(BF16))
            sa, vb = _segsum_many([s * a_ref[row, :], spread_v], bd)
            s = s * w_ref[row, :] + sa * b_ref[row, :] + vb * k_ref[row, :]
            (yb,) = _segsum_many([((s * r_ref[row, :]).astype(BF16),)], bd)
            y_ref[row, :] = jnp.sum(yb * eye, axis=0, keepdims=True)
            s_ref[...] = s

    blk = pl.BlockSpec((ts, width), lambda i: (i, 0))
    return pl.pallas_call(
        body, name="rwkv_fwd", grid=(rows // ts,),
        in_specs=[blk] * 6,
        out_specs=[blk, pl.BlockSpec((ts, HEAD_B, width), lambda i: (i, 0, 0))],
        out_shape=[jax.ShapeDtypeStruct((rows, width), F32), jax.ShapeDtypeStruct((rows, HEAD_B, width), F32)],
        scratch_shapes=[pltpu.VMEM((HEAD_B, width), F32)],
        compiler_params=_cp("arbitrary"),
    )(r, w, k, v, a, b)


def _rwkv_bwd(r, w, k, v, a, b, hist, dy):
    rows, width = r.shape
    ts = SCAN_STEPS
    nb = rows // ts

    def body(r_ref, w_ref, k_ref, v_ref, a_ref, b_ref, hist_ref, dy_ref,
             dr_ref, dw_ref, dk_ref, dv_ref, da_ref, db_ref, g_ref):
        @pl.when(pl.program_id(0) == 0)
        def _():
            g_ref[...] = jnp.zeros_like(g_ref)

        eye = _diag_mask(width)
        bd = _block_diag_ones()

        def colsum(x):
            return jnp.sum(x, axis=0, keepdims=True)

        for j in reversed(range(ts)):
            row = pl.ds(j, 1)
            sp = hist_ref[j]
            a_t, b_t, k_t, w_t, r_t = a_ref[row, :], b_ref[row, :], k_ref[row, :], w_ref[row, :], r_ref[row, :]
            sa, vb, dyb = _segsum_many([((sp * a_t).astype(BF16),), ((eye * v_ref[row, :]).astype(BF16),),
                                        ((eye * dy_ref[row, :]).astype(BF16),)], bd)
            st = sp * w_t + sa * b_t + vb * k_t
            g = g_ref[...] + dyb * r_t
            dr_ref[row, :] = colsum(st * dyb)
            dw_ref[row, :] = colsum(g * sp)
            dk_ref[row, :] = colsum(g * vb)
            db_ref[row, :] = colsum(g * sa)
            dsa, dvb = _segsum_many([g * b_t, ((g * k_t).astype(BF16),)], bd)
            dv_ref[row, :] = colsum(dvb * eye)
            da_ref[row, :] = colsum(sp * dsa)
            g_ref[...] = g * w_t + dsa * a_t

    blk = pl.BlockSpec((ts, width), lambda i: (nb - 1 - i, 0))
    return pl.pallas_call(
        body, name="rwkv_bwd", grid=(nb,),
        in_specs=[blk] * 6 + [pl.BlockSpec((ts, HEAD_B, width), lambda i: (nb - 1 - i, 0, 0)), blk],
        out_specs=[blk] * 6,
        out_shape=[jax.ShapeDtypeStruct((rows, width), F32)] * 6,
        scratch_shapes=[pltpu.VMEM((HEAD_B, width), F32)],
        compiler_params=_cp("arbitrary"),
    )(r, w, k, v, a, b, hist, dy)


def _live(row0, shape):
    return (row0 + lax.broadcasted_iota(jnp.int32, shape, 0)) >= PAD


def _norm_fn(row0, h, gain):
    return (_rms(h, gain),)


def _norm_res_fn(row0, h, gain):
    return _rms(h, gain), h


def _make_bg_fn(heads):
    def fn(row0, x, log_rate, dt_bias):
        lane = lax.broadcasted_iota(jnp.int32, x.shape, 1)
        beta = _sigmoid(x)
        g = -jnp.exp(log_rate) * _softplus(x + dt_bias)
        out = jnp.where(lane < heads, beta, jnp.where(lane < 2 * heads, g, 0.0))
        return (jnp.where(_live(row0, x.shape), out, 0.0),)
    return fn


def _b_pre_fn(row0, zf, w0, w_up, a0, a_up, g_up, k_k, k_a):
    d = w0.shape[1]
    r, k, v = zf[:, :d], zf[:, d:2 * d], zf[:, 2 * d:3 * d]
    lo = zf[:, 3 * d:3 * d + 128]
    lg = zf[:, 3 * d + 128:3 * d + LORA_PAD]
    lane = lax.broadcasted_iota(jnp.int32, lo.shape, 1)
    lw = _dot(jnp.where(lane < LORA_W, jnp.tanh(lo), 0.0), w_up)
    la = _dot(jnp.where(lane >= LORA_W, lo, 0.0), a_up)
    lane_g = lax.broadcasted_iota(jnp.int32, lg.shape, 1)
    gate = _dot(jnp.where(lane_g < LORA_G, _sigmoid(lg), 0.0), g_up)
    decay = jnp.exp(-jnp.exp(-_softplus(-(w0 + lw)) - 0.5))
    a = _sigmoid(a0 + la)
    kx = k * k_k
    kk = kx * lax.rsqrt(_segsum64(kx * kx) + 1e-6)
    k2 = k * (1.0 + (a - 1.0) * k_a)
    return r, decay, k2, v, -kk, kk * a, gate


def _post_fn(row0, o, zg, y, r, k2, v, gate, out_gain, r_k, ln_g, ln_b):
    d = o.shape[1]
    az, ga, gb = zg[:, :d], zg[:, d:2 * d], zg[:, 2 * d:]
    heads = d // HEAD_A
    parts = []
    for h in range(heads):
        oh = o[:, h * HEAD_A:(h + 1) * HEAD_A]
        parts.append(oh * lax.rsqrt(jnp.mean(oh * oh, axis=-1, keepdims=True) + EPS) * out_gain)
    o_a = jnp.concatenate(parts, axis=1) * _silu(az)
    mean = _segsum64(y) * (1.0 / HEAD_B)
    yc = y - mean
    var = _segsum64(yc * yc) * (1.0 / HEAD_B)
    yn = yc * lax.rsqrt(var + GN_EPS) * ln_g + ln_b
    o_b = (yn + _segsum64(r * k2 * r_k) * v) * gate
    return (_sigmoid(ga) * o_a + _sigmoid(gb) * o_b,)


def _loss(h3, target, gain, tb):
    rows, d = h3.shape

    def body(h_ref, t_ref, g_ref, dh_ref, dg_ref, l_ref):
        i = pl.program_id(0)
        live = (i * tb + lax.broadcasted_iota(jnp.int32, (tb, 1), 0)) >= CHUNK
        tgt = t_ref[...]

        def f(h, g):
            err = _rms(h, g) - tgt
            return 0.5 * jnp.sum(jnp.where(live, jnp.mean(err * err, axis=-1, keepdims=True), 0.0))

        val, vjp = jax.vjp(f, h_ref[...], g_ref[...])
        dh, dg = vjp(jnp.ones((), F32))
        dh_ref[...] = dh

        @pl.when(i == 0)
        def _():
            dg_ref[...] = jnp.zeros_like(dg_ref)
            l_ref[...] = jnp.zeros_like(l_ref)

        dg_ref[...] += dg
        l_ref[...] += jnp.full((1, 128), val, F32)

    blk = pl.BlockSpec((tb, d), lambda i: (i, 0))
    return pl.pallas_call(
        body, name="loss", grid=(rows // tb,),
        in_specs=[blk, blk, pl.BlockSpec((1, d), lambda i: (0, 0))],
        out_specs=[blk, pl.BlockSpec((1, d), lambda i: (0, 0)), pl.BlockSpec((1, 128), lambda i: (0, 0))],
        out_shape=[jax.ShapeDtypeStruct((rows, d), F32), jax.ShapeDtypeStruct((1, d), F32),
                   jax.ShapeDtypeStruct((1, 128), F32)],
        compiler_params=_cp("arbitrary"),
    )(h3, target, gain)


def _adamw_math(w, g, m, v):
    m2 = ADAM_B1 * m + (1.0 - ADAM_B1) * g
    v2 = ADAM_B2 * v + (1.0 - ADAM_B2) * (g * g)
    m_hat = m2 / (1.0 - ADAM_B1 ** ADAM_STEP)
    v_hat = v2 / (1.0 - ADAM_B2 ** ADAM_STEP)
    return -ADAM_LR * (m_hat / (jnp.sqrt(v_hat) + ADAM_EPS) + ADAM_WD * w), m2, v2


def _adamw(name, slabs, w, m, v, rb):
    rows, cols = w.shape

    def body(s_ref, w_ref, m_ref, v_ref, g_ref, d_ref, m2_ref, v2_ref):
        g = s_ref[0].astype(F32)
        for dev in range(1, N_DEV):
            g = g + s_ref[dev].astype(F32)
        g_ref[...] = g
        d_ref[...], m2_ref[...], v2_ref[...] = _adamw_math(w_ref[...], g, m_ref[...], v_ref[...])

    blk = pl.BlockSpec((rb, cols), lambda i: (i, 0))
    return pl.pallas_call(
        body, name=name, grid=(rows // rb,),
        in_specs=[pl.BlockSpec((N_DEV, rb, cols), lambda i: (0, i, 0)), blk, blk, blk],
        out_specs=[blk] * 4, out_shape=[jax.ShapeDtypeStruct((rows, cols), F32)] * 4,
        compiler_params=_cp("parallel"),
    )(slabs, w, m, v)


def _sum_slabs(name, slabs, rb):
    _, rows, cols = slabs.shape

    def body(s_ref, o_ref):
        g = s_ref[0]
        for dev in range(1, N_DEV):
            g = g + s_ref[dev]
        o_ref[...] = g

    return pl.pallas_call(
        body, name=name, grid=(rows // rb,),
        in_specs=[pl.BlockSpec((N_DEV, rb, cols), lambda i: (0, i, 0))],
        out_specs=pl.BlockSpec((rb, cols), lambda i: (i, 0)),
        out_shape=jax.ShapeDtypeStruct((rows, cols), F32), compiler_params=_cp("parallel"),
    )(slabs)


def _adamw_small(w, g, m, v):
    def body(w_ref, g_ref, m_ref, v_ref, d_ref, m2_ref, v2_ref):
        d_ref[...], m2_ref[...], v2_ref[...] = _adamw_math(w_ref[...], g_ref[...], m_ref[...], v_ref[...])

    return pl.pallas_call(body, name="adamw_small", out_shape=[jax.ShapeDtypeStruct(w.shape, F32)] * 3)(w, g, m, v)


def _place():
    return lax.axis_index("x"), lax.axis_index("y"), lax.axis_index("c")


def _index(p):
    return 4 * p[0] + 2 * p[1] + p[2]


def _all_gather(name, xs):
    n = len(xs)

    def body(*refs):
        x_refs, o_refs = refs[:n], refs[n:2 * n]
        send_sems, recv_sems, local_sems = refs[2 * n:]
        x, y, c = _place()
        me, sibling = (x, y, c), (x, y, 1 - c)
        chips = [(1 - x, y), (x, 1 - y), (1 - x, 1 - y)]

        def copy(i, k, block, to, src=None):
            dst = o_refs[i].at[_index(block)]
            return pltpu.make_async_remote_copy(src_ref=dst if src is None else src, dst_ref=dst,
                                                send_sem=send_sems.at[i, k], recv_sem=recv_sems.at[i, k],
                                                device_id=to, device_id_type=MESH_ID)

        mine = [pltpu.make_async_copy(x_refs[i], o_refs[i].at[_index(me)], local_sems.at[i]) for i in range(n)]
        for cp in mine:
            cp.start()
        first = []
        for i in range(n):
            first.append(copy(i, 0, me, sibling, src=x_refs[i]))
            first += [copy(i, 1 + j, me, (*chip, c), src=x_refs[i]) for j, chip in enumerate(chips)]
        for cp in first:
            cp.start()
        passed = []
        for j, chip in enumerate(chips):
            for i in range(n):
                copy(i, 1 + j, (*chip, c), me).wait_recv()
                cp = copy(i, 4 + j, (*chip, c), sibling)
                cp.start()
                passed.append(cp)
        for i in range(n):
            copy(i, 0, sibling, me).wait_recv()
            for j, chip in enumerate(chips):
                copy(i, 4 + j, (*chip, 1 - c), me).wait_recv()
        for cp in first + passed:
            cp.wait_send()
        for cp in mine:
            cp.wait()

    return pl.pallas_call(
        body, name=name, in_specs=[ANY] * n, out_specs=[ANY] * n,
        out_shape=[jax.ShapeDtypeStruct((N_DEV,) + x.shape, x.dtype) for x in xs],
        scratch_shapes=[pltpu.SemaphoreType.DMA((n, 7)), pltpu.SemaphoreType.DMA((n, 7)), pltpu.SemaphoreType.DMA((n,))],
    )(*xs)


def _exchange(name, xs):
    n = len(xs)

    def body(*refs):
        x_refs, o_refs = refs[:n], refs[n:2 * n]
        send_sems, recv_sems, local_sems = refs[2 * n:]
        x, y, c = _place()
        me = (x, y, c)
        peers = [((1 - x) if k & 4 else x, (1 - y) if k & 2 else y, (1 - c) if k & 1 else c) for k in range(1, N_DEV)]

        def copy(i, k, peer):
            return pltpu.make_async_remote_copy(src_ref=x_refs[i].at[_index(peer)], dst_ref=o_refs[i].at[_index(me)],
                                                send_sem=send_sems.at[i, k], recv_sem=recv_sems.at[i, k],
                                                device_id=peer, device_id_type=MESH_ID)

        def arrival(i, k, peer):
            return pltpu.make_async_remote_copy(src_ref=x_refs[i].at[_index(peer)], dst_ref=o_refs[i].at[_index(peer)],
                                                send_sem=send_sems.at[i, k], recv_sem=recv_sems.at[i, k],
                                                device_id=peer, device_id_type=MESH_ID)

        mine = [pltpu.make_async_copy(x_refs[i].at[_index(me)], o_refs[i].at[_index(me)], local_sems.at[i])
                for i in range(n)]
        for cp in mine:
            cp.start()
        sent = [copy(i, k, peer) for k, peer in enumerate(peers) for i in range(n)]
        for cp in sent:
            cp.start()
        for k, peer in enumerate(peers):
            for i in range(n):
                arrival(i, k, peer).wait_recv()
        for cp in sent:
            cp.wait_send()
        for cp in mine:
            cp.wait()

    return pl.pallas_call(
        body, name=name, in_specs=[ANY] * n, out_specs=[ANY] * n,
        out_shape=[jax.ShapeDtypeStruct(x.shape, x.dtype) for x in xs],
        scratch_shapes=[pltpu.SemaphoreType.DMA((n, 7)), pltpu.SemaphoreType.DMA((n, 7)), pltpu.SemaphoreType.DMA((n,))],
    )(*xs)


def _pack(arrays):
    flat = jnp.concatenate([a.reshape(-1) for a in arrays])
    pad = (-flat.shape[0]) % 1024
    return jnp.pad(flat, (0, pad)).reshape(-1, 128)


def _unpack(packed, shapes):
    flat = packed.reshape(-1)
    out, pos = [], 0
    for s in shapes:
        size = 1
        for dim in s:
            size *= dim
        out.append(flat[pos:pos + size].reshape(s))
        pos += size
    return out


def _cols_from_slabs(stack):
    return jnp.transpose(stack, (1, 0, 2)).reshape(stack.shape[1], -1)


def _cols_to_slabs(full):
    return jnp.transpose(full.reshape(full.shape[0], N_DEV, -1), (1, 0, 2))


def kernel(x, meta_tokens, ffn1_norm, ffn1_w_gu, ffn1_w_down, mix_norm, w_in, a_conv_w, a_log_rate, a_dt_bias, a_out_norm, b_shift_mu, b_w0, b_w_up, b_a0, b_a_up, b_g_up, b_k_k, b_k_a, b_r_k, b_ln_gain, b_ln_bias, w_out, ffn2_norm, ffn2_w_gu, ffn2_w_down, final_norm, loss_target, m_meta_tokens, m_ffn1_norm, m_ffn1_w_gu, m_ffn1_w_down, m_mix_norm, m_w_in, m_a_conv_w, m_a_log_rate, m_a_dt_bias, m_a_out_norm, m_b_shift_mu, m_b_w0, m_b_w_up, m_b_a0, m_b_a_up, m_b_g_up, m_b_k_k, m_b_k_a, m_b_r_k, m_b_ln_gain, m_b_ln_bias, m_w_out, m_ffn2_norm, m_ffn2_w_gu, m_ffn2_w_down, m_final_norm, v_meta_tokens, v_ffn1_norm, v_ffn1_w_gu, v_ffn1_w_down, v_mix_norm, v_w_in, v_a_conv_w, v_a_log_rate, v_a_dt_bias, v_a_out_norm, v_b_shift_mu, v_b_w0, v_b_w_up, v_b_a0, v_b_a_up, v_b_g_up, v_b_k_k, v_b_k_a, v_b_r_k, v_b_ln_gain, v_b_ln_bias, v_w_out, v_ffn2_norm, v_ffn2_w_gu, v_ffn2_w_down, v_final_norm):
    names = ['meta_tokens', 'ffn1_norm', 'ffn1_w_gu', 'ffn1_w_down', 'mix_norm', 'w_in', 'a_conv_w', 'a_log_rate',
             'a_dt_bias', 'a_out_norm', 'b_shift_mu', 'b_w0', 'b_w_up', 'b_a0', 'b_a_up', 'b_g_up', 'b_k_k', 'b_k_a',
             'b_r_k', 'b_ln_gain', 'b_ln_bias', 'w_out', 'ffn2_norm', 'ffn2_w_gu', 'ffn2_w_down', 'final_norm']
    env = dict(locals())
    wts = {k: env[k] for k in names}
    mom_m = {k: env['m_' + k] for k in names}
    mom_v = {k: env['v_' + k] for k in names}
    big = ['ffn1_w_gu', 'ffn1_w_down', 'w_in', 'w_out', 'ffn2_w_gu', 'ffn2_w_down']
    col_sharded = {'ffn1_w_gu', 'w_in', 'ffn2_w_gu'}
    small_sharded = ['meta_tokens', 'a_conv_w', 'b_w_up', 'b_a_up', 'b_g_up']
    replicated = [k for k in names if k not in big and k not in small_sharded]

    seq, d = x.shape[1], x.shape[2]
    rows = PAD + N_META + seq
    heads_a = d // HEAD_A
    tb_mm = _tb(rows, 416)
    tb_vjp = _tb(rows, 208)
    me = _index(_place())

    big_local = [wts[k][0].astype(BF16) for k in big]
    small_local = [wts['meta_tokens']] + [wts[k][0] for k in small_sharded[1:]]
    gathered = _all_gather("gather_weights", big_local + small_local)
    gw = dict(zip(big + small_sharded, gathered))
    full = {k: (gw[k] if k in ('ffn1_w_gu', 'ffn2_w_gu') else _cols_from_slabs(gw[k]) if k in col_sharded
                else gw[k].reshape(-1, gw[k].shape[-1])) for k in big}
    for k in small_sharded:
        full[k] = _cols_from_slabs(gw[k])
    for k in replicated:
        full[k] = wts[k].reshape(1, -1)

    win = full['w_in']
    n_b = 3 * d + LORA_W + LORA_A + LORA_G
    off_beta, off_b = 4 * d, 4 * d + 2 * heads_a
    off_ga = off_b + n_b
    b_width = 3 * d + LORA_PAD
    zcols = lambda r, c: jnp.zeros((r, c), BF16)
    w_qkv = win[:, :3 * d]
    w_zg = jnp.concatenate([win[:, 3 * d:4 * d], win[:, off_ga:off_ga + 2 * d]], axis=1)
    w_b = jnp.concatenate([win[:, off_b:off_b + n_b], zcols(d, b_width - n_b)], axis=1)
    w_bg = jnp.concatenate([win[:, off_beta:off_beta + 2 * heads_a], zcols(d, 128 - 2 * heads_a)], axis=1)

    def lanes(vec, start, width):
        return jnp.pad(vec.reshape(1, -1), ((0, 0), (start, width - start - vec.size)))

    log_rate = lanes(wts['a_log_rate'], heads_a, 128)
    dt_bias = lanes(wts['a_dt_bias'], heads_a, 128)
    mu = lanes(wts['b_shift_mu'], 0, b_width)
    w_up = jnp.pad(full['b_w_up'], ((0, 128 - LORA_W), (0, 0)))
    a_up = jnp.pad(full['b_a_up'], ((LORA_W, 0), (0, 0)))
    g_up = jnp.pad(full['b_g_up'], ((0, 256 - LORA_G), (0, 0)))
    b_pars = [full['b_w0'], w_up, full['b_a0'], a_up, g_up, full['b_k_k'], full['b_k_a']]
    post_pars = [full['a_out_norm'], full['b_r_k'], full['b_ln_gain'], full['b_ln_bias']]
    bg_fn = _make_bg_fn(heads_a)

    h0 = jnp.concatenate([jnp.zeros((PAD, d), F32), full['meta_tokens'], x[0]], axis=0)
    h1 = _ffn_fwd("ffn1_fwd", h0, full['ffn1_norm'], full['ffn1_w_gu'], full['ffn1_w_down'], tb_mm)
    (u,) = _tok_fwd("mix_norm_fwd", _norm_fn, [h1], [full['mix_norm']], [(d, BF16)], tb_mm)
    z_qkv = _mm("in_qkv", u, w_qkv, tb=tb_mm, tn=_col_tile(3 * d, 1536))
    z_zg = _mm("in_zg", u, w_zg, tb=tb_mm, tn=_col_tile(3 * d, 1536))
    z_b = _mm("in_b", u, w_b, tb=tb_mm, tn=_col_tile(b_width, 1536))
    z_bg = _mm("in_bg", u, w_bg, tb=tb_mm, tn=128)
    qkv = _a_pre_fwd(z_qkv, full['a_conv_w'])
    (bg,) = _tok_fwd("bg_fwd", bg_fn, [z_bg], [log_rate, dt_bias], [(128, F32)], tb_mm)
    o_dn, dn_hist = _dn_fwd(qkv, bg)
    zf = _shift_fwd(z_b, mu)
    rr, ww, kk2, vv, av, bv, gate = _tok_fwd("b_pre_fwd", _b_pre_fn, [zf], b_pars, [(d, F32)] * 7, tb_vjp)
    y_b, b_hist = _rwkv_fwd(rr, ww, kk2, vv, av, bv)
    post_toks = [o_dn, z_zg, y_b, rr, kk2, vv, gate]
    (merged,) = _tok_fwd("post_fwd", _post_fn, post_toks, post_pars, [(d, BF16)], tb_vjp)
    h2 = _mm("out_proj", merged, full['w_out'], add=h1, tb=tb_mm, tn=d)
    h3 = _ffn_fwd("ffn2_fwd", h2, full['ffn2_norm'], full['ffn2_w_gu'], full['ffn2_w_down'], tb_mm)

    target = jnp.pad(loss_target[0], ((CHUNK, 0), (0, 0)))
    dh3, g_final, loss_part = _loss(h3, target, full['final_norm'].reshape(1, d), tb_vjp)

    def ffn_backward(tag, h, dout, key_norm, key_gu, key_down):
        dh, dh_bf, dgain, xn, act, dgate, dup, dhalf = _ffn_bwd(tag + "_bwd", h, full[key_norm], dout, full[key_gu],
                                                                full[key_down], tb_mm)
        d_gu = jnp.concatenate([_mm_tn_to_slabs(tag + "_dw_gate", xn, dgate, tk=tb_mm),
                                _mm_tn_to_slabs(tag + "_dw_up", xn, dup, tk=tb_mm)], axis=0)
        d_down = _mm_tn_from_slabs(tag + "_dw_down", act, dhalf, tk=tb_mm).reshape(N_DEV, -1, d)
        return dh, dh_bf, dgain, d_gu, d_down

    dh2, dh2_bf, g_ffn2_norm, g_ffn2_gu, g_ffn2_down = ffn_backward("ffn2", h2, dh3, 'ffn2_norm', 'ffn2_w_gu',
                                                                    'ffn2_w_down')
    g_w_out = _mm_tn("dw_out", merged, dh2_bf, tm=d, tn=d, tk=tb_mm).reshape(N_DEV, -1, d)
    dmerged = _mm("d_merged", dh2_bf, full['w_out'], trans_b=True, tb=tb_mm, tn=d)
    post_grads = _tok_bwd("post_bwd", _post_fn, post_toks, post_pars, [[dmerged]], list(range(7)), tb_vjp,
                          [F32, BF16] + [F32] * 5)
    do_dn, dz_zg, dy_b, dr1, dk1, dv1, dgate = post_grads[:7]
    g_out_norm, g_r_k, g_ln_g, g_ln_b = post_grads[7:]
    dr2, dw2, dk2, dv2, da2, db2 = _rwkv_bwd(rr, ww, kk2, vv, av, bv, b_hist, dy_b)
    b_grads = _tok_bwd("b_pre_bwd", _b_pre_fn, [zf], b_pars,
                       [[dr1, dr2], [dw2], [dk1, dk2], [dv1, dv2], [da2], [db2], [dgate]], [0], tb_vjp)
    dzf = b_grads[0]
    g_w0, g_w_up, g_a0, g_a_up, g_g_up, g_k_k, g_k_a = b_grads[1:]
    dz_b, g_mu = _shift_bwd(z_b, mu, dzf)
    dqkv, dbg = _dn_bwd(qkv, bg, dn_hist, do_dn)
    dz_qkv, g_conv = _a_pre_bwd(z_qkv, full['a_conv_w'], dqkv)
    dz_bg, g_log_rate, g_dt_bias = _tok_bwd("bg_bwd", bg_fn, [z_bg], [log_rate, dt_bias], [[dbg]], [0], tb_mm, [BF16])

    du = None
    g_w_in_parts = []
    for tag, dz, wpiece in (("qkv", dz_qkv, w_qkv), ("zg", dz_zg, w_zg), ("b", dz_b, w_b), ("bg", dz_bg, w_bg)):
        du = _mm("du_" + tag, dz, wpiece, trans_b=True, add=du, tb=tb_mm, tn=d)
        g_w_in_parts.append(_mm_tn("dw_in_" + tag, u, dz, tm=d, tn=_col_tile(dz.shape[1], 1536), tk=tb_mm))
    dh1, g_mix_norm = _tok_bwd("mix_norm_bwd", _norm_res_fn, [h1], [full['mix_norm']], [[du], [dh2]], [0], tb_vjp)
    dh0, _, g_ffn1_norm, g_ffn1_gu, g_ffn1_down = ffn_backward("ffn1", h0, dh1, 'ffn1_norm', 'ffn1_w_gu', 'ffn1_w_down')

    gp_qkv, gp_zg, gp_b, gp_bg = g_w_in_parts
    g_w_in = _cols_to_slabs(jnp.concatenate(
        [gp_qkv, gp_zg[:, :d], gp_bg[:, :2 * heads_a], gp_b[:, :n_b], gp_zg[:, d:]], axis=1))

    slabs = {'ffn1_w_gu': g_ffn1_gu, 'ffn1_w_down': g_ffn1_down, 'w_in': g_w_in, 'w_out': g_w_out,
             'ffn2_w_gu': g_ffn2_gu, 'ffn2_w_down': g_ffn2_down}
    received = dict(zip(big, _exchange("exchange_grads", [slabs[k] for k in big])))

    small_full = {
        'meta_tokens': dh0[PAD:CHUNK], 'ffn1_norm': g_ffn1_norm, 'mix_norm': g_mix_norm, 'a_conv_w': g_conv,
        'a_log_rate': g_log_rate[:, heads_a:2 * heads_a], 'a_dt_bias': g_dt_bias[:, heads_a:2 * heads_a],
        'a_out_norm': g_out_norm, 'b_shift_mu': g_mu[:, :n_b], 'b_w0': g_w0, 'b_w_up': g_w_up[:LORA_W],
        'b_a0': g_a0, 'b_a_up': g_a_up[LORA_W:], 'b_g_up': g_g_up[:LORA_G], 'b_k_k': g_k_k, 'b_k_a': g_k_a,
        'b_r_k': g_r_k, 'b_ln_gain': g_ln_g, 'b_ln_bias': g_ln_b, 'ffn2_norm': g_ffn2_norm, 'final_norm': g_final,
    }
    small_names = list(small_full)
    packed = _pack([small_full[k] for k in small_names] + [loss_part[:, :1]])
    (all_parts,) = _all_gather("gather_small_grads", [packed])
    summed = _sum_slabs("sum_small_grads", all_parts, packed.shape[0])
    pieces = _unpack(summed, [small_full[k].shape for k in small_names] + [(1, 1)])
    small_grad = dict(zip(small_names, pieces[:-1]))
    loss = pieces[-1].reshape(())

    grads, deltas, new_m, new_v = {}, {}, {}, {}
    for k in big:
        shard = wts[k][0]
        rb = _tb(shard.shape[0], 128) if shard.shape[0] % 16 == 0 else shard.shape[0]
        g, dl, m2, v2 = _adamw("adamw_" + k, received[k], shard, mom_m[k][0], mom_v[k][0], rb)
        grads[k], deltas[k], new_m[k], new_v[k] = g[None], dl[None], m2[None], v2[None]

    local_small = {}
    for k in small_names:
        g = small_grad[k]
        if k in small_sharded:
            width = wts[k].shape[-1]
            g = lax.dynamic_slice_in_dim(g, me * width, width, axis=1)
        local_small[k] = g.reshape(wts[k].shape)
    pk = lambda tree: _pack([tree[k] for k in small_names])
    dl_s, m_s, v_s = _adamw_small(pk(wts), pk(local_small), pk(mom_m), pk(mom_v))
    shapes = [wts[k].shape for k in small_names]
    for k, dl, m2, v2 in zip(small_names, _unpack(dl_s, shapes), _unpack(m_s, shapes), _unpack(v_s, shapes)):
        grads[k], deltas[k], new_m[k], new_v[k] = local_small[k], dl, m2, v2

    grad_x = dh0[CHUNK:][None]
    return (loss, grad_x, *[grads[k] for k in names], *[deltas[k] for k in names],
            *[new_m[k] for k in names], *[new_v[k] for k in names])
```

```python
import functools

import jax
import jax.numpy as jnp
from jax import lax
from jax.experimental import pallas as pl
from jax.experimental.pallas import tpu as pltpu

F32 = jnp.float32
BF16 = jnp.bfloat16
N_DEV = 8
N_META = 16
CHUNK = 64
PAD = CHUNK - N_META
HEAD_A = 128
HEAD_B = 64
LORA_W, LORA_A, LORA_G = 64, 64, 160
LORA_PAD = 384
EPS = 1e-6
GN_EPS = HEAD_B * 1e-5
ADAM_LR, ADAM_B1, ADAM_B2, ADAM_EPS, ADAM_WD, ADAM_STEP = 0.001, 0.9, 0.999, 1e-08, 0.01, 10
SCAN_STEPS = 16
MXU_WIDTH = 256
VMEM_LIMIT = 56 * 1024 * 1024
DN_PRECISION = lax.Precision.HIGH
MESH_ID = pl.DeviceIdType.MESH
ANY = pl.BlockSpec(memory_space=pl.ANY)


def _cp(*sem):
    return pltpu.CompilerParams(dimension_semantics=sem, vmem_limit_bytes=VMEM_LIMIT)


def _tb(t, target):
    best = 16
    for d in range(16, target + 1, 16):
        if t % d == 0:
            best = d
    return best


def _sigmoid(x):
    return 1.0 / (1.0 + jnp.exp(-x))


def _silu(x):
    return x * _sigmoid(x)


def _softplus(x):
    return jnp.maximum(x, 0.0) + jnp.log(1.0 + jnp.exp(-jnp.abs(x)))


def _dot_nt(a, b, precision=None):
    return lax.dot_general(a, b, (((1,), (1,)), ((), ())), preferred_element_type=F32, precision=precision)


def _dot_tn(a, b, precision=None):
    return lax.dot_general(a, b, (((0,), (0,)), ((), ())), preferred_element_type=F32, precision=precision)


def _dot(a, b, precision=None):
    return jnp.dot(a, b, preferred_element_type=F32, precision=precision)


def _block_diag_ones():
    i = lax.broadcasted_iota(jnp.int32, (MXU_WIDTH, MXU_WIDTH), 0) // HEAD_B
    j = lax.broadcasted_iota(jnp.int32, (MXU_WIDTH, MXU_WIDTH), 1) // HEAD_B
    return (i == j).astype(BF16)


def _hi_lo(x):
    hi = x.astype(BF16)
    return hi, (x - hi.astype(F32)).astype(BF16)


def _segsum_many(xs, bd):
    groups = [x if isinstance(x, tuple) else _hi_lo(x) for x in xs]
    rows = groups[0][0].shape[0]
    stacked = jnp.concatenate([p for grp in groups for p in grp], axis=0)
    out = jnp.concatenate([_dot(stacked[:, s:s + MXU_WIDTH], bd) for s in range(0, stacked.shape[1], MXU_WIDTH)], axis=1)
    res, pos = [], 0
    for grp in groups:
        acc = out[pos:pos + rows]
        for j in range(1, len(grp)):
            acc = acc + out[pos + j * rows:pos + (j + 1) * rows]
        res.append(acc)
        pos += len(grp) * rows
    return res


def _segsum_impl(x):
    return _segsum_many([x], _block_diag_ones())[0]


@jax.custom_vjp
def _segsum64(x):
    return _segsum_impl(x)


_segsum64.defvjp(lambda x: (_segsum_impl(x), None), lambda _, ct: (_segsum_impl(ct),))


def _tok(t):
    return t if isinstance(t, tuple) else (t, t.shape[1], 0)


def _tok_spec(tb, width, colblk):
    return pl.BlockSpec((tb, width), lambda i: (i, colblk))


def _par_spec(p):
    return pl.BlockSpec(p.shape, lambda i: (0, 0))


def _tok_fwd(name, fn, toks, pars, outs, tb):
    toks = [_tok(t) for t in toks]
    rows = toks[0][0].shape[0]
    n_in = len(toks) + len(pars)

    def body(*refs):
        row0 = pl.program_id(0) * tb
        res = fn(row0, *[r[...] for r in refs[:n_in]])
        for r, o in zip(refs[n_in:], res):
            r[...] = o.astype(r.dtype)

    return pl.pallas_call(
        body, name=name, grid=(rows // tb,),
        in_specs=[_tok_spec(tb, w, c) for _, w, c in toks] + [_par_spec(p) for p in pars],
        out_specs=[_tok_spec(tb, w, 0) for w, _ in outs],
        out_shape=[jax.ShapeDtypeStruct((rows, w), dt) for w, dt in outs],
        compiler_params=_cp("parallel"),
    )(*[a for a, _, _ in toks], *pars)


def _tok_bwd(name, fn, toks, pars, cts, want, tb, want_dtypes=None):
    toks = [_tok(t) for t in toks]
    want_dtypes = want_dtypes or [F32] * len(want)
    cts = [[_tok(c) for c in group] for group in cts]
    flat_cts = [c for group in cts for c in group]
    rows = toks[0][0].shape[0]
    n_tok, n_par, n_ct = len(toks), len(pars), len(flat_cts)

    def body(*refs):
        i = pl.program_id(0)
        row0 = i * tb
        prim = [r[...].astype(F32) for r in refs[:n_tok + n_par]]
        ct_refs = list(refs[n_tok + n_par:n_tok + n_par + n_ct])
        out_refs = refs[n_tok + n_par + n_ct:]
        res, vjp = jax.vjp(lambda *a: fn(row0, *a), *prim)
        ct = []
        for group, o in zip(cts, res):
            acc = None
            for _ in group:
                v = ct_refs.pop(0)[...].astype(F32)
                acc = v if acc is None else acc + v
            ct.append(acc.astype(o.dtype))
        grads = vjp(tuple(ct))
        for r, k in zip(out_refs[:len(want)], want):
            r[...] = grads[k].astype(r.dtype)

        @pl.when(i == 0)
        def _():
            for r in out_refs[len(want):]:
                r[...] = jnp.zeros_like(r)

        for r, g in zip(out_refs[len(want):], grads[n_tok:]):
            r[...] += g

    return pl.pallas_call(
        body, name=name, grid=(rows // tb,),
        in_specs=[_tok_spec(tb, w, c) for _, w, c in toks] + [_par_spec(p) for p in pars]
        + [_tok_spec(tb, w, c) for _, w, c in flat_cts],
        out_specs=[_tok_spec(tb, toks[k][1], 0) for k in want] + [_par_spec(p) for p in pars],
        out_shape=[jax.ShapeDtypeStruct((rows, toks[k][1]), dt) for k, dt in zip(want, want_dtypes)]
        + [jax.ShapeDtypeStruct(p.shape, F32) for p in pars],
        compiler_params=_cp("arbitrary"),
    )(*[a for a, _, _ in toks], *pars, *[a for a, _, _ in flat_cts])


def _mm(name, a, b, *, trans_b=False, add=None, tb, tn):
    rows, k = a.shape
    n = b.shape[0] if trans_b else b.shape[1]

    def body(*refs):
        a_ref, b_ref = refs[:2]
        o_ref = refs[-1]
        acc = _dot_nt(a_ref[...], b_ref[...]) if trans_b else _dot(a_ref[...], b_ref[...])
        if add is not None:
            acc = acc + refs[2][...]
        o_ref[...] = acc

    in_specs = [pl.BlockSpec((tb, k), lambda i, j: (i, 0)),
                pl.BlockSpec((tn, k), lambda i, j: (j, 0)) if trans_b else pl.BlockSpec((k, tn), lambda i, j: (0, j))]
    args = [a, b]
    if add is not None:
        in_specs.append(pl.BlockSpec((tb, tn), lambda i, j: (i, j)))
        args.append(add)
    return pl.pallas_call(
        body, name=name, grid=(rows // tb, n // tn), in_specs=in_specs,
        out_specs=pl.BlockSpec((tb, tn), lambda i, j: (i, j)),
        out_shape=jax.ShapeDtypeStruct((rows, n), F32),
        compiler_params=_cp("parallel", "parallel"),
    )(*args)


def _mm_tn_call(name, grid, a, b, a_spec, b_spec, o_spec, acc_shape, out_shape):
    last = len(grid) - 1

    def body(a_ref, b_ref, o_ref, acc_ref):
        k = pl.program_id(last)

        @pl.when(k == 0)
        def _():
            acc_ref[...] = jnp.zeros_like(acc_ref)

        a_blk = a_ref[0] if len(a_ref.shape) == 3 else a_ref[...]
        b_blk = b_ref[0] if len(b_ref.shape) == 3 else b_ref[...]
        acc_ref[...] += _dot_tn(a_blk, b_blk)

        @pl.when(k == grid[last] - 1)
        def _():
            if len(o_ref.shape) == 3:
                o_ref[0] = acc_ref[...].astype(o_ref.dtype)
            else:
                o_ref[...] = acc_ref[...].astype(o_ref.dtype)

    return pl.pallas_call(
        body, name=name, grid=grid, in_specs=[a_spec, b_spec], out_specs=o_spec,
        out_shape=jax.ShapeDtypeStruct(out_shape, BF16), scratch_shapes=[pltpu.VMEM(acc_shape, F32)],
        compiler_params=_cp(*(["parallel"] * last + ["arbitrary"])),
    )(a, b)


def _mm_tn(name, a, b, *, tm, tn, tk):
    rows, m = a.shape
    n = b.shape[1]
    return _mm_tn_call(name, (m // tm, n // tn, rows // tk), a, b,
                       pl.BlockSpec((tk, tm), lambda i, j, k: (k, i)), pl.BlockSpec((tk, tn), lambda i, j, k: (k, j)),
                       pl.BlockSpec((tm, tn), lambda i, j, k: (i, j)), (tm, tn), (m, n))


def _mm_tn_to_slabs(name, a, b3, *, tk):
    rows, m = a.shape
    s, _, c = b3.shape
    return _mm_tn_call(name, (s, rows // tk), a, b3,
                       pl.BlockSpec((tk, m), lambda i, k: (k, 0)), pl.BlockSpec((1, tk, c), lambda i, k: (i, k, 0)),
                       pl.BlockSpec((1, m, c), lambda i, k: (i, 0, 0)), (m, c), (s, m, c))


def _mm_tn_from_slabs(name, a3, b, *, tk):
    s, rows, c = a3.shape
    n = b.shape[1]
    return _mm_tn_call(name, (s, rows // tk), a3, b,
                       pl.BlockSpec((1, tk, c), lambda i, k: (i, k, 0)), pl.BlockSpec((tk, n), lambda i, k: (k, 0)),
                       pl.BlockSpec((c, n), lambda i, k: (i, 0)), (c, n), (s * c, n))


def _col_tile(n, target):
    if n <= target:
        return n
    best = 128
    for d in range(128, target + 1, 128):
        if n % d == 0:
            best = d
    return best


def _rms(x, gain):
    return x * lax.rsqrt(jnp.mean(x * x, axis=-1, keepdims=True) + EPS) * gain


def _ffn_specs(d, fc, nj):
    return [pl.BlockSpec((1, d, fc), lambda i, j: (j, 0, 0)), pl.BlockSpec((1, d, fc), lambda i, j: (nj + j, 0, 0)),
            pl.BlockSpec((fc, d), lambda i, j: (j, 0))]


def _ffn_fwd(name, h, gain, wgu, wd, tb):
    rows, d = h.shape
    nj = wgu.shape[0] // 2
    fc = wgu.shape[2]

    def body(h_ref, g_ref, wg_ref, wu_ref, wd_ref, o_ref, xn_s, acc_s):
        j = pl.program_id(1)

        @pl.when(j == 0)
        def _():
            xn_s[...] = _rms(h_ref[...], g_ref[...]).astype(BF16)
            acc_s[...] = jnp.zeros_like(acc_s)

        xn = xn_s[...]
        gate = _dot(xn, wg_ref[0])
        up = _dot(xn, wu_ref[0])
        acc_s[...] += _dot((_silu(gate) * up).astype(BF16), wd_ref[...])

        @pl.when(j == nj - 1)
        def _():
            o_ref[...] = h_ref[...] + 0.5 * acc_s[...]

    return pl.pallas_call(
        body, name=name, grid=(rows // tb, nj),
        in_specs=[pl.BlockSpec((tb, d), lambda i, j: (i, 0)), pl.BlockSpec((1, d), lambda i, j: (0, 0))]
        + _ffn_specs(d, fc, nj),
        out_specs=pl.BlockSpec((tb, d), lambda i, j: (i, 0)),
        out_shape=jax.ShapeDtypeStruct((rows, d), F32),
        scratch_shapes=[pltpu.VMEM((tb, d), BF16), pltpu.VMEM((tb, d), F32)],
        compiler_params=_cp("parallel", "arbitrary"),
    )(h, gain, wgu, wgu, wd)


def _ffn_bwd(name, h, gain, dout, wgu, wd, tb):
    rows, d = h.shape
    nj = wgu.shape[0] // 2
    fc = wgu.shape[2]

    def body(h_ref, g_ref, do_ref, wg_ref, wu_ref, wd_ref,
             dh_ref, dhb_ref, dg_ref, xn_ref, act_ref, dgate_ref, dup_ref, dhalf_ref, dxn_s):
        i, j = pl.program_id(0), pl.program_id(1)

        @pl.when(j == 0)
        def _():
            xn_ref[...] = _rms(h_ref[...], g_ref[...]).astype(BF16)
            dhalf_ref[...] = (0.5 * do_ref[...]).astype(BF16)
            dxn_s[...] = jnp.zeros_like(dxn_s)

        xn = xn_ref[...]
        wg, wu = wg_ref[0], wu_ref[0]
        gate = _dot(xn, wg)
        up = _dot(xn, wu)
        sg = _sigmoid(gate)
        dact = _dot_nt(dhalf_ref[...], wd_ref[...])
        act_ref[0] = (gate * sg * up).astype(BF16)
        dgate = (dact * up * (sg * (1.0 + gate * (1.0 - sg)))).astype(BF16)
        dup = (dact * gate * sg).astype(BF16)
        dgate_ref[0] = dgate
        dup_ref[0] = dup
        dxn_s[...] += _dot_nt(dgate, wg) + _dot_nt(dup, wu)

        @pl.when((i == 0) & (j == 0))
        def _():
            dg_ref[...] = jnp.zeros_like(dg_ref)

        @pl.when(j == nj - 1)
        def _():
            x = h_ref[...]
            r = lax.rsqrt(jnp.mean(x * x, axis=-1, keepdims=True) + EPS)
            dxn = dxn_s[...]
            dyg = dxn * g_ref[...]
            dh = do_ref[...] + r * dyg - x * (r * r * r) * jnp.mean(dyg * x, axis=-1, keepdims=True)
            dh_ref[...] = dh
            dhb_ref[...] = dh.astype(BF16)
            dg_ref[...] += jnp.sum(dxn * x * r, axis=0, keepdims=True)

    row_d = pl.BlockSpec((tb, d), lambda i, j: (i, 0))
    slab = pl.BlockSpec((1, tb, fc), lambda i, j: (j, i, 0))
    hidden = jax.ShapeDtypeStruct((nj, rows, fc), BF16)
    return pl.pallas_call(
        body, name=name, grid=(rows // tb, nj),
        in_specs=[row_d, pl.BlockSpec((1, d), lambda i, j: (0, 0)), row_d] + _ffn_specs(d, fc, nj),
        out_specs=[row_d, row_d, pl.BlockSpec((1, d), lambda i, j: (0, 0)), row_d, slab, slab, slab, row_d],
        out_shape=[jax.ShapeDtypeStruct((rows, d), F32), jax.ShapeDtypeStruct((rows, d), BF16),
                   jax.ShapeDtypeStruct((1, d), F32), jax.ShapeDtypeStruct((rows, d), BF16),
                   hidden, hidden, hidden, jax.ShapeDtypeStruct((rows, d), BF16)],
        scratch_shapes=[pltpu.VMEM((tb, d), F32)],
        compiler_params=_cp("arbitrary", "arbitrary"),
    )(h, gain, dout, wgu, wgu, wd)


def _shift_rows(x, s):
    return pltpu.roll(x, s % x.shape[0], 0)


def _a_post(c, which):
    s = _silu(c)
    n = s * lax.rsqrt(jnp.sum(s * s, axis=-1, keepdims=True) + 1e-6)
    scale = jnp.where(which == 0, HEAD_A ** -0.5, 1.0)
    return jnp.where(which == 2, s, n * scale)


def _conv(x, w):
    return x * w[3:4] + _shift_rows(x, 1) * w[2:3] + _shift_rows(x, 2) * w[1:2] + _shift_rows(x, 3) * w[0:1]


def _a_pre_fwd(zqkv, conv_w):
    rows, width = zqkv.shape
    heads = width // (3 * HEAD_A)

    def body(x_ref, w_ref, o_ref):
        which = pl.program_id(0) // heads
        live = lax.broadcasted_iota(jnp.int32, (rows, HEAD_A), 0) >= PAD
        o_ref[...] = jnp.where(live, _a_post(_conv(x_ref[...], w_ref[...]), which), 0.0)

    return pl.pallas_call(
        body, name="a_pre_fwd", grid=(width // HEAD_A,),
        in_specs=[pl.BlockSpec((rows, HEAD_A), lambda c: (0, c)), pl.BlockSpec((4, HEAD_A), lambda c: (0, c))],
        out_specs=pl.BlockSpec((rows, HEAD_A), lambda c: (0, c)),
        out_shape=jax.ShapeDtypeStruct((rows, width), F32),
        compiler_params=_cp("parallel"),
    )(zqkv, conv_w)


def _a_pre_bwd(zqkv, conv_w, dqkv):
    rows, width = zqkv.shape
    heads = width // (3 * HEAD_A)

    def body(x_ref, w_ref, ct_ref, dx_ref, dw_ref):
        which = pl.program_id(0) // heads
        live = lax.broadcasted_iota(jnp.int32, (rows, HEAD_A), 0) >= PAD
        x, w = x_ref[...], w_ref[...]
        _, vjp = jax.vjp(lambda c: _a_post(c, which), _conv(x, w))
        (dc,) = vjp(jnp.where(live, ct_ref[...], 0.0))
        dc = jnp.where(live, dc, 0.0)
        dx_ref[...] = (dc * w[3:4] + _shift_rows(dc, -1) * w[2:3] + _shift_rows(dc, -2) * w[1:2]
                       + _shift_rows(dc, -3) * w[0:1]).astype(BF16)
        dw_ref[...] = jnp.concatenate(
            [jnp.sum(dc * (_shift_rows(x, 3 - j) if j < 3 else x), axis=0, keepdims=True) for j in range(4)], axis=0)

    col = pl.BlockSpec((rows, HEAD_A), lambda c: (0, c))
    wsp = pl.BlockSpec((4, HEAD_A), lambda c: (0, c))
    return pl.pallas_call(
        body, name="a_pre_bwd", grid=(width // HEAD_A,),
        in_specs=[col, wsp, col], out_specs=[col, wsp],
        out_shape=[jax.ShapeDtypeStruct((rows, width), BF16), jax.ShapeDtypeStruct((4, width), F32)],
        compiler_params=_cp("parallel"),
    )(zqkv, conv_w, dqkv)


SHIFT_TILE = 384


def _shift_fwd(zb, mu):
    rows, width = zb.shape

    def body(x_ref, mu_ref, o_ref):
        x = x_ref[...]
        first = lax.broadcasted_iota(jnp.int32, x.shape, 0) == 0
        prev = jnp.where(first, 0.0, _shift_rows(x, 1))
        o_ref[...] = x + (prev - x) * mu_ref[...]

    col = pl.BlockSpec((rows, SHIFT_TILE), lambda c: (0, c))
    return pl.pallas_call(
        body, name="shift_fwd", grid=(width // SHIFT_TILE,),
        in_specs=[col, pl.BlockSpec((1, SHIFT_TILE), lambda c: (0, c))], out_specs=col,
        out_shape=jax.ShapeDtypeStruct((rows, width), F32), compiler_params=_cp("parallel"),
    )(zb, mu)


def _shift_bwd(zb, mu, dzf):
    rows, width = zb.shape

    def body(x_ref, mu_ref, ct_ref, dx_ref, dmu_ref):
        x, ct, mu_v = x_ref[...], ct_ref[...], mu_ref[...]
        row = lax.broadcasted_iota(jnp.int32, x.shape, 0)
        prev = jnp.where(row == 0, 0.0, _shift_rows(x, 1))
        nxt = jnp.where(row == rows - 1, 0.0, _shift_rows(ct, -1))
        dx_ref[...] = (ct * (1.0 - mu_v) + nxt * mu_v).astype(BF16)
        dmu_ref[...] = jnp.sum(ct * (prev - x), axis=0, keepdims=True)

    col = pl.BlockSpec((rows, SHIFT_TILE), lambda c: (0, c))
    msp = pl.BlockSpec((1, SHIFT_TILE), lambda c: (0, c))
    return pl.pallas_call(
        body, name="shift_bwd", grid=(width // SHIFT_TILE,),
        in_specs=[col, msp, col], out_specs=[col, msp],
        out_shape=[jax.ShapeDtypeStruct((rows, width), BF16), jax.ShapeDtypeStruct((1, width), F32)],
        compiler_params=_cp("parallel"),
    )(zb, mu, dzf)


def _dn_chunk(q, k, v, beta, g, state):
    heads = range(len(q))
    ri = lax.broadcasted_iota(jnp.int32, (CHUNK, CHUNK), 0)
    ci = lax.broadcasted_iota(jnp.int32, (CHUNK, CHUNK), 1)
    eye = (ri == ci).astype(F32)
    incl = ri >= ci
    last = lax.broadcasted_iota(jnp.int32, (CHUNK, 1), 0) == CHUNK - 1
    g_row = [jnp.sum(g[h] * eye, axis=0, keepdims=True) for h in heads]
    gc = [jnp.sum(jnp.where(incl, g_row[h], 0.0), axis=1, keepdims=True) for h in heads]
    gc_row = [jnp.sum(gc[h] * eye, axis=0, keepdims=True) for h in heads]
    decay = [jnp.where(incl, jnp.exp(jnp.where(incl, gc[h] - gc_row[h], 0.0)), 0.0) for h in heads]
    kb = [k[h] * beta[h] for h in heads]
    vb = [v[h] * beta[h] for h in heads]
    p = [-jnp.where(ri > ci, _dot_nt(kb[h], k[h], DN_PRECISION) * decay[h], 0.0) for h in heads]
    tinv = [eye + p[h] for h in heads]
    for _ in range(5):
        p = [_dot(p[h], p[h], DN_PRECISION) for h in heads]
        tinv = [tinv[h] + _dot(tinv[h], p[h], DN_PRECISION) for h in heads]
    eg = [jnp.exp(gc[h]) for h in heads]
    u = [_dot(tinv[h], vb[h], DN_PRECISION) for h in heads]
    wk = [_dot(tinv[h], kb[h] * eg[h], DN_PRECISION) for h in heads]
    attn = [_dot_nt(q[h], k[h], DN_PRECISION) * decay[h] for h in heads]
    g_last = [jnp.sum(jnp.where(last, gc[h], 0.0), axis=0, keepdims=True) for h in heads]
    k_tail = [k[h] * jnp.exp(g_last[h] - gc[h]) for h in heads]
    v_new = [u[h] - _dot(wk[h], state[h], DN_PRECISION) for h in heads]
    o = [_dot(q[h] * eg[h], state[h], DN_PRECISION) + _dot(attn[h], v_new[h], DN_PRECISION) for h in heads]
    new = [state[h] * jnp.exp(g_last[h]) + _dot_tn(k_tail[h], v_new[h], DN_PRECISION) for h in heads]
    return o, new


def _bg_cols(bg, h, heads):
    lane = lax.broadcasted_iota(jnp.int32, bg.shape, 1)
    beta = jnp.sum(jnp.where(lane == h, bg, 0.0), axis=1, keepdims=True)
    g = jnp.sum(jnp.where(lane == heads + h, bg, 0.0), axis=1, keepdims=True)
    return beta, g


def _dn_fwd(qkv, bg):
    rows = qkv.shape[0]
    heads = qkv.shape[1] // (3 * HEAD_A)
    n = rows // CHUNK
    hp, groups = heads, 1

    def body(q_ref, k_ref, v_ref, bg_ref, o_ref, hist_ref, s_ref):
        c, grp = pl.program_id(0), pl.program_id(1)

        @pl.when(c == 0)
        def _():
            for i in range(hp):
                s_ref[grp * hp + i] = jnp.zeros((HEAD_A, HEAD_A), F32)

        bg_v = bg_ref[...]
        cols = [slice(i * HEAD_A, (i + 1) * HEAD_A) for i in range(hp)]
        state = [s_ref[grp * hp + i] for i in range(hp)]
        beta_g = [_bg_cols(bg_v, grp * hp + i, heads) for i in range(hp)]
        o, new = _dn_chunk([q_ref[:, c_] for c_ in cols], [k_ref[:, c_] for c_ in cols], [v_ref[:, c_] for c_ in cols],
                           [b for b, _ in beta_g], [g for _, g in beta_g], state)
        for i in range(hp):
            hist_ref[0, i] = state[i]
            o_ref[:, cols[i]] = o[i]
            s_ref[grp * hp + i] = new[i]

    def part(p):
        return pl.BlockSpec((CHUNK, hp * HEAD_A), lambda c, grp: (c, p * groups + grp))

    return pl.pallas_call(
        body, name="deltanet_fwd", grid=(n, groups),
        in_specs=[part(0), part(1), part(2), pl.BlockSpec((CHUNK, 128), lambda c, grp: (c, 0))],
        out_specs=[part(0), pl.BlockSpec((1, hp, HEAD_A, HEAD_A), lambda c, grp: (c, grp, 0, 0))],
        out_shape=[jax.ShapeDtypeStruct((rows, heads * HEAD_A), F32),
                   jax.ShapeDtypeStruct((n, heads, HEAD_A, HEAD_A), F32)],
        scratch_shapes=[pltpu.VMEM((heads, HEAD_A, HEAD_A), F32)],
        compiler_params=_cp("arbitrary", "arbitrary"),
    )(qkv, qkv, qkv, bg)


def _dn_bwd(qkv, bg, hist, do):
    rows = qkv.shape[0]
    heads = qkv.shape[1] // (3 * HEAD_A)
    n = rows // CHUNK
    hp, groups = heads, 1

    def body(q_ref, k_ref, v_ref, bg_ref, hist_ref, do_ref, dqkv_ref, dbg_ref, ds_ref):
        c, grp = pl.program_id(0), pl.program_id(1)

        @pl.when(c == 0)
        def _():
            for i in range(hp):
                ds_ref[grp * hp + i] = jnp.zeros((HEAD_A, HEAD_A), F32)

        bg_v = bg_ref[...]
        lane = lax.broadcasted_iota(jnp.int32, (CHUNK, 128), 1)
        cols = [slice(i * HEAD_A, (i + 1) * HEAD_A) for i in range(hp)]
        beta_g = [_bg_cols(bg_v, grp * hp + i, heads) for i in range(hp)]
        _, vjp = jax.vjp(_dn_chunk, [q_ref[:, c_] for c_ in cols], [k_ref[:, c_] for c_ in cols],
                         [v_ref[:, c_] for c_ in cols], [b for b, _ in beta_g], [g for _, g in beta_g],
                         [hist_ref[0, i] for i in range(hp)])
        dq, dk, dv, dbeta, dg, ds = vjp(([do_ref[:, c_] for c_ in cols], [ds_ref[grp * hp + i] for i in range(hp)]))
        dbg = jnp.zeros((CHUNK, 128), F32)
        for i in range(hp):
            h = grp * hp + i
            for p, part_grad in enumerate((dq, dk, dv)):
                dqkv_ref[:, pl.ds((p * heads + i) * HEAD_A, HEAD_A)] = part_grad[i]
            ds_ref[h] = ds[i]
            dbg = dbg + jnp.where(lane == h, dbeta[i], 0.0) + jnp.where(lane == heads + h, dg[i], 0.0)

        @pl.when(grp == 0)
        def _():
            dbg_ref[...] = jnp.zeros_like(dbg_ref)

        dbg_ref[...] += dbg

    def part(p):
        return pl.BlockSpec((CHUNK, hp * HEAD_A), lambda c, grp: (n - 1 - c, p * groups + grp))

    return pl.pallas_call(
        body, name="deltanet_bwd", grid=(n, groups),
        in_specs=[part(0), part(1), part(2), pl.BlockSpec((CHUNK, 128), lambda c, grp: (n - 1 - c, 0)),
                  pl.BlockSpec((1, hp, HEAD_A, HEAD_A), lambda c, grp: (n - 1 - c, grp, 0, 0)), part(0)],
        out_specs=[pl.BlockSpec((CHUNK, 3 * heads * HEAD_A), lambda c, grp: (n - 1 - c, 0)),
                   pl.BlockSpec((CHUNK, 128), lambda c, grp: (n - 1 - c, 0))],
        out_shape=[jax.ShapeDtypeStruct(qkv.shape, F32), jax.ShapeDtypeStruct((rows, 128), F32)],
        scratch_shapes=[pltpu.VMEM((heads, HEAD_A, HEAD_A), F32)],
        compiler_params=_cp("arbitrary", "arbitrary"),
    )(qkv, qkv, qkv, bg, hist, do)


def _head_mask(heads, width):
    return (lax.broadcasted_iota(jnp.int32, (heads, width), 0)
            == lax.broadcasted_iota(jnp.int32, (heads, width), 1) // HEAD_B)


def _masked_rows(mask, row):
    return jnp.where(mask, row, 0.0).astype(BF16)


def _rwkv_fwd(r, w, k, v, a, b):
    rows, width = r.shape
    heads = width // HEAD_B
    ts = SCAN_STEPS

    def body(r_ref, w_ref, k_ref, v_ref, a_ref, b_ref, y_ref, hist_ref, s_ref):
        @pl.when(pl.program_id(0) == 0)
        def _():
            s_ref[...] = jnp.zeros_like(s_ref)

        mask = _head_mask(heads, width)
        onehot = mask.astype(BF16)
        onehot2 = jnp.concatenate([onehot, onehot], axis=0)
        bd = _block_diag_ones()

        for j in range(ts):
            row = pl.ds(j, 1)
            s = s_ref[...]
            hist_ref[j] = s
            vb = _dot_tn(jnp.concatenate(_hi_lo(v_ref[j]), axis=0), onehot2)
            (sa,) = _segsum_many([s * a_ref[row, :]], bd)
            s = s * w_ref[row, :] + sa * b_ref[row, :] + vb * k_ref[row, :]
            y_ref[j] = _dot_nt(_masked_rows(mask, r_ref[row, :]), s.astype(BF16))
            s_ref[...] = s

    blk = pl.BlockSpec((ts, width), lambda i: (i, 0))
    blk3 = pl.BlockSpec((ts, heads, HEAD_B), lambda i: (i, 0, 0))
    return pl.pallas_call(
        body, name="rwkv_fwd", grid=(rows // ts,),
        in_specs=[blk, blk, blk, blk3, blk, blk],
        out_specs=[blk3, pl.BlockSpec((ts, HEAD_B, width), lambda i: (i, 0, 0)),
                   pl.BlockSpec((HEAD_B, width), lambda i: (0, 0))],
        out_shape=[jax.ShapeDtypeStruct((rows, heads, HEAD_B), F32), jax.ShapeDtypeStruct((rows, HEAD_B, width), F32),
                   jax.ShapeDtypeStruct((HEAD_B, width), F32)],
        compiler_params=_cp("arbitrary"),
    )(r, w, k, v, a, b)


def _rwkv_bwd(r, w, k, v, a, b, hist, last, dy):
    rows, width = r.shape
    heads = width // HEAD_B
    ts = SCAN_STEPS
    nb = rows // ts

    def body(r_ref, w_ref, k_ref, v_ref, a_ref, b_ref, hist_ref, last_ref, dy_ref,
             dr_ref, dw_ref, dk_ref, dv_ref, da_ref, db_ref, g_ref, after_ref):
        @pl.when(pl.program_id(0) == 0)
        def _():
            g_ref[...] = jnp.zeros_like(g_ref)
            after_ref[...] = last_ref[...]

        mask = _head_mask(heads, width)
        onehot = mask.astype(BF16)
        bd = _block_diag_ones()

        def own_lanes(x):
            return jnp.sum(jnp.where(mask, x, 0.0), axis=0, keepdims=True)

        def colsum(x):
            return jnp.sum(x, axis=0, keepdims=True)

        for j in reversed(range(ts)):
            row = pl.ds(j, 1)
            sp = hist_ref[j]
            st = hist_ref[j + 1] if j < ts - 1 else after_ref[...]
            a_t, b_t, k_t, w_t, r_t = a_ref[row, :], b_ref[row, :], k_ref[row, :], w_ref[row, :], r_ref[row, :]
            dy_m = dy_ref[j].astype(BF16)
            g = g_ref[...] + _dot_tn(dy_m, onehot) * r_t
            g_b = g.astype(BF16)
            sa_m = _dot_nt(_masked_rows(mask, a_t), sp.astype(BF16))
            both = _dot(jnp.concatenate([v_ref[j].astype(BF16), sa_m.astype(BF16)], axis=0), g_b)
            dk_ref[row, :] = own_lanes(both[:heads])
            db_ref[row, :] = own_lanes(both[heads:])
            dr_ref[row, :] = own_lanes(_dot(dy_m, st.astype(BF16)))
            dv_ref[j] = _dot_nt(_masked_rows(mask, k_t), g_b)
            dw_ref[row, :] = colsum(g * sp)
            (dsa,) = _segsum_many([g * b_t], bd)
            da_ref[row, :] = colsum(sp * dsa)
            g_ref[...] = g * w_t + dsa * a_t

        after_ref[...] = hist_ref[0]

    blk = pl.BlockSpec((ts, width), lambda i: (nb - 1 - i, 0))
    blk3 = pl.BlockSpec((ts, heads, HEAD_B), lambda i: (nb - 1 - i, 0, 0))
    state = pl.BlockSpec((HEAD_B, width), lambda i: (0, 0))
    return pl.pallas_call(
        body, name="rwkv_bwd", grid=(nb,),
        in_specs=[blk, blk, blk, blk3, blk, blk, pl.BlockSpec((ts, HEAD_B, width), lambda i: (nb - 1 - i, 0, 0)),
                  state, blk3],
        out_specs=[blk, blk, blk, blk3, blk, blk],
        out_shape=[jax.ShapeDtypeStruct((rows, width), F32)] * 3 + [jax.ShapeDtypeStruct((rows, heads, HEAD_B), F32)]
        + [jax.ShapeDtypeStruct((rows, width), F32)] * 2,
        scratch_shapes=[pltpu.VMEM((HEAD_B, width), F32), pltpu.VMEM((HEAD_B, width), F32)],
        compiler_params=_cp("arbitrary"),
    )(r, w, k, v, a, b, hist, last, dy)


def _live(row0, shape):
    return (row0 + lax.broadcasted_iota(jnp.int32, shape, 0)) >= PAD


def _norm_fn(row0, h, gain):
    return (_rms(h, gain),)


def _norm_res_fn(row0, h, gain):
    return _rms(h, gain), h


def _make_bg_fn(heads):
    def fn(row0, x, log_rate, dt_bias):
        lane = lax.broadcasted_iota(jnp.int32, x.shape, 1)
        beta = _sigmoid(x)
        g = -jnp.exp(log_rate) * _softplus(x + dt_bias)
        out = jnp.where(lane < heads, beta, jnp.where(lane < 2 * heads, g, 0.0))
        return (jnp.where(_live(row0, x.shape), out, 0.0),)
    return fn


def _b_pre_fn(row0, zf, w0, w_up, a0, a_up, g_up, k_k, k_a):
    d = w0.shape[1]
    r, k, v = zf[:, :d], zf[:, d:2 * d], zf[:, 2 * d:3 * d]
    lo = zf[:, 3 * d:3 * d + 128]
    lg = zf[:, 3 * d + 128:3 * d + LORA_PAD]
    lane = lax.broadcasted_iota(jnp.int32, lo.shape, 1)
    lw = _dot(jnp.where(lane < LORA_W, jnp.tanh(lo), 0.0), w_up)
    la = _dot(jnp.where(lane >= LORA_W, lo, 0.0), a_up)
    lane_g = lax.broadcasted_iota(jnp.int32, lg.shape, 1)
    gate = _dot(jnp.where(lane_g < LORA_G, _sigmoid(lg), 0.0), g_up)
    decay = jnp.exp(-jnp.exp(-_softplus(-(w0 + lw)) - 0.5))
    a = _sigmoid(a0 + la)
    kx = k * k_k
    kk = kx * lax.rsqrt(_segsum64(kx * kx) + 1e-6)
    k2 = k * (1.0 + (a - 1.0) * k_a)
    return r, decay, k2, v, -kk, kk * a, gate


def _post_fn(row0, o, zg, y, r, k2, v, gate, out_gain, r_k, ln_g, ln_b):
    d = o.shape[1]
    az, ga, gb = zg[:, :d], zg[:, d:2 * d], zg[:, 2 * d:]
    heads = d // HEAD_A
    parts = []
    for h in range(heads):
        oh = o[:, h * HEAD_A:(h + 1) * HEAD_A]
        parts.append(oh * lax.rsqrt(jnp.mean(oh * oh, axis=-1, keepdims=True) + EPS) * out_gain)
    o_a = jnp.concatenate(parts, axis=1) * _silu(az)
    mean = _segsum64(y) * (1.0 / HEAD_B)
    yc = y - mean
    var = _segsum64(yc * yc) * (1.0 / HEAD_B)
    yn = yc * lax.rsqrt(var + GN_EPS) * ln_g + ln_b
    o_b = (yn + _segsum64(r * k2 * r_k) * v) * gate
    return (_sigmoid(ga) * o_a + _sigmoid(gb) * o_b,)


def _loss(h3, target, gain, tb):
    rows, d = h3.shape

    def body(h_ref, t_ref, g_ref, dh_ref, dg_ref, l_ref):
        i = pl.program_id(0)
        live = (i * tb + lax.broadcasted_iota(jnp.int32, (tb, 1), 0)) >= CHUNK
        tgt = t_ref[...]

        def f(h, g):
            err = _rms(h, g) - tgt
            return 0.5 * jnp.sum(jnp.where(live, jnp.mean(err * err, axis=-1, keepdims=True), 0.0))

        val, vjp = jax.vjp(f, h_ref[...], g_ref[...])
        dh, dg = vjp(jnp.ones((), F32))
        dh_ref[...] = dh

        @pl.when(i == 0)
        def _():
            dg_ref[...] = jnp.zeros_like(dg_ref)
            l_ref[...] = jnp.zeros_like(l_ref)

        dg_ref[...] += dg
        l_ref[...] += jnp.full((1, 128), val, F32)

    blk = pl.BlockSpec((tb, d), lambda i: (i, 0))
    return pl.pallas_call(
        body, name="loss", grid=(rows // tb,),
        in_specs=[blk, blk, pl.BlockSpec((1, d), lambda i: (0, 0))],
        out_specs=[blk, pl.BlockSpec((1, d), lambda i: (0, 0)), pl.BlockSpec((1, 128), lambda i: (0, 0))],
        out_shape=[jax.ShapeDtypeStruct((rows, d), F32), jax.ShapeDtypeStruct((1, d), F32),
                   jax.ShapeDtypeStruct((1, 128), F32)],
        compiler_params=_cp("arbitrary"),
    )(h3, target, gain)


def _adamw_math(w, g, m, v):
    m2 = ADAM_B1 * m + (1.0 - ADAM_B1) * g
    v2 = ADAM_B2 * v + (1.0 - ADAM_B2) * (g * g)
    m_hat = m2 / (1.0 - ADAM_B1 ** ADAM_STEP)
    v_hat = v2 / (1.0 - ADAM_B2 ** ADAM_STEP)
    return -ADAM_LR * (m_hat / (jnp.sqrt(v_hat) + ADAM_EPS) + ADAM_WD * w), m2, v2


def _adamw(name, slabs, w, m, v, rb):
    rows, cols = w.shape

    def body(s_ref, w_ref, m_ref, v_ref, g_ref, d_ref, m2_ref, v2_ref):
        g = s_ref[0].astype(F32)
        for dev in range(1, N_DEV):
            g = g + s_ref[dev].astype(F32)
        g_ref[...] = g
        d_ref[...], m2_ref[...], v2_ref[...] = _adamw_math(w_ref[...], g, m_ref[...], v_ref[...])

    blk = pl.BlockSpec((rb, cols), lambda i: (i, 0))
    return pl.pallas_call(
        body, name=name, grid=(rows // rb,),
        in_specs=[pl.BlockSpec((N_DEV, rb, cols), lambda i: (0, i, 0)), blk, blk, blk],
        out_specs=[blk] * 4, out_shape=[jax.ShapeDtypeStruct((rows, cols), F32)] * 4,
        compiler_params=_cp("parallel"),
    )(slabs, w, m, v)


def _sum_slabs(name, slabs, rb):
    _, rows, cols = slabs.shape

    def body(s_ref, o_ref):
        g = s_ref[0]
        for dev in range(1, N_DEV):
            g = g + s_ref[dev]
        o_ref[...] = g

    return pl.pallas_call(
        body, name=name, grid=(rows // rb,),
        in_specs=[pl.BlockSpec((N_DEV, rb, cols), lambda i: (0, i, 0))],
        out_specs=pl.BlockSpec((rb, cols), lambda i: (i, 0)),
        out_shape=jax.ShapeDtypeStruct((rows, cols), F32), compiler_params=_cp("parallel"),
    )(slabs)


def _adamw_small(w, g, m, v):
    def body(w_ref, g_ref, m_ref, v_ref, d_ref, m2_ref, v2_ref):
        d_ref[...], m2_ref[...], v2_ref[...] = _adamw_math(w_ref[...], g_ref[...], m_ref[...], v_ref[...])

    return pl.pallas_call(body, name="adamw_small", out_shape=[jax.ShapeDtypeStruct(w.shape, F32)] * 3)(w, g, m, v)


def _place():
    return lax.axis_index("x"), lax.axis_index("y"), lax.axis_index("c")


def _index(p):
    return 4 * p[0] + 2 * p[1] + p[2]


def _all_gather(name, xs):
    n = len(xs)

    def body(*refs):
        x_refs, o_refs = refs[:n], refs[n:2 * n]
        send_sems, recv_sems, local_sems = refs[2 * n:]
        x, y, c = _place()
        me, sibling = (x, y, c), (x, y, 1 - c)
        chips = [(1 - x, y), (x, 1 - y), (1 - x, 1 - y)]

        def copy(i, k, block, to, src=None):
            dst = o_refs[i].at[_index(block)]
            return pltpu.make_async_remote_copy(src_ref=dst if src is None else src, dst_ref=dst,
                                                send_sem=send_sems.at[i, k], recv_sem=recv_sems.at[i, k],
                                                device_id=to, device_id_type=MESH_ID)

        mine = [pltpu.make_async_copy(x_refs[i], o_refs[i].at[_index(me)], local_sems.at[i]) for i in range(n)]
        for cp in mine:
            cp.start()
        first = []
        for i in range(n):
            first.append(copy(i, 0, me, sibling, src=x_refs[i]))
            first += [copy(i, 1 + j, me, (*chip, c), src=x_refs[i]) for j, chip in enumerate(chips)]
        for cp in first:
            cp.start()
        passed = []
        for j, chip in enumerate(chips):
            for i in range(n):
                copy(i, 1 + j, (*chip, c), me).wait_recv()
                cp = copy(i, 4 + j, (*chip, c), sibling)
                cp.start()
                passed.append(cp)
        for i in range(n):
            copy(i, 0, sibling, me).wait_recv()
            for j, chip in enumerate(chips):
                copy(i, 4 + j, (*chip, 1 - c), me).wait_recv()
        for cp in first + passed:
            cp.wait_send()
        for cp in mine:
            cp.wait()

    return pl.pallas_call(
        body, name=name, in_specs=[ANY] * n, out_specs=[ANY] * n,
        out_shape=[jax.ShapeDtypeStruct((N_DEV,) + x.shape, x.dtype) for x in xs],
        scratch_shapes=[pltpu.SemaphoreType.DMA((n, 7)), pltpu.SemaphoreType.DMA((n, 7)), pltpu.SemaphoreType.DMA((n,))],
    )(*xs)


def _exchange(name, xs):
    n = len(xs)

    def body(*refs):
        x_refs, o_refs = refs[:n], refs[n:2 * n]
        send_sems, recv_sems, local_sems = refs[2 * n:]
        x, y, c = _place()
        me = (x, y, c)
        peers = [((1 - x) if k & 4 else x, (1 - y) if k & 2 else y, (1 - c) if k & 1 else c) for k in range(1, N_DEV)]

        def copy(i, k, peer):
            return pltpu.make_async_remote_copy(src_ref=x_refs[i].at[_index(peer)], dst_ref=o_refs[i].at[_index(me)],
                                                send_sem=send_sems.at[i, k], recv_sem=recv_sems.at[i, k],
                                                device_id=peer, device_id_type=MESH_ID)

        def arrival(i, k, peer):
            return pltpu.make_async_remote_copy(src_ref=x_refs[i].at[_index(peer)], dst_ref=o_refs[i].at[_index(peer)],
                                                send_sem=send_sems.at[i, k], recv_sem=recv_sems.at[i, k],
                                                device_id=peer, device_id_type=MESH_ID)

        mine = [pltpu.make_async_copy(x_refs[i].at[_index(me)], o_refs[i].at[_index(me)], local_sems.at[i])
                for i in range(n)]
        for cp in mine:
            cp.start()
        sent = [copy(i, k, peer) for k, peer in enumerate(peers) for i in range(n)]
        for cp in sent:
            cp.start()
        for k, peer in enumerate(peers):
            for i in range(n):
                arrival(i, k, peer).wait_recv()
        for cp in sent:
            cp.wait_send()
        for cp in mine:
            cp.wait()

    return pl.pallas_call(
        body, name=name, in_specs=[ANY] * n, out_specs=[ANY] * n,
        out_shape=[jax.ShapeDtypeStruct(x.shape, x.dtype) for x in xs],
        scratch_shapes=[pltpu.SemaphoreType.DMA((n, 7)), pltpu.SemaphoreType.DMA((n, 7)), pltpu.SemaphoreType.DMA((n,))],
    )(*xs)


def _pack(arrays):
    flat = jnp.concatenate([a.reshape(-1) for a in arrays])
    pad = (-flat.shape[0]) % 1024
    return jnp.pad(flat, (0, pad)).reshape(-1, 128)


def _unpack(packed, shapes):
    flat = packed.reshape(-1)
    out, pos = [], 0
    for s in shapes:
        size = 1
        for dim in s:
            size *= dim
        out.append(flat[pos:pos + size].reshape(s))
        pos += size
    return out


def _cols_from_slabs(stack):
    return jnp.transpose(stack, (1, 0, 2)).reshape(stack.shape[1], -1)


def _cols_to_slabs(full):
    return jnp.transpose(full.reshape(full.shape[0], N_DEV, -1), (1, 0, 2))


def kernel(x, meta_tokens, ffn1_norm, ffn1_w_gu, ffn1_w_down, mix_norm, w_in, a_conv_w, a_log_rate, a_dt_bias, a_out_norm, b_shift_mu, b_w0, b_w_up, b_a0, b_a_up, b_g_up, b_k_k, b_k_a, b_r_k, b_ln_gain, b_ln_bias, w_out, ffn2_norm, ffn2_w_gu, ffn2_w_down, final_norm, loss_target, m_meta_tokens, m_ffn1_norm, m_ffn1_w_gu, m_ffn1_w_down, m_mix_norm, m_w_in, m_a_conv_w, m_a_log_rate, m_a_dt_bias, m_a_out_norm, m_b_shift_mu, m_b_w0, m_b_w_up, m_b_a0, m_b_a_up, m_b_g_up, m_b_k_k, m_b_k_a, m_b_r_k, m_b_ln_gain, m_b_ln_bias, m_w_out, m_ffn2_norm, m_ffn2_w_gu, m_ffn2_w_down, m_final_norm, v_meta_tokens, v_ffn1_norm, v_ffn1_w_gu, v_ffn1_w_down, v_mix_norm, v_w_in, v_a_conv_w, v_a_log_rate, v_a_dt_bias, v_a_out_norm, v_b_shift_mu, v_b_w0, v_b_w_up, v_b_a0, v_b_a_up, v_b_g_up, v_b_k_k, v_b_k_a, v_b_r_k, v_b_ln_gain, v_b_ln_bias, v_w_out, v_ffn2_norm, v_ffn2_w_gu, v_ffn2_w_down, v_final_norm):
    names = ['meta_tokens', 'ffn1_norm', 'ffn1_w_gu', 'ffn1_w_down', 'mix_norm', 'w_in', 'a_conv_w', 'a_log_rate',
             'a_dt_bias', 'a_out_norm', 'b_shift_mu', 'b_w0', 'b_w_up', 'b_a0', 'b_a_up', 'b_g_up', 'b_k_k', 'b_k_a',
             'b_r_k', 'b_ln_gain', 'b_ln_bias', 'w_out', 'ffn2_norm', 'ffn2_w_gu', 'ffn2_w_down', 'final_norm']
    env = dict(locals())
    wts = {k: env[k] for k in names}
    mom_m = {k: env['m_' + k] for k in names}
    mom_v = {k: env['v_' + k] for k in names}
    big = ['ffn1_w_gu', 'ffn1_w_down', 'w_in', 'w_out', 'ffn2_w_gu', 'ffn2_w_down']
    col_sharded = {'ffn1_w_gu', 'w_in', 'ffn2_w_gu'}
    small_sharded = ['meta_tokens', 'a_conv_w', 'b_w_up', 'b_a_up', 'b_g_up']
    replicated = [k for k in names if k not in big and k not in small_sharded]

    seq, d = x.shape[1], x.shape[2]
    rows = PAD + N_META + seq
    heads_a = d // HEAD_A
    tb_mm = _tb(rows, 416)
    tb_vjp = _tb(rows, 208)
    me = _index(_place())

    big_local = [wts[k][0].astype(BF16) for k in big]
    small_local = [wts['meta_tokens']] + [wts[k][0] for k in small_sharded[1:]]
    gathered = _all_gather("gather_weights", big_local + small_local)
    gw = dict(zip(big + small_sharded, gathered))
    full = {k: (gw[k] if k in ('ffn1_w_gu', 'ffn2_w_gu') else _cols_from_slabs(gw[k]) if k in col_sharded
                else gw[k].reshape(-1, gw[k].shape[-1])) for k in big}
    for k in small_sharded:
        full[k] = _cols_from_slabs(gw[k])
    for k in replicated:
        full[k] = wts[k].reshape(1, -1)

    win = full['w_in']
    n_b = 3 * d + LORA_W + LORA_A + LORA_G
    off_beta, off_b = 4 * d, 4 * d + 2 * heads_a
    off_ga = off_b + n_b
    b_width = 3 * d + LORA_PAD
    zcols = lambda r, c: jnp.zeros((r, c), BF16)
    w_qkv = win[:, :3 * d]
    w_zg = jnp.concatenate([win[:, 3 * d:4 * d], win[:, off_ga:off_ga + 2 * d]], axis=1)
    w_b = jnp.concatenate([win[:, off_b:off_b + n_b], zcols(d, b_width - n_b)], axis=1)
    w_bg = jnp.concatenate([win[:, off_beta:off_beta + 2 * heads_a], zcols(d, 128 - 2 * heads_a)], axis=1)

    def lanes(vec, start, width):
        return jnp.pad(vec.reshape(1, -1), ((0, 0), (start, width - start - vec.size)))

    log_rate = lanes(wts['a_log_rate'], heads_a, 128)
    dt_bias = lanes(wts['a_dt_bias'], heads_a, 128)
    mu = lanes(wts['b_shift_mu'], 0, b_width)
    w_up = jnp.pad(full['b_w_up'], ((0, 128 - LORA_W), (0, 0)))
    a_up = jnp.pad(full['b_a_up'], ((LORA_W, 0), (0, 0)))
    g_up = jnp.pad(full['b_g_up'], ((0, 256 - LORA_G), (0, 0)))
    b_pars = [full['b_w0'], w_up, full['b_a0'], a_up, g_up, full['b_k_k'], full['b_k_a']]
    post_pars = [full['a_out_norm'], full['b_r_k'], full['b_ln_gain'], full['b_ln_bias']]
    bg_fn = _make_bg_fn(heads_a)

    h0 = jnp.concatenate([jnp.zeros((PAD, d), F32), full['meta_tokens'], x[0]], axis=0)
    h1 = _ffn_fwd("ffn1_fwd", h0, full['ffn1_norm'], full['ffn1_w_gu'], full['ffn1_w_down'], tb_mm)
    (u,) = _tok_fwd("mix_norm_fwd", _norm_fn, [h1], [full['mix_norm']], [(d, BF16)], tb_mm)
    z_qkv = _mm("in_qkv", u, w_qkv, tb=tb_mm, tn=_col_tile(3 * d, 1536))
    z_zg = _mm("in_zg", u, w_zg, tb=tb_mm, tn=_col_tile(3 * d, 1536))
    z_b = _mm("in_b", u, w_b, tb=tb_mm, tn=_col_tile(b_width, 1536))
    z_bg = _mm("in_bg", u, w_bg, tb=tb_mm, tn=128)
    qkv = _a_pre_fwd(z_qkv, full['a_conv_w'])
    (bg,) = _tok_fwd("bg_fwd", bg_fn, [z_bg], [log_rate, dt_bias], [(128, F32)], tb_mm)
    o_dn, dn_hist = _dn_fwd(qkv, bg)
    zf = _shift_fwd(z_b, mu)
    rr, ww, kk2, vv, av, bv, gate = _tok_fwd("b_pre_fwd", _b_pre_fn, [zf], b_pars, [(d, F32)] * 7, tb_vjp)
    per_head = lambda t: t.reshape(rows, d // HEAD_B, HEAD_B)
    y_heads, b_hist, b_last = _rwkv_fwd(rr, ww, kk2, per_head(vv), av, bv)
    y_b = y_heads.reshape(rows, d)
    post_toks = [o_dn, z_zg, y_b, rr, kk2, vv, gate]
    (merged,) = _tok_fwd("post_fwd", _post_fn, post_toks, post_pars, [(d, BF16)], tb_vjp)
    h2 = _mm("out_proj", merged, full['w_out'], add=h1, tb=tb_mm, tn=d)
    h3 = _ffn_fwd("ffn2_fwd", h2, full['ffn2_norm'], full['ffn2_w_gu'], full['ffn2_w_down'], tb_mm)

    target = jnp.pad(loss_target[0], ((CHUNK, 0), (0, 0)))
    dh3, g_final, loss_part = _loss(h3, target, full['final_norm'].reshape(1, d), tb_vjp)

    def ffn_backward(tag, h, dout, key_norm, key_gu, key_down):
        dh, dh_bf, dgain, xn, act, dgate, dup, dhalf = _ffn_bwd(tag + "_bwd", h, full[key_norm], dout, full[key_gu],
                                                                full[key_down], tb_mm)
        d_gu = jnp.concatenate([_mm_tn_to_slabs(tag + "_dw_gate", xn, dgate, tk=tb_mm),
                                _mm_tn_to_slabs(tag + "_dw_up", xn, dup, tk=tb_mm)], axis=0)
        d_down = _mm_tn_from_slabs(tag + "_dw_down", act, dhalf, tk=tb_mm).reshape(N_DEV, -1, d)
        return dh, dh_bf, dgain, d_gu, d_down

    dh2, dh2_bf, g_ffn2_norm, g_ffn2_gu, g_ffn2_down = ffn_backward("ffn2", h2, dh3, 'ffn2_norm', 'ffn2_w_gu',
                                                                    'ffn2_w_down')
    g_w_out = _mm_tn("dw_out", merged, dh2_bf, tm=d, tn=d, tk=tb_mm).reshape(N_DEV, -1, d)
    dmerged = _mm("d_merged", dh2_bf, full['w_out'], trans_b=True, tb=tb_mm, tn=d)
    post_grads = _tok_bwd("post_bwd", _post_fn, post_toks, post_pars, [[dmerged]], list(range(7)), tb_vjp,
                          [F32, BF16] + [F32] * 5)
    do_dn, dz_zg, dy_b, dr1, dk1, dv1, dgate = post_grads[:7]
    g_out_norm, g_r_k, g_ln_g, g_ln_b = post_grads[7:]
    dr2, dw2, dk2, dv_heads, da2, db2 = _rwkv_bwd(rr, ww, kk2, per_head(vv), av, bv, b_hist, b_last, per_head(dy_b))
    dv2 = dv_heads.reshape(rows, d)
    b_grads = _tok_bwd("b_pre_bwd", _b_pre_fn, [zf], b_pars,
                       [[dr1, dr2], [dw2], [dk1, dk2], [dv1, dv2], [da2], [db2], [dgate]], [0], tb_vjp)
    dzf = b_grads[0]
    g_w0, g_w_up, g_a0, g_a_up, g_g_up, g_k_k, g_k_a = b_grads[1:]
    dz_b, g_mu = _shift_bwd(z_b, mu, dzf)
    dqkv, dbg = _dn_bwd(qkv, bg, dn_hist, do_dn)
    dz_qkv, g_conv = _a_pre_bwd(z_qkv, full['a_conv_w'], dqkv)
    dz_bg, g_log_rate, g_dt_bias = _tok_bwd("bg_bwd", bg_fn, [z_bg], [log_rate, dt_bias], [[dbg]], [0], tb_mm, [BF16])

    du = None
    g_w_in_parts = []
    for tag, dz, wpiece in (("qkv", dz_qkv, w_qkv), ("zg", dz_zg, w_zg), ("b", dz_b, w_b), ("bg", dz_bg, w_bg)):
        du = _mm("du_" + tag, dz, wpiece, trans_b=True, add=du, tb=tb_mm, tn=d)
        g_w_in_parts.append(_mm_tn("dw_in_" + tag, u, dz, tm=d, tn=_col_tile(dz.shape[1], 1536), tk=tb_mm))
    dh1, g_mix_norm = _tok_bwd("mix_norm_bwd", _norm_res_fn, [h1], [full['mix_norm']], [[du], [dh2]], [0], tb_vjp)
    dh0, _, g_ffn1_norm, g_ffn1_gu, g_ffn1_down = ffn_backward("ffn1", h0, dh1, 'ffn1_norm', 'ffn1_w_gu', 'ffn1_w_down')

    gp_qkv, gp_zg, gp_b, gp_bg = g_w_in_parts
    g_w_in = _cols_to_slabs(jnp.concatenate(
        [gp_qkv, gp_zg[:, :d], gp_bg[:, :2 * heads_a], gp_b[:, :n_b], gp_zg[:, d:]], axis=1))

    slabs = {'ffn1_w_gu': g_ffn1_gu, 'ffn1_w_down': g_ffn1_down, 'w_in': g_w_in, 'w_out': g_w_out,
             'ffn2_w_gu': g_ffn2_gu, 'ffn2_w_down': g_ffn2_down}
    received = dict(zip(big, _exchange("exchange_grads", [slabs[k] for k in big])))

    small_full = {
        'meta_tokens': dh0[PAD:CHUNK], 'ffn1_norm': g_ffn1_norm, 'mix_norm': g_mix_norm, 'a_conv_w': g_conv,
        'a_log_rate': g_log_rate[:, heads_a:2 * heads_a], 'a_dt_bias': g_dt_bias[:, heads_a:2 * heads_a],
        'a_out_norm': g_out_norm, 'b_shift_mu': g_mu[:, :n_b], 'b_w0': g_w0, 'b_w_up': g_w_up[:LORA_W],
        'b_a0': g_a0, 'b_a_up': g_a_up[LORA_W:], 'b_g_up': g_g_up[:LORA_G], 'b_k_k': g_k_k, 'b_k_a': g_k_a,
        'b_r_k': g_r_k, 'b_ln_gain': g_ln_g, 'b_ln_bias': g_ln_b, 'ffn2_norm': g_ffn2_norm, 'final_norm': g_final,
    }
    small_names = list(small_full)
    packed = _pack([small_full[k] for k in small_names] + [loss_part[:, :1]])
    (all_parts,) = _all_gather("gather_small_grads", [packed])
    summed = _sum_slabs("sum_small_grads", all_parts, packed.shape[0])
    pieces = _unpack(summed, [small_full[k].shape for k in small_names] + [(1, 1)])
    small_grad = dict(zip(small_names, pieces[:-1]))
    loss = pieces[-1].reshape(())

    grads, deltas, new_m, new_v = {}, {}, {}, {}
    for k in big:
        shard = wts[k][0]
        rb = _tb(shard.shape[0], 128) if shard.shape[0] % 16 == 0 else shard.shape[0]
        g, dl, m2, v2 = _adamw("adamw_" + k, received[k], shard, mom_m[k][0], mom_v[k][0], rb)
        grads[k], deltas[k], new_m[k], new_v[k] = g[None], dl[None], m2[None], v2[None]

    local_small = {}
    for k in small_names:
        g = small_grad[k]
        if k in small_sharded:
            width = wts[k].shape[-1]
            g = lax.dynamic_slice_in_dim(g, me * width, width, axis=1)
        local_small[k] = g.reshape(wts[k].shape)
    pk = lambda tree: _pack([tree[k] for k in small_names])
    dl_s, m_s, v_s = _adamw_small(pk(wts), pk(local_small), pk(mom_m), pk(mom_v))
    shapes = [wts[k].shape for k in small_names]
    for k, dl, m2, v2 in zip(small_names, _unpack(dl_s, shapes), _unpack(m_s, shapes), _unpack(v_s, shapes)):
        grads[k], deltas[k], new_m[k], new_v[k] = local_small[k], dl, m2, v2

    grad_x = dh0[CHUNK:][None]
    return (loss, grad_x, *[grads[k] for k in names], *[deltas[k] for k in names],
            *[new_m[k] for k in names], *[new_v[k] for k in names])
```

```python
import functools

import jax
import jax.numpy as jnp
from jax import lax
from jax.experimental import pallas as pl
from jax.experimental.pallas import tpu as pltpu

F32 = jnp.float32
BF16 = jnp.bfloat16
N_DEV = 8
N_META = 16
CHUNK = 64
PAD = CHUNK - N_META
HEAD_A = 128
HEAD_B = 64
LORA_W, LORA_A, LORA_G = 64, 64, 160
LORA_PAD = 384
EPS = 1e-6
GN_EPS = HEAD_B * 1e-5
ADAM_LR, ADAM_B1, ADAM_B2, ADAM_EPS, ADAM_WD, ADAM_STEP = 0.001, 0.9, 0.999, 1e-08, 0.01, 10
SCAN_STEPS = 16
MXU_WIDTH = 256
VMEM_LIMIT = 56 * 1024 * 1024
DN_PRECISION = lax.Precision.HIGH
MESH_ID = pl.DeviceIdType.MESH
ANY = pl.BlockSpec(memory_space=pl.ANY)


def _cp(*sem):
    return pltpu.CompilerParams(dimension_semantics=sem, vmem_limit_bytes=VMEM_LIMIT)


def _tb(t, target):
    best = 16
    for d in range(16, target + 1, 16):
        if t % d == 0:
            best = d
    return best


def _sigmoid(x):
    return 1.0 / (1.0 + jnp.exp(-x))


def _silu(x):
    return x * _sigmoid(x)


def _softplus(x):
    return jnp.maximum(x, 0.0) + jnp.log(1.0 + jnp.exp(-jnp.abs(x)))


def _dot_nt(a, b, precision=None):
    return lax.dot_general(a, b, (((1,), (1,)), ((), ())), preferred_element_type=F32, precision=precision)


def _dot_tn(a, b, precision=None):
    return lax.dot_general(a, b, (((0,), (0,)), ((), ())), preferred_element_type=F32, precision=precision)


def _dot(a, b, precision=None):
    return jnp.dot(a, b, preferred_element_type=F32, precision=precision)


def _block_diag_ones():
    i = lax.broadcasted_iota(jnp.int32, (MXU_WIDTH, MXU_WIDTH), 0) // HEAD_B
    j = lax.broadcasted_iota(jnp.int32, (MXU_WIDTH, MXU_WIDTH), 1) // HEAD_B
    return (i == j).astype(BF16)


def _hi_lo(x):
    hi = x.astype(BF16)
    return hi, (x - hi.astype(F32)).astype(BF16)


def _segsum_many(xs, bd):
    groups = [x if isinstance(x, tuple) else _hi_lo(x) for x in xs]
    rows = groups[0][0].shape[0]
    stacked = jnp.concatenate([p for grp in groups for p in grp], axis=0)
    out = jnp.concatenate([_dot(stacked[:, s:s + MXU_WIDTH], bd) for s in range(0, stacked.shape[1], MXU_WIDTH)], axis=1)
    res, pos = [], 0
    for grp in groups:
        acc = out[pos:pos + rows]
        for j in range(1, len(grp)):
            acc = acc + out[pos + j * rows:pos + (j + 1) * rows]
        res.append(acc)
        pos += len(grp) * rows
    return res


def _segsum_impl(x):
    return _segsum_many([x], _block_diag_ones())[0]


@jax.custom_vjp
def _segsum64(x):
    return _segsum_impl(x)


_segsum64.defvjp(lambda x: (_segsum_impl(x), None), lambda _, ct: (_segsum_impl(ct),))


def _tok(t):
    return t if isinstance(t, tuple) else (t, t.shape[1], 0)


def _tok_spec(tb, width, colblk):
    return pl.BlockSpec((tb, width), lambda i: (i, colblk))


def _par_spec(p):
    return pl.BlockSpec(p.shape, lambda i: (0, 0))


def _tok_fwd(name, fn, toks, pars, outs, tb):
    toks = [_tok(t) for t in toks]
    rows = toks[0][0].shape[0]
    n_in = len(toks) + len(pars)

    def body(*refs):
        row0 = pl.program_id(0) * tb
        res = fn(row0, *[r[...] for r in refs[:n_in]])
        for r, o in zip(refs[n_in:], res):
            r[...] = o.astype(r.dtype)

    return pl.pallas_call(
        body, name=name, grid=(rows // tb,),
        in_specs=[_tok_spec(tb, w, c) for _, w, c in toks] + [_par_spec(p) for p in pars],
        out_specs=[_tok_spec(tb, w, 0) for w, _ in outs],
        out_shape=[jax.ShapeDtypeStruct((rows, w), dt) for w, dt in outs],
        compiler_params=_cp("parallel"),
    )(*[a for a, _, _ in toks], *pars)


def _tok_bwd(name, fn, toks, pars, cts, want, tb, want_dtypes=None):
    toks = [_tok(t) for t in toks]
    want_dtypes = want_dtypes or [F32] * len(want)
    cts = [[_tok(c) for c in group] for group in cts]
    flat_cts = [c for group in cts for c in group]
    rows = toks[0][0].shape[0]
    n_tok, n_par, n_ct = len(toks), len(pars), len(flat_cts)

    def body(*refs):
        i = pl.program_id(0)
        row0 = i * tb
        prim = [r[...].astype(F32) for r in refs[:n_tok + n_par]]
        ct_refs = list(refs[n_tok + n_par:n_tok + n_par + n_ct])
        out_refs = refs[n_tok + n_par + n_ct:]
        res, vjp = jax.vjp(lambda *a: fn(row0, *a), *prim)
        ct = []
        for group, o in zip(cts, res):
            acc = None
            for _ in group:
                v = ct_refs.pop(0)[...].astype(F32)
                acc = v if acc is None else acc + v
            ct.append(acc.astype(o.dtype))
        grads = vjp(tuple(ct))
        for r, k in zip(out_refs[:len(want)], want):
            r[...] = grads[k].astype(r.dtype)

        @pl.when(i == 0)
        def _():
            for r in out_refs[len(want):]:
                r[...] = jnp.zeros_like(r)

        for r, g in zip(out_refs[len(want):], grads[n_tok:]):
            r[...] += g

    return pl.pallas_call(
        body, name=name, grid=(rows // tb,),
        in_specs=[_tok_spec(tb, w, c) for _, w, c in toks] + [_par_spec(p) for p in pars]
        + [_tok_spec(tb, w, c) for _, w, c in flat_cts],
        out_specs=[_tok_spec(tb, toks[k][1], 0) for k in want] + [_par_spec(p) for p in pars],
        out_shape=[jax.ShapeDtypeStruct((rows, toks[k][1]), dt) for k, dt in zip(want, want_dtypes)]
        + [jax.ShapeDtypeStruct(p.shape, F32) for p in pars],
        compiler_params=_cp("arbitrary"),
    )(*[a for a, _, _ in toks], *pars, *[a for a, _, _ in flat_cts])


def _mm(name, a, b, *, trans_b=False, add=None, tb, tn):
    rows, k = a.shape
    n = b.shape[0] if trans_b else b.shape[1]

    def body(*refs):
        a_ref, b_ref = refs[:2]
        o_ref = refs[-1]
        acc = _dot_nt(a_ref[...], b_ref[...]) if trans_b else _dot(a_ref[...], b_ref[...])
        if add is not None:
            acc = acc + refs[2][...]
        o_ref[...] = acc

    in_specs = [pl.BlockSpec((tb, k), lambda i, j: (i, 0)),
                pl.BlockSpec((tn, k), lambda i, j: (j, 0)) if trans_b else pl.BlockSpec((k, tn), lambda i, j: (0, j))]
    args = [a, b]
    if add is not None:
        in_specs.append(pl.BlockSpec((tb, tn), lambda i, j: (i, j)))
        args.append(add)
    return pl.pallas_call(
        body, name=name, grid=(rows // tb, n // tn), in_specs=in_specs,
        out_specs=pl.BlockSpec((tb, tn), lambda i, j: (i, j)),
        out_shape=jax.ShapeDtypeStruct((rows, n), F32),
        compiler_params=_cp("parallel", "parallel"),
    )(*args)


def _mm_tn_call(name, grid, a, b, a_spec, b_spec, o_spec, acc_shape, out_shape):
    last = len(grid) - 1

    def body(a_ref, b_ref, o_ref, acc_ref):
        k = pl.program_id(last)

        @pl.when(k == 0)
        def _():
            acc_ref[...] = jnp.zeros_like(acc_ref)

        a_blk = a_ref[0] if len(a_ref.shape) == 3 else a_ref[...]
        b_blk = b_ref[0] if len(b_ref.shape) == 3 else b_ref[...]
        acc_ref[...] += _dot_tn(a_blk, b_blk)

        @pl.when(k == grid[last] - 1)
        def _():
            if len(o_ref.shape) == 3:
                o_ref[0] = acc_ref[...].astype(o_ref.dtype)
            else:
                o_ref[...] = acc_ref[...].astype(o_ref.dtype)

    return pl.pallas_call(
        body, name=name, grid=grid, in_specs=[a_spec, b_spec], out_specs=o_spec,
        out_shape=jax.ShapeDtypeStruct(out_shape, BF16), scratch_shapes=[pltpu.VMEM(acc_shape, F32)],
        compiler_params=_cp(*(["parallel"] * last + ["arbitrary"])),
    )(a, b)


def _mm_tn(name, a, b, *, tm, tn, tk):
    rows, m = a.shape
    n = b.shape[1]
    return _mm_tn_call(name, (m // tm, n // tn, rows // tk), a, b,
                       pl.BlockSpec((tk, tm), lambda i, j, k: (k, i)), pl.BlockSpec((tk, tn), lambda i, j, k: (k, j)),
                       pl.BlockSpec((tm, tn), lambda i, j, k: (i, j)), (tm, tn), (m, n))


def _mm_tn_to_slabs(name, a, b3, *, tk):
    rows, m = a.shape
    s, _, c = b3.shape
    return _mm_tn_call(name, (s, rows // tk), a, b3,
                       pl.BlockSpec((tk, m), lambda i, k: (k, 0)), pl.BlockSpec((1, tk, c), lambda i, k: (i, k, 0)),
                       pl.BlockSpec((1, m, c), lambda i, k: (i, 0, 0)), (m, c), (s, m, c))


def _mm_tn_from_slabs(name, a3, b, *, tk):
    s, rows, c = a3.shape
    n = b.shape[1]
    return _mm_tn_call(name, (s, rows // tk), a3, b,
                       pl.BlockSpec((1, tk, c), lambda i, k: (i, k, 0)), pl.BlockSpec((tk, n), lambda i, k: (k, 0)),
                       pl.BlockSpec((c, n), lambda i, k: (i, 0)), (c, n), (s * c, n))


def _col_tile(n, target):
    if n <= target:
        return n
    best = 128
    for d in range(128, target + 1, 128):
        if n % d == 0:
            best = d
    return best


def _rms(x, gain):
    return x * lax.rsqrt(jnp.mean(x * x, axis=-1, keepdims=True) + EPS) * gain


def _ffn_specs(d, fc, nj):
    return [pl.BlockSpec((1, d, fc), lambda i, j: (j, 0, 0)), pl.BlockSpec((1, d, fc), lambda i, j: (nj + j, 0, 0)),
            pl.BlockSpec((fc, d), lambda i, j: (j, 0))]


def _ffn_fwd(name, h, gain, wgu, wd, tb):
    rows, d = h.shape
    nj = wgu.shape[0] // 2
    fc = wgu.shape[2]

    def body(h_ref, g_ref, wg_ref, wu_ref, wd_ref, o_ref, xn_s, acc_s):
        j = pl.program_id(1)

        @pl.when(j == 0)
        def _():
            xn_s[...] = _rms(h_ref[...], g_ref[...]).astype(BF16)
            acc_s[...] = jnp.zeros_like(acc_s)

        xn = xn_s[...]
        gate = _dot(xn, wg_ref[0])
        up = _dot(xn, wu_ref[0])
        acc_s[...] += _dot((_silu(gate) * up).astype(BF16), wd_ref[...])

        @pl.when(j == nj - 1)
        def _():
            o_ref[...] = h_ref[...] + 0.5 * acc_s[...]

    return pl.pallas_call(
        body, name=name, grid=(rows // tb, nj),
        in_specs=[pl.BlockSpec((tb, d), lambda i, j: (i, 0)), pl.BlockSpec((1, d), lambda i, j: (0, 0))]
        + _ffn_specs(d, fc, nj),
        out_specs=pl.BlockSpec((tb, d), lambda i, j: (i, 0)),
        out_shape=jax.ShapeDtypeStruct((rows, d), F32),
        scratch_shapes=[pltpu.VMEM((tb, d), BF16), pltpu.VMEM((tb, d), F32)],
        compiler_params=_cp("parallel", "arbitrary"),
    )(h, gain, wgu, wgu, wd)


def _ffn_bwd(name, h, gain, dout, wgu, wd, tb):
    rows, d = h.shape
    nj = wgu.shape[0] // 2
    fc = wgu.shape[2]

    def body(h_ref, g_ref, do_ref, wg_ref, wu_ref, wd_ref,
             dh_ref, dhb_ref, dg_ref, xn_ref, act_ref, dgate_ref, dup_ref, dhalf_ref, dxn_s):
        i, j = pl.program_id(0), pl.program_id(1)

        @pl.when(j == 0)
        def _():
            xn_ref[...] = _rms(h_ref[...], g_ref[...]).astype(BF16)
            dhalf_ref[...] = (0.5 * do_ref[...]).astype(BF16)
            dxn_s[...] = jnp.zeros_like(dxn_s)

        xn = xn_ref[...]
        wg, wu = wg_ref[0], wu_ref[0]
        gate = _dot(xn, wg)
        up = _dot(xn, wu)
        sg = _sigmoid(gate)
        dact = _dot_nt(dhalf_ref[...], wd_ref[...])
        act_ref[0] = (gate * sg * up).astype(BF16)
        dgate = (dact * up * (sg * (1.0 + gate * (1.0 - sg)))).astype(BF16)
        dup = (dact * gate * sg).astype(BF16)
        dgate_ref[0] = dgate
        dup_ref[0] = dup
        dxn_s[...] += _dot_nt(dgate, wg) + _dot_nt(dup, wu)

        @pl.when((i == 0) & (j == 0))
        def _():
            dg_ref[...] = jnp.zeros_like(dg_ref)

        @pl.when(j == nj - 1)
        def _():
            x = h_ref[...]
            r = lax.rsqrt(jnp.mean(x * x, axis=-1, keepdims=True) + EPS)
            dxn = dxn_s[...]
            dyg = dxn * g_ref[...]
            dh = do_ref[...] + r * dyg - x * (r * r * r) * jnp.mean(dyg * x, axis=-1, keepdims=True)
            dh_ref[...] = dh
            dhb_ref[...] = dh.astype(BF16)
            dg_ref[...] += jnp.sum(dxn * x * r, axis=0, keepdims=True)

    row_d = pl.BlockSpec((tb, d), lambda i, j: (i, 0))
    slab = pl.BlockSpec((1, tb, fc), lambda i, j: (j, i, 0))
    hidden = jax.ShapeDtypeStruct((nj, rows, fc), BF16)
    return pl.pallas_call(
        body, name=name, grid=(rows // tb, nj),
        in_specs=[row_d, pl.BlockSpec((1, d), lambda i, j: (0, 0)), row_d] + _ffn_specs(d, fc, nj),
        out_specs=[row_d, row_d, pl.BlockSpec((1, d), lambda i, j: (0, 0)), row_d, slab, slab, slab, row_d],
        out_shape=[jax.ShapeDtypeStruct((rows, d), F32), jax.ShapeDtypeStruct((rows, d), BF16),
                   jax.ShapeDtypeStruct((1, d), F32), jax.ShapeDtypeStruct((rows, d), BF16),
                   hidden, hidden, hidden, jax.ShapeDtypeStruct((rows, d), BF16)],
        scratch_shapes=[pltpu.VMEM((tb, d), F32)],
        compiler_params=_cp("arbitrary", "arbitrary"),
    )(h, gain, dout, wgu, wgu, wd)


def _shift_rows(x, s):
    return pltpu.roll(x, s % x.shape[0], 0)


def _a_post(c, which):
    s = _silu(c)
    n = s * lax.rsqrt(jnp.sum(s * s, axis=-1, keepdims=True) + 1e-6)
    scale = jnp.where(which == 0, HEAD_A ** -0.5, 1.0)
    return jnp.where(which == 2, s, n * scale)


def _conv(x, w):
    return x * w[3:4] + _shift_rows(x, 1) * w[2:3] + _shift_rows(x, 2) * w[1:2] + _shift_rows(x, 3) * w[0:1]


def _a_pre_fwd(zqkv, conv_w):
    rows, width = zqkv.shape
    heads = width // (3 * HEAD_A)

    def body(x_ref, w_ref, o_ref):
        which = pl.program_id(0) // heads
        live = lax.broadcasted_iota(jnp.int32, (rows, HEAD_A), 0) >= PAD
        o_ref[...] = jnp.where(live, _a_post(_conv(x_ref[...], w_ref[...]), which), 0.0)

    return pl.pallas_call(
        body, name="a_pre_fwd", grid=(width // HEAD_A,),
        in_specs=[pl.BlockSpec((rows, HEAD_A), lambda c: (0, c)), pl.BlockSpec((4, HEAD_A), lambda c: (0, c))],
        out_specs=pl.BlockSpec((rows, HEAD_A), lambda c: (0, c)),
        out_shape=jax.ShapeDtypeStruct((rows, width), F32),
        compiler_params=_cp("parallel"),
    )(zqkv, conv_w)


def _a_pre_bwd(zqkv, conv_w, dqkv):
    rows, width = zqkv.shape
    heads = width // (3 * HEAD_A)

    def body(x_ref, w_ref, ct_ref, dx_ref, dw_ref):
        which = pl.program_id(0) // heads
        live = lax.broadcasted_iota(jnp.int32, (rows, HEAD_A), 0) >= PAD
        x, w = x_ref[...], w_ref[...]
        _, vjp = jax.vjp(lambda c: _a_post(c, which), _conv(x, w))
        (dc,) = vjp(jnp.where(live, ct_ref[...], 0.0))
        dc = jnp.where(live, dc, 0.0)
        dx_ref[...] = (dc * w[3:4] + _shift_rows(dc, -1) * w[2:3] + _shift_rows(dc, -2) * w[1:2]
                       + _shift_rows(dc, -3) * w[0:1]).astype(BF16)
        dw_ref[...] = jnp.concatenate(
            [jnp.sum(dc * (_shift_rows(x, 3 - j) if j < 3 else x), axis=0, keepdims=True) for j in range(4)], axis=0)

    col = pl.BlockSpec((rows, HEAD_A), lambda c: (0, c))
    wsp = pl.BlockSpec((4, HEAD_A), lambda c: (0, c))
    return pl.pallas_call(
        body, name="a_pre_bwd", grid=(width // HEAD_A,),
        in_specs=[col, wsp, col], out_specs=[col, wsp],
        out_shape=[jax.ShapeDtypeStruct((rows, width), BF16), jax.ShapeDtypeStruct((4, width), F32)],
        compiler_params=_cp("parallel"),
    )(zqkv, conv_w, dqkv)


SHIFT_TILE = 384


def _shift_fwd(zb, mu):
    rows, width = zb.shape

    def body(x_ref, mu_ref, o_ref):
        x = x_ref[...]
        first = lax.broadcasted_iota(jnp.int32, x.shape, 0) == 0
        prev = jnp.where(first, 0.0, _shift_rows(x, 1))
        o_ref[...] = x + (prev - x) * mu_ref[...]

    col = pl.BlockSpec((rows, SHIFT_TILE), lambda c: (0, c))
    return pl.pallas_call(
        body, name="shift_fwd", grid=(width // SHIFT_TILE,),
        in_specs=[col, pl.BlockSpec((1, SHIFT_TILE), lambda c: (0, c))], out_specs=col,
        out_shape=jax.ShapeDtypeStruct((rows, width), F32), compiler_params=_cp("parallel"),
    )(zb, mu)


def _shift_bwd(zb, mu, dzf):
    rows, width = zb.shape

    def body(x_ref, mu_ref, ct_ref, dx_ref, dmu_ref):
        x, ct, mu_v = x_ref[...], ct_ref[...], mu_ref[...]
        row = lax.broadcasted_iota(jnp.int32, x.shape, 0)
        prev = jnp.where(row == 0, 0.0, _shift_rows(x, 1))
        nxt = jnp.where(row == rows - 1, 0.0, _shift_rows(ct, -1))
        dx_ref[...] = (ct * (1.0 - mu_v) + nxt * mu_v).astype(BF16)
        dmu_ref[...] = jnp.sum(ct * (prev - x), axis=0, keepdims=True)

    col = pl.BlockSpec((rows, SHIFT_TILE), lambda c: (0, c))
    msp = pl.BlockSpec((1, SHIFT_TILE), lambda c: (0, c))
    return pl.pallas_call(
        body, name="shift_bwd", grid=(width // SHIFT_TILE,),
        in_specs=[col, msp, col], out_specs=[col, msp],
        out_shape=[jax.ShapeDtypeStruct((rows, width), BF16), jax.ShapeDtypeStruct((1, width), F32)],
        compiler_params=_cp("parallel"),
    )(zb, mu, dzf)


def _dn_chunk(q, k, v, beta, g, state):
    heads = range(len(q))
    ri = lax.broadcasted_iota(jnp.int32, (CHUNK, CHUNK), 0)
    ci = lax.broadcasted_iota(jnp.int32, (CHUNK, CHUNK), 1)
    eye = (ri == ci).astype(F32)
    incl = ri >= ci
    last = lax.broadcasted_iota(jnp.int32, (CHUNK, 1), 0) == CHUNK - 1
    g_row = [jnp.sum(g[h] * eye, axis=0, keepdims=True) for h in heads]
    gc = [jnp.sum(jnp.where(incl, g_row[h], 0.0), axis=1, keepdims=True) for h in heads]
    gc_row = [jnp.sum(gc[h] * eye, axis=0, keepdims=True) for h in heads]
    decay = [jnp.where(incl, jnp.exp(jnp.where(incl, gc[h] - gc_row[h], 0.0)), 0.0) for h in heads]
    kb = [k[h] * beta[h] for h in heads]
    vb = [v[h] * beta[h] for h in heads]
    p = [-jnp.where(ri > ci, _dot_nt(kb[h], k[h], DN_PRECISION) * decay[h], 0.0) for h in heads]
    tinv = [eye + p[h] for h in heads]
    for _ in range(5):
        p = [_dot(p[h], p[h], DN_PRECISION) for h in heads]
        tinv = [tinv[h] + _dot(tinv[h], p[h], DN_PRECISION) for h in heads]
    eg = [jnp.exp(gc[h]) for h in heads]
    u = [_dot(tinv[h], vb[h], DN_PRECISION) for h in heads]
    wk = [_dot(tinv[h], kb[h] * eg[h], DN_PRECISION) for h in heads]
    attn = [_dot_nt(q[h], k[h], DN_PRECISION) * decay[h] for h in heads]
    g_last = [jnp.sum(jnp.where(last, gc[h], 0.0), axis=0, keepdims=True) for h in heads]
    k_tail = [k[h] * jnp.exp(g_last[h] - gc[h]) for h in heads]
    v_new = [u[h] - _dot(wk[h], state[h], DN_PRECISION) for h in heads]
    o = [_dot(q[h] * eg[h], state[h], DN_PRECISION) + _dot(attn[h], v_new[h], DN_PRECISION) for h in heads]
    new = [state[h] * jnp.exp(g_last[h]) + _dot_tn(k_tail[h], v_new[h], DN_PRECISION) for h in heads]
    return o, new


def _bg_cols(bg, h, heads):
    lane = lax.broadcasted_iota(jnp.int32, bg.shape, 1)
    beta = jnp.sum(jnp.where(lane == h, bg, 0.0), axis=1, keepdims=True)
    g = jnp.sum(jnp.where(lane == heads + h, bg, 0.0), axis=1, keepdims=True)
    return beta, g


def _dn_fwd(qkv, bg):
    rows = qkv.shape[0]
    heads = qkv.shape[1] // (3 * HEAD_A)
    n = rows // CHUNK
    hp, groups = heads, 1

    def body(q_ref, k_ref, v_ref, bg_ref, o_ref, hist_ref, s_ref):
        c, grp = pl.program_id(0), pl.program_id(1)

        @pl.when(c == 0)
        def _():
            for i in range(hp):
                s_ref[grp * hp + i] = jnp.zeros((HEAD_A, HEAD_A), F32)

        bg_v = bg_ref[...]
        cols = [slice(i * HEAD_A, (i + 1) * HEAD_A) for i in range(hp)]
        state = [s_ref[grp * hp + i] for i in range(hp)]
        beta_g = [_bg_cols(bg_v, grp * hp + i, heads) for i in range(hp)]
        o, new = _dn_chunk([q_ref[:, c_] for c_ in cols], [k_ref[:, c_] for c_ in cols], [v_ref[:, c_] for c_ in cols],
                           [b for b, _ in beta_g], [g for _, g in beta_g], state)
        for i in range(hp):
            hist_ref[0, i] = state[i]
            o_ref[:, cols[i]] = o[i]
            s_ref[grp * hp + i] = new[i]

    def part(p):
        return pl.BlockSpec((CHUNK, hp * HEAD_A), lambda c, grp: (c, p * groups + grp))

    return pl.pallas_call(
        body, name="deltanet_fwd", grid=(n, groups),
        in_specs=[part(0), part(1), part(2), pl.BlockSpec((CHUNK, 128), lambda c, grp: (c, 0))],
        out_specs=[part(0), pl.BlockSpec((1, hp, HEAD_A, HEAD_A), lambda c, grp: (c, grp, 0, 0))],
        out_shape=[jax.ShapeDtypeStruct((rows, heads * HEAD_A), F32),
                   jax.ShapeDtypeStruct((n, heads, HEAD_A, HEAD_A), F32)],
        scratch_shapes=[pltpu.VMEM((heads, HEAD_A, HEAD_A), F32)],
        compiler_params=_cp("arbitrary", "arbitrary"),
    )(qkv, qkv, qkv, bg)


def _dn_bwd(qkv, bg, hist, do):
    rows = qkv.shape[0]
    heads = qkv.shape[1] // (3 * HEAD_A)
    n = rows // CHUNK
    hp, groups = heads, 1

    def body(q_ref, k_ref, v_ref, bg_ref, hist_ref, do_ref, dqkv_ref, dbg_ref, ds_ref):
        c, grp = pl.program_id(0), pl.program_id(1)

        @pl.when(c == 0)
        def _():
            for i in range(hp):
                ds_ref[grp * hp + i] = jnp.zeros((HEAD_A, HEAD_A), F32)

        bg_v = bg_ref[...]
        lane = lax.broadcasted_iota(jnp.int32, (CHUNK, 128), 1)
        cols = [slice(i * HEAD_A, (i + 1) * HEAD_A) for i in range(hp)]
        beta_g = [_bg_cols(bg_v, grp * hp + i, heads) for i in range(hp)]
        _, vjp = jax.vjp(_dn_chunk, [q_ref[:, c_] for c_ in cols], [k_ref[:, c_] for c_ in cols],
                         [v_ref[:, c_] for c_ in cols], [b for b, _ in beta_g], [g for _, g in beta_g],
                         [hist_ref[0, i] for i in range(hp)])
        dq, dk, dv, dbeta, dg, ds = vjp(([do_ref[:, c_] for c_ in cols], [ds_ref[grp * hp + i] for i in range(hp)]))
        dbg = jnp.zeros((CHUNK, 128), F32)
        for i in range(hp):
            h = grp * hp + i
            for p, part_grad in enumerate((dq, dk, dv)):
                dqkv_ref[:, pl.ds((p * heads + i) * HEAD_A, HEAD_A)] = part_grad[i]
            ds_ref[h] = ds[i]
            dbg = dbg + jnp.where(lane == h, dbeta[i], 0.0) + jnp.where(lane == heads + h, dg[i], 0.0)

        @pl.when(grp == 0)
        def _():
            dbg_ref[...] = jnp.zeros_like(dbg_ref)

        dbg_ref[...] += dbg

    def part(p):
        return pl.BlockSpec((CHUNK, hp * HEAD_A), lambda c, grp: (n - 1 - c, p * groups + grp))

    return pl.pallas_call(
        body, name="deltanet_bwd", grid=(n, groups),
        in_specs=[part(0), part(1), part(2), pl.BlockSpec((CHUNK, 128), lambda c, grp: (n - 1 - c, 0)),
                  pl.BlockSpec((1, hp, HEAD_A, HEAD_A), lambda c, grp: (n - 1 - c, grp, 0, 0)), part(0)],
        out_specs=[pl.BlockSpec((CHUNK, 3 * heads * HEAD_A), lambda c, grp: (n - 1 - c, 0)),
                   pl.BlockSpec((CHUNK, 128), lambda c, grp: (n - 1 - c, 0))],
        out_shape=[jax.ShapeDtypeStruct(qkv.shape, F32), jax.ShapeDtypeStruct((rows, 128), F32)],
        scratch_shapes=[pltpu.VMEM((heads, HEAD_A, HEAD_A), F32)],
        compiler_params=_cp("arbitrary", "arbitrary"),
    )(qkv, qkv, qkv, bg, hist, do)


def _head_mask(heads, width):
    return (lax.broadcasted_iota(jnp.int32, (heads, width), 0)
            == lax.broadcasted_iota(jnp.int32, (heads, width), 1) // HEAD_B)


def _masked_rows(mask, row):
    return jnp.where(mask, row, 0.0).astype(BF16)


def _rwkv_fwd(r, w, k, v, a, b):
    rows, width = r.shape
    heads = width // HEAD_B
    ts = SCAN_STEPS

    def body(r_ref, w_ref, k_ref, v_ref, a_ref, b_ref, y_ref, hist_ref, s_ref):
        @pl.when(pl.program_id(0) == 0)
        def _():
            s_ref[...] = jnp.zeros_like(s_ref)

        mask = _head_mask(heads, width)
        onehot = mask.astype(BF16)
        onehot2 = jnp.concatenate([onehot, onehot], axis=0)
        bd = _block_diag_ones()

        spread_v = [_dot_tn(jnp.concatenate(_hi_lo(v_ref[j]), axis=0), onehot2) for j in range(ts)]
        s = s_ref[...]
        ys = []
        for j in range(ts):
            row = pl.ds(j, 1)
            hist_ref[j] = s
            (sa,) = _segsum_many([s * a_ref[row, :]], bd)
            s = s * w_ref[row, :] + sa * b_ref[row, :] + spread_v[j] * k_ref[row, :]
            ys.append(_dot_nt(_masked_rows(mask, r_ref[row, :]), s.astype(BF16)))
        for j in range(ts):
            y_ref[j] = ys[j]
        s_ref[...] = s

    blk = pl.BlockSpec((ts, width), lambda i: (i, 0))
    blk3 = pl.BlockSpec((ts, heads, HEAD_B), lambda i: (i, 0, 0))
    return pl.pallas_call(
        body, name="rwkv_fwd", grid=(rows // ts,),
        in_specs=[blk, blk, blk, blk3, blk, blk],
        out_specs=[blk3, pl.BlockSpec((ts, HEAD_B, width), lambda i: (i, 0, 0)),
                   pl.BlockSpec((HEAD_B, width), lambda i: (0, 0))],
        out_shape=[jax.ShapeDtypeStruct((rows, heads, HEAD_B), F32), jax.ShapeDtypeStruct((rows, HEAD_B, width), F32),
                   jax.ShapeDtypeStruct((HEAD_B, width), F32)],
        compiler_params=_cp("arbitrary"),
    )(r, w, k, v, a, b)


def _rwkv_bwd(r, w, k, v, a, b, hist, last, dy):
    rows, width = r.shape
    heads = width // HEAD_B
    ts = SCAN_STEPS
    nb = rows // ts

    def body(r_ref, w_ref, k_ref, v_ref, a_ref, b_ref, hist_ref, last_ref, dy_ref,
             dr_ref, dw_ref, dk_ref, dv_ref, da_ref, db_ref, g_ref, after_ref):
        @pl.when(pl.program_id(0) == 0)
        def _():
            g_ref[...] = jnp.zeros_like(g_ref)
            after_ref[...] = last_ref[...]

        mask = _head_mask(heads, width)
        onehot = mask.astype(BF16)
        bd = _block_diag_ones()

        def own_lanes(x):
            return jnp.sum(jnp.where(mask, x, 0.0), axis=0, keepdims=True)

        def colsum(x):
            return jnp.sum(x, axis=0, keepdims=True)

        dy_m = [dy_ref[j].astype(BF16) for j in range(ts)]
        spread_dy = [_dot_tn(dy_m[j], onehot) for j in range(ts)]
        state_after = [hist_ref[j + 1] if j < ts - 1 else after_ref[...] for j in range(ts)]
        dr = [own_lanes(_dot(dy_m[j], state_after[j].astype(BF16))) for j in range(ts)]
        sa_m = [_dot_nt(_masked_rows(mask, a_ref[pl.ds(j, 1), :]), hist_ref[j].astype(BF16)) for j in range(ts)]
        g = g_ref[...]
        dw, dk, db, da, dv = {}, {}, {}, {}, {}
        for j in reversed(range(ts)):
            row = pl.ds(j, 1)
            sp = hist_ref[j]
            g = g + spread_dy[j] * r_ref[row, :]
            (dsa,) = _segsum_many([g * b_ref[row, :]], bd)
            g_b = g.astype(BF16)
            both = _dot(jnp.concatenate([v_ref[j].astype(BF16), sa_m[j].astype(BF16)], axis=0), g_b)
            dk[j], db[j] = own_lanes(both[:heads]), own_lanes(both[heads:])
            dv[j] = _dot_nt(_masked_rows(mask, k_ref[row, :]), g_b)
            dw[j] = colsum(g * sp)
            da[j] = colsum(sp * dsa)
            g = g * w_ref[row, :] + dsa * a_ref[row, :]
        g_ref[...] = g
        after_ref[...] = hist_ref[0]
        for j in range(ts):
            dv_ref[j] = dv[j]
            for ref, vals in ((dr_ref, dr), (dw_ref, dw), (dk_ref, dk), (da_ref, da), (db_ref, db)):
                ref[pl.ds(j, 1), :] = vals[j]

    blk = pl.BlockSpec((ts, width), lambda i: (nb - 1 - i, 0))
    blk3 = pl.BlockSpec((ts, heads, HEAD_B), lambda i: (nb - 1 - i, 0, 0))
    state = pl.BlockSpec((HEAD_B, width), lambda i: (0, 0))
    return pl.pallas_call(
        body, name="rwkv_bwd", grid=(nb,),
        in_specs=[blk, blk, blk, blk3, blk, blk, pl.BlockSpec((ts, HEAD_B, width), lambda i: (nb - 1 - i, 0, 0)),
                  state, blk3],
        out_specs=[blk, blk, blk, blk3, blk, blk],
        out_shape=[jax.ShapeDtypeStruct((rows, width), F32)] * 3 + [jax.ShapeDtypeStruct((rows, heads, HEAD_B), F32)]
        + [jax.ShapeDtypeStruct((rows, width), F32)] * 2,
        scratch_shapes=[pltpu.VMEM((HEAD_B, width), F32), pltpu.VMEM((HEAD_B, width), F32)],
        compiler_params=_cp("arbitrary"),
    )(r, w, k, v, a, b, hist, last, dy)


def _live(row0, shape):
    return (row0 + lax.broadcasted_iota(jnp.int32, shape, 0)) >= PAD


def _norm_fn(row0, h, gain):
    return (_rms(h, gain),)


def _norm_res_fn(row0, h, gain):
    return _rms(h, gain), h


def _make_bg_fn(heads):
    def fn(row0, x, log_rate, dt_bias):
        lane = lax.broadcasted_iota(jnp.int32, x.shape, 1)
        beta = _sigmoid(x)
        g = -jnp.exp(log_rate) * _softplus(x + dt_bias)
        out = jnp.where(lane < heads, beta, jnp.where(lane < 2 * heads, g, 0.0))
        return (jnp.where(_live(row0, x.shape), out, 0.0),)
    return fn


def _b_pre_fn(row0, zf, w0, w_up, a0, a_up, g_up, k_k, k_a):
    d = w0.shape[1]
    r, k, v = zf[:, :d], zf[:, d:2 * d], zf[:, 2 * d:3 * d]
    lo = zf[:, 3 * d:3 * d + 128]
    lg = zf[:, 3 * d + 128:3 * d + LORA_PAD]
    lane = lax.broadcasted_iota(jnp.int32, lo.shape, 1)
    lw = _dot(jnp.where(lane < LORA_W, jnp.tanh(lo), 0.0), w_up)
    la = _dot(jnp.where(lane >= LORA_W, lo, 0.0), a_up)
    lane_g = lax.broadcasted_iota(jnp.int32, lg.shape, 1)
    gate = _dot(jnp.where(lane_g < LORA_G, _sigmoid(lg), 0.0), g_up)
    decay = jnp.exp(-jnp.exp(-_softplus(-(w0 + lw)) - 0.5))
    a = _sigmoid(a0 + la)
    kx = k * k_k
    kk = kx * lax.rsqrt(_segsum64(kx * kx) + 1e-6)
    k2 = k * (1.0 + (a - 1.0) * k_a)
    return r, decay, k2, v, -kk, kk * a, gate


def _post_fn(row0, o, zg, y, r, k2, v, gate, out_gain, r_k, ln_g, ln_b):
    d = o.shape[1]
    az, ga, gb = zg[:, :d], zg[:, d:2 * d], zg[:, 2 * d:]
    heads = d // HEAD_A
    parts = []
    for h in range(heads):
        oh = o[:, h * HEAD_A:(h + 1) * HEAD_A]
        parts.append(oh * lax.rsqrt(jnp.mean(oh * oh, axis=-1, keepdims=True) + EPS) * out_gain)
    o_a = jnp.concatenate(parts, axis=1) * _silu(az)
    mean = _segsum64(y) * (1.0 / HEAD_B)
    yc = y - mean
    var = _segsum64(yc * yc) * (1.0 / HEAD_B)
    yn = yc * lax.rsqrt(var + GN_EPS) * ln_g + ln_b
    o_b = (yn + _segsum64(r * k2 * r_k) * v) * gate
    return (_sigmoid(ga) * o_a + _sigmoid(gb) * o_b,)


def _loss(h3, target, gain, tb):
    rows, d = h3.shape

    def body(h_ref, t_ref, g_ref, dh_ref, dg_ref, l_ref):
        i = pl.program_id(0)
        live = (i * tb + lax.broadcasted_iota(jnp.int32, (tb, 1), 0)) >= CHUNK
        tgt = t_ref[...]

        def f(h, g):
            err = _rms(h, g) - tgt
            return 0.5 * jnp.sum(jnp.where(live, jnp.mean(err * err, axis=-1, keepdims=True), 0.0))

        val, vjp = jax.vjp(f, h_ref[...], g_ref[...])
        dh, dg = vjp(jnp.ones((), F32))
        dh_ref[...] = dh

        @pl.when(i == 0)
        def _():
            dg_ref[...] = jnp.zeros_like(dg_ref)
            l_ref[...] = jnp.zeros_like(l_ref)

        dg_ref[...] += dg
        l_ref[...] += jnp.full((1, 128), val, F32)

    blk = pl.BlockSpec((tb, d), lambda i: (i, 0))
    return pl.pallas_call(
        body, name="loss", grid=(rows // tb,),
        in_specs=[blk, blk, pl.BlockSpec((1, d), lambda i: (0, 0))],
        out_specs=[blk, pl.BlockSpec((1, d), lambda i: (0, 0)), pl.BlockSpec((1, 128), lambda i: (0, 0))],
        out_shape=[jax.ShapeDtypeStruct((rows, d), F32), jax.ShapeDtypeStruct((1, d), F32),
                   jax.ShapeDtypeStruct((1, 128), F32)],
        compiler_params=_cp("arbitrary"),
    )(h3, target, gain)


def _adamw_math(w, g, m, v):
    m2 = ADAM_B1 * m + (1.0 - ADAM_B1) * g
    v2 = ADAM_B2 * v + (1.0 - ADAM_B2) * (g * g)
    m_hat = m2 / (1.0 - ADAM_B1 ** ADAM_STEP)
    v_hat = v2 / (1.0 - ADAM_B2 ** ADAM_STEP)
    return -ADAM_LR * (m_hat / (jnp.sqrt(v_hat) + ADAM_EPS) + ADAM_WD * w), m2, v2


def _adamw(name, slabs, w, m, v, rb):
    rows, cols = w.shape

    def body(s_ref, w_ref, m_ref, v_ref, g_ref, d_ref, m2_ref, v2_ref):
        g = s_ref[0].astype(F32)
        for dev in range(1, N_DEV):
            g = g + s_ref[dev].astype(F32)
        g_ref[...] = g
        d_ref[...], m2_ref[...], v2_ref[...] = _adamw_math(w_ref[...], g, m_ref[...], v_ref[...])

    blk = pl.BlockSpec((rb, cols), lambda i: (i, 0))
    return pl.pallas_call(
        body, name=name, grid=(rows // rb,),
        in_specs=[pl.BlockSpec((N_DEV, rb, cols), lambda i: (0, i, 0)), blk, blk, blk],
        out_specs=[blk] * 4, out_shape=[jax.ShapeDtypeStruct((rows, cols), F32)] * 4,
        compiler_params=_cp("parallel"),
    )(slabs, w, m, v)


def _sum_slabs(name, slabs, rb):
    _, rows, cols = slabs.shape

    def body(s_ref, o_ref):
        g = s_ref[0]
        for dev in range(1, N_DEV):
            g = g + s_ref[dev]
        o_ref[...] = g

    return pl.pallas_call(
        body, name=name, grid=(rows // rb,),
        in_specs=[pl.BlockSpec((N_DEV, rb, cols), lambda i: (0, i, 0))],
        out_specs=pl.BlockSpec((rb, cols), lambda i: (i, 0)),
        out_shape=jax.ShapeDtypeStruct((rows, cols), F32), compiler_params=_cp("parallel"),
    )(slabs)


def _adamw_small(w, g, m, v):
    def body(w_ref, g_ref, m_ref, v_ref, d_ref, m2_ref, v2_ref):
        d_ref[...], m2_ref[...], v2_ref[...] = _adamw_math(w_ref[...], g_ref[...], m_ref[...], v_ref[...])

    return pl.pallas_call(body, name="adamw_small", out_shape=[jax.ShapeDtypeStruct(w.shape, F32)] * 3)(w, g, m, v)


def _place():
    return lax.axis_index("x"), lax.axis_index("y"), lax.axis_index("c")


def _index(p):
    return 4 * p[0] + 2 * p[1] + p[2]


def _all_gather(name, xs):
    n = len(xs)

    def body(*refs):
        x_refs, o_refs = refs[:n], refs[n:2 * n]
        send_sems, recv_sems, local_sems = refs[2 * n:]
        x, y, c = _place()
        me, sibling = (x, y, c), (x, y, 1 - c)
        chips = [(1 - x, y), (x, 1 - y), (1 - x, 1 - y)]

        def copy(i, k, block, to, src=None):
            dst = o_refs[i].at[_index(block)]
            return pltpu.make_async_remote_copy(src_ref=dst if src is None else src, dst_ref=dst,
                                                send_sem=send_sems.at[i, k], recv_sem=recv_sems.at[i, k],
                                                device_id=to, device_id_type=MESH_ID)

        mine = [pltpu.make_async_copy(x_refs[i], o_refs[i].at[_index(me)], local_sems.at[i]) for i in range(n)]
        for cp in mine:
            cp.start()
        first = []
        for i in range(n):
            first.append(copy(i, 0, me, sibling, src=x_refs[i]))
            first += [copy(i, 1 + j, me, (*chip, c), src=x_refs[i]) for j, chip in enumerate(chips)]
        for cp in first:
            cp.start()
        passed = []
        for j, chip in enumerate(chips):
            for i in range(n):
                copy(i, 1 + j, (*chip, c), me).wait_recv()
                cp = copy(i, 4 + j, (*chip, c), sibling)
                cp.start()
                passed.append(cp)
        for i in range(n):
            copy(i, 0, sibling, me).wait_recv()
            for j, chip in enumerate(chips):
                copy(i, 4 + j, (*chip, 1 - c), me).wait_recv()
        for cp in first + passed:
            cp.wait_send()
        for cp in mine:
            cp.wait()

    return pl.pallas_call(
        body, name=name, in_specs=[ANY] * n, out_specs=[ANY] * n,
        out_shape=[jax.ShapeDtypeStruct((N_DEV,) + x.shape, x.dtype) for x in xs],
        scratch_shapes=[pltpu.SemaphoreType.DMA((n, 7)), pltpu.SemaphoreType.DMA((n, 7)), pltpu.SemaphoreType.DMA((n,))],
    )(*xs)


def _exchange(name, xs):
    n = len(xs)

    def body(*refs):
        x_refs, o_refs = refs[:n], refs[n:2 * n]
        send_sems, recv_sems, local_sems = refs[2 * n:]
        x, y, c = _place()
        me = (x, y, c)
        peers = [((1 - x) if k & 4 else x, (1 - y) if k & 2 else y, (1 - c) if k & 1 else c) for k in range(1, N_DEV)]

        def copy(i, k, peer):
            return pltpu.make_async_remote_copy(src_ref=x_refs[i].at[_index(peer)], dst_ref=o_refs[i].at[_index(me)],
                                                send_sem=send_sems.at[i, k], recv_sem=recv_sems.at[i, k],
                                                device_id=peer, device_id_type=MESH_ID)

        def arrival(i, k, peer):
            return pltpu.make_async_remote_copy(src_ref=x_refs[i].at[_index(peer)], dst_ref=o_refs[i].at[_index(peer)],
                                                send_sem=send_sems.at[i, k], recv_sem=recv_sems.at[i, k],
                                                device_id=peer, device_id_type=MESH_ID)

        mine = [pltpu.make_async_copy(x_refs[i].at[_index(me)], o_refs[i].at[_index(me)], local_sems.at[i])
                for i in range(n)]
        for cp in mine:
            cp.start()
        sent = [copy(i, k, peer) for k, peer in enumerate(peers) for i in range(n)]
        for cp in sent:
            cp.start()
        for k, peer in enumerate(peers):
            for i in range(n):
                arrival(i, k, peer).wait_recv()
        for cp in sent:
            cp.wait_send()
        for cp in mine:
            cp.wait()

    return pl.pallas_call(
        body, name=name, in_specs=[ANY] * n, out_specs=[ANY] * n,
        out_shape=[jax.ShapeDtypeStruct(x.shape, x.dtype) for x in xs],
        scratch_shapes=[pltpu.SemaphoreType.DMA((n, 7)), pltpu.SemaphoreType.DMA((n, 7)), pltpu.SemaphoreType.DMA((n,))],
    )(*xs)


def _pack(arrays):
    flat = jnp.concatenate([a.reshape(-1) for a in arrays])
    pad = (-flat.shape[0]) % 1024
    return jnp.pad(flat, (0, pad)).reshape(-1, 128)


def _unpack(packed, shapes):
    flat = packed.reshape(-1)
    out, pos = [], 0
    for s in shapes:
        size = 1
        for dim in s:
            size *= dim
        out.append(flat[pos:pos + size].reshape(s))
        pos += size
    return out


def _cols_from_slabs(stack):
    return jnp.transpose(stack, (1, 0, 2)).reshape(stack.shape[1], -1)


def _cols_to_slabs(full):
    return jnp.transpose(full.reshape(full.shape[0], N_DEV, -1), (1, 0, 2))


def kernel(x, meta_tokens, ffn1_norm, ffn1_w_gu, ffn1_w_down, mix_norm, w_in, a_conv_w, a_log_rate, a_dt_bias, a_out_norm, b_shift_mu, b_w0, b_w_up, b_a0, b_a_up, b_g_up, b_k_k, b_k_a, b_r_k, b_ln_gain, b_ln_bias, w_out, ffn2_norm, ffn2_w_gu, ffn2_w_down, final_norm, loss_target, m_meta_tokens, m_ffn1_norm, m_ffn1_w_gu, m_ffn1_w_down, m_mix_norm, m_w_in, m_a_conv_w, m_a_log_rate, m_a_dt_bias, m_a_out_norm, m_b_shift_mu, m_b_w0, m_b_w_up, m_b_a0, m_b_a_up, m_b_g_up, m_b_k_k, m_b_k_a, m_b_r_k, m_b_ln_gain, m_b_ln_bias, m_w_out, m_ffn2_norm, m_ffn2_w_gu, m_ffn2_w_down, m_final_norm, v_meta_tokens, v_ffn1_norm, v_ffn1_w_gu, v_ffn1_w_down, v_mix_norm, v_w_in, v_a_conv_w, v_a_log_rate, v_a_dt_bias, v_a_out_norm, v_b_shift_mu, v_b_w0, v_b_w_up, v_b_a0, v_b_a_up, v_b_g_up, v_b_k_k, v_b_k_a, v_b_r_k, v_b_ln_gain, v_b_ln_bias, v_w_out, v_ffn2_norm, v_ffn2_w_gu, v_ffn2_w_down, v_final_norm):
    names = ['meta_tokens', 'ffn1_norm', 'ffn1_w_gu', 'ffn1_w_down', 'mix_norm', 'w_in', 'a_conv_w', 'a_log_rate',
             'a_dt_bias', 'a_out_norm', 'b_shift_mu', 'b_w0', 'b_w_up', 'b_a0', 'b_a_up', 'b_g_up', 'b_k_k', 'b_k_a',
             'b_r_k', 'b_ln_gain', 'b_ln_bias', 'w_out', 'ffn2_norm', 'ffn2_w_gu', 'ffn2_w_down', 'final_norm']
    env = dict(locals())
    wts = {k: env[k] for k in names}
    mom_m = {k: env['m_' + k] for k in names}
    mom_v = {k: env['v_' + k] for k in names}
    big = ['ffn1_w_gu', 'ffn1_w_down', 'w_in', 'w_out', 'ffn2_w_gu', 'ffn2_w_down']
    col_sharded = {'ffn1_w_gu', 'w_in', 'ffn2_w_gu'}
    small_sharded = ['meta_tokens', 'a_conv_w', 'b_w_up', 'b_a_up', 'b_g_up']
    replicated = [k for k in names if k not in big and k not in small_sharded]

    seq, d = x.shape[1], x.shape[2]
    rows = PAD + N_META + seq
    heads_a = d // HEAD_A
    tb_mm = _tb(rows, 416)
    tb_vjp = _tb(rows, 208)
    me = _index(_place())

    big_local = [wts[k][0].astype(BF16) for k in big]
    small_local = [wts['meta_tokens']] + [wts[k][0] for k in small_sharded[1:]]
    gathered = _all_gather("gather_weights", big_local + small_local)
    gw = dict(zip(big + small_sharded, gathered))
    full = {k: (gw[k] if k in ('ffn1_w_gu', 'ffn2_w_gu') else _cols_from_slabs(gw[k]) if k in col_sharded
                else gw[k].reshape(-1, gw[k].shape[-1])) for k in big}
    for k in small_sharded:
        full[k] = _cols_from_slabs(gw[k])
    for k in replicated:
        full[k] = wts[k].reshape(1, -1)

    win = full['w_in']
    n_b = 3 * d + LORA_W + LORA_A + LORA_G
    off_beta, off_b = 4 * d, 4 * d + 2 * heads_a
    off_ga = off_b + n_b
    b_width = 3 * d + LORA_PAD
    zcols = lambda r, c: jnp.zeros((r, c), BF16)
    w_qkv = win[:, :3 * d]
    w_zg = jnp.concatenate([win[:, 3 * d:4 * d], win[:, off_ga:off_ga + 2 * d]], axis=1)
    w_b = jnp.concatenate([win[:, off_b:off_b + n_b], zcols(d, b_width - n_b)], axis=1)
    w_bg = jnp.concatenate([win[:, off_beta:off_beta + 2 * heads_a], zcols(d, 128 - 2 * heads_a)], axis=1)

    def lanes(vec, start, width):
        return jnp.pad(vec.reshape(1, -1), ((0, 0), (start, width - start - vec.size)))

    log_rate = lanes(wts['a_log_rate'], heads_a, 128)
    dt_bias = lanes(wts['a_dt_bias'], heads_a, 128)
    mu = lanes(wts['b_shift_mu'], 0, b_width)
    w_up = jnp.pad(full['b_w_up'], ((0, 128 - LORA_W), (0, 0)))
    a_up = jnp.pad(full['b_a_up'], ((LORA_W, 0), (0, 0)))
    g_up = jnp.pad(full['b_g_up'], ((0, 256 - LORA_G), (0, 0)))
    b_pars = [full['b_w0'], w_up, full['b_a0'], a_up, g_up, full['b_k_k'], full['b_k_a']]
    post_pars = [full['a_out_norm'], full['b_r_k'], full['b_ln_gain'], full['b_ln_bias']]
    bg_fn = _make_bg_fn(heads_a)

    h0 = jnp.concatenate([jnp.zeros((PAD, d), F32), full['meta_tokens'], x[0]], axis=0)
    h1 = _ffn_fwd("ffn1_fwd", h0, full['ffn1_norm'], full['ffn1_w_gu'], full['ffn1_w_down'], tb_mm)
    (u,) = _tok_fwd("mix_norm_fwd", _norm_fn, [h1], [full['mix_norm']], [(d, BF16)], tb_mm)
    z_qkv = _mm("in_qkv", u, w_qkv, tb=tb_mm, tn=_col_tile(3 * d, 1536))
    z_zg = _mm("in_zg", u, w_zg, tb=tb_mm, tn=_col_tile(3 * d, 1536))
    z_b = _mm("in_b", u, w_b, tb=tb_mm, tn=_col_tile(b_width, 1536))
    z_bg = _mm("in_bg", u, w_bg, tb=tb_mm, tn=128)
    qkv = _a_pre_fwd(z_qkv, full['a_conv_w'])
    (bg,) = _tok_fwd("bg_fwd", bg_fn, [z_bg], [log_rate, dt_bias], [(128, F32)], tb_mm)
    o_dn, dn_hist = _dn_fwd(qkv, bg)
    zf = _shift_fwd(z_b, mu)
    rr, ww, kk2, vv, av, bv, gate = _tok_fwd("b_pre_fwd", _b_pre_fn, [zf], b_pars, [(d, F32)] * 7, tb_vjp)
    per_head = lambda t: t.reshape(rows, d // HEAD_B, HEAD_B)
    y_heads, b_hist, b_last = _rwkv_fwd(rr, ww, kk2, per_head(vv), av, bv)
    y_b = y_heads.reshape(rows, d)
    post_toks = [o_dn, z_zg, y_b, rr, kk2, vv, gate]
    (merged,) = _tok_fwd("post_fwd", _post_fn, post_toks, post_pars, [(d, BF16)], tb_vjp)
    h2 = _mm("out_proj", merged, full['w_out'], add=h1, tb=tb_mm, tn=d)
    h3 = _ffn_fwd("ffn2_fwd", h2, full['ffn2_norm'], full['ffn2_w_gu'], full['ffn2_w_down'], tb_mm)

    target = jnp.pad(loss_target[0], ((CHUNK, 0), (0, 0)))
    dh3, g_final, loss_part = _loss(h3, target, full['final_norm'].reshape(1, d), tb_vjp)

    def ffn_backward(tag, h, dout, key_norm, key_gu, key_down):
        dh, dh_bf, dgain, xn, act, dgate, dup, dhalf = _ffn_bwd(tag + "_bwd", h, full[key_norm], dout, full[key_gu],
                                                                full[key_down], tb_mm)
        d_gu = jnp.concatenate([_mm_tn_to_slabs(tag + "_dw_gate", xn, dgate, tk=tb_mm),
                                _mm_tn_to_slabs(tag + "_dw_up", xn, dup, tk=tb_mm)], axis=0)
        d_down = _mm_tn_from_slabs(tag + "_dw_down", act, dhalf, tk=tb_mm).reshape(N_DEV, -1, d)
        return dh, dh_bf, dgain, d_gu, d_down

    dh2, dh2_bf, g_ffn2_norm, g_ffn2_gu, g_ffn2_down = ffn_backward("ffn2", h2, dh3, 'ffn2_norm', 'ffn2_w_gu',
                                                                    'ffn2_w_down')
    g_w_out = _mm_tn("dw_out", merged, dh2_bf, tm=d, tn=d, tk=tb_mm).reshape(N_DEV, -1, d)
    dmerged = _mm("d_merged", dh2_bf, full['w_out'], trans_b=True, tb=tb_mm, tn=d)
    post_grads = _tok_bwd("post_bwd", _post_fn, post_toks, post_pars, [[dmerged]], list(range(7)), tb_vjp,
                          [F32, BF16] + [F32] * 5)
    do_dn, dz_zg, dy_b, dr1, dk1, dv1, dgate = post_grads[:7]
    g_out_norm, g_r_k, g_ln_g, g_ln_b = post_grads[7:]
    dr2, dw2, dk2, dv_heads, da2, db2 = _rwkv_bwd(rr, ww, kk2, per_head(vv), av, bv, b_hist, b_last, per_head(dy_b))
    dv2 = dv_heads.reshape(rows, d)
    b_grads = _tok_bwd("b_pre_bwd", _b_pre_fn, [zf], b_pars,
                       [[dr1, dr2], [dw2], [dk1, dk2], [dv1, dv2], [da2], [db2], [dgate]], [0], tb_vjp)
    dzf = b_grads[0]
    g_w0, g_w_up, g_a0, g_a_up, g_g_up, g_k_k, g_k_a = b_grads[1:]
    dz_b, g_mu = _shift_bwd(z_b, mu, dzf)
    dqkv, dbg = _dn_bwd(qkv, bg, dn_hist, do_dn)
    dz_qkv, g_conv = _a_pre_bwd(z_qkv, full['a_conv_w'], dqkv)
    dz_bg, g_log_rate, g_dt_bias = _tok_bwd("bg_bwd", bg_fn, [z_bg], [log_rate, dt_bias], [[dbg]], [0], tb_mm, [BF16])

    du = None
    g_w_in_parts = []
    for tag, dz, wpiece in (("qkv", dz_qkv, w_qkv), ("zg", dz_zg, w_zg), ("b", dz_b, w_b), ("bg", dz_bg, w_bg)):
        du = _mm("du_" + tag, dz, wpiece, trans_b=True, add=du, tb=tb_mm, tn=d)
        g_w_in_parts.append(_mm_tn("dw_in_" + tag, u, dz, tm=d, tn=_col_tile(dz.shape[1], 1536), tk=tb_mm))
    dh1, g_mix_norm = _tok_bwd("mix_norm_bwd", _norm_res_fn, [h1], [full['mix_norm']], [[du], [dh2]], [0], tb_vjp)
    dh0, _, g_ffn1_norm, g_ffn1_gu, g_ffn1_down = ffn_backward("ffn1", h0, dh1, 'ffn1_norm', 'ffn1_w_gu', 'ffn1_w_down')

    gp_qkv, gp_zg, gp_b, gp_bg = g_w_in_parts
    g_w_in = _cols_to_slabs(jnp.concatenate(
        [gp_qkv, gp_zg[:, :d], gp_bg[:, :2 * heads_a], gp_b[:, :n_b], gp_zg[:, d:]], axis=1))

    slabs = {'ffn1_w_gu': g_ffn1_gu, 'ffn1_w_down': g_ffn1_down, 'w_in': g_w_in, 'w_out': g_w_out,
             'ffn2_w_gu': g_ffn2_gu, 'ffn2_w_down': g_ffn2_down}
    received = dict(zip(big, _exchange("exchange_grads", [slabs[k] for k in big])))

    small_full = {
        'meta_tokens': dh0[PAD:CHUNK], 'ffn1_norm': g_ffn1_norm, 'mix_norm': g_mix_norm, 'a_conv_w': g_conv,
        'a_log_rate': g_log_rate[:, heads_a:2 * heads_a], 'a_dt_bias': g_dt_bias[:, heads_a:2 * heads_a],
        'a_out_norm': g_out_norm, 'b_shift_mu': g_mu[:, :n_b], 'b_w0': g_w0, 'b_w_up': g_w_up[:LORA_W],
        'b_a0': g_a0, 'b_a_up': g_a_up[LORA_W:], 'b_g_up': g_g_up[:LORA_G], 'b_k_k': g_k_k, 'b_k_a': g_k_a,
        'b_r_k': g_r_k, 'b_ln_gain': g_ln_g, 'b_ln_bias': g_ln_b, 'ffn2_norm': g_ffn2_norm, 'final_norm': g_final,
    }
    small_names = list(small_full)
    packed = _pack([small_full[k] for k in small_names] + [loss_part[:, :1]])
    (all_parts,) = _all_gather("gather_small_grads", [packed])
    summed = _sum_slabs("sum_small_grads", all_parts, packed.shape[0])
    pieces = _unpack(summed, [small_full[k].shape for k in small_names] + [(1, 1)])
    small_grad = dict(zip(small_names, pieces[:-1]))
    loss = pieces[-1].reshape(())

    grads, deltas, new_m, new_v = {}, {}, {}, {}
    for k in big:
        shard = wts[k][0]
        rb = _tb(shard.shape[0], 128) if shard.shape[0] % 16 == 0 else shard.shape[0]
        g, dl, m2, v2 = _adamw("adamw_" + k, received[k], shard, mom_m[k][0], mom_v[k][0], rb)
        grads[k], deltas[k], new_m[k], new_v[k] = g[None], dl[None], m2[None], v2[None]

    local_small = {}
    for k in small_names:
        g = small_grad[k]
        if k in small_sharded:
            width = wts[k].shape[-1]
            g = lax.dynamic_slice_in_dim(g, me * width, width, axis=1)
        local_small[k] = g.reshape(wts[k].shape)
    pk = lambda tree: _pack([tree[k] for k in small_names])
    dl_s, m_s, v_s = _adamw_small(pk(wts), pk(local_small), pk(mom_m), pk(mom_v))
    shapes = [wts[k].shape for k in small_names]
    for k, dl, m2, v2 in zip(small_names, _unpack(dl_s, shapes), _unpack(m_s, shapes), _unpack(v_s, shapes)):
        grads[k], deltas[k], new_m[k], new_v[k] = local_small[k], dl, m2, v2

    grad_x = dh0[CHUNK:][None]
    return (loss, grad_x, *[grads[k] for k in names], *[deltas[k] for k in names],
            *[new_m[k] for k in names], *[new_v[k] for k in names])
```

```python
import functools

import jax
import jax.numpy as jnp
from jax import lax
from jax.experimental import pallas as pl
from jax.experimental.pallas import tpu as pltpu

F32 = jnp.float32
BF16 = jnp.bfloat16
N_DEV = 8
N_META = 16
CHUNK = 64
PAD = CHUNK - N_META
HEAD_A = 128
HEAD_B = 64
LORA_W, LORA_A, LORA_G = 64, 64, 160
LORA_PAD = 384
EPS = 1e-6
GN_EPS = HEAD_B * 1e-5
ADAM_LR, ADAM_B1, ADAM_B2, ADAM_EPS, ADAM_WD, ADAM_STEP = 0.001, 0.9, 0.999, 1e-08, 0.01, 10
SCAN_STEPS = 16
MXU_WIDTH = 256
VMEM_LIMIT = 56 * 1024 * 1024
DN_PRECISION = lax.Precision.HIGH
MESH_ID = pl.DeviceIdType.MESH
ANY = pl.BlockSpec(memory_space=pl.ANY)
HBM_SPEC = pl.BlockSpec(memory_space=pltpu.HBM)
SEM_SPEC = pl.BlockSpec(memory_space=pltpu.SEMAPHORE)


def _cp(*sem):
    return pltpu.CompilerParams(dimension_semantics=sem, vmem_limit_bytes=VMEM_LIMIT)


def _tb(t, target):
    best = 16
    for d in range(16, target + 1, 16):
        if t % d == 0:
            best = d
    return best


def _sigmoid(x):
    return 1.0 / (1.0 + jnp.exp(-x))


def _silu(x):
    return x * _sigmoid(x)


def _softplus(x):
    return jnp.maximum(x, 0.0) + jnp.log(1.0 + jnp.exp(-jnp.abs(x)))


def _dot_nt(a, b, precision=None):
    return lax.dot_general(a, b, (((1,), (1,)), ((), ())), preferred_element_type=F32, precision=precision)


def _dot_tn(a, b, precision=None):
    return lax.dot_general(a, b, (((0,), (0,)), ((), ())), preferred_element_type=F32, precision=precision)


def _dot(a, b, precision=None):
    return jnp.dot(a, b, preferred_element_type=F32, precision=precision)


def _block_diag_ones():
    i = lax.broadcasted_iota(jnp.int32, (MXU_WIDTH, MXU_WIDTH), 0) // HEAD_B
    j = lax.broadcasted_iota(jnp.int32, (MXU_WIDTH, MXU_WIDTH), 1) // HEAD_B
    return (i == j).astype(BF16)


def _hi_lo(x):
    hi = x.astype(BF16)
    return hi, (x - hi.astype(F32)).astype(BF16)


def _segsum_many(xs, bd):
    groups = [x if isinstance(x, tuple) else _hi_lo(x) for x in xs]
    rows = groups[0][0].shape[0]
    stacked = jnp.concatenate([p for grp in groups for p in grp], axis=0)
    out = jnp.concatenate([_dot(stacked[:, s:s + MXU_WIDTH], bd) for s in range(0, stacked.shape[1], MXU_WIDTH)], axis=1)
    res, pos = [], 0
    for grp in groups:
        acc = out[pos:pos + rows]
        for j in range(1, len(grp)):
            acc = acc + out[pos + j * rows:pos + (j + 1) * rows]
        res.append(acc)
        pos += len(grp) * rows
    return res


def _segsum_impl(x):
    return _segsum_many([x], _block_diag_ones())[0]


@jax.custom_vjp
def _segsum64(x):
    return _segsum_impl(x)


_segsum64.defvjp(lambda x: (_segsum_impl(x), None), lambda _, ct: (_segsum_impl(ct),))


def _tok(t):
    return t if isinstance(t, tuple) else (t, t.shape[1], 0)


def _tok_spec(tb, width, colblk):
    return pl.BlockSpec((tb, width), lambda i: (i, colblk))


def _par_spec(p):
    return pl.BlockSpec(p.shape, lambda i: (0, 0))


def _tok_fwd(name, fn, toks, pars, outs, tb):
    toks = [_tok(t) for t in toks]
    rows = toks[0][0].shape[0]
    n_in = len(toks) + len(pars)

    def body(*refs):
        row0 = pl.program_id(0) * tb
        res = fn(row0, *[r[...] for r in refs[:n_in]])
        for r, o in zip(refs[n_in:], res):
            r[...] = o.astype(r.dtype)

    return pl.pallas_call(
        body, name=name, grid=(rows // tb,),
        in_specs=[_tok_spec(tb, w, c) for _, w, c in toks] + [_par_spec(p) for p in pars],
        out_specs=[_tok_spec(tb, w, 0) for w, _ in outs],
        out_shape=[jax.ShapeDtypeStruct((rows, w), dt) for w, dt in outs],
        compiler_params=_cp("parallel"),
    )(*[a for a, _, _ in toks], *pars)


def _tok_bwd(name, fn, toks, pars, cts, want, tb, want_dtypes=None, after=None):
    toks = [_tok(t) for t in toks]
    want_dtypes = want_dtypes or [F32] * len(want)
    cts = [[_tok(c) for c in group] for group in cts]
    flat_cts = [c for group in cts for c in group]
    rows = toks[0][0].shape[0]
    n_tok, n_par, n_ct = len(toks), len(pars), len(flat_cts)
    extra = [] if after is None else [after]

    def body(*refs):
        i = pl.program_id(0)
        row0 = i * tb
        prim = [r[...].astype(F32) for r in refs[:n_tok + n_par]]
        ct_refs = list(refs[n_tok + n_par:n_tok + n_par + n_ct])
        out_refs = refs[n_tok + n_par + n_ct + len(extra):]
        res, vjp = jax.vjp(lambda *a: fn(row0, *a), *prim)
        ct = []
        for group, o in zip(cts, res):
            acc = None
            for _ in group:
                v = ct_refs.pop(0)[...].astype(F32)
                acc = v if acc is None else acc + v
            ct.append(acc.astype(o.dtype))
        grads = vjp(tuple(ct))
        for r, k in zip(out_refs[:len(want)], want):
            r[...] = grads[k].astype(r.dtype)

        @pl.when(i == 0)
        def _():
            for r in out_refs[len(want):]:
                r[...] = jnp.zeros_like(r)

        for r, g in zip(out_refs[len(want):], grads[n_tok:]):
            r[...] += g

    return pl.pallas_call(
        body, name=name, grid=(rows // tb,),
        in_specs=[_tok_spec(tb, w, c) for _, w, c in toks] + [_par_spec(p) for p in pars]
        + [_tok_spec(tb, w, c) for _, w, c in flat_cts] + [ANY] * len(extra),
        out_specs=[_tok_spec(tb, toks[k][1], 0) for k in want] + [_par_spec(p) for p in pars],
        out_shape=[jax.ShapeDtypeStruct((rows, toks[k][1]), dt) for k, dt in zip(want, want_dtypes)]
        + [jax.ShapeDtypeStruct(p.shape, F32) for p in pars],
        compiler_params=_cp("arbitrary"),
    )(*[a for a, _, _ in toks], *pars, *[a for a, _, _ in flat_cts], *extra)


def _mm(name, a, b, *, trans_b=False, add=None, after=None, tb, tn):
    rows, k = a.shape
    n = b.shape[0] if trans_b else b.shape[1]

    def body(*refs):
        a_ref, b_ref = refs[:2]
        o_ref = refs[-1]
        acc = _dot_nt(a_ref[...], b_ref[...]) if trans_b else _dot(a_ref[...], b_ref[...])
        if add is not None:
            acc = acc + refs[2][...]
        o_ref[...] = acc

    in_specs = [pl.BlockSpec((tb, k), lambda i, j: (i, 0)),
                pl.BlockSpec((tn, k), lambda i, j: (j, 0)) if trans_b else pl.BlockSpec((k, tn), lambda i, j: (0, j))]
    args = [a, b]
    if add is not None:
        in_specs.append(pl.BlockSpec((tb, tn), lambda i, j: (i, j)))
        args.append(add)
    if after is not None:
        in_specs.append(ANY)
        args.append(after)
    return pl.pallas_call(
        body, name=name, grid=(rows // tb, n // tn), in_specs=in_specs,
        out_specs=pl.BlockSpec((tb, tn), lambda i, j: (i, j)),
        out_shape=jax.ShapeDtypeStruct((rows, n), F32),
        compiler_params=_cp("parallel", "parallel"),
    )(*args)


def _mm_tn_call(name, grid, a, b, a_spec, b_spec, o_spec, acc_shape, out_shape):
    last = len(grid) - 1

    def body(a_ref, b_ref, o_ref, acc_ref):
        k = pl.program_id(last)

        @pl.when(k == 0)
        def _():
            acc_ref[...] = jnp.zeros_like(acc_ref)

        a_blk = a_ref[0] if len(a_ref.shape) == 3 else a_ref[...]
        b_blk = b_ref[0] if len(b_ref.shape) == 3 else b_ref[...]
        acc_ref[...] += _dot_tn(a_blk, b_blk)

        @pl.when(k == grid[last] - 1)
        def _():
            if len(o_ref.shape) == 3:
                o_ref[0] = acc_ref[...].astype(o_ref.dtype)
            else:
                o_ref[...] = acc_ref[...].astype(o_ref.dtype)

    return pl.pallas_call(
        body, name=name, grid=grid, in_specs=[a_spec, b_spec], out_specs=o_spec,
        out_shape=jax.ShapeDtypeStruct(out_shape, BF16), scratch_shapes=[pltpu.VMEM(acc_shape, F32)],
        compiler_params=_cp(*(["parallel"] * last + ["arbitrary"])),
    )(a, b)


def _mm_tn(name, a, b, *, tm, tn, tk):
    rows, m = a.shape
    n = b.shape[1]
    return _mm_tn_call(name, (m // tm, n // tn, rows // tk), a, b,
                       pl.BlockSpec((tk, tm), lambda i, j, k: (k, i)), pl.BlockSpec((tk, tn), lambda i, j, k: (k, j)),
                       pl.BlockSpec((tm, tn), lambda i, j, k: (i, j)), (tm, tn), (m, n))


def _mm_tn_to_slabs(name, a, b3, *, tk):
    rows, m = a.shape
    s, _, c = b3.shape
    return _mm_tn_call(name, (s, rows // tk), a, b3,
                       pl.BlockSpec((tk, m), lambda i, k: (k, 0)), pl.BlockSpec((1, tk, c), lambda i, k: (i, k, 0)),
                       pl.BlockSpec((1, m, c), lambda i, k: (i, 0, 0)), (m, c), (s, m, c))


def _mm_tn_from_slabs(name, a3, b, *, tk):
    s, rows, c = a3.shape
    n = b.shape[1]
    return _mm_tn_call(name, (s, rows // tk), a3, b,
                       pl.BlockSpec((1, tk, c), lambda i, k: (i, k, 0)), pl.BlockSpec((tk, n), lambda i, k: (k, 0)),
                       pl.BlockSpec((c, n), lambda i, k: (i, 0)), (c, n), (s * c, n))


def _col_tile(n, target):
    if n <= target:
        return n
    best = 128
    for d in range(128, target + 1, 128):
        if n % d == 0:
            best = d
    return best


def _rms(x, gain):
    return x * lax.rsqrt(jnp.mean(x * x, axis=-1, keepdims=True) + EPS) * gain


def _ffn_specs(d, fc, nj):
    return [pl.BlockSpec((1, d, fc), lambda i, j: (j, 0, 0)), pl.BlockSpec((1, d, fc), lambda i, j: (nj + j, 0, 0)),
            pl.BlockSpec((fc, d), lambda i, j: (j, 0))]


def _ffn_fwd(name, h, gain, wgu, wd, tb):
    rows, d = h.shape
    nj = wgu.shape[0] // 2
    fc = wgu.shape[2]

    def body(h_ref, g_ref, wg_ref, wu_ref, wd_ref, o_ref, xn_s, acc_s):
        j = pl.program_id(1)

        @pl.when(j == 0)
        def _():
            xn_s[...] = _rms(h_ref[...], g_ref[...]).astype(BF16)
            acc_s[...] = jnp.zeros_like(acc_s)

        xn = xn_s[...]
        gate = _dot(xn, wg_ref[0])
        up = _dot(xn, wu_ref[0])
        acc_s[...] += _dot((_silu(gate) * up).astype(BF16), wd_ref[...])

        @pl.when(j == nj - 1)
        def _():
            o_ref[...] = h_ref[...] + 0.5 * acc_s[...]

    return pl.pallas_call(
        body, name=name, grid=(rows // tb, nj),
        in_specs=[pl.BlockSpec((tb, d), lambda i, j: (i, 0)), pl.BlockSpec((1, d), lambda i, j: (0, 0))]
        + _ffn_specs(d, fc, nj),
        out_specs=pl.BlockSpec((tb, d), lambda i, j: (i, 0)),
        out_shape=jax.ShapeDtypeStruct((rows, d), F32),
        scratch_shapes=[pltpu.VMEM((tb, d), BF16), pltpu.VMEM((tb, d), F32)],
        compiler_params=_cp("parallel", "arbitrary"),
    )(h, gain, wgu, wgu, wd)


def _ffn_bwd(name, h, gain, dout, wgu, wd, tb):
    rows, d = h.shape
    nj = wgu.shape[0] // 2
    fc = wgu.shape[2]

    def body(h_ref, g_ref, do_ref, wg_ref, wu_ref, wd_ref,
             dh_ref, dhb_ref, dg_ref, xn_ref, act_ref, dgate_ref, dup_ref, dhalf_ref, dxn_s):
        i, j = pl.program_id(0), pl.program_id(1)

        @pl.when(j == 0)
        def _():
            xn_ref[...] = _rms(h_ref[...], g_ref[...]).astype(BF16)
            dhalf_ref[...] = (0.5 * do_ref[...]).astype(BF16)
            dxn_s[...] = jnp.zeros_like(dxn_s)

        xn = xn_ref[...]
        wg, wu = wg_ref[0], wu_ref[0]
        gate = _dot(xn, wg)
        up = _dot(xn, wu)
        sg = _sigmoid(gate)
        dact = _dot_nt(dhalf_ref[...], wd_ref[...])
        act_ref[0] = (gate * sg * up).astype(BF16)
        dgate = (dact * up * (sg * (1.0 + gate * (1.0 - sg)))).astype(BF16)
        dup = (dact * gate * sg).astype(BF16)
        dgate_ref[0] = dgate
        dup_ref[0] = dup
        dxn_s[...] += _dot_nt(dgate, wg) + _dot_nt(dup, wu)

        @pl.when((i == 0) & (j == 0))
        def _():
            dg_ref[...] = jnp.zeros_like(dg_ref)

        @pl.when(j == nj - 1)
        def _():
            x = h_ref[...]
            r = lax.rsqrt(jnp.mean(x * x, axis=-1, keepdims=True) + EPS)
            dxn = dxn_s[...]
            dyg = dxn * g_ref[...]
            dh = do_ref[...] + r * dyg - x * (r * r * r) * jnp.mean(dyg * x, axis=-1, keepdims=True)
            dh_ref[...] = dh
            dhb_ref[...] = dh.astype(BF16)
            dg_ref[...] += jnp.sum(dxn * x * r, axis=0, keepdims=True)

    row_d = pl.BlockSpec((tb, d), lambda i, j: (i, 0))
    slab = pl.BlockSpec((1, tb, fc), lambda i, j: (j, i, 0))
    hidden = jax.ShapeDtypeStruct((nj, rows, fc), BF16)
    return pl.pallas_call(
        body, name=name, grid=(rows // tb, nj),
        in_specs=[row_d, pl.BlockSpec((1, d), lambda i, j: (0, 0)), row_d] + _ffn_specs(d, fc, nj),
        out_specs=[row_d, row_d, pl.BlockSpec((1, d), lambda i, j: (0, 0)), row_d, slab, slab, slab, row_d],
        out_shape=[jax.ShapeDtypeStruct((rows, d), F32), jax.ShapeDtypeStruct((rows, d), BF16),
                   jax.ShapeDtypeStruct((1, d), F32), jax.ShapeDtypeStruct((rows, d), BF16),
                   hidden, hidden, hidden, jax.ShapeDtypeStruct((rows, d), BF16)],
        scratch_shapes=[pltpu.VMEM((tb, d), F32)],
        compiler_params=_cp("arbitrary", "arbitrary"),
    )(h, gain, dout, wgu, wgu, wd)


def _shift_rows(x, s):
    return pltpu.roll(x, s % x.shape[0], 0)


def _a_post(c, which):
    s = _silu(c)
    n = s * lax.rsqrt(jnp.sum(s * s, axis=-1, keepdims=True) + 1e-6)
    scale = jnp.where(which == 0, HEAD_A ** -0.5, 1.0)
    return jnp.where(which == 2, s, n * scale)


def _conv(x, w):
    return x * w[3:4] + _shift_rows(x, 1) * w[2:3] + _shift_rows(x, 2) * w[1:2] + _shift_rows(x, 3) * w[0:1]


def _a_pre_fwd(zqkv, conv_w):
    rows, width = zqkv.shape
    heads = width // (3 * HEAD_A)

    def body(x_ref, w_ref, o_ref):
        which = pl.program_id(0) // heads
        live = lax.broadcasted_iota(jnp.int32, (rows, HEAD_A), 0) >= PAD
        o_ref[...] = jnp.where(live, _a_post(_conv(x_ref[...], w_ref[...]), which), 0.0)

    return pl.pallas_call(
        body, name="a_pre_fwd", grid=(width // HEAD_A,),
        in_specs=[pl.BlockSpec((rows, HEAD_A), lambda c: (0, c)), pl.BlockSpec((4, HEAD_A), lambda c: (0, c))],
        out_specs=pl.BlockSpec((rows, HEAD_A), lambda c: (0, c)),
        out_shape=jax.ShapeDtypeStruct((rows, width), F32),
        compiler_params=_cp("parallel"),
    )(zqkv, conv_w)


def _a_pre_bwd(zqkv, conv_w, dqkv):
    rows, width = zqkv.shape
    heads = width // (3 * HEAD_A)

    def body(x_ref, w_ref, ct_ref, dx_ref, dw_ref):
        which = pl.program_id(0) // heads
        live = lax.broadcasted_iota(jnp.int32, (rows, HEAD_A), 0) >= PAD
        x, w = x_ref[...], w_ref[...]
        _, vjp = jax.vjp(lambda c: _a_post(c, which), _conv(x, w))
        (dc,) = vjp(jnp.where(live, ct_ref[...], 0.0))
        dc = jnp.where(live, dc, 0.0)
        dx_ref[...] = (dc * w[3:4] + _shift_rows(dc, -1) * w[2:3] + _shift_rows(dc, -2) * w[1:2]
                       + _shift_rows(dc, -3) * w[0:1]).astype(BF16)
        dw_ref[...] = jnp.concatenate(
            [jnp.sum(dc * (_shift_rows(x, 3 - j) if j < 3 else x), axis=0, keepdims=True) for j in range(4)], axis=0)

    col = pl.BlockSpec((rows, HEAD_A), lambda c: (0, c))
    wsp = pl.BlockSpec((4, HEAD_A), lambda c: (0, c))
    return pl.pallas_call(
        body, name="a_pre_bwd", grid=(width // HEAD_A,),
        in_specs=[col, wsp, col], out_specs=[col, wsp],
        out_shape=[jax.ShapeDtypeStruct((rows, width), BF16), jax.ShapeDtypeStruct((4, width), F32)],
        compiler_params=_cp("parallel"),
    )(zqkv, conv_w, dqkv)


SHIFT_TILE = 384


def _shift_fwd(zb, mu):
    rows, width = zb.shape

    def body(x_ref, mu_ref, o_ref):
        x = x_ref[...]
        first = lax.broadcasted_iota(jnp.int32, x.shape, 0) == 0
        prev = jnp.where(first, 0.0, _shift_rows(x, 1))
        o_ref[...] = x + (prev - x) * mu_ref[...]

    col = pl.BlockSpec((rows, SHIFT_TILE), lambda c: (0, c))
    return pl.pallas_call(
        body, name="shift_fwd", grid=(width // SHIFT_TILE,),
        in_specs=[col, pl.BlockSpec((1, SHIFT_TILE), lambda c: (0, c))], out_specs=col,
        out_shape=jax.ShapeDtypeStruct((rows, width), F32), compiler_params=_cp("parallel"),
    )(zb, mu)


def _shift_bwd(zb, mu, dzf):
    rows, width = zb.shape

    def body(x_ref, mu_ref, ct_ref, dx_ref, dmu_ref):
        x, ct, mu_v = x_ref[...], ct_ref[...], mu_ref[...]
        row = lax.broadcasted_iota(jnp.int32, x.shape, 0)
        prev = jnp.where(row == 0, 0.0, _shift_rows(x, 1))
        nxt = jnp.where(row == rows - 1, 0.0, _shift_rows(ct, -1))
        dx_ref[...] = (ct * (1.0 - mu_v) + nxt * mu_v).astype(BF16)
        dmu_ref[...] = jnp.sum(ct * (prev - x), axis=0, keepdims=True)

    col = pl.BlockSpec((rows, SHIFT_TILE), lambda c: (0, c))
    msp = pl.BlockSpec((1, SHIFT_TILE), lambda c: (0, c))
    return pl.pallas_call(
        body, name="shift_bwd", grid=(width // SHIFT_TILE,),
        in_specs=[col, msp, col], out_specs=[col, msp],
        out_shape=[jax.ShapeDtypeStruct((rows, width), BF16), jax.ShapeDtypeStruct((1, width), F32)],
        compiler_params=_cp("parallel"),
    )(zb, mu, dzf)


def _dn_chunk(q, k, v, beta, g, state):
    heads = range(len(q))
    ri = lax.broadcasted_iota(jnp.int32, (CHUNK, CHUNK), 0)
    ci = lax.broadcasted_iota(jnp.int32, (CHUNK, CHUNK), 1)
    eye = (ri == ci).astype(F32)
    incl = ri >= ci
    last = lax.broadcasted_iota(jnp.int32, (CHUNK, 1), 0) == CHUNK - 1
    g_row = [jnp.sum(g[h] * eye, axis=0, keepdims=True) for h in heads]
    gc = [jnp.sum(jnp.where(incl, g_row[h], 0.0), axis=1, keepdims=True) for h in heads]
    gc_row = [jnp.sum(gc[h] * eye, axis=0, keepdims=True) for h in heads]
    decay = [jnp.where(incl, jnp.exp(jnp.where(incl, gc[h] - gc_row[h], 0.0)), 0.0) for h in heads]
    kb = [k[h] * beta[h] for h in heads]
    vb = [v[h] * beta[h] for h in heads]
    p = [-jnp.where(ri > ci, _dot_nt(kb[h], k[h], DN_PRECISION) * decay[h], 0.0) for h in heads]
    tinv = [eye + p[h] for h in heads]
    for _ in range(5):
        p = [_dot(p[h], p[h], DN_PRECISION) for h in heads]
        tinv = [tinv[h] + _dot(tinv[h], p[h], DN_PRECISION) for h in heads]
    eg = [jnp.exp(gc[h]) for h in heads]
    u = [_dot(tinv[h], vb[h], DN_PRECISION) for h in heads]
    wk = [_dot(tinv[h], kb[h] * eg[h], DN_PRECISION) for h in heads]
    attn = [_dot_nt(q[h], k[h], DN_PRECISION) * decay[h] for h in heads]
    g_last = [jnp.sum(jnp.where(last, gc[h], 0.0), axis=0, keepdims=True) for h in heads]
    k_tail = [k[h] * jnp.exp(g_last[h] - gc[h]) for h in heads]
    v_new = [u[h] - _dot(wk[h], state[h], DN_PRECISION) for h in heads]
    o = [_dot(q[h] * eg[h], state[h], DN_PRECISION) + _dot(attn[h], v_new[h], DN_PRECISION) for h in heads]
    new = [state[h] * jnp.exp(g_last[h]) + _dot_tn(k_tail[h], v_new[h], DN_PRECISION) for h in heads]
    return o, new


def _bg_cols(bg, h, heads):
    lane = lax.broadcasted_iota(jnp.int32, bg.shape, 1)
    beta = jnp.sum(jnp.where(lane == h, bg, 0.0), axis=1, keepdims=True)
    g = jnp.sum(jnp.where(lane == heads + h, bg, 0.0), axis=1, keepdims=True)
    return beta, g


def _dn_fwd(qkv, bg):
    rows = qkv.shape[0]
    heads = qkv.shape[1] // (3 * HEAD_A)
    n = rows // CHUNK
    hp, groups = heads, 1

    def body(q_ref, k_ref, v_ref, bg_ref, o_ref, hist_ref, s_ref):
        c, grp = pl.program_id(0), pl.program_id(1)

        @pl.when(c == 0)
        def _():
            for i in range(hp):
                s_ref[grp * hp + i] = jnp.zeros((HEAD_A, HEAD_A), F32)

        bg_v = bg_ref[...]
        cols = [slice(i * HEAD_A, (i + 1) * HEAD_A) for i in range(hp)]
        state = [s_ref[grp * hp + i] for i in range(hp)]
        beta_g = [_bg_cols(bg_v, grp * hp + i, heads) for i in range(hp)]
        o, new = _dn_chunk([q_ref[:, c_] for c_ in cols], [k_ref[:, c_] for c_ in cols], [v_ref[:, c_] for c_ in cols],
                           [b for b, _ in beta_g], [g for _, g in beta_g], state)
        for i in range(hp):
            hist_ref[0, i] = state[i]
            o_ref[:, cols[i]] = o[i]
            s_ref[grp * hp + i] = new[i]

    def part(p):
        return pl.BlockSpec((CHUNK, hp * HEAD_A), lambda c, grp: (c, p * groups + grp))

    return pl.pallas_call(
        body, name="deltanet_fwd", grid=(n, groups),
        in_specs=[part(0), part(1), part(2), pl.BlockSpec((CHUNK, 128), lambda c, grp: (c, 0))],
        out_specs=[part(0), pl.BlockSpec((1, hp, HEAD_A, HEAD_A), lambda c, grp: (c, grp, 0, 0))],
        out_shape=[jax.ShapeDtypeStruct((rows, heads * HEAD_A), F32),
                   jax.ShapeDtypeStruct((n, heads, HEAD_A, HEAD_A), F32)],
        scratch_shapes=[pltpu.VMEM((heads, HEAD_A, HEAD_A), F32)],
        compiler_params=_cp("arbitrary", "arbitrary"),
    )(qkv, qkv, qkv, bg)


def _dn_bwd(qkv, bg, hist, do):
    rows = qkv.shape[0]
    heads = qkv.shape[1] // (3 * HEAD_A)
    n = rows // CHUNK
    hp, groups = heads, 1

    def body(q_ref, k_ref, v_ref, bg_ref, hist_ref, do_ref, dqkv_ref, dbg_ref, ds_ref):
        c, grp = pl.program_id(0), pl.program_id(1)

        @pl.when(c == 0)
        def _():
            for i in range(hp):
                ds_ref[grp * hp + i] = jnp.zeros((HEAD_A, HEAD_A), F32)

        bg_v = bg_ref[...]
        lane = lax.broadcasted_iota(jnp.int32, (CHUNK, 128), 1)
        cols = [slice(i * HEAD_A, (i + 1) * HEAD_A) for i in range(hp)]
        beta_g = [_bg_cols(bg_v, grp * hp + i, heads) for i in range(hp)]
        _, vjp = jax.vjp(_dn_chunk, [q_ref[:, c_] for c_ in cols], [k_ref[:, c_] for c_ in cols],
                         [v_ref[:, c_] for c_ in cols], [b for b, _ in beta_g], [g for _, g in beta_g],
                         [hist_ref[0, i] for i in range(hp)])
        dq, dk, dv, dbeta, dg, ds = vjp(([do_ref[:, c_] for c_ in cols], [ds_ref[grp * hp + i] for i in range(hp)]))
        dbg = jnp.zeros((CHUNK, 128), F32)
        for i in range(hp):
            h = grp * hp + i
            for p, part_grad in enumerate((dq, dk, dv)):
                dqkv_ref[:, pl.ds((p * heads + i) * HEAD_A, HEAD_A)] = part_grad[i]
            ds_ref[h] = ds[i]
            dbg = dbg + jnp.where(lane == h, dbeta[i], 0.0) + jnp.where(lane == heads + h, dg[i], 0.0)

        @pl.when(grp == 0)
        def _():
            dbg_ref[...] = jnp.zeros_like(dbg_ref)

        dbg_ref[...] += dbg

    def part(p):
        return pl.BlockSpec((CHUNK, hp * HEAD_A), lambda c, grp: (n - 1 - c, p * groups + grp))

    return pl.pallas_call(
        body, name="deltanet_bwd", grid=(n, groups),
        in_specs=[part(0), part(1), part(2), pl.BlockSpec((CHUNK, 128), lambda c, grp: (n - 1 - c, 0)),
                  pl.BlockSpec((1, hp, HEAD_A, HEAD_A), lambda c, grp: (n - 1 - c, grp, 0, 0)), part(0)],
        out_specs=[pl.BlockSpec((CHUNK, 3 * heads * HEAD_A), lambda c, grp: (n - 1 - c, 0)),
                   pl.BlockSpec((CHUNK, 128), lambda c, grp: (n - 1 - c, 0))],
        out_shape=[jax.ShapeDtypeStruct(qkv.shape, F32), jax.ShapeDtypeStruct((rows, 128), F32)],
        scratch_shapes=[pltpu.VMEM((heads, HEAD_A, HEAD_A), F32)],
        compiler_params=_cp("arbitrary", "arbitrary"),
    )(qkv, qkv, qkv, bg, hist, do)


def _head_mask(heads, width):
    return (lax.broadcasted_iota(jnp.int32, (heads, width), 0)
            == lax.broadcasted_iota(jnp.int32, (heads, width), 1) // HEAD_B)


def _masked_rows(mask, row):
    return jnp.where(mask, row, 0.0).astype(BF16)


def _rwkv_fwd(r, w, k, v, a, b):
    rows, width = r.shape
    heads = width // HEAD_B
    ts = SCAN_STEPS

    def body(r_ref, w_ref, k_ref, v_ref, a_ref, b_ref, y_ref, hist_ref, s_ref):
        @pl.when(pl.program_id(0) == 0)
        def _():
            s_ref[...] = jnp.zeros_like(s_ref)

        mask = _head_mask(heads, width)
        onehot = mask.astype(BF16)
        onehot2 = jnp.concatenate([onehot, onehot], axis=0)
        bd = _block_diag_ones()

        spread_v = [_dot_tn(jnp.concatenate(_hi_lo(v_ref[j]), axis=0), onehot2) for j in range(ts)]
        s = s_ref[...]
        ys = []
        for j in range(ts):
            row = pl.ds(j, 1)
            hist_ref[j] = s
            (sa,) = _segsum_many([s * a_ref[row, :]], bd)
            s = s * w_ref[row, :] + sa * b_ref[row, :] + spread_v[j] * k_ref[row, :]
            ys.append(_dot_nt(_masked_rows(mask, r_ref[row, :]), s.astype(BF16)))
        for j in range(ts):
            y_ref[j] = ys[j]
        s_ref[...] = s

    blk = pl.BlockSpec((ts, width), lambda i: (i, 0))
    blk3 = pl.BlockSpec((ts, heads, HEAD_B), lambda i: (i, 0, 0))
    return pl.pallas_call(
        body, name="rwkv_fwd", grid=(rows // ts,),
        in_specs=[blk, blk, blk, blk3, blk, blk],
        out_specs=[blk3, pl.BlockSpec((ts, HEAD_B, width), lambda i: (i, 0, 0)),
                   pl.BlockSpec((HEAD_B, width), lambda i: (0, 0))],
        out_shape=[jax.ShapeDtypeStruct((rows, heads, HEAD_B), F32), jax.ShapeDtypeStruct((rows, HEAD_B, width), F32),
                   jax.ShapeDtypeStruct((HEAD_B, width), F32)],
        compiler_params=_cp("arbitrary"),
    )(r, w, k, v, a, b)


def _rwkv_bwd(r, w, k, v, a, b, hist, last, dy):
    rows, width = r.shape
    heads = width // HEAD_B
    ts = SCAN_STEPS
    nb = rows // ts

    def body(r_ref, w_ref, k_ref, v_ref, a_ref, b_ref, hist_ref, last_ref, dy_ref,
             dr_ref, dw_ref, dk_ref, dv_ref, da_ref, db_ref, g_ref, after_ref):
        @pl.when(pl.program_id(0) == 0)
        def _():
            g_ref[...] = jnp.zeros_like(g_ref)
            after_ref[...] = last_ref[...]

        mask = _head_mask(heads, width)
        onehot = mask.astype(BF16)
        bd = _block_diag_ones()

        def own_lanes(x):
            return jnp.sum(jnp.where(mask, x, 0.0), axis=0, keepdims=True)

        def colsum(x):
            return jnp.sum(x, axis=0, keepdims=True)

        dy_m = [dy_ref[j].astype(BF16) for j in range(ts)]
        spread_dy = [_dot_tn(dy_m[j], onehot) for j in range(ts)]
        state_after = [hist_ref[j + 1] if j < ts - 1 else after_ref[...] for j in range(ts)]
        dr = [own_lanes(_dot(dy_m[j], state_after[j].astype(BF16))) for j in range(ts)]
        sa_m = [_dot_nt(_masked_rows(mask, a_ref[pl.ds(j, 1), :]), hist_ref[j].astype(BF16)) for j in range(ts)]
        g = g_ref[...]
        dw, dk, db, da, dv = {}, {}, {}, {}, {}
        for j in reversed(range(ts)):
            row = pl.ds(j, 1)
            sp = hist_ref[j]
            g = g + spread_dy[j] * r_ref[row, :]
            (dsa,) = _segsum_many([g * b_ref[row, :]], bd)
            g_b = g.astype(BF16)
            both = _dot(jnp.concatenate([v_ref[j].astype(BF16), sa_m[j].astype(BF16)], axis=0), g_b)
            dk[j], db[j] = own_lanes(both[:heads]), own_lanes(both[heads:])
            dv[j] = _dot_nt(_masked_rows(mask, k_ref[row, :]), g_b)
            dw[j] = colsum(g * sp)
            da[j] = colsum(sp * dsa)
            g = g * w_ref[row, :] + dsa * a_ref[row, :]
        g_ref[...] = g
        after_ref[...] = hist_ref[0]
        for j in range(ts):
            dv_ref[j] = dv[j]
            for ref, vals in ((dr_ref, dr), (dw_ref, dw), (dk_ref, dk), (da_ref, da), (db_ref, db)):
                ref[pl.ds(j, 1), :] = vals[j]

    blk = pl.BlockSpec((ts, width), lambda i: (nb - 1 - i, 0))
    blk3 = pl.BlockSpec((ts, heads, HEAD_B), lambda i: (nb - 1 - i, 0, 0))
    state = pl.BlockSpec((HEAD_B, width), lambda i: (0, 0))
    return pl.pallas_call(
        body, name="rwkv_bwd", grid=(nb,),
        in_specs=[blk, blk, blk, blk3, blk, blk, pl.BlockSpec((ts, HEAD_B, width), lambda i: (nb - 1 - i, 0, 0)),
                  state, blk3],
        out_specs=[blk, blk, blk, blk3, blk, blk],
        out_shape=[jax.ShapeDtypeStruct((rows, width), F32)] * 3 + [jax.ShapeDtypeStruct((rows, heads, HEAD_B), F32)]
        + [jax.ShapeDtypeStruct((rows, width), F32)] * 2,
        scratch_shapes=[pltpu.VMEM((HEAD_B, width), F32), pltpu.VMEM((HEAD_B, width), F32)],
        compiler_params=_cp("arbitrary"),
    )(r, w, k, v, a, b, hist, last, dy)


def _live(row0, shape):
    return (row0 + lax.broadcasted_iota(jnp.int32, shape, 0)) >= PAD


def _norm_fn(row0, h, gain):
    return (_rms(h, gain),)


def _norm_res_fn(row0, h, gain):
    return _rms(h, gain), h


def _make_bg_fn(heads):
    def fn(row0, x, log_rate, dt_bias):
        lane = lax.broadcasted_iota(jnp.int32, x.shape, 1)
        beta = _sigmoid(x)
        g = -jnp.exp(log_rate) * _softplus(x + dt_bias)
        out = jnp.where(lane < heads, beta, jnp.where(lane < 2 * heads, g, 0.0))
        return (jnp.where(_live(row0, x.shape), out, 0.0),)
    return fn


def _b_pre_fn(row0, zf, w0, w_up, a0, a_up, g_up, k_k, k_a):
    d = w0.shape[1]
    r, k, v = zf[:, :d], zf[:, d:2 * d], zf[:, 2 * d:3 * d]
    lo = zf[:, 3 * d:3 * d + 128]
    lg = zf[:, 3 * d + 128:3 * d + LORA_PAD]
    lane = lax.broadcasted_iota(jnp.int32, lo.shape, 1)
    lw = _dot(jnp.where(lane < LORA_W, jnp.tanh(lo), 0.0), w_up)
    la = _dot(jnp.where(lane >= LORA_W, lo, 0.0), a_up)
    lane_g = lax.broadcasted_iota(jnp.int32, lg.shape, 1)
    gate = _dot(jnp.where(lane_g < LORA_G, _sigmoid(lg), 0.0), g_up)
    decay = jnp.exp(-jnp.exp(-_softplus(-(w0 + lw)) - 0.5))
    a = _sigmoid(a0 + la)
    kx = k * k_k
    kk = kx * lax.rsqrt(_segsum64(kx * kx) + 1e-6)
    k2 = k * (1.0 + (a - 1.0) * k_a)
    return r, decay, k2, v, -kk, kk * a, gate


def _post_fn(row0, o, zg, y, r, k2, v, gate, out_gain, r_k, ln_g, ln_b):
    d = o.shape[1]
    az, ga, gb = zg[:, :d], zg[:, d:2 * d], zg[:, 2 * d:]
    heads = d // HEAD_A
    parts = []
    for h in range(heads):
        oh = o[:, h * HEAD_A:(h + 1) * HEAD_A]
        parts.append(oh * lax.rsqrt(jnp.mean(oh * oh, axis=-1, keepdims=True) + EPS) * out_gain)
    o_a = jnp.concatenate(parts, axis=1) * _silu(az)
    mean = _segsum64(y) * (1.0 / HEAD_B)
    yc = y - mean
    var = _segsum64(yc * yc) * (1.0 / HEAD_B)
    yn = yc * lax.rsqrt(var + GN_EPS) * ln_g + ln_b
    o_b = (yn + _segsum64(r * k2 * r_k) * v) * gate
    return (_sigmoid(ga) * o_a + _sigmoid(gb) * o_b,)


def _loss(h3, target, gain, tb):
    rows, d = h3.shape

    def body(h_ref, t_ref, g_ref, dh_ref, dg_ref, l_ref):
        i = pl.program_id(0)
        live = (i * tb + lax.broadcasted_iota(jnp.int32, (tb, 1), 0)) >= CHUNK
        tgt = t_ref[...]

        def f(h, g):
            err = _rms(h, g) - tgt
            return 0.5 * jnp.sum(jnp.where(live, jnp.mean(err * err, axis=-1, keepdims=True), 0.0))

        val, vjp = jax.vjp(f, h_ref[...], g_ref[...])
        dh, dg = vjp(jnp.ones((), F32))
        dh_ref[...] = dh

        @pl.when(i == 0)
        def _():
            dg_ref[...] = jnp.zeros_like(dg_ref)
            l_ref[...] = jnp.zeros_like(l_ref)

        dg_ref[...] += dg
        l_ref[...] += jnp.full((1, 128), val, F32)

    blk = pl.BlockSpec((tb, d), lambda i: (i, 0))
    return pl.pallas_call(
        body, name="loss", grid=(rows // tb,),
        in_specs=[blk, blk, pl.BlockSpec((1, d), lambda i: (0, 0))],
        out_specs=[blk, pl.BlockSpec((1, d), lambda i: (0, 0)), pl.BlockSpec((1, 128), lambda i: (0, 0))],
        out_shape=[jax.ShapeDtypeStruct((rows, d), F32), jax.ShapeDtypeStruct((1, d), F32),
                   jax.ShapeDtypeStruct((1, 128), F32)],
        compiler_params=_cp("arbitrary"),
    )(h3, target, gain)


def _adamw_math(w, g, m, v):
    m2 = ADAM_B1 * m + (1.0 - ADAM_B1) * g
    v2 = ADAM_B2 * v + (1.0 - ADAM_B2) * (g * g)
    m_hat = m2 / (1.0 - ADAM_B1 ** ADAM_STEP)
    v_hat = v2 / (1.0 - ADAM_B2 ** ADAM_STEP)
    return -ADAM_LR * (m_hat / (jnp.sqrt(v_hat) + ADAM_EPS) + ADAM_WD * w), m2, v2


def _adamw(name, own, landed, w, m, v, rb):
    rows, cols = w.shape

    def body(o_ref, s_ref, w_ref, m_ref, v_ref, g_ref, d_ref, m2_ref, v2_ref):
        g = o_ref[...].astype(F32)
        for peer in range(N_DEV - 1):
            g = g + s_ref[peer].astype(F32)
        g_ref[...] = g
        d_ref[...], m2_ref[...], v2_ref[...] = _adamw_math(w_ref[...], g, m_ref[...], v_ref[...])

    blk = pl.BlockSpec((rb, cols), lambda i: (i, 0))
    return pl.pallas_call(
        body, name=name, grid=(rows // rb,),
        in_specs=[blk, pl.BlockSpec((N_DEV - 1, rb, cols), lambda i: (0, i, 0)), blk, blk, blk],
        out_specs=[blk] * 4, out_shape=[jax.ShapeDtypeStruct((rows, cols), F32)] * 4,
        compiler_params=_cp("parallel"),
    )(own, landed, w, m, v)


def _sum_slabs(name, slabs, rb):
    _, rows, cols = slabs.shape

    def body(s_ref, o_ref):
        g = s_ref[0]
        for dev in range(1, N_DEV):
            g = g + s_ref[dev]
        o_ref[...] = g

    return pl.pallas_call(
        body, name=name, grid=(rows // rb,),
        in_specs=[pl.BlockSpec((N_DEV, rb, cols), lambda i: (0, i, 0))],
        out_specs=pl.BlockSpec((rb, cols), lambda i: (i, 0)),
        out_shape=jax.ShapeDtypeStruct((rows, cols), F32), compiler_params=_cp("parallel"),
    )(slabs)


def _adamw_small(w, g, m, v):
    def body(w_ref, g_ref, m_ref, v_ref, d_ref, m2_ref, v2_ref):
        d_ref[...], m2_ref[...], v2_ref[...] = _adamw_math(w_ref[...], g_ref[...], m_ref[...], v_ref[...])

    return pl.pallas_call(body, name="adamw_small", out_shape=[jax.ShapeDtypeStruct(w.shape, F32)] * 3)(w, g, m, v)


def _place():
    return lax.axis_index("x"), lax.axis_index("y"), lax.axis_index("c")


def _index(p):
    return 4 * p[0] + 2 * p[1] + p[2]


def _all_gather(name, xs):
    n = len(xs)

    def body(*refs):
        x_refs, o_refs = refs[:n], refs[n:2 * n]
        send_sems, recv_sems, local_sems = refs[2 * n:]
        x, y, c = _place()
        me, sibling = (x, y, c), (x, y, 1 - c)
        chips = [(1 - x, y), (x, 1 - y), (1 - x, 1 - y)]

        def copy(i, k, block, to, src=None):
            dst = o_refs[i].at[_index(block)]
            return pltpu.make_async_remote_copy(src_ref=dst if src is None else src, dst_ref=dst,
                                                send_sem=send_sems.at[i, k], recv_sem=recv_sems.at[i, k],
                                                device_id=to, device_id_type=MESH_ID)

        mine = [pltpu.make_async_copy(x_refs[i], o_refs[i].at[_index(me)], local_sems.at[i]) for i in range(n)]
        for cp in mine:
            cp.start()
        first = []
        for i in range(n):
            first.append(copy(i, 0, me, sibling, src=x_refs[i]))
            first += [copy(i, 1 + j, me, (*chip, c), src=x_refs[i]) for j, chip in enumerate(chips)]
        for cp in first:
            cp.start()
        passed = []
        for j, chip in enumerate(chips):
            for i in range(n):
                copy(i, 1 + j, (*chip, c), me).wait_recv()
                cp = copy(i, 4 + j, (*chip, c), sibling)
                cp.start()
                passed.append(cp)
        for i in range(n):
            copy(i, 0, sibling, me).wait_recv()
            for j, chip in enumerate(chips):
                copy(i, 4 + j, (*chip, 1 - c), me).wait_recv()
        for cp in first + passed:
            cp.wait_send()
        for cp in mine:
            cp.wait()

    return pl.pallas_call(
        body, name=name, in_specs=[ANY] * n, out_specs=[ANY] * n,
        out_shape=[jax.ShapeDtypeStruct((N_DEV,) + x.shape, x.dtype) for x in xs],
        scratch_shapes=[pltpu.SemaphoreType.DMA((n, 7)), pltpu.SemaphoreType.DMA((n, 7)), pltpu.SemaphoreType.DMA((n,))],
    )(*xs)


def _exchange_start(name, xs):
    n = len(xs)
    copies = n * (N_DEV - 1)

    def body(*refs):
        x_refs, land_refs = refs[:n], refs[n:2 * n]
        sems = refs[2 * n:2 * n + 2 * copies]
        token = refs[-1]
        for i, k, peer in _exchange_copies(n):
            _exchange_copy(x_refs, land_refs, sems, i, k, peer).start()
        token[...] = jnp.zeros_like(token)

    lands = [lax.empty((N_DEV - 1,) + x.shape[1:], x.dtype) for x in xs]
    out = pl.pallas_call(
        body, name=name,
        out_shape=(*[pltpu.SemaphoreType.DMA(())] * (2 * copies), *[pltpu.HBM(x.shape, x.dtype) for x in xs],
                   *[pltpu.HBM(l.shape, l.dtype) for l in lands], jax.ShapeDtypeStruct((8, 128), F32)),
        in_specs=[HBM_SPEC] * (2 * n),
        out_specs=(*[SEM_SPEC] * (2 * copies), *[HBM_SPEC] * (2 * n), pl.BlockSpec(memory_space=pltpu.VMEM)),
        input_output_aliases={i: 2 * copies + i for i in range(2 * n)},
        compiler_params=pltpu.CompilerParams(has_side_effects=pltpu.SideEffectType.DATAFLOW_SIDE_EFFECTING),
    )(*[pltpu.with_memory_space_constraint(a, pltpu.HBM) for a in list(xs) + lands])
    sems, rest = list(out[:2 * copies]), out[2 * copies:]
    return sems, list(rest[:n]), list(rest[n:2 * n]), rest[-1]


def _exchange_copies(n):
    x, y, c = _place()
    for k in range(1, N_DEV):
        peer = ((1 - x) if k & 4 else x, (1 - y) if k & 2 else y, (1 - c) if k & 1 else c)
        for i in range(n):
            yield i, k - 1, peer


def _exchange_copy(x_refs, land_refs, sems, i, k, peer):
    copies = len(sems) // 2
    which = i * (N_DEV - 1) + k
    return pltpu.make_async_remote_copy(src_ref=x_refs[i].at[_index(peer)], dst_ref=land_refs[i].at[k],
                                        send_sem=sems[which], recv_sem=sems[copies + which],
                                        device_id=peer, device_id_type=MESH_ID)


def _exchange_wait(name, sems, xs, lands, after):
    n = len(xs)

    def body(*refs):
        x_refs, land_refs = refs[:n], refs[n:2 * n]
        sem_refs = refs[2 * n:2 * n + len(sems)]
        for i, k, peer in _exchange_copies(n):
            copy = _exchange_copy(x_refs, land_refs, sem_refs, i, k, peer)
            copy.wait_send()
            copy.wait_recv()

    out = pl.pallas_call(
        body, name=name,
        out_shape=(*[pltpu.HBM(x.shape, x.dtype) for x in xs], *[pltpu.HBM(l.shape, l.dtype) for l in lands]),
        in_specs=[HBM_SPEC] * (2 * n) + [SEM_SPEC] * len(sems) + [ANY], out_specs=tuple([HBM_SPEC] * (2 * n)),
        input_output_aliases={i: i for i in range(2 * n)},
        compiler_params=pltpu.CompilerParams(has_side_effects=pltpu.SideEffectType.DATAFLOW_SIDE_EFFECTING),
    )(*xs, *lands, *sems, after)
    return list(out[:n]), list(out[n:])


def _pack(arrays):
    flat = jnp.concatenate([a.reshape(-1) for a in arrays])
    pad = (-flat.shape[0]) % 1024
    return jnp.pad(flat, (0, pad)).reshape(-1, 128)


def _unpack(packed, shapes):
    flat = packed.reshape(-1)
    out, pos = [], 0
    for s in shapes:
        size = 1
        for dim in s:
            size *= dim
        out.append(flat[pos:pos + size].reshape(s))
        pos += size
    return out


def _cols_from_slabs(stack):
    return jnp.transpose(stack, (1, 0, 2)).reshape(stack.shape[1], -1)


def _cols_to_slabs(full):
    return jnp.transpose(full.reshape(full.shape[0], N_DEV, -1), (1, 0, 2))


def kernel(x, meta_tokens, ffn1_norm, ffn1_w_gu, ffn1_w_down, mix_norm, w_in, a_conv_w, a_log_rate, a_dt_bias, a_out_norm, b_shift_mu, b_w0, b_w_up, b_a0, b_a_up, b_g_up, b_k_k, b_k_a, b_r_k, b_ln_gain, b_ln_bias, w_out, ffn2_norm, ffn2_w_gu, ffn2_w_down, final_norm, loss_target, m_meta_tokens, m_ffn1_norm, m_ffn1_w_gu, m_ffn1_w_down, m_mix_norm, m_w_in, m_a_conv_w, m_a_log_rate, m_a_dt_bias, m_a_out_norm, m_b_shift_mu, m_b_w0, m_b_w_up, m_b_a0, m_b_a_up, m_b_g_up, m_b_k_k, m_b_k_a, m_b_r_k, m_b_ln_gain, m_b_ln_bias, m_w_out, m_ffn2_norm, m_ffn2_w_gu, m_ffn2_w_down, m_final_norm, v_meta_tokens, v_ffn1_norm, v_ffn1_w_gu, v_ffn1_w_down, v_mix_norm, v_w_in, v_a_conv_w, v_a_log_rate, v_a_dt_bias, v_a_out_norm, v_b_shift_mu, v_b_w0, v_b_w_up, v_b_a0, v_b_a_up, v_b_g_up, v_b_k_k, v_b_k_a, v_b_r_k, v_b_ln_gain, v_b_ln_bias, v_w_out, v_ffn2_norm, v_ffn2_w_gu, v_ffn2_w_down, v_final_norm):
    names = ['meta_tokens', 'ffn1_norm', 'ffn1_w_gu', 'ffn1_w_down', 'mix_norm', 'w_in', 'a_conv_w', 'a_log_rate',
             'a_dt_bias', 'a_out_norm', 'b_shift_mu', 'b_w0', 'b_w_up', 'b_a0', 'b_a_up', 'b_g_up', 'b_k_k', 'b_k_a',
             'b_r_k', 'b_ln_gain', 'b_ln_bias', 'w_out', 'ffn2_norm', 'ffn2_w_gu', 'ffn2_w_down', 'final_norm']
    env = dict(locals())
    wts = {k: env[k] for k in names}
    mom_m = {k: env['m_' + k] for k in names}
    mom_v = {k: env['v_' + k] for k in names}
    big = ['ffn1_w_gu', 'ffn1_w_down', 'w_in', 'w_out', 'ffn2_w_gu', 'ffn2_w_down']
    col_sharded = {'ffn1_w_gu', 'w_in', 'ffn2_w_gu'}
    small_sharded = ['meta_tokens', 'a_conv_w', 'b_w_up', 'b_a_up', 'b_g_up']
    replicated = [k for k in names if k not in big and k not in small_sharded]

    seq, d = x.shape[1], x.shape[2]
    rows = PAD + N_META + seq
    heads_a = d // HEAD_A
    tb_mm = _tb(rows, 416)
    tb_vjp = _tb(rows, 208)
    me = _index(_place())

    big_local = [wts[k][0].astype(BF16) for k in big]
    small_local = [wts['meta_tokens']] + [wts[k][0] for k in small_sharded[1:]]
    gathered = _all_gather("gather_weights", big_local + small_local)
    gw = dict(zip(big + small_sharded, gathered))
    full = {k: (gw[k] if k in ('ffn1_w_gu', 'ffn2_w_gu') else _cols_from_slabs(gw[k]) if k in col_sharded
                else gw[k].reshape(-1, gw[k].shape[-1])) for k in big}
    for k in small_sharded:
        full[k] = _cols_from_slabs(gw[k])
    for k in replicated:
        full[k] = wts[k].reshape(1, -1)

    win = full['w_in']
    n_b = 3 * d + LORA_W + LORA_A + LORA_G
    off_beta, off_b = 4 * d, 4 * d + 2 * heads_a
    off_ga = off_b + n_b
    b_width = 3 * d + LORA_PAD
    zcols = lambda r, c: jnp.zeros((r, c), BF16)
    w_qkv = win[:, :3 * d]
    w_zg = jnp.concatenate([win[:, 3 * d:4 * d], win[:, off_ga:off_ga + 2 * d]], axis=1)
    w_b = jnp.concatenate([win[:, off_b:off_b + n_b], zcols(d, b_width - n_b)], axis=1)
    w_bg = jnp.concatenate([win[:, off_beta:off_beta + 2 * heads_a], zcols(d, 128 - 2 * heads_a)], axis=1)

    def lanes(vec, start, width):
        return jnp.pad(vec.reshape(1, -1), ((0, 0), (start, width - start - vec.size)))

    log_rate = lanes(wts['a_log_rate'], heads_a, 128)
    dt_bias = lanes(wts['a_dt_bias'], heads_a, 128)
    mu = lanes(wts['b_shift_mu'], 0, b_width)
    w_up = jnp.pad(full['b_w_up'], ((0, 128 - LORA_W), (0, 0)))
    a_up = jnp.pad(full['b_a_up'], ((LORA_W, 0), (0, 0)))
    g_up = jnp.pad(full['b_g_up'], ((0, 256 - LORA_G), (0, 0)))
    b_pars = [full['b_w0'], w_up, full['b_a0'], a_up, g_up, full['b_k_k'], full['b_k_a']]
    post_pars = [full['a_out_norm'], full['b_r_k'], full['b_ln_gain'], full['b_ln_bias']]
    bg_fn = _make_bg_fn(heads_a)

    h0 = jnp.concatenate([jnp.zeros((PAD, d), F32), full['meta_tokens'], x[0]], axis=0)
    h1 = _ffn_fwd("ffn1_fwd", h0, full['ffn1_norm'], full['ffn1_w_gu'], full['ffn1_w_down'], tb_mm)
    (u,) = _tok_fwd("mix_norm_fwd", _norm_fn, [h1], [full['mix_norm']], [(d, BF16)], tb_mm)
    z_qkv = _mm("in_qkv", u, w_qkv, tb=tb_mm, tn=_col_tile(3 * d, 1536))
    z_zg = _mm("in_zg", u, w_zg, tb=tb_mm, tn=_col_tile(3 * d, 1536))
    z_b = _mm("in_b", u, w_b, tb=tb_mm, tn=_col_tile(b_width, 1536))
    z_bg = _mm("in_bg", u, w_bg, tb=tb_mm, tn=128)
    qkv = _a_pre_fwd(z_qkv, full['a_conv_w'])
    (bg,) = _tok_fwd("bg_fwd", bg_fn, [z_bg], [log_rate, dt_bias], [(128, F32)], tb_mm)
    o_dn, dn_hist = _dn_fwd(qkv, bg)
    zf = _shift_fwd(z_b, mu)
    rr, ww, kk2, vv, av, bv, gate = _tok_fwd("b_pre_fwd", _b_pre_fn, [zf], b_pars, [(d, F32)] * 7, tb_vjp)
    per_head = lambda t: t.reshape(rows, d // HEAD_B, HEAD_B)
    y_heads, b_hist, b_last = _rwkv_fwd(rr, ww, kk2, per_head(vv), av, bv)
    y_b = y_heads.reshape(rows, d)
    post_toks = [o_dn, z_zg, y_b, rr, kk2, vv, gate]
    (merged,) = _tok_fwd("post_fwd", _post_fn, post_toks, post_pars, [(d, BF16)], tb_vjp)
    h2 = _mm("out_proj", merged, full['w_out'], add=h1, tb=tb_mm, tn=d)
    h3 = _ffn_fwd("ffn2_fwd", h2, full['ffn2_norm'], full['ffn2_w_gu'], full['ffn2_w_down'], tb_mm)

    target = jnp.pad(loss_target[0], ((CHUNK, 0), (0, 0)))
    dh3, g_final, loss_part = _loss(h3, target, full['final_norm'].reshape(1, d), tb_vjp)

    def ffn_backward(tag, h, dout, key_norm, key_gu, key_down):
        dh, dh_bf, dgain, xn, act, dgate, dup, dhalf = _ffn_bwd(tag + "_bwd", h, full[key_norm], dout, full[key_gu],
                                                                full[key_down], tb_mm)
        d_gu = jnp.concatenate([_mm_tn_to_slabs(tag + "_dw_gate", xn, dgate, tk=tb_mm),
                                _mm_tn_to_slabs(tag + "_dw_up", xn, dup, tk=tb_mm)], axis=0)
        d_down = _mm_tn_from_slabs(tag + "_dw_down", act, dhalf, tk=tb_mm).reshape(N_DEV, -1, d)
        return dh, dh_bf, dgain, d_gu, d_down

    dh2, dh2_bf, g_ffn2_norm, g_ffn2_gu, g_ffn2_down = ffn_backward("ffn2", h2, dh3, 'ffn2_norm', 'ffn2_w_gu',
                                                                    'ffn2_w_down')
    g_w_out = _mm_tn("dw_out", merged, dh2_bf, tm=d, tn=d, tk=tb_mm).reshape(N_DEV, -1, d)

    def start_exchange(tag, keys, slabs):
        sems, kept, lands, token = _exchange_start("exchange_start_" + tag, slabs)
        return (tag, keys, sems, kept, lands), token

    ex_ffn2, token_ffn2 = start_exchange("ffn2", ['ffn2_w_gu', 'ffn2_w_down', 'w_out'], [g_ffn2_gu, g_ffn2_down, g_w_out])
    dmerged = _mm("d_merged", dh2_bf, full['w_out'], trans_b=True, after=token_ffn2, tb=tb_mm, tn=d)
    post_grads = _tok_bwd("post_bwd", _post_fn, post_toks, post_pars, [[dmerged]], list(range(7)), tb_vjp,
                          [F32, BF16] + [F32] * 5)
    do_dn, dz_zg, dy_b, dr1, dk1, dv1, dgate = post_grads[:7]
    g_out_norm, g_r_k, g_ln_g, g_ln_b = post_grads[7:]
    dr2, dw2, dk2, dv_heads, da2, db2 = _rwkv_bwd(rr, ww, kk2, per_head(vv), av, bv, b_hist, b_last, per_head(dy_b))
    dv2 = dv_heads.reshape(rows, d)
    b_grads = _tok_bwd("b_pre_bwd", _b_pre_fn, [zf], b_pars,
                       [[dr1, dr2], [dw2], [dk1, dk2], [dv1, dv2], [da2], [db2], [dgate]], [0], tb_vjp)
    dzf = b_grads[0]
    g_w0, g_w_up, g_a0, g_a_up, g_g_up, g_k_k, g_k_a = b_grads[1:]
    dz_b, g_mu = _shift_bwd(z_b, mu, dzf)
    dqkv, dbg = _dn_bwd(qkv, bg, dn_hist, do_dn)
    dz_qkv, g_conv = _a_pre_bwd(z_qkv, full['a_conv_w'], dqkv)
    dz_bg, g_log_rate, g_dt_bias = _tok_bwd("bg_bwd", bg_fn, [z_bg], [log_rate, dt_bias], [[dbg]], [0], tb_mm, [BF16])

    du = None
    g_w_in_parts = []
    for tag, dz, wpiece in (("qkv", dz_qkv, w_qkv), ("zg", dz_zg, w_zg), ("b", dz_b, w_b), ("bg", dz_bg, w_bg)):
        du = _mm("du_" + tag, dz, wpiece, trans_b=True, add=du, tb=tb_mm, tn=d)
        g_w_in_parts.append(_mm_tn("dw_in_" + tag, u, dz, tm=d, tn=_col_tile(dz.shape[1], 1536), tk=tb_mm))
    gp_qkv, gp_zg, gp_b, gp_bg = g_w_in_parts
    g_w_in = _cols_to_slabs(jnp.concatenate(
        [gp_qkv, gp_zg[:, :d], gp_bg[:, :2 * heads_a], gp_b[:, :n_b], gp_zg[:, d:]], axis=1))
    ex_w_in, token_w_in = start_exchange("w_in", ['w_in'], [g_w_in])
    dh1, g_mix_norm = _tok_bwd("mix_norm_bwd", _norm_res_fn, [h1], [full['mix_norm']], [[du], [dh2]], [0], tb_vjp,
                               after=token_w_in)
    dh0, _, g_ffn1_norm, g_ffn1_gu, g_ffn1_down = ffn_backward("ffn1", h0, dh1, 'ffn1_norm', 'ffn1_w_gu', 'ffn1_w_down')
    ex_ffn1, token_ffn1 = start_exchange("ffn1", ['ffn1_w_gu', 'ffn1_w_down'], [g_ffn1_gu, g_ffn1_down])

    small_full = {
        'meta_tokens': dh0[PAD:CHUNK], 'ffn1_norm': g_ffn1_norm, 'mix_norm': g_mix_norm, 'a_conv_w': g_conv,
        'a_log_rate': g_log_rate[:, heads_a:2 * heads_a], 'a_dt_bias': g_dt_bias[:, heads_a:2 * heads_a],
        'a_out_norm': g_out_norm, 'b_shift_mu': g_mu[:, :n_b], 'b_w0': g_w0, 'b_w_up': g_w_up[:LORA_W],
        'b_a0': g_a0, 'b_a_up': g_a_up[LORA_W:], 'b_g_up': g_g_up[:LORA_G], 'b_k_k': g_k_k, 'b_k_a': g_k_a,
        'b_r_k': g_r_k, 'b_ln_gain': g_ln_g, 'b_ln_bias': g_ln_b, 'ffn2_norm': g_ffn2_norm, 'final_norm': g_final,
    }
    small_names = list(small_full)
    packed = _pack([small_full[k] for k in small_names] + [loss_part[:, :1] + token_ffn1[:1, :1]])
    (all_parts,) = _all_gather("gather_small_grads", [packed])
    summed = _sum_slabs("sum_small_grads", all_parts, packed.shape[0])
    pieces = _unpack(summed, [small_full[k].shape for k in small_names] + [(1, 1)])
    small_grad = dict(zip(small_names, pieces[:-1]))
    loss = pieces[-1].reshape(())

    grads, deltas, new_m, new_v = {}, {}, {}, {}
    local_small = {}
    for k in small_names:
        g = small_grad[k]
        if k in small_sharded:
            width = wts[k].shape[-1]
            g = lax.dynamic_slice_in_dim(g, me * width, width, axis=1)
        local_small[k] = g.reshape(wts[k].shape)
    pk = lambda tree: _pack([tree[k] for k in small_names])
    dl_s, m_s, v_s = _adamw_small(pk(wts), pk(local_small), pk(mom_m), pk(mom_v))
    shapes = [wts[k].shape for k in small_names]
    for k, dl, m2, v2 in zip(small_names, _unpack(dl_s, shapes), _unpack(m_s, shapes), _unpack(v_s, shapes)):
        grads[k], deltas[k], new_m[k], new_v[k] = local_small[k], dl, m2, v2

    done = dl_s
    for tag, keys, sems, kept, lands in (ex_ffn2, ex_w_in, ex_ffn1):
        kept, lands = _exchange_wait("exchange_wait_" + tag, sems, kept, lands, done)
        for k, slabs, landed in zip(keys, kept, lands):
            shard = wts[k][0]
            rb = _tb(shard.shape[0], 128)
            own = lax.dynamic_index_in_dim(slabs, me, axis=0, keepdims=False)
            g, dl, m2, v2 = _adamw("adamw_" + k, own, landed, shard, mom_m[k][0], mom_v[k][0], rb)
            grads[k], deltas[k], new_m[k], new_v[k] = g[None], dl[None], m2[None], v2[None]
            done = dl

    grad_x = dh0[CHUNK:][None]
    return (loss, grad_x, *[grads[k] for k in names], *[deltas[k] for k in names],
            *[new_m[k] for k in names], *[new_v[k] for k in names])
```

```python
import functools

import jax
import jax.numpy as jnp
from jax import lax
from jax.experimental import pallas as pl
from jax.experimental.pallas import tpu as pltpu

F32 = jnp.float32
BF16 = jnp.bfloat16
N_DEV = 8
N_META = 16
CHUNK = 64
PAD = CHUNK - N_META
HEAD_A = 128
HEAD_B = 64
LORA_W, LORA_A, LORA_G = 64, 64, 160
LORA_PAD = 384
EPS = 1e-6
GN_EPS = HEAD_B * 1e-5
ADAM_LR, ADAM_B1, ADAM_B2, ADAM_EPS, ADAM_WD, ADAM_STEP = 0.001, 0.9, 0.999, 1e-08, 0.01, 10
SCAN_STEPS = 16
MXU_WIDTH = 256
VMEM_LIMIT = 56 * 1024 * 1024
DN_PRECISION = lax.Precision.HIGH
MESH_ID = pl.DeviceIdType.MESH
ANY = pl.BlockSpec(memory_space=pl.ANY)
HBM_SPEC = pl.BlockSpec(memory_space=pltpu.HBM)
SEM_SPEC = pl.BlockSpec(memory_space=pltpu.SEMAPHORE)


def _cp(*sem):
    return pltpu.CompilerParams(dimension_semantics=sem, vmem_limit_bytes=VMEM_LIMIT)


def _tb(t, target):
    best = 16
    for d in range(16, target + 1, 16):
        if t % d == 0:
            best = d
    return best


def _sigmoid(x):
    return 1.0 / (1.0 + jnp.exp(-x))


def _silu(x):
    return x * _sigmoid(x)


def _softplus(x):
    return jnp.maximum(x, 0.0) + jnp.log(1.0 + jnp.exp(-jnp.abs(x)))


def _dot_nt(a, b, precision=None):
    return lax.dot_general(a, b, (((1,), (1,)), ((), ())), preferred_element_type=F32, precision=precision)


def _dot_tn(a, b, precision=None):
    return lax.dot_general(a, b, (((0,), (0,)), ((), ())), preferred_element_type=F32, precision=precision)


def _dot(a, b, precision=None):
    return jnp.dot(a, b, preferred_element_type=F32, precision=precision)


def _block_diag_ones():
    i = lax.broadcasted_iota(jnp.int32, (MXU_WIDTH, MXU_WIDTH), 0) // HEAD_B
    j = lax.broadcasted_iota(jnp.int32, (MXU_WIDTH, MXU_WIDTH), 1) // HEAD_B
    return (i == j).astype(BF16)


def _hi_lo(x):
    hi = x.astype(BF16)
    return hi, (x - hi.astype(F32)).astype(BF16)


def _segsum_many(xs, bd):
    groups = [x if isinstance(x, tuple) else _hi_lo(x) for x in xs]
    rows = groups[0][0].shape[0]
    stacked = jnp.concatenate([p for grp in groups for p in grp], axis=0)
    out = jnp.concatenate([_dot(stacked[:, s:s + MXU_WIDTH], bd) for s in range(0, stacked.shape[1], MXU_WIDTH)], axis=1)
    res, pos = [], 0
    for grp in groups:
        acc = out[pos:pos + rows]
        for j in range(1, len(grp)):
            acc = acc + out[pos + j * rows:pos + (j + 1) * rows]
        res.append(acc)
        pos += len(grp) * rows
    return res


def _segsum_impl(x):
    return _segsum_many([x], _block_diag_ones())[0]


@jax.custom_vjp
def _segsum64(x):
    return _segsum_impl(x)


_segsum64.defvjp(lambda x: (_segsum_impl(x), None), lambda _, ct: (_segsum_impl(ct),))


def _tok(t):
    return t if isinstance(t, tuple) else (t, t.shape[1], 0)


def _tok_spec(tb, width, colblk):
    return pl.BlockSpec((tb, width), lambda i: (i, colblk))


def _par_spec(p):
    return pl.BlockSpec(p.shape, lambda i: (0, 0))


def _tok_fwd(name, fn, toks, pars, outs, tb):
    toks = [_tok(t) for t in toks]
    rows = toks[0][0].shape[0]
    n_in = len(toks) + len(pars)

    def body(*refs):
        row0 = pl.program_id(0) * tb
        res = fn(row0, *[r[...] for r in refs[:n_in]])
        for r, o in zip(refs[n_in:], res):
            r[...] = o.astype(r.dtype)

    return pl.pallas_call(
        body, name=name, grid=(rows // tb,),
        in_specs=[_tok_spec(tb, w, c) for _, w, c in toks] + [_par_spec(p) for p in pars],
        out_specs=[_tok_spec(tb, w, 0) for w, _ in outs],
        out_shape=[jax.ShapeDtypeStruct((rows, w), dt) for w, dt in outs],
        compiler_params=_cp("parallel"),
    )(*[a for a, _, _ in toks], *pars)


def _tok_bwd(name, fn, toks, pars, cts, want, tb, want_dtypes=None, after=None):
    toks = [_tok(t) for t in toks]
    want_dtypes = want_dtypes or [F32] * len(want)
    cts = [[_tok(c) for c in group] for group in cts]
    flat_cts = [c for group in cts for c in group]
    rows = toks[0][0].shape[0]
    n_tok, n_par, n_ct = len(toks), len(pars), len(flat_cts)
    extra = [] if after is None else [after]

    def body(*refs):
        i = pl.program_id(0)
        row0 = i * tb
        prim = [r[...].astype(F32) for r in refs[:n_tok + n_par]]
        ct_refs = list(refs[n_tok + n_par:n_tok + n_par + n_ct])
        out_refs = refs[n_tok + n_par + n_ct + len(extra):]
        res, vjp = jax.vjp(lambda *a: fn(row0, *a), *prim)
        ct = []
        for group, o in zip(cts, res):
            acc = None
            for _ in group:
                v = ct_refs.pop(0)[...].astype(F32)
                acc = v if acc is None else acc + v
            ct.append(acc.astype(o.dtype))
        grads = vjp(tuple(ct))
        for r, k in zip(out_refs[:len(want)], want):
            r[...] = grads[k].astype(r.dtype)

        @pl.when(i == 0)
        def _():
            for r in out_refs[len(want):]:
                r[...] = jnp.zeros_like(r)

        for r, g in zip(out_refs[len(want):], grads[n_tok:]):
            r[...] += g

    return pl.pallas_call(
        body, name=name, grid=(rows // tb,),
        in_specs=[_tok_spec(tb, w, c) for _, w, c in toks] + [_par_spec(p) for p in pars]
        + [_tok_spec(tb, w, c) for _, w, c in flat_cts] + [ANY] * len(extra),
        out_specs=[_tok_spec(tb, toks[k][1], 0) for k in want] + [_par_spec(p) for p in pars],
        out_shape=[jax.ShapeDtypeStruct((rows, toks[k][1]), dt) for k, dt in zip(want, want_dtypes)]
        + [jax.ShapeDtypeStruct(p.shape, F32) for p in pars],
        compiler_params=_cp("arbitrary"),
    )(*[a for a, _, _ in toks], *pars, *[a for a, _, _ in flat_cts], *extra)


def _mm(name, a, b, *, trans_b=False, add=None, after=None, tb, tn):
    rows, k = a.shape
    n = b.shape[0] if trans_b else b.shape[1]

    def body(*refs):
        a_ref, b_ref = refs[:2]
        o_ref = refs[-1]
        acc = _dot_nt(a_ref[...], b_ref[...]) if trans_b else _dot(a_ref[...], b_ref[...])
        if add is not None:
            acc = acc + refs[2][...]
        o_ref[...] = acc

    in_specs = [pl.BlockSpec((tb, k), lambda i, j: (i, 0)),
                pl.BlockSpec((tn, k), lambda i, j: (j, 0)) if trans_b else pl.BlockSpec((k, tn), lambda i, j: (0, j))]
    args = [a, b]
    if add is not None:
        in_specs.append(pl.BlockSpec((tb, tn), lambda i, j: (i, j)))
        args.append(add)
    if after is not None:
        in_specs.append(ANY)
        args.append(after)
    return pl.pallas_call(
        body, name=name, grid=(rows // tb, n // tn), in_specs=in_specs,
        out_specs=pl.BlockSpec((tb, tn), lambda i, j: (i, j)),
        out_shape=jax.ShapeDtypeStruct((rows, n), F32),
        compiler_params=_cp("parallel", "parallel"),
    )(*args)


def _mm_tn_call(name, grid, a, b, a_spec, b_spec, o_spec, acc_shape, out_shape):
    last = len(grid) - 1

    def body(a_ref, b_ref, o_ref, acc_ref):
        k = pl.program_id(last)

        @pl.when(k == 0)
        def _():
            acc_ref[...] = jnp.zeros_like(acc_ref)

        a_blk = a_ref[0] if len(a_ref.shape) == 3 else a_ref[...]
        b_blk = b_ref[0] if len(b_ref.shape) == 3 else b_ref[...]
        acc_ref[...] += _dot_tn(a_blk, b_blk)

        @pl.when(k == grid[last] - 1)
        def _():
            if len(o_ref.shape) == 3:
                o_ref[0] = acc_ref[...].astype(o_ref.dtype)
            else:
                o_ref[...] = acc_ref[...].astype(o_ref.dtype)

    return pl.pallas_call(
        body, name=name, grid=grid, in_specs=[a_spec, b_spec], out_specs=o_spec,
        out_shape=jax.ShapeDtypeStruct(out_shape, BF16), scratch_shapes=[pltpu.VMEM(acc_shape, F32)],
        compiler_params=_cp(*(["parallel"] * last + ["arbitrary"])),
    )(a, b)


def _mm_tn(name, a, b, *, tm, tn, tk):
    rows, m = a.shape
    n = b.shape[1]
    return _mm_tn_call(name, (m // tm, n // tn, rows // tk), a, b,
                       pl.BlockSpec((tk, tm), lambda i, j, k: (k, i)), pl.BlockSpec((tk, tn), lambda i, j, k: (k, j)),
                       pl.BlockSpec((tm, tn), lambda i, j, k: (i, j)), (tm, tn), (m, n))


def _mm_tn_to_slabs(name, a, b3, *, tk):
    rows, m = a.shape
    s, _, c = b3.shape
    return _mm_tn_call(name, (s, rows // tk), a, b3,
                       pl.BlockSpec((tk, m), lambda i, k: (k, 0)), pl.BlockSpec((1, tk, c), lambda i, k: (i, k, 0)),
                       pl.BlockSpec((1, m, c), lambda i, k: (i, 0, 0)), (m, c), (s, m, c))


def _mm_tn_from_slabs(name, a3, b, *, tk):
    s, rows, c = a3.shape
    n = b.shape[1]
    return _mm_tn_call(name, (s, rows // tk), a3, b,
                       pl.BlockSpec((1, tk, c), lambda i, k: (i, k, 0)), pl.BlockSpec((tk, n), lambda i, k: (k, 0)),
                       pl.BlockSpec((c, n), lambda i, k: (i, 0)), (c, n), (s * c, n))


def _col_tile(n, target):
    if n <= target:
        return n
    best = 128
    for d in range(128, target + 1, 128):
        if n % d == 0:
            best = d
    return best


def _rms(x, gain):
    return x * lax.rsqrt(jnp.mean(x * x, axis=-1, keepdims=True) + EPS) * gain


def _ffn_specs(d, fc, nj):
    return [pl.BlockSpec((1, d, fc), lambda i, j: (j, 0, 0)), pl.BlockSpec((1, d, fc), lambda i, j: (nj + j, 0, 0)),
            pl.BlockSpec((fc, d), lambda i, j: (j, 0))]


def _ffn_fwd(name, h, gain, wgu, wd, tb):
    rows, d = h.shape
    nj = wgu.shape[0] // 2
    fc = wgu.shape[2]

    def body(h_ref, g_ref, wg_ref, wu_ref, wd_ref, o_ref, xn_s, acc_s):
        j = pl.program_id(1)

        @pl.when(j == 0)
        def _():
            xn_s[...] = _rms(h_ref[...], g_ref[...]).astype(BF16)
            acc_s[...] = jnp.zeros_like(acc_s)

        xn = xn_s[...]
        gate = _dot(xn, wg_ref[0])
        up = _dot(xn, wu_ref[0])
        acc_s[...] += _dot((_silu(gate) * up).astype(BF16), wd_ref[...])

        @pl.when(j == nj - 1)
        def _():
            o_ref[...] = h_ref[...] + 0.5 * acc_s[...]

    return pl.pallas_call(
        body, name=name, grid=(rows // tb, nj),
        in_specs=[pl.BlockSpec((tb, d), lambda i, j: (i, 0)), pl.BlockSpec((1, d), lambda i, j: (0, 0))]
        + _ffn_specs(d, fc, nj),
        out_specs=pl.BlockSpec((tb, d), lambda i, j: (i, 0)),
        out_shape=jax.ShapeDtypeStruct((rows, d), F32),
        scratch_shapes=[pltpu.VMEM((tb, d), BF16), pltpu.VMEM((tb, d), F32)],
        compiler_params=_cp("parallel", "arbitrary"),
    )(h, gain, wgu, wgu, wd)


def _ffn_bwd(name, h, gain, dout, wgu, wd, tb):
    rows, d = h.shape
    nj = wgu.shape[0] // 2
    fc = wgu.shape[2]

    def body(h_ref, g_ref, do_ref, wg_ref, wu_ref, wd_ref,
             dh_ref, dhb_ref, dg_ref, xn_ref, act_ref, dgate_ref, dup_ref, dhalf_ref, dxn_s):
        i, j = pl.program_id(0), pl.program_id(1)

        @pl.when(j == 0)
        def _():
            xn_ref[...] = _rms(h_ref[...], g_ref[...]).astype(BF16)
            dhalf_ref[...] = (0.5 * do_ref[...]).astype(BF16)
            dxn_s[...] = jnp.zeros_like(dxn_s)

        xn = xn_ref[...]
        wg, wu = wg_ref[0], wu_ref[0]
        gate = _dot(xn, wg)
        up = _dot(xn, wu)
        sg = _sigmoid(gate)
        dact = _dot_nt(dhalf_ref[...], wd_ref[...])
        act_ref[0] = (gate * sg * up).astype(BF16)
        dgate = (dact * up * (sg * (1.0 + gate * (1.0 - sg)))).astype(BF16)
        dup = (dact * gate * sg).astype(BF16)
        dgate_ref[0] = dgate
        dup_ref[0] = dup
        dxn_s[...] += _dot_nt(dgate, wg) + _dot_nt(dup, wu)

        @pl.when((i == 0) & (j == 0))
        def _():
            dg_ref[...] = jnp.zeros_like(dg_ref)

        @pl.when(j == nj - 1)
        def _():
            x = h_ref[...]
            r = lax.rsqrt(jnp.mean(x * x, axis=-1, keepdims=True) + EPS)
            dxn = dxn_s[...]
            dyg = dxn * g_ref[...]
            dh = do_ref[...] + r * dyg - x * (r * r * r) * jnp.mean(dyg * x, axis=-1, keepdims=True)
            dh_ref[...] = dh
            dhb_ref[...] = dh.astype(BF16)
            dg_ref[...] += jnp.sum(dxn * x * r, axis=0, keepdims=True)

    row_d = pl.BlockSpec((tb, d), lambda i, j: (i, 0))
    slab = pl.BlockSpec((1, tb, fc), lambda i, j: (j, i, 0))
    hidden = jax.ShapeDtypeStruct((nj, rows, fc), BF16)
    return pl.pallas_call(
        body, name=name, grid=(rows // tb, nj),
        in_specs=[row_d, pl.BlockSpec((1, d), lambda i, j: (0, 0)), row_d] + _ffn_specs(d, fc, nj),
        out_specs=[row_d, row_d, pl.BlockSpec((1, d), lambda i, j: (0, 0)), row_d, slab, slab, slab, row_d],
        out_shape=[jax.ShapeDtypeStruct((rows, d), F32), jax.ShapeDtypeStruct((rows, d), BF16),
                   jax.ShapeDtypeStruct((1, d), F32), jax.ShapeDtypeStruct((rows, d), BF16),
                   hidden, hidden, hidden, jax.ShapeDtypeStruct((rows, d), BF16)],
        scratch_shapes=[pltpu.VMEM((tb, d), F32)],
        compiler_params=_cp("arbitrary", "arbitrary"),
    )(h, gain, dout, wgu, wgu, wd)


def _shift_rows(x, s):
    return pltpu.roll(x, s % x.shape[0], 0)


def _a_post(c, which):
    s = _silu(c)
    n = s * lax.rsqrt(jnp.sum(s * s, axis=-1, keepdims=True) + 1e-6)
    scale = jnp.where(which == 0, HEAD_A ** -0.5, 1.0)
    return jnp.where(which == 2, s, n * scale)


def _conv(x, w):
    return x * w[3:4] + _shift_rows(x, 1) * w[2:3] + _shift_rows(x, 2) * w[1:2] + _shift_rows(x, 3) * w[0:1]


def _a_pre_fwd(zqkv, conv_w):
    rows, width = zqkv.shape
    heads = width // (3 * HEAD_A)

    def body(x_ref, w_ref, o_ref):
        which = pl.program_id(0) // heads
        live = lax.broadcasted_iota(jnp.int32, (rows, HEAD_A), 0) >= PAD
        o_ref[...] = jnp.where(live, _a_post(_conv(x_ref[...], w_ref[...]), which), 0.0)

    return pl.pallas_call(
        body, name="a_pre_fwd", grid=(width // HEAD_A,),
        in_specs=[pl.BlockSpec((rows, HEAD_A), lambda c: (0, c)), pl.BlockSpec((4, HEAD_A), lambda c: (0, c))],
        out_specs=pl.BlockSpec((rows, HEAD_A), lambda c: (0, c)),
        out_shape=jax.ShapeDtypeStruct((rows, width), F32),
        compiler_params=_cp("parallel"),
    )(zqkv, conv_w)


def _a_pre_bwd(zqkv, conv_w, dqkv):
    rows, width = zqkv.shape
    heads = width // (3 * HEAD_A)

    def body(x_ref, w_ref, ct_ref, dx_ref, dw_ref):
        which = pl.program_id(0) // heads
        live = lax.broadcasted_iota(jnp.int32, (rows, HEAD_A), 0) >= PAD
        x, w = x_ref[...], w_ref[...]
        _, vjp = jax.vjp(lambda c: _a_post(c, which), _conv(x, w))
        (dc,) = vjp(jnp.where(live, ct_ref[...], 0.0))
        dc = jnp.where(live, dc, 0.0)
        dx_ref[...] = (dc * w[3:4] + _shift_rows(dc, -1) * w[2:3] + _shift_rows(dc, -2) * w[1:2]
                       + _shift_rows(dc, -3) * w[0:1]).astype(BF16)
        dw_ref[...] = jnp.concatenate(
            [jnp.sum(dc * (_shift_rows(x, 3 - j) if j < 3 else x), axis=0, keepdims=True) for j in range(4)], axis=0)

    col = pl.BlockSpec((rows, HEAD_A), lambda c: (0, c))
    wsp = pl.BlockSpec((4, HEAD_A), lambda c: (0, c))
    return pl.pallas_call(
        body, name="a_pre_bwd", grid=(width // HEAD_A,),
        in_specs=[col, wsp, col], out_specs=[col, wsp],
        out_shape=[jax.ShapeDtypeStruct((rows, width), BF16), jax.ShapeDtypeStruct((4, width), F32)],
        compiler_params=_cp("parallel"),
    )(zqkv, conv_w, dqkv)


SHIFT_TILE = 384


def _shift_fwd(zb, mu):
    rows, width = zb.shape

    def body(x_ref, mu_ref, o_ref):
        x = x_ref[...]
        first = lax.broadcasted_iota(jnp.int32, x.shape, 0) == 0
        prev = jnp.where(first, 0.0, _shift_rows(x, 1))
        o_ref[...] = x + (prev - x) * mu_ref[...]

    col = pl.BlockSpec((rows, SHIFT_TILE), lambda c: (0, c))
    return pl.pallas_call(
        body, name="shift_fwd", grid=(width // SHIFT_TILE,),
        in_specs=[col, pl.BlockSpec((1, SHIFT_TILE), lambda c: (0, c))], out_specs=col,
        out_shape=jax.ShapeDtypeStruct((rows, width), F32), compiler_params=_cp("parallel"),
    )(zb, mu)


def _shift_bwd(zb, mu, dzf):
    rows, width = zb.shape

    def body(x_ref, mu_ref, ct_ref, dx_ref, dmu_ref):
        x, ct, mu_v = x_ref[...], ct_ref[...], mu_ref[...]
        row = lax.broadcasted_iota(jnp.int32, x.shape, 0)
        prev = jnp.where(row == 0, 0.0, _shift_rows(x, 1))
        nxt = jnp.where(row == rows - 1, 0.0, _shift_rows(ct, -1))
        dx_ref[...] = (ct * (1.0 - mu_v) + nxt * mu_v).astype(BF16)
        dmu_ref[...] = jnp.sum(ct * (prev - x), axis=0, keepdims=True)

    col = pl.BlockSpec((rows, SHIFT_TILE), lambda c: (0, c))
    msp = pl.BlockSpec((1, SHIFT_TILE), lambda c: (0, c))
    return pl.pallas_call(
        body, name="shift_bwd", grid=(width // SHIFT_TILE,),
        in_specs=[col, msp, col], out_specs=[col, msp],
        out_shape=[jax.ShapeDtypeStruct((rows, width), BF16), jax.ShapeDtypeStruct((1, width), F32)],
        compiler_params=_cp("parallel"),
    )(zb, mu, dzf)


def _dn_chunk(q, k, v, beta, g, state):
    heads = range(len(q))
    ri = lax.broadcasted_iota(jnp.int32, (CHUNK, CHUNK), 0)
    ci = lax.broadcasted_iota(jnp.int32, (CHUNK, CHUNK), 1)
    eye = (ri == ci).astype(F32)
    incl = ri >= ci
    last = lax.broadcasted_iota(jnp.int32, (CHUNK, 1), 0) == CHUNK - 1
    g_row = [jnp.sum(g[h] * eye, axis=0, keepdims=True) for h in heads]
    gc = [jnp.sum(jnp.where(incl, g_row[h], 0.0), axis=1, keepdims=True) for h in heads]
    gc_row = [jnp.sum(gc[h] * eye, axis=0, keepdims=True) for h in heads]
    decay = [jnp.where(incl, jnp.exp(jnp.where(incl, gc[h] - gc_row[h], 0.0)), 0.0) for h in heads]
    kb = [k[h] * beta[h] for h in heads]
    vb = [v[h] * beta[h] for h in heads]
    p = [-jnp.where(ri > ci, _dot_nt(kb[h], k[h], DN_PRECISION) * decay[h], 0.0) for h in heads]
    tinv = [eye + p[h] for h in heads]
    for _ in range(5):
        p = [_dot(p[h], p[h], DN_PRECISION) for h in heads]
        tinv = [tinv[h] + _dot(tinv[h], p[h], DN_PRECISION) for h in heads]
    eg = [jnp.exp(gc[h]) for h in heads]
    u = [_dot(tinv[h], vb[h], DN_PRECISION) for h in heads]
    wk = [_dot(tinv[h], kb[h] * eg[h], DN_PRECISION) for h in heads]
    attn = [_dot_nt(q[h], k[h], DN_PRECISION) * decay[h] for h in heads]
    g_last = [jnp.sum(jnp.where(last, gc[h], 0.0), axis=0, keepdims=True) for h in heads]
    k_tail = [k[h] * jnp.exp(g_last[h] - gc[h]) for h in heads]
    v_new = [u[h] - _dot(wk[h], state[h], DN_PRECISION) for h in heads]
    o = [_dot(q[h] * eg[h], state[h], DN_PRECISION) + _dot(attn[h], v_new[h], DN_PRECISION) for h in heads]
    new = [state[h] * jnp.exp(g_last[h]) + _dot_tn(k_tail[h], v_new[h], DN_PRECISION) for h in heads]
    return o, new


def _bg_cols(bg, h, heads):
    lane = lax.broadcasted_iota(jnp.int32, bg.shape, 1)
    beta = jnp.sum(jnp.where(lane == h, bg, 0.0), axis=1, keepdims=True)
    g = jnp.sum(jnp.where(lane == heads + h, bg, 0.0), axis=1, keepdims=True)
    return beta, g


def _dn_fwd(qkv, bg):
    rows = qkv.shape[0]
    heads = qkv.shape[1] // (3 * HEAD_A)
    n = rows // CHUNK
    hp, groups = heads, 1

    def body(q_ref, k_ref, v_ref, bg_ref, o_ref, hist_ref, s_ref):
        c, grp = pl.program_id(0), pl.program_id(1)

        @pl.when(c == 0)
        def _():
            for i in range(hp):
                s_ref[grp * hp + i] = jnp.zeros((HEAD_A, HEAD_A), F32)

        bg_v = bg_ref[...]
        cols = [slice(i * HEAD_A, (i + 1) * HEAD_A) for i in range(hp)]
        state = [s_ref[grp * hp + i] for i in range(hp)]
        beta_g = [_bg_cols(bg_v, grp * hp + i, heads) for i in range(hp)]
        o, new = _dn_chunk([q_ref[:, c_] for c_ in cols], [k_ref[:, c_] for c_ in cols], [v_ref[:, c_] for c_ in cols],
                           [b for b, _ in beta_g], [g for _, g in beta_g], state)
        for i in range(hp):
            hist_ref[0, i] = state[i]
            o_ref[:, cols[i]] = o[i]
            s_ref[grp * hp + i] = new[i]

    def part(p):
        return pl.BlockSpec((CHUNK, hp * HEAD_A), lambda c, grp: (c, p * groups + grp))

    return pl.pallas_call(
        body, name="deltanet_fwd", grid=(n, groups),
        in_specs=[part(0), part(1), part(2), pl.BlockSpec((CHUNK, 128), lambda c, grp: (c, 0))],
        out_specs=[part(0), pl.BlockSpec((1, hp, HEAD_A, HEAD_A), lambda c, grp: (c, grp, 0, 0))],
        out_shape=[jax.ShapeDtypeStruct((rows, heads * HEAD_A), F32),
                   jax.ShapeDtypeStruct((n, heads, HEAD_A, HEAD_A), F32)],
        scratch_shapes=[pltpu.VMEM((heads, HEAD_A, HEAD_A), F32)],
        compiler_params=_cp("arbitrary", "arbitrary"),
    )(qkv, qkv, qkv, bg)


def _dn_bwd(qkv, bg, hist, do):
    rows = qkv.shape[0]
    heads = qkv.shape[1] // (3 * HEAD_A)
    n = rows // CHUNK
    hp, groups = heads, 1

    def body(q_ref, k_ref, v_ref, bg_ref, hist_ref, do_ref, dqkv_ref, dbg_ref, ds_ref):
        c, grp = pl.program_id(0), pl.program_id(1)

        @pl.when(c == 0)
        def _():
            for i in range(hp):
                ds_ref[grp * hp + i] = jnp.zeros((HEAD_A, HEAD_A), F32)

        bg_v = bg_ref[...]
        lane = lax.broadcasted_iota(jnp.int32, (CHUNK, 128), 1)
        cols = [slice(i * HEAD_A, (i + 1) * HEAD_A) for i in range(hp)]
        beta_g = [_bg_cols(bg_v, grp * hp + i, heads) for i in range(hp)]
        _, vjp = jax.vjp(_dn_chunk, [q_ref[:, c_] for c_ in cols], [k_ref[:, c_] for c_ in cols],
                         [v_ref[:, c_] for c_ in cols], [b for b, _ in beta_g], [g for _, g in beta_g],
                         [hist_ref[0, i] for i in range(hp)])
        dq, dk, dv, dbeta, dg, ds = vjp(([do_ref[:, c_] for c_ in cols], [ds_ref[grp * hp + i] for i in range(hp)]))
        dbg = jnp.zeros((CHUNK, 128), F32)
        for i in range(hp):
            h = grp * hp + i
            for p, part_grad in enumerate((dq, dk, dv)):
                dqkv_ref[:, pl.ds((p * heads + i) * HEAD_A, HEAD_A)] = part_grad[i]
            ds_ref[h] = ds[i]
            dbg = dbg + jnp.where(lane == h, dbeta[i], 0.0) + jnp.where(lane == heads + h, dg[i], 0.0)

        @pl.when(grp == 0)
        def _():
            dbg_ref[...] = jnp.zeros_like(dbg_ref)

        dbg_ref[...] += dbg

    def part(p):
        return pl.BlockSpec((CHUNK, hp * HEAD_A), lambda c, grp: (n - 1 - c, p * groups + grp))

    return pl.pallas_call(
        body, name="deltanet_bwd", grid=(n, groups),
        in_specs=[part(0), part(1), part(2), pl.BlockSpec((CHUNK, 128), lambda c, grp: (n - 1 - c, 0)),
                  pl.BlockSpec((1, hp, HEAD_A, HEAD_A), lambda c, grp: (n - 1 - c, grp, 0, 0)), part(0)],
        out_specs=[pl.BlockSpec((CHUNK, 3 * heads * HEAD_A), lambda c, grp: (n - 1 - c, 0)),
                   pl.BlockSpec((CHUNK, 128), lambda c, grp: (n - 1 - c, 0))],
        out_shape=[jax.ShapeDtypeStruct(qkv.shape, F32), jax.ShapeDtypeStruct((rows, 128), F32)],
        scratch_shapes=[pltpu.VMEM((heads, HEAD_A, HEAD_A), F32)],
        compiler_params=_cp("arbitrary", "arbitrary"),
    )(qkv, qkv, qkv, bg, hist, do)


def _head_mask(heads, width):
    return (lax.broadcasted_iota(jnp.int32, (heads, width), 0)
            == lax.broadcasted_iota(jnp.int32, (heads, width), 1) // HEAD_B)


def _masked_rows(mask, row):
    return jnp.where(mask, row, 0.0).astype(BF16)


def _rwkv_fwd(r, w, k, v, a, b):
    rows, width = r.shape
    heads = width // HEAD_B
    ts = SCAN_STEPS

    def body(r_ref, w_ref, k_ref, v_ref, a_ref, b_ref, y_ref, hist_ref, s_ref):
        @pl.when(pl.program_id(0) == 0)
        def _():
            s_ref[...] = jnp.zeros_like(s_ref)

        mask = _head_mask(heads, width)
        onehot = mask.astype(BF16)
        onehot2 = jnp.concatenate([onehot, onehot], axis=0)
        bd = _block_diag_ones()

        spread_v = [_dot_tn(jnp.concatenate(_hi_lo(v_ref[j]), axis=0), onehot2) for j in range(ts)]
        s = s_ref[...]
        ys = []
        for j in range(ts):
            row = pl.ds(j, 1)
            hist_ref[j] = s
            (sa,) = _segsum_many([((s * a_ref[row, :]).astype(BF16),)], bd)
            s = s * w_ref[row, :] + sa * b_ref[row, :] + spread_v[j] * k_ref[row, :]
            ys.append(_dot_nt(_masked_rows(mask, r_ref[row, :]), s.astype(BF16)))
        for j in range(ts):
            y_ref[j] = ys[j]
        s_ref[...] = s

    blk = pl.BlockSpec((ts, width), lambda i: (i, 0))
    blk3 = pl.BlockSpec((ts, heads, HEAD_B), lambda i: (i, 0, 0))
    return pl.pallas_call(
        body, name="rwkv_fwd", grid=(rows // ts,),
        in_specs=[blk, blk, blk, blk3, blk, blk],
        out_specs=[blk3, pl.BlockSpec((ts, HEAD_B, width), lambda i: (i, 0, 0)),
                   pl.BlockSpec((HEAD_B, width), lambda i: (0, 0))],
        out_shape=[jax.ShapeDtypeStruct((rows, heads, HEAD_B), F32), jax.ShapeDtypeStruct((rows, HEAD_B, width), F32),
                   jax.ShapeDtypeStruct((HEAD_B, width), F32)],
        compiler_params=_cp("arbitrary"),
    )(r, w, k, v, a, b)


def _rwkv_bwd(r, w, k, v, a, b, hist, last, dy):
    rows, width = r.shape
    heads = width // HEAD_B
    ts = SCAN_STEPS
    nb = rows // ts

    def body(r_ref, w_ref, k_ref, v_ref, a_ref, b_ref, hist_ref, last_ref, dy_ref,
             dr_ref, dw_ref, dk_ref, dv_ref, da_ref, db_ref, g_ref, after_ref):
        @pl.when(pl.program_id(0) == 0)
        def _():
            g_ref[...] = jnp.zeros_like(g_ref)
            after_ref[...] = last_ref[...]

        mask = _head_mask(heads, width)
        onehot = mask.astype(BF16)
        bd = _block_diag_ones()

        def own_lanes(x):
            return jnp.sum(jnp.where(mask, x, 0.0), axis=0, keepdims=True)

        def colsum(x):
            return jnp.sum(x, axis=0, keepdims=True)

        dy_m = [dy_ref[j].astype(BF16) for j in range(ts)]
        spread_dy = [_dot_tn(dy_m[j], onehot) for j in range(ts)]
        state_after = [hist_ref[j + 1] if j < ts - 1 else after_ref[...] for j in range(ts)]
        dr = [own_lanes(_dot(dy_m[j], state_after[j].astype(BF16))) for j in range(ts)]
        sa_m = [_dot_nt(_masked_rows(mask, a_ref[pl.ds(j, 1), :]), hist_ref[j].astype(BF16)) for j in range(ts)]
        g = g_ref[...]
        dw, dk, db, da, dv = {}, {}, {}, {}, {}
        for j in reversed(range(ts)):
            row = pl.ds(j, 1)
            sp = hist_ref[j]
            g = g + spread_dy[j] * r_ref[row, :]
            (dsa,) = _segsum_many([((g * b_ref[row, :]).astype(BF16),)], bd)
            g_b = g.astype(BF16)
            both = _dot(jnp.concatenate([v_ref[j].astype(BF16), sa_m[j].astype(BF16)], axis=0), g_b)
            dk[j], db[j] = own_lanes(both[:heads]), own_lanes(both[heads:])
            dv[j] = _dot_nt(_masked_rows(mask, k_ref[row, :]), g_b)
            dw[j] = colsum(g * sp)
            da[j] = colsum(sp * dsa)
            g = g * w_ref[row, :] + dsa * a_ref[row, :]
        g_ref[...] = g
        after_ref[...] = hist_ref[0]
        for j in range(ts):
            dv_ref[j] = dv[j]
            for ref, vals in ((dr_ref, dr), (dw_ref, dw), (dk_ref, dk), (da_ref, da), (db_ref, db)):
                ref[pl.ds(j, 1), :] = vals[j]

    blk = pl.BlockSpec((ts, width), lambda i: (nb - 1 - i, 0))
    blk3 = pl.BlockSpec((ts, heads, HEAD_B), lambda i: (nb - 1 - i, 0, 0))
    state = pl.BlockSpec((HEAD_B, width), lambda i: (0, 0))
    return pl.pallas_call(
        body, name="rwkv_bwd", grid=(nb,),
        in_specs=[blk, blk, blk, blk3, blk, blk, pl.BlockSpec((ts, HEAD_B, width), lambda i: (nb - 1 - i, 0, 0)),
                  state, blk3],
        out_specs=[blk, blk, blk, blk3, blk, blk],
        out_shape=[jax.ShapeDtypeStruct((rows, width), F32)] * 3 + [jax.ShapeDtypeStruct((rows, heads, HEAD_B), F32)]
        + [jax.ShapeDtypeStruct((rows, width), F32)] * 2,
        scratch_shapes=[pltpu.VMEM((HEAD_B, width), F32), pltpu.VMEM((HEAD_B, width), F32)],
        compiler_params=_cp("arbitrary"),
    )(r, w, k, v, a, b, hist, last, dy)


def _live(row0, shape):
    return (row0 + lax.broadcasted_iota(jnp.int32, shape, 0)) >= PAD


def _norm_fn(row0, h, gain):
    return (_rms(h, gain),)


def _norm_res_fn(row0, h, gain):
    return _rms(h, gain), h


def _make_bg_fn(heads):
    def fn(row0, x, log_rate, dt_bias):
        lane = lax.broadcasted_iota(jnp.int32, x.shape, 1)
        beta = _sigmoid(x)
        g = -jnp.exp(log_rate) * _softplus(x + dt_bias)
        out = jnp.where(lane < heads, beta, jnp.where(lane < 2 * heads, g, 0.0))
        return (jnp.where(_live(row0, x.shape), out, 0.0),)
    return fn


def _b_pre_fn(row0, zf, w0, w_up, a0, a_up, g_up, k_k, k_a):
    d = w0.shape[1]
    r, k, v = zf[:, :d], zf[:, d:2 * d], zf[:, 2 * d:3 * d]
    lo = zf[:, 3 * d:3 * d + 128]
    lg = zf[:, 3 * d + 128:3 * d + LORA_PAD]
    lane = lax.broadcasted_iota(jnp.int32, lo.shape, 1)
    lw = _dot(jnp.where(lane < LORA_W, jnp.tanh(lo), 0.0), w_up)
    la = _dot(jnp.where(lane >= LORA_W, lo, 0.0), a_up)
    lane_g = lax.broadcasted_iota(jnp.int32, lg.shape, 1)
    gate = _dot(jnp.where(lane_g < LORA_G, _sigmoid(lg), 0.0), g_up)
    decay = jnp.exp(-jnp.exp(-_softplus(-(w0 + lw)) - 0.5))
    a = _sigmoid(a0 + la)
    kx = k * k_k
    kk = kx * lax.rsqrt(_segsum64(kx * kx) + 1e-6)
    k2 = k * (1.0 + (a - 1.0) * k_a)
    return r, decay, k2, v, -kk, kk * a, gate


def _post_fn(row0, o, zg, y, r, k2, v, gate, out_gain, r_k, ln_g, ln_b):
    d = o.shape[1]
    az, ga, gb = zg[:, :d], zg[:, d:2 * d], zg[:, 2 * d:]
    heads = d // HEAD_A
    parts = []
    for h in range(heads):
        oh = o[:, h * HEAD_A:(h + 1) * HEAD_A]
        parts.append(oh * lax.rsqrt(jnp.mean(oh * oh, axis=-1, keepdims=True) + EPS) * out_gain)
    o_a = jnp.concatenate(parts, axis=1) * _silu(az)
    mean = _segsum64(y) * (1.0 / HEAD_B)
    yc = y - mean
    var = _segsum64(yc * yc) * (1.0 / HEAD_B)
    yn = yc * lax.rsqrt(var + GN_EPS) * ln_g + ln_b
    o_b = (yn + _segsum64(r * k2 * r_k) * v) * gate
    return (_sigmoid(ga) * o_a + _sigmoid(gb) * o_b,)


def _loss(h3, target, gain, tb):
    rows, d = h3.shape

    def body(h_ref, t_ref, g_ref, dh_ref, dg_ref, l_ref):
        i = pl.program_id(0)
        live = (i * tb + lax.broadcasted_iota(jnp.int32, (tb, 1), 0)) >= CHUNK
        tgt = t_ref[...]

        def f(h, g):
            err = _rms(h, g) - tgt
            return 0.5 * jnp.sum(jnp.where(live, jnp.mean(err * err, axis=-1, keepdims=True), 0.0))

        val, vjp = jax.vjp(f, h_ref[...], g_ref[...])
        dh, dg = vjp(jnp.ones((), F32))
        dh_ref[...] = dh

        @pl.when(i == 0)
        def _():
            dg_ref[...] = jnp.zeros_like(dg_ref)
            l_ref[...] = jnp.zeros_like(l_ref)

        dg_ref[...] += dg
        l_ref[...] += jnp.full((1, 128), val, F32)

    blk = pl.BlockSpec((tb, d), lambda i: (i, 0))
    return pl.pallas_call(
        body, name="loss", grid=(rows // tb,),
        in_specs=[blk, blk, pl.BlockSpec((1, d), lambda i: (0, 0))],
        out_specs=[blk, pl.BlockSpec((1, d), lambda i: (0, 0)), pl.BlockSpec((1, 128), lambda i: (0, 0))],
        out_shape=[jax.ShapeDtypeStruct((rows, d), F32), jax.ShapeDtypeStruct((1, d), F32),
                   jax.ShapeDtypeStruct((1, 128), F32)],
        compiler_params=_cp("arbitrary"),
    )(h3, target, gain)


def _adamw_math(w, g, m, v):
    m2 = ADAM_B1 * m + (1.0 - ADAM_B1) * g
    v2 = ADAM_B2 * v + (1.0 - ADAM_B2) * (g * g)
    m_hat = m2 / (1.0 - ADAM_B1 ** ADAM_STEP)
    v_hat = v2 / (1.0 - ADAM_B2 ** ADAM_STEP)
    return -ADAM_LR * (m_hat / (jnp.sqrt(v_hat) + ADAM_EPS) + ADAM_WD * w), m2, v2


def _adamw(name, own, landed, w, m, v, rb):
    rows, cols = w.shape

    def body(o_ref, s_ref, w_ref, m_ref, v_ref, g_ref, d_ref, m2_ref, v2_ref):
        g = o_ref[...].astype(F32)
        for peer in range(N_DEV - 1):
            g = g + s_ref[peer].astype(F32)
        g_ref[...] = g
        d_ref[...], m2_ref[...], v2_ref[...] = _adamw_math(w_ref[...], g, m_ref[...], v_ref[...])

    blk = pl.BlockSpec((rb, cols), lambda i: (i, 0))
    return pl.pallas_call(
        body, name=name, grid=(rows // rb,),
        in_specs=[blk, pl.BlockSpec((N_DEV - 1, rb, cols), lambda i: (0, i, 0)), blk, blk, blk],
        out_specs=[blk] * 4, out_shape=[jax.ShapeDtypeStruct((rows, cols), F32)] * 4,
        compiler_params=_cp("parallel"),
    )(own, landed, w, m, v)


def _sum_slabs(name, slabs, rb):
    _, rows, cols = slabs.shape

    def body(s_ref, o_ref):
        g = s_ref[0]
        for dev in range(1, N_DEV):
            g = g + s_ref[dev]
        o_ref[...] = g

    return pl.pallas_call(
        body, name=name, grid=(rows // rb,),
        in_specs=[pl.BlockSpec((N_DEV, rb, cols), lambda i: (0, i, 0))],
        out_specs=pl.BlockSpec((rb, cols), lambda i: (i, 0)),
        out_shape=jax.ShapeDtypeStruct((rows, cols), F32), compiler_params=_cp("parallel"),
    )(slabs)


def _adamw_small(w, g, m, v):
    def body(w_ref, g_ref, m_ref, v_ref, d_ref, m2_ref, v2_ref):
        d_ref[...], m2_ref[...], v2_ref[...] = _adamw_math(w_ref[...], g_ref[...], m_ref[...], v_ref[...])

    return pl.pallas_call(body, name="adamw_small", out_shape=[jax.ShapeDtypeStruct(w.shape, F32)] * 3)(w, g, m, v)


def _place():
    return lax.axis_index("x"), lax.axis_index("y"), lax.axis_index("c")


def _index(p):
    return 4 * p[0] + 2 * p[1] + p[2]


def _all_gather(name, xs):
    n = len(xs)

    def body(*refs):
        x_refs, o_refs = refs[:n], refs[n:2 * n]
        send_sems, recv_sems, local_sems = refs[2 * n:]
        x, y, c = _place()
        me, sibling = (x, y, c), (x, y, 1 - c)
        chips = [(1 - x, y), (x, 1 - y), (1 - x, 1 - y)]

        def copy(i, k, block, to, src=None):
            dst = o_refs[i].at[_index(block)]
            return pltpu.make_async_remote_copy(src_ref=dst if src is None else src, dst_ref=dst,
                                                send_sem=send_sems.at[i, k], recv_sem=recv_sems.at[i, k],
                                                device_id=to, device_id_type=MESH_ID)

        mine = [pltpu.make_async_copy(x_refs[i], o_refs[i].at[_index(me)], local_sems.at[i]) for i in range(n)]
        for cp in mine:
            cp.start()
        first = []
        for i in range(n):
            first.append(copy(i, 0, me, sibling, src=x_refs[i]))
            first += [copy(i, 1 + j, me, (*chip, c), src=x_refs[i]) for j, chip in enumerate(chips)]
        for cp in first:
            cp.start()
        passed = []
        for j, chip in enumerate(chips):
            for i in range(n):
                copy(i, 1 + j, (*chip, c), me).wait_recv()
                cp = copy(i, 4 + j, (*chip, c), sibling)
                cp.start()
                passed.append(cp)
        for i in range(n):
            copy(i, 0, sibling, me).wait_recv()
            for j, chip in enumerate(chips):
                copy(i, 4 + j, (*chip, 1 - c), me).wait_recv()
        for cp in first + passed:
            cp.wait_send()
        for cp in mine:
            cp.wait()

    return pl.pallas_call(
        body, name=name, in_specs=[ANY] * n, out_specs=[ANY] * n,
        out_shape=[jax.ShapeDtypeStruct((N_DEV,) + x.shape, x.dtype) for x in xs],
        scratch_shapes=[pltpu.SemaphoreType.DMA((n, 7)), pltpu.SemaphoreType.DMA((n, 7)), pltpu.SemaphoreType.DMA((n,))],
    )(*xs)


def _exchange_start(name, xs, after=None):
    n = len(xs)
    copies = n * (N_DEV - 1)
    extra = [] if after is None else [after]

    def body(*refs):
        x_refs, land_refs = refs[:n], refs[n:2 * n]
        sems = refs[2 * n + len(extra):2 * n + len(extra) + 2 * copies]
        token = refs[-1]
        for i, k, peer in _exchange_copies(n):
            _exchange_copy(x_refs, land_refs, sems, i, k, peer).start()
        token[...] = jnp.zeros_like(token)

    lands = [lax.empty((N_DEV - 1,) + x.shape[1:], x.dtype) for x in xs]
    out = pl.pallas_call(
        body, name=name,
        out_shape=(*[pltpu.SemaphoreType.DMA(())] * (2 * copies), *[pltpu.HBM(x.shape, x.dtype) for x in xs],
                   *[pltpu.HBM(l.shape, l.dtype) for l in lands], jax.ShapeDtypeStruct((8, 128), F32)),
        in_specs=[HBM_SPEC] * (2 * n) + [ANY] * len(extra),
        out_specs=(*[SEM_SPEC] * (2 * copies), *[HBM_SPEC] * (2 * n), pl.BlockSpec(memory_space=pltpu.VMEM)),
        input_output_aliases={i: 2 * copies + i for i in range(2 * n)},
        compiler_params=pltpu.CompilerParams(has_side_effects=pltpu.SideEffectType.DATAFLOW_SIDE_EFFECTING),
    )(*[pltpu.with_memory_space_constraint(a, pltpu.HBM) for a in list(xs) + lands], *extra)
    sems, rest = list(out[:2 * copies]), out[2 * copies:]
    return sems, list(rest[:n]), list(rest[n:2 * n]), rest[-1]


def _exchange_copies(n):
    x, y, c = _place()
    for k in range(1, N_DEV):
        peer = ((1 - x) if k & 4 else x, (1 - y) if k & 2 else y, (1 - c) if k & 1 else c)
        for i in range(n):
            yield i, k - 1, peer


def _exchange_copy(x_refs, land_refs, sems, i, k, peer):
    copies = len(sems) // 2
    which = i * (N_DEV - 1) + k
    return pltpu.make_async_remote_copy(src_ref=x_refs[i].at[_index(peer)], dst_ref=land_refs[i].at[k],
                                        send_sem=sems[which], recv_sem=sems[copies + which],
                                        device_id=peer, device_id_type=MESH_ID)


def _exchange_wait(name, sems, xs, lands, after):
    n = len(xs)

    def body(*refs):
        x_refs, land_refs = refs[:n], refs[n:2 * n]
        sem_refs = refs[2 * n:2 * n + len(sems)]
        for i, k, peer in _exchange_copies(n):
            copy = _exchange_copy(x_refs, land_refs, sem_refs, i, k, peer)
            copy.wait_send()
            copy.wait_recv()

    out = pl.pallas_call(
        body, name=name,
        out_shape=(*[pltpu.HBM(x.shape, x.dtype) for x in xs], *[pltpu.HBM(l.shape, l.dtype) for l in lands]),
        in_specs=[HBM_SPEC] * (2 * n) + [SEM_SPEC] * len(sems) + [ANY], out_specs=tuple([HBM_SPEC] * (2 * n)),
        input_output_aliases={i: i for i in range(2 * n)},
        compiler_params=pltpu.CompilerParams(has_side_effects=pltpu.SideEffectType.DATAFLOW_SIDE_EFFECTING),
    )(*xs, *lands, *sems, after)
    return list(out[:n]), list(out[n:])


def _pack(arrays):
    flat = jnp.concatenate([a.reshape(-1) for a in arrays])
    pad = (-flat.shape[0]) % 1024
    return jnp.pad(flat, (0, pad)).reshape(-1, 128)


def _unpack(packed, shapes):
    flat = packed.reshape(-1)
    out, pos = [], 0
    for s in shapes:
        size = 1
        for dim in s:
            size *= dim
        out.append(flat[pos:pos + size].reshape(s))
        pos += size
    return out


def _cols_from_slabs(stack):
    return jnp.transpose(stack, (1, 0, 2)).reshape(stack.shape[1], -1)


def _cols_to_slabs(full):
    return jnp.transpose(full.reshape(full.shape[0], N_DEV, -1), (1, 0, 2))


def kernel(x, meta_tokens, ffn1_norm, ffn1_w_gu, ffn1_w_down, mix_norm, w_in, a_conv_w, a_log_rate, a_dt_bias, a_out_norm, b_shift_mu, b_w0, b_w_up, b_a0, b_a_up, b_g_up, b_k_k, b_k_a, b_r_k, b_ln_gain, b_ln_bias, w_out, ffn2_norm, ffn2_w_gu, ffn2_w_down, final_norm, loss_target, m_meta_tokens, m_ffn1_norm, m_ffn1_w_gu, m_ffn1_w_down, m_mix_norm, m_w_in, m_a_conv_w, m_a_log_rate, m_a_dt_bias, m_a_out_norm, m_b_shift_mu, m_b_w0, m_b_w_up, m_b_a0, m_b_a_up, m_b_g_up, m_b_k_k, m_b_k_a, m_b_r_k, m_b_ln_gain, m_b_ln_bias, m_w_out, m_ffn2_norm, m_ffn2_w_gu, m_ffn2_w_down, m_final_norm, v_meta_tokens, v_ffn1_norm, v_ffn1_w_gu, v_ffn1_w_down, v_mix_norm, v_w_in, v_a_conv_w, v_a_log_rate, v_a_dt_bias, v_a_out_norm, v_b_shift_mu, v_b_w0, v_b_w_up, v_b_a0, v_b_a_up, v_b_g_up, v_b_k_k, v_b_k_a, v_b_r_k, v_b_ln_gain, v_b_ln_bias, v_w_out, v_ffn2_norm, v_ffn2_w_gu, v_ffn2_w_down, v_final_norm):
    names = ['meta_tokens', 'ffn1_norm', 'ffn1_w_gu', 'ffn1_w_down', 'mix_norm', 'w_in', 'a_conv_w', 'a_log_rate',
             'a_dt_bias', 'a_out_norm', 'b_shift_mu', 'b_w0', 'b_w_up', 'b_a0', 'b_a_up', 'b_g_up', 'b_k_k', 'b_k_a',
             'b_r_k', 'b_ln_gain', 'b_ln_bias', 'w_out', 'ffn2_norm', 'ffn2_w_gu', 'ffn2_w_down', 'final_norm']
    env = dict(locals())
    wts = {k: env[k] for k in names}
    mom_m = {k: env['m_' + k] for k in names}
    mom_v = {k: env['v_' + k] for k in names}
    big = ['ffn1_w_gu', 'ffn1_w_down', 'w_in', 'w_out', 'ffn2_w_gu', 'ffn2_w_down']
    col_sharded = {'ffn1_w_gu', 'w_in', 'ffn2_w_gu'}
    small_sharded = ['meta_tokens', 'a_conv_w', 'b_w_up', 'b_a_up', 'b_g_up']
    replicated = [k for k in names if k not in big and k not in small_sharded]

    seq, d = x.shape[1], x.shape[2]
    rows = PAD + N_META + seq
    heads_a = d // HEAD_A
    tb_mm = _tb(rows, 416)
    tb_vjp = _tb(rows, 208)
    me = _index(_place())

    big_local = [wts[k][0].astype(BF16) for k in big]
    small_local = [wts['meta_tokens']] + [wts[k][0] for k in small_sharded[1:]]
    gathered = _all_gather("gather_weights", big_local + small_local)
    gw = dict(zip(big + small_sharded, gathered))
    full = {k: (gw[k] if k in ('ffn1_w_gu', 'ffn2_w_gu') else _cols_from_slabs(gw[k]) if k in col_sharded
                else gw[k].reshape(-1, gw[k].shape[-1])) for k in big}
    for k in small_sharded:
        full[k] = _cols_from_slabs(gw[k])
    for k in replicated:
        full[k] = wts[k].reshape(1, -1)

    win = full['w_in']
    n_b = 3 * d + LORA_W + LORA_A + LORA_G
    off_beta, off_b = 4 * d, 4 * d + 2 * heads_a
    off_ga = off_b + n_b
    b_width = 3 * d + LORA_PAD
    zcols = lambda r, c: jnp.zeros((r, c), BF16)
    w_qkv = win[:, :3 * d]
    w_zg = jnp.concatenate([win[:, 3 * d:4 * d], win[:, off_ga:off_ga + 2 * d]], axis=1)
    w_b = jnp.concatenate([win[:, off_b:off_b + n_b], zcols(d, b_width - n_b)], axis=1)
    w_bg = jnp.concatenate([win[:, off_beta:off_beta + 2 * heads_a], zcols(d, 128 - 2 * heads_a)], axis=1)

    def lanes(vec, start, width):
        return jnp.pad(vec.reshape(1, -1), ((0, 0), (start, width - start - vec.size)))

    log_rate = lanes(wts['a_log_rate'], heads_a, 128)
    dt_bias = lanes(wts['a_dt_bias'], heads_a, 128)
    mu = lanes(wts['b_shift_mu'], 0, b_width)
    w_up = jnp.pad(full['b_w_up'], ((0, 128 - LORA_W), (0, 0)))
    a_up = jnp.pad(full['b_a_up'], ((LORA_W, 0), (0, 0)))
    g_up = jnp.pad(full['b_g_up'], ((0, 256 - LORA_G), (0, 0)))
    b_pars = [full['b_w0'], w_up, full['b_a0'], a_up, g_up, full['b_k_k'], full['b_k_a']]
    post_pars = [full['a_out_norm'], full['b_r_k'], full['b_ln_gain'], full['b_ln_bias']]
    bg_fn = _make_bg_fn(heads_a)

    h0 = jnp.concatenate([jnp.zeros((PAD, d), F32), full['meta_tokens'], x[0]], axis=0)
    h1 = _ffn_fwd("ffn1_fwd", h0, full['ffn1_norm'], full['ffn1_w_gu'], full['ffn1_w_down'], tb_mm)
    (u,) = _tok_fwd("mix_norm_fwd", _norm_fn, [h1], [full['mix_norm']], [(d, BF16)], tb_mm)
    z_qkv = _mm("in_qkv", u, w_qkv, tb=tb_mm, tn=_col_tile(3 * d, 1536))
    z_zg = _mm("in_zg", u, w_zg, tb=tb_mm, tn=_col_tile(3 * d, 1536))
    z_b = _mm("in_b", u, w_b, tb=tb_mm, tn=_col_tile(b_width, 1536))
    z_bg = _mm("in_bg", u, w_bg, tb=tb_mm, tn=128)
    qkv = _a_pre_fwd(z_qkv, full['a_conv_w'])
    (bg,) = _tok_fwd("bg_fwd", bg_fn, [z_bg], [log_rate, dt_bias], [(128, F32)], tb_mm)
    o_dn, dn_hist = _dn_fwd(qkv, bg)
    zf = _shift_fwd(z_b, mu)
    rr, ww, kk2, vv, av, bv, gate = _tok_fwd("b_pre_fwd", _b_pre_fn, [zf], b_pars, [(d, F32)] * 7, tb_vjp)
    per_head = lambda t: t.reshape(rows, d // HEAD_B, HEAD_B)
    y_heads, b_hist, b_last = _rwkv_fwd(rr, ww, kk2, per_head(vv), av, bv)
    y_b = y_heads.reshape(rows, d)
    post_toks = [o_dn, z_zg, y_b, rr, kk2, vv, gate]
    (merged,) = _tok_fwd("post_fwd", _post_fn, post_toks, post_pars, [(d, BF16)], tb_vjp)
    h2 = _mm("out_proj", merged, full['w_out'], add=h1, tb=tb_mm, tn=d)
    h3 = _ffn_fwd("ffn2_fwd", h2, full['ffn2_norm'], full['ffn2_w_gu'], full['ffn2_w_down'], tb_mm)

    target = jnp.pad(loss_target[0], ((CHUNK, 0), (0, 0)))
    dh3, g_final, loss_part = _loss(h3, target, full['final_norm'].reshape(1, d), tb_vjp)

    def ffn_backward(tag, h, dout, key_norm, key_gu, key_down):
        dh, dh_bf, dgain, xn, act, dgate, dup, dhalf = _ffn_bwd(tag + "_bwd", h, full[key_norm], dout, full[key_gu],
                                                                full[key_down], tb_mm)
        d_gu = jnp.concatenate([_mm_tn_to_slabs(tag + "_dw_gate", xn, dgate, tk=tb_mm),
                                _mm_tn_to_slabs(tag + "_dw_up", xn, dup, tk=tb_mm)], axis=0)
        d_down = _mm_tn_from_slabs(tag + "_dw_down", act, dhalf, tk=tb_mm).reshape(N_DEV, -1, d)
        return dh, dh_bf, dgain, d_gu, d_down

    dh2, dh2_bf, g_ffn2_norm, g_ffn2_gu, g_ffn2_down = ffn_backward("ffn2", h2, dh3, 'ffn2_norm', 'ffn2_w_gu',
                                                                    'ffn2_w_down')
    g_w_out = _mm_tn("dw_out", merged, dh2_bf, tm=d, tn=d, tk=tb_mm).reshape(N_DEV, -1, d)

    def start_exchange(tag, keys, slabs, after=None):
        sems, kept, lands, token = _exchange_start("exchange_start_" + tag, slabs, after)
        return (tag, keys, sems, kept, lands), token

    ex_ffn2, token_ffn2 = start_exchange("ffn2", ['ffn2_w_gu', 'ffn2_w_down', 'w_out'], [g_ffn2_gu, g_ffn2_down, g_w_out])
    dmerged = _mm("d_merged", dh2_bf, full['w_out'], trans_b=True, after=token_ffn2, tb=tb_mm, tn=d)
    post_grads = _tok_bwd("post_bwd", _post_fn, post_toks, post_pars, [[dmerged]], list(range(7)), tb_vjp,
                          [F32, BF16] + [F32] * 5)
    do_dn, dz_zg, dy_b, dr1, dk1, dv1, dgate = post_grads[:7]
    g_out_norm, g_r_k, g_ln_g, g_ln_b = post_grads[7:]
    dr2, dw2, dk2, dv_heads, da2, db2 = _rwkv_bwd(rr, ww, kk2, per_head(vv), av, bv, b_hist, b_last, per_head(dy_b))
    dv2 = dv_heads.reshape(rows, d)
    b_grads = _tok_bwd("b_pre_bwd", _b_pre_fn, [zf], b_pars,
                       [[dr1, dr2], [dw2], [dk1, dk2], [dv1, dv2], [da2], [db2], [dgate]], [0], tb_vjp)
    dzf = b_grads[0]
    g_w0, g_w_up, g_a0, g_a_up, g_g_up, g_k_k, g_k_a = b_grads[1:]
    dz_b, g_mu = _shift_bwd(z_b, mu, dzf)
    dqkv, dbg = _dn_bwd(qkv, bg, dn_hist, do_dn)
    dz_qkv, g_conv = _a_pre_bwd(z_qkv, full['a_conv_w'], dqkv)
    dz_bg, g_log_rate, g_dt_bias = _tok_bwd("bg_bwd", bg_fn, [z_bg], [log_rate, dt_bias], [[dbg]], [0], tb_mm, [BF16])

    du = None
    g_w_in_parts = []
    for tag, dz, wpiece in (("qkv", dz_qkv, w_qkv), ("zg", dz_zg, w_zg), ("b", dz_b, w_b), ("bg", dz_bg, w_bg)):
        du = _mm("du_" + tag, dz, wpiece, trans_b=True, add=du, tb=tb_mm, tn=d)
        g_w_in_parts.append(_mm_tn("dw_in_" + tag, u, dz, tm=d, tn=_col_tile(dz.shape[1], 1536), tk=tb_mm))
    gp_qkv, gp_zg, gp_b, gp_bg = g_w_in_parts
    g_w_in = _cols_to_slabs(jnp.concatenate(
        [gp_qkv, gp_zg[:, :d], gp_bg[:, :2 * heads_a], gp_b[:, :n_b], gp_zg[:, d:]], axis=1))
    ex_w_in, token_w_in = start_exchange("w_in", ['w_in'], [g_w_in])
    dh1, g_mix_norm = _tok_bwd("mix_norm_bwd", _norm_res_fn, [h1], [full['mix_norm']], [[du], [dh2]], [0], tb_vjp,
                               after=token_w_in)
    dh0, _, g_ffn1_norm, g_ffn1_gu, g_ffn1_down = ffn_backward("ffn1", h0, dh1, 'ffn1_norm', 'ffn1_w_gu', 'ffn1_w_down')

    small_full = {
        'meta_tokens': dh0[PAD:CHUNK], 'ffn1_norm': g_ffn1_norm, 'mix_norm': g_mix_norm, 'a_conv_w': g_conv,
        'a_log_rate': g_log_rate[:, heads_a:2 * heads_a], 'a_dt_bias': g_dt_bias[:, heads_a:2 * heads_a],
        'a_out_norm': g_out_norm, 'b_shift_mu': g_mu[:, :n_b], 'b_w0': g_w0, 'b_w_up': g_w_up[:LORA_W],
        'b_a0': g_a0, 'b_a_up': g_a_up[LORA_W:], 'b_g_up': g_g_up[:LORA_G], 'b_k_k': g_k_k, 'b_k_a': g_k_a,
        'b_r_k': g_r_k, 'b_ln_gain': g_ln_g, 'b_ln_bias': g_ln_b, 'ffn2_norm': g_ffn2_norm, 'final_norm': g_final,
    }
    small_names = list(small_full)
    packed = _pack([small_full[k] for k in small_names] + [loss_part[:, :1]])
    (all_parts,) = _all_gather("gather_small_grads", [packed])
    ex_ffn1, _ = start_exchange("ffn1", ['ffn1_w_gu', 'ffn1_w_down'], [g_ffn1_gu, g_ffn1_down], after=all_parts)
    summed = _sum_slabs("sum_small_grads", all_parts, packed.shape[0])
    pieces = _unpack(summed, [small_full[k].shape for k in small_names] + [(1, 1)])
    small_grad = dict(zip(small_names, pieces[:-1]))
    loss = pieces[-1].reshape(())

    grads, deltas, new_m, new_v = {}, {}, {}, {}
    local_small = {}
    for k in small_names:
        g = small_grad[k]
        if k in small_sharded:
            width = wts[k].shape[-1]
            g = lax.dynamic_slice_in_dim(g, me * width, width, axis=1)
        local_small[k] = g.reshape(wts[k].shape)
    pk = lambda tree: _pack([tree[k] for k in small_names])
    dl_s, m_s, v_s = _adamw_small(pk(wts), pk(local_small), pk(mom_m), pk(mom_v))
    shapes = [wts[k].shape for k in small_names]
    for k, dl, m2, v2 in zip(small_names, _unpack(dl_s, shapes), _unpack(m_s, shapes), _unpack(v_s, shapes)):
        grads[k], deltas[k], new_m[k], new_v[k] = local_small[k], dl, m2, v2

    done = dl_s
    for tag, keys, sems, kept, lands in (ex_ffn2, ex_w_in, ex_ffn1):
        kept, lands = _exchange_wait("exchange_wait_" + tag, sems, kept, lands, done)
        for k, slabs, landed in zip(keys, kept, lands):
            shard = wts[k][0]
            rb = _tb(shard.shape[0], 128)
            own = lax.dynamic_index_in_dim(slabs, me, axis=0, keepdims=False)
            g, dl, m2, v2 = _adamw("adamw_" + k, own, landed, shard, mom_m[k][0], mom_v[k][0], rb)
            grads[k], deltas[k], new_m[k], new_v[k] = g[None], dl[None], m2[None], v2[None]
            done = dl

    grad_x = dh0[CHUNK:][None]
    return (loss, grad_x, *[grads[k] for k in names], *[deltas[k] for k in names],
            *[new_m[k] for k in names], *[new_v[k] for k in names])
```

```python
import functools

import jax
import jax.numpy as jnp
from jax import lax
from jax.experimental import pallas as pl
from jax.experimental.pallas import tpu as pltpu

F32 = jnp.float32
BF16 = jnp.bfloat16
N_DEV = 8
N_META = 16
CHUNK = 64
PAD = CHUNK - N_META
HEAD_A = 128
HEAD_B = 64
LORA_W, LORA_A, LORA_G = 64, 64, 160
LORA_PAD = 384
EPS = 1e-6
GN_EPS = HEAD_B * 1e-5
ADAM_LR, ADAM_B1, ADAM_B2, ADAM_EPS, ADAM_WD, ADAM_STEP = 0.001, 0.9, 0.999, 1e-08, 0.01, 10
SCAN_STEPS = 16
MXU_WIDTH = 256
VMEM_LIMIT = 56 * 1024 * 1024
DN_PRECISION = lax.Precision.HIGH
MESH_ID = pl.DeviceIdType.MESH
ANY = pl.BlockSpec(memory_space=pl.ANY)
HBM_SPEC = pl.BlockSpec(memory_space=pltpu.HBM)
SEM_SPEC = pl.BlockSpec(memory_space=pltpu.SEMAPHORE)


def _cp(*sem):
    return pltpu.CompilerParams(dimension_semantics=sem, vmem_limit_bytes=VMEM_LIMIT)


def _tb(t, target):
    best = 16
    for d in range(16, target + 1, 16):
        if t % d == 0:
            best = d
    return best


def _sigmoid(x):
    return 1.0 / (1.0 + jnp.exp(-x))


def _silu(x):
    return x * _sigmoid(x)


def _softplus(x):
    return jnp.maximum(x, 0.0) + jnp.log(1.0 + jnp.exp(-jnp.abs(x)))


def _dot_nt(a, b, precision=None):
    return lax.dot_general(a, b, (((1,), (1,)), ((), ())), preferred_element_type=F32, precision=precision)


def _dot_tn(a, b, precision=None):
    return lax.dot_general(a, b, (((0,), (0,)), ((), ())), preferred_element_type=F32, precision=precision)


def _dot(a, b, precision=None):
    return jnp.dot(a, b, preferred_element_type=F32, precision=precision)


def _block_diag_ones():
    i = lax.broadcasted_iota(jnp.int32, (MXU_WIDTH, MXU_WIDTH), 0) // HEAD_B
    j = lax.broadcasted_iota(jnp.int32, (MXU_WIDTH, MXU_WIDTH), 1) // HEAD_B
    return (i == j).astype(BF16)


def _hi_lo(x):
    hi = x.astype(BF16)
    return hi, (x - hi.astype(F32)).astype(BF16)


def _segsum_many(xs, bd):
    groups = [x if isinstance(x, tuple) else _hi_lo(x) for x in xs]
    rows = groups[0][0].shape[0]
    stacked = jnp.concatenate([p for grp in groups for p in grp], axis=0)
    out = jnp.concatenate([_dot(stacked[:, s:s + MXU_WIDTH], bd) for s in range(0, stacked.shape[1], MXU_WIDTH)], axis=1)
    res, pos = [], 0
    for grp in groups:
        acc = out[pos:pos + rows]
        for j in range(1, len(grp)):
            acc = acc + out[pos + j * rows:pos + (j + 1) * rows]
        res.append(acc)
        pos += len(grp) * rows
    return res


def _segsum_impl(x):
    return _segsum_many([x], _block_diag_ones())[0]


@jax.custom_vjp
def _segsum64(x):
    return _segsum_impl(x)


_segsum64.defvjp(lambda x: (_segsum_impl(x), None), lambda _, ct: (_segsum_impl(ct),))


def _tok(t):
    return t if isinstance(t, tuple) else (t, t.shape[1], 0)


def _tok_spec(tb, width, colblk):
    return pl.BlockSpec((tb, width), lambda i: (i, colblk))


def _par_spec(p):
    return pl.BlockSpec(p.shape, lambda i: (0, 0))


def _tok_fwd(name, fn, toks, pars, outs, tb):
    toks = [_tok(t) for t in toks]
    rows = toks[0][0].shape[0]
    n_in = len(toks) + len(pars)

    def body(*refs):
        row0 = pl.program_id(0) * tb
        res = fn(row0, *[r[...] for r in refs[:n_in]])
        for r, o in zip(refs[n_in:], res):
            r[...] = o.astype(r.dtype)

    return pl.pallas_call(
        body, name=name, grid=(rows // tb,),
        in_specs=[_tok_spec(tb, w, c) for _, w, c in toks] + [_par_spec(p) for p in pars],
        out_specs=[_tok_spec(tb, w, 0) for w, _ in outs],
        out_shape=[jax.ShapeDtypeStruct((rows, w), dt) for w, dt in outs],
        compiler_params=_cp("parallel"),
    )(*[a for a, _, _ in toks], *pars)


def _tok_bwd(name, fn, toks, pars, cts, want, tb, want_dtypes=None, after=None):
    toks = [_tok(t) for t in toks]
    want_dtypes = want_dtypes or [F32] * len(want)
    cts = [[_tok(c) for c in group] for group in cts]
    flat_cts = [c for group in cts for c in group]
    rows = toks[0][0].shape[0]
    n_tok, n_par, n_ct = len(toks), len(pars), len(flat_cts)
    extra = [] if after is None else [after]

    def body(*refs):
        i = pl.program_id(0)
        row0 = i * tb
        prim = [r[...].astype(F32) for r in refs[:n_tok + n_par]]
        ct_refs = list(refs[n_tok + n_par:n_tok + n_par + n_ct])
        out_refs = refs[n_tok + n_par + n_ct + len(extra):]
        res, vjp = jax.vjp(lambda *a: fn(row0, *a), *prim)
        ct = []
        for group, o in zip(cts, res):
            acc = None
            for _ in group:
                v = ct_refs.pop(0)[...].astype(F32)
                acc = v if acc is None else acc + v
            ct.append(acc.astype(o.dtype))
        grads = vjp(tuple(ct))
        for r, k in zip(out_refs[:len(want)], want):
            r[...] = grads[k].astype(r.dtype)

        @pl.when(i == 0)
        def _():
            for r in out_refs[len(want):]:
                r[...] = jnp.zeros_like(r)

        for r, g in zip(out_refs[len(want):], grads[n_tok:]):
            r[...] += g

    return pl.pallas_call(
        body, name=name, grid=(rows // tb,),
        in_specs=[_tok_spec(tb, w, c) for _, w, c in toks] + [_par_spec(p) for p in pars]
        + [_tok_spec(tb, w, c) for _, w, c in flat_cts] + [ANY] * len(extra),
        out_specs=[_tok_spec(tb, toks[k][1], 0) for k in want] + [_par_spec(p) for p in pars],
        out_shape=[jax.ShapeDtypeStruct((rows, toks[k][1]), dt) for k, dt in zip(want, want_dtypes)]
        + [jax.ShapeDtypeStruct(p.shape, F32) for p in pars],
        compiler_params=_cp("arbitrary"),
    )(*[a for a, _, _ in toks], *pars, *[a for a, _, _ in flat_cts], *extra)


def _mm(name, a, b, *, trans_b=False, add=None, after=None, tb, tn):
    rows, k = a.shape
    n = b.shape[0] if trans_b else b.shape[1]

    def body(*refs):
        a_ref, b_ref = refs[:2]
        o_ref = refs[-1]
        acc = _dot_nt(a_ref[...], b_ref[...]) if trans_b else _dot(a_ref[...], b_ref[...])
        if add is not None:
            acc = acc + refs[2][...]
        o_ref[...] = acc

    in_specs = [pl.BlockSpec((tb, k), lambda i, j: (i, 0)),
                pl.BlockSpec((tn, k), lambda i, j: (j, 0)) if trans_b else pl.BlockSpec((k, tn), lambda i, j: (0, j))]
    args = [a, b]
    if add is not None:
        in_specs.append(pl.BlockSpec((tb, tn), lambda i, j: (i, j)))
        args.append(add)
    if after is not None:
        in_specs.append(ANY)
        args.append(after)
    return pl.pallas_call(
        body, name=name, grid=(rows // tb, n // tn), in_specs=in_specs,
        out_specs=pl.BlockSpec((tb, tn), lambda i, j: (i, j)),
        out_shape=jax.ShapeDtypeStruct((rows, n), F32),
        compiler_params=_cp("parallel", "parallel"),
    )(*args)


def _mm_tn_call(name, grid, a, b, a_spec, b_spec, o_spec, acc_shape, out_shape):
    last = len(grid) - 1

    def body(a_ref, b_ref, o_ref, acc_ref):
        k = pl.program_id(last)

        @pl.when(k == 0)
        def _():
            acc_ref[...] = jnp.zeros_like(acc_ref)

        a_blk = a_ref[0] if len(a_ref.shape) == 3 else a_ref[...]
        b_blk = b_ref[0] if len(b_ref.shape) == 3 else b_ref[...]
        acc_ref[...] += _dot_tn(a_blk, b_blk)

        @pl.when(k == grid[last] - 1)
        def _():
            if len(o_ref.shape) == 3:
                o_ref[0] = acc_ref[...].astype(o_ref.dtype)
            else:
                o_ref[...] = acc_ref[...].astype(o_ref.dtype)

    return pl.pallas_call(
        body, name=name, grid=grid, in_specs=[a_spec, b_spec], out_specs=o_spec,
        out_shape=jax.ShapeDtypeStruct(out_shape, BF16), scratch_shapes=[pltpu.VMEM(acc_shape, F32)],
        compiler_params=_cp(*(["parallel"] * last + ["arbitrary"])),
    )(a, b)


def _mm_tn(name, a, b, *, tm, tn, tk):
    rows, m = a.shape
    n = b.shape[1]
    return _mm_tn_call(name, (m // tm, n // tn, rows // tk), a, b,
                       pl.BlockSpec((tk, tm), lambda i, j, k: (k, i)), pl.BlockSpec((tk, tn), lambda i, j, k: (k, j)),
                       pl.BlockSpec((tm, tn), lambda i, j, k: (i, j)), (tm, tn), (m, n))


def _mm_tn_to_slabs(name, a, b3, *, tk):
    rows, m = a.shape
    s, _, c = b3.shape
    return _mm_tn_call(name, (s, rows // tk), a, b3,
                       pl.BlockSpec((tk, m), lambda i, k: (k, 0)), pl.BlockSpec((1, tk, c), lambda i, k: (i, k, 0)),
                       pl.BlockSpec((1, m, c), lambda i, k: (i, 0, 0)), (m, c), (s, m, c))


def _mm_tn_from_slabs(name, a3, b, *, tk):
    s, rows, c = a3.shape
    n = b.shape[1]
    return _mm_tn_call(name, (s, rows // tk), a3, b,
                       pl.BlockSpec((1, tk, c), lambda i, k: (i, k, 0)), pl.BlockSpec((tk, n), lambda i, k: (k, 0)),
                       pl.BlockSpec((c, n), lambda i, k: (i, 0)), (c, n), (s * c, n))


def _col_tile(n, target):
    if n <= target:
        return n
    best = 128
    for d in range(128, target + 1, 128):
        if n % d == 0:
            best = d
    return best


def _rms(x, gain):
    return x * lax.rsqrt(jnp.mean(x * x, axis=-1, keepdims=True) + EPS) * gain


def _ffn_specs(d, fc, nj):
    return [pl.BlockSpec((1, d, fc), lambda i, j: (j, 0, 0)), pl.BlockSpec((1, d, fc), lambda i, j: (nj + j, 0, 0)),
            pl.BlockSpec((fc, d), lambda i, j: (j, 0))]


def _ffn_fwd(name, h, gain, wgu, wd, tb):
    rows, d = h.shape
    nj = wgu.shape[0] // 2
    fc = wgu.shape[2]

    def body(h_ref, g_ref, wg_ref, wu_ref, wd_ref, o_ref, xn_s, acc_s):
        j = pl.program_id(1)

        @pl.when(j == 0)
        def _():
            xn_s[...] = _rms(h_ref[...], g_ref[...]).astype(BF16)
            acc_s[...] = jnp.zeros_like(acc_s)

        xn = xn_s[...]
        gate = _dot(xn, wg_ref[0])
        up = _dot(xn, wu_ref[0])
        acc_s[...] += _dot((_silu(gate) * up).astype(BF16), wd_ref[...])

        @pl.when(j == nj - 1)
        def _():
            o_ref[...] = h_ref[...] + 0.5 * acc_s[...]

    return pl.pallas_call(
        body, name=name, grid=(rows // tb, nj),
        in_specs=[pl.BlockSpec((tb, d), lambda i, j: (i, 0)), pl.BlockSpec((1, d), lambda i, j: (0, 0))]
        + _ffn_specs(d, fc, nj),
        out_specs=pl.BlockSpec((tb, d), lambda i, j: (i, 0)),
        out_shape=jax.ShapeDtypeStruct((rows, d), F32),
        scratch_shapes=[pltpu.VMEM((tb, d), BF16), pltpu.VMEM((tb, d), F32)],
        compiler_params=_cp("parallel", "arbitrary"),
    )(h, gain, wgu, wgu, wd)


def _ffn_bwd(name, h, gain, dout, wgu, wd, tb):
    rows, d = h.shape
    nj = wgu.shape[0] // 2
    fc = wgu.shape[2]

    def body(h_ref, g_ref, do_ref, wg_ref, wu_ref, wd_ref,
             dh_ref, dhb_ref, dg_ref, xn_ref, act_ref, dgate_ref, dup_ref, dhalf_ref, dxn_s):
        i, j = pl.program_id(0), pl.program_id(1)

        @pl.when(j == 0)
        def _():
            xn_ref[...] = _rms(h_ref[...], g_ref[...]).astype(BF16)
            dhalf_ref[...] = (0.5 * do_ref[...]).astype(BF16)
            dxn_s[...] = jnp.zeros_like(dxn_s)

        xn = xn_ref[...]
        wg, wu = wg_ref[0], wu_ref[0]
        gate = _dot(xn, wg)
        up = _dot(xn, wu)
        sg = _sigmoid(gate)
        dact = _dot_nt(dhalf_ref[...], wd_ref[...])
        act_ref[0] = (gate * sg * up).astype(BF16)
        dgate = (dact * up * (sg * (1.0 + gate * (1.0 - sg)))).astype(BF16)
        dup = (dact * gate * sg).astype(BF16)
        dgate_ref[0] = dgate
        dup_ref[0] = dup
        dxn_s[...] += _dot_nt(dgate, wg) + _dot_nt(dup, wu)

        @pl.when((i == 0) & (j == 0))
        def _():
            dg_ref[...] = jnp.zeros_like(dg_ref)

        @pl.when(j == nj - 1)
        def _():
            x = h_ref[...]
            r = lax.rsqrt(jnp.mean(x * x, axis=-1, keepdims=True) + EPS)
            dxn = dxn_s[...]
            dyg = dxn * g_ref[...]
            dh = do_ref[...] + r * dyg - x * (r * r * r) * jnp.mean(dyg * x, axis=-1, keepdims=True)
            dh_ref[...] = dh
            dhb_ref[...] = dh.astype(BF16)
            dg_ref[...] += jnp.sum(dxn * x * r, axis=0, keepdims=True)

    row_d = pl.BlockSpec((tb, d), lambda i, j: (i, 0))
    slab = pl.BlockSpec((1, tb, fc), lambda i, j: (j, i, 0))
    hidden = jax.ShapeDtypeStruct((nj, rows, fc), BF16)
    return pl.pallas_call(
        body, name=name, grid=(rows // tb, nj),
        in_specs=[row_d, pl.BlockSpec((1, d), lambda i, j: (0, 0)), row_d] + _ffn_specs(d, fc, nj),
        out_specs=[row_d, row_d, pl.BlockSpec((1, d), lambda i, j: (0, 0)), row_d, slab, slab, slab, row_d],
        out_shape=[jax.ShapeDtypeStruct((rows, d), F32), jax.ShapeDtypeStruct((rows, d), BF16),
                   jax.ShapeDtypeStruct((1, d), F32), jax.ShapeDtypeStruct((rows, d), BF16),
                   hidden, hidden, hidden, jax.ShapeDtypeStruct((rows, d), BF16)],
        scratch_shapes=[pltpu.VMEM((tb, d), F32)],
        compiler_params=_cp("arbitrary", "arbitrary"),
    )(h, gain, dout, wgu, wgu, wd)


def _shift_rows(x, s):
    return pltpu.roll(x, s % x.shape[0], 0)


def _a_post(c, which):
    s = _silu(c)
    n = s * lax.rsqrt(jnp.sum(s * s, axis=-1, keepdims=True) + 1e-6)
    scale = jnp.where(which == 0, HEAD_A ** -0.5, 1.0)
    return jnp.where(which == 2, s, n * scale)


def _conv(x, w):
    return x * w[3:4] + _shift_rows(x, 1) * w[2:3] + _shift_rows(x, 2) * w[1:2] + _shift_rows(x, 3) * w[0:1]


def _a_pre_fwd(zqkv, conv_w):
    rows, width = zqkv.shape
    heads = width // (3 * HEAD_A)

    def body(x_ref, w_ref, o_ref):
        which = pl.program_id(0) // heads
        live = lax.broadcasted_iota(jnp.int32, (rows, HEAD_A), 0) >= PAD
        o_ref[...] = jnp.where(live, _a_post(_conv(x_ref[...], w_ref[...]), which), 0.0)

    return pl.pallas_call(
        body, name="a_pre_fwd", grid=(width // HEAD_A,),
        in_specs=[pl.BlockSpec((rows, HEAD_A), lambda c: (0, c)), pl.BlockSpec((4, HEAD_A), lambda c: (0, c))],
        out_specs=pl.BlockSpec((rows, HEAD_A), lambda c: (0, c)),
        out_shape=jax.ShapeDtypeStruct((rows, width), F32),
        compiler_params=_cp("parallel"),
    )(zqkv, conv_w)


def _a_pre_bwd(zqkv, conv_w, dqkv):
    rows, width = zqkv.shape
    heads = width // (3 * HEAD_A)

    def body(x_ref, w_ref, ct_ref, dx_ref, dw_ref):
        which = pl.program_id(0) // heads
        live = lax.broadcasted_iota(jnp.int32, (rows, HEAD_A), 0) >= PAD
        x, w = x_ref[...], w_ref[...]
        _, vjp = jax.vjp(lambda c: _a_post(c, which), _conv(x, w))
        (dc,) = vjp(jnp.where(live, ct_ref[...], 0.0))
        dc = jnp.where(live, dc, 0.0)
        dx_ref[...] = (dc * w[3:4] + _shift_rows(dc, -1) * w[2:3] + _shift_rows(dc, -2) * w[1:2]
                       + _shift_rows(dc, -3) * w[0:1]).astype(BF16)
        dw_ref[...] = jnp.concatenate(
            [jnp.sum(dc * (_shift_rows(x, 3 - j) if j < 3 else x), axis=0, keepdims=True) for j in range(4)], axis=0)

    col = pl.BlockSpec((rows, HEAD_A), lambda c: (0, c))
    wsp = pl.BlockSpec((4, HEAD_A), lambda c: (0, c))
    return pl.pallas_call(
        body, name="a_pre_bwd", grid=(width // HEAD_A,),
        in_specs=[col, wsp, col], out_specs=[col, wsp],
        out_shape=[jax.ShapeDtypeStruct((rows, width), BF16), jax.ShapeDtypeStruct((4, width), F32)],
        compiler_params=_cp("parallel"),
    )(zqkv, conv_w, dqkv)


SHIFT_TILE = 384


def _shift_fwd(zb, mu):
    rows, width = zb.shape

    def body(x_ref, mu_ref, o_ref):
        x = x_ref[...]
        first = lax.broadcasted_iota(jnp.int32, x.shape, 0) == 0
        prev = jnp.where(first, 0.0, _shift_rows(x, 1))
        o_ref[...] = x + (prev - x) * mu_ref[...]

    col = pl.BlockSpec((rows, SHIFT_TILE), lambda c: (0, c))
    return pl.pallas_call(
        body, name="shift_fwd", grid=(width // SHIFT_TILE,),
        in_specs=[col, pl.BlockSpec((1, SHIFT_TILE), lambda c: (0, c))], out_specs=col,
        out_shape=jax.ShapeDtypeStruct((rows, width), F32), compiler_params=_cp("parallel"),
    )(zb, mu)


def _shift_bwd(zb, mu, dzf):
    rows, width = zb.shape

    def body(x_ref, mu_ref, ct_ref, dx_ref, dmu_ref):
        x, ct, mu_v = x_ref[...], ct_ref[...], mu_ref[...]
        row = lax.broadcasted_iota(jnp.int32, x.shape, 0)
        prev = jnp.where(row == 0, 0.0, _shift_rows(x, 1))
        nxt = jnp.where(row == rows - 1, 0.0, _shift_rows(ct, -1))
        dx_ref[...] = (ct * (1.0 - mu_v) + nxt * mu_v).astype(BF16)
        dmu_ref[...] = jnp.sum(ct * (prev - x), axis=0, keepdims=True)

    col = pl.BlockSpec((rows, SHIFT_TILE), lambda c: (0, c))
    msp = pl.BlockSpec((1, SHIFT_TILE), lambda c: (0, c))
    return pl.pallas_call(
        body, name="shift_bwd", grid=(width // SHIFT_TILE,),
        in_specs=[col, msp, col], out_specs=[col, msp],
        out_shape=[jax.ShapeDtypeStruct((rows, width), BF16), jax.ShapeDtypeStruct((1, width), F32)],
        compiler_params=_cp("parallel"),
    )(zb, mu, dzf)


def _dn_chunk(q, k, v, beta, g, state):
    heads = range(len(q))
    ri = lax.broadcasted_iota(jnp.int32, (CHUNK, CHUNK), 0)
    ci = lax.broadcasted_iota(jnp.int32, (CHUNK, CHUNK), 1)
    eye = (ri == ci).astype(F32)
    incl = ri >= ci
    last = lax.broadcasted_iota(jnp.int32, (CHUNK, 1), 0) == CHUNK - 1
    g_row = [jnp.sum(g[h] * eye, axis=0, keepdims=True) for h in heads]
    gc = [jnp.sum(jnp.where(incl, g_row[h], 0.0), axis=1, keepdims=True) for h in heads]
    gc_row = [jnp.sum(gc[h] * eye, axis=0, keepdims=True) for h in heads]
    decay = [jnp.where(incl, jnp.exp(jnp.where(incl, gc[h] - gc_row[h], 0.0)), 0.0) for h in heads]
    kb = [k[h] * beta[h] for h in heads]
    vb = [v[h] * beta[h] for h in heads]
    p = [-jnp.where(ri > ci, _dot_nt(kb[h], k[h], DN_PRECISION) * decay[h], 0.0) for h in heads]
    tinv = [eye + p[h] for h in heads]
    for _ in range(5):
        p = [_dot(p[h], p[h], DN_PRECISION) for h in heads]
        tinv = [tinv[h] + _dot(tinv[h], p[h], DN_PRECISION) for h in heads]
    eg = [jnp.exp(gc[h]) for h in heads]
    u = [_dot(tinv[h], vb[h], DN_PRECISION) for h in heads]
    wk = [_dot(tinv[h], kb[h] * eg[h], DN_PRECISION) for h in heads]
    attn = [_dot_nt(q[h], k[h]) * decay[h] for h in heads]
    g_last = [jnp.sum(jnp.where(last, gc[h], 0.0), axis=0, keepdims=True) for h in heads]
    k_tail = [k[h] * jnp.exp(g_last[h] - gc[h]) for h in heads]
    v_new = [u[h] - _dot(wk[h], state[h]) for h in heads]
    o = [_dot(q[h] * eg[h], state[h]) + _dot(attn[h], v_new[h]) for h in heads]
    new = [state[h] * jnp.exp(g_last[h]) + _dot_tn(k_tail[h], v_new[h]) for h in heads]
    return o, new


def _bg_cols(bg, h, heads):
    lane = lax.broadcasted_iota(jnp.int32, bg.shape, 1)
    beta = jnp.sum(jnp.where(lane == h, bg, 0.0), axis=1, keepdims=True)
    g = jnp.sum(jnp.where(lane == heads + h, bg, 0.0), axis=1, keepdims=True)
    return beta, g


def _dn_fwd(qkv, bg):
    rows = qkv.shape[0]
    heads = qkv.shape[1] // (3 * HEAD_A)
    n = rows // CHUNK
    hp, groups = heads, 1

    def body(q_ref, k_ref, v_ref, bg_ref, o_ref, hist_ref, s_ref):
        c, grp = pl.program_id(0), pl.program_id(1)

        @pl.when(c == 0)
        def _():
            for i in range(hp):
                s_ref[grp * hp + i] = jnp.zeros((HEAD_A, HEAD_A), F32)

        bg_v = bg_ref[...]
        cols = [slice(i * HEAD_A, (i + 1) * HEAD_A) for i in range(hp)]
        state = [s_ref[grp * hp + i] for i in range(hp)]
        beta_g = [_bg_cols(bg_v, grp * hp + i, heads) for i in range(hp)]
        o, new = _dn_chunk([q_ref[:, c_] for c_ in cols], [k_ref[:, c_] for c_ in cols], [v_ref[:, c_] for c_ in cols],
                           [b for b, _ in beta_g], [g for _, g in beta_g], state)
        for i in range(hp):
            hist_ref[0, i] = state[i]
            o_ref[:, cols[i]] = o[i]
            s_ref[grp * hp + i] = new[i]

    def part(p):
        return pl.BlockSpec((CHUNK, hp * HEAD_A), lambda c, grp: (c, p * groups + grp))

    return pl.pallas_call(
        body, name="deltanet_fwd", grid=(n, groups),
        in_specs=[part(0), part(1), part(2), pl.BlockSpec((CHUNK, 128), lambda c, grp: (c, 0))],
        out_specs=[part(0), pl.BlockSpec((1, hp, HEAD_A, HEAD_A), lambda c, grp: (c, grp, 0, 0))],
        out_shape=[jax.ShapeDtypeStruct((rows, heads * HEAD_A), F32),
                   jax.ShapeDtypeStruct((n, heads, HEAD_A, HEAD_A), F32)],
        scratch_shapes=[pltpu.VMEM((heads, HEAD_A, HEAD_A), F32)],
        compiler_params=_cp("arbitrary", "arbitrary"),
    )(qkv, qkv, qkv, bg)


def _dn_bwd(qkv, bg, hist, do):
    rows = qkv.shape[0]
    heads = qkv.shape[1] // (3 * HEAD_A)
    n = rows // CHUNK
    hp, groups = heads, 1

    def body(q_ref, k_ref, v_ref, bg_ref, hist_ref, do_ref, dqkv_ref, dbg_ref, ds_ref):
        c, grp = pl.program_id(0), pl.program_id(1)

        @pl.when(c == 0)
        def _():
            for i in range(hp):
                ds_ref[grp * hp + i] = jnp.zeros((HEAD_A, HEAD_A), F32)

        bg_v = bg_ref[...]
        lane = lax.broadcasted_iota(jnp.int32, (CHUNK, 128), 1)
        cols = [slice(i * HEAD_A, (i + 1) * HEAD_A) for i in range(hp)]
        beta_g = [_bg_cols(bg_v, grp * hp + i, heads) for i in range(hp)]
        _, vjp = jax.vjp(_dn_chunk, [q_ref[:, c_] for c_ in cols], [k_ref[:, c_] for c_ in cols],
                         [v_ref[:, c_] for c_ in cols], [b for b, _ in beta_g], [g for _, g in beta_g],
                         [hist_ref[0, i] for i in range(hp)])
        dq, dk, dv, dbeta, dg, ds = vjp(([do_ref[:, c_] for c_ in cols], [ds_ref[grp * hp + i] for i in range(hp)]))
        dbg = jnp.zeros((CHUNK, 128), F32)
        for i in range(hp):
            h = grp * hp + i
            for p, part_grad in enumerate((dq, dk, dv)):
                dqkv_ref[:, pl.ds((p * heads + i) * HEAD_A, HEAD_A)] = part_grad[i]
            ds_ref[h] = ds[i]
            dbg = dbg + jnp.where(lane == h, dbeta[i], 0.0) + jnp.where(lane == heads + h, dg[i], 0.0)

        @pl.when(grp == 0)
        def _():
            dbg_ref[...] = jnp.zeros_like(dbg_ref)

        dbg_ref[...] += dbg

    def part(p):
        return pl.BlockSpec((CHUNK, hp * HEAD_A), lambda c, grp: (n - 1 - c, p * groups + grp))

    return pl.pallas_call(
        body, name="deltanet_bwd", grid=(n, groups),
        in_specs=[part(0), part(1), part(2), pl.BlockSpec((CHUNK, 128), lambda c, grp: (n - 1 - c, 0)),
                  pl.BlockSpec((1, hp, HEAD_A, HEAD_A), lambda c, grp: (n - 1 - c, grp, 0, 0)), part(0)],
        out_specs=[pl.BlockSpec((CHUNK, 3 * heads * HEAD_A), lambda c, grp: (n - 1 - c, 0)),
                   pl.BlockSpec((CHUNK, 128), lambda c, grp: (n - 1 - c, 0))],
        out_shape=[jax.ShapeDtypeStruct(qkv.shape, F32), jax.ShapeDtypeStruct((rows, 128), F32)],
        scratch_shapes=[pltpu.VMEM((heads, HEAD_A, HEAD_A), F32)],
        compiler_params=_cp("arbitrary", "arbitrary"),
    )(qkv, qkv, qkv, bg, hist, do)


def _head_mask(heads, width):
    return (lax.broadcasted_iota(jnp.int32, (heads, width), 0)
            == lax.broadcasted_iota(jnp.int32, (heads, width), 1) // HEAD_B)


def _masked_rows(mask, row):
    return jnp.where(mask, row, 0.0).astype(BF16)


def _rwkv_fwd(r, w, k, v, a, b):
    rows, width = r.shape
    heads = width // HEAD_B
    ts = SCAN_STEPS

    def body(r_ref, w_ref, k_ref, v_ref, a_ref, b_ref, y_ref, hist_ref, s_ref):
        @pl.when(pl.program_id(0) == 0)
        def _():
            s_ref[...] = jnp.zeros_like(s_ref)

        mask = _head_mask(heads, width)
        onehot = mask.astype(BF16)
        onehot2 = jnp.concatenate([onehot, onehot], axis=0)
        bd = _block_diag_ones()

        spread_v = [_dot_tn(jnp.concatenate(_hi_lo(v_ref[j]), axis=0), onehot2) for j in range(ts)]
        s = s_ref[...]
        ys = []
        for j in range(ts):
            row = pl.ds(j, 1)
            hist_ref[j] = s
            (sa,) = _segsum_many([((s * a_ref[row, :]).astype(BF16),)], bd)
            s = s * w_ref[row, :] + sa * b_ref[row, :] + spread_v[j] * k_ref[row, :]
            ys.append(_dot_nt(_masked_rows(mask, r_ref[row, :]), s.astype(BF16)))
        for j in range(ts):
            y_ref[j] = ys[j]
        s_ref[...] = s

    blk = pl.BlockSpec((ts, width), lambda i: (i, 0))
    blk3 = pl.BlockSpec((ts, heads, HEAD_B), lambda i: (i, 0, 0))
    return pl.pallas_call(
        body, name="rwkv_fwd", grid=(rows // ts,),
        in_specs=[blk, blk, blk, blk3, blk, blk],
        out_specs=[blk3, pl.BlockSpec((ts, HEAD_B, width), lambda i: (i, 0, 0)),
                   pl.BlockSpec((HEAD_B, width), lambda i: (0, 0))],
        out_shape=[jax.ShapeDtypeStruct((rows, heads, HEAD_B), F32), jax.ShapeDtypeStruct((rows, HEAD_B, width), F32),
                   jax.ShapeDtypeStruct((HEAD_B, width), F32)],
        compiler_params=_cp("arbitrary"),
    )(r, w, k, v, a, b)


def _rwkv_bwd(r, w, k, v, a, b, hist, last, dy):
    rows, width = r.shape
    heads = width // HEAD_B
    ts = SCAN_STEPS
    nb = rows // ts

    def body(r_ref, w_ref, k_ref, v_ref, a_ref, b_ref, hist_ref, last_ref, dy_ref,
             dr_ref, dw_ref, dk_ref, dv_ref, da_ref, db_ref, g_ref, after_ref):
        @pl.when(pl.program_id(0) == 0)
        def _():
            g_ref[...] = jnp.zeros_like(g_ref)
            after_ref[...] = last_ref[...]

        mask = _head_mask(heads, width)
        onehot = mask.astype(BF16)
        bd = _block_diag_ones()

        def own_lanes(x):
            return jnp.sum(jnp.where(mask, x, 0.0), axis=0, keepdims=True)

        def colsum(x):
            return jnp.sum(x, axis=0, keepdims=True)

        dy_m = [dy_ref[j].astype(BF16) for j in range(ts)]
        spread_dy = [_dot_tn(dy_m[j], onehot) for j in range(ts)]
        state_after = [hist_ref[j + 1] if j < ts - 1 else after_ref[...] for j in range(ts)]
        dr = [own_lanes(_dot(dy_m[j], state_after[j].astype(BF16))) for j in range(ts)]
        sa_m = [_dot_nt(_masked_rows(mask, a_ref[pl.ds(j, 1), :]), hist_ref[j].astype(BF16)) for j in range(ts)]
        g = g_ref[...]
        dw, dk, db, da, dv = {}, {}, {}, {}, {}
        for j in reversed(range(ts)):
            row = pl.ds(j, 1)
            sp = hist_ref[j]
            g = g + spread_dy[j] * r_ref[row, :]
            (dsa,) = _segsum_many([((g * b_ref[row, :]).astype(BF16),)], bd)
            g_b = g.astype(BF16)
            both = _dot(jnp.concatenate([v_ref[j].astype(BF16), sa_m[j].astype(BF16)], axis=0), g_b)
            dk[j], db[j] = own_lanes(both[:heads]), own_lanes(both[heads:])
            dv[j] = _dot_nt(_masked_rows(mask, k_ref[row, :]), g_b)
            dw[j] = colsum(g * sp)
            da[j] = colsum(sp * dsa)
            g = g * w_ref[row, :] + dsa * a_ref[row, :]
        g_ref[...] = g
        after_ref[...] = hist_ref[0]
        for j in range(ts):
            dv_ref[j] = dv[j]
            for ref, vals in ((dr_ref, dr), (dw_ref, dw), (dk_ref, dk), (da_ref, da), (db_ref, db)):
                ref[pl.ds(j, 1), :] = vals[j]

    blk = pl.BlockSpec((ts, width), lambda i: (nb - 1 - i, 0))
    blk3 = pl.BlockSpec((ts, heads, HEAD_B), lambda i: (nb - 1 - i, 0, 0))
    state = pl.BlockSpec((HEAD_B, width), lambda i: (0, 0))
    return pl.pallas_call(
        body, name="rwkv_bwd", grid=(nb,),
        in_specs=[blk, blk, blk, blk3, blk, blk, pl.BlockSpec((ts, HEAD_B, width), lambda i: (nb - 1 - i, 0, 0)),
                  state, blk3],
        out_specs=[blk, blk, blk, blk3, blk, blk],
        out_shape=[jax.ShapeDtypeStruct((rows, width), F32)] * 3 + [jax.ShapeDtypeStruct((rows, heads, HEAD_B), F32)]
        + [jax.ShapeDtypeStruct((rows, width), F32)] * 2,
        scratch_shapes=[pltpu.VMEM((HEAD_B, width), F32), pltpu.VMEM((HEAD_B, width), F32)],
        compiler_params=_cp("arbitrary"),
    )(r, w, k, v, a, b, hist, last, dy)


def _live(row0, shape):
    return (row0 + lax.broadcasted_iota(jnp.int32, shape, 0)) >= PAD


def _norm_fn(row0, h, gain):
    return (_rms(h, gain),)


def _norm_res_fn(row0, h, gain):
    return _rms(h, gain), h


def _make_bg_fn(heads):
    def fn(row0, x, log_rate, dt_bias):
        lane = lax.broadcasted_iota(jnp.int32, x.shape, 1)
        beta = _sigmoid(x)
        g = -jnp.exp(log_rate) * _softplus(x + dt_bias)
        out = jnp.where(lane < heads, beta, jnp.where(lane < 2 * heads, g, 0.0))
        return (jnp.where(_live(row0, x.shape), out, 0.0),)
    return fn


def _b_pre_fn(row0, zf, w0, w_up, a0, a_up, g_up, k_k, k_a):
    d = w0.shape[1]
    r, k, v = zf[:, :d], zf[:, d:2 * d], zf[:, 2 * d:3 * d]
    lo = zf[:, 3 * d:3 * d + 128]
    lg = zf[:, 3 * d + 128:3 * d + LORA_PAD]
    lane = lax.broadcasted_iota(jnp.int32, lo.shape, 1)
    lw = _dot(jnp.where(lane < LORA_W, jnp.tanh(lo), 0.0), w_up)
    la = _dot(jnp.where(lane >= LORA_W, lo, 0.0), a_up)
    lane_g = lax.broadcasted_iota(jnp.int32, lg.shape, 1)
    gate = _dot(jnp.where(lane_g < LORA_G, _sigmoid(lg), 0.0), g_up)
    decay = jnp.exp(-jnp.exp(-_softplus(-(w0 + lw)) - 0.5))
    a = _sigmoid(a0 + la)
    kx = k * k_k
    kk = kx * lax.rsqrt(_segsum64(kx * kx) + 1e-6)
    k2 = k * (1.0 + (a - 1.0) * k_a)
    return r, decay, k2, v, -kk, kk * a, gate


def _post_fn(row0, o, zg, y, r, k2, v, gate, out_gain, r_k, ln_g, ln_b):
    d = o.shape[1]
    az, ga, gb = zg[:, :d], zg[:, d:2 * d], zg[:, 2 * d:]
    heads = d // HEAD_A
    parts = []
    for h in range(heads):
        oh = o[:, h * HEAD_A:(h + 1) * HEAD_A]
        parts.append(oh * lax.rsqrt(jnp.mean(oh * oh, axis=-1, keepdims=True) + EPS) * out_gain)
    o_a = jnp.concatenate(parts, axis=1) * _silu(az)
    mean = _segsum64(y) * (1.0 / HEAD_B)
    yc = y - mean
    var = _segsum64(yc * yc) * (1.0 / HEAD_B)
    yn = yc * lax.rsqrt(var + GN_EPS) * ln_g + ln_b
    o_b = (yn + _segsum64(r * k2 * r_k) * v) * gate
    return (_sigmoid(ga) * o_a + _sigmoid(gb) * o_b,)


def _loss(h3, target, gain, tb):
    rows, d = h3.shape

    def body(h_ref, t_ref, g_ref, dh_ref, dg_ref, l_ref):
        i = pl.program_id(0)
        live = (i * tb + lax.broadcasted_iota(jnp.int32, (tb, 1), 0)) >= CHUNK
        tgt = t_ref[...]

        def f(h, g):
            err = _rms(h, g) - tgt
            return 0.5 * jnp.sum(jnp.where(live, jnp.mean(err * err, axis=-1, keepdims=True), 0.0))

        val, vjp = jax.vjp(f, h_ref[...], g_ref[...])
        dh, dg = vjp(jnp.ones((), F32))
        dh_ref[...] = dh

        @pl.when(i == 0)
        def _():
            dg_ref[...] = jnp.zeros_like(dg_ref)
            l_ref[...] = jnp.zeros_like(l_ref)

        dg_ref[...] += dg
        l_ref[...] += jnp.full((1, 128), val, F32)

    blk = pl.BlockSpec((tb, d), lambda i: (i, 0))
    return pl.pallas_call(
        body, name="loss", grid=(rows // tb,),
        in_specs=[blk, blk, pl.BlockSpec((1, d), lambda i: (0, 0))],
        out_specs=[blk, pl.BlockSpec((1, d), lambda i: (0, 0)), pl.BlockSpec((1, 128), lambda i: (0, 0))],
        out_shape=[jax.ShapeDtypeStruct((rows, d), F32), jax.ShapeDtypeStruct((1, d), F32),
                   jax.ShapeDtypeStruct((1, 128), F32)],
        compiler_params=_cp("arbitrary"),
    )(h3, target, gain)


def _adamw_math(w, g, m, v):
    m2 = ADAM_B1 * m + (1.0 - ADAM_B1) * g
    v2 = ADAM_B2 * v + (1.0 - ADAM_B2) * (g * g)
    m_hat = m2 / (1.0 - ADAM_B1 ** ADAM_STEP)
    v_hat = v2 / (1.0 - ADAM_B2 ** ADAM_STEP)
    return -ADAM_LR * (m_hat / (jnp.sqrt(v_hat) + ADAM_EPS) + ADAM_WD * w), m2, v2


def _adamw(name, own, landed, w, m, v, rb):
    rows, cols = w.shape

    def body(o_ref, s_ref, w_ref, m_ref, v_ref, g_ref, d_ref, m2_ref, v2_ref):
        g = o_ref[...].astype(F32)
        for peer in range(N_DEV - 1):
            g = g + s_ref[peer].astype(F32)
        g_ref[...] = g
        d_ref[...], m2_ref[...], v2_ref[...] = _adamw_math(w_ref[...], g, m_ref[...], v_ref[...])

    blk = pl.BlockSpec((rb, cols), lambda i: (i, 0))
    return pl.pallas_call(
        body, name=name, grid=(rows // rb,),
        in_specs=[blk, pl.BlockSpec((N_DEV - 1, rb, cols), lambda i: (0, i, 0)), blk, blk, blk],
        out_specs=[blk] * 4, out_shape=[jax.ShapeDtypeStruct((rows, cols), F32)] * 4,
        compiler_params=_cp("parallel"),
    )(own, landed, w, m, v)


def _sum_slabs(name, slabs, rb):
    _, rows, cols = slabs.shape

    def body(s_ref, o_ref):
        g = s_ref[0]
        for dev in range(1, N_DEV):
            g = g + s_ref[dev]
        o_ref[...] = g

    return pl.pallas_call(
        body, name=name, grid=(rows // rb,),
        in_specs=[pl.BlockSpec((N_DEV, rb, cols), lambda i: (0, i, 0))],
        out_specs=pl.BlockSpec((rb, cols), lambda i: (i, 0)),
        out_shape=jax.ShapeDtypeStruct((rows, cols), F32), compiler_params=_cp("parallel"),
    )(slabs)


def _adamw_small(w, g, m, v):
    def body(w_ref, g_ref, m_ref, v_ref, d_ref, m2_ref, v2_ref):
        d_ref[...], m2_ref[...], v2_ref[...] = _adamw_math(w_ref[...], g_ref[...], m_ref[...], v_ref[...])

    return pl.pallas_call(body, name="adamw_small", out_shape=[jax.ShapeDtypeStruct(w.shape, F32)] * 3)(w, g, m, v)


def _place():
    return lax.axis_index("x"), lax.axis_index("y"), lax.axis_index("c")


def _index(p):
    return 4 * p[0] + 2 * p[1] + p[2]


def _all_gather(name, xs):
    n = len(xs)

    def body(*refs):
        x_refs, o_refs = refs[:n], refs[n:2 * n]
        send_sems, recv_sems, local_sems = refs[2 * n:]
        x, y, c = _place()
        me, sibling = (x, y, c), (x, y, 1 - c)
        chips = [(1 - x, y), (x, 1 - y), (1 - x, 1 - y)]

        def copy(i, k, block, to, src=None):
            dst = o_refs[i].at[_index(block)]
            return pltpu.make_async_remote_copy(src_ref=dst if src is None else src, dst_ref=dst,
                                                send_sem=send_sems.at[i, k], recv_sem=recv_sems.at[i, k],
                                                device_id=to, device_id_type=MESH_ID)

        mine = [pltpu.make_async_copy(x_refs[i], o_refs[i].at[_index(me)], local_sems.at[i]) for i in range(n)]
        for cp in mine:
            cp.start()
        first = []
        for i in range(n):
            first.append(copy(i, 0, me, sibling, src=x_refs[i]))
            first += [copy(i, 1 + j, me, (*chip, c), src=x_refs[i]) for j, chip in enumerate(chips)]
        for cp in first:
            cp.start()
        passed = []
        for j, chip in enumerate(chips):
            for i in range(n):
                copy(i, 1 + j, (*chip, c), me).wait_recv()
                cp = copy(i, 4 + j, (*chip, c), sibling)
                cp.start()
                passed.append(cp)
        for i in range(n):
            copy(i, 0, sibling, me).wait_recv()
            for j, chip in enumerate(chips):
                copy(i, 4 + j, (*chip, 1 - c), me).wait_recv()
        for cp in first + passed:
            cp.wait_send()
        for cp in mine:
            cp.wait()

    return pl.pallas_call(
        body, name=name, in_specs=[ANY] * n, out_specs=[ANY] * n,
        out_shape=[jax.ShapeDtypeStruct((N_DEV,) + x.shape, x.dtype) for x in xs],
        scratch_shapes=[pltpu.SemaphoreType.DMA((n, 7)), pltpu.SemaphoreType.DMA((n, 7)), pltpu.SemaphoreType.DMA((n,))],
    )(*xs)


def _exchange_start(name, xs, after=None):
    n = len(xs)
    copies = n * (N_DEV - 1)
    extra = [] if after is None else [after]

    def body(*refs):
        x_refs, land_refs = refs[:n], refs[n:2 * n]
        sems = refs[2 * n + len(extra):2 * n + len(extra) + 2 * copies]
        token = refs[-1]
        for i, k, peer in _exchange_copies(n):
            _exchange_copy(x_refs, land_refs, sems, i, k, peer).start()
        token[...] = jnp.zeros_like(token)

    lands = [lax.empty((N_DEV - 1,) + x.shape[1:], x.dtype) for x in xs]
    out = pl.pallas_call(
        body, name=name,
        out_shape=(*[pltpu.SemaphoreType.DMA(())] * (2 * copies), *[pltpu.HBM(x.shape, x.dtype) for x in xs],
                   *[pltpu.HBM(l.shape, l.dtype) for l in lands], jax.ShapeDtypeStruct((8, 128), F32)),
        in_specs=[HBM_SPEC] * (2 * n) + [ANY] * len(extra),
        out_specs=(*[SEM_SPEC] * (2 * copies), *[HBM_SPEC] * (2 * n), pl.BlockSpec(memory_space=pltpu.VMEM)),
        input_output_aliases={i: 2 * copies + i for i in range(2 * n)},
        compiler_params=pltpu.CompilerParams(has_side_effects=pltpu.SideEffectType.DATAFLOW_SIDE_EFFECTING),
    )(*[pltpu.with_memory_space_constraint(a, pltpu.HBM) for a in list(xs) + lands], *extra)
    sems, rest = list(out[:2 * copies]), out[2 * copies:]
    return sems, list(rest[:n]), list(rest[n:2 * n]), rest[-1]


def _exchange_copies(n):
    x, y, c = _place()
    for k in range(1, N_DEV):
        peer = ((1 - x) if k & 4 else x, (1 - y) if k & 2 else y, (1 - c) if k & 1 else c)
        for i in range(n):
            yield i, k - 1, peer


def _exchange_copy(x_refs, land_refs, sems, i, k, peer):
    copies = len(sems) // 2
    which = i * (N_DEV - 1) + k
    return pltpu.make_async_remote_copy(src_ref=x_refs[i].at[_index(peer)], dst_ref=land_refs[i].at[k],
                                        send_sem=sems[which], recv_sem=sems[copies + which],
                                        device_id=peer, device_id_type=MESH_ID)


def _exchange_wait(name, sems, xs, lands, after):
    n = len(xs)

    def body(*refs):
        x_refs, land_refs = refs[:n], refs[n:2 * n]
        sem_refs = refs[2 * n:2 * n + len(sems)]
        for i, k, peer in _exchange_copies(n):
            copy = _exchange_copy(x_refs, land_refs, sem_refs, i, k, peer)
            copy.wait_send()
            copy.wait_recv()

    out = pl.pallas_call(
        body, name=name,
        out_shape=(*[pltpu.HBM(x.shape, x.dtype) for x in xs], *[pltpu.HBM(l.shape, l.dtype) for l in lands]),
        in_specs=[HBM_SPEC] * (2 * n) + [SEM_SPEC] * len(sems) + [ANY], out_specs=tuple([HBM_SPEC] * (2 * n)),
        input_output_aliases={i: i for i in range(2 * n)},
        compiler_params=pltpu.CompilerParams(has_side_effects=pltpu.SideEffectType.DATAFLOW_SIDE_EFFECTING),
    )(*xs, *lands, *sems, after)
    return list(out[:n]), list(out[n:])


def _pack(arrays):
    flat = jnp.concatenate([a.reshape(-1) for a in arrays])
    pad = (-flat.shape[0]) % 1024
    return jnp.pad(flat, (0, pad)).reshape(-1, 128)


def _unpack(packed, shapes):
    flat = packed.reshape(-1)
    out, pos = [], 0
    for s in shapes:
        size = 1
        for dim in s:
            size *= dim
        out.append(flat[pos:pos + size].reshape(s))
        pos += size
    return out


def _cols_from_slabs(stack):
    return jnp.transpose(stack, (1, 0, 2)).reshape(stack.shape[1], -1)


def _cols_to_slabs(full):
    return jnp.transpose(full.reshape(full.shape[0], N_DEV, -1), (1, 0, 2))


def kernel(x, meta_tokens, ffn1_norm, ffn1_w_gu, ffn1_w_down, mix_norm, w_in, a_conv_w, a_log_rate, a_dt_bias, a_out_norm, b_shift_mu, b_w0, b_w_up, b_a0, b_a_up, b_g_up, b_k_k, b_k_a, b_r_k, b_ln_gain, b_ln_bias, w_out, ffn2_norm, ffn2_w_gu, ffn2_w_down, final_norm, loss_target, m_meta_tokens, m_ffn1_norm, m_ffn1_w_gu, m_ffn1_w_down, m_mix_norm, m_w_in, m_a_conv_w, m_a_log_rate, m_a_dt_bias, m_a_out_norm, m_b_shift_mu, m_b_w0, m_b_w_up, m_b_a0, m_b_a_up, m_b_g_up, m_b_k_k, m_b_k_a, m_b_r_k, m_b_ln_gain, m_b_ln_bias, m_w_out, m_ffn2_norm, m_ffn2_w_gu, m_ffn2_w_down, m_final_norm, v_meta_tokens, v_ffn1_norm, v_ffn1_w_gu, v_ffn1_w_down, v_mix_norm, v_w_in, v_a_conv_w, v_a_log_rate, v_a_dt_bias, v_a_out_norm, v_b_shift_mu, v_b_w0, v_b_w_up, v_b_a0, v_b_a_up, v_b_g_up, v_b_k_k, v_b_k_a, v_b_r_k, v_b_ln_gain, v_b_ln_bias, v_w_out, v_ffn2_norm, v_ffn2_w_gu, v_ffn2_w_down, v_final_norm):
    names = ['meta_tokens', 'ffn1_norm', 'ffn1_w_gu', 'ffn1_w_down', 'mix_norm', 'w_in', 'a_conv_w', 'a_log_rate',
             'a_dt_bias', 'a_out_norm', 'b_shift_mu', 'b_w0', 'b_w_up', 'b_a0', 'b_a_up', 'b_g_up', 'b_k_k', 'b_k_a',
             'b_r_k', 'b_ln_gain', 'b_ln_bias', 'w_out', 'ffn2_norm', 'ffn2_w_gu', 'ffn2_w_down', 'final_norm']
    env = dict(locals())
    wts = {k: env[k] for k in names}
    mom_m = {k: env['m_' + k] for k in names}
    mom_v = {k: env['v_' + k] for k in names}
    big = ['ffn1_w_gu', 'ffn1_w_down', 'w_in', 'w_out', 'ffn2_w_gu', 'ffn2_w_down']
    col_sharded = {'ffn1_w_gu', 'w_in', 'ffn2_w_gu'}
    small_sharded = ['meta_tokens', 'a_conv_w', 'b_w_up', 'b_a_up', 'b_g_up']
    replicated = [k for k in names if k not in big and k not in small_sharded]

    seq, d = x.shape[1], x.shape[2]
    rows = PAD + N_META + seq
    heads_a = d // HEAD_A
    tb_mm = _tb(rows, 416)
    tb_vjp = _tb(rows, 208)
    me = _index(_place())

    big_local = [wts[k][0].astype(BF16) for k in big]
    small_local = [wts['meta_tokens']] + [wts[k][0] for k in small_sharded[1:]]
    gathered = _all_gather("gather_weights", big_local + small_local)
    gw = dict(zip(big + small_sharded, gathered))
    full = {k: (gw[k] if k in ('ffn1_w_gu', 'ffn2_w_gu') else _cols_from_slabs(gw[k]) if k in col_sharded
                else gw[k].reshape(-1, gw[k].shape[-1])) for k in big}
    for k in small_sharded:
        full[k] = _cols_from_slabs(gw[k])
    for k in replicated:
        full[k] = wts[k].reshape(1, -1)

    win = full['w_in']
    n_b = 3 * d + LORA_W + LORA_A + LORA_G
    off_beta, off_b = 4 * d, 4 * d + 2 * heads_a
    off_ga = off_b + n_b
    b_width = 3 * d + LORA_PAD
    zcols = lambda r, c: jnp.zeros((r, c), BF16)
    w_qkv = win[:, :3 * d]
    w_zg = jnp.concatenate([win[:, 3 * d:4 * d], win[:, off_ga:off_ga + 2 * d]], axis=1)
    w_b = jnp.concatenate([win[:, off_b:off_b + n_b], zcols(d, b_width - n_b)], axis=1)
    w_bg = jnp.concatenate([win[:, off_beta:off_beta + 2 * heads_a], zcols(d, 128 - 2 * heads_a)], axis=1)

    def lanes(vec, start, width):
        return jnp.pad(vec.reshape(1, -1), ((0, 0), (start, width - start - vec.size)))

    log_rate = lanes(wts['a_log_rate'], heads_a, 128)
    dt_bias = lanes(wts['a_dt_bias'], heads_a, 128)
    mu = lanes(wts['b_shift_mu'], 0, b_width)
    w_up = jnp.pad(full['b_w_up'], ((0, 128 - LORA_W), (0, 0)))
    a_up = jnp.pad(full['b_a_up'], ((LORA_W, 0), (0, 0)))
    g_up = jnp.pad(full['b_g_up'], ((0, 256 - LORA_G), (0, 0)))
    b_pars = [full['b_w0'], w_up, full['b_a0'], a_up, g_up, full['b_k_k'], full['b_k_a']]
    post_pars = [full['a_out_norm'], full['b_r_k'], full['b_ln_gain'], full['b_ln_bias']]
    bg_fn = _make_bg_fn(heads_a)

    h0 = jnp.concatenate([jnp.zeros((PAD, d), F32), full['meta_tokens'], x[0]], axis=0)
    h1 = _ffn_fwd("ffn1_fwd", h0, full['ffn1_norm'], full['ffn1_w_gu'], full['ffn1_w_down'], tb_mm)
    (u,) = _tok_fwd("mix_norm_fwd", _norm_fn, [h1], [full['mix_norm']], [(d, BF16)], tb_mm)
    z_qkv = _mm("in_qkv", u, w_qkv, tb=tb_mm, tn=_col_tile(3 * d, 1536))
    z_zg = _mm("in_zg", u, w_zg, tb=tb_mm, tn=_col_tile(3 * d, 1536))
    z_b = _mm("in_b", u, w_b, tb=tb_mm, tn=_col_tile(b_width, 1536))
    z_bg = _mm("in_bg", u, w_bg, tb=tb_mm, tn=128)
    qkv = _a_pre_fwd(z_qkv, full['a_conv_w'])
    (bg,) = _tok_fwd("bg_fwd", bg_fn, [z_bg], [log_rate, dt_bias], [(128, F32)], tb_mm)
    o_dn, dn_hist = _dn_fwd(qkv, bg)
    zf = _shift_fwd(z_b, mu)
    rr, ww, kk2, vv, av, bv, gate = _tok_fwd("b_pre_fwd", _b_pre_fn, [zf], b_pars, [(d, F32)] * 7, tb_vjp)
    per_head = lambda t: t.reshape(rows, d // HEAD_B, HEAD_B)
    y_heads, b_hist, b_last = _rwkv_fwd(rr, ww, kk2, per_head(vv), av, bv)
    y_b = y_heads.reshape(rows, d)
    post_toks = [o_dn, z_zg, y_b, rr, kk2, vv, gate]
    (merged,) = _tok_fwd("post_fwd", _post_fn, post_toks, post_pars, [(d, BF16)], tb_vjp)
    h2 = _mm("out_proj", merged, full['w_out'], add=h1, tb=tb_mm, tn=d)
    h3 = _ffn_fwd("ffn2_fwd", h2, full['ffn2_norm'], full['ffn2_w_gu'], full['ffn2_w_down'], tb_mm)

    target = jnp.pad(loss_target[0], ((CHUNK, 0), (0, 0)))
    dh3, g_final, loss_part = _loss(h3, target, full['final_norm'].reshape(1, d), tb_vjp)

    def ffn_backward(tag, h, dout, key_norm, key_gu, key_down):
        dh, dh_bf, dgain, xn, act, dgate, dup, dhalf = _ffn_bwd(tag + "_bwd", h, full[key_norm], dout, full[key_gu],
                                                                full[key_down], tb_mm)
        d_gu = jnp.concatenate([_mm_tn_to_slabs(tag + "_dw_gate", xn, dgate, tk=tb_mm),
                                _mm_tn_to_slabs(tag + "_dw_up", xn, dup, tk=tb_mm)], axis=0)
        d_down = _mm_tn_from_slabs(tag + "_dw_down", act, dhalf, tk=tb_mm).reshape(N_DEV, -1, d)
        return dh, dh_bf, dgain, d_gu, d_down

    dh2, dh2_bf, g_ffn2_norm, g_ffn2_gu, g_ffn2_down = ffn_backward("ffn2", h2, dh3, 'ffn2_norm', 'ffn2_w_gu',
                                                                    'ffn2_w_down')
    g_w_out = _mm_tn("dw_out", merged, dh2_bf, tm=d, tn=d, tk=tb_mm).reshape(N_DEV, -1, d)

    def start_exchange(tag, keys, slabs, after=None):
        sems, kept, lands, token = _exchange_start("exchange_start_" + tag, slabs, after)
        return (tag, keys, sems, kept, lands), token

    ex_ffn2, token_ffn2 = start_exchange("ffn2", ['ffn2_w_gu', 'ffn2_w_down', 'w_out'], [g_ffn2_gu, g_ffn2_down, g_w_out])
    dmerged = _mm("d_merged", dh2_bf, full['w_out'], trans_b=True, after=token_ffn2, tb=tb_mm, tn=d)
    post_grads = _tok_bwd("post_bwd", _post_fn, post_toks, post_pars, [[dmerged]], list(range(7)), tb_vjp,
                          [F32, BF16] + [F32] * 5)
    do_dn, dz_zg, dy_b, dr1, dk1, dv1, dgate = post_grads[:7]
    g_out_norm, g_r_k, g_ln_g, g_ln_b = post_grads[7:]
    dr2, dw2, dk2, dv_heads, da2, db2 = _rwkv_bwd(rr, ww, kk2, per_head(vv), av, bv, b_hist, b_last, per_head(dy_b))
    dv2 = dv_heads.reshape(rows, d)
    b_grads = _tok_bwd("b_pre_bwd", _b_pre_fn, [zf], b_pars,
                       [[dr1, dr2], [dw2], [dk1, dk2], [dv1, dv2], [da2], [db2], [dgate]], [0], tb_vjp)
    dzf = b_grads[0]
    g_w0, g_w_up, g_a0, g_a_up, g_g_up, g_k_k, g_k_a = b_grads[1:]
    dz_b, g_mu = _shift_bwd(z_b, mu, dzf)
    dqkv, dbg = _dn_bwd(qkv, bg, dn_hist, do_dn)
    dz_qkv, g_conv = _a_pre_bwd(z_qkv, full['a_conv_w'], dqkv)
    dz_bg, g_log_rate, g_dt_bias = _tok_bwd("bg_bwd", bg_fn, [z_bg], [log_rate, dt_bias], [[dbg]], [0], tb_mm, [BF16])

    du = None
    g_w_in_parts = []
    for tag, dz, wpiece in (("qkv", dz_qkv, w_qkv), ("zg", dz_zg, w_zg), ("b", dz_b, w_b), ("bg", dz_bg, w_bg)):
        du = _mm("du_" + tag, dz, wpiece, trans_b=True, add=du, tb=tb_mm, tn=d)
        g_w_in_parts.append(_mm_tn("dw_in_" + tag, u, dz, tm=d, tn=_col_tile(dz.shape[1], 1536), tk=tb_mm))
    gp_qkv, gp_zg, gp_b, gp_bg = g_w_in_parts
    g_w_in = _cols_to_slabs(jnp.concatenate(
        [gp_qkv, gp_zg[:, :d], gp_bg[:, :2 * heads_a], gp_b[:, :n_b], gp_zg[:, d:]], axis=1))
    ex_w_in, token_w_in = start_exchange("w_in", ['w_in'], [g_w_in])
    dh1, g_mix_norm = _tok_bwd("mix_norm_bwd", _norm_res_fn, [h1], [full['mix_norm']], [[du], [dh2]], [0], tb_vjp,
                               after=token_w_in)
    dh0, _, g_ffn1_norm, g_ffn1_gu, g_ffn1_down = ffn_backward("ffn1", h0, dh1, 'ffn1_norm', 'ffn1_w_gu', 'ffn1_w_down')

    small_full = {
        'meta_tokens': dh0[PAD:CHUNK], 'ffn1_norm': g_ffn1_norm, 'mix_norm': g_mix_norm, 'a_conv_w': g_conv,
        'a_log_rate': g_log_rate[:, heads_a:2 * heads_a], 'a_dt_bias': g_dt_bias[:, heads_a:2 * heads_a],
        'a_out_norm': g_out_norm, 'b_shift_mu': g_mu[:, :n_b], 'b_w0': g_w0, 'b_w_up': g_w_up[:LORA_W],
        'b_a0': g_a0, 'b_a_up': g_a_up[LORA_W:], 'b_g_up': g_g_up[:LORA_G], 'b_k_k': g_k_k, 'b_k_a': g_k_a,
        'b_r_k': g_r_k, 'b_ln_gain': g_ln_g, 'b_ln_bias': g_ln_b, 'ffn2_norm': g_ffn2_norm, 'final_norm': g_final,
    }
    small_names = list(small_full)
    packed = _pack([small_full[k] for k in small_names] + [loss_part[:, :1]])
    (all_parts,) = _all_gather("gather_small_grads", [packed])
    ex_ffn1, _ = start_exchange("ffn1", ['ffn1_w_gu', 'ffn1_w_down'], [g_ffn1_gu, g_ffn1_down], after=all_parts)
    summed = _sum_slabs("sum_small_grads", all_parts, packed.shape[0])
    pieces = _unpack(summed, [small_full[k].shape for k in small_names] + [(1, 1)])
    small_grad = dict(zip(small_names, pieces[:-1]))
    loss = pieces[-1].reshape(())

    grads, deltas, new_m, new_v = {}, {}, {}, {}
    local_small = {}
    for k in small_names:
        g = small_grad[k]
        if k in small_sharded:
            width = wts[k].shape[-1]
            g = lax.dynamic_slice_in_dim(g, me * width, width, axis=1)
        local_small[k] = g.reshape(wts[k].shape)
    pk = lambda tree: _pack([tree[k] for k in small_names])
    dl_s, m_s, v_s = _adamw_small(pk(wts), pk(local_small), pk(mom_m), pk(mom_v))
    shapes = [wts[k].shape for k in small_names]
    for k, dl, m2, v2 in zip(small_names, _unpack(dl_s, shapes), _unpack(m_s, shapes), _unpack(v_s, shapes)):
        grads[k], deltas[k], new_m[k], new_v[k] = local_small[k], dl, m2, v2

    done = dl_s
    for tag, keys, sems, kept, lands in (ex_ffn2, ex_w_in, ex_ffn1):
        kept, lands = _exchange_wait("exchange_wait_" + tag, sems, kept, lands, done)
        for k, slabs, landed in zip(keys, kept, lands):
            shard = wts[k][0]
            rb = _tb(shard.shape[0], 128)
            own = lax.dynamic_index_in_dim(slabs, me, axis=0, keepdims=False)
            g, dl, m2, v2 = _adamw("adamw_" + k, own, landed, shard, mom_m[k][0], mom_v[k][0], rb)
            grads[k], deltas[k], new_m[k], new_v[k] = g[None], dl[None], m2[None], v2[None]
            done = dl

    grad_x = dh0[CHUNK:][None]
    return (loss, grad_x, *[grads[k] for k in names], *[deltas[k] for k in names],
            *[new_m[k] for k in names], *[new_v[k] for k in names])
```

```python
import functools

import jax
import jax.numpy as jnp
from jax import lax
from jax.experimental import pallas as pl
from jax.experimental.pallas import tpu as pltpu

F32 = jnp.float32
BF16 = jnp.bfloat16
N_DEV = 8
N_META = 16
CHUNK = 64
PAD = CHUNK - N_META
HEAD_A = 128
HEAD_B = 64
LORA_W, LORA_A, LORA_G = 64, 64, 160
LORA_PAD = 384
EPS = 1e-6
GN_EPS = HEAD_B * 1e-5
ADAM_LR, ADAM_B1, ADAM_B2, ADAM_EPS, ADAM_WD, ADAM_STEP = 0.001, 0.9, 0.999, 1e-08, 0.01, 10
SCAN_STEPS = 16
MXU_WIDTH = 256
VMEM_LIMIT = 56 * 1024 * 1024
DN_PRECISION = lax.Precision.HIGH
MESH_ID = pl.DeviceIdType.MESH
ANY = pl.BlockSpec(memory_space=pl.ANY)
HBM_SPEC = pl.BlockSpec(memory_space=pltpu.HBM)
SEM_SPEC = pl.BlockSpec(memory_space=pltpu.SEMAPHORE)


def _cp(*sem):
    return pltpu.CompilerParams(dimension_semantics=sem, vmem_limit_bytes=VMEM_LIMIT)


def _tb(t, target):
    best = 16
    for d in range(16, target + 1, 16):
        if t % d == 0:
            best = d
    return best


def _sigmoid(x):
    return 1.0 / (1.0 + jnp.exp(-x))


def _silu(x):
    return x * _sigmoid(x)


def _softplus(x):
    return jnp.maximum(x, 0.0) + jnp.log(1.0 + jnp.exp(-jnp.abs(x)))


def _dot_nt(a, b, precision=None):
    return lax.dot_general(a, b, (((1,), (1,)), ((), ())), preferred_element_type=F32, precision=precision)


def _dot_tn(a, b, precision=None):
    return lax.dot_general(a, b, (((0,), (0,)), ((), ())), preferred_element_type=F32, precision=precision)


def _dot(a, b, precision=None):
    return jnp.dot(a, b, preferred_element_type=F32, precision=precision)


def _block_diag_ones():
    i = lax.broadcasted_iota(jnp.int32, (MXU_WIDTH, MXU_WIDTH), 0) // HEAD_B
    j = lax.broadcasted_iota(jnp.int32, (MXU_WIDTH, MXU_WIDTH), 1) // HEAD_B
    return (i == j).astype(BF16)


def _hi_lo(x):
    hi = x.astype(BF16)
    return hi, (x - hi.astype(F32)).astype(BF16)


def _segsum_many(xs, bd):
    groups = [x if isinstance(x, tuple) else _hi_lo(x) for x in xs]
    rows = groups[0][0].shape[0]
    stacked = jnp.concatenate([p for grp in groups for p in grp], axis=0)
    out = jnp.concatenate([_dot(stacked[:, s:s + MXU_WIDTH], bd) for s in range(0, stacked.shape[1], MXU_WIDTH)], axis=1)
    res, pos = [], 0
    for grp in groups:
        acc = out[pos:pos + rows]
        for j in range(1, len(grp)):
            acc = acc + out[pos + j * rows:pos + (j + 1) * rows]
        res.append(acc)
        pos += len(grp) * rows
    return res


def _segsum_impl(x):
    return _segsum_many([x], _block_diag_ones())[0]


@jax.custom_vjp
def _segsum64(x):
    return _segsum_impl(x)


_segsum64.defvjp(lambda x: (_segsum_impl(x), None), lambda _, ct: (_segsum_impl(ct),))


def _tok(t):
    return t if isinstance(t, tuple) else (t, t.shape[1], 0)


def _tok_spec(tb, width, colblk):
    return pl.BlockSpec((tb, width), lambda i: (i, colblk))


def _par_spec(p):
    return pl.BlockSpec(p.shape, lambda i: (0, 0))


def _tok_fwd(name, fn, toks, pars, outs, tb):
    toks = [_tok(t) for t in toks]
    rows = toks[0][0].shape[0]
    n_in = len(toks) + len(pars)

    def body(*refs):
        row0 = pl.program_id(0) * tb
        res = fn(row0, *[r[...] for r in refs[:n_in]])
        for r, o in zip(refs[n_in:], res):
            r[...] = o.astype(r.dtype)

    return pl.pallas_call(
        body, name=name, grid=(rows // tb,),
        in_specs=[_tok_spec(tb, w, c) for _, w, c in toks] + [_par_spec(p) for p in pars],
        out_specs=[_tok_spec(tb, w, 0) for w, _ in outs],
        out_shape=[jax.ShapeDtypeStruct((rows, w), dt) for w, dt in outs],
        compiler_params=_cp("parallel"),
    )(*[a for a, _, _ in toks], *pars)


def _tok_bwd(name, fn, toks, pars, cts, want, tb, want_dtypes=None, after=None):
    toks = [_tok(t) for t in toks]
    want_dtypes = want_dtypes or [F32] * len(want)
    cts = [[_tok(c) for c in group] for group in cts]
    flat_cts = [c for group in cts for c in group]
    rows = toks[0][0].shape[0]
    n_tok, n_par, n_ct = len(toks), len(pars), len(flat_cts)
    extra = [] if after is None else [after]

    def body(*refs):
        i = pl.program_id(0)
        row0 = i * tb
        prim = [r[...].astype(F32) for r in refs[:n_tok + n_par]]
        ct_refs = list(refs[n_tok + n_par:n_tok + n_par + n_ct])
        out_refs = refs[n_tok + n_par + n_ct + len(extra):]
        res, vjp = jax.vjp(lambda *a: fn(row0, *a), *prim)
        ct = []
        for group, o in zip(cts, res):
            acc = None
            for _ in group:
                v = ct_refs.pop(0)[...].astype(F32)
                acc = v if acc is None else acc + v
            ct.append(acc.astype(o.dtype))
        grads = vjp(tuple(ct))
        for r, k in zip(out_refs[:len(want)], want):
            r[...] = grads[k].astype(r.dtype)

        @pl.when(i == 0)
        def _():
            for r in out_refs[len(want):]:
                r[...] = jnp.zeros_like(r)

        for r, g in zip(out_refs[len(want):], grads[n_tok:]):
            r[...] += g

    return pl.pallas_call(
        body, name=name, grid=(rows // tb,),
        in_specs=[_tok_spec(tb, w, c) for _, w, c in toks] + [_par_spec(p) for p in pars]
        + [_tok_spec(tb, w, c) for _, w, c in flat_cts] + [ANY] * len(extra),
        out_specs=[_tok_spec(tb, toks[k][1], 0) for k in want] + [_par_spec(p) for p in pars],
        out_shape=[jax.ShapeDtypeStruct((rows, toks[k][1]), dt) for k, dt in zip(want, want_dtypes)]
        + [jax.ShapeDtypeStruct(p.shape, F32) for p in pars],
        compiler_params=_cp("arbitrary"),
    )(*[a for a, _, _ in toks], *pars, *[a for a, _, _ in flat_cts], *extra)


def _mm(name, a, b, *, trans_b=False, add=None, after=None, tb, tn):
    rows, k = a.shape
    n = b.shape[0] if trans_b else b.shape[1]

    def body(*refs):
        a_ref, b_ref = refs[:2]
        o_ref = refs[-1]
        acc = _dot_nt(a_ref[...], b_ref[...]) if trans_b else _dot(a_ref[...], b_ref[...])
        if add is not None:
            acc = acc + refs[2][...]
        o_ref[...] = acc

    in_specs = [pl.BlockSpec((tb, k), lambda i, j: (i, 0)),
                pl.BlockSpec((tn, k), lambda i, j: (j, 0)) if trans_b else pl.BlockSpec((k, tn), lambda i, j: (0, j))]
    args = [a, b]
    if add is not None:
        in_specs.append(pl.BlockSpec((tb, tn), lambda i, j: (i, j)))
        args.append(add)
    if after is not None:
        in_specs.append(ANY)
        args.append(after)
    return pl.pallas_call(
        body, name=name, grid=(rows // tb, n // tn), in_specs=in_specs,
        out_specs=pl.BlockSpec((tb, tn), lambda i, j: (i, j)),
        out_shape=jax.ShapeDtypeStruct((rows, n), F32),
        compiler_params=_cp("parallel", "parallel"),
    )(*args)


def _mm_tn_call(name, grid, a, b, a_spec, b_spec, o_spec, acc_shape, out_shape):
    last = len(grid) - 1

    def body(a_ref, b_ref, o_ref, acc_ref):
        k = pl.program_id(last)

        @pl.when(k == 0)
        def _():
            acc_ref[...] = jnp.zeros_like(acc_ref)

        a_blk = a_ref[0] if len(a_ref.shape) == 3 else a_ref[...]
        b_blk = b_ref[0] if len(b_ref.shape) == 3 else b_ref[...]
        acc_ref[...] += _dot_tn(a_blk, b_blk)

        @pl.when(k == grid[last] - 1)
        def _():
            if len(o_ref.shape) == 3:
                o_ref[0] = acc_ref[...].astype(o_ref.dtype)
            else:
                o_ref[...] = acc_ref[...].astype(o_ref.dtype)

    return pl.pallas_call(
        body, name=name, grid=grid, in_specs=[a_spec, b_spec], out_specs=o_spec,
        out_shape=jax.ShapeDtypeStruct(out_shape, BF16), scratch_shapes=[pltpu.VMEM(acc_shape, F32)],
        compiler_params=_cp(*(["parallel"] * last + ["arbitrary"])),
    )(a, b)


def _mm_tn(name, a, b, *, tm, tn, tk):
    rows, m = a.shape
    n = b.shape[1]
    return _mm_tn_call(name, (m // tm, n // tn, rows // tk), a, b,
                       pl.BlockSpec((tk, tm), lambda i, j, k: (k, i)), pl.BlockSpec((tk, tn), lambda i, j, k: (k, j)),
                       pl.BlockSpec((tm, tn), lambda i, j, k: (i, j)), (tm, tn), (m, n))


def _mm_tn_to_slabs(name, a, b3, *, tk):
    rows, m = a.shape
    s, _, c = b3.shape
    return _mm_tn_call(name, (s, rows // tk), a, b3,
                       pl.BlockSpec((tk, m), lambda i, k: (k, 0)), pl.BlockSpec((1, tk, c), lambda i, k: (i, k, 0)),
                       pl.BlockSpec((1, m, c), lambda i, k: (i, 0, 0)), (m, c), (s, m, c))


def _mm_tn_from_slabs(name, a3, b, *, tk):
    s, rows, c = a3.shape
    n = b.shape[1]
    return _mm_tn_call(name, (s, rows // tk), a3, b,
                       pl.BlockSpec((1, tk, c), lambda i, k: (i, k, 0)), pl.BlockSpec((tk, n), lambda i, k: (k, 0)),
                       pl.BlockSpec((c, n), lambda i, k: (i, 0)), (c, n), (s * c, n))


def _col_tile(n, target):
    if n <= target:
        return n
    best = 128
    for d in range(128, target + 1, 128):
        if n % d == 0:
            best = d
    return best


def _rms(x, gain):
    return x * lax.rsqrt(jnp.mean(x * x, axis=-1, keepdims=True) + EPS) * gain


def _ffn_specs(d, fc, nj):
    return [pl.BlockSpec((1, d, fc), lambda i, j: (j, 0, 0)), pl.BlockSpec((1, d, fc), lambda i, j: (nj + j, 0, 0)),
            pl.BlockSpec((fc, d), lambda i, j: (j, 0))]


def _ffn_fwd(name, h, gain, wgu, wd, tb):
    rows, d = h.shape
    nj = wgu.shape[0] // 2
    fc = wgu.shape[2]

    def body(h_ref, g_ref, wg_ref, wu_ref, wd_ref, o_ref, xn_s, acc_s):
        j = pl.program_id(1)

        @pl.when(j == 0)
        def _():
            xn_s[...] = _rms(h_ref[...], g_ref[...]).astype(BF16)
            acc_s[...] = jnp.zeros_like(acc_s)

        wg, wu, wdn = wg_ref[0], wu_ref[0], wd_ref[...]
        for half in range(2):
            rs = pl.ds(half * (tb // 2), tb // 2)
            xn = xn_s[rs, :]
            gate = _dot(xn, wg)
            up = _dot(xn, wu)
            acc_s[rs, :] += _dot((_silu(gate) * up).astype(BF16), wdn)

        @pl.when(j == nj - 1)
        def _():
            o_ref[...] = h_ref[...] + 0.5 * acc_s[...]

    return pl.pallas_call(
        body, name=name, grid=(rows // tb, nj),
        in_specs=[pl.BlockSpec((tb, d), lambda i, j: (i, 0)), pl.BlockSpec((1, d), lambda i, j: (0, 0))]
        + _ffn_specs(d, fc, nj),
        out_specs=pl.BlockSpec((tb, d), lambda i, j: (i, 0)),
        out_shape=jax.ShapeDtypeStruct((rows, d), F32),
        scratch_shapes=[pltpu.VMEM((tb, d), BF16), pltpu.VMEM((tb, d), F32)],
        compiler_params=_cp("parallel", "arbitrary"),
    )(h, gain, wgu, wgu, wd)


def _ffn_bwd(name, h, gain, dout, wgu, wd, tb):
    rows, d = h.shape
    nj = wgu.shape[0] // 2
    fc = wgu.shape[2]

    def body(h_ref, g_ref, do_ref, wg_ref, wu_ref, wd_ref,
             dh_ref, dhb_ref, dg_ref, xn_ref, act_ref, dgate_ref, dup_ref, dhalf_ref, dxn_s):
        i, j = pl.program_id(0), pl.program_id(1)

        @pl.when(j == 0)
        def _():
            xn_ref[...] = _rms(h_ref[...], g_ref[...]).astype(BF16)
            dhalf_ref[...] = (0.5 * do_ref[...]).astype(BF16)
            dxn_s[...] = jnp.zeros_like(dxn_s)

        wg, wu, wdn = wg_ref[0], wu_ref[0], wd_ref[...]
        for half in range(2):
            rs = pl.ds(half * (tb // 2), tb // 2)
            xn = xn_ref[rs, :]
            gate = _dot(xn, wg)
            up = _dot(xn, wu)
            sg = _sigmoid(gate)
            dact = _dot_nt(dhalf_ref[rs, :], wdn)
            act_ref[0, rs, :] = (gate * sg * up).astype(BF16)
            dgate = (dact * up * (sg * (1.0 + gate * (1.0 - sg)))).astype(BF16)
            dup = (dact * gate * sg).astype(BF16)
            dgate_ref[0, rs, :] = dgate
            dup_ref[0, rs, :] = dup
            dxn_s[rs, :] += _dot_nt(dgate, wg) + _dot_nt(dup, wu)

        @pl.when((i == 0) & (j == 0))
        def _():
            dg_ref[...] = jnp.zeros_like(dg_ref)

        @pl.when(j == nj - 1)
        def _():
            x = h_ref[...]
            r = lax.rsqrt(jnp.mean(x * x, axis=-1, keepdims=True) + EPS)
            dxn = dxn_s[...]
            dyg = dxn * g_ref[...]
            dh = do_ref[...] + r * dyg - x * (r * r * r) * jnp.mean(dyg * x, axis=-1, keepdims=True)
            dh_ref[...] = dh
            dhb_ref[...] = dh.astype(BF16)
            dg_ref[...] += jnp.sum(dxn * x * r, axis=0, keepdims=True)

    row_d = pl.BlockSpec((tb, d), lambda i, j: (i, 0))
    slab = pl.BlockSpec((1, tb, fc), lambda i, j: (j, i, 0))
    hidden = jax.ShapeDtypeStruct((nj, rows, fc), BF16)
    return pl.pallas_call(
        body, name=name, grid=(rows // tb, nj),
        in_specs=[row_d, pl.BlockSpec((1, d), lambda i, j: (0, 0)), row_d] + _ffn_specs(d, fc, nj),
        out_specs=[row_d, row_d, pl.BlockSpec((1, d), lambda i, j: (0, 0)), row_d, slab, slab, slab, row_d],
        out_shape=[jax.ShapeDtypeStruct((rows, d), F32), jax.ShapeDtypeStruct((rows, d), BF16),
                   jax.ShapeDtypeStruct((1, d), F32), jax.ShapeDtypeStruct((rows, d), BF16),
                   hidden, hidden, hidden, jax.ShapeDtypeStruct((rows, d), BF16)],
        scratch_shapes=[pltpu.VMEM((tb, d), F32)],
        compiler_params=_cp("arbitrary", "arbitrary"),
    )(h, gain, dout, wgu, wgu, wd)


def _shift_rows(x, s):
    return pltpu.roll(x, s % x.shape[0], 0)


def _a_post(c, which):
    s = _silu(c)
    n = s * lax.rsqrt(jnp.sum(s * s, axis=-1, keepdims=True) + 1e-6)
    scale = jnp.where(which == 0, HEAD_A ** -0.5, 1.0)
    return jnp.where(which == 2, s, n * scale)


def _conv(x, w):
    return x * w[3:4] + _shift_rows(x, 1) * w[2:3] + _shift_rows(x, 2) * w[1:2] + _shift_rows(x, 3) * w[0:1]


def _a_pre_fwd(zqkv, conv_w):
    rows, width = zqkv.shape
    heads = width // (3 * HEAD_A)

    def body(x_ref, w_ref, o_ref):
        which = pl.program_id(0) // heads
        live = lax.broadcasted_iota(jnp.int32, (rows, HEAD_A), 0) >= PAD
        o_ref[...] = jnp.where(live, _a_post(_conv(x_ref[...], w_ref[...]), which), 0.0)

    return pl.pallas_call(
        body, name="a_pre_fwd", grid=(width // HEAD_A,),
        in_specs=[pl.BlockSpec((rows, HEAD_A), lambda c: (0, c)), pl.BlockSpec((4, HEAD_A), lambda c: (0, c))],
        out_specs=pl.BlockSpec((rows, HEAD_A), lambda c: (0, c)),
        out_shape=jax.ShapeDtypeStruct((rows, width), F32),
        compiler_params=_cp("parallel"),
    )(zqkv, conv_w)


def _a_pre_bwd(zqkv, conv_w, dqkv):
    rows, width = zqkv.shape
    heads = width // (3 * HEAD_A)

    def body(x_ref, w_ref, ct_ref, dx_ref, dw_ref):
        which = pl.program_id(0) // heads
        live = lax.broadcasted_iota(jnp.int32, (rows, HEAD_A), 0) >= PAD
        x, w = x_ref[...], w_ref[...]
        _, vjp = jax.vjp(lambda c: _a_post(c, which), _conv(x, w))
        (dc,) = vjp(jnp.where(live, ct_ref[...], 0.0))
        dc = jnp.where(live, dc, 0.0)
        dx_ref[...] = (dc * w[3:4] + _shift_rows(dc, -1) * w[2:3] + _shift_rows(dc, -2) * w[1:2]
                       + _shift_rows(dc, -3) * w[0:1]).astype(BF16)
        dw_ref[...] = jnp.concatenate(
            [jnp.sum(dc * (_shift_rows(x, 3 - j) if j < 3 else x), axis=0, keepdims=True) for j in range(4)], axis=0)

    col = pl.BlockSpec((rows, HEAD_A), lambda c: (0, c))
    wsp = pl.BlockSpec((4, HEAD_A), lambda c: (0, c))
    return pl.pallas_call(
        body, name="a_pre_bwd", grid=(width // HEAD_A,),
        in_specs=[col, wsp, col], out_specs=[col, wsp],
        out_shape=[jax.ShapeDtypeStruct((rows, width), BF16), jax.ShapeDtypeStruct((4, width), F32)],
        compiler_params=_cp("parallel"),
    )(zqkv, conv_w, dqkv)


SHIFT_TILE = 384


def _shift_fwd(zb, mu):
    rows, width = zb.shape

    def body(x_ref, mu_ref, o_ref):
        x = x_ref[...]
        first = lax.broadcasted_iota(jnp.int32, x.shape, 0) == 0
        prev = jnp.where(first, 0.0, _shift_rows(x, 1))
        o_ref[...] = x + (prev - x) * mu_ref[...]

    col = pl.BlockSpec((rows, SHIFT_TILE), lambda c: (0, c))
    return pl.pallas_call(
        body, name="shift_fwd", grid=(width // SHIFT_TILE,),
        in_specs=[col, pl.BlockSpec((1, SHIFT_TILE), lambda c: (0, c))], out_specs=col,
        out_shape=jax.ShapeDtypeStruct((rows, width), F32), compiler_params=_cp("parallel"),
    )(zb, mu)


def _shift_bwd(zb, mu, dzf):
    rows, width = zb.shape

    def body(x_ref, mu_ref, ct_ref, dx_ref, dmu_ref):
        x, ct, mu_v = x_ref[...], ct_ref[...], mu_ref[...]
        row = lax.broadcasted_iota(jnp.int32, x.shape, 0)
        prev = jnp.where(row == 0, 0.0, _shift_rows(x, 1))
        nxt = jnp.where(row == rows - 1, 0.0, _shift_rows(ct, -1))
        dx_ref[...] = (ct * (1.0 - mu_v) + nxt * mu_v).astype(BF16)
        dmu_ref[...] = jnp.sum(ct * (prev - x), axis=0, keepdims=True)

    col = pl.BlockSpec((rows, SHIFT_TILE), lambda c: (0, c))
    msp = pl.BlockSpec((1, SHIFT_TILE), lambda c: (0, c))
    return pl.pallas_call(
        body, name="shift_bwd", grid=(width // SHIFT_TILE,),
        in_specs=[col, msp, col], out_specs=[col, msp],
        out_shape=[jax.ShapeDtypeStruct((rows, width), BF16), jax.ShapeDtypeStruct((1, width), F32)],
        compiler_params=_cp("parallel"),
    )(zb, mu, dzf)


def _dn_chunk(q, k, v, beta, g, state):
    heads = range(len(q))
    ri = lax.broadcasted_iota(jnp.int32, (CHUNK, CHUNK), 0)
    ci = lax.broadcasted_iota(jnp.int32, (CHUNK, CHUNK), 1)
    eye = (ri == ci).astype(F32)
    incl = ri >= ci
    last = lax.broadcasted_iota(jnp.int32, (CHUNK, 1), 0) == CHUNK - 1
    g_row = [jnp.sum(g[h] * eye, axis=0, keepdims=True) for h in heads]
    gc = [jnp.sum(jnp.where(incl, g_row[h], 0.0), axis=1, keepdims=True) for h in heads]
    gc_row = [jnp.sum(gc[h] * eye, axis=0, keepdims=True) for h in heads]
    decay = [jnp.where(incl, jnp.exp(jnp.where(incl, gc[h] - gc_row[h], 0.0)), 0.0) for h in heads]
    kb = [k[h] * beta[h] for h in heads]
    vb = [v[h] * beta[h] for h in heads]
    p = [-jnp.where(ri > ci, _dot_nt(kb[h], k[h], DN_PRECISION) * decay[h], 0.0) for h in heads]
    tinv = [eye + p[h] for h in heads]
    for _ in range(5):
        p = [_dot(p[h], p[h], DN_PRECISION) for h in heads]
        tinv = [tinv[h] + _dot(tinv[h], p[h], DN_PRECISION) for h in heads]
    eg = [jnp.exp(gc[h]) for h in heads]
    u = [_dot(tinv[h], vb[h], DN_PRECISION) for h in heads]
    wk = [_dot(tinv[h], kb[h] * eg[h], DN_PRECISION) for h in heads]
    attn = [_dot_nt(q[h], k[h]) * decay[h] for h in heads]
    g_last = [jnp.sum(jnp.where(last, gc[h], 0.0), axis=0, keepdims=True) for h in heads]
    k_tail = [k[h] * jnp.exp(g_last[h] - gc[h]) for h in heads]
    v_new = [u[h] - _dot(wk[h], state[h]) for h in heads]
    o = [_dot(q[h] * eg[h], state[h]) + _dot(attn[h], v_new[h]) for h in heads]
    new = [state[h] * jnp.exp(g_last[h]) + _dot_tn(k_tail[h], v_new[h]) for h in heads]
    return o, new


def _bg_cols(bg, h, heads):
    lane = lax.broadcasted_iota(jnp.int32, bg.shape, 1)
    beta = jnp.sum(jnp.where(lane == h, bg, 0.0), axis=1, keepdims=True)
    g = jnp.sum(jnp.where(lane == heads + h, bg, 0.0), axis=1, keepdims=True)
    return beta, g


def _dn_fwd(qkv, bg):
    rows = qkv.shape[0]
    heads = qkv.shape[1] // (3 * HEAD_A)
    n = rows // CHUNK
    hp, groups = heads, 1

    def body(q_ref, k_ref, v_ref, bg_ref, o_ref, hist_ref, s_ref):
        c, grp = pl.program_id(0), pl.program_id(1)

        @pl.when(c == 0)
        def _():
            for i in range(hp):
                s_ref[grp * hp + i] = jnp.zeros((HEAD_A, HEAD_A), F32)

        bg_v = bg_ref[...]
        cols = [slice(i * HEAD_A, (i + 1) * HEAD_A) for i in range(hp)]
        state = [s_ref[grp * hp + i] for i in range(hp)]
        beta_g = [_bg_cols(bg_v, grp * hp + i, heads) for i in range(hp)]
        o, new = _dn_chunk([q_ref[:, c_] for c_ in cols], [k_ref[:, c_] for c_ in cols], [v_ref[:, c_] for c_ in cols],
                           [b for b, _ in beta_g], [g for _, g in beta_g], state)
        for i in range(hp):
            hist_ref[0, i] = state[i]
            o_ref[:, cols[i]] = o[i]
            s_ref[grp * hp + i] = new[i]

    def part(p):
        return pl.BlockSpec((CHUNK, hp * HEAD_A), lambda c, grp: (c, p * groups + grp))

    return pl.pallas_call(
        body, name="deltanet_fwd", grid=(n, groups),
        in_specs=[part(0), part(1), part(2), pl.BlockSpec((CHUNK, 128), lambda c, grp: (c, 0))],
        out_specs=[part(0), pl.BlockSpec((1, hp, HEAD_A, HEAD_A), lambda c, grp: (c, grp, 0, 0))],
        out_shape=[jax.ShapeDtypeStruct((rows, heads * HEAD_A), F32),
                   jax.ShapeDtypeStruct((n, heads, HEAD_A, HEAD_A), F32)],
        scratch_shapes=[pltpu.VMEM((heads, HEAD_A, HEAD_A), F32)],
        compiler_params=_cp("arbitrary", "arbitrary"),
    )(qkv, qkv, qkv, bg)


def _dn_bwd(qkv, bg, hist, do):
    rows = qkv.shape[0]
    heads = qkv.shape[1] // (3 * HEAD_A)
    n = rows // CHUNK
    hp, groups = heads, 1

    def body(q_ref, k_ref, v_ref, bg_ref, hist_ref, do_ref, dqkv_ref, dbg_ref, ds_ref):
        c, grp = pl.program_id(0), pl.program_id(1)

        @pl.when(c == 0)
        def _():
            for i in range(hp):
                ds_ref[grp * hp + i] = jnp.zeros((HEAD_A, HEAD_A), F32)

        bg_v = bg_ref[...]
        lane = lax.broadcasted_iota(jnp.int32, (CHUNK, 128), 1)
        cols = [slice(i * HEAD_A, (i + 1) * HEAD_A) for i in range(hp)]
        beta_g = [_bg_cols(bg_v, grp * hp + i, heads) for i in range(hp)]
        _, vjp = jax.vjp(_dn_chunk, [q_ref[:, c_] for c_ in cols], [k_ref[:, c_] for c_ in cols],
                         [v_ref[:, c_] for c_ in cols], [b for b, _ in beta_g], [g for _, g in beta_g],
                         [hist_ref[0, i] for i in range(hp)])
        dq, dk, dv, dbeta, dg, ds = vjp(([do_ref[:, c_] for c_ in cols], [ds_ref[grp * hp + i] for i in range(hp)]))
        dbg = jnp.zeros((CHUNK, 128), F32)
        for i in range(hp):
            h = grp * hp + i
            for p, part_grad in enumerate((dq, dk, dv)):
                dqkv_ref[:, pl.ds((p * heads + i) * HEAD_A, HEAD_A)] = part_grad[i]
            ds_ref[h] = ds[i]
            dbg = dbg + jnp.where(lane == h, dbeta[i], 0.0) + jnp.where(lane == heads + h, dg[i], 0.0)

        @pl.when(grp == 0)
        def _():
            dbg_ref[...] = jnp.zeros_like(dbg_ref)

        dbg_ref[...] += dbg

    def part(p):
        return pl.BlockSpec((CHUNK, hp * HEAD_A), lambda c, grp: (n - 1 - c, p * groups + grp))

    return pl.pallas_call(
        body, name="deltanet_bwd", grid=(n, groups),
        in_specs=[part(0), part(1), part(2), pl.BlockSpec((CHUNK, 128), lambda c, grp: (n - 1 - c, 0)),
                  pl.BlockSpec((1, hp, HEAD_A, HEAD_A), lambda c, grp: (n - 1 - c, grp, 0, 0)), part(0)],
        out_specs=[pl.BlockSpec((CHUNK, 3 * heads * HEAD_A), lambda c, grp: (n - 1 - c, 0)),
                   pl.BlockSpec((CHUNK, 128), lambda c, grp: (n - 1 - c, 0))],
        out_shape=[jax.ShapeDtypeStruct(qkv.shape, F32), jax.ShapeDtypeStruct((rows, 128), F32)],
        scratch_shapes=[pltpu.VMEM((heads, HEAD_A, HEAD_A), F32)],
        compiler_params=_cp("arbitrary", "arbitrary"),
    )(qkv, qkv, qkv, bg, hist, do)


def _head_mask(heads, width):
    return (lax.broadcasted_iota(jnp.int32, (heads, width), 0)
            == lax.broadcasted_iota(jnp.int32, (heads, width), 1) // HEAD_B)


def _masked_rows(mask, row):
    return jnp.where(mask, row, 0.0).astype(BF16)


def _rwkv_fwd(r, w, k, v, a, b):
    rows, width = r.shape
    heads = width // HEAD_B
    ts = SCAN_STEPS

    def body(r_ref, w_ref, k_ref, v_ref, a_ref, b_ref, y_ref, hist_ref, s_ref):
        @pl.when(pl.program_id(0) == 0)
        def _():
            s_ref[...] = jnp.zeros_like(s_ref)

        mask = _head_mask(heads, width)
        onehot = mask.astype(BF16)
        onehot2 = jnp.concatenate([onehot, onehot], axis=0)
        bd = _block_diag_ones()

        spread_v = [_dot_tn(jnp.concatenate(_hi_lo(v_ref[j]), axis=0), onehot2) for j in range(ts)]
        s = s_ref[...]
        ys = []
        for j in range(ts):
            row = pl.ds(j, 1)
            hist_ref[j] = s
            (sa,) = _segsum_many([((s * a_ref[row, :]).astype(BF16),)], bd)
            s = s * w_ref[row, :] + sa * b_ref[row, :] + spread_v[j] * k_ref[row, :]
            ys.append(_dot_nt(_masked_rows(mask, r_ref[row, :]), s.astype(BF16)))
        for j in range(ts):
            y_ref[j] = ys[j]
        s_ref[...] = s

    blk = pl.BlockSpec((ts, width), lambda i: (i, 0))
    blk3 = pl.BlockSpec((ts, heads, HEAD_B), lambda i: (i, 0, 0))
    return pl.pallas_call(
        body, name="rwkv_fwd", grid=(rows // ts,),
        in_specs=[blk, blk, blk, blk3, blk, blk],
        out_specs=[blk3, pl.BlockSpec((ts, HEAD_B, width), lambda i: (i, 0, 0)),
                   pl.BlockSpec((HEAD_B, width), lambda i: (0, 0))],
        out_shape=[jax.ShapeDtypeStruct((rows, heads, HEAD_B), F32), jax.ShapeDtypeStruct((rows, HEAD_B, width), F32),
                   jax.ShapeDtypeStruct((HEAD_B, width), F32)],
        compiler_params=_cp("arbitrary"),
    )(r, w, k, v, a, b)


def _rwkv_bwd(r, w, k, v, a, b, hist, last, dy):
    rows, width = r.shape
    heads = width // HEAD_B
    ts = SCAN_STEPS
    nb = rows // ts

    def body(r_ref, w_ref, k_ref, v_ref, a_ref, b_ref, hist_ref, last_ref, dy_ref,
             dr_ref, dw_ref, dk_ref, dv_ref, da_ref, db_ref, g_ref, after_ref):
        @pl.when(pl.program_id(0) == 0)
        def _():
            g_ref[...] = jnp.zeros_like(g_ref)
            after_ref[...] = last_ref[...]

        mask = _head_mask(heads, width)
        onehot = mask.astype(BF16)
        bd = _block_diag_ones()

        def own_lanes(x):
            return jnp.sum(jnp.where(mask, x, 0.0), axis=0, keepdims=True)

        def colsum(x):
            return jnp.sum(x, axis=0, keepdims=True)

        dy_m = [dy_ref[j].astype(BF16) for j in range(ts)]
        spread_dy = [_dot_tn(dy_m[j], onehot) for j in range(ts)]
        state_after = [hist_ref[j + 1] if j < ts - 1 else after_ref[...] for j in range(ts)]
        dr = [own_lanes(_dot(dy_m[j], state_after[j].astype(BF16))) for j in range(ts)]
        sa_m = [_dot_nt(_masked_rows(mask, a_ref[pl.ds(j, 1), :]), hist_ref[j].astype(BF16)) for j in range(ts)]
        g = g_ref[...]
        dw, dk, db, da, dv = {}, {}, {}, {}, {}
        for j in reversed(range(ts)):
            row = pl.ds(j, 1)
            sp = hist_ref[j]
            g = g + spread_dy[j] * r_ref[row, :]
            (dsa,) = _segsum_many([((g * b_ref[row, :]).astype(BF16),)], bd)
            g_b = g.astype(BF16)
            both = _dot(jnp.concatenate([v_ref[j].astype(BF16), sa_m[j].astype(BF16)], axis=0), g_b)
            dk[j], db[j] = own_lanes(both[:heads]), own_lanes(both[heads:])
            dv[j] = _dot_nt(_masked_rows(mask, k_ref[row, :]), g_b)
            dw[j] = colsum(g * sp)
            da[j] = colsum(sp * dsa)
            g = g * w_ref[row, :] + dsa * a_ref[row, :]
        g_ref[...] = g
        after_ref[...] = hist_ref[0]
        for j in range(ts):
            dv_ref[j] = dv[j]
            for ref, vals in ((dr_ref, dr), (dw_ref, dw), (dk_ref, dk), (da_ref, da), (db_ref, db)):
                ref[pl.ds(j, 1), :] = vals[j]

    blk = pl.BlockSpec((ts, width), lambda i: (nb - 1 - i, 0))
    blk3 = pl.BlockSpec((ts, heads, HEAD_B), lambda i: (nb - 1 - i, 0, 0))
    state = pl.BlockSpec((HEAD_B, width), lambda i: (0, 0))
    return pl.pallas_call(
        body, name="rwkv_bwd", grid=(nb,),
        in_specs=[blk, blk, blk, blk3, blk, blk, pl.BlockSpec((ts, HEAD_B, width), lambda i: (nb - 1 - i, 0, 0)),
                  state, blk3],
        out_specs=[blk, blk, blk, blk3, blk, blk],
        out_shape=[jax.ShapeDtypeStruct((rows, width), F32)] * 3 + [jax.ShapeDtypeStruct((rows, heads, HEAD_B), F32)]
        + [jax.ShapeDtypeStruct((rows, width), F32)] * 2,
        scratch_shapes=[pltpu.VMEM((HEAD_B, width), F32), pltpu.VMEM((HEAD_B, width), F32)],
        compiler_params=_cp("arbitrary"),
    )(r, w, k, v, a, b, hist, last, dy)


def _live(row0, shape):
    return (row0 + lax.broadcasted_iota(jnp.int32, shape, 0)) >= PAD


def _norm_fn(row0, h, gain):
    return (_rms(h, gain),)


def _norm_res_fn(row0, h, gain):
    return _rms(h, gain), h


def _make_bg_fn(heads):
    def fn(row0, x, log_rate, dt_bias):
        lane = lax.broadcasted_iota(jnp.int32, x.shape, 1)
        beta = _sigmoid(x)
        g = -jnp.exp(log_rate) * _softplus(x + dt_bias)
        out = jnp.where(lane < heads, beta, jnp.where(lane < 2 * heads, g, 0.0))
        return (jnp.where(_live(row0, x.shape), out, 0.0),)
    return fn


def _b_pre_fn(row0, zf, w0, w_up, a0, a_up, g_up, k_k, k_a):
    d = w0.shape[1]
    r, k, v = zf[:, :d], zf[:, d:2 * d], zf[:, 2 * d:3 * d]
    lo = zf[:, 3 * d:3 * d + 128]
    lg = zf[:, 3 * d + 128:3 * d + LORA_PAD]
    lane = lax.broadcasted_iota(jnp.int32, lo.shape, 1)
    lw = _dot(jnp.where(lane < LORA_W, jnp.tanh(lo), 0.0), w_up)
    la = _dot(jnp.where(lane >= LORA_W, lo, 0.0), a_up)
    lane_g = lax.broadcasted_iota(jnp.int32, lg.shape, 1)
    gate = _dot(jnp.where(lane_g < LORA_G, _sigmoid(lg), 0.0), g_up)
    decay = jnp.exp(-jnp.exp(-_softplus(-(w0 + lw)) - 0.5))
    a = _sigmoid(a0 + la)
    kx = k * k_k
    kk = kx * lax.rsqrt(_segsum64(kx * kx) + 1e-6)
    k2 = k * (1.0 + (a - 1.0) * k_a)
    return r, decay, k2, v, -kk, kk * a, gate


def _post_fn(row0, o, zg, y, r, k2, v, gate, out_gain, r_k, ln_g, ln_b):
    d = o.shape[1]
    az, ga, gb = zg[:, :d], zg[:, d:2 * d], zg[:, 2 * d:]
    heads = d // HEAD_A
    parts = []
    for h in range(heads):
        oh = o[:, h * HEAD_A:(h + 1) * HEAD_A]
        parts.append(oh * lax.rsqrt(jnp.mean(oh * oh, axis=-1, keepdims=True) + EPS) * out_gain)
    o_a = jnp.concatenate(parts, axis=1) * _silu(az)
    mean = _segsum64(y) * (1.0 / HEAD_B)
    yc = y - mean
    var = _segsum64(yc * yc) * (1.0 / HEAD_B)
    yn = yc * lax.rsqrt(var + GN_EPS) * ln_g + ln_b
    o_b = (yn + _segsum64(r * k2 * r_k) * v) * gate
    return (_sigmoid(ga) * o_a + _sigmoid(gb) * o_b,)


def _loss(h3, target, gain, tb):
    rows, d = h3.shape

    def body(h_ref, t_ref, g_ref, dh_ref, dg_ref, l_ref):
        i = pl.program_id(0)
        live = (i * tb + lax.broadcasted_iota(jnp.int32, (tb, 1), 0)) >= CHUNK
        tgt = t_ref[...]

        def f(h, g):
            err = _rms(h, g) - tgt
            return 0.5 * jnp.sum(jnp.where(live, jnp.mean(err * err, axis=-1, keepdims=True), 0.0))

        val, vjp = jax.vjp(f, h_ref[...], g_ref[...])
        dh, dg = vjp(jnp.ones((), F32))
        dh_ref[...] = dh

        @pl.when(i == 0)
        def _():
            dg_ref[...] = jnp.zeros_like(dg_ref)
            l_ref[...] = jnp.zeros_like(l_ref)

        dg_ref[...] += dg
        l_ref[...] += jnp.full((1, 128), val, F32)

    blk = pl.BlockSpec((tb, d), lambda i: (i, 0))
    return pl.pallas_call(
        body, name="loss", grid=(rows // tb,),
        in_specs=[blk, blk, pl.BlockSpec((1, d), lambda i: (0, 0))],
        out_specs=[blk, pl.BlockSpec((1, d), lambda i: (0, 0)), pl.BlockSpec((1, 128), lambda i: (0, 0))],
        out_shape=[jax.ShapeDtypeStruct((rows, d), F32), jax.ShapeDtypeStruct((1, d), F32),
                   jax.ShapeDtypeStruct((1, 128), F32)],
        compiler_params=_cp("arbitrary"),
    )(h3, target, gain)


def _adamw_math(w, g, m, v):
    m2 = ADAM_B1 * m + (1.0 - ADAM_B1) * g
    v2 = ADAM_B2 * v + (1.0 - ADAM_B2) * (g * g)
    m_hat = m2 / (1.0 - ADAM_B1 ** ADAM_STEP)
    v_hat = v2 / (1.0 - ADAM_B2 ** ADAM_STEP)
    return -ADAM_LR * (m_hat / (jnp.sqrt(v_hat) + ADAM_EPS) + ADAM_WD * w), m2, v2


def _adamw(name, own, landed, w, m, v, rb):
    rows, cols = w.shape

    def body(o_ref, s_ref, w_ref, m_ref, v_ref, g_ref, d_ref, m2_ref, v2_ref):
        g = o_ref[...].astype(F32)
        for peer in range(N_DEV - 1):
            g = g + s_ref[peer].astype(F32)
        g_ref[...] = g
        d_ref[...], m2_ref[...], v2_ref[...] = _adamw_math(w_ref[...], g, m_ref[...], v_ref[...])

    blk = pl.BlockSpec((rb, cols), lambda i: (i, 0))
    return pl.pallas_call(
        body, name=name, grid=(rows // rb,),
        in_specs=[blk, pl.BlockSpec((N_DEV - 1, rb, cols), lambda i: (0, i, 0)), blk, blk, blk],
        out_specs=[blk] * 4, out_shape=[jax.ShapeDtypeStruct((rows, cols), F32)] * 4,
        compiler_params=_cp("parallel"),
    )(own, landed, w, m, v)


def _sum_slabs(name, slabs, rb):
    _, rows, cols = slabs.shape

    def body(s_ref, o_ref):
        g = s_ref[0]
        for dev in range(1, N_DEV):
            g = g + s_ref[dev]
        o_ref[...] = g

    return pl.pallas_call(
        body, name=name, grid=(rows // rb,),
        in_specs=[pl.BlockSpec((N_DEV, rb, cols), lambda i: (0, i, 0))],
        out_specs=pl.BlockSpec((rb, cols), lambda i: (i, 0)),
        out_shape=jax.ShapeDtypeStruct((rows, cols), F32), compiler_params=_cp("parallel"),
    )(slabs)


def _adamw_small(w, g, m, v):
    def body(w_ref, g_ref, m_ref, v_ref, d_ref, m2_ref, v2_ref):
        d_ref[...], m2_ref[...], v2_ref[...] = _adamw_math(w_ref[...], g_ref[...], m_ref[...], v_ref[...])

    return pl.pallas_call(body, name="adamw_small", out_shape=[jax.ShapeDtypeStruct(w.shape, F32)] * 3)(w, g, m, v)


def _place():
    return lax.axis_index("x"), lax.axis_index("y"), lax.axis_index("c")


def _index(p):
    return 4 * p[0] + 2 * p[1] + p[2]


def _all_gather(name, xs):
    n = len(xs)

    def body(*refs):
        x_refs, o_refs = refs[:n], refs[n:2 * n]
        send_sems, recv_sems, local_sems = refs[2 * n:]
        x, y, c = _place()
        me, sibling = (x, y, c), (x, y, 1 - c)
        chips = [(1 - x, y), (x, 1 - y), (1 - x, 1 - y)]

        def copy(i, k, block, to, src=None):
            dst = o_refs[i].at[_index(block)]
            return pltpu.make_async_remote_copy(src_ref=dst if src is None else src, dst_ref=dst,
                                                send_sem=send_sems.at[i, k], recv_sem=recv_sems.at[i, k],
                                                device_id=to, device_id_type=MESH_ID)

        mine = [pltpu.make_async_copy(x_refs[i], o_refs[i].at[_index(me)], local_sems.at[i]) for i in range(n)]
        for cp in mine:
            cp.start()
        first = []
        for i in range(n):
            first.append(copy(i, 0, me, sibling, src=x_refs[i]))
            first += [copy(i, 1 + j, me, (*chip, c), src=x_refs[i]) for j, chip in enumerate(chips)]
        for cp in first:
            cp.start()
        passed = []
        for j, chip in enumerate(chips):
            for i in range(n):
                copy(i, 1 + j, (*chip, c), me).wait_recv()
                cp = copy(i, 4 + j, (*chip, c), sibling)
                cp.start()
                passed.append(cp)
        for i in range(n):
            copy(i, 0, sibling, me).wait_recv()
            for j, chip in enumerate(chips):
                copy(i, 4 + j, (*chip, 1 - c), me).wait_recv()
        for cp in first + passed:
            cp.wait_send()
        for cp in mine:
            cp.wait()

    return pl.pallas_call(
        body, name=name, in_specs=[ANY] * n, out_specs=[ANY] * n,
        out_shape=[jax.ShapeDtypeStruct((N_DEV,) + x.shape, x.dtype) for x in xs],
        scratch_shapes=[pltpu.SemaphoreType.DMA((n, 7)), pltpu.SemaphoreType.DMA((n, 7)), pltpu.SemaphoreType.DMA((n,))],
    )(*xs)


def _exchange_start(name, xs, after=None):
    n = len(xs)
    copies = n * (N_DEV - 1)
    extra = [] if after is None else [after]

    def body(*refs):
        x_refs, land_refs = refs[:n], refs[n:2 * n]
        sems = refs[2 * n + len(extra):2 * n + len(extra) + 2 * copies]
        token = refs[-1]
        for i, k, peer in _exchange_copies(n):
            _exchange_copy(x_refs, land_refs, sems, i, k, peer).start()
        token[...] = jnp.zeros_like(token)

    lands = [lax.empty((N_DEV - 1,) + x.shape[1:], x.dtype) for x in xs]
    out = pl.pallas_call(
        body, name=name,
        out_shape=(*[pltpu.SemaphoreType.DMA(())] * (2 * copies), *[pltpu.HBM(x.shape, x.dtype) for x in xs],
                   *[pltpu.HBM(l.shape, l.dtype) for l in lands], jax.ShapeDtypeStruct((8, 128), F32)),
        in_specs=[HBM_SPEC] * (2 * n) + [ANY] * len(extra),
        out_specs=(*[SEM_SPEC] * (2 * copies), *[HBM_SPEC] * (2 * n), pl.BlockSpec(memory_space=pltpu.VMEM)),
        input_output_aliases={i: 2 * copies + i for i in range(2 * n)},
        compiler_params=pltpu.CompilerParams(has_side_effects=pltpu.SideEffectType.DATAFLOW_SIDE_EFFECTING),
    )(*[pltpu.with_memory_space_constraint(a, pltpu.HBM) for a in list(xs) + lands], *extra)
    sems, rest = list(out[:2 * copies]), out[2 * copies:]
    return sems, list(rest[:n]), list(rest[n:2 * n]), rest[-1]


def _exchange_copies(n):
    x, y, c = _place()
    for k in range(1, N_DEV):
        peer = ((1 - x) if k & 4 else x, (1 - y) if k & 2 else y, (1 - c) if k & 1 else c)
        for i in range(n):
            yield i, k - 1, peer


def _exchange_copy(x_refs, land_refs, sems, i, k, peer):
    copies = len(sems) // 2
    which = i * (N_DEV - 1) + k
    return pltpu.make_async_remote_copy(src_ref=x_refs[i].at[_index(peer)], dst_ref=land_refs[i].at[k],
                                        send_sem=sems[which], recv_sem=sems[copies + which],
                                        device_id=peer, device_id_type=MESH_ID)


def _exchange_wait(name, sems, xs, lands, after):
    n = len(xs)

    def body(*refs):
        x_refs, land_refs = refs[:n], refs[n:2 * n]
        sem_refs = refs[2 * n:2 * n + len(sems)]
        for i, k, peer in _exchange_copies(n):
            copy = _exchange_copy(x_refs, land_refs, sem_refs, i, k, peer)
            copy.wait_send()
            copy.wait_recv()

    out = pl.pallas_call(
        body, name=name,
        out_shape=(*[pltpu.HBM(x.shape, x.dtype) for x in xs], *[pltpu.HBM(l.shape, l.dtype) for l in lands]),
        in_specs=[HBM_SPEC] * (2 * n) + [SEM_SPEC] * len(sems) + [ANY], out_specs=tuple([HBM_SPEC] * (2 * n)),
        input_output_aliases={i: i for i in range(2 * n)},
        compiler_params=pltpu.CompilerParams(has_side_effects=pltpu.SideEffectType.DATAFLOW_SIDE_EFFECTING),
    )(*xs, *lands, *sems, after)
    return list(out[:n]), list(out[n:])


def _pack(arrays):
    flat = jnp.concatenate([a.reshape(-1) for a in arrays])
    pad = (-flat.shape[0]) % 1024
    return jnp.pad(flat, (0, pad)).reshape(-1, 128)


def _unpack(packed, shapes):
    flat = packed.reshape(-1)
    out, pos = [], 0
    for s in shapes:
        size = 1
        for dim in s:
            size *= dim
        out.append(flat[pos:pos + size].reshape(s))
        pos += size
    return out


def _cols_from_slabs(stack):
    return jnp.transpose(stack, (1, 0, 2)).reshape(stack.shape[1], -1)


def _cols_to_slabs(full):
    return jnp.transpose(full.reshape(full.shape[0], N_DEV, -1), (1, 0, 2))


def kernel(x, meta_tokens, ffn1_norm, ffn1_w_gu, ffn1_w_down, mix_norm, w_in, a_conv_w, a_log_rate, a_dt_bias, a_out_norm, b_shift_mu, b_w0, b_w_up, b_a0, b_a_up, b_g_up, b_k_k, b_k_a, b_r_k, b_ln_gain, b_ln_bias, w_out, ffn2_norm, ffn2_w_gu, ffn2_w_down, final_norm, loss_target, m_meta_tokens, m_ffn1_norm, m_ffn1_w_gu, m_ffn1_w_down, m_mix_norm, m_w_in, m_a_conv_w, m_a_log_rate, m_a_dt_bias, m_a_out_norm, m_b_shift_mu, m_b_w0, m_b_w_up, m_b_a0, m_b_a_up, m_b_g_up, m_b_k_k, m_b_k_a, m_b_r_k, m_b_ln_gain, m_b_ln_bias, m_w_out, m_ffn2_norm, m_ffn2_w_gu, m_ffn2_w_down, m_final_norm, v_meta_tokens, v_ffn1_norm, v_ffn1_w_gu, v_ffn1_w_down, v_mix_norm, v_w_in, v_a_conv_w, v_a_log_rate, v_a_dt_bias, v_a_out_norm, v_b_shift_mu, v_b_w0, v_b_w_up, v_b_a0, v_b_a_up, v_b_g_up, v_b_k_k, v_b_k_a, v_b_r_k, v_b_ln_gain, v_b_ln_bias, v_w_out, v_ffn2_norm, v_ffn2_w_gu, v_ffn2_w_down, v_final_norm):
    names = ['meta_tokens', 'ffn1_norm', 'ffn1_w_gu', 'ffn1_w_down', 'mix_norm', 'w_in', 'a_conv_w', 'a_log_rate',
             'a_dt_bias', 'a_out_norm', 'b_shift_mu', 'b_w0', 'b_w_up', 'b_a0', 'b_a_up', 'b_g_up', 'b_k_k', 'b_k_a',
             'b_r_k', 'b_ln_gain', 'b_ln_bias', 'w_out', 'ffn2_norm', 'ffn2_w_gu', 'ffn2_w_down', 'final_norm']
    env = dict(locals())
    wts = {k: env[k] for k in names}
    mom_m = {k: env['m_' + k] for k in names}
    mom_v = {k: env['v_' + k] for k in names}
    big = ['ffn1_w_gu', 'ffn1_w_down', 'w_in', 'w_out', 'ffn2_w_gu', 'ffn2_w_down']
    col_sharded = {'ffn1_w_gu', 'w_in', 'ffn2_w_gu'}
    small_sharded = ['meta_tokens', 'a_conv_w', 'b_w_up', 'b_a_up', 'b_g_up']
    replicated = [k for k in names if k not in big and k not in small_sharded]

    seq, d = x.shape[1], x.shape[2]
    rows = PAD + N_META + seq
    heads_a = d // HEAD_A
    tb_mm = _tb(rows, 416)
    tb_vjp = _tb(rows, 208)
    tb_dw = _tb(rows, 2080)
    me = _index(_place())

    big_local = [wts[k][0].astype(BF16) for k in big]
    small_local = [wts['meta_tokens']] + [wts[k][0] for k in small_sharded[1:]]
    gathered = _all_gather("gather_weights", big_local + small_local)
    gw = dict(zip(big + small_sharded, gathered))
    full = {k: (gw[k] if k in ('ffn1_w_gu', 'ffn2_w_gu') else _cols_from_slabs(gw[k]) if k in col_sharded
                else gw[k].reshape(-1, gw[k].shape[-1])) for k in big}
    for k in small_sharded:
        full[k] = _cols_from_slabs(gw[k])
    for k in replicated:
        full[k] = wts[k].reshape(1, -1)

    win = full['w_in']
    n_b = 3 * d + LORA_W + LORA_A + LORA_G
    off_beta, off_b = 4 * d, 4 * d + 2 * heads_a
    off_ga = off_b + n_b
    b_width = 3 * d + LORA_PAD
    zcols = lambda r, c: jnp.zeros((r, c), BF16)
    w_qkv = win[:, :3 * d]
    w_zg = jnp.concatenate([win[:, 3 * d:4 * d], win[:, off_ga:off_ga + 2 * d]], axis=1)
    w_b = jnp.concatenate([win[:, off_b:off_b + n_b], zcols(d, b_width - n_b)], axis=1)
    w_bg = jnp.concatenate([win[:, off_beta:off_beta + 2 * heads_a], zcols(d, 128 - 2 * heads_a)], axis=1)

    def lanes(vec, start, width):
        return jnp.pad(vec.reshape(1, -1), ((0, 0), (start, width - start - vec.size)))

    log_rate = lanes(wts['a_log_rate'], heads_a, 128)
    dt_bias = lanes(wts['a_dt_bias'], heads_a, 128)
    mu = lanes(wts['b_shift_mu'], 0, b_width)
    w_up = jnp.pad(full['b_w_up'], ((0, 128 - LORA_W), (0, 0)))
    a_up = jnp.pad(full['b_a_up'], ((LORA_W, 0), (0, 0)))
    g_up = jnp.pad(full['b_g_up'], ((0, 256 - LORA_G), (0, 0)))
    b_pars = [full['b_w0'], w_up, full['b_a0'], a_up, g_up, full['b_k_k'], full['b_k_a']]
    post_pars = [full['a_out_norm'], full['b_r_k'], full['b_ln_gain'], full['b_ln_bias']]
    bg_fn = _make_bg_fn(heads_a)

    h0 = jnp.concatenate([jnp.zeros((PAD, d), F32), full['meta_tokens'], x[0]], axis=0)
    h1 = _ffn_fwd("ffn1_fwd", h0, full['ffn1_norm'], full['ffn1_w_gu'], full['ffn1_w_down'], tb_mm)
    (u,) = _tok_fwd("mix_norm_fwd", _norm_fn, [h1], [full['mix_norm']], [(d, BF16)], tb_mm)
    z_qkv = _mm("in_qkv", u, w_qkv, tb=tb_mm, tn=_col_tile(3 * d, 1536))
    z_zg = _mm("in_zg", u, w_zg, tb=tb_mm, tn=_col_tile(3 * d, 1536))
    z_b = _mm("in_b", u, w_b, tb=tb_mm, tn=_col_tile(b_width, 1536))
    z_bg = _mm("in_bg", u, w_bg, tb=tb_mm, tn=128)
    qkv = _a_pre_fwd(z_qkv, full['a_conv_w'])
    (bg,) = _tok_fwd("bg_fwd", bg_fn, [z_bg], [log_rate, dt_bias], [(128, F32)], tb_mm)
    o_dn, dn_hist = _dn_fwd(qkv, bg)
    zf = _shift_fwd(z_b, mu)
    rr, ww, kk2, vv, av, bv, gate = _tok_fwd("b_pre_fwd", _b_pre_fn, [zf], b_pars, [(d, F32)] * 7, tb_vjp)
    per_head = lambda t: t.reshape(rows, d // HEAD_B, HEAD_B)
    y_heads, b_hist, b_last = _rwkv_fwd(rr, ww, kk2, per_head(vv), av, bv)
    y_b = y_heads.reshape(rows, d)
    post_toks = [o_dn, z_zg, y_b, rr, kk2, vv, gate]
    (merged,) = _tok_fwd("post_fwd", _post_fn, post_toks, post_pars, [(d, BF16)], tb_vjp)
    h2 = _mm("out_proj", merged, full['w_out'], add=h1, tb=tb_mm, tn=d)
    h3 = _ffn_fwd("ffn2_fwd", h2, full['ffn2_norm'], full['ffn2_w_gu'], full['ffn2_w_down'], tb_mm)

    target = jnp.pad(loss_target[0], ((CHUNK, 0), (0, 0)))
    dh3, g_final, loss_part = _loss(h3, target, full['final_norm'].reshape(1, d), tb_vjp)

    def ffn_backward(tag, h, dout, key_norm, key_gu, key_down):
        dh, dh_bf, dgain, xn, act, dgate, dup, dhalf = _ffn_bwd(tag + "_bwd", h, full[key_norm], dout, full[key_gu],
                                                                full[key_down], tb_mm)
        d_gu = jnp.concatenate([_mm_tn_to_slabs(tag + "_dw_gate", xn, dgate, tk=tb_dw),
                                _mm_tn_to_slabs(tag + "_dw_up", xn, dup, tk=tb_dw)], axis=0)
        d_down = _mm_tn_from_slabs(tag + "_dw_down", act, dhalf, tk=tb_dw).reshape(N_DEV, -1, d)
        return dh, dh_bf, dgain, d_gu, d_down

    dh2, dh2_bf, g_ffn2_norm, g_ffn2_gu, g_ffn2_down = ffn_backward("ffn2", h2, dh3, 'ffn2_norm', 'ffn2_w_gu',
                                                                    'ffn2_w_down')
    g_w_out = _mm_tn("dw_out", merged, dh2_bf, tm=d, tn=d, tk=tb_dw).reshape(N_DEV, -1, d)

    def start_exchange(tag, keys, slabs, after=None):
        sems, kept, lands, token = _exchange_start("exchange_start_" + tag, slabs, after)
        return (tag, keys, sems, kept, lands), token

    ex_ffn2, token_ffn2 = start_exchange("ffn2", ['ffn2_w_gu', 'ffn2_w_down', 'w_out'], [g_ffn2_gu, g_ffn2_down, g_w_out])
    dmerged = _mm("d_merged", dh2_bf, full['w_out'], trans_b=True, after=token_ffn2, tb=tb_mm, tn=d)
    post_grads = _tok_bwd("post_bwd", _post_fn, post_toks, post_pars, [[dmerged]], list(range(7)), tb_vjp,
                          [F32, BF16] + [F32] * 5)
    do_dn, dz_zg, dy_b, dr1, dk1, dv1, dgate = post_grads[:7]
    g_out_norm, g_r_k, g_ln_g, g_ln_b = post_grads[7:]
    dr2, dw2, dk2, dv_heads, da2, db2 = _rwkv_bwd(rr, ww, kk2, per_head(vv), av, bv, b_hist, b_last, per_head(dy_b))
    dv2 = dv_heads.reshape(rows, d)
    b_grads = _tok_bwd("b_pre_bwd", _b_pre_fn, [zf], b_pars,
                       [[dr1, dr2], [dw2], [dk1, dk2], [dv1, dv2], [da2], [db2], [dgate]], [0], tb_vjp)
    dzf = b_grads[0]
    g_w0, g_w_up, g_a0, g_a_up, g_g_up, g_k_k, g_k_a = b_grads[1:]
    dz_b, g_mu = _shift_bwd(z_b, mu, dzf)
    dqkv, dbg = _dn_bwd(qkv, bg, dn_hist, do_dn)
    dz_qkv, g_conv = _a_pre_bwd(z_qkv, full['a_conv_w'], dqkv)
    dz_bg, g_log_rate, g_dt_bias = _tok_bwd("bg_bwd", bg_fn, [z_bg], [log_rate, dt_bias], [[dbg]], [0], tb_mm, [BF16])

    du = None
    g_w_in_parts = []
    for tag, dz, wpiece in (("qkv", dz_qkv, w_qkv), ("zg", dz_zg, w_zg), ("b", dz_b, w_b), ("bg", dz_bg, w_bg)):
        du = _mm("du_" + tag, dz, wpiece, trans_b=True, add=du, tb=tb_mm, tn=d)
        g_w_in_parts.append(_mm_tn("dw_in_" + tag, u, dz, tm=d, tn=_col_tile(dz.shape[1], 1536), tk=tb_dw))
    gp_qkv, gp_zg, gp_b, gp_bg = g_w_in_parts
    g_w_in = _cols_to_slabs(jnp.concatenate(
        [gp_qkv, gp_zg[:, :d], gp_bg[:, :2 * heads_a], gp_b[:, :n_b], gp_zg[:, d:]], axis=1))
    ex_w_in, token_w_in = start_exchange("w_in", ['w_in'], [g_w_in])
    dh1, g_mix_norm = _tok_bwd("mix_norm_bwd", _norm_res_fn, [h1], [full['mix_norm']], [[du], [dh2]], [0], tb_vjp,
                               after=token_w_in)
    dh0, _, g_ffn1_norm, g_ffn1_gu, g_ffn1_down = ffn_backward("ffn1", h0, dh1, 'ffn1_norm', 'ffn1_w_gu', 'ffn1_w_down')

    small_full = {
        'meta_tokens': dh0[PAD:CHUNK], 'ffn1_norm': g_ffn1_norm, 'mix_norm': g_mix_norm, 'a_conv_w': g_conv,
        'a_log_rate': g_log_rate[:, heads_a:2 * heads_a], 'a_dt_bias': g_dt_bias[:, heads_a:2 * heads_a],
        'a_out_norm': g_out_norm, 'b_shift_mu': g_mu[:, :n_b], 'b_w0': g_w0, 'b_w_up': g_w_up[:LORA_W],
        'b_a0': g_a0, 'b_a_up': g_a_up[LORA_W:], 'b_g_up': g_g_up[:LORA_G], 'b_k_k': g_k_k, 'b_k_a': g_k_a,
        'b_r_k': g_r_k, 'b_ln_gain': g_ln_g, 'b_ln_bias': g_ln_b, 'ffn2_norm': g_ffn2_norm, 'final_norm': g_final,
    }
    small_names = list(small_full)
    packed = _pack([small_full[k] for k in small_names] + [loss_part[:, :1]])
    (all_parts,) = _all_gather("gather_small_grads", [packed])
    ex_ffn1, _ = start_exchange("ffn1", ['ffn1_w_gu', 'ffn1_w_down'], [g_ffn1_gu, g_ffn1_down], after=all_parts)
    summed = _sum_slabs("sum_small_grads", all_parts, packed.shape[0])
    pieces = _unpack(summed, [small_full[k].shape for k in small_names] + [(1, 1)])
    small_grad = dict(zip(small_names, pieces[:-1]))
    loss = pieces[-1].reshape(())

    grads, deltas, new_m, new_v = {}, {}, {}, {}
    local_small = {}
    for k in small_names:
        g = small_grad[k]
        if k in small_sharded:
            width = wts[k].shape[-1]
            g = lax.dynamic_slice_in_dim(g, me * width, width, axis=1)
        local_small[k] = g.reshape(wts[k].shape)
    pk = lambda tree: _pack([tree[k] for k in small_names])
    dl_s, m_s, v_s = _adamw_small(pk(wts), pk(local_small), pk(mom_m), pk(mom_v))
    shapes = [wts[k].shape for k in small_names]
    for k, dl, m2, v2 in zip(small_names, _unpack(dl_s, shapes), _unpack(m_s, shapes), _unpack(v_s, shapes)):
        grads[k], deltas[k], new_m[k], new_v[k] = local_small[k], dl, m2, v2

    done = dl_s
    for tag, keys, sems, kept, lands in (ex_ffn2, ex_w_in, ex_ffn1):
        kept, lands = _exchange_wait("exchange_wait_" + tag, sems, kept, lands, done)
        for k, slabs, landed in zip(keys, kept, lands):
            shard = wts[k][0]
            rb = _tb(shard.shape[0], 128)
            own = lax.dynamic_index_in_dim(slabs, me, axis=0, keepdims=False)
            g, dl, m2, v2 = _adamw("adamw_" + k, own, landed, shard, mom_m[k][0], mom_v[k][0], rb)
            grads[k], deltas[k], new_m[k], new_v[k] = g[None], dl[None], m2[None], v2[None]
            done = dl

    grad_x = dh0[CHUNK:][None]
    return (loss, grad_x, *[grads[k] for k in names], *[deltas[k] for k in names],
            *[new_m[k] for k in names], *[new_v[k] for k in names])
```

```python
import functools

import jax
import jax.numpy as jnp
from jax import lax
from jax.experimental import pallas as pl
from jax.experimental.pallas import tpu as pltpu

F32 = jnp.float32
BF16 = jnp.bfloat16
N_DEV = 8
N_META = 16
CHUNK = 64
PAD = CHUNK - N_META
HEAD_A = 128
HEAD_B = 64
LORA_W, LORA_A, LORA_G = 64, 64, 160
LORA_PAD = 384
EPS = 1e-6
GN_EPS = HEAD_B * 1e-5
ADAM_LR, ADAM_B1, ADAM_B2, ADAM_EPS, ADAM_WD, ADAM_STEP = 0.001, 0.9, 0.999, 1e-08, 0.01, 10
SCAN_STEPS = 16
MXU_WIDTH = 256
VMEM_LIMIT = 56 * 1024 * 1024
DN_PRECISION = lax.Precision.HIGH
MESH_ID = pl.DeviceIdType.MESH
ANY = pl.BlockSpec(memory_space=pl.ANY)
HBM_SPEC = pl.BlockSpec(memory_space=pltpu.HBM)
SEM_SPEC = pl.BlockSpec(memory_space=pltpu.SEMAPHORE)


def _cp(*sem):
    return pltpu.CompilerParams(dimension_semantics=sem, vmem_limit_bytes=VMEM_LIMIT)


def _tb(t, target):
    best = 16
    for d in range(16, target + 1, 16):
        if t % d == 0:
            best = d
    return best


def _sigmoid(x):
    return 1.0 / (1.0 + jnp.exp(-x))


def _silu(x):
    return x * _sigmoid(x)


def _softplus(x):
    return jnp.maximum(x, 0.0) + jnp.log(1.0 + jnp.exp(-jnp.abs(x)))


def _dot_nt(a, b, precision=None):
    return lax.dot_general(a, b, (((1,), (1,)), ((), ())), preferred_element_type=F32, precision=precision)


def _dot_tn(a, b, precision=None):
    return lax.dot_general(a, b, (((0,), (0,)), ((), ())), preferred_element_type=F32, precision=precision)


def _dot(a, b, precision=None):
    return jnp.dot(a, b, preferred_element_type=F32, precision=precision)


def _block_diag_ones():
    i = lax.broadcasted_iota(jnp.int32, (MXU_WIDTH, MXU_WIDTH), 0) // HEAD_B
    j = lax.broadcasted_iota(jnp.int32, (MXU_WIDTH, MXU_WIDTH), 1) // HEAD_B
    return (i == j).astype(BF16)


def _hi_lo(x):
    hi = x.astype(BF16)
    return hi, (x - hi.astype(F32)).astype(BF16)


def _segsum_many(xs, bd):
    groups = [x if isinstance(x, tuple) else _hi_lo(x) for x in xs]
    rows = groups[0][0].shape[0]
    stacked = jnp.concatenate([p for grp in groups for p in grp], axis=0)
    out = jnp.concatenate([_dot(stacked[:, s:s + MXU_WIDTH], bd) for s in range(0, stacked.shape[1], MXU_WIDTH)], axis=1)
    res, pos = [], 0
    for grp in groups:
        acc = out[pos:pos + rows]
        for j in range(1, len(grp)):
            acc = acc + out[pos + j * rows:pos + (j + 1) * rows]
        res.append(acc)
        pos += len(grp) * rows
    return res


def _segsum_impl(x):
    return _segsum_many([x], _block_diag_ones())[0]


@jax.custom_vjp
def _segsum64(x):
    return _segsum_impl(x)


_segsum64.defvjp(lambda x: (_segsum_impl(x), None), lambda _, ct: (_segsum_impl(ct),))


def _tok(t):
    return t if isinstance(t, tuple) else (t, t.shape[1], 0)


def _tok_spec(tb, width, colblk):
    return pl.BlockSpec((tb, width), lambda i: (i, colblk))


def _par_spec(p):
    return pl.BlockSpec(p.shape, lambda i: (0, 0))


def _tok_fwd(name, fn, toks, pars, outs, tb):
    toks = [_tok(t) for t in toks]
    rows = toks[0][0].shape[0]
    n_in = len(toks) + len(pars)

    def body(*refs):
        row0 = pl.program_id(0) * tb
        res = fn(row0, *[r[...] for r in refs[:n_in]])
        for r, o in zip(refs[n_in:], res):
            r[...] = o.astype(r.dtype)

    return pl.pallas_call(
        body, name=name, grid=(rows // tb,),
        in_specs=[_tok_spec(tb, w, c) for _, w, c in toks] + [_par_spec(p) for p in pars],
        out_specs=[_tok_spec(tb, w, 0) for w, _ in outs],
        out_shape=[jax.ShapeDtypeStruct((rows, w), dt) for w, dt in outs],
        compiler_params=_cp("parallel"),
    )(*[a for a, _, _ in toks], *pars)


def _tok_bwd(name, fn, toks, pars, cts, want, tb, want_dtypes=None, after=None):
    toks = [_tok(t) for t in toks]
    want_dtypes = want_dtypes or [F32] * len(want)
    cts = [[_tok(c) for c in group] for group in cts]
    flat_cts = [c for group in cts for c in group]
    rows = toks[0][0].shape[0]
    n_tok, n_par, n_ct = len(toks), len(pars), len(flat_cts)
    extra = [] if after is None else [after]

    def body(*refs):
        i = pl.program_id(0)
        row0 = i * tb
        prim = [r[...].astype(F32) for r in refs[:n_tok + n_par]]
        ct_refs = list(refs[n_tok + n_par:n_tok + n_par + n_ct])
        out_refs = refs[n_tok + n_par + n_ct + len(extra):]
        res, vjp = jax.vjp(lambda *a: fn(row0, *a), *prim)
        ct = []
        for group, o in zip(cts, res):
            acc = None
            for _ in group:
                v = ct_refs.pop(0)[...].astype(F32)
                acc = v if acc is None else acc + v
            ct.append(acc.astype(o.dtype))
        grads = vjp(tuple(ct))
        for r, k in zip(out_refs[:len(want)], want):
            r[...] = grads[k].astype(r.dtype)

        @pl.when(i == 0)
        def _():
            for r in out_refs[len(want):]:
                r[...] = jnp.zeros_like(r)

        for r, g in zip(out_refs[len(want):], grads[n_tok:]):
            r[...] += g

    return pl.pallas_call(
        body, name=name, grid=(rows // tb,),
        in_specs=[_tok_spec(tb, w, c) for _, w, c in toks] + [_par_spec(p) for p in pars]
        + [_tok_spec(tb, w, c) for _, w, c in flat_cts] + [ANY] * len(extra),
        out_specs=[_tok_spec(tb, toks[k][1], 0) for k in want] + [_par_spec(p) for p in pars],
        out_shape=[jax.ShapeDtypeStruct((rows, toks[k][1]), dt) for k, dt in zip(want, want_dtypes)]
        + [jax.ShapeDtypeStruct(p.shape, F32) for p in pars],
        compiler_params=_cp("arbitrary"),
    )(*[a for a, _, _ in toks], *pars, *[a for a, _, _ in flat_cts], *extra)


def _mm(name, a, b, *, trans_b=False, add=None, after=None, tb, tn):
    rows, k = a.shape
    n = b.shape[0] if trans_b else b.shape[1]

    def body(*refs):
        a_ref, b_ref = refs[:2]
        o_ref = refs[-1]
        acc = _dot_nt(a_ref[...], b_ref[...]) if trans_b else _dot(a_ref[...], b_ref[...])
        if add is not None:
            acc = acc + refs[2][...]
        o_ref[...] = acc

    in_specs = [pl.BlockSpec((tb, k), lambda i, j: (i, 0)),
                pl.BlockSpec((tn, k), lambda i, j: (j, 0)) if trans_b else pl.BlockSpec((k, tn), lambda i, j: (0, j))]
    args = [a, b]
    if add is not None:
        in_specs.append(pl.BlockSpec((tb, tn), lambda i, j: (i, j)))
        args.append(add)
    if after is not None:
        in_specs.append(ANY)
        args.append(after)
    return pl.pallas_call(
        body, name=name, grid=(rows // tb, n // tn), in_specs=in_specs,
        out_specs=pl.BlockSpec((tb, tn), lambda i, j: (i, j)),
        out_shape=jax.ShapeDtypeStruct((rows, n), F32),
        compiler_params=_cp("parallel", "parallel"),
    )(*args)


def _mm_tn_call(name, grid, a, b, a_spec, b_spec, o_spec, acc_shape, out_shape):
    last = len(grid) - 1

    def body(a_ref, b_ref, o_ref, acc_ref):
        k = pl.program_id(last)

        @pl.when(k == 0)
        def _():
            acc_ref[...] = jnp.zeros_like(acc_ref)

        a_blk = a_ref[0] if len(a_ref.shape) == 3 else a_ref[...]
        b_blk = b_ref[0] if len(b_ref.shape) == 3 else b_ref[...]
        acc_ref[...] += _dot_tn(a_blk, b_blk)

        @pl.when(k == grid[last] - 1)
        def _():
            if len(o_ref.shape) == 3:
                o_ref[0] = acc_ref[...].astype(o_ref.dtype)
            else:
                o_ref[...] = acc_ref[...].astype(o_ref.dtype)

    return pl.pallas_call(
        body, name=name, grid=grid, in_specs=[a_spec, b_spec], out_specs=o_spec,
        out_shape=jax.ShapeDtypeStruct(out_shape, BF16), scratch_shapes=[pltpu.VMEM(acc_shape, F32)],
        compiler_params=_cp(*(["parallel"] * last + ["arbitrary"])),
    )(a, b)


def _mm_tn(name, a, b, *, tm, tn, tk):
    rows, m = a.shape
    n = b.shape[1]
    return _mm_tn_call(name, (m // tm, n // tn, rows // tk), a, b,
                       pl.BlockSpec((tk, tm), lambda i, j, k: (k, i)), pl.BlockSpec((tk, tn), lambda i, j, k: (k, j)),
                       pl.BlockSpec((tm, tn), lambda i, j, k: (i, j)), (tm, tn), (m, n))


def _mm_tn_to_slabs(name, a, b3, *, tk):
    rows, m = a.shape
    s, _, c = b3.shape
    return _mm_tn_call(name, (s, rows // tk), a, b3,
                       pl.BlockSpec((tk, m), lambda i, k: (k, 0)), pl.BlockSpec((1, tk, c), lambda i, k: (i, k, 0)),
                       pl.BlockSpec((1, m, c), lambda i, k: (i, 0, 0)), (m, c), (s, m, c))


def _mm_tn_from_slabs(name, a3, b, *, tk):
    s, rows, c = a3.shape
    n = b.shape[1]
    return _mm_tn_call(name, (s, rows // tk), a3, b,
                       pl.BlockSpec((1, tk, c), lambda i, k: (i, k, 0)), pl.BlockSpec((tk, n), lambda i, k: (k, 0)),
                       pl.BlockSpec((c, n), lambda i, k: (i, 0)), (c, n), (s * c, n))


def _col_tile(n, target):
    if n <= target:
        return n
    best = 128
    for d in range(128, target + 1, 128):
        if n % d == 0:
            best = d
    return best


def _rms(x, gain):
    return x * lax.rsqrt(jnp.mean(x * x, axis=-1, keepdims=True) + EPS) * gain


def _ffn_specs(d, fc, nj):
    return [pl.BlockSpec((1, fc, d), lambda i, j: (j, 0, 0)), pl.BlockSpec((1, fc, d), lambda i, j: (nj + j, 0, 0)),
            pl.BlockSpec((fc, d), lambda i, j: (j, 0))]


def _ffn_fwd(name, h, gain, wgu, wd, tb):
    rows, d = h.shape
    nj = wgu.shape[0] // 2
    fc = wgu.shape[1]

    def body(h_ref, g_ref, wg_ref, wu_ref, wd_ref, o_ref, xn_s, acc_s):
        j = pl.program_id(1)

        @pl.when(j == 0)
        def _():
            xn_s[...] = _rms(h_ref[...], g_ref[...]).astype(BF16)
            acc_s[...] = jnp.zeros_like(acc_s)

        wg, wu, wdn = wg_ref[0], wu_ref[0], wd_ref[...]
        for half in range(2):
            rs = pl.ds(half * (tb // 2), tb // 2)
            xn = xn_s[rs, :]
            gate = _dot_nt(xn, wg)
            up = _dot_nt(xn, wu)
            acc_s[rs, :] += _dot((_silu(gate) * up).astype(BF16), wdn)

        @pl.when(j == nj - 1)
        def _():
            o_ref[...] = h_ref[...] + 0.5 * acc_s[...]

    return pl.pallas_call(
        body, name=name, grid=(rows // tb, nj),
        in_specs=[pl.BlockSpec((tb, d), lambda i, j: (i, 0)), pl.BlockSpec((1, d), lambda i, j: (0, 0))]
        + _ffn_specs(d, fc, nj),
        out_specs=pl.BlockSpec((tb, d), lambda i, j: (i, 0)),
        out_shape=jax.ShapeDtypeStruct((rows, d), F32),
        scratch_shapes=[pltpu.VMEM((tb, d), BF16), pltpu.VMEM((tb, d), F32)],
        compiler_params=_cp("parallel", "arbitrary"),
    )(h, gain, wgu, wgu, wd)


def _ffn_bwd(name, h, gain, dout, wgu, wd, tb):
    rows, d = h.shape
    nj = wgu.shape[0] // 2
    fc = wgu.shape[1]

    def body(h_ref, g_ref, do_ref, wg_ref, wu_ref, wd_ref,
             dh_ref, dhb_ref, dg_ref, xn_ref, act_ref, dgate_ref, dup_ref, dhalf_ref, dxn_s):
        i, j = pl.program_id(0), pl.program_id(1)

        @pl.when(j == 0)
        def _():
            xn_ref[...] = _rms(h_ref[...], g_ref[...]).astype(BF16)
            dhalf_ref[...] = (0.5 * do_ref[...]).astype(BF16)
            dxn_s[...] = jnp.zeros_like(dxn_s)

        wg, wu, wdn = wg_ref[0], wu_ref[0], wd_ref[...]
        for half in range(2):
            rs = pl.ds(half * (tb // 2), tb // 2)
            xn = xn_ref[rs, :]
            gate = _dot_nt(xn, wg)
            up = _dot_nt(xn, wu)
            sg = _sigmoid(gate)
            dact = _dot_nt(dhalf_ref[rs, :], wdn)
            act_ref[0, rs, :] = (gate * sg * up).astype(BF16)
            dgate = (dact * up * (sg * (1.0 + gate * (1.0 - sg)))).astype(BF16)
            dup = (dact * gate * sg).astype(BF16)
            dgate_ref[0, rs, :] = dgate
            dup_ref[0, rs, :] = dup
            dxn_s[rs, :] += _dot(dgate, wg) + _dot(dup, wu)

        @pl.when((i == 0) & (j == 0))
        def _():
            dg_ref[...] = jnp.zeros_like(dg_ref)

        @pl.when(j == nj - 1)
        def _():
            x = h_ref[...]
            r = lax.rsqrt(jnp.mean(x * x, axis=-1, keepdims=True) + EPS)
            dxn = dxn_s[...]
            dyg = dxn * g_ref[...]
            dh = do_ref[...] + r * dyg - x * (r * r * r) * jnp.mean(dyg * x, axis=-1, keepdims=True)
            dh_ref[...] = dh
            dhb_ref[...] = dh.astype(BF16)
            dg_ref[...] += jnp.sum(dxn * x * r, axis=0, keepdims=True)

    row_d = pl.BlockSpec((tb, d), lambda i, j: (i, 0))
    slab = pl.BlockSpec((1, tb, fc), lambda i, j: (j, i, 0))
    hidden = jax.ShapeDtypeStruct((nj, rows, fc), BF16)
    return pl.pallas_call(
        body, name=name, grid=(rows // tb, nj),
        in_specs=[row_d, pl.BlockSpec((1, d), lambda i, j: (0, 0)), row_d] + _ffn_specs(d, fc, nj),
        out_specs=[row_d, row_d, pl.BlockSpec((1, d), lambda i, j: (0, 0)), row_d, slab, slab, slab, row_d],
        out_shape=[jax.ShapeDtypeStruct((rows, d), F32), jax.ShapeDtypeStruct((rows, d), BF16),
                   jax.ShapeDtypeStruct((1, d), F32), jax.ShapeDtypeStruct((rows, d), BF16),
                   hidden, hidden, hidden, jax.ShapeDtypeStruct((rows, d), BF16)],
        scratch_shapes=[pltpu.VMEM((tb, d), F32)],
        compiler_params=_cp("arbitrary", "arbitrary"),
    )(h, gain, dout, wgu, wgu, wd)


def _shift_rows(x, s):
    return pltpu.roll(x, s % x.shape[0], 0)


def _a_post(c, which):
    s = _silu(c)
    n = s * lax.rsqrt(jnp.sum(s * s, axis=-1, keepdims=True) + 1e-6)
    scale = jnp.where(which == 0, HEAD_A ** -0.5, 1.0)
    return jnp.where(which == 2, s, n * scale)


def _conv(x, w):
    return x * w[3:4] + _shift_rows(x, 1) * w[2:3] + _shift_rows(x, 2) * w[1:2] + _shift_rows(x, 3) * w[0:1]


def _a_pre_fwd(zqkv, conv_w):
    rows, width = zqkv.shape
    heads = width // (3 * HEAD_A)

    def body(x_ref, w_ref, o_ref):
        which = pl.program_id(0) // heads
        live = lax.broadcasted_iota(jnp.int32, (rows, HEAD_A), 0) >= PAD
        o_ref[...] = jnp.where(live, _a_post(_conv(x_ref[...], w_ref[...]), which), 0.0)

    return pl.pallas_call(
        body, name="a_pre_fwd", grid=(width // HEAD_A,),
        in_specs=[pl.BlockSpec((rows, HEAD_A), lambda c: (0, c)), pl.BlockSpec((4, HEAD_A), lambda c: (0, c))],
        out_specs=pl.BlockSpec((rows, HEAD_A), lambda c: (0, c)),
        out_shape=jax.ShapeDtypeStruct((rows, width), F32),
        compiler_params=_cp("parallel"),
    )(zqkv, conv_w)


def _a_pre_bwd(zqkv, conv_w, dqkv):
    rows, width = zqkv.shape
    heads = width // (3 * HEAD_A)

    def body(x_ref, w_ref, ct_ref, dx_ref, dw_ref):
        which = pl.program_id(0) // heads
        live = lax.broadcasted_iota(jnp.int32, (rows, HEAD_A), 0) >= PAD
        x, w = x_ref[...], w_ref[...]
        _, vjp = jax.vjp(lambda c: _a_post(c, which), _conv(x, w))
        (dc,) = vjp(jnp.where(live, ct_ref[...], 0.0))
        dc = jnp.where(live, dc, 0.0)
        dx_ref[...] = (dc * w[3:4] + _shift_rows(dc, -1) * w[2:3] + _shift_rows(dc, -2) * w[1:2]
                       + _shift_rows(dc, -3) * w[0:1]).astype(BF16)
        dw_ref[...] = jnp.concatenate(
            [jnp.sum(dc * (_shift_rows(x, 3 - j) if j < 3 else x), axis=0, keepdims=True) for j in range(4)], axis=0)

    col = pl.BlockSpec((rows, HEAD_A), lambda c: (0, c))
    wsp = pl.BlockSpec((4, HEAD_A), lambda c: (0, c))
    return pl.pallas_call(
        body, name="a_pre_bwd", grid=(width // HEAD_A,),
        in_specs=[col, wsp, col], out_specs=[col, wsp],
        out_shape=[jax.ShapeDtypeStruct((rows, width), BF16), jax.ShapeDtypeStruct((4, width), F32)],
        compiler_params=_cp("parallel"),
    )(zqkv, conv_w, dqkv)


SHIFT_TILE = 384


def _shift_fwd(zb, mu):
    rows, width = zb.shape

    def body(x_ref, mu_ref, o_ref):
        x = x_ref[...]
        first = lax.broadcasted_iota(jnp.int32, x.shape, 0) == 0
        prev = jnp.where(first, 0.0, _shift_rows(x, 1))
        o_ref[...] = x + (prev - x) * mu_ref[...]

    col = pl.BlockSpec((rows, SHIFT_TILE), lambda c: (0, c))
    return pl.pallas_call(
        body, name="shift_fwd", grid=(width // SHIFT_TILE,),
        in_specs=[col, pl.BlockSpec((1, SHIFT_TILE), lambda c: (0, c))], out_specs=col,
        out_shape=jax.ShapeDtypeStruct((rows, width), F32), compiler_params=_cp("parallel"),
    )(zb, mu)


def _shift_bwd(zb, mu, dzf):
    rows, width = zb.shape

    def body(x_ref, mu_ref, ct_ref, dx_ref, dmu_ref):
        x, ct, mu_v = x_ref[...], ct_ref[...], mu_ref[...]
        row = lax.broadcasted_iota(jnp.int32, x.shape, 0)
        prev = jnp.where(row == 0, 0.0, _shift_rows(x, 1))
        nxt = jnp.where(row == rows - 1, 0.0, _shift_rows(ct, -1))
        dx_ref[...] = (ct * (1.0 - mu_v) + nxt * mu_v).astype(BF16)
        dmu_ref[...] = jnp.sum(ct * (prev - x), axis=0, keepdims=True)

    col = pl.BlockSpec((rows, SHIFT_TILE), lambda c: (0, c))
    msp = pl.BlockSpec((1, SHIFT_TILE), lambda c: (0, c))
    return pl.pallas_call(
        body, name="shift_bwd", grid=(width // SHIFT_TILE,),
        in_specs=[col, msp, col], out_specs=[col, msp],
        out_shape=[jax.ShapeDtypeStruct((rows, width), BF16), jax.ShapeDtypeStruct((1, width), F32)],
        compiler_params=_cp("parallel"),
    )(zb, mu, dzf)


def _dn_chunk(q, k, v, beta, g, state):
    heads = range(len(q))
    ri = lax.broadcasted_iota(jnp.int32, (CHUNK, CHUNK), 0)
    ci = lax.broadcasted_iota(jnp.int32, (CHUNK, CHUNK), 1)
    eye = (ri == ci).astype(F32)
    incl = ri >= ci
    last = lax.broadcasted_iota(jnp.int32, (CHUNK, 1), 0) == CHUNK - 1
    g_row = [jnp.sum(g[h] * eye, axis=0, keepdims=True) for h in heads]
    gc = [jnp.sum(jnp.where(incl, g_row[h], 0.0), axis=1, keepdims=True) for h in heads]
    gc_row = [jnp.sum(gc[h] * eye, axis=0, keepdims=True) for h in heads]
    decay = [jnp.where(incl, jnp.exp(jnp.where(incl, gc[h] - gc_row[h], 0.0)), 0.0) for h in heads]
    kb = [k[h] * beta[h] for h in heads]
    vb = [v[h] * beta[h] for h in heads]
    p = [-jnp.where(ri > ci, _dot_nt(kb[h], k[h], DN_PRECISION) * decay[h], 0.0) for h in heads]
    tinv = [eye + p[h] for h in heads]
    for _ in range(5):
        p = [_dot(p[h], p[h], DN_PRECISION) for h in heads]
        tinv = [tinv[h] + _dot(tinv[h], p[h], DN_PRECISION) for h in heads]
    eg = [jnp.exp(gc[h]) for h in heads]
    u = [_dot(tinv[h], vb[h], DN_PRECISION) for h in heads]
    wk = [_dot(tinv[h], kb[h] * eg[h], DN_PRECISION) for h in heads]
    attn = [_dot_nt(q[h], k[h]) * decay[h] for h in heads]
    g_last = [jnp.sum(jnp.where(last, gc[h], 0.0), axis=0, keepdims=True) for h in heads]
    k_tail = [k[h] * jnp.exp(g_last[h] - gc[h]) for h in heads]
    v_new = [u[h] - _dot(wk[h], state[h]) for h in heads]
    o = [_dot(q[h] * eg[h], state[h]) + _dot(attn[h], v_new[h]) for h in heads]
    new = [state[h] * jnp.exp(g_last[h]) + _dot_tn(k_tail[h], v_new[h]) for h in heads]
    return o, new


def _bg_cols(bg, h, heads):
    lane = lax.broadcasted_iota(jnp.int32, bg.shape, 1)
    beta = jnp.sum(jnp.where(lane == h, bg, 0.0), axis=1, keepdims=True)
    g = jnp.sum(jnp.where(lane == heads + h, bg, 0.0), axis=1, keepdims=True)
    return beta, g


def _dn_fwd(qkv, bg):
    rows = qkv.shape[0]
    heads = qkv.shape[1] // (3 * HEAD_A)
    n = rows // CHUNK
    hp, groups = heads, 1

    def body(q_ref, k_ref, v_ref, bg_ref, o_ref, hist_ref, s_ref):
        c, grp = pl.program_id(0), pl.program_id(1)

        @pl.when(c == 0)
        def _():
            for i in range(hp):
                s_ref[grp * hp + i] = jnp.zeros((HEAD_A, HEAD_A), F32)

        bg_v = bg_ref[...]
        cols = [slice(i * HEAD_A, (i + 1) * HEAD_A) for i in range(hp)]
        state = [s_ref[grp * hp + i] for i in range(hp)]
        beta_g = [_bg_cols(bg_v, grp * hp + i, heads) for i in range(hp)]
        o, new = _dn_chunk([q_ref[:, c_] for c_ in cols], [k_ref[:, c_] for c_ in cols], [v_ref[:, c_] for c_ in cols],
                           [b for b, _ in beta_g], [g for _, g in beta_g], state)
        for i in range(hp):
            hist_ref[0, i] = state[i]
            o_ref[:, cols[i]] = o[i]
            s_ref[grp * hp + i] = new[i]

    def part(p):
        return pl.BlockSpec((CHUNK, hp * HEAD_A), lambda c, grp: (c, p * groups + grp))

    return pl.pallas_call(
        body, name="deltanet_fwd", grid=(n, groups),
        in_specs=[part(0), part(1), part(2), pl.BlockSpec((CHUNK, 128), lambda c, grp: (c, 0))],
        out_specs=[part(0), pl.BlockSpec((1, hp, HEAD_A, HEAD_A), lambda c, grp: (c, grp, 0, 0))],
        out_shape=[jax.ShapeDtypeStruct((rows, heads * HEAD_A), F32),
                   jax.ShapeDtypeStruct((n, heads, HEAD_A, HEAD_A), F32)],
        scratch_shapes=[pltpu.VMEM((heads, HEAD_A, HEAD_A), F32)],
        compiler_params=_cp("arbitrary", "arbitrary"),
    )(qkv, qkv, qkv, bg)


def _dn_bwd(qkv, bg, hist, do):
    rows = qkv.shape[0]
    heads = qkv.shape[1] // (3 * HEAD_A)
    n = rows // CHUNK
    hp, groups = heads, 1

    def body(q_ref, k_ref, v_ref, bg_ref, hist_ref, do_ref, dqkv_ref, dbg_ref, ds_ref):
        c, grp = pl.program_id(0), pl.program_id(1)

        @pl.when(c == 0)
        def _():
            for i in range(hp):
                ds_ref[grp * hp + i] = jnp.zeros((HEAD_A, HEAD_A), F32)

        bg_v = bg_ref[...]
        lane = lax.broadcasted_iota(jnp.int32, (CHUNK, 128), 1)
        cols = [slice(i * HEAD_A, (i + 1) * HEAD_A) for i in range(hp)]
        beta_g = [_bg_cols(bg_v, grp * hp + i, heads) for i in range(hp)]
        _, vjp = jax.vjp(_dn_chunk, [q_ref[:, c_] for c_ in cols], [k_ref[:, c_] for c_ in cols],
                         [v_ref[:, c_] for c_ in cols], [b for b, _ in beta_g], [g for _, g in beta_g],
                         [hist_ref[0, i] for i in range(hp)])
        dq, dk, dv, dbeta, dg, ds = vjp(([do_ref[:, c_] for c_ in cols], [ds_ref[grp * hp + i] for i in range(hp)]))
        dbg = jnp.zeros((CHUNK, 128), F32)
        for i in range(hp):
            h = grp * hp + i
            for p, part_grad in enumerate((dq, dk, dv)):
                dqkv_ref[:, pl.ds((p * heads + i) * HEAD_A, HEAD_A)] = part_grad[i]
            ds_ref[h] = ds[i]
            dbg = dbg + jnp.where(lane == h, dbeta[i], 0.0) + jnp.where(lane == heads + h, dg[i], 0.0)

        @pl.when(grp == 0)
        def _():
            dbg_ref[...] = jnp.zeros_like(dbg_ref)

        dbg_ref[...] += dbg

    def part(p):
        return pl.BlockSpec((CHUNK, hp * HEAD_A), lambda c, grp: (n - 1 - c, p * groups + grp))

    return pl.pallas_call(
        body, name="deltanet_bwd", grid=(n, groups),
        in_specs=[part(0), part(1), part(2), pl.BlockSpec((CHUNK, 128), lambda c, grp: (n - 1 - c, 0)),
                  pl.BlockSpec((1, hp, HEAD_A, HEAD_A), lambda c, grp: (n - 1 - c, grp, 0, 0)), part(0)],
        out_specs=[pl.BlockSpec((CHUNK, 3 * heads * HEAD_A), lambda c, grp: (n - 1 - c, 0)),
                   pl.BlockSpec((CHUNK, 128), lambda c, grp: (n - 1 - c, 0))],
        out_shape=[jax.ShapeDtypeStruct(qkv.shape, F32), jax.ShapeDtypeStruct((rows, 128), F32)],
        scratch_shapes=[pltpu.VMEM((heads, HEAD_A, HEAD_A), F32)],
        compiler_params=_cp("arbitrary", "arbitrary"),
    )(qkv, qkv, qkv, bg, hist, do)


def _head_mask(heads, width):
    return (lax.broadcasted_iota(jnp.int32, (heads, width), 0)
            == lax.broadcasted_iota(jnp.int32, (heads, width), 1) // HEAD_B)


def _masked_rows(mask, row):
    return jnp.where(mask, row, 0.0).astype(BF16)


def _rwkv_fwd(r, w, k, v, a, b):
    rows, width = r.shape
    heads = width // HEAD_B
    ts = SCAN_STEPS

    def body(r_ref, w_ref, k_ref, v_ref, a_ref, b_ref, y_ref, hist_ref, s_ref):
        @pl.when(pl.program_id(0) == 0)
        def _():
            s_ref[...] = jnp.zeros_like(s_ref)

        mask = _head_mask(heads, width)
        onehot = mask.astype(BF16)
        onehot2 = jnp.concatenate([onehot, onehot], axis=0)
        bd = _block_diag_ones()

        spread_v = [_dot_tn(jnp.concatenate(_hi_lo(v_ref[j]), axis=0), onehot2) for j in range(ts)]
        s = s_ref[...]
        ys = []
        for j in range(ts):
            row = pl.ds(j, 1)
            hist_ref[j] = s
            (sa,) = _segsum_many([((s * a_ref[row, :]).astype(BF16),)], bd)
            s = s * w_ref[row, :] + sa * b_ref[row, :] + spread_v[j] * k_ref[row, :]
            ys.append(_dot_nt(_masked_rows(mask, r_ref[row, :]), s.astype(BF16)))
        for j in range(ts):
            y_ref[j] = ys[j]
        s_ref[...] = s

    blk = pl.BlockSpec((ts, width), lambda i: (i, 0))
    blk3 = pl.BlockSpec((ts, heads, HEAD_B), lambda i: (i, 0, 0))
    return pl.pallas_call(
        body, name="rwkv_fwd", grid=(rows // ts,),
        in_specs=[blk, blk, blk, blk3, blk, blk],
        out_specs=[blk3, pl.BlockSpec((ts, HEAD_B, width), lambda i: (i, 0, 0)),
                   pl.BlockSpec((HEAD_B, width), lambda i: (0, 0))],
        out_shape=[jax.ShapeDtypeStruct((rows, heads, HEAD_B), F32), jax.ShapeDtypeStruct((rows, HEAD_B, width), F32),
                   jax.ShapeDtypeStruct((HEAD_B, width), F32)],
        compiler_params=_cp("arbitrary"),
    )(r, w, k, v, a, b)


def _rwkv_bwd(r, w, k, v, a, b, hist, last, dy):
    rows, width = r.shape
    heads = width // HEAD_B
    ts = SCAN_STEPS
    nb = rows // ts

    def body(r_ref, w_ref, k_ref, v_ref, a_ref, b_ref, hist_ref, last_ref, dy_ref,
             dr_ref, dw_ref, dk_ref, dv_ref, da_ref, db_ref, g_ref, after_ref):
        @pl.when(pl.program_id(0) == 0)
        def _():
            g_ref[...] = jnp.zeros_like(g_ref)
            after_ref[...] = last_ref[...]

        mask = _head_mask(heads, width)
        onehot = mask.astype(BF16)
        bd = _block_diag_ones()

        def own_lanes(x):
            return jnp.sum(jnp.where(mask, x, 0.0), axis=0, keepdims=True)

        def colsum(x):
            return jnp.sum(x, axis=0, keepdims=True)

        dy_m = [dy_ref[j].astype(BF16) for j in range(ts)]
        spread_dy = [_dot_tn(dy_m[j], onehot) for j in range(ts)]
        state_after = [hist_ref[j + 1] if j < ts - 1 else after_ref[...] for j in range(ts)]
        dr = [own_lanes(_dot(dy_m[j], state_after[j].astype(BF16))) for j in range(ts)]
        sa_m = [_dot_nt(_masked_rows(mask, a_ref[pl.ds(j, 1), :]), hist_ref[j].astype(BF16)) for j in range(ts)]
        g = g_ref[...]
        dw, dk, db, da, dv = {}, {}, {}, {}, {}
        for j in reversed(range(ts)):
            row = pl.ds(j, 1)
            sp = hist_ref[j]
            g = g + spread_dy[j] * r_ref[row, :]
            (dsa,) = _segsum_many([((g * b_ref[row, :]).astype(BF16),)], bd)
            g_b = g.astype(BF16)
            both = _dot(jnp.concatenate([v_ref[j].astype(BF16), sa_m[j].astype(BF16)], axis=0), g_b)
            dk[j], db[j] = own_lanes(both[:heads]), own_lanes(both[heads:])
            dv[j] = _dot_nt(_masked_rows(mask, k_ref[row, :]), g_b)
            dw[j] = colsum(g * sp)
            da[j] = colsum(sp * dsa)
            g = g * w_ref[row, :] + dsa * a_ref[row, :]
        g_ref[...] = g
        after_ref[...] = hist_ref[0]
        for j in range(ts):
            dv_ref[j] = dv[j]
            for ref, vals in ((dr_ref, dr), (dw_ref, dw), (dk_ref, dk), (da_ref, da), (db_ref, db)):
                ref[pl.ds(j, 1), :] = vals[j]

    blk = pl.BlockSpec((ts, width), lambda i: (nb - 1 - i, 0))
    blk3 = pl.BlockSpec((ts, heads, HEAD_B), lambda i: (nb - 1 - i, 0, 0))
    state = pl.BlockSpec((HEAD_B, width), lambda i: (0, 0))
    return pl.pallas_call(
        body, name="rwkv_bwd", grid=(nb,),
        in_specs=[blk, blk, blk, blk3, blk, blk, pl.BlockSpec((ts, HEAD_B, width), lambda i: (nb - 1 - i, 0, 0)),
                  state, blk3],
        out_specs=[blk, blk, blk, blk3, blk, blk],
        out_shape=[jax.ShapeDtypeStruct((rows, width), F32)] * 3 + [jax.ShapeDtypeStruct((rows, heads, HEAD_B), F32)]
        + [jax.ShapeDtypeStruct((rows, width), F32)] * 2,
        scratch_shapes=[pltpu.VMEM((HEAD_B, width), F32), pltpu.VMEM((HEAD_B, width), F32)],
        compiler_params=_cp("arbitrary"),
    )(r, w, k, v, a, b, hist, last, dy)


def _live(row0, shape):
    return (row0 + lax.broadcasted_iota(jnp.int32, shape, 0)) >= PAD


def _norm_fn(row0, h, gain):
    return (_rms(h, gain),)


def _norm_res_fn(row0, h, gain):
    return _rms(h, gain), h


def _make_bg_fn(heads):
    def fn(row0, x, log_rate, dt_bias):
        lane = lax.broadcasted_iota(jnp.int32, x.shape, 1)
        beta = _sigmoid(x)
        g = -jnp.exp(log_rate) * _softplus(x + dt_bias)
        out = jnp.where(lane < heads, beta, jnp.where(lane < 2 * heads, g, 0.0))
        return (jnp.where(_live(row0, x.shape), out, 0.0),)
    return fn


def _b_pre_fn(row0, zf, w0, w_up, a0, a_up, g_up, k_k, k_a):
    d = w0.shape[1]
    r, k, v = zf[:, :d], zf[:, d:2 * d], zf[:, 2 * d:3 * d]
    lo = zf[:, 3 * d:3 * d + 128]
    lg = zf[:, 3 * d + 128:3 * d + LORA_PAD]
    lane = lax.broadcasted_iota(jnp.int32, lo.shape, 1)
    lw = _dot(jnp.where(lane < LORA_W, jnp.tanh(lo), 0.0), w_up)
    la = _dot(jnp.where(lane >= LORA_W, lo, 0.0), a_up)
    lane_g = lax.broadcasted_iota(jnp.int32, lg.shape, 1)
    gate = _dot(jnp.where(lane_g < LORA_G, _sigmoid(lg), 0.0), g_up)
    decay = jnp.exp(-jnp.exp(-_softplus(-(w0 + lw)) - 0.5))
    a = _sigmoid(a0 + la)
    kx = k * k_k
    kk = kx * lax.rsqrt(_segsum64(kx * kx) + 1e-6)
    k2 = k * (1.0 + (a - 1.0) * k_a)
    return r, decay, k2, v, -kk, kk * a, gate


def _post_fn(row0, o, zg, y, r, k2, v, gate, out_gain, r_k, ln_g, ln_b):
    d = o.shape[1]
    az, ga, gb = zg[:, :d], zg[:, d:2 * d], zg[:, 2 * d:]
    heads = d // HEAD_A
    parts = []
    for h in range(heads):
        oh = o[:, h * HEAD_A:(h + 1) * HEAD_A]
        parts.append(oh * lax.rsqrt(jnp.mean(oh * oh, axis=-1, keepdims=True) + EPS) * out_gain)
    o_a = jnp.concatenate(parts, axis=1) * _silu(az)
    mean = _segsum64(y) * (1.0 / HEAD_B)
    yc = y - mean
    var = _segsum64(yc * yc) * (1.0 / HEAD_B)
    yn = yc * lax.rsqrt(var + GN_EPS) * ln_g + ln_b
    o_b = (yn + _segsum64(r * k2 * r_k) * v) * gate
    return (_sigmoid(ga) * o_a + _sigmoid(gb) * o_b,)


def _loss(h3, target, gain, tb):
    rows, d = h3.shape

    def body(h_ref, t_ref, g_ref, dh_ref, dg_ref, l_ref):
        i = pl.program_id(0)
        live = (i * tb + lax.broadcasted_iota(jnp.int32, (tb, 1), 0)) >= CHUNK
        tgt = t_ref[...]

        def f(h, g):
            err = _rms(h, g) - tgt
            return 0.5 * jnp.sum(jnp.where(live, jnp.mean(err * err, axis=-1, keepdims=True), 0.0))

        val, vjp = jax.vjp(f, h_ref[...], g_ref[...])
        dh, dg = vjp(jnp.ones((), F32))
        dh_ref[...] = dh

        @pl.when(i == 0)
        def _():
            dg_ref[...] = jnp.zeros_like(dg_ref)
            l_ref[...] = jnp.zeros_like(l_ref)

        dg_ref[...] += dg
        l_ref[...] += jnp.full((1, 128), val, F32)

    blk = pl.BlockSpec((tb, d), lambda i: (i, 0))
    return pl.pallas_call(
        body, name="loss", grid=(rows // tb,),
        in_specs=[blk, blk, pl.BlockSpec((1, d), lambda i: (0, 0))],
        out_specs=[blk, pl.BlockSpec((1, d), lambda i: (0, 0)), pl.BlockSpec((1, 128), lambda i: (0, 0))],
        out_shape=[jax.ShapeDtypeStruct((rows, d), F32), jax.ShapeDtypeStruct((1, d), F32),
                   jax.ShapeDtypeStruct((1, 128), F32)],
        compiler_params=_cp("arbitrary"),
    )(h3, target, gain)


def _adamw_math(w, g, m, v):
    m2 = ADAM_B1 * m + (1.0 - ADAM_B1) * g
    v2 = ADAM_B2 * v + (1.0 - ADAM_B2) * (g * g)
    m_hat = m2 / (1.0 - ADAM_B1 ** ADAM_STEP)
    v_hat = v2 / (1.0 - ADAM_B2 ** ADAM_STEP)
    return -ADAM_LR * (m_hat / (jnp.sqrt(v_hat) + ADAM_EPS) + ADAM_WD * w), m2, v2


def _adamw(name, own, landed, w, m, v):
    rows, cols = w.shape
    if rows % 16 == 0:
        rb = _tb(rows, 128)
        grid, blk = (rows // rb,), pl.BlockSpec((rb, cols), lambda i: (i, 0))
        landed_blk = pl.BlockSpec((N_DEV - 1, rb, cols), lambda i: (0, i, 0))
    else:
        grid, blk = (cols // 128,), pl.BlockSpec((rows, 128), lambda i: (0, i))
        landed_blk = pl.BlockSpec((N_DEV - 1, rows, 128), lambda i: (0, 0, i))

    def body(o_ref, s_ref, w_ref, m_ref, v_ref, g_ref, d_ref, m2_ref, v2_ref):
        g = o_ref[...].astype(F32)
        for peer in range(N_DEV - 1):
            g = g + s_ref[peer].astype(F32)
        g_ref[...] = g
        d_ref[...], m2_ref[...], v2_ref[...] = _adamw_math(w_ref[...], g, m_ref[...], v_ref[...])

    return pl.pallas_call(
        body, name=name, grid=grid, in_specs=[blk, landed_blk, blk, blk, blk],
        out_specs=[blk] * 4, out_shape=[jax.ShapeDtypeStruct((rows, cols), F32)] * 4,
        compiler_params=_cp("parallel"),
    )(own, landed, w, m, v)


def _sum_slabs(name, slabs, rb):
    _, rows, cols = slabs.shape

    def body(s_ref, o_ref):
        g = s_ref[0]
        for dev in range(1, N_DEV):
            g = g + s_ref[dev]
        o_ref[...] = g

    return pl.pallas_call(
        body, name=name, grid=(rows // rb,),
        in_specs=[pl.BlockSpec((N_DEV, rb, cols), lambda i: (0, i, 0))],
        out_specs=pl.BlockSpec((rb, cols), lambda i: (i, 0)),
        out_shape=jax.ShapeDtypeStruct((rows, cols), F32), compiler_params=_cp("parallel"),
    )(slabs)


def _adamw_small(w, g, m, v):
    def body(w_ref, g_ref, m_ref, v_ref, d_ref, m2_ref, v2_ref):
        d_ref[...], m2_ref[...], v2_ref[...] = _adamw_math(w_ref[...], g_ref[...], m_ref[...], v_ref[...])

    return pl.pallas_call(body, name="adamw_small", out_shape=[jax.ShapeDtypeStruct(w.shape, F32)] * 3)(w, g, m, v)


def _place():
    return lax.axis_index("x"), lax.axis_index("y"), lax.axis_index("c")


def _index(p):
    return 4 * p[0] + 2 * p[1] + p[2]


def _all_gather(name, xs):
    n = len(xs)

    def body(*refs):
        x_refs, o_refs = refs[:n], refs[n:2 * n]
        send_sems, recv_sems, local_sems = refs[2 * n:]
        x, y, c = _place()
        me, sibling = (x, y, c), (x, y, 1 - c)
        chips = [(1 - x, y), (x, 1 - y), (1 - x, 1 - y)]

        def copy(i, k, block, to, src=None):
            dst = o_refs[i].at[_index(block)]
            return pltpu.make_async_remote_copy(src_ref=dst if src is None else src, dst_ref=dst,
                                                send_sem=send_sems.at[i, k], recv_sem=recv_sems.at[i, k],
                                                device_id=to, device_id_type=MESH_ID)

        mine = [pltpu.make_async_copy(x_refs[i], o_refs[i].at[_index(me)], local_sems.at[i]) for i in range(n)]
        for cp in mine:
            cp.start()
        first = []
        for i in range(n):
            first.append(copy(i, 0, me, sibling, src=x_refs[i]))
            first += [copy(i, 1 + j, me, (*chip, c), src=x_refs[i]) for j, chip in enumerate(chips)]
        for cp in first:
            cp.start()
        passed = []
        for j, chip in enumerate(chips):
            for i in range(n):
                copy(i, 1 + j, (*chip, c), me).wait_recv()
                cp = copy(i, 4 + j, (*chip, c), sibling)
                cp.start()
                passed.append(cp)
        for i in range(n):
            copy(i, 0, sibling, me).wait_recv()
            for j, chip in enumerate(chips):
                copy(i, 4 + j, (*chip, 1 - c), me).wait_recv()
        for cp in first + passed:
            cp.wait_send()
        for cp in mine:
            cp.wait()

    return pl.pallas_call(
        body, name=name, in_specs=[ANY] * n, out_specs=[ANY] * n,
        out_shape=[jax.ShapeDtypeStruct((N_DEV,) + x.shape, x.dtype) for x in xs],
        scratch_shapes=[pltpu.SemaphoreType.DMA((n, 7)), pltpu.SemaphoreType.DMA((n, 7)), pltpu.SemaphoreType.DMA((n,))],
    )(*xs)


def _exchange_start(name, xs, after=None):
    n = len(xs)
    copies = n * (N_DEV - 1)
    extra = [] if after is None else [after]

    def body(*refs):
        x_refs, land_refs = refs[:n], refs[n:2 * n]
        sems = refs[2 * n + len(extra):2 * n + len(extra) + 2 * copies]
        token = refs[-1]
        for i, k, peer in _exchange_copies(n):
            _exchange_copy(x_refs, land_refs, sems, i, k, peer).start()
        token[...] = jnp.zeros_like(token)

    lands = [lax.empty((N_DEV - 1,) + x.shape[1:], x.dtype) for x in xs]
    out = pl.pallas_call(
        body, name=name,
        out_shape=(*[pltpu.SemaphoreType.DMA(())] * (2 * copies), *[pltpu.HBM(x.shape, x.dtype) for x in xs],
                   *[pltpu.HBM(l.shape, l.dtype) for l in lands], jax.ShapeDtypeStruct((8, 128), F32)),
        in_specs=[HBM_SPEC] * (2 * n) + [ANY] * len(extra),
        out_specs=(*[SEM_SPEC] * (2 * copies), *[HBM_SPEC] * (2 * n), pl.BlockSpec(memory_space=pltpu.VMEM)),
        input_output_aliases={i: 2 * copies + i for i in range(2 * n)},
        compiler_params=pltpu.CompilerParams(has_side_effects=pltpu.SideEffectType.DATAFLOW_SIDE_EFFECTING),
    )(*[pltpu.with_memory_space_constraint(a, pltpu.HBM) for a in list(xs) + lands], *extra)
    sems, rest = list(out[:2 * copies]), out[2 * copies:]
    return sems, list(rest[:n]), list(rest[n:2 * n]), rest[-1]


def _exchange_copies(n):
    x, y, c = _place()
    for k in range(1, N_DEV):
        peer = ((1 - x) if k & 4 else x, (1 - y) if k & 2 else y, (1 - c) if k & 1 else c)
        for i in range(n):
            yield i, k - 1, peer


def _exchange_copy(x_refs, land_refs, sems, i, k, peer):
    copies = len(sems) // 2
    which = i * (N_DEV - 1) + k
    return pltpu.make_async_remote_copy(src_ref=x_refs[i].at[_index(peer)], dst_ref=land_refs[i].at[k],
                                        send_sem=sems[which], recv_sem=sems[copies + which],
                                        device_id=peer, device_id_type=MESH_ID)


def _exchange_wait(name, sems, xs, lands, after):
    n = len(xs)

    def body(*refs):
        x_refs, land_refs = refs[:n], refs[n:2 * n]
        sem_refs = refs[2 * n:2 * n + len(sems)]
        for i, k, peer in _exchange_copies(n):
            copy = _exchange_copy(x_refs, land_refs, sem_refs, i, k, peer)
            copy.wait_send()
            copy.wait_recv()

    out = pl.pallas_call(
        body, name=name,
        out_shape=(*[pltpu.HBM(x.shape, x.dtype) for x in xs], *[pltpu.HBM(l.shape, l.dtype) for l in lands]),
        in_specs=[HBM_SPEC] * (2 * n) + [SEM_SPEC] * len(sems) + [ANY], out_specs=tuple([HBM_SPEC] * (2 * n)),
        input_output_aliases={i: i for i in range(2 * n)},
        compiler_params=pltpu.CompilerParams(has_side_effects=pltpu.SideEffectType.DATAFLOW_SIDE_EFFECTING),
    )(*xs, *lands, *sems, after)
    return list(out[:n]), list(out[n:])


def _pack(arrays):
    flat = jnp.concatenate([a.reshape(-1) for a in arrays])
    pad = (-flat.shape[0]) % 1024
    return jnp.pad(flat, (0, pad)).reshape(-1, 128)


def _unpack(packed, shapes):
    flat = packed.reshape(-1)
    out, pos = [], 0
    for s in shapes:
        size = 1
        for dim in s:
            size *= dim
        out.append(flat[pos:pos + size].reshape(s))
        pos += size
    return out


def _cols_from_slabs(stack):
    return jnp.transpose(stack, (1, 0, 2)).reshape(stack.shape[1], -1)


def kernel(x, meta_tokens, ffn1_norm, ffn1_w_gu, ffn1_w_down, mix_norm, w_in, a_conv_w, a_log_rate, a_dt_bias, a_out_norm, b_shift_mu, b_w0, b_w_up, b_a0, b_a_up, b_g_up, b_k_k, b_k_a, b_r_k, b_ln_gain, b_ln_bias, w_out, ffn2_norm, ffn2_w_gu, ffn2_w_down, final_norm, loss_target, m_meta_tokens, m_ffn1_norm, m_ffn1_w_gu, m_ffn1_w_down, m_mix_norm, m_w_in, m_a_conv_w, m_a_log_rate, m_a_dt_bias, m_a_out_norm, m_b_shift_mu, m_b_w0, m_b_w_up, m_b_a0, m_b_a_up, m_b_g_up, m_b_k_k, m_b_k_a, m_b_r_k, m_b_ln_gain, m_b_ln_bias, m_w_out, m_ffn2_norm, m_ffn2_w_gu, m_ffn2_w_down, m_final_norm, v_meta_tokens, v_ffn1_norm, v_ffn1_w_gu, v_ffn1_w_down, v_mix_norm, v_w_in, v_a_conv_w, v_a_log_rate, v_a_dt_bias, v_a_out_norm, v_b_shift_mu, v_b_w0, v_b_w_up, v_b_a0, v_b_a_up, v_b_g_up, v_b_k_k, v_b_k_a, v_b_r_k, v_b_ln_gain, v_b_ln_bias, v_w_out, v_ffn2_norm, v_ffn2_w_gu, v_ffn2_w_down, v_final_norm):
    names = ['meta_tokens', 'ffn1_norm', 'ffn1_w_gu', 'ffn1_w_down', 'mix_norm', 'w_in', 'a_conv_w', 'a_log_rate',
             'a_dt_bias', 'a_out_norm', 'b_shift_mu', 'b_w0', 'b_w_up', 'b_a0', 'b_a_up', 'b_g_up', 'b_k_k', 'b_k_a',
             'b_r_k', 'b_ln_gain', 'b_ln_bias', 'w_out', 'ffn2_norm', 'ffn2_w_gu', 'ffn2_w_down', 'final_norm']
    env = dict(locals())
    wts = {k: env[k] for k in names}
    mom_m = {k: env['m_' + k] for k in names}
    mom_v = {k: env['v_' + k] for k in names}
    big = ['ffn1_w_gu', 'ffn1_w_down', 'w_in', 'w_out', 'ffn2_w_gu', 'ffn2_w_down']
    col_sharded = {'ffn1_w_gu', 'w_in', 'ffn2_w_gu'}
    shard_of = lambda tree, k: tree[k][0].T if k in col_sharded else tree[k][0]
    small_sharded = ['meta_tokens', 'a_conv_w', 'b_w_up', 'b_a_up', 'b_g_up']
    replicated = [k for k in names if k not in big and k not in small_sharded]

    seq, d = x.shape[1], x.shape[2]
    rows = PAD + N_META + seq
    heads_a = d // HEAD_A
    tb_mm = _tb(rows, 416)
    tb_vjp = _tb(rows, 208)
    tb_dw = _tb(rows, 2080)
    me = _index(_place())

    big_local = [shard_of(wts, k).astype(BF16) for k in big]
    small_local = [wts['meta_tokens']] + [wts[k][0] for k in small_sharded[1:]]
    gathered = _all_gather("gather_weights", big_local + small_local)
    gw = dict(zip(big + small_sharded, gathered))
    full = {k: (gw[k] if k in ('ffn1_w_gu', 'ffn2_w_gu') else gw[k].reshape(-1, gw[k].shape[-1])) for k in big}
    for k in small_sharded:
        full[k] = _cols_from_slabs(gw[k])
    for k in replicated:
        full[k] = wts[k].reshape(1, -1)

    win = full['w_in']
    n_b = 3 * d + LORA_W + LORA_A + LORA_G
    off_beta, off_b = 4 * d, 4 * d + 2 * heads_a
    off_ga = off_b + n_b
    b_width = 3 * d + LORA_PAD
    zrows = lambda r: jnp.zeros((r, d), BF16)
    w_qkv = win[:3 * d]
    w_zg = jnp.concatenate([win[3 * d:4 * d], win[off_ga:off_ga + 2 * d]], axis=0)
    w_b = jnp.concatenate([win[off_b:off_b + n_b], zrows(b_width - n_b)], axis=0)
    w_bg = jnp.concatenate([win[off_beta:off_beta + 2 * heads_a], zrows(128 - 2 * heads_a)], axis=0)

    def lanes(vec, start, width):
        return jnp.pad(vec.reshape(1, -1), ((0, 0), (start, width - start - vec.size)))

    log_rate = lanes(wts['a_log_rate'], heads_a, 128)
    dt_bias = lanes(wts['a_dt_bias'], heads_a, 128)
    mu = lanes(wts['b_shift_mu'], 0, b_width)
    w_up = jnp.pad(full['b_w_up'], ((0, 128 - LORA_W), (0, 0)))
    a_up = jnp.pad(full['b_a_up'], ((LORA_W, 0), (0, 0)))
    g_up = jnp.pad(full['b_g_up'], ((0, 256 - LORA_G), (0, 0)))
    b_pars = [full['b_w0'], w_up, full['b_a0'], a_up, g_up, full['b_k_k'], full['b_k_a']]
    post_pars = [full['a_out_norm'], full['b_r_k'], full['b_ln_gain'], full['b_ln_bias']]
    bg_fn = _make_bg_fn(heads_a)

    h0 = jnp.concatenate([jnp.zeros((PAD, d), F32), full['meta_tokens'], x[0]], axis=0)
    h1 = _ffn_fwd("ffn1_fwd", h0, full['ffn1_norm'], full['ffn1_w_gu'], full['ffn1_w_down'], tb_mm)
    (u,) = _tok_fwd("mix_norm_fwd", _norm_fn, [h1], [full['mix_norm']], [(d, BF16)], tb_mm)
    z_qkv = _mm("in_qkv", u, w_qkv, trans_b=True, tb=tb_mm, tn=_col_tile(3 * d, 1536))
    z_zg = _mm("in_zg", u, w_zg, trans_b=True, tb=tb_mm, tn=_col_tile(3 * d, 1536))
    z_b = _mm("in_b", u, w_b, trans_b=True, tb=tb_mm, tn=_col_tile(b_width, 1536))
    z_bg = _mm("in_bg", u, w_bg, trans_b=True, tb=tb_mm, tn=128)
    qkv = _a_pre_fwd(z_qkv, full['a_conv_w'])
    (bg,) = _tok_fwd("bg_fwd", bg_fn, [z_bg], [log_rate, dt_bias], [(128, F32)], tb_mm)
    o_dn, dn_hist = _dn_fwd(qkv, bg)
    zf = _shift_fwd(z_b, mu)
    rr, ww, kk2, vv, av, bv, gate = _tok_fwd("b_pre_fwd", _b_pre_fn, [zf], b_pars, [(d, F32)] * 7, tb_vjp)
    per_head = lambda t: t.reshape(rows, d // HEAD_B, HEAD_B)
    y_heads, b_hist, b_last = _rwkv_fwd(rr, ww, kk2, per_head(vv), av, bv)
    y_b = y_heads.reshape(rows, d)
    post_toks = [o_dn, z_zg, y_b, rr, kk2, vv, gate]
    (merged,) = _tok_fwd("post_fwd", _post_fn, post_toks, post_pars, [(d, BF16)], tb_vjp)
    h2 = _mm("out_proj", merged, full['w_out'], add=h1, tb=tb_mm, tn=d)
    h3 = _ffn_fwd("ffn2_fwd", h2, full['ffn2_norm'], full['ffn2_w_gu'], full['ffn2_w_down'], tb_mm)

    target = jnp.pad(loss_target[0], ((CHUNK, 0), (0, 0)))
    dh3, g_final, loss_part = _loss(h3, target, full['final_norm'].reshape(1, d), tb_vjp)

    def ffn_backward(tag, h, dout, key_norm, key_gu, key_down):
        dh, dh_bf, dgain, xn, act, dgate, dup, dhalf = _ffn_bwd(tag + "_bwd", h, full[key_norm], dout, full[key_gu],
                                                                full[key_down], tb_mm)
        fc = dgate.shape[2]
        d_gu = jnp.concatenate([_mm_tn_from_slabs(tag + "_dw_gate", dgate, xn, tk=tb_dw).reshape(-1, fc, d),
                                _mm_tn_from_slabs(tag + "_dw_up", dup, xn, tk=tb_dw).reshape(-1, fc, d)], axis=0)
        d_down = _mm_tn_from_slabs(tag + "_dw_down", act, dhalf, tk=tb_dw).reshape(N_DEV, -1, d)
        return dh, dh_bf, dgain, d_gu, d_down

    dh2, dh2_bf, g_ffn2_norm, g_ffn2_gu, g_ffn2_down = ffn_backward("ffn2", h2, dh3, 'ffn2_norm', 'ffn2_w_gu',
                                                                    'ffn2_w_down')
    g_w_out = _mm_tn("dw_out", merged, dh2_bf, tm=d, tn=d, tk=tb_dw).reshape(N_DEV, -1, d)

    def start_exchange(tag, keys, slabs, after=None):
        sems, kept, lands, token = _exchange_start("exchange_start_" + tag, slabs, after)
        return (tag, keys, sems, kept, lands), token

    ex_ffn2, token_ffn2 = start_exchange("ffn2", ['ffn2_w_gu', 'ffn2_w_down', 'w_out'], [g_ffn2_gu, g_ffn2_down, g_w_out])
    dmerged = _mm("d_merged", dh2_bf, full['w_out'], trans_b=True, after=token_ffn2, tb=tb_mm, tn=d)
    post_grads = _tok_bwd("post_bwd", _post_fn, post_toks, post_pars, [[dmerged]], list(range(7)), tb_vjp,
                          [F32, BF16] + [F32] * 5)
    do_dn, dz_zg, dy_b, dr1, dk1, dv1, dgate = post_grads[:7]
    g_out_norm, g_r_k, g_ln_g, g_ln_b = post_grads[7:]
    dr2, dw2, dk2, dv_heads, da2, db2 = _rwkv_bwd(rr, ww, kk2, per_head(vv), av, bv, b_hist, b_last, per_head(dy_b))
    dv2 = dv_heads.reshape(rows, d)
    b_grads = _tok_bwd("b_pre_bwd", _b_pre_fn, [zf], b_pars,
                       [[dr1, dr2], [dw2], [dk1, dk2], [dv1, dv2], [da2], [db2], [dgate]], [0], tb_vjp)
    dzf = b_grads[0]
    g_w0, g_w_up, g_a0, g_a_up, g_g_up, g_k_k, g_k_a = b_grads[1:]
    dz_b, g_mu = _shift_bwd(z_b, mu, dzf)
    dqkv, dbg = _dn_bwd(qkv, bg, dn_hist, do_dn)
    dz_qkv, g_conv = _a_pre_bwd(z_qkv, full['a_conv_w'], dqkv)
    dz_bg, g_log_rate, g_dt_bias = _tok_bwd("bg_bwd", bg_fn, [z_bg], [log_rate, dt_bias], [[dbg]], [0], tb_mm, [BF16])

    du = None
    g_w_in_parts = []
    for tag, dz, wpiece in (("qkv", dz_qkv, w_qkv), ("zg", dz_zg, w_zg), ("b", dz_b, w_b), ("bg", dz_bg, w_bg)):
        du = _mm("du_" + tag, dz, wpiece, add=du, tb=tb_mm, tn=d)
        g_w_in_parts.append(_mm_tn("dw_in_" + tag, dz, u, tm=_col_tile(dz.shape[1], 1536), tn=d, tk=tb_dw))
    gp_qkv, gp_zg, gp_b, gp_bg = g_w_in_parts
    g_w_in = jnp.concatenate([gp_qkv, gp_zg[:d], gp_bg[:2 * heads_a], gp_b[:n_b], gp_zg[d:]],
                             axis=0).reshape(N_DEV, -1, d)
    ex_w_in, token_w_in = start_exchange("w_in", ['w_in'], [g_w_in])
    dh1, g_mix_norm = _tok_bwd("mix_norm_bwd", _norm_res_fn, [h1], [full['mix_norm']], [[du], [dh2]], [0], tb_vjp,
                               after=token_w_in)
    dh0, _, g_ffn1_norm, g_ffn1_gu, g_ffn1_down = ffn_backward("ffn1", h0, dh1, 'ffn1_norm', 'ffn1_w_gu', 'ffn1_w_down')

    small_full = {
        'meta_tokens': dh0[PAD:CHUNK], 'ffn1_norm': g_ffn1_norm, 'mix_norm': g_mix_norm, 'a_conv_w': g_conv,
        'a_log_rate': g_log_rate[:, heads_a:2 * heads_a], 'a_dt_bias': g_dt_bias[:, heads_a:2 * heads_a],
        'a_out_norm': g_out_norm, 'b_shift_mu': g_mu[:, :n_b], 'b_w0': g_w0, 'b_w_up': g_w_up[:LORA_W],
        'b_a0': g_a0, 'b_a_up': g_a_up[LORA_W:], 'b_g_up': g_g_up[:LORA_G], 'b_k_k': g_k_k, 'b_k_a': g_k_a,
        'b_r_k': g_r_k, 'b_ln_gain': g_ln_g, 'b_ln_bias': g_ln_b, 'ffn2_norm': g_ffn2_norm, 'final_norm': g_final,
    }
    small_names = list(small_full)
    packed = _pack([small_full[k] for k in small_names] + [loss_part[:, :1]])
    (all_parts,) = _all_gather("gather_small_grads", [packed])
    ex_ffn1, _ = start_exchange("ffn1", ['ffn1_w_gu', 'ffn1_w_down'], [g_ffn1_gu, g_ffn1_down], after=all_parts)
    summed = _sum_slabs("sum_small_grads", all_parts, packed.shape[0])
    pieces = _unpack(summed, [small_full[k].shape for k in small_names] + [(1, 1)])
    small_grad = dict(zip(small_names, pieces[:-1]))
    loss = pieces[-1].reshape(())

    grads, deltas, new_m, new_v = {}, {}, {}, {}
    local_small = {}
    for k in small_names:
        g = small_grad[k]
        if k in small_sharded:
            width = wts[k].shape[-1]
            g = lax.dynamic_slice_in_dim(g, me * width, width, axis=1)
        local_small[k] = g.reshape(wts[k].shape)
    pk = lambda tree: _pack([tree[k] for k in small_names])
    dl_s, m_s, v_s = _adamw_small(pk(wts), pk(local_small), pk(mom_m), pk(mom_v))
    shapes = [wts[k].shape for k in small_names]
    for k, dl, m2, v2 in zip(small_names, _unpack(dl_s, shapes), _unpack(m_s, shapes), _unpack(v_s, shapes)):
        grads[k], deltas[k], new_m[k], new_v[k] = local_small[k], dl, m2, v2

    done = dl_s
    for tag, keys, sems, kept, lands in (ex_ffn2, ex_w_in, ex_ffn1):
        kept, lands = _exchange_wait("exchange_wait_" + tag, sems, kept, lands, done)
        for k, slabs, landed in zip(keys, kept, lands):
            own = lax.dynamic_index_in_dim(slabs, me, axis=0, keepdims=False)
            res = _adamw("adamw_" + k, own, landed, shard_of(wts, k), shard_of(mom_m, k), shard_of(mom_v, k))
            done = res[1]
            res = [(t.T if k in col_sharded else t)[None] for t in res]
            grads[k], deltas[k], new_m[k], new_v[k] = res

    grad_x = dh0[CHUNK:][None]
    return (loss, grad_x, *[grads[k] for k in names], *[deltas[k] for k in names],
            *[new_m[k] for k in names], *[new_v[k] for k in names])
```

```python
import functools

import jax
import jax.numpy as jnp
from jax import lax
from jax.experimental import pallas as pl
from jax.experimental.pallas import tpu as pltpu

F32 = jnp.float32
BF16 = jnp.bfloat16
N_DEV = 8
N_META = 16
CHUNK = 64
PAD = CHUNK - N_META
HEAD_A = 128
HEAD_B = 64
LORA_W, LORA_A, LORA_G = 64, 64, 160
LORA_PAD = 384
EPS = 1e-6
GN_EPS = HEAD_B * 1e-5
ADAM_LR, ADAM_B1, ADAM_B2, ADAM_EPS, ADAM_WD, ADAM_STEP = 0.001, 0.9, 0.999, 1e-08, 0.01, 10
SCAN_STEPS = 16
MXU_WIDTH = 256
VMEM_LIMIT = 56 * 1024 * 1024
DN_PRECISION = lax.Precision.HIGH
MESH_ID = pl.DeviceIdType.MESH
ANY = pl.BlockSpec(memory_space=pl.ANY)
HBM_SPEC = pl.BlockSpec(memory_space=pltpu.HBM)
SEM_SPEC = pl.BlockSpec(memory_space=pltpu.SEMAPHORE)


def _cp(*sem):
    return pltpu.CompilerParams(dimension_semantics=sem, vmem_limit_bytes=VMEM_LIMIT)


def _tb(t, target):
    best = 16
    for d in range(16, target + 1, 16):
        if t % d == 0:
            best = d
    return best


def _sigmoid(x):
    return 1.0 / (1.0 + jnp.exp(-x))


def _silu(x):
    return x * _sigmoid(x)


def _softplus(x):
    return jnp.maximum(x, 0.0) + jnp.log(1.0 + jnp.exp(-jnp.abs(x)))


def _dot_nt(a, b, precision=None):
    return lax.dot_general(a, b, (((1,), (1,)), ((), ())), preferred_element_type=F32, precision=precision)


def _dot_tn(a, b, precision=None):
    return lax.dot_general(a, b, (((0,), (0,)), ((), ())), preferred_element_type=F32, precision=precision)


def _dot(a, b, precision=None):
    return jnp.dot(a, b, preferred_element_type=F32, precision=precision)


def _block_diag_ones():
    i = lax.broadcasted_iota(jnp.int32, (MXU_WIDTH, MXU_WIDTH), 0) // HEAD_B
    j = lax.broadcasted_iota(jnp.int32, (MXU_WIDTH, MXU_WIDTH), 1) // HEAD_B
    return (i == j).astype(BF16)


def _hi_lo(x):
    hi = x.astype(BF16)
    return hi, (x - hi.astype(F32)).astype(BF16)


def _segsum_many(xs, bd):
    groups = [x if isinstance(x, tuple) else _hi_lo(x) for x in xs]
    rows = groups[0][0].shape[0]
    stacked = jnp.concatenate([p for grp in groups for p in grp], axis=0)
    out = jnp.concatenate([_dot(stacked[:, s:s + MXU_WIDTH], bd) for s in range(0, stacked.shape[1], MXU_WIDTH)], axis=1)
    res, pos = [], 0
    for grp in groups:
        acc = out[pos:pos + rows]
        for j in range(1, len(grp)):
            acc = acc + out[pos + j * rows:pos + (j + 1) * rows]
        res.append(acc)
        pos += len(grp) * rows
    return res


def _segsum_impl(x):
    return _segsum_many([x], _block_diag_ones())[0]


@jax.custom_vjp
def _segsum64(x):
    return _segsum_impl(x)


_segsum64.defvjp(lambda x: (_segsum_impl(x), None), lambda _, ct: (_segsum_impl(ct),))


def _tok(t):
    return t if isinstance(t, tuple) else (t, t.shape[1], 0)


def _tok_spec(tb, width, colblk):
    return pl.BlockSpec((tb, width), lambda i: (i, colblk))


def _par_spec(p):
    return pl.BlockSpec(p.shape, lambda i: (0, 0))


def _tok_fwd(name, fn, toks, pars, outs, tb):
    toks = [_tok(t) for t in toks]
    rows = toks[0][0].shape[0]
    n_in = len(toks) + len(pars)

    def body(*refs):
        row0 = pl.program_id(0) * tb
        res = fn(row0, *[r[...] for r in refs[:n_in]])
        for r, o in zip(refs[n_in:], res):
            r[...] = o.astype(r.dtype)

    return pl.pallas_call(
        body, name=name, grid=(rows // tb,),
        in_specs=[_tok_spec(tb, w, c) for _, w, c in toks] + [_par_spec(p) for p in pars],
        out_specs=[_tok_spec(tb, w, 0) for w, _ in outs],
        out_shape=[jax.ShapeDtypeStruct((rows, w), dt) for w, dt in outs],
        compiler_params=_cp("parallel"),
    )(*[a for a, _, _ in toks], *pars)


def _tok_bwd(name, fn, toks, pars, cts, want, tb, want_dtypes=None, after=None):
    toks = [_tok(t) for t in toks]
    want_dtypes = want_dtypes or [F32] * len(want)
    cts = [[_tok(c) for c in group] for group in cts]
    flat_cts = [c for group in cts for c in group]
    rows = toks[0][0].shape[0]
    n_tok, n_par, n_ct = len(toks), len(pars), len(flat_cts)
    extra = [] if after is None else [after]

    def body(*refs):
        i = pl.program_id(0)
        row0 = i * tb
        prim = [r[...].astype(F32) for r in refs[:n_tok + n_par]]
        ct_refs = list(refs[n_tok + n_par:n_tok + n_par + n_ct])
        out_refs = refs[n_tok + n_par + n_ct + len(extra):]
        res, vjp = jax.vjp(lambda *a: fn(row0, *a), *prim)
        ct = []
        for group, o in zip(cts, res):
            acc = None
            for _ in group:
                v = ct_refs.pop(0)[...].astype(F32)
                acc = v if acc is None else acc + v
            ct.append(acc.astype(o.dtype))
        grads = vjp(tuple(ct))
        for r, k in zip(out_refs[:len(want)], want):
            r[...] = grads[k].astype(r.dtype)

        @pl.when(i == 0)
        def _():
            for r in out_refs[len(want):]:
                r[...] = jnp.zeros_like(r)

        for r, g in zip(out_refs[len(want):], grads[n_tok:]):
            r[...] += g

    return pl.pallas_call(
        body, name=name, grid=(rows // tb,),
        in_specs=[_tok_spec(tb, w, c) for _, w, c in toks] + [_par_spec(p) for p in pars]
        + [_tok_spec(tb, w, c) for _, w, c in flat_cts] + [ANY] * len(extra),
        out_specs=[_tok_spec(tb, toks[k][1], 0) for k in want] + [_par_spec(p) for p in pars],
        out_shape=[jax.ShapeDtypeStruct((rows, toks[k][1]), dt) for k, dt in zip(want, want_dtypes)]
        + [jax.ShapeDtypeStruct(p.shape, F32) for p in pars],
        compiler_params=_cp("arbitrary"),
    )(*[a for a, _, _ in toks], *pars, *[a for a, _, _ in flat_cts], *extra)


def _mm(name, a, b, *, trans_b=False, add=None, after=None, tb, tn):
    rows, k = a.shape
    n = b.shape[0] if trans_b else b.shape[1]

    def body(*refs):
        a_ref, b_ref = refs[:2]
        o_ref = refs[-1]
        acc = _dot_nt(a_ref[...], b_ref[...]) if trans_b else _dot(a_ref[...], b_ref[...])
        if add is not None:
            acc = acc + refs[2][...]
        o_ref[...] = acc

    in_specs = [pl.BlockSpec((tb, k), lambda i, j: (i, 0)),
                pl.BlockSpec((tn, k), lambda i, j: (j, 0)) if trans_b else pl.BlockSpec((k, tn), lambda i, j: (0, j))]
    args = [a, b]
    if add is not None:
        in_specs.append(pl.BlockSpec((tb, tn), lambda i, j: (i, j)))
        args.append(add)
    if after is not None:
        in_specs.append(ANY)
        args.append(after)
    return pl.pallas_call(
        body, name=name, grid=(rows // tb, n // tn), in_specs=in_specs,
        out_specs=pl.BlockSpec((tb, tn), lambda i, j: (i, j)),
        out_shape=jax.ShapeDtypeStruct((rows, n), F32),
        compiler_params=_cp("parallel", "parallel"),
    )(*args)


def _mm_tn_call(name, grid, a, b, a_spec, b_spec, o_spec, acc_shape, out_shape):
    last = len(grid) - 1

    def body(a_ref, b_ref, o_ref, acc_ref):
        k = pl.program_id(last)

        @pl.when(k == 0)
        def _():
            acc_ref[...] = jnp.zeros_like(acc_ref)

        a_blk = a_ref[0] if len(a_ref.shape) == 3 else a_ref[...]
        b_blk = b_ref[0] if len(b_ref.shape) == 3 else b_ref[...]
        acc_ref[...] += _dot_tn(a_blk, b_blk)

        @pl.when(k == grid[last] - 1)
        def _():
            if len(o_ref.shape) == 3:
                o_ref[0] = acc_ref[...].astype(o_ref.dtype)
            else:
                o_ref[...] = acc_ref[...].astype(o_ref.dtype)

    return pl.pallas_call(
        body, name=name, grid=grid, in_specs=[a_spec, b_spec], out_specs=o_spec,
        out_shape=jax.ShapeDtypeStruct(out_shape, BF16), scratch_shapes=[pltpu.VMEM(acc_shape, F32)],
        compiler_params=_cp(*(["parallel"] * last + ["arbitrary"])),
    )(a, b)


def _mm_tn(name, a, b, *, tm, tn, tk):
    rows, m = a.shape
    n = b.shape[1]
    return _mm_tn_call(name, (m // tm, n // tn, rows // tk), a, b,
                       pl.BlockSpec((tk, tm), lambda i, j, k: (k, i)), pl.BlockSpec((tk, tn), lambda i, j, k: (k, j)),
                       pl.BlockSpec((tm, tn), lambda i, j, k: (i, j)), (tm, tn), (m, n))


def _mm_tn_from_slabs(name, a3, b, *, tk):
    s, rows, c = a3.shape
    n = b.shape[1]
    return _mm_tn_call(name, (s, rows // tk), a3, b,
                       pl.BlockSpec((1, tk, c), lambda i, k: (i, k, 0)), pl.BlockSpec((tk, n), lambda i, k: (k, 0)),
                       pl.BlockSpec((c, n), lambda i, k: (i, 0)), (c, n), (s * c, n))


def _col_tile(n, target):
    if n <= target:
        return n
    best = 128
    for d in range(128, target + 1, 128):
        if n % d == 0:
            best = d
    return best


def _rms(x, gain):
    return x * lax.rsqrt(jnp.mean(x * x, axis=-1, keepdims=True) + EPS) * gain


def _ffn_specs(d, fc, nj):
    return [pl.BlockSpec((1, fc, d), lambda i, j: (j, 0, 0)), pl.BlockSpec((1, fc, d), lambda i, j: (nj + j, 0, 0)),
            pl.BlockSpec((fc, d), lambda i, j: (j, 0))]


def _ffn_fwd(name, h, gain, wgu, wd, tb):
    rows, d = h.shape
    nj = wgu.shape[0] // 2
    fc = wgu.shape[1]

    def body(h_ref, g_ref, wg_ref, wu_ref, wd_ref, o_ref, xn_s, acc_s):
        j = pl.program_id(1)

        @pl.when(j == 0)
        def _():
            xn_s[...] = _rms(h_ref[...], g_ref[...]).astype(BF16)
            acc_s[...] = jnp.zeros_like(acc_s)

        wg, wu, wdn = wg_ref[0], wu_ref[0], wd_ref[...]
        for half in range(2):
            rs = pl.ds(half * (tb // 2), tb // 2)
            xn = xn_s[rs, :]
            gate = _dot_nt(xn, wg)
            up = _dot_nt(xn, wu)
            acc_s[rs, :] += _dot((_silu(gate) * up).astype(BF16), wdn)

        @pl.when(j == nj - 1)
        def _():
            o_ref[...] = h_ref[...] + 0.5 * acc_s[...]

    return pl.pallas_call(
        body, name=name, grid=(rows // tb, nj),
        in_specs=[pl.BlockSpec((tb, d), lambda i, j: (i, 0)), pl.BlockSpec((1, d), lambda i, j: (0, 0))]
        + _ffn_specs(d, fc, nj),
        out_specs=pl.BlockSpec((tb, d), lambda i, j: (i, 0)),
        out_shape=jax.ShapeDtypeStruct((rows, d), F32),
        scratch_shapes=[pltpu.VMEM((tb, d), BF16), pltpu.VMEM((tb, d), F32)],
        compiler_params=_cp("parallel", "arbitrary"),
    )(h, gain, wgu, wgu, wd)


def _ffn_bwd(name, h, gain, dout, wgu, wd, tb):
    rows, d = h.shape
    nj = wgu.shape[0] // 2
    fc = wgu.shape[1]

    def body(h_ref, g_ref, do_ref, wg_ref, wu_ref, wd_ref,
             dh_ref, dhb_ref, dg_ref, xn_ref, act_ref, dgate_ref, dup_ref, dhalf_ref, dxn_s):
        i, j = pl.program_id(0), pl.program_id(1)

        @pl.when(j == 0)
        def _():
            xn_ref[...] = _rms(h_ref[...], g_ref[...]).astype(BF16)
            dhalf_ref[...] = (0.5 * do_ref[...]).astype(BF16)
            dxn_s[...] = jnp.zeros_like(dxn_s)

        wg, wu, wdn = wg_ref[0], wu_ref[0], wd_ref[...]
        for half in range(2):
            rs = pl.ds(half * (tb // 2), tb // 2)
            xn = xn_ref[rs, :]
            gate = _dot_nt(xn, wg)
            up = _dot_nt(xn, wu)
            sg = _sigmoid(gate)
            dact = _dot_nt(dhalf_ref[rs, :], wdn)
            act_ref[0, rs, :] = (gate * sg * up).astype(BF16)
            dgate = (dact * up * (sg * (1.0 + gate * (1.0 - sg)))).astype(BF16)
            dup = (dact * gate * sg).astype(BF16)
            dgate_ref[0, rs, :] = dgate
            dup_ref[0, rs, :] = dup
            dxn_s[rs, :] += _dot(dgate, wg) + _dot(dup, wu)

        @pl.when((i == 0) & (j == 0))
        def _():
            dg_ref[...] = jnp.zeros_like(dg_ref)

        @pl.when(j == nj - 1)
        def _():
            x = h_ref[...]
            r = lax.rsqrt(jnp.mean(x * x, axis=-1, keepdims=True) + EPS)
            dxn = dxn_s[...]
            dyg = dxn * g_ref[...]
            dh = do_ref[...] + r * dyg - x * (r * r * r) * jnp.mean(dyg * x, axis=-1, keepdims=True)
            dh_ref[...] = dh
            dhb_ref[...] = dh.astype(BF16)
            dg_ref[...] += jnp.sum(dxn * x * r, axis=0, keepdims=True)

    row_d = pl.BlockSpec((tb, d), lambda i, j: (i, 0))
    slab = pl.BlockSpec((1, tb, fc), lambda i, j: (j, i, 0))
    hidden = jax.ShapeDtypeStruct((nj, rows, fc), BF16)
    return pl.pallas_call(
        body, name=name, grid=(rows // tb, nj),
        in_specs=[row_d, pl.BlockSpec((1, d), lambda i, j: (0, 0)), row_d] + _ffn_specs(d, fc, nj),
        out_specs=[row_d, row_d, pl.BlockSpec((1, d), lambda i, j: (0, 0)), row_d, slab, slab, slab, row_d],
        out_shape=[jax.ShapeDtypeStruct((rows, d), F32), jax.ShapeDtypeStruct((rows, d), BF16),
                   jax.ShapeDtypeStruct((1, d), F32), jax.ShapeDtypeStruct((rows, d), BF16),
                   hidden, hidden, hidden, jax.ShapeDtypeStruct((rows, d), BF16)],
        scratch_shapes=[pltpu.VMEM((tb, d), F32)],
        compiler_params=_cp("arbitrary", "arbitrary"),
    )(h, gain, dout, wgu, wgu, wd)


def _shift_rows(x, s):
    return pltpu.roll(x, s % x.shape[0], 0)


def _a_post(c, which):
    s = _silu(c)
    n = s * lax.rsqrt(jnp.sum(s * s, axis=-1, keepdims=True) + 1e-6)
    scale = jnp.where(which == 0, HEAD_A ** -0.5, 1.0)
    return jnp.where(which == 2, s, n * scale)


def _conv(x, w):
    return x * w[3:4] + _shift_rows(x, 1) * w[2:3] + _shift_rows(x, 2) * w[1:2] + _shift_rows(x, 3) * w[0:1]


def _a_pre_fwd(zqkv, conv_w):
    rows, width = zqkv.shape
    heads = width // (3 * HEAD_A)

    def body(x_ref, w_ref, o_ref):
        which = pl.program_id(0) // heads
        live = lax.broadcasted_iota(jnp.int32, (rows, HEAD_A), 0) >= PAD
        o_ref[...] = jnp.where(live, _a_post(_conv(x_ref[...], w_ref[...]), which), 0.0)

    return pl.pallas_call(
        body, name="a_pre_fwd", grid=(width // HEAD_A,),
        in_specs=[pl.BlockSpec((rows, HEAD_A), lambda c: (0, c)), pl.BlockSpec((4, HEAD_A), lambda c: (0, c))],
        out_specs=pl.BlockSpec((rows, HEAD_A), lambda c: (0, c)),
        out_shape=jax.ShapeDtypeStruct((rows, width), F32),
        compiler_params=_cp("parallel"),
    )(zqkv, conv_w)


def _a_pre_bwd(zqkv, conv_w, dqkv):
    rows, width = zqkv.shape
    heads = width // (3 * HEAD_A)

    def body(x_ref, w_ref, ct_ref, dx_ref, dw_ref):
        which = pl.program_id(0) // heads
        live = lax.broadcasted_iota(jnp.int32, (rows, HEAD_A), 0) >= PAD
        x, w = x_ref[...], w_ref[...]
        _, vjp = jax.vjp(lambda c: _a_post(c, which), _conv(x, w))
        (dc,) = vjp(jnp.where(live, ct_ref[...], 0.0))
        dc = jnp.where(live, dc, 0.0)
        dx_ref[...] = (dc * w[3:4] + _shift_rows(dc, -1) * w[2:3] + _shift_rows(dc, -2) * w[1:2]
                       + _shift_rows(dc, -3) * w[0:1]).astype(BF16)
        dw_ref[...] = jnp.concatenate(
            [jnp.sum(dc * (_shift_rows(x, 3 - j) if j < 3 else x), axis=0, keepdims=True) for j in range(4)], axis=0)

    col = pl.BlockSpec((rows, HEAD_A), lambda c: (0, c))
    wsp = pl.BlockSpec((4, HEAD_A), lambda c: (0, c))
    return pl.pallas_call(
        body, name="a_pre_bwd", grid=(width // HEAD_A,),
        in_specs=[col, wsp, col], out_specs=[col, wsp],
        out_shape=[jax.ShapeDtypeStruct((rows, width), BF16), jax.ShapeDtypeStruct((4, width), F32)],
        compiler_params=_cp("parallel"),
    )(zqkv, conv_w, dqkv)


SHIFT_TILE = 384


def _shift_fwd(zb, mu):
    rows, width = zb.shape

    def body(x_ref, mu_ref, o_ref):
        x = x_ref[...]
        first = lax.broadcasted_iota(jnp.int32, x.shape, 0) == 0
        prev = jnp.where(first, 0.0, _shift_rows(x, 1))
        o_ref[...] = x + (prev - x) * mu_ref[...]

    col = pl.BlockSpec((rows, SHIFT_TILE), lambda c: (0, c))
    return pl.pallas_call(
        body, name="shift_fwd", grid=(width // SHIFT_TILE,),
        in_specs=[col, pl.BlockSpec((1, SHIFT_TILE), lambda c: (0, c))], out_specs=col,
        out_shape=jax.ShapeDtypeStruct((rows, width), F32), compiler_params=_cp("parallel"),
    )(zb, mu)


def _shift_bwd(zb, mu, dzf):
    rows, width = zb.shape

    def body(x_ref, mu_ref, ct_ref, dx_ref, dmu_ref):
        x, ct, mu_v = x_ref[...], ct_ref[...], mu_ref[...]
        row = lax.broadcasted_iota(jnp.int32, x.shape, 0)
        prev = jnp.where(row == 0, 0.0, _shift_rows(x, 1))
        nxt = jnp.where(row == rows - 1, 0.0, _shift_rows(ct, -1))
        dx_ref[...] = (ct * (1.0 - mu_v) + nxt * mu_v).astype(BF16)
        dmu_ref[...] = jnp.sum(ct * (prev - x), axis=0, keepdims=True)

    col = pl.BlockSpec((rows, SHIFT_TILE), lambda c: (0, c))
    msp = pl.BlockSpec((1, SHIFT_TILE), lambda c: (0, c))
    return pl.pallas_call(
        body, name="shift_bwd", grid=(width // SHIFT_TILE,),
        in_specs=[col, msp, col], out_specs=[col, msp],
        out_shape=[jax.ShapeDtypeStruct((rows, width), BF16), jax.ShapeDtypeStruct((1, width), F32)],
        compiler_params=_cp("parallel"),
    )(zb, mu, dzf)


def _dn_chunk(q, k, v, beta, g, state):
    heads = range(len(q))
    ri = lax.broadcasted_iota(jnp.int32, (CHUNK, CHUNK), 0)
    ci = lax.broadcasted_iota(jnp.int32, (CHUNK, CHUNK), 1)
    eye = (ri == ci).astype(F32)
    incl = ri >= ci
    last = lax.broadcasted_iota(jnp.int32, (CHUNK, 1), 0) == CHUNK - 1
    g_row = [jnp.sum(g[h] * eye, axis=0, keepdims=True) for h in heads]
    gc = [jnp.sum(jnp.where(incl, g_row[h], 0.0), axis=1, keepdims=True) for h in heads]
    gc_row = [jnp.sum(gc[h] * eye, axis=0, keepdims=True) for h in heads]
    decay = [jnp.where(incl, jnp.exp(jnp.where(incl, gc[h] - gc_row[h], 0.0)), 0.0) for h in heads]
    kb = [k[h] * beta[h] for h in heads]
    vb = [v[h] * beta[h] for h in heads]
    p = [-jnp.where(ri > ci, _dot_nt(kb[h], k[h], DN_PRECISION) * decay[h], 0.0) for h in heads]
    tinv = [eye + p[h] for h in heads]
    for _ in range(5):
        p = [_dot(p[h], p[h], DN_PRECISION) for h in heads]
        tinv = [tinv[h] + _dot(tinv[h], p[h], DN_PRECISION) for h in heads]
    eg = [jnp.exp(gc[h]) for h in heads]
    u = [_dot(tinv[h], vb[h], DN_PRECISION) for h in heads]
    wk = [_dot(tinv[h], kb[h] * eg[h], DN_PRECISION) for h in heads]
    attn = [_dot_nt(q[h], k[h]) * decay[h] for h in heads]
    g_last = [jnp.sum(jnp.where(last, gc[h], 0.0), axis=0, keepdims=True) for h in heads]
    k_tail = [k[h] * jnp.exp(g_last[h] - gc[h]) for h in heads]
    v_new = [u[h] - _dot(wk[h], state[h]) for h in heads]
    o = [_dot(q[h] * eg[h], state[h]) + _dot(attn[h], v_new[h]) for h in heads]
    new = [state[h] * jnp.exp(g_last[h]) + _dot_tn(k_tail[h], v_new[h]) for h in heads]
    return o, new


def _bg_cols(bg, h, heads):
    lane = lax.broadcasted_iota(jnp.int32, bg.shape, 1)
    beta = jnp.sum(jnp.where(lane == h, bg, 0.0), axis=1, keepdims=True)
    g = jnp.sum(jnp.where(lane == heads + h, bg, 0.0), axis=1, keepdims=True)
    return beta, g


def _dn_fwd(qkv, bg):
    rows = qkv.shape[0]
    heads = qkv.shape[1] // (3 * HEAD_A)
    n = rows // CHUNK
    hp, groups = heads, 1

    def body(q_ref, k_ref, v_ref, bg_ref, o_ref, hist_ref, s_ref):
        c, grp = pl.program_id(0), pl.program_id(1)

        @pl.when(c == 0)
        def _():
            for i in range(hp):
                s_ref[grp * hp + i] = jnp.zeros((HEAD_A, HEAD_A), F32)

        bg_v = bg_ref[...]
        cols = [slice(i * HEAD_A, (i + 1) * HEAD_A) for i in range(hp)]
        state = [s_ref[grp * hp + i] for i in range(hp)]
        beta_g = [_bg_cols(bg_v, grp * hp + i, heads) for i in range(hp)]
        o, new = _dn_chunk([q_ref[:, c_] for c_ in cols], [k_ref[:, c_] for c_ in cols], [v_ref[:, c_] for c_ in cols],
                           [b for b, _ in beta_g], [g for _, g in beta_g], state)
        for i in range(hp):
            hist_ref[0, i] = state[i]
            o_ref[:, cols[i]] = o[i]
            s_ref[grp * hp + i] = new[i]

    def part(p):
        return pl.BlockSpec((CHUNK, hp * HEAD_A), lambda c, grp: (c, p * groups + grp))

    return pl.pallas_call(
        body, name="deltanet_fwd", grid=(n, groups),
        in_specs=[part(0), part(1), part(2), pl.BlockSpec((CHUNK, 128), lambda c, grp: (c, 0))],
        out_specs=[part(0), pl.BlockSpec((1, hp, HEAD_A, HEAD_A), lambda c, grp: (c, grp, 0, 0))],
        out_shape=[jax.ShapeDtypeStruct((rows, heads * HEAD_A), F32),
                   jax.ShapeDtypeStruct((n, heads, HEAD_A, HEAD_A), F32)],
        scratch_shapes=[pltpu.VMEM((heads, HEAD_A, HEAD_A), F32)],
        compiler_params=_cp("arbitrary", "arbitrary"),
    )(qkv, qkv, qkv, bg)


def _dn_bwd(qkv, bg, hist, do):
    rows = qkv.shape[0]
    heads = qkv.shape[1] // (3 * HEAD_A)
    n = rows // CHUNK
    hp, groups = heads, 1

    def body(q_ref, k_ref, v_ref, bg_ref, hist_ref, do_ref, dqkv_ref, dbg_ref, ds_ref):
        c, grp = pl.program_id(0), pl.program_id(1)

        @pl.when(c == 0)
        def _():
            for i in range(hp):
                ds_ref[grp * hp + i] = jnp.zeros((HEAD_A, HEAD_A), F32)

        bg_v = bg_ref[...]
        lane = lax.broadcasted_iota(jnp.int32, (CHUNK, 128), 1)
        cols = [slice(i * HEAD_A, (i + 1) * HEAD_A) for i in range(hp)]
        beta_g = [_bg_cols(bg_v, grp * hp + i, heads) for i in range(hp)]
        _, vjp = jax.vjp(_dn_chunk, [q_ref[:, c_] for c_ in cols], [k_ref[:, c_] for c_ in cols],
                         [v_ref[:, c_] for c_ in cols], [b for b, _ in beta_g], [g for _, g in beta_g],
                         [hist_ref[0, i] for i in range(hp)])
        dq, dk, dv, dbeta, dg, ds = vjp(([do_ref[:, c_] for c_ in cols], [ds_ref[grp * hp + i] for i in range(hp)]))
        dbg = jnp.zeros((CHUNK, 128), F32)
        for i in range(hp):
            h = grp * hp + i
            for p, part_grad in enumerate((dq, dk, dv)):
                dqkv_ref[:, pl.ds((p * heads + i) * HEAD_A, HEAD_A)] = part_grad[i]
            ds_ref[h] = ds[i]
            dbg = dbg + jnp.where(lane == h, dbeta[i], 0.0) + jnp.where(lane == heads + h, dg[i], 0.0)

        @pl.when(grp == 0)
        def _():
            dbg_ref[...] = jnp.zeros_like(dbg_ref)

        dbg_ref[...] += dbg

    def part(p):
        return pl.BlockSpec((CHUNK, hp * HEAD_A), lambda c, grp: (n - 1 - c, p * groups + grp))

    return pl.pallas_call(
        body, name="deltanet_bwd", grid=(n, groups),
        in_specs=[part(0), part(1), part(2), pl.BlockSpec((CHUNK, 128), lambda c, grp: (n - 1 - c, 0)),
                  pl.BlockSpec((1, hp, HEAD_A, HEAD_A), lambda c, grp: (n - 1 - c, grp, 0, 0)), part(0)],
        out_specs=[pl.BlockSpec((CHUNK, 3 * heads * HEAD_A), lambda c, grp: (n - 1 - c, 0)),
                   pl.BlockSpec((CHUNK, 128), lambda c, grp: (n - 1 - c, 0))],
        out_shape=[jax.ShapeDtypeStruct(qkv.shape, F32), jax.ShapeDtypeStruct((rows, 128), F32)],
        scratch_shapes=[pltpu.VMEM((heads, HEAD_A, HEAD_A), F32)],
        compiler_params=_cp("arbitrary", "arbitrary"),
    )(qkv, qkv, qkv, bg, hist, do)


def _head_mask(heads, width):
    return (lax.broadcasted_iota(jnp.int32, (heads, width), 0)
            == lax.broadcasted_iota(jnp.int32, (heads, width), 1) // HEAD_B)


def _masked_rows(mask, row):
    return jnp.where(mask, row, 0.0).astype(BF16)


def _rwkv_fwd(r, w, k, v, a, b):
    rows, width = r.shape
    heads = width // HEAD_B
    ts = SCAN_STEPS

    def body(r_ref, w_ref, k_ref, v_ref, a_ref, b_ref, y_ref, hist_ref, s_ref):
        @pl.when(pl.program_id(0) == 0)
        def _():
            s_ref[...] = jnp.zeros_like(s_ref)

        mask = _head_mask(heads, width)
        onehot = mask.astype(BF16)
        onehot2 = jnp.concatenate([onehot, onehot], axis=0)
        bd = _block_diag_ones()

        spread_v = [_dot_tn(jnp.concatenate(_hi_lo(v_ref[j]), axis=0), onehot2) for j in range(ts)]
        s = s_ref[...]
        ys = []
        for j in range(ts):
            row = pl.ds(j, 1)
            hist_ref[j] = s
            (sa,) = _segsum_many([((s * a_ref[row, :]).astype(BF16),)], bd)
            s = s * w_ref[row, :] + sa * b_ref[row, :] + spread_v[j] * k_ref[row, :]
            ys.append(_dot_nt(_masked_rows(mask, r_ref[row, :]), s.astype(BF16)))
        for j in range(ts):
            y_ref[j] = ys[j]
        s_ref[...] = s

    blk = pl.BlockSpec((ts, width), lambda i: (i, 0))
    blk3 = pl.BlockSpec((ts, heads, HEAD_B), lambda i: (i, 0, 0))
    return pl.pallas_call(
        body, name="rwkv_fwd", grid=(rows // ts,),
        in_specs=[blk, blk, blk, blk3, blk, blk],
        out_specs=[blk3, pl.BlockSpec((ts, HEAD_B, width), lambda i: (i, 0, 0)),
                   pl.BlockSpec((HEAD_B, width), lambda i: (0, 0))],
        out_shape=[jax.ShapeDtypeStruct((rows, heads, HEAD_B), F32), jax.ShapeDtypeStruct((rows, HEAD_B, width), F32),
                   jax.ShapeDtypeStruct((HEAD_B, width), F32)],
        compiler_params=_cp("arbitrary"),
    )(r, w, k, v, a, b)


def _rwkv_bwd(r, w, k, v, a, b, hist, last, dy):
    rows, width = r.shape
    heads = width // HEAD_B
    ts = SCAN_STEPS
    nb = rows // ts

    def body(r_ref, w_ref, k_ref, v_ref, a_ref, b_ref, hist_ref, last_ref, dy_ref,
             dr_ref, dw_ref, dk_ref, dv_ref, da_ref, db_ref, g_ref, after_ref):
        @pl.when(pl.program_id(0) == 0)
        def _():
            g_ref[...] = jnp.zeros_like(g_ref)
            after_ref[...] = last_ref[...]

        mask = _head_mask(heads, width)
        onehot = mask.astype(BF16)
        bd = _block_diag_ones()

        def own_lanes(x):
            return jnp.sum(jnp.where(mask, x, 0.0), axis=0, keepdims=True)

        def colsum(x):
            return jnp.sum(x, axis=0, keepdims=True)

        dy_m = [dy_ref[j].astype(BF16) for j in range(ts)]
        spread_dy = [_dot_tn(dy_m[j], onehot) for j in range(ts)]
        state_after = [hist_ref[j + 1] if j < ts - 1 else after_ref[...] for j in range(ts)]
        dr = [own_lanes(_dot(dy_m[j], state_after[j].astype(BF16))) for j in range(ts)]
        sa_m = [_dot_nt(_masked_rows(mask, a_ref[pl.ds(j, 1), :]), hist_ref[j].astype(BF16)) for j in range(ts)]
        g = g_ref[...]
        dw, dk, db, da, dv = {}, {}, {}, {}, {}
        for j in reversed(range(ts)):
            row = pl.ds(j, 1)
            sp = hist_ref[j]
            g = g + spread_dy[j] * r_ref[row, :]
            (dsa,) = _segsum_many([((g * b_ref[row, :]).astype(BF16),)], bd)
            g_b = g.astype(BF16)
            both = _dot(jnp.concatenate([v_ref[j].astype(BF16), sa_m[j].astype(BF16)], axis=0), g_b)
            dk[j], db[j] = own_lanes(both[:heads]), own_lanes(both[heads:])
            dv[j] = _dot_nt(_masked_rows(mask, k_ref[row, :]), g_b)
            dw[j] = colsum(g * sp)
            da[j] = colsum(sp * dsa)
            g = g * w_ref[row, :] + dsa * a_ref[row, :]
        g_ref[...] = g
        after_ref[...] = hist_ref[0]
        for j in range(ts):
            dv_ref[j] = dv[j]
            for ref, vals in ((dr_ref, dr), (dw_ref, dw), (dk_ref, dk), (da_ref, da), (db_ref, db)):
                ref[pl.ds(j, 1), :] = vals[j]

    blk = pl.BlockSpec((ts, width), lambda i: (nb - 1 - i, 0))
    blk3 = pl.BlockSpec((ts, heads, HEAD_B), lambda i: (nb - 1 - i, 0, 0))
    state = pl.BlockSpec((HEAD_B, width), lambda i: (0, 0))
    return pl.pallas_call(
        body, name="rwkv_bwd", grid=(nb,),
        in_specs=[blk, blk, blk, blk3, blk, blk, pl.BlockSpec((ts, HEAD_B, width), lambda i: (nb - 1 - i, 0, 0)),
                  state, blk3],
        out_specs=[blk, blk, blk, blk3, blk, blk],
        out_shape=[jax.ShapeDtypeStruct((rows, width), F32)] * 3 + [jax.ShapeDtypeStruct((rows, heads, HEAD_B), F32)]
        + [jax.ShapeDtypeStruct((rows, width), F32)] * 2,
        scratch_shapes=[pltpu.VMEM((HEAD_B, width), F32), pltpu.VMEM((HEAD_B, width), F32)],
        compiler_params=_cp("arbitrary"),
    )(r, w, k, v, a, b, hist, last, dy)


def _live(row0, shape):
    return (row0 + lax.broadcasted_iota(jnp.int32, shape, 0)) >= PAD


def _norm_fn(row0, h, gain):
    return (_rms(h, gain),)


def _norm_res_fn(row0, h, gain):
    return _rms(h, gain), h


def _make_bg_fn(heads):
    def fn(row0, x, log_rate, dt_bias):
        lane = lax.broadcasted_iota(jnp.int32, x.shape, 1)
        beta = _sigmoid(x)
        g = -jnp.exp(log_rate) * _softplus(x + dt_bias)
        out = jnp.where(lane < heads, beta, jnp.where(lane < 2 * heads, g, 0.0))
        return (jnp.where(_live(row0, x.shape), out, 0.0),)
    return fn


def _b_pre_fn(row0, zf, w0, w_up, a0, a_up, g_up, k_k, k_a):
    d = w0.shape[1]
    r, k, v = zf[:, :d], zf[:, d:2 * d], zf[:, 2 * d:3 * d]
    lo = zf[:, 3 * d:3 * d + 128]
    lg = zf[:, 3 * d + 128:3 * d + LORA_PAD]
    lane = lax.broadcasted_iota(jnp.int32, lo.shape, 1)
    lw = _dot(jnp.where(lane < LORA_W, jnp.tanh(lo), 0.0), w_up)
    la = _dot(jnp.where(lane >= LORA_W, lo, 0.0), a_up)
    lane_g = lax.broadcasted_iota(jnp.int32, lg.shape, 1)
    gate = _dot(jnp.where(lane_g < LORA_G, _sigmoid(lg), 0.0), g_up)
    decay = jnp.exp(-jnp.exp(-_softplus(-(w0 + lw)) - 0.5))
    a = _sigmoid(a0 + la)
    kx = k * k_k
    kk = kx * lax.rsqrt(_segsum64(kx * kx) + 1e-6)
    k2 = k * (1.0 + (a - 1.0) * k_a)
    return r, decay, k2, v, -kk, kk * a, gate


def _post_fn(row0, o, zg, y, r, k2, v, gate, out_gain, r_k, ln_g, ln_b):
    d = o.shape[1]
    az, ga, gb = zg[:, :d], zg[:, d:2 * d], zg[:, 2 * d:]
    heads = d // HEAD_A
    parts = []
    for h in range(heads):
        oh = o[:, h * HEAD_A:(h + 1) * HEAD_A]
        parts.append(oh * lax.rsqrt(jnp.mean(oh * oh, axis=-1, keepdims=True) + EPS) * out_gain)
    o_a = jnp.concatenate(parts, axis=1) * _silu(az)
    mean = _segsum64(y) * (1.0 / HEAD_B)
    yc = y - mean
    var = _segsum64(yc * yc) * (1.0 / HEAD_B)
    yn = yc * lax.rsqrt(var + GN_EPS) * ln_g + ln_b
    o_b = (yn + _segsum64(r * k2 * r_k) * v) * gate
    return (_sigmoid(ga) * o_a + _sigmoid(gb) * o_b,)


def _loss(h3, target, gain, tb):
    rows, d = h3.shape

    def body(h_ref, t_ref, g_ref, dh_ref, dg_ref, l_ref):
        i = pl.program_id(0)
        live = (i * tb + lax.broadcasted_iota(jnp.int32, (tb, 1), 0)) >= CHUNK
        tgt = t_ref[...]

        def f(h, g):
            err = _rms(h, g) - tgt
            return 0.5 * jnp.sum(jnp.where(live, jnp.mean(err * err, axis=-1, keepdims=True), 0.0))

        val, vjp = jax.vjp(f, h_ref[...], g_ref[...])
        dh, dg = vjp(jnp.ones((), F32))
        dh_ref[...] = dh

        @pl.when(i == 0)
        def _():
            dg_ref[...] = jnp.zeros_like(dg_ref)
            l_ref[...] = jnp.zeros_like(l_ref)

        dg_ref[...] += dg
        l_ref[...] += jnp.full((1, 128), val, F32)

    blk = pl.BlockSpec((tb, d), lambda i: (i, 0))
    return pl.pallas_call(
        body, name="loss", grid=(rows // tb,),
        in_specs=[blk, blk, pl.BlockSpec((1, d), lambda i: (0, 0))],
        out_specs=[blk, pl.BlockSpec((1, d), lambda i: (0, 0)), pl.BlockSpec((1, 128), lambda i: (0, 0))],
        out_shape=[jax.ShapeDtypeStruct((rows, d), F32), jax.ShapeDtypeStruct((1, d), F32),
                   jax.ShapeDtypeStruct((1, 128), F32)],
        compiler_params=_cp("arbitrary"),
    )(h3, target, gain)


def _adamw_math(w, g, m, v):
    m2 = ADAM_B1 * m + (1.0 - ADAM_B1) * g
    v2 = ADAM_B2 * v + (1.0 - ADAM_B2) * (g * g)
    m_hat = m2 / (1.0 - ADAM_B1 ** ADAM_STEP)
    v_hat = v2 / (1.0 - ADAM_B2 ** ADAM_STEP)
    return -ADAM_LR * (m_hat / (jnp.sqrt(v_hat) + ADAM_EPS) + ADAM_WD * w), m2, v2


def _adamw(name, own, landed, w, m, v):
    rows, cols = w.shape
    if rows % 16 == 0:
        rb = _tb(rows, 128)
        grid, blk = (rows // rb,), pl.BlockSpec((rb, cols), lambda i: (i, 0))
        landed_blk = pl.BlockSpec((N_DEV - 1, rb, cols), lambda i: (0, i, 0))
    else:
        grid, blk = (cols // 128,), pl.BlockSpec((rows, 128), lambda i: (0, i))
        landed_blk = pl.BlockSpec((N_DEV - 1, rows, 128), lambda i: (0, 0, i))

    def body(o_ref, s_ref, w_ref, m_ref, v_ref, g_ref, d_ref, m2_ref, v2_ref):
        g = o_ref[...].astype(F32)
        for peer in range(N_DEV - 1):
            g = g + s_ref[peer].astype(F32)
        g_ref[...] = g
        d_ref[...], m2_ref[...], v2_ref[...] = _adamw_math(w_ref[...], g, m_ref[...], v_ref[...])

    return pl.pallas_call(
        body, name=name, grid=grid, in_specs=[blk, landed_blk, blk, blk, blk],
        out_specs=[blk] * 4, out_shape=[jax.ShapeDtypeStruct((rows, cols), F32)] * 4,
        compiler_params=_cp("parallel"),
    )(own, landed, w, m, v)


def _sum_slabs(name, slabs, rb):
    _, rows, cols = slabs.shape

    def body(s_ref, o_ref):
        g = s_ref[0]
        for dev in range(1, N_DEV):
            g = g + s_ref[dev]
        o_ref[...] = g

    return pl.pallas_call(
        body, name=name, grid=(rows // rb,),
        in_specs=[pl.BlockSpec((N_DEV, rb, cols), lambda i: (0, i, 0))],
        out_specs=pl.BlockSpec((rb, cols), lambda i: (i, 0)),
        out_shape=jax.ShapeDtypeStruct((rows, cols), F32), compiler_params=_cp("parallel"),
    )(slabs)


def _adamw_small(w, g, m, v):
    def body(w_ref, g_ref, m_ref, v_ref, d_ref, m2_ref, v2_ref):
        d_ref[...], m2_ref[...], v2_ref[...] = _adamw_math(w_ref[...], g_ref[...], m_ref[...], v_ref[...])

    return pl.pallas_call(body, name="adamw_small", out_shape=[jax.ShapeDtypeStruct(w.shape, F32)] * 3)(w, g, m, v)


def _place():
    return lax.axis_index("x"), lax.axis_index("y"), lax.axis_index("c")


def _index(p):
    return 4 * p[0] + 2 * p[1] + p[2]


def _all_gather(name, xs):
    n = len(xs)

    def body(*refs):
        x_refs, o_refs = refs[:n], refs[n:2 * n]
        send_sems, recv_sems, local_sems = refs[2 * n:]
        x, y, c = _place()
        me, sibling = (x, y, c), (x, y, 1 - c)
        chips = [(1 - x, y), (x, 1 - y), (1 - x, 1 - y)]

        def copy(i, k, block, to, src=None):
            dst = o_refs[i].at[_index(block)]
            return pltpu.make_async_remote_copy(src_ref=dst if src is None else src, dst_ref=dst,
                                                send_sem=send_sems.at[i, k], recv_sem=recv_sems.at[i, k],
                                                device_id=to, device_id_type=MESH_ID)

        mine = [pltpu.make_async_copy(x_refs[i], o_refs[i].at[_index(me)], local_sems.at[i]) for i in range(n)]
        for cp in mine:
            cp.start()
        first = []
        for i in range(n):
            first.append(copy(i, 0, me, sibling, src=x_refs[i]))
            first += [copy(i, 1 + j, me, (*chip, c), src=x_refs[i]) for j, chip in enumerate(chips)]
        for cp in first:
            cp.start()
        passed = []
        for j, chip in enumerate(chips):
            for i in range(n):
                copy(i, 1 + j, (*chip, c), me).wait_recv()
                cp = copy(i, 4 + j, (*chip, c), sibling)
                cp.start()
                passed.append(cp)
        for i in range(n):
            copy(i, 0, sibling, me).wait_recv()
            for j, chip in enumerate(chips):
                copy(i, 4 + j, (*chip, 1 - c), me).wait_recv()
        for cp in first + passed:
            cp.wait_send()
        for cp in mine:
            cp.wait()

    return pl.pallas_call(
        body, name=name, in_specs=[ANY] * n, out_specs=[ANY] * n,
        out_shape=[jax.ShapeDtypeStruct((N_DEV,) + x.shape, x.dtype) for x in xs],
        scratch_shapes=[pltpu.SemaphoreType.DMA((n, 7)), pltpu.SemaphoreType.DMA((n, 7)), pltpu.SemaphoreType.DMA((n,))],
    )(*xs)


def _exchange_start(name, xs, after=None, gather=False):
    n = len(xs)
    copies = n * (N_DEV - 1)
    extra = [] if after is None else [after]

    def body(*refs):
        x_refs, land_refs = refs[:n], refs[n:2 * n]
        sems = refs[2 * n + len(extra):2 * n + len(extra) + 2 * copies]
        token = refs[-1]
        for i, k, peer in _exchange_copies(n):
            _exchange_copy(x_refs, land_refs, sems, i, k, peer, gather).start()
        token[...] = jnp.zeros_like(token)

    lands = [lax.empty((N_DEV,) + x.shape if gather else (N_DEV - 1,) + x.shape[1:], x.dtype) for x in xs]
    out = pl.pallas_call(
        body, name=name,
        out_shape=(*[pltpu.SemaphoreType.DMA(())] * (2 * copies), *[pltpu.HBM(x.shape, x.dtype) for x in xs],
                   *[pltpu.HBM(l.shape, l.dtype) for l in lands], jax.ShapeDtypeStruct((8, 128), F32)),
        in_specs=[HBM_SPEC] * (2 * n) + [ANY] * len(extra),
        out_specs=(*[SEM_SPEC] * (2 * copies), *[HBM_SPEC] * (2 * n), pl.BlockSpec(memory_space=pltpu.VMEM)),
        input_output_aliases={i: 2 * copies + i for i in range(2 * n)},
        compiler_params=pltpu.CompilerParams(has_side_effects=pltpu.SideEffectType.DATAFLOW_SIDE_EFFECTING),
    )(*[pltpu.with_memory_space_constraint(a, pltpu.HBM) for a in list(xs) + lands], *extra)
    sems, rest = list(out[:2 * copies]), out[2 * copies:]
    return sems, list(rest[:n]), list(rest[n:2 * n]), rest[-1]


def _exchange_copies(n):
    x, y, c = _place()
    for k in range(1, N_DEV):
        peer = ((1 - x) if k & 4 else x, (1 - y) if k & 2 else y, (1 - c) if k & 1 else c)
        for i in range(n):
            yield i, k - 1, peer


def _exchange_copy(x_refs, land_refs, sems, i, k, peer, gather, arriving=False):
    copies = len(sems) // 2
    which = i * (N_DEV - 1) + k
    src = x_refs[i] if gather else x_refs[i].at[_index(peer)]
    dst = land_refs[i].at[_index(peer if arriving else _place())] if gather else land_refs[i].at[k]
    return pltpu.make_async_remote_copy(src_ref=src, dst_ref=dst, send_sem=sems[which], recv_sem=sems[copies + which],
                                        device_id=peer, device_id_type=MESH_ID)


def _exchange_wait(name, sems, xs, lands, after, gather=False):
    n = len(xs)

    def body(*refs):
        x_refs, land_refs = refs[:n], refs[n:2 * n]
        sem_refs = refs[2 * n:2 * n + len(sems)]
        for i, k, peer in _exchange_copies(n):
            _exchange_copy(x_refs, land_refs, sem_refs, i, k, peer, gather).wait_send()
            _exchange_copy(x_refs, land_refs, sem_refs, i, k, peer, gather, arriving=True).wait_recv()

    out = pl.pallas_call(
        body, name=name,
        out_shape=(*[pltpu.HBM(x.shape, x.dtype) for x in xs], *[pltpu.HBM(l.shape, l.dtype) for l in lands]),
        in_specs=[HBM_SPEC] * (2 * n) + [SEM_SPEC] * len(sems) + [ANY], out_specs=tuple([HBM_SPEC] * (2 * n)),
        input_output_aliases={i: i for i in range(2 * n)},
        compiler_params=pltpu.CompilerParams(has_side_effects=pltpu.SideEffectType.DATAFLOW_SIDE_EFFECTING),
    )(*xs, *lands, *sems, after)
    return list(out[:n]), list(out[n:])


def _pack(arrays):
    flat = jnp.concatenate([a.reshape(-1) for a in arrays])
    pad = (-flat.shape[0]) % 1024
    return jnp.pad(flat, (0, pad)).reshape(-1, 128)


def _unpack(packed, shapes):
    flat = packed.reshape(-1)
    out, pos = [], 0
    for s in shapes:
        size = 1
        for dim in s:
            size *= dim
        out.append(flat[pos:pos + size].reshape(s))
        pos += size
    return out


def _cols_from_slabs(stack):
    return jnp.transpose(stack, (1, 0, 2)).reshape(stack.shape[1], -1)


def kernel(x, meta_tokens, ffn1_norm, ffn1_w_gu, ffn1_w_down, mix_norm, w_in, a_conv_w, a_log_rate, a_dt_bias, a_out_norm, b_shift_mu, b_w0, b_w_up, b_a0, b_a_up, b_g_up, b_k_k, b_k_a, b_r_k, b_ln_gain, b_ln_bias, w_out, ffn2_norm, ffn2_w_gu, ffn2_w_down, final_norm, loss_target, m_meta_tokens, m_ffn1_norm, m_ffn1_w_gu, m_ffn1_w_down, m_mix_norm, m_w_in, m_a_conv_w, m_a_log_rate, m_a_dt_bias, m_a_out_norm, m_b_shift_mu, m_b_w0, m_b_w_up, m_b_a0, m_b_a_up, m_b_g_up, m_b_k_k, m_b_k_a, m_b_r_k, m_b_ln_gain, m_b_ln_bias, m_w_out, m_ffn2_norm, m_ffn2_w_gu, m_ffn2_w_down, m_final_norm, v_meta_tokens, v_ffn1_norm, v_ffn1_w_gu, v_ffn1_w_down, v_mix_norm, v_w_in, v_a_conv_w, v_a_log_rate, v_a_dt_bias, v_a_out_norm, v_b_shift_mu, v_b_w0, v_b_w_up, v_b_a0, v_b_a_up, v_b_g_up, v_b_k_k, v_b_k_a, v_b_r_k, v_b_ln_gain, v_b_ln_bias, v_w_out, v_ffn2_norm, v_ffn2_w_gu, v_ffn2_w_down, v_final_norm):
    names = ['meta_tokens', 'ffn1_norm', 'ffn1_w_gu', 'ffn1_w_down', 'mix_norm', 'w_in', 'a_conv_w', 'a_log_rate',
             'a_dt_bias', 'a_out_norm', 'b_shift_mu', 'b_w0', 'b_w_up', 'b_a0', 'b_a_up', 'b_g_up', 'b_k_k', 'b_k_a',
             'b_r_k', 'b_ln_gain', 'b_ln_bias', 'w_out', 'ffn2_norm', 'ffn2_w_gu', 'ffn2_w_down', 'final_norm']
    env = dict(locals())
    wts = {k: env[k] for k in names}
    mom_m = {k: env['m_' + k] for k in names}
    mom_v = {k: env['v_' + k] for k in names}
    big = ['ffn1_w_gu', 'ffn1_w_down', 'w_in', 'w_out', 'ffn2_w_gu', 'ffn2_w_down']
    col_sharded = {'ffn1_w_gu', 'w_in', 'ffn2_w_gu'}
    shard_of = lambda tree, k: tree[k][0].T if k in col_sharded else tree[k][0]
    small_sharded = ['meta_tokens', 'a_conv_w', 'b_w_up', 'b_a_up', 'b_g_up']
    replicated = [k for k in names if k not in big and k not in small_sharded]

    seq, d = x.shape[1], x.shape[2]
    rows = PAD + N_META + seq
    heads_a = d // HEAD_A
    tb_mm = _tb(rows, 416)
    tb_vjp = _tb(rows, 208)
    tb_dw = _tb(rows, 2080)
    me = _index(_place())

    local_bf = {k: shard_of(wts, k).astype(BF16) for k in big}
    gu1, down1, meta = _all_gather("gather_ffn1", [local_bf['ffn1_w_gu'], local_bf['ffn1_w_down'], wts['meta_tokens']])
    small_rest = small_sharded[1:]
    late_keys = ['w_out', 'ffn2_w_gu', 'ffn2_w_down']
    gather_mid = _exchange_start("gather_start_mid", [local_bf['w_in'], _pack([wts[k][0] for k in small_rest])],
                                 after=gu1, gather=True)
    gather_late = _exchange_start("gather_start_late", [local_bf[k] for k in late_keys], after=gather_mid[-1],
                                  gather=True)

    def finish_gather(tag, started, after):
        sems, mine, lands, _ = started
        mine, lands = _exchange_wait("gather_wait_" + tag, sems, mine, lands, after, gather=True)
        return [lax.dynamic_update_index_in_dim(land, own[None], me, 0) for land, own in zip(lands, mine)]

    full = {'ffn1_w_gu': gu1, 'ffn1_w_down': down1.reshape(-1, d), 'meta_tokens': _cols_from_slabs(meta)}
    for k in replicated:
        full[k] = wts[k].reshape(1, -1)

    h0 = jnp.concatenate([jnp.zeros((PAD, d), F32) + gather_late[-1][:1, :1], full['meta_tokens'], x[0]], axis=0)
    h1 = _ffn_fwd("ffn1_fwd", h0, full['ffn1_norm'], full['ffn1_w_gu'], full['ffn1_w_down'], tb_mm)
    (u,) = _tok_fwd("mix_norm_fwd", _norm_fn, [h1], [full['mix_norm']], [(d, BF16)], tb_mm)

    win_stack, small_stack = finish_gather("mid", gather_mid, u)
    full['w_in'] = win_stack.reshape(-1, d)
    small_flat, pos = small_stack.reshape(N_DEV, -1), 0
    for k in small_rest:
        shape = wts[k][0].shape
        full[k] = _cols_from_slabs(small_flat[:, pos:pos + shape[0] * shape[1]].reshape((N_DEV,) + shape))
        pos += shape[0] * shape[1]

    win = full['w_in']
    n_b = 3 * d + LORA_W + LORA_A + LORA_G
    off_beta, off_b = 4 * d, 4 * d + 2 * heads_a
    off_ga = off_b + n_b
    b_width = 3 * d + LORA_PAD
    zrows = lambda r: jnp.zeros((r, d), BF16)
    w_qkv = win[:3 * d]
    w_zg = jnp.concatenate([win[3 * d:4 * d], win[off_ga:off_ga + 2 * d]], axis=0)
    w_b = jnp.concatenate([win[off_b:off_b + n_b], zrows(b_width - n_b)], axis=0)
    w_bg = jnp.concatenate([win[off_beta:off_beta + 2 * heads_a], zrows(128 - 2 * heads_a)], axis=0)

    def lanes(vec, start, width):
        return jnp.pad(vec.reshape(1, -1), ((0, 0), (start, width - start - vec.size)))

    log_rate = lanes(wts['a_log_rate'], heads_a, 128)
    dt_bias = lanes(wts['a_dt_bias'], heads_a, 128)
    mu = lanes(wts['b_shift_mu'], 0, b_width)
    w_up = jnp.pad(full['b_w_up'], ((0, 128 - LORA_W), (0, 0)))
    a_up = jnp.pad(full['b_a_up'], ((LORA_W, 0), (0, 0)))
    g_up = jnp.pad(full['b_g_up'], ((0, 256 - LORA_G), (0, 0)))
    b_pars = [full['b_w0'], w_up, full['b_a0'], a_up, g_up, full['b_k_k'], full['b_k_a']]
    post_pars = [full['a_out_norm'], full['b_r_k'], full['b_ln_gain'], full['b_ln_bias']]
    bg_fn = _make_bg_fn(heads_a)

    z_qkv = _mm("in_qkv", u, w_qkv, trans_b=True, tb=tb_mm, tn=_col_tile(3 * d, 1536))
    z_zg = _mm("in_zg", u, w_zg, trans_b=True, tb=tb_mm, tn=_col_tile(3 * d, 1536))
    z_b = _mm("in_b", u, w_b, trans_b=True, tb=tb_mm, tn=_col_tile(b_width, 1536))
    z_bg = _mm("in_bg", u, w_bg, trans_b=True, tb=tb_mm, tn=128)
    qkv = _a_pre_fwd(z_qkv, full['a_conv_w'])
    (bg,) = _tok_fwd("bg_fwd", bg_fn, [z_bg], [log_rate, dt_bias], [(128, F32)], tb_mm)
    o_dn, dn_hist = _dn_fwd(qkv, bg)
    zf = _shift_fwd(z_b, mu)
    rr, ww, kk2, vv, av, bv, gate = _tok_fwd("b_pre_fwd", _b_pre_fn, [zf], b_pars, [(d, F32)] * 7, tb_vjp)
    per_head = lambda t: t.reshape(rows, d // HEAD_B, HEAD_B)
    y_heads, b_hist, b_last = _rwkv_fwd(rr, ww, kk2, per_head(vv), av, bv)
    y_b = y_heads.reshape(rows, d)
    w_out_stack, full['ffn2_w_gu'], down2 = finish_gather("late", gather_late, y_heads)
    full['w_out'], full['ffn2_w_down'] = w_out_stack.reshape(-1, d), down2.reshape(-1, d)
    post_toks = [o_dn, z_zg, y_b, rr, kk2, vv, gate]
    (merged,) = _tok_fwd("post_fwd", _post_fn, post_toks, post_pars, [(d, BF16)], tb_vjp)
    h2 = _mm("out_proj", merged, full['w_out'], add=h1, tb=tb_mm, tn=d)
    h3 = _ffn_fwd("ffn2_fwd", h2, full['ffn2_norm'], full['ffn2_w_gu'], full['ffn2_w_down'], tb_mm)

    target = jnp.pad(loss_target[0], ((CHUNK, 0), (0, 0)))
    dh3, g_final, loss_part = _loss(h3, target, full['final_norm'].reshape(1, d), tb_vjp)

    def ffn_backward(tag, h, dout, key_norm, key_gu, key_down):
        dh, dh_bf, dgain, xn, act, dgate, dup, dhalf = _ffn_bwd(tag + "_bwd", h, full[key_norm], dout, full[key_gu],
                                                                full[key_down], tb_mm)
        fc = dgate.shape[2]
        d_gu = jnp.concatenate([_mm_tn_from_slabs(tag + "_dw_gate", dgate, xn, tk=tb_dw).reshape(-1, fc, d),
                                _mm_tn_from_slabs(tag + "_dw_up", dup, xn, tk=tb_dw).reshape(-1, fc, d)], axis=0)
        d_down = _mm_tn_from_slabs(tag + "_dw_down", act, dhalf, tk=tb_dw).reshape(N_DEV, -1, d)
        return dh, dh_bf, dgain, d_gu, d_down

    dh2, dh2_bf, g_ffn2_norm, g_ffn2_gu, g_ffn2_down = ffn_backward("ffn2", h2, dh3, 'ffn2_norm', 'ffn2_w_gu',
                                                                    'ffn2_w_down')
    g_w_out = _mm_tn("dw_out", merged, dh2_bf, tm=d, tn=d, tk=tb_dw).reshape(N_DEV, -1, d)

    def start_exchange(tag, keys, slabs, after=None):
        sems, kept, lands, token = _exchange_start("exchange_start_" + tag, slabs, after)
        return (tag, keys, sems, kept, lands), token

    ex_ffn2, token_ffn2 = start_exchange("ffn2", ['ffn2_w_gu', 'ffn2_w_down', 'w_out'], [g_ffn2_gu, g_ffn2_down, g_w_out])
    dmerged = _mm("d_merged", dh2_bf, full['w_out'], trans_b=True, after=token_ffn2, tb=tb_mm, tn=d)
    post_grads = _tok_bwd("post_bwd", _post_fn, post_toks, post_pars, [[dmerged]], list(range(7)), tb_vjp,
                          [F32, BF16] + [F32] * 5)
    do_dn, dz_zg, dy_b, dr1, dk1, dv1, dgate = post_grads[:7]
    g_out_norm, g_r_k, g_ln_g, g_ln_b = post_grads[7:]
    dr2, dw2, dk2, dv_heads, da2, db2 = _rwkv_bwd(rr, ww, kk2, per_head(vv), av, bv, b_hist, b_last, per_head(dy_b))
    dv2 = dv_heads.reshape(rows, d)
    b_grads = _tok_bwd("b_pre_bwd", _b_pre_fn, [zf], b_pars,
                       [[dr1, dr2], [dw2], [dk1, dk2], [dv1, dv2], [da2], [db2], [dgate]], [0], tb_vjp)
    dzf = b_grads[0]
    g_w0, g_w_up, g_a0, g_a_up, g_g_up, g_k_k, g_k_a = b_grads[1:]
    dz_b, g_mu = _shift_bwd(z_b, mu, dzf)
    dqkv, dbg = _dn_bwd(qkv, bg, dn_hist, do_dn)
    dz_qkv, g_conv = _a_pre_bwd(z_qkv, full['a_conv_w'], dqkv)
    dz_bg, g_log_rate, g_dt_bias = _tok_bwd("bg_bwd", bg_fn, [z_bg], [log_rate, dt_bias], [[dbg]], [0], tb_mm, [BF16])

    du = None
    g_w_in_parts = []
    for tag, dz, wpiece in (("qkv", dz_qkv, w_qkv), ("zg", dz_zg, w_zg), ("b", dz_b, w_b), ("bg", dz_bg, w_bg)):
        du = _mm("du_" + tag, dz, wpiece, add=du, tb=tb_mm, tn=d)
        g_w_in_parts.append(_mm_tn("dw_in_" + tag, dz, u, tm=_col_tile(dz.shape[1], 1536), tn=d, tk=tb_dw))
    gp_qkv, gp_zg, gp_b, gp_bg = g_w_in_parts
    g_w_in = jnp.concatenate([gp_qkv, gp_zg[:d], gp_bg[:2 * heads_a], gp_b[:n_b], gp_zg[d:]],
                             axis=0).reshape(N_DEV, -1, d)
    ex_w_in, token_w_in = start_exchange("w_in", ['w_in'], [g_w_in])
    dh1, g_mix_norm = _tok_bwd("mix_norm_bwd", _norm_res_fn, [h1], [full['mix_norm']], [[du], [dh2]], [0], tb_vjp,
                               after=token_w_in)
    dh0, _, g_ffn1_norm, g_ffn1_gu, g_ffn1_down = ffn_backward("ffn1", h0, dh1, 'ffn1_norm', 'ffn1_w_gu', 'ffn1_w_down')

    small_full = {
        'meta_tokens': dh0[PAD:CHUNK], 'ffn1_norm': g_ffn1_norm, 'mix_norm': g_mix_norm, 'a_conv_w': g_conv,
        'a_log_rate': g_log_rate[:, heads_a:2 * heads_a], 'a_dt_bias': g_dt_bias[:, heads_a:2 * heads_a],
        'a_out_norm': g_out_norm, 'b_shift_mu': g_mu[:, :n_b], 'b_w0': g_w0, 'b_w_up': g_w_up[:LORA_W],
        'b_a0': g_a0, 'b_a_up': g_a_up[LORA_W:], 'b_g_up': g_g_up[:LORA_G], 'b_k_k': g_k_k, 'b_k_a': g_k_a,
        'b_r_k': g_r_k, 'b_ln_gain': g_ln_g, 'b_ln_bias': g_ln_b, 'ffn2_norm': g_ffn2_norm, 'final_norm': g_final,
    }
    small_names = list(small_full)
    packed = _pack([small_full[k] for k in small_names] + [loss_part[:, :1]])
    (all_parts,) = _all_gather("gather_small_grads", [packed])
    ex_ffn1, _ = start_exchange("ffn1", ['ffn1_w_gu', 'ffn1_w_down'], [g_ffn1_gu, g_ffn1_down], after=all_parts)
    summed = _sum_slabs("sum_small_grads", all_parts, packed.shape[0])
    pieces = _unpack(summed, [small_full[k].shape for k in small_names] + [(1, 1)])
    small_grad = dict(zip(small_names, pieces[:-1]))
    loss = pieces[-1].reshape(())

    grads, deltas, new_m, new_v = {}, {}, {}, {}
    local_small = {}
    for k in small_names:
        g = small_grad[k]
        if k in small_sharded:
            width = wts[k].shape[-1]
            g = lax.dynamic_slice_in_dim(g, me * width, width, axis=1)
        local_small[k] = g.reshape(wts[k].shape)
    pk = lambda tree: _pack([tree[k] for k in small_names])
    dl_s, m_s, v_s = _adamw_small(pk(wts), pk(local_small), pk(mom_m), pk(mom_v))
    shapes = [wts[k].shape for k in small_names]
    for k, dl, m2, v2 in zip(small_names, _unpack(dl_s, shapes), _unpack(m_s, shapes), _unpack(v_s, shapes)):
        grads[k], deltas[k], new_m[k], new_v[k] = local_small[k], dl, m2, v2

    done = dl_s
    for tag, keys, sems, kept, lands in (ex_ffn2, ex_w_in, ex_ffn1):
        kept, lands = _exchange_wait("exchange_wait_" + tag, sems, kept, lands, done)
        for k, slabs, landed in zip(keys, kept, lands):
            own = lax.dynamic_index_in_dim(slabs, me, axis=0, keepdims=False)
            res = _adamw("adamw_" + k, own, landed, shard_of(wts, k), shard_of(mom_m, k), shard_of(mom_v, k))
            done = res[1]
            res = [(t.T if k in col_sharded else t)[None] for t in res]
            grads[k], deltas[k], new_m[k], new_v[k] = res

    grad_x = dh0[CHUNK:][None]
    return (loss, grad_x, *[grads[k] for k in names], *[deltas[k] for k in names],
            *[new_m[k] for k in names], *[new_v[k] for k in names])
```

```python
import functools

import jax
import jax.numpy as jnp
from jax import lax
from jax.experimental import pallas as pl
from jax.experimental.pallas import tpu as pltpu

F32 = jnp.float32
BF16 = jnp.bfloat16
N_DEV = 8
N_META = 16
CHUNK = 64
PAD = CHUNK - N_META
HEAD_A = 128
HEAD_B = 64
LORA_W, LORA_A, LORA_G = 64, 64, 160
LORA_PAD = 384
EPS = 1e-6
GN_EPS = HEAD_B * 1e-5
ADAM_LR, ADAM_B1, ADAM_B2, ADAM_EPS, ADAM_WD, ADAM_STEP = 0.001, 0.9, 0.999, 1e-08, 0.01, 10
SCAN_STEPS = 16
MXU_WIDTH = 256
VMEM_LIMIT = 56 * 1024 * 1024
DN_PRECISION = lax.Precision.HIGH
MESH_ID = pl.DeviceIdType.MESH
ANY = pl.BlockSpec(memory_space=pl.ANY)
HBM_SPEC = pl.BlockSpec(memory_space=pltpu.HBM)
SEM_SPEC = pl.BlockSpec(memory_space=pltpu.SEMAPHORE)


def _cp(*sem):
    return pltpu.CompilerParams(dimension_semantics=sem, vmem_limit_bytes=VMEM_LIMIT)


def _tb(t, target):
    best = 16
    for d in range(16, target + 1, 16):
        if t % d == 0:
            best = d
    return best


def _sigmoid(x):
    return 1.0 / (1.0 + jnp.exp(-x))


def _silu(x):
    return x * _sigmoid(x)


def _softplus(x):
    return jnp.maximum(x, 0.0) + jnp.log(1.0 + jnp.exp(-jnp.abs(x)))


def _dot_nt(a, b, precision=None):
    return lax.dot_general(a, b, (((1,), (1,)), ((), ())), preferred_element_type=F32, precision=precision)


def _dot_tn(a, b, precision=None):
    return lax.dot_general(a, b, (((0,), (0,)), ((), ())), preferred_element_type=F32, precision=precision)


def _dot(a, b, precision=None):
    return jnp.dot(a, b, preferred_element_type=F32, precision=precision)


def _block_diag_ones():
    i = lax.broadcasted_iota(jnp.int32, (MXU_WIDTH, MXU_WIDTH), 0) // HEAD_B
    j = lax.broadcasted_iota(jnp.int32, (MXU_WIDTH, MXU_WIDTH), 1) // HEAD_B
    return (i == j).astype(BF16)


def _hi_lo(x):
    hi = x.astype(BF16)
    return hi, (x - hi.astype(F32)).astype(BF16)


def _segsum_many(xs, bd):
    groups = [x if isinstance(x, tuple) else _hi_lo(x) for x in xs]
    rows = groups[0][0].shape[0]
    stacked = jnp.concatenate([p for grp in groups for p in grp], axis=0)
    out = jnp.concatenate([_dot(stacked[:, s:s + MXU_WIDTH], bd) for s in range(0, stacked.shape[1], MXU_WIDTH)], axis=1)
    res, pos = [], 0
    for grp in groups:
        acc = out[pos:pos + rows]
        for j in range(1, len(grp)):
            acc = acc + out[pos + j * rows:pos + (j + 1) * rows]
        res.append(acc)
        pos += len(grp) * rows
    return res


def _segsum_impl(x):
    return _segsum_many([x], _block_diag_ones())[0]


@jax.custom_vjp
def _segsum64(x):
    return _segsum_impl(x)


_segsum64.defvjp(lambda x: (_segsum_impl(x), None), lambda _, ct: (_segsum_impl(ct),))


def _tok(t):
    return t if isinstance(t, tuple) else (t, t.shape[1], 0)


def _tok_spec(tb, width, colblk):
    return pl.BlockSpec((tb, width), lambda i: (i, colblk))


def _par_spec(p):
    return pl.BlockSpec(p.shape, lambda i: (0, 0))


def _tok_fwd(name, fn, toks, pars, outs, tb):
    toks = [_tok(t) for t in toks]
    rows = toks[0][0].shape[0]
    n_in = len(toks) + len(pars)

    def body(*refs):
        row0 = pl.program_id(0) * tb
        res = fn(row0, *[r[...] for r in refs[:n_in]])
        for r, o in zip(refs[n_in:], res):
            r[...] = o.astype(r.dtype)

    return pl.pallas_call(
        body, name=name, grid=(rows // tb,),
        in_specs=[_tok_spec(tb, w, c) for _, w, c in toks] + [_par_spec(p) for p in pars],
        out_specs=[_tok_spec(tb, w, 0) for w, _ in outs],
        out_shape=[jax.ShapeDtypeStruct((rows, w), dt) for w, dt in outs],
        compiler_params=_cp("parallel"),
    )(*[a for a, _, _ in toks], *pars)


def _tok_bwd(name, fn, toks, pars, cts, want, tb, want_dtypes=None, after=None):
    toks = [_tok(t) for t in toks]
    want_dtypes = want_dtypes or [F32] * len(want)
    cts = [[_tok(c) for c in group] for group in cts]
    flat_cts = [c for group in cts for c in group]
    rows = toks[0][0].shape[0]
    n_tok, n_par, n_ct = len(toks), len(pars), len(flat_cts)
    extra = [] if after is None else [after]

    def body(*refs):
        i = pl.program_id(0)
        row0 = i * tb
        prim = [r[...].astype(F32) for r in refs[:n_tok + n_par]]
        ct_refs = list(refs[n_tok + n_par:n_tok + n_par + n_ct])
        out_refs = refs[n_tok + n_par + n_ct + len(extra):]
        res, vjp = jax.vjp(lambda *a: fn(row0, *a), *prim)
        ct = []
        for group, o in zip(cts, res):
            acc = None
            for _ in group:
                v = ct_refs.pop(0)[...].astype(F32)
                acc = v if acc is None else acc + v
            ct.append(acc.astype(o.dtype))
        grads = vjp(tuple(ct))
        for r, k in zip(out_refs[:len(want)], want):
            r[...] = grads[k].astype(r.dtype)

        @pl.when(i == 0)
        def _():
            for r in out_refs[len(want):]:
                r[...] = jnp.zeros_like(r)

        for r, g in zip(out_refs[len(want):], grads[n_tok:]):
            r[...] += g

    return pl.pallas_call(
        body, name=name, grid=(rows // tb,),
        in_specs=[_tok_spec(tb, w, c) for _, w, c in toks] + [_par_spec(p) for p in pars]
        + [_tok_spec(tb, w, c) for _, w, c in flat_cts] + [ANY] * len(extra),
        out_specs=[_tok_spec(tb, toks[k][1], 0) for k in want] + [_par_spec(p) for p in pars],
        out_shape=[jax.ShapeDtypeStruct((rows, toks[k][1]), dt) for k, dt in zip(want, want_dtypes)]
        + [jax.ShapeDtypeStruct(p.shape, F32) for p in pars],
        compiler_params=_cp("arbitrary"),
    )(*[a for a, _, _ in toks], *pars, *[a for a, _, _ in flat_cts], *extra)


def _mm(name, a, b, *, trans_b=False, add=None, after=None, tb, tn):
    rows, k = a.shape
    n = b.shape[0] if trans_b else b.shape[1]

    def body(*refs):
        a_ref, b_ref = refs[:2]
        o_ref = refs[-1]
        acc = _dot_nt(a_ref[...], b_ref[...]) if trans_b else _dot(a_ref[...], b_ref[...])
        if add is not None:
            acc = acc + refs[2][...]
        o_ref[...] = acc

    in_specs = [pl.BlockSpec((tb, k), lambda i, j: (i, 0)),
                pl.BlockSpec((tn, k), lambda i, j: (j, 0)) if trans_b else pl.BlockSpec((k, tn), lambda i, j: (0, j))]
    args = [a, b]
    if add is not None:
        in_specs.append(pl.BlockSpec((tb, tn), lambda i, j: (i, j)))
        args.append(add)
    if after is not None:
        in_specs.append(ANY)
        args.append(after)
    return pl.pallas_call(
        body, name=name, grid=(rows // tb, n // tn), in_specs=in_specs,
        out_specs=pl.BlockSpec((tb, tn), lambda i, j: (i, j)),
        out_shape=jax.ShapeDtypeStruct((rows, n), F32),
        compiler_params=_cp("parallel", "parallel"),
    )(*args)


def _mm_tn_call(name, grid, a, b, a_spec, b_spec, o_spec, acc_shape, out_shape):
    last = len(grid) - 1

    def body(a_ref, b_ref, o_ref, acc_ref):
        k = pl.program_id(last)

        @pl.when(k == 0)
        def _():
            acc_ref[...] = jnp.zeros_like(acc_ref)

        a_blk = a_ref[0] if len(a_ref.shape) == 3 else a_ref[...]
        b_blk = b_ref[0] if len(b_ref.shape) == 3 else b_ref[...]
        acc_ref[...] += _dot_tn(a_blk, b_blk)

        @pl.when(k == grid[last] - 1)
        def _():
            if len(o_ref.shape) == 3:
                o_ref[0] = acc_ref[...].astype(o_ref.dtype)
            else:
                o_ref[...] = acc_ref[...].astype(o_ref.dtype)

    return pl.pallas_call(
        body, name=name, grid=grid, in_specs=[a_spec, b_spec], out_specs=o_spec,
        out_shape=jax.ShapeDtypeStruct(out_shape, BF16), scratch_shapes=[pltpu.VMEM(acc_shape, F32)],
        compiler_params=_cp(*(["parallel"] * last + ["arbitrary"])),
    )(a, b)


def _mm_tn(name, a, b, *, tm, tn, tk):
    rows, m = a.shape
    n = b.shape[1]
    return _mm_tn_call(name, (m // tm, n // tn, rows // tk), a, b,
                       pl.BlockSpec((tk, tm), lambda i, j, k: (k, i)), pl.BlockSpec((tk, tn), lambda i, j, k: (k, j)),
                       pl.BlockSpec((tm, tn), lambda i, j, k: (i, j)), (tm, tn), (m, n))


def _mm_tn_from_slabs(name, a3, b, *, tk):
    s, rows, c = a3.shape
    n = b.shape[1]
    return _mm_tn_call(name, (s, rows // tk), a3, b,
                       pl.BlockSpec((1, tk, c), lambda i, k: (i, k, 0)), pl.BlockSpec((tk, n), lambda i, k: (k, 0)),
                       pl.BlockSpec((c, n), lambda i, k: (i, 0)), (c, n), (s * c, n))


def _col_tile(n, target):
    if n <= target:
        return n
    best = 128
    for d in range(128, target + 1, 128):
        if n % d == 0:
            best = d
    return best


def _rms(x, gain):
    return x * lax.rsqrt(jnp.mean(x * x, axis=-1, keepdims=True) + EPS) * gain


def _ffn_specs(d, fc, nj):
    return [pl.BlockSpec((1, fc, d), lambda i, j: (j, 0, 0)), pl.BlockSpec((1, fc, d), lambda i, j: (nj + j, 0, 0)),
            pl.BlockSpec((fc, d), lambda i, j: (j, 0))]


def _ffn_fwd(name, h, gain, wgu, wd, tb):
    rows, d = h.shape
    nj = wgu.shape[0] // 2
    fc = wgu.shape[1]

    def body(h_ref, g_ref, wg_ref, wu_ref, wd_ref, o_ref, xn_s, acc_s):
        j = pl.program_id(1)

        @pl.when(j == 0)
        def _():
            xn_s[...] = _rms(h_ref[...], g_ref[...]).astype(BF16)
            acc_s[...] = jnp.zeros_like(acc_s)

        wg, wu, wdn = wg_ref[0], wu_ref[0], wd_ref[...]
        for half in range(2):
            rs = pl.ds(half * (tb // 2), tb // 2)
            xn = xn_s[rs, :]
            gate = _dot_nt(xn, wg)
            up = _dot_nt(xn, wu)
            acc_s[rs, :] += _dot((_silu(gate) * up).astype(BF16), wdn)

        @pl.when(j == nj - 1)
        def _():
            o_ref[...] = h_ref[...] + 0.5 * acc_s[...]

    return pl.pallas_call(
        body, name=name, grid=(rows // tb, nj),
        in_specs=[pl.BlockSpec((tb, d), lambda i, j: (i, 0)), pl.BlockSpec((1, d), lambda i, j: (0, 0))]
        + _ffn_specs(d, fc, nj),
        out_specs=pl.BlockSpec((tb, d), lambda i, j: (i, 0)),
        out_shape=jax.ShapeDtypeStruct((rows, d), F32),
        scratch_shapes=[pltpu.VMEM((tb, d), BF16), pltpu.VMEM((tb, d), F32)],
        compiler_params=_cp("parallel", "arbitrary"),
    )(h, gain, wgu, wgu, wd)


def _ffn_bwd(name, h, gain, dout, wgu, wd, tb):
    rows, d = h.shape
    nj = wgu.shape[0] // 2
    fc = wgu.shape[1]

    def body(h_ref, g_ref, do_ref, wg_ref, wu_ref, wd_ref,
             dh_ref, dhb_ref, dg_ref, xn_ref, act_ref, dgate_ref, dup_ref, dhalf_ref, dxn_s):
        i, j = pl.program_id(0), pl.program_id(1)

        @pl.when(j == 0)
        def _():
            xn_ref[...] = _rms(h_ref[...], g_ref[...]).astype(BF16)
            dhalf_ref[...] = (0.5 * do_ref[...]).astype(BF16)
            dxn_s[...] = jnp.zeros_like(dxn_s)

        wg, wu, wdn = wg_ref[0], wu_ref[0], wd_ref[...]
        for half in range(2):
            rs = pl.ds(half * (tb // 2), tb // 2)
            xn = xn_ref[rs, :]
            gate = _dot_nt(xn, wg)
            up = _dot_nt(xn, wu)
            sg = _sigmoid(gate)
            dact = _dot_nt(dhalf_ref[rs, :], wdn)
            act_ref[0, rs, :] = (gate * sg * up).astype(BF16)
            dgate = (dact * up * (sg * (1.0 + gate * (1.0 - sg)))).astype(BF16)
            dup = (dact * gate * sg).astype(BF16)
            dgate_ref[0, rs, :] = dgate
            dup_ref[0, rs, :] = dup
            dxn_s[rs, :] += _dot(dgate, wg) + _dot(dup, wu)

        @pl.when((i == 0) & (j == 0))
        def _():
            dg_ref[...] = jnp.zeros_like(dg_ref)

        @pl.when(j == nj - 1)
        def _():
            x = h_ref[...]
            r = lax.rsqrt(jnp.mean(x * x, axis=-1, keepdims=True) + EPS)
            dxn = dxn_s[...]
            dyg = dxn * g_ref[...]
            dh = do_ref[...] + r * dyg - x * (r * r * r) * jnp.mean(dyg * x, axis=-1, keepdims=True)
            dh_ref[...] = dh
            dhb_ref[...] = dh.astype(BF16)
            dg_ref[...] += jnp.sum(dxn * x * r, axis=0, keepdims=True)

    row_d = pl.BlockSpec((tb, d), lambda i, j: (i, 0))
    slab = pl.BlockSpec((1, tb, fc), lambda i, j: (j, i, 0))
    hidden = jax.ShapeDtypeStruct((nj, rows, fc), BF16)
    return pl.pallas_call(
        body, name=name, grid=(rows // tb, nj),
        in_specs=[row_d, pl.BlockSpec((1, d), lambda i, j: (0, 0)), row_d] + _ffn_specs(d, fc, nj),
        out_specs=[row_d, row_d, pl.BlockSpec((1, d), lambda i, j: (0, 0)), row_d, slab, slab, slab, row_d],
        out_shape=[jax.ShapeDtypeStruct((rows, d), F32), jax.ShapeDtypeStruct((rows, d), BF16),
                   jax.ShapeDtypeStruct((1, d), F32), jax.ShapeDtypeStruct((rows, d), BF16),
                   hidden, hidden, hidden, jax.ShapeDtypeStruct((rows, d), BF16)],
        scratch_shapes=[pltpu.VMEM((tb, d), F32)],
        compiler_params=_cp("arbitrary", "arbitrary"),
    )(h, gain, dout, wgu, wgu, wd)


def _shift_rows(x, s):
    return pltpu.roll(x, s % x.shape[0], 0)


def _a_post(c, which):
    s = _silu(c)
    n = s * lax.rsqrt(jnp.sum(s * s, axis=-1, keepdims=True) + 1e-6)
    scale = jnp.where(which == 0, HEAD_A ** -0.5, 1.0)
    return jnp.where(which == 2, s, n * scale)


def _conv(x, w):
    return x * w[3:4] + _shift_rows(x, 1) * w[2:3] + _shift_rows(x, 2) * w[1:2] + _shift_rows(x, 3) * w[0:1]


def _a_pre_fwd(zqkv, conv_w):
    rows, width = zqkv.shape
    heads = width // (3 * HEAD_A)

    def body(x_ref, w_ref, o_ref):
        which = pl.program_id(0) // heads
        live = lax.broadcasted_iota(jnp.int32, (rows, HEAD_A), 0) >= PAD
        o_ref[...] = jnp.where(live, _a_post(_conv(x_ref[...], w_ref[...]), which), 0.0)

    return pl.pallas_call(
        body, name="a_pre_fwd", grid=(width // HEAD_A,),
        in_specs=[pl.BlockSpec((rows, HEAD_A), lambda c: (0, c)), pl.BlockSpec((4, HEAD_A), lambda c: (0, c))],
        out_specs=pl.BlockSpec((rows, HEAD_A), lambda c: (0, c)),
        out_shape=jax.ShapeDtypeStruct((rows, width), F32),
        compiler_params=_cp("parallel"),
    )(zqkv, conv_w)


def _a_pre_bwd(zqkv, conv_w, dqkv):
    rows, width = zqkv.shape
    heads = width // (3 * HEAD_A)

    def body(x_ref, w_ref, ct_ref, dx_ref, dw_ref):
        which = pl.program_id(0) // heads
        live = lax.broadcasted_iota(jnp.int32, (rows, HEAD_A), 0) >= PAD
        x, w = x_ref[...], w_ref[...]
        _, vjp = jax.vjp(lambda c: _a_post(c, which), _conv(x, w))
        (dc,) = vjp(jnp.where(live, ct_ref[...], 0.0))
        dc = jnp.where(live, dc, 0.0)
        dx_ref[...] = (dc * w[3:4] + _shift_rows(dc, -1) * w[2:3] + _shift_rows(dc, -2) * w[1:2]
                       + _shift_rows(dc, -3) * w[0:1]).astype(BF16)
        dw_ref[...] = jnp.concatenate(
            [jnp.sum(dc * (_shift_rows(x, 3 - j) if j < 3 else x), axis=0, keepdims=True) for j in range(4)], axis=0)

    col = pl.BlockSpec((rows, HEAD_A), lambda c: (0, c))
    wsp = pl.BlockSpec((4, HEAD_A), lambda c: (0, c))
    return pl.pallas_call(
        body, name="a_pre_bwd", grid=(width // HEAD_A,),
        in_specs=[col, wsp, col], out_specs=[col, wsp],
        out_shape=[jax.ShapeDtypeStruct((rows, width), BF16), jax.ShapeDtypeStruct((4, width), F32)],
        compiler_params=_cp("parallel"),
    )(zqkv, conv_w, dqkv)


SHIFT_TILE = 384


def _shift_fwd(zb, mu):
    rows, width = zb.shape

    def body(x_ref, mu_ref, o_ref):
        x = x_ref[...]
        first = lax.broadcasted_iota(jnp.int32, x.shape, 0) == 0
        prev = jnp.where(first, 0.0, _shift_rows(x, 1))
        o_ref[...] = x + (prev - x) * mu_ref[...]

    col = pl.BlockSpec((rows, SHIFT_TILE), lambda c: (0, c))
    return pl.pallas_call(
        body, name="shift_fwd", grid=(width // SHIFT_TILE,),
        in_specs=[col, pl.BlockSpec((1, SHIFT_TILE), lambda c: (0, c))], out_specs=col,
        out_shape=jax.ShapeDtypeStruct((rows, width), F32), compiler_params=_cp("parallel"),
    )(zb, mu)


def _shift_bwd(zb, mu, dzf):
    rows, width = zb.shape

    def body(x_ref, mu_ref, ct_ref, dx_ref, dmu_ref):
        x, ct, mu_v = x_ref[...], ct_ref[...], mu_ref[...]
        row = lax.broadcasted_iota(jnp.int32, x.shape, 0)
        prev = jnp.where(row == 0, 0.0, _shift_rows(x, 1))
        nxt = jnp.where(row == rows - 1, 0.0, _shift_rows(ct, -1))
        dx_ref[...] = (ct * (1.0 - mu_v) + nxt * mu_v).astype(BF16)
        dmu_ref[...] = jnp.sum(ct * (prev - x), axis=0, keepdims=True)

    col = pl.BlockSpec((rows, SHIFT_TILE), lambda c: (0, c))
    msp = pl.BlockSpec((1, SHIFT_TILE), lambda c: (0, c))
    return pl.pallas_call(
        body, name="shift_bwd", grid=(width // SHIFT_TILE,),
        in_specs=[col, msp, col], out_specs=[col, msp],
        out_shape=[jax.ShapeDtypeStruct((rows, width), BF16), jax.ShapeDtypeStruct((1, width), F32)],
        compiler_params=_cp("parallel"),
    )(zb, mu, dzf)


def _dn_chunk(q, k, v, beta, g, state):
    heads = range(len(q))
    ri = lax.broadcasted_iota(jnp.int32, (CHUNK, CHUNK), 0)
    ci = lax.broadcasted_iota(jnp.int32, (CHUNK, CHUNK), 1)
    eye = (ri == ci).astype(F32)
    incl = ri >= ci
    last = lax.broadcasted_iota(jnp.int32, (CHUNK, 1), 0) == CHUNK - 1
    g_row = [jnp.sum(g[h] * eye, axis=0, keepdims=True) for h in heads]
    gc = [jnp.sum(jnp.where(incl, g_row[h], 0.0), axis=1, keepdims=True) for h in heads]
    gc_row = [jnp.sum(gc[h] * eye, axis=0, keepdims=True) for h in heads]
    decay = [jnp.where(incl, jnp.exp(jnp.where(incl, gc[h] - gc_row[h], 0.0)), 0.0) for h in heads]
    kb = [k[h] * beta[h] for h in heads]
    vb = [v[h] * beta[h] for h in heads]
    p = [-jnp.where(ri > ci, _dot_nt(kb[h], k[h]) * decay[h], 0.0) for h in heads]
    tinv = [eye + p[h] for h in heads]
    for _ in range(5):
        p = [_dot(p[h], p[h], DN_PRECISION) for h in heads]
        tinv = [tinv[h] + _dot(tinv[h], p[h], DN_PRECISION) for h in heads]
    eg = [jnp.exp(gc[h]) for h in heads]
    u = [_dot(tinv[h], vb[h]) for h in heads]
    wk = [_dot(tinv[h], kb[h] * eg[h]) for h in heads]
    attn = [_dot_nt(q[h], k[h]) * decay[h] for h in heads]
    g_last = [jnp.sum(jnp.where(last, gc[h], 0.0), axis=0, keepdims=True) for h in heads]
    k_tail = [k[h] * jnp.exp(g_last[h] - gc[h]) for h in heads]
    v_new = [u[h] - _dot(wk[h], state[h]) for h in heads]
    o = [_dot(q[h] * eg[h], state[h]) + _dot(attn[h], v_new[h]) for h in heads]
    new = [state[h] * jnp.exp(g_last[h]) + _dot_tn(k_tail[h], v_new[h]) for h in heads]
    return o, new


def _bg_cols(bg, h, heads):
    lane = lax.broadcasted_iota(jnp.int32, bg.shape, 1)
    beta = jnp.sum(jnp.where(lane == h, bg, 0.0), axis=1, keepdims=True)
    g = jnp.sum(jnp.where(lane == heads + h, bg, 0.0), axis=1, keepdims=True)
    return beta, g


def _dn_fwd(qkv, bg):
    rows = qkv.shape[0]
    heads = qkv.shape[1] // (3 * HEAD_A)
    n = rows // CHUNK
    hp, groups = heads, 1

    def body(q_ref, k_ref, v_ref, bg_ref, o_ref, hist_ref, s_ref):
        c, grp = pl.program_id(0), pl.program_id(1)

        @pl.when(c == 0)
        def _():
            for i in range(hp):
                s_ref[grp * hp + i] = jnp.zeros((HEAD_A, HEAD_A), F32)

        bg_v = bg_ref[...]
        cols = [slice(i * HEAD_A, (i + 1) * HEAD_A) for i in range(hp)]
        state = [s_ref[grp * hp + i] for i in range(hp)]
        beta_g = [_bg_cols(bg_v, grp * hp + i, heads) for i in range(hp)]
        o, new = _dn_chunk([q_ref[:, c_] for c_ in cols], [k_ref[:, c_] for c_ in cols], [v_ref[:, c_] for c_ in cols],
                           [b for b, _ in beta_g], [g for _, g in beta_g], state)
        for i in range(hp):
            hist_ref[0, i] = state[i]
            o_ref[:, cols[i]] = o[i]
            s_ref[grp * hp + i] = new[i]

    def part(p):
        return pl.BlockSpec((CHUNK, hp * HEAD_A), lambda c, grp: (c, p * groups + grp))

    return pl.pallas_call(
        body, name="deltanet_fwd", grid=(n, groups),
        in_specs=[part(0), part(1), part(2), pl.BlockSpec((CHUNK, 128), lambda c, grp: (c, 0))],
        out_specs=[part(0), pl.BlockSpec((1, hp, HEAD_A, HEAD_A), lambda c, grp: (c, grp, 0, 0))],
        out_shape=[jax.ShapeDtypeStruct((rows, heads * HEAD_A), F32),
                   jax.ShapeDtypeStruct((n, heads, HEAD_A, HEAD_A), F32)],
        scratch_shapes=[pltpu.VMEM((heads, HEAD_A, HEAD_A), F32)],
        compiler_params=_cp("arbitrary", "arbitrary"),
    )(qkv, qkv, qkv, bg)


def _dn_bwd(qkv, bg, hist, do):
    rows = qkv.shape[0]
    heads = qkv.shape[1] // (3 * HEAD_A)
    n = rows // CHUNK
    hp, groups = heads, 1

    def body(q_ref, k_ref, v_ref, bg_ref, hist_ref, do_ref, dqkv_ref, dbg_ref, ds_ref):
        c, grp = pl.program_id(0), pl.program_id(1)

        @pl.when(c == 0)
        def _():
            for i in range(hp):
                ds_ref[grp * hp + i] = jnp.zeros((HEAD_A, HEAD_A), F32)

        bg_v = bg_ref[...]
        lane = lax.broadcasted_iota(jnp.int32, (CHUNK, 128), 1)
        cols = [slice(i * HEAD_A, (i + 1) * HEAD_A) for i in range(hp)]
        beta_g = [_bg_cols(bg_v, grp * hp + i, heads) for i in range(hp)]
        _, vjp = jax.vjp(_dn_chunk, [q_ref[:, c_] for c_ in cols], [k_ref[:, c_] for c_ in cols],
                         [v_ref[:, c_] for c_ in cols], [b for b, _ in beta_g], [g for _, g in beta_g],
                         [hist_ref[0, i] for i in range(hp)])
        dq, dk, dv, dbeta, dg, ds = vjp(([do_ref[:, c_] for c_ in cols], [ds_ref[grp * hp + i] for i in range(hp)]))
        dbg = jnp.zeros((CHUNK, 128), F32)
        for i in range(hp):
            h = grp * hp + i
            for p, part_grad in enumerate((dq, dk, dv)):
                dqkv_ref[:, pl.ds((p * heads + i) * HEAD_A, HEAD_A)] = part_grad[i]
            ds_ref[h] = ds[i]
            dbg = dbg + jnp.where(lane == h, dbeta[i], 0.0) + jnp.where(lane == heads + h, dg[i], 0.0)

        @pl.when(grp == 0)
        def _():
            dbg_ref[...] = jnp.zeros_like(dbg_ref)

        dbg_ref[...] += dbg

    def part(p):
        return pl.BlockSpec((CHUNK, hp * HEAD_A), lambda c, grp: (n - 1 - c, p * groups + grp))

    return pl.pallas_call(
        body, name="deltanet_bwd", grid=(n, groups),
        in_specs=[part(0), part(1), part(2), pl.BlockSpec((CHUNK, 128), lambda c, grp: (n - 1 - c, 0)),
                  pl.BlockSpec((1, hp, HEAD_A, HEAD_A), lambda c, grp: (n - 1 - c, grp, 0, 0)), part(0)],
        out_specs=[pl.BlockSpec((CHUNK, 3 * heads * HEAD_A), lambda c, grp: (n - 1 - c, 0)),
                   pl.BlockSpec((CHUNK, 128), lambda c, grp: (n - 1 - c, 0))],
        out_shape=[jax.ShapeDtypeStruct(qkv.shape, F32), jax.ShapeDtypeStruct((rows, 128), F32)],
        scratch_shapes=[pltpu.VMEM((heads, HEAD_A, HEAD_A), F32)],
        compiler_params=_cp("arbitrary", "arbitrary"),
    )(qkv, qkv, qkv, bg, hist, do)


def _head_mask(heads, width):
    return (lax.broadcasted_iota(jnp.int32, (heads, width), 0)
            == lax.broadcasted_iota(jnp.int32, (heads, width), 1) // HEAD_B)


def _masked_rows(mask, row):
    return jnp.where(mask, row, 0.0).astype(BF16)


def _rwkv_fwd(r, w, k, v, a, b):
    rows, width = r.shape
    heads = width // HEAD_B
    ts = SCAN_STEPS

    def body(r_ref, w_ref, k_ref, v_ref, a_ref, b_ref, y_ref, hist_ref, s_ref):
        @pl.when(pl.program_id(0) == 0)
        def _():
            s_ref[...] = jnp.zeros_like(s_ref)

        mask = _head_mask(heads, width)
        onehot = mask.astype(BF16)
        onehot2 = jnp.concatenate([onehot, onehot], axis=0)
        bd = _block_diag_ones()

        spread_v = [_dot_tn(jnp.concatenate(_hi_lo(v_ref[j]), axis=0), onehot2) for j in range(ts)]
        s = s_ref[...]
        ys = []
        for j in range(ts):
            row = pl.ds(j, 1)
            hist_ref[j] = s
            (sa,) = _segsum_many([((s * a_ref[row, :]).astype(BF16),)], bd)
            s = s * w_ref[row, :] + sa * b_ref[row, :] + spread_v[j] * k_ref[row, :]
            ys.append(_dot_nt(_masked_rows(mask, r_ref[row, :]), s.astype(BF16)))
        for j in range(ts):
            y_ref[j] = ys[j]
        s_ref[...] = s

    blk = pl.BlockSpec((ts, width), lambda i: (i, 0))
    blk3 = pl.BlockSpec((ts, heads, HEAD_B), lambda i: (i, 0, 0))
    return pl.pallas_call(
        body, name="rwkv_fwd", grid=(rows // ts,),
        in_specs=[blk, blk, blk, blk3, blk, blk],
        out_specs=[blk3, pl.BlockSpec((ts, HEAD_B, width), lambda i: (i, 0, 0)),
                   pl.BlockSpec((HEAD_B, width), lambda i: (0, 0))],
        out_shape=[jax.ShapeDtypeStruct((rows, heads, HEAD_B), F32), jax.ShapeDtypeStruct((rows, HEAD_B, width), F32),
                   jax.ShapeDtypeStruct((HEAD_B, width), F32)],
        compiler_params=_cp("arbitrary"),
    )(r, w, k, v, a, b)


def _rwkv_bwd(r, w, k, v, a, b, hist, last, dy):
    rows, width = r.shape
    heads = width // HEAD_B
    ts = SCAN_STEPS
    nb = rows // ts

    def body(r_ref, w_ref, k_ref, v_ref, a_ref, b_ref, hist_ref, last_ref, dy_ref,
             dr_ref, dw_ref, dk_ref, dv_ref, da_ref, db_ref, g_ref, after_ref):
        @pl.when(pl.program_id(0) == 0)
        def _():
            g_ref[...] = jnp.zeros_like(g_ref)
            after_ref[...] = last_ref[...]

        mask = _head_mask(heads, width)
        onehot = mask.astype(BF16)
        bd = _block_diag_ones()

        def own_lanes(x):
            return jnp.sum(jnp.where(mask, x, 0.0), axis=0, keepdims=True)

        def colsum(x):
            return jnp.sum(x, axis=0, keepdims=True)

        dy_m = [dy_ref[j].astype(BF16) for j in range(ts)]
        spread_dy = [_dot_tn(dy_m[j], onehot) for j in range(ts)]
        state_after = [hist_ref[j + 1] if j < ts - 1 else after_ref[...] for j in range(ts)]
        dr = [own_lanes(_dot(dy_m[j], state_after[j].astype(BF16))) for j in range(ts)]
        sa_m = [_dot_nt(_masked_rows(mask, a_ref[pl.ds(j, 1), :]), hist_ref[j].astype(BF16)) for j in range(ts)]
        g = g_ref[...]
        dw, dk, db, da, dv = {}, {}, {}, {}, {}
        for j in reversed(range(ts)):
            row = pl.ds(j, 1)
            sp = hist_ref[j]
            g = g + spread_dy[j] * r_ref[row, :]
            (dsa,) = _segsum_many([((g * b_ref[row, :]).astype(BF16),)], bd)
            g_b = g.astype(BF16)
            both = _dot(jnp.concatenate([v_ref[j].astype(BF16), sa_m[j].astype(BF16)], axis=0), g_b)
            dk[j], db[j] = own_lanes(both[:heads]), own_lanes(both[heads:])
            dv[j] = _dot_nt(_masked_rows(mask, k_ref[row, :]), g_b)
            dw[j] = colsum(g * sp)
            da[j] = colsum(sp * dsa)
            g = g * w_ref[row, :] + dsa * a_ref[row, :]
        g_ref[...] = g
        after_ref[...] = hist_ref[0]
        for j in range(ts):
            dv_ref[j] = dv[j]
            for ref, vals in ((dr_ref, dr), (dw_ref, dw), (dk_ref, dk), (da_ref, da), (db_ref, db)):
                ref[pl.ds(j, 1), :] = vals[j]

    blk = pl.BlockSpec((ts, width), lambda i: (nb - 1 - i, 0))
    blk3 = pl.BlockSpec((ts, heads, HEAD_B), lambda i: (nb - 1 - i, 0, 0))
    state = pl.BlockSpec((HEAD_B, width), lambda i: (0, 0))
    return pl.pallas_call(
        body, name="rwkv_bwd", grid=(nb,),
        in_specs=[blk, blk, blk, blk3, blk, blk, pl.BlockSpec((ts, HEAD_B, width), lambda i: (nb - 1 - i, 0, 0)),
                  state, blk3],
        out_specs=[blk, blk, blk, blk3, blk, blk],
        out_shape=[jax.ShapeDtypeStruct((rows, width), F32)] * 3 + [jax.ShapeDtypeStruct((rows, heads, HEAD_B), F32)]
        + [jax.ShapeDtypeStruct((rows, width), F32)] * 2,
        scratch_shapes=[pltpu.VMEM((HEAD_B, width), F32), pltpu.VMEM((HEAD_B, width), F32)],
        compiler_params=_cp("arbitrary"),
    )(r, w, k, v, a, b, hist, last, dy)


def _live(row0, shape):
    return (row0 + lax.broadcasted_iota(jnp.int32, shape, 0)) >= PAD


def _norm_fn(row0, h, gain):
    return (_rms(h, gain),)


def _norm_res_fn(row0, h, gain):
    return _rms(h, gain), h


def _make_bg_fn(heads):
    def fn(row0, x, log_rate, dt_bias):
        lane = lax.broadcasted_iota(jnp.int32, x.shape, 1)
        beta = _sigmoid(x)
        g = -jnp.exp(log_rate) * _softplus(x + dt_bias)
        out = jnp.where(lane < heads, beta, jnp.where(lane < 2 * heads, g, 0.0))
        return (jnp.where(_live(row0, x.shape), out, 0.0),)
    return fn


def _b_pre_fn(row0, zf, w0, w_up, a0, a_up, g_up, k_k, k_a):
    d = w0.shape[1]
    r, k, v = zf[:, :d], zf[:, d:2 * d], zf[:, 2 * d:3 * d]
    lo = zf[:, 3 * d:3 * d + 128]
    lg = zf[:, 3 * d + 128:3 * d + LORA_PAD]
    lane = lax.broadcasted_iota(jnp.int32, lo.shape, 1)
    lw = _dot(jnp.where(lane < LORA_W, jnp.tanh(lo), 0.0), w_up)
    la = _dot(jnp.where(lane >= LORA_W, lo, 0.0), a_up)
    lane_g = lax.broadcasted_iota(jnp.int32, lg.shape, 1)
    gate = _dot(jnp.where(lane_g < LORA_G, _sigmoid(lg), 0.0), g_up)
    decay = jnp.exp(-jnp.exp(-_softplus(-(w0 + lw)) - 0.5))
    a = _sigmoid(a0 + la)
    kx = k * k_k
    kk = kx * lax.rsqrt(_segsum64(kx * kx) + 1e-6)
    k2 = k * (1.0 + (a - 1.0) * k_a)
    return r, decay, k2, v, -kk, kk * a, gate


def _post_fn(row0, o, zg, y, r, k2, v, gate, out_gain, r_k, ln_g, ln_b):
    d = o.shape[1]
    az, ga, gb = zg[:, :d], zg[:, d:2 * d], zg[:, 2 * d:]
    heads = d // HEAD_A
    parts = []
    for h in range(heads):
        oh = o[:, h * HEAD_A:(h + 1) * HEAD_A]
        parts.append(oh * lax.rsqrt(jnp.mean(oh * oh, axis=-1, keepdims=True) + EPS) * out_gain)
    o_a = jnp.concatenate(parts, axis=1) * _silu(az)
    mean = _segsum64(y) * (1.0 / HEAD_B)
    yc = y - mean
    var = _segsum64(yc * yc) * (1.0 / HEAD_B)
    yn = yc * lax.rsqrt(var + GN_EPS) * ln_g + ln_b
    o_b = (yn + _segsum64(r * k2 * r_k) * v) * gate
    return (_sigmoid(ga) * o_a + _sigmoid(gb) * o_b,)


def _loss(h3, target, gain, tb):
    rows, d = h3.shape

    def body(h_ref, t_ref, g_ref, dh_ref, dg_ref, l_ref):
        i = pl.program_id(0)
        live = (i * tb + lax.broadcasted_iota(jnp.int32, (tb, 1), 0)) >= CHUNK
        tgt = t_ref[...]

        def f(h, g):
            err = _rms(h, g) - tgt
            return 0.5 * jnp.sum(jnp.where(live, jnp.mean(err * err, axis=-1, keepdims=True), 0.0))

        val, vjp = jax.vjp(f, h_ref[...], g_ref[...])
        dh, dg = vjp(jnp.ones((), F32))
        dh_ref[...] = dh

        @pl.when(i == 0)
        def _():
            dg_ref[...] = jnp.zeros_like(dg_ref)
            l_ref[...] = jnp.zeros_like(l_ref)

        dg_ref[...] += dg
        l_ref[...] += jnp.full((1, 128), val, F32)

    blk = pl.BlockSpec((tb, d), lambda i: (i, 0))
    return pl.pallas_call(
        body, name="loss", grid=(rows // tb,),
        in_specs=[blk, blk, pl.BlockSpec((1, d), lambda i: (0, 0))],
        out_specs=[blk, pl.BlockSpec((1, d), lambda i: (0, 0)), pl.BlockSpec((1, 128), lambda i: (0, 0))],
        out_shape=[jax.ShapeDtypeStruct((rows, d), F32), jax.ShapeDtypeStruct((1, d), F32),
                   jax.ShapeDtypeStruct((1, 128), F32)],
        compiler_params=_cp("arbitrary"),
    )(h3, target, gain)


def _adamw_math(w, g, m, v):
    m2 = ADAM_B1 * m + (1.0 - ADAM_B1) * g
    v2 = ADAM_B2 * v + (1.0 - ADAM_B2) * (g * g)
    m_hat = m2 / (1.0 - ADAM_B1 ** ADAM_STEP)
    v_hat = v2 / (1.0 - ADAM_B2 ** ADAM_STEP)
    return -ADAM_LR * (m_hat / (jnp.sqrt(v_hat) + ADAM_EPS) + ADAM_WD * w), m2, v2


def _adamw(name, own, landed, w, m, v):
    rows, cols = w.shape
    if rows % 16 == 0:
        rb = _tb(rows, 128)
        grid, blk = (rows // rb,), pl.BlockSpec((rb, cols), lambda i: (i, 0))
        landed_blk = pl.BlockSpec((N_DEV - 1, rb, cols), lambda i: (0, i, 0))
    else:
        grid, blk = (cols // 128,), pl.BlockSpec((rows, 128), lambda i: (0, i))
        landed_blk = pl.BlockSpec((N_DEV - 1, rows, 128), lambda i: (0, 0, i))

    def body(o_ref, s_ref, w_ref, m_ref, v_ref, g_ref, d_ref, m2_ref, v2_ref):
        g = o_ref[...].astype(F32)
        for peer in range(N_DEV - 1):
            g = g + s_ref[peer].astype(F32)
        g_ref[...] = g
        d_ref[...], m2_ref[...], v2_ref[...] = _adamw_math(w_ref[...], g, m_ref[...], v_ref[...])

    return pl.pallas_call(
        body, name=name, grid=grid, in_specs=[blk, landed_blk, blk, blk, blk],
        out_specs=[blk] * 4, out_shape=[jax.ShapeDtypeStruct((rows, cols), F32)] * 4,
        compiler_params=_cp("parallel"),
    )(own, landed, w, m, v)


def _sum_slabs(name, slabs, rb):
    _, rows, cols = slabs.shape

    def body(s_ref, o_ref):
        g = s_ref[0]
        for dev in range(1, N_DEV):
            g = g + s_ref[dev]
        o_ref[...] = g

    return pl.pallas_call(
        body, name=name, grid=(rows // rb,),
        in_specs=[pl.BlockSpec((N_DEV, rb, cols), lambda i: (0, i, 0))],
        out_specs=pl.BlockSpec((rb, cols), lambda i: (i, 0)),
        out_shape=jax.ShapeDtypeStruct((rows, cols), F32), compiler_params=_cp("parallel"),
    )(slabs)


def _adamw_small(w, g, m, v):
    def body(w_ref, g_ref, m_ref, v_ref, d_ref, m2_ref, v2_ref):
        d_ref[...], m2_ref[...], v2_ref[...] = _adamw_math(w_ref[...], g_ref[...], m_ref[...], v_ref[...])

    return pl.pallas_call(body, name="adamw_small", out_shape=[jax.ShapeDtypeStruct(w.shape, F32)] * 3)(w, g, m, v)


def _place():
    return lax.axis_index("x"), lax.axis_index("y"), lax.axis_index("c")


def _index(p):
    return 4 * p[0] + 2 * p[1] + p[2]


def _all_gather(name, xs):
    n = len(xs)

    def body(*refs):
        x_refs, o_refs = refs[:n], refs[n:2 * n]
        send_sems, recv_sems, local_sems = refs[2 * n:]
        x, y, c = _place()
        me, sibling = (x, y, c), (x, y, 1 - c)
        chips = [(1 - x, y), (x, 1 - y), (1 - x, 1 - y)]

        def copy(i, k, block, to, src=None):
            dst = o_refs[i].at[_index(block)]
            return pltpu.make_async_remote_copy(src_ref=dst if src is None else src, dst_ref=dst,
                                                send_sem=send_sems.at[i, k], recv_sem=recv_sems.at[i, k],
                                                device_id=to, device_id_type=MESH_ID)

        mine = [pltpu.make_async_copy(x_refs[i], o_refs[i].at[_index(me)], local_sems.at[i]) for i in range(n)]
        for cp in mine:
            cp.start()
        first = []
        for i in range(n):
            first.append(copy(i, 0, me, sibling, src=x_refs[i]))
            first += [copy(i, 1 + j, me, (*chip, c), src=x_refs[i]) for j, chip in enumerate(chips)]
        for cp in first:
            cp.start()
        passed = []
        for j, chip in enumerate(chips):
            for i in range(n):
                copy(i, 1 + j, (*chip, c), me).wait_recv()
                cp = copy(i, 4 + j, (*chip, c), sibling)
                cp.start()
                passed.append(cp)
        for i in range(n):
            copy(i, 0, sibling, me).wait_recv()
            for j, chip in enumerate(chips):
                copy(i, 4 + j, (*chip, 1 - c), me).wait_recv()
        for cp in first + passed:
            cp.wait_send()
        for cp in mine:
            cp.wait()

    return pl.pallas_call(
        body, name=name, in_specs=[ANY] * n, out_specs=[ANY] * n,
        out_shape=[jax.ShapeDtypeStruct((N_DEV,) + x.shape, x.dtype) for x in xs],
        scratch_shapes=[pltpu.SemaphoreType.DMA((n, 7)), pltpu.SemaphoreType.DMA((n, 7)), pltpu.SemaphoreType.DMA((n,))],
    )(*xs)


def _exchange_start(name, xs, after=None, gather=False):
    n = len(xs)
    copies = n * (N_DEV - 1)
    extra = [] if after is None else [after]

    def body(*refs):
        x_refs, land_refs = refs[:n], refs[n:2 * n]
        sems = refs[2 * n + len(extra):2 * n + len(extra) + 2 * copies]
        token = refs[-1]
        for i, k, peer in _exchange_copies(n):
            _exchange_copy(x_refs, land_refs, sems, i, k, peer, gather).start()
        token[...] = jnp.zeros_like(token)

    lands = [lax.empty((N_DEV,) + x.shape if gather else (N_DEV - 1,) + x.shape[1:], x.dtype) for x in xs]
    out = pl.pallas_call(
        body, name=name,
        out_shape=(*[pltpu.SemaphoreType.DMA(())] * (2 * copies), *[pltpu.HBM(x.shape, x.dtype) for x in xs],
                   *[pltpu.HBM(l.shape, l.dtype) for l in lands], jax.ShapeDtypeStruct((8, 128), F32)),
        in_specs=[HBM_SPEC] * (2 * n) + [ANY] * len(extra),
        out_specs=(*[SEM_SPEC] * (2 * copies), *[HBM_SPEC] * (2 * n), pl.BlockSpec(memory_space=pltpu.VMEM)),
        input_output_aliases={i: 2 * copies + i for i in range(2 * n)},
        compiler_params=pltpu.CompilerParams(has_side_effects=pltpu.SideEffectType.DATAFLOW_SIDE_EFFECTING),
    )(*[pltpu.with_memory_space_constraint(a, pltpu.HBM) for a in list(xs) + lands], *extra)
    sems, rest = list(out[:2 * copies]), out[2 * copies:]
    return sems, list(rest[:n]), list(rest[n:2 * n]), rest[-1]


def _exchange_copies(n):
    x, y, c = _place()
    for k in range(1, N_DEV):
        peer = ((1 - x) if k & 4 else x, (1 - y) if k & 2 else y, (1 - c) if k & 1 else c)
        for i in range(n):
            yield i, k - 1, peer


def _exchange_copy(x_refs, land_refs, sems, i, k, peer, gather, arriving=False):
    copies = len(sems) // 2
    which = i * (N_DEV - 1) + k
    src = x_refs[i] if gather else x_refs[i].at[_index(peer)]
    dst = land_refs[i].at[_index(peer if arriving else _place())] if gather else land_refs[i].at[k]
    return pltpu.make_async_remote_copy(src_ref=src, dst_ref=dst, send_sem=sems[which], recv_sem=sems[copies + which],
                                        device_id=peer, device_id_type=MESH_ID)


def _exchange_wait(name, sems, xs, lands, after, gather=False):
    n = len(xs)

    def body(*refs):
        x_refs, land_refs = refs[:n], refs[n:2 * n]
        sem_refs = refs[2 * n:2 * n + len(sems)]
        for i, k, peer in _exchange_copies(n):
            _exchange_copy(x_refs, land_refs, sem_refs, i, k, peer, gather).wait_send()
            _exchange_copy(x_refs, land_refs, sem_refs, i, k, peer, gather, arriving=True).wait_recv()

    out = pl.pallas_call(
        body, name=name,
        out_shape=(*[pltpu.HBM(x.shape, x.dtype) for x in xs], *[pltpu.HBM(l.shape, l.dtype) for l in lands]),
        in_specs=[HBM_SPEC] * (2 * n) + [SEM_SPEC] * len(sems) + [ANY], out_specs=tuple([HBM_SPEC] * (2 * n)),
        input_output_aliases={i: i for i in range(2 * n)},
        compiler_params=pltpu.CompilerParams(has_side_effects=pltpu.SideEffectType.DATAFLOW_SIDE_EFFECTING),
    )(*xs, *lands, *sems, after)
    return list(out[:n]), list(out[n:])


def _pack(arrays):
    flat = jnp.concatenate([a.reshape(-1) for a in arrays])
    pad = (-flat.shape[0]) % 1024
    return jnp.pad(flat, (0, pad)).reshape(-1, 128)


def _unpack(packed, shapes):
    flat = packed.reshape(-1)
    out, pos = [], 0
    for s in shapes:
        size = 1
        for dim in s:
            size *= dim
        out.append(flat[pos:pos + size].reshape(s))
        pos += size
    return out


def _cols_from_slabs(stack):
    return jnp.transpose(stack, (1, 0, 2)).reshape(stack.shape[1], -1)


def kernel(x, meta_tokens, ffn1_norm, ffn1_w_gu, ffn1_w_down, mix_norm, w_in, a_conv_w, a_log_rate, a_dt_bias, a_out_norm, b_shift_mu, b_w0, b_w_up, b_a0, b_a_up, b_g_up, b_k_k, b_k_a, b_r_k, b_ln_gain, b_ln_bias, w_out, ffn2_norm, ffn2_w_gu, ffn2_w_down, final_norm, loss_target, m_meta_tokens, m_ffn1_norm, m_ffn1_w_gu, m_ffn1_w_down, m_mix_norm, m_w_in, m_a_conv_w, m_a_log_rate, m_a_dt_bias, m_a_out_norm, m_b_shift_mu, m_b_w0, m_b_w_up, m_b_a0, m_b_a_up, m_b_g_up, m_b_k_k, m_b_k_a, m_b_r_k, m_b_ln_gain, m_b_ln_bias, m_w_out, m_ffn2_norm, m_ffn2_w_gu, m_ffn2_w_down, m_final_norm, v_meta_tokens, v_ffn1_norm, v_ffn1_w_gu, v_ffn1_w_down, v_mix_norm, v_w_in, v_a_conv_w, v_a_log_rate, v_a_dt_bias, v_a_out_norm, v_b_shift_mu, v_b_w0, v_b_w_up, v_b_a0, v_b_a_up, v_b_g_up, v_b_k_k, v_b_k_a, v_b_r_k, v_b_ln_gain, v_b_ln_bias, v_w_out, v_ffn2_norm, v_ffn2_w_gu, v_ffn2_w_down, v_final_norm):
    names = ['meta_tokens', 'ffn1_norm', 'ffn1_w_gu', 'ffn1_w_down', 'mix_norm', 'w_in', 'a_conv_w', 'a_log_rate',
             'a_dt_bias', 'a_out_norm', 'b_shift_mu', 'b_w0', 'b_w_up', 'b_a0', 'b_a_up', 'b_g_up', 'b_k_k', 'b_k_a',
             'b_r_k', 'b_ln_gain', 'b_ln_bias', 'w_out', 'ffn2_norm', 'ffn2_w_gu', 'ffn2_w_down', 'final_norm']
    env = dict(locals())
    wts = {k: env[k] for k in names}
    mom_m = {k: env['m_' + k] for k in names}
    mom_v = {k: env['v_' + k] for k in names}
    big = ['ffn1_w_gu', 'ffn1_w_down', 'w_in', 'w_out', 'ffn2_w_gu', 'ffn2_w_down']
    col_sharded = {'ffn1_w_gu', 'w_in', 'ffn2_w_gu'}
    shard_of = lambda tree, k: tree[k][0].T if k in col_sharded else tree[k][0]
    small_sharded = ['meta_tokens', 'a_conv_w', 'b_w_up', 'b_a_up', 'b_g_up']
    replicated = [k for k in names if k not in big and k not in small_sharded]

    seq, d = x.shape[1], x.shape[2]
    rows = PAD + N_META + seq
    heads_a = d // HEAD_A
    tb_mm = _tb(rows, 416)
    tb_vjp = _tb(rows, 208)
    tb_dw = _tb(rows, 2080)
    tb_ffn = _tb(rows, 832)
    me = _index(_place())

    local_bf = {k: shard_of(wts, k).astype(BF16) for k in big}
    gu1, down1, meta = _all_gather("gather_ffn1", [local_bf['ffn1_w_gu'], local_bf['ffn1_w_down'], wts['meta_tokens']])
    small_rest = small_sharded[1:]
    late_keys = ['w_out', 'ffn2_w_gu', 'ffn2_w_down']
    gather_mid = _exchange_start("gather_start_mid", [local_bf['w_in'], _pack([wts[k][0] for k in small_rest])],
                                 after=gu1, gather=True)
    gather_late = _exchange_start("gather_start_late", [local_bf[k] for k in late_keys], after=gather_mid[-1],
                                  gather=True)

    def finish_gather(tag, started, after):
        sems, mine, lands, _ = started
        mine, lands = _exchange_wait("gather_wait_" + tag, sems, mine, lands, after, gather=True)
        return [lax.dynamic_update_index_in_dim(land, own[None], me, 0) for land, own in zip(lands, mine)]

    full = {'ffn1_w_gu': gu1, 'ffn1_w_down': down1.reshape(-1, d), 'meta_tokens': _cols_from_slabs(meta)}
    for k in replicated:
        full[k] = wts[k].reshape(1, -1)

    h0 = jnp.concatenate([jnp.zeros((PAD, d), F32) + gather_late[-1][:1, :1], full['meta_tokens'], x[0]], axis=0)
    h1 = _ffn_fwd("ffn1_fwd", h0, full['ffn1_norm'], full['ffn1_w_gu'], full['ffn1_w_down'], tb_ffn)
    (u,) = _tok_fwd("mix_norm_fwd", _norm_fn, [h1], [full['mix_norm']], [(d, BF16)], tb_mm)

    win_stack, small_stack = finish_gather("mid", gather_mid, u)
    full['w_in'] = win_stack.reshape(-1, d)
    small_flat, pos = small_stack.reshape(N_DEV, -1), 0
    for k in small_rest:
        shape = wts[k][0].shape
        full[k] = _cols_from_slabs(small_flat[:, pos:pos + shape[0] * shape[1]].reshape((N_DEV,) + shape))
        pos += shape[0] * shape[1]

    win = full['w_in']
    n_b = 3 * d + LORA_W + LORA_A + LORA_G
    off_beta, off_b = 4 * d, 4 * d + 2 * heads_a
    off_ga = off_b + n_b
    b_width = 3 * d + LORA_PAD
    zrows = lambda r: jnp.zeros((r, d), BF16)
    w_qkv = win[:3 * d]
    w_zg = jnp.concatenate([win[3 * d:4 * d], win[off_ga:off_ga + 2 * d]], axis=0)
    w_b = jnp.concatenate([win[off_b:off_b + n_b], zrows(b_width - n_b)], axis=0)
    w_bg = jnp.concatenate([win[off_beta:off_beta + 2 * heads_a], zrows(128 - 2 * heads_a)], axis=0)

    def lanes(vec, start, width):
        return jnp.pad(vec.reshape(1, -1), ((0, 0), (start, width - start - vec.size)))

    log_rate = lanes(wts['a_log_rate'], heads_a, 128)
    dt_bias = lanes(wts['a_dt_bias'], heads_a, 128)
    mu = lanes(wts['b_shift_mu'], 0, b_width)
    w_up = jnp.pad(full['b_w_up'], ((0, 128 - LORA_W), (0, 0)))
    a_up = jnp.pad(full['b_a_up'], ((LORA_W, 0), (0, 0)))
    g_up = jnp.pad(full['b_g_up'], ((0, 256 - LORA_G), (0, 0)))
    b_pars = [full['b_w0'], w_up, full['b_a0'], a_up, g_up, full['b_k_k'], full['b_k_a']]
    post_pars = [full['a_out_norm'], full['b_r_k'], full['b_ln_gain'], full['b_ln_bias']]
    bg_fn = _make_bg_fn(heads_a)

    z_qkv = _mm("in_qkv", u, w_qkv, trans_b=True, tb=tb_mm, tn=_col_tile(3 * d, 1536))
    z_zg = _mm("in_zg", u, w_zg, trans_b=True, tb=tb_mm, tn=_col_tile(3 * d, 1536))
    z_b = _mm("in_b", u, w_b, trans_b=True, tb=tb_mm, tn=_col_tile(b_width, 1536))
    z_bg = _mm("in_bg", u, w_bg, trans_b=True, tb=tb_mm, tn=128)
    qkv = _a_pre_fwd(z_qkv, full['a_conv_w'])
    (bg,) = _tok_fwd("bg_fwd", bg_fn, [z_bg], [log_rate, dt_bias], [(128, F32)], tb_mm)
    o_dn, dn_hist = _dn_fwd(qkv, bg)
    zf = _shift_fwd(z_b, mu)
    rr, ww, kk2, vv, av, bv, gate = _tok_fwd("b_pre_fwd", _b_pre_fn, [zf], b_pars, [(d, F32)] * 7, tb_vjp)
    per_head = lambda t: t.reshape(rows, d // HEAD_B, HEAD_B)
    y_heads, b_hist, b_last = _rwkv_fwd(rr, ww, kk2, per_head(vv), av, bv)
    y_b = y_heads.reshape(rows, d)
    w_out_stack, full['ffn2_w_gu'], down2 = finish_gather("late", gather_late, y_heads)
    full['w_out'], full['ffn2_w_down'] = w_out_stack.reshape(-1, d), down2.reshape(-1, d)
    post_toks = [o_dn, z_zg, y_b, rr, kk2, vv, gate]
    (merged,) = _tok_fwd("post_fwd", _post_fn, post_toks, post_pars, [(d, BF16)], tb_vjp)
    h2 = _mm("out_proj", merged, full['w_out'], add=h1, tb=tb_mm, tn=d)
    h3 = _ffn_fwd("ffn2_fwd", h2, full['ffn2_norm'], full['ffn2_w_gu'], full['ffn2_w_down'], tb_ffn)

    target = jnp.pad(loss_target[0], ((CHUNK, 0), (0, 0)))
    dh3, g_final, loss_part = _loss(h3, target, full['final_norm'].reshape(1, d), tb_vjp)

    def ffn_backward(tag, h, dout, key_norm, key_gu, key_down):
        dh, dh_bf, dgain, xn, act, dgate, dup, dhalf = _ffn_bwd(tag + "_bwd", h, full[key_norm], dout, full[key_gu],
                                                                full[key_down], tb_mm)
        fc = dgate.shape[2]
        d_gu = jnp.concatenate([_mm_tn_from_slabs(tag + "_dw_gate", dgate, xn, tk=tb_dw).reshape(-1, fc, d),
                                _mm_tn_from_slabs(tag + "_dw_up", dup, xn, tk=tb_dw).reshape(-1, fc, d)], axis=0)
        d_down = _mm_tn_from_slabs(tag + "_dw_down", act, dhalf, tk=tb_dw).reshape(N_DEV, -1, d)
        return dh, dh_bf, dgain, d_gu, d_down

    dh2, dh2_bf, g_ffn2_norm, g_ffn2_gu, g_ffn2_down = ffn_backward("ffn2", h2, dh3, 'ffn2_norm', 'ffn2_w_gu',
                                                                    'ffn2_w_down')
    g_w_out = _mm_tn("dw_out", merged, dh2_bf, tm=d, tn=d, tk=tb_dw).reshape(N_DEV, -1, d)

    def start_exchange(tag, keys, slabs, after=None):
        sems, kept, lands, token = _exchange_start("exchange_start_" + tag, slabs, after)
        return (tag, keys, sems, kept, lands), token

    ex_ffn2, token_ffn2 = start_exchange("ffn2", ['ffn2_w_gu', 'ffn2_w_down', 'w_out'], [g_ffn2_gu, g_ffn2_down, g_w_out])
    dmerged = _mm("d_merged", dh2_bf, full['w_out'], trans_b=True, after=token_ffn2, tb=tb_mm, tn=d)
    post_grads = _tok_bwd("post_bwd", _post_fn, post_toks, post_pars, [[dmerged]], list(range(7)), tb_vjp,
                          [F32, BF16] + [F32] * 5)
    do_dn, dz_zg, dy_b, dr1, dk1, dv1, dgate = post_grads[:7]
    g_out_norm, g_r_k, g_ln_g, g_ln_b = post_grads[7:]
    dr2, dw2, dk2, dv_heads, da2, db2 = _rwkv_bwd(rr, ww, kk2, per_head(vv), av, bv, b_hist, b_last, per_head(dy_b))
    dv2 = dv_heads.reshape(rows, d)
    b_grads = _tok_bwd("b_pre_bwd", _b_pre_fn, [zf], b_pars,
                       [[dr1, dr2], [dw2], [dk1, dk2], [dv1, dv2], [da2], [db2], [dgate]], [0], tb_vjp)
    dzf = b_grads[0]
    g_w0, g_w_up, g_a0, g_a_up, g_g_up, g_k_k, g_k_a = b_grads[1:]
    dz_b, g_mu = _shift_bwd(z_b, mu, dzf)
    dqkv, dbg = _dn_bwd(qkv, bg, dn_hist, do_dn)
    dz_qkv, g_conv = _a_pre_bwd(z_qkv, full['a_conv_w'], dqkv)
    dz_bg, g_log_rate, g_dt_bias = _tok_bwd("bg_bwd", bg_fn, [z_bg], [log_rate, dt_bias], [[dbg]], [0], tb_mm, [BF16])

    du = None
    g_w_in_parts = []
    for tag, dz, wpiece in (("qkv", dz_qkv, w_qkv), ("zg", dz_zg, w_zg), ("b", dz_b, w_b), ("bg", dz_bg, w_bg)):
        du = _mm("du_" + tag, dz, wpiece, add=du, tb=tb_mm, tn=d)
        g_w_in_parts.append(_mm_tn("dw_in_" + tag, dz, u, tm=_col_tile(dz.shape[1], 1536), tn=d, tk=tb_dw))
    gp_qkv, gp_zg, gp_b, gp_bg = g_w_in_parts
    g_w_in = jnp.concatenate([gp_qkv, gp_zg[:d], gp_bg[:2 * heads_a], gp_b[:n_b], gp_zg[d:]],
                             axis=0).reshape(N_DEV, -1, d)
    ex_w_in, token_w_in = start_exchange("w_in", ['w_in'], [g_w_in])
    dh1, g_mix_norm = _tok_bwd("mix_norm_bwd", _norm_res_fn, [h1], [full['mix_norm']], [[du], [dh2]], [0], tb_vjp,
                               after=token_w_in)
    dh0, _, g_ffn1_norm, g_ffn1_gu, g_ffn1_down = ffn_backward("ffn1", h0, dh1, 'ffn1_norm', 'ffn1_w_gu', 'ffn1_w_down')

    small_full = {
        'meta_tokens': dh0[PAD:CHUNK], 'ffn1_norm': g_ffn1_norm, 'mix_norm': g_mix_norm, 'a_conv_w': g_conv,
        'a_log_rate': g_log_rate[:, heads_a:2 * heads_a], 'a_dt_bias': g_dt_bias[:, heads_a:2 * heads_a],
        'a_out_norm': g_out_norm, 'b_shift_mu': g_mu[:, :n_b], 'b_w0': g_w0, 'b_w_up': g_w_up[:LORA_W],
        'b_a0': g_a0, 'b_a_up': g_a_up[LORA_W:], 'b_g_up': g_g_up[:LORA_G], 'b_k_k': g_k_k, 'b_k_a': g_k_a,
        'b_r_k': g_r_k, 'b_ln_gain': g_ln_g, 'b_ln_bias': g_ln_b, 'ffn2_norm': g_ffn2_norm, 'final_norm': g_final,
    }
    small_names = list(small_full)
    packed = _pack([small_full[k] for k in small_names] + [loss_part[:, :1]])
    (all_parts,) = _all_gather("gather_small_grads", [packed])
    ex_ffn1, _ = start_exchange("ffn1", ['ffn1_w_gu', 'ffn1_w_down'], [g_ffn1_gu, g_ffn1_down], after=all_parts)
    summed = _sum_slabs("sum_small_grads", all_parts, packed.shape[0])
    pieces = _unpack(summed, [small_full[k].shape for k in small_names] + [(1, 1)])
    small_grad = dict(zip(small_names, pieces[:-1]))
    loss = pieces[-1].reshape(())

    grads, deltas, new_m, new_v = {}, {}, {}, {}
    local_small = {}
    for k in small_names:
        g = small_grad[k]
        if k in small_sharded:
            width = wts[k].shape[-1]
            g = lax.dynamic_slice_in_dim(g, me * width, width, axis=1)
        local_small[k] = g.reshape(wts[k].shape)
    pk = lambda tree: _pack([tree[k] for k in small_names])
    dl_s, m_s, v_s = _adamw_small(pk(wts), pk(local_small), pk(mom_m), pk(mom_v))
    shapes = [wts[k].shape for k in small_names]
    for k, dl, m2, v2 in zip(small_names, _unpack(dl_s, shapes), _unpack(m_s, shapes), _unpack(v_s, shapes)):
        grads[k], deltas[k], new_m[k], new_v[k] = local_small[k], dl, m2, v2

    done = dl_s
    for tag, keys, sems, kept, lands in (ex_ffn2, ex_w_in, ex_ffn1):
        kept, lands = _exchange_wait("exchange_wait_" + tag, sems, kept, lands, done)
        for k, slabs, landed in zip(keys, kept, lands):
            own = lax.dynamic_index_in_dim(slabs, me, axis=0, keepdims=False)
            res = _adamw("adamw_" + k, own, landed, shard_of(wts, k), shard_of(mom_m, k), shard_of(mom_v, k))
            done = res[1]
            res = [(t.T if k in col_sharded else t)[None] for t in res]
            grads[k], deltas[k], new_m[k], new_v[k] = res

    grad_x = dh0[CHUNK:][None]
    return (loss, grad_x, *[grads[k] for k in names], *[deltas[k] for k in names],
            *[new_m[k] for k in names], *[new_v[k] for k in names])
```

```python
import functools

import jax
import jax.numpy as jnp
from jax import lax
from jax.experimental import pallas as pl
from jax.experimental.pallas import tpu as pltpu

F32 = jnp.float32
BF16 = jnp.bfloat16
N_DEV = 8
N_META = 16
CHUNK = 64
PAD = CHUNK - N_META
HEAD_A = 128
HEAD_B = 64
LORA_W, LORA_A, LORA_G = 64, 64, 160
LORA_PAD = 384
EPS = 1e-6
GN_EPS = HEAD_B * 1e-5
ADAM_LR, ADAM_B1, ADAM_B2, ADAM_EPS, ADAM_WD, ADAM_STEP = 0.001, 0.9, 0.999, 1e-08, 0.01, 10
SCAN_STEPS = 16
MXU_WIDTH = 256
VMEM_LIMIT = 56 * 1024 * 1024
DN_PRECISION = lax.Precision.HIGH
MESH_ID = pl.DeviceIdType.MESH
ANY = pl.BlockSpec(memory_space=pl.ANY)
HBM_SPEC = pl.BlockSpec(memory_space=pltpu.HBM)
SEM_SPEC = pl.BlockSpec(memory_space=pltpu.SEMAPHORE)


def _cp(*sem):
    return pltpu.CompilerParams(dimension_semantics=sem, vmem_limit_bytes=VMEM_LIMIT)


def _tb(t, target):
    best = 16
    for d in range(16, target + 1, 16):
        if t % d == 0:
            best = d
    return best


def _sigmoid(x):
    return 1.0 / (1.0 + jnp.exp(-x))


def _silu(x):
    return x * _sigmoid(x)


def _softplus(x):
    return jnp.maximum(x, 0.0) + jnp.log(1.0 + jnp.exp(-jnp.abs(x)))


def _dot_nt(a, b, precision=None):
    return lax.dot_general(a, b, (((1,), (1,)), ((), ())), preferred_element_type=F32, precision=precision)


def _dot_tn(a, b, precision=None):
    return lax.dot_general(a, b, (((0,), (0,)), ((), ())), preferred_element_type=F32, precision=precision)


def _dot(a, b, precision=None):
    return jnp.dot(a, b, preferred_element_type=F32, precision=precision)


def _block_diag_ones():
    i = lax.broadcasted_iota(jnp.int32, (MXU_WIDTH, MXU_WIDTH), 0) // HEAD_B
    j = lax.broadcasted_iota(jnp.int32, (MXU_WIDTH, MXU_WIDTH), 1) // HEAD_B
    return (i == j).astype(BF16)


def _hi_lo(x):
    hi = x.astype(BF16)
    return hi, (x - hi.astype(F32)).astype(BF16)


def _segsum_many(xs, bd):
    groups = [x if isinstance(x, tuple) else _hi_lo(x) for x in xs]
    rows = groups[0][0].shape[0]
    stacked = jnp.concatenate([p for grp in groups for p in grp], axis=0)
    out = jnp.concatenate([_dot(stacked[:, s:s + MXU_WIDTH], bd) for s in range(0, stacked.shape[1], MXU_WIDTH)], axis=1)
    res, pos = [], 0
    for grp in groups:
        acc = out[pos:pos + rows]
        for j in range(1, len(grp)):
            acc = acc + out[pos + j * rows:pos + (j + 1) * rows]
        res.append(acc)
        pos += len(grp) * rows
    return res


def _segsum_impl(x):
    return _segsum_many([x], _block_diag_ones())[0]


@jax.custom_vjp
def _segsum64(x):
    return _segsum_impl(x)


_segsum64.defvjp(lambda x: (_segsum_impl(x), None), lambda _, ct: (_segsum_impl(ct),))


def _tok(t):
    return t if isinstance(t, tuple) else (t, t.shape[1], 0)


def _tok_spec(tb, width, colblk):
    return pl.BlockSpec((tb, width), lambda i: (i, colblk))


def _par_spec(p):
    return pl.BlockSpec(p.shape, lambda i: (0, 0))


def _tok_fwd(name, fn, toks, pars, outs, tb):
    toks = [_tok(t) for t in toks]
    rows = toks[0][0].shape[0]
    n_in = len(toks) + len(pars)

    def body(*refs):
        row0 = pl.program_id(0) * tb
        res = fn(row0, *[r[...] for r in refs[:n_in]])
        for r, o in zip(refs[n_in:], res):
            r[...] = o.astype(r.dtype)

    return pl.pallas_call(
        body, name=name, grid=(rows // tb,),
        in_specs=[_tok_spec(tb, w, c) for _, w, c in toks] + [_par_spec(p) for p in pars],
        out_specs=[_tok_spec(tb, w, 0) for w, _ in outs],
        out_shape=[jax.ShapeDtypeStruct((rows, w), dt) for w, dt in outs],
        compiler_params=_cp("parallel"),
    )(*[a for a, _, _ in toks], *pars)


def _tok_bwd(name, fn, toks, pars, cts, want, tb, want_dtypes=None, after=None):
    toks = [_tok(t) for t in toks]
    want_dtypes = want_dtypes or [F32] * len(want)
    cts = [[_tok(c) for c in group] for group in cts]
    flat_cts = [c for group in cts for c in group]
    rows = toks[0][0].shape[0]
    n_tok, n_par, n_ct = len(toks), len(pars), len(flat_cts)
    extra = [] if after is None else [after]

    def body(*refs):
        i = pl.program_id(0)
        row0 = i * tb
        prim = [r[...].astype(F32) for r in refs[:n_tok + n_par]]
        ct_refs = list(refs[n_tok + n_par:n_tok + n_par + n_ct])
        out_refs = refs[n_tok + n_par + n_ct + len(extra):]
        res, vjp = jax.vjp(lambda *a: fn(row0, *a), *prim)
        ct = []
        for group, o in zip(cts, res):
            acc = None
            for _ in group:
                v = ct_refs.pop(0)[...].astype(F32)
                acc = v if acc is None else acc + v
            ct.append(acc.astype(o.dtype))
        grads = vjp(tuple(ct))
        for r, k in zip(out_refs[:len(want)], want):
            r[...] = grads[k].astype(r.dtype)

        @pl.when(i == 0)
        def _():
            for r in out_refs[len(want):]:
                r[...] = jnp.zeros_like(r)

        for r, g in zip(out_refs[len(want):], grads[n_tok:]):
            r[...] += g

    return pl.pallas_call(
        body, name=name, grid=(rows // tb,),
        in_specs=[_tok_spec(tb, w, c) for _, w, c in toks] + [_par_spec(p) for p in pars]
        + [_tok_spec(tb, w, c) for _, w, c in flat_cts] + [ANY] * len(extra),
        out_specs=[_tok_spec(tb, toks[k][1], 0) for k in want] + [_par_spec(p) for p in pars],
        out_shape=[jax.ShapeDtypeStruct((rows, toks[k][1]), dt) for k, dt in zip(want, want_dtypes)]
        + [jax.ShapeDtypeStruct(p.shape, F32) for p in pars],
        compiler_params=_cp("arbitrary"),
    )(*[a for a, _, _ in toks], *pars, *[a for a, _, _ in flat_cts], *extra)


def _mm(name, a, b, *, trans_b=False, add=None, after=None, tb, tn):
    rows, k = a.shape
    n = b.shape[0] if trans_b else b.shape[1]

    def body(*refs):
        a_ref, b_ref = refs[:2]
        o_ref = refs[-1]
        acc = _dot_nt(a_ref[...], b_ref[...]) if trans_b else _dot(a_ref[...], b_ref[...])
        if add is not None:
            acc = acc + refs[2][...]
        o_ref[...] = acc

    in_specs = [pl.BlockSpec((tb, k), lambda i, j: (i, 0)),
                pl.BlockSpec((tn, k), lambda i, j: (j, 0)) if trans_b else pl.BlockSpec((k, tn), lambda i, j: (0, j))]
    args = [a, b]
    if add is not None:
        in_specs.append(pl.BlockSpec((tb, tn), lambda i, j: (i, j)))
        args.append(add)
    if after is not None:
        in_specs.append(ANY)
        args.append(after)
    return pl.pallas_call(
        body, name=name, grid=(rows // tb, n // tn), in_specs=in_specs,
        out_specs=pl.BlockSpec((tb, tn), lambda i, j: (i, j)),
        out_shape=jax.ShapeDtypeStruct((rows, n), F32),
        compiler_params=_cp("parallel", "parallel"),
    )(*args)


def _mm_tn_call(name, grid, a, b, a_spec, b_spec, o_spec, acc_shape, out_shape):
    last = len(grid) - 1

    def body(a_ref, b_ref, o_ref, acc_ref):
        k = pl.program_id(last)

        @pl.when(k == 0)
        def _():
            acc_ref[...] = jnp.zeros_like(acc_ref)

        a_blk = a_ref[0] if len(a_ref.shape) == 3 else a_ref[...]
        b_blk = b_ref[0] if len(b_ref.shape) == 3 else b_ref[...]
        acc_ref[...] += _dot_tn(a_blk, b_blk)

        @pl.when(k == grid[last] - 1)
        def _():
            if len(o_ref.shape) == 3:
                o_ref[0] = acc_ref[...].astype(o_ref.dtype)
            else:
                o_ref[...] = acc_ref[...].astype(o_ref.dtype)

    return pl.pallas_call(
        body, name=name, grid=grid, in_specs=[a_spec, b_spec], out_specs=o_spec,
        out_shape=jax.ShapeDtypeStruct(out_shape, BF16), scratch_shapes=[pltpu.VMEM(acc_shape, F32)],
        compiler_params=_cp(*(["parallel"] * last + ["arbitrary"])),
    )(a, b)


def _mm_tn(name, a, b, *, tm, tn, tk):
    rows, m = a.shape
    n = b.shape[1]
    return _mm_tn_call(name, (m // tm, n // tn, rows // tk), a, b,
                       pl.BlockSpec((tk, tm), lambda i, j, k: (k, i)), pl.BlockSpec((tk, tn), lambda i, j, k: (k, j)),
                       pl.BlockSpec((tm, tn), lambda i, j, k: (i, j)), (tm, tn), (m, n))


def _mm_tn_from_slabs(name, a3, b, *, tk):
    s, rows, c = a3.shape
    n = b.shape[1]
    return _mm_tn_call(name, (s, rows // tk), a3, b,
                       pl.BlockSpec((1, tk, c), lambda i, k: (i, k, 0)), pl.BlockSpec((tk, n), lambda i, k: (k, 0)),
                       pl.BlockSpec((c, n), lambda i, k: (i, 0)), (c, n), (s * c, n))


def _col_tile(n, target):
    if n <= target:
        return n
    best = 128
    for d in range(128, target + 1, 128):
        if n % d == 0:
            best = d
    return best


def _rms(x, gain):
    return x * lax.rsqrt(jnp.mean(x * x, axis=-1, keepdims=True) + EPS) * gain


def _ffn_specs(d, fc, nj):
    return [pl.BlockSpec((1, fc, d), lambda i, j: (j, 0, 0)), pl.BlockSpec((1, fc, d), lambda i, j: (nj + j, 0, 0)),
            pl.BlockSpec((fc, d), lambda i, j: (j, 0))]


def _ffn_fwd(name, h, gain, wgu, wd, tb):
    rows, d = h.shape
    nj = wgu.shape[0] // 2
    fc = wgu.shape[1]

    def body(h_ref, g_ref, wg_ref, wu_ref, wd_ref, o_ref, xn_s, acc_s):
        j = pl.program_id(1)

        @pl.when(j == 0)
        def _():
            xn_s[...] = _rms(h_ref[...], g_ref[...]).astype(BF16)
            acc_s[...] = jnp.zeros_like(acc_s)

        wg, wu, wdn = wg_ref[0], wu_ref[0], wd_ref[...]
        for half in range(2):
            rs = pl.ds(half * (tb // 2), tb // 2)
            xn = xn_s[rs, :]
            gate = _dot_nt(xn, wg)
            up = _dot_nt(xn, wu)
            acc_s[rs, :] += _dot((_silu(gate) * up).astype(BF16), wdn)

        @pl.when(j == nj - 1)
        def _():
            o_ref[...] = h_ref[...] + 0.5 * acc_s[...]

    return pl.pallas_call(
        body, name=name, grid=(rows // tb, nj),
        in_specs=[pl.BlockSpec((tb, d), lambda i, j: (i, 0)), pl.BlockSpec((1, d), lambda i, j: (0, 0))]
        + _ffn_specs(d, fc, nj),
        out_specs=pl.BlockSpec((tb, d), lambda i, j: (i, 0)),
        out_shape=jax.ShapeDtypeStruct((rows, d), F32),
        scratch_shapes=[pltpu.VMEM((tb, d), BF16), pltpu.VMEM((tb, d), F32)],
        compiler_params=_cp("parallel", "arbitrary"),
    )(h, gain, wgu, wgu, wd)


def _ffn_bwd(name, h, gain, dout, wgu, wd, tb):
    rows, d = h.shape
    nj = wgu.shape[0] // 2
    fc = wgu.shape[1]

    def body(h_ref, g_ref, do_ref, wg_ref, wu_ref, wd_ref,
             dh_ref, dhb_ref, dg_ref, xn_ref, act_ref, dgate_ref, dup_ref, dhalf_ref, dxn_s):
        i, j = pl.program_id(0), pl.program_id(1)

        @pl.when(j == 0)
        def _():
            xn_ref[...] = _rms(h_ref[...], g_ref[...]).astype(BF16)
            dhalf_ref[...] = (0.5 * do_ref[...]).astype(BF16)
            dxn_s[...] = jnp.zeros_like(dxn_s)

        wg, wu, wdn = wg_ref[0], wu_ref[0], wd_ref[...]
        for half in range(2):
            rs = pl.ds(half * (tb // 2), tb // 2)
            xn = xn_ref[rs, :]
            gate = _dot_nt(xn, wg)
            up = _dot_nt(xn, wu)
            sg = _sigmoid(gate)
            dact = _dot_nt(dhalf_ref[rs, :], wdn)
            act_ref[0, rs, :] = (gate * sg * up).astype(BF16)
            dgate = (dact * up * (sg * (1.0 + gate * (1.0 - sg)))).astype(BF16)
            dup = (dact * gate * sg).astype(BF16)
            dgate_ref[0, rs, :] = dgate
            dup_ref[0, rs, :] = dup
            dxn_s[rs, :] += _dot(dgate, wg) + _dot(dup, wu)

        @pl.when((i == 0) & (j == 0))
        def _():
            dg_ref[...] = jnp.zeros_like(dg_ref)

        @pl.when(j == nj - 1)
        def _():
            x = h_ref[...]
            r = lax.rsqrt(jnp.mean(x * x, axis=-1, keepdims=True) + EPS)
            dxn = dxn_s[...]
            dyg = dxn * g_ref[...]
            dh = do_ref[...] + r * dyg - x * (r * r * r) * jnp.mean(dyg * x, axis=-1, keepdims=True)
            dh_ref[...] = dh
            dhb_ref[...] = dh.astype(BF16)
            dg_ref[...] += jnp.sum(dxn * x * r, axis=0, keepdims=True)

    row_d = pl.BlockSpec((tb, d), lambda i, j: (i, 0))
    slab = pl.BlockSpec((1, tb, fc), lambda i, j: (j, i, 0))
    hidden = jax.ShapeDtypeStruct((nj, rows, fc), BF16)
    return pl.pallas_call(
        body, name=name, grid=(rows // tb, nj),
        in_specs=[row_d, pl.BlockSpec((1, d), lambda i, j: (0, 0)), row_d] + _ffn_specs(d, fc, nj),
        out_specs=[row_d, row_d, pl.BlockSpec((1, d), lambda i, j: (0, 0)), row_d, slab, slab, slab, row_d],
        out_shape=[jax.ShapeDtypeStruct((rows, d), F32), jax.ShapeDtypeStruct((rows, d), BF16),
                   jax.ShapeDtypeStruct((1, d), F32), jax.ShapeDtypeStruct((rows, d), BF16),
                   hidden, hidden, hidden, jax.ShapeDtypeStruct((rows, d), BF16)],
        scratch_shapes=[pltpu.VMEM((tb, d), F32)],
        compiler_params=_cp("arbitrary", "arbitrary"),
    )(h, gain, dout, wgu, wgu, wd)


def _shift_rows(x, s):
    return pltpu.roll(x, s % x.shape[0], 0)


def _a_post(c, which):
    s = _silu(c)
    n = s * lax.rsqrt(jnp.sum(s * s, axis=-1, keepdims=True) + 1e-6)
    scale = jnp.where(which == 0, HEAD_A ** -0.5, 1.0)
    return jnp.where(which == 2, s, n * scale)


def _conv(x, w):
    return x * w[3:4] + _shift_rows(x, 1) * w[2:3] + _shift_rows(x, 2) * w[1:2] + _shift_rows(x, 3) * w[0:1]


def _a_pre_fwd(zqkv, conv_w):
    rows, width = zqkv.shape
    heads = width // (3 * HEAD_A)

    def body(x_ref, w_ref, o_ref):
        which = pl.program_id(0) // heads
        live = lax.broadcasted_iota(jnp.int32, (rows, HEAD_A), 0) >= PAD
        o_ref[...] = jnp.where(live, _a_post(_conv(x_ref[...], w_ref[...]), which), 0.0)

    return pl.pallas_call(
        body, name="a_pre_fwd", grid=(width // HEAD_A,),
        in_specs=[pl.BlockSpec((rows, HEAD_A), lambda c: (0, c)), pl.BlockSpec((4, HEAD_A), lambda c: (0, c))],
        out_specs=pl.BlockSpec((rows, HEAD_A), lambda c: (0, c)),
        out_shape=jax.ShapeDtypeStruct((rows, width), F32),
        compiler_params=_cp("parallel"),
    )(zqkv, conv_w)


def _a_pre_bwd(zqkv, conv_w, dqkv):
    rows, width = zqkv.shape
    heads = width // (3 * HEAD_A)

    def body(x_ref, w_ref, ct_ref, dx_ref, dw_ref):
        which = pl.program_id(0) // heads
        live = lax.broadcasted_iota(jnp.int32, (rows, HEAD_A), 0) >= PAD
        x, w = x_ref[...], w_ref[...]
        _, vjp = jax.vjp(lambda c: _a_post(c, which), _conv(x, w))
        (dc,) = vjp(jnp.where(live, ct_ref[...], 0.0))
        dc = jnp.where(live, dc, 0.0)
        dx_ref[...] = (dc * w[3:4] + _shift_rows(dc, -1) * w[2:3] + _shift_rows(dc, -2) * w[1:2]
                       + _shift_rows(dc, -3) * w[0:1]).astype(BF16)
        dw_ref[...] = jnp.concatenate(
            [jnp.sum(dc * (_shift_rows(x, 3 - j) if j < 3 else x), axis=0, keepdims=True) for j in range(4)], axis=0)

    col = pl.BlockSpec((rows, HEAD_A), lambda c: (0, c))
    wsp = pl.BlockSpec((4, HEAD_A), lambda c: (0, c))
    return pl.pallas_call(
        body, name="a_pre_bwd", grid=(width // HEAD_A,),
        in_specs=[col, wsp, col], out_specs=[col, wsp],
        out_shape=[jax.ShapeDtypeStruct((rows, width), BF16), jax.ShapeDtypeStruct((4, width), F32)],
        compiler_params=_cp("parallel"),
    )(zqkv, conv_w, dqkv)


SHIFT_TILE = 384


def _shift_fwd(zb, mu):
    rows, width = zb.shape

    def body(x_ref, mu_ref, o_ref):
        x = x_ref[...]
        first = lax.broadcasted_iota(jnp.int32, x.shape, 0) == 0
        prev = jnp.where(first, 0.0, _shift_rows(x, 1))
        o_ref[...] = x + (prev - x) * mu_ref[...]

    col = pl.BlockSpec((rows, SHIFT_TILE), lambda c: (0, c))
    return pl.pallas_call(
        body, name="shift_fwd", grid=(width // SHIFT_TILE,),
        in_specs=[col, pl.BlockSpec((1, SHIFT_TILE), lambda c: (0, c))], out_specs=col,
        out_shape=jax.ShapeDtypeStruct((rows, width), F32), compiler_params=_cp("parallel"),
    )(zb, mu)


def _shift_bwd(zb, mu, dzf):
    rows, width = zb.shape

    def body(x_ref, mu_ref, ct_ref, dx_ref, dmu_ref):
        x, ct, mu_v = x_ref[...], ct_ref[...], mu_ref[...]
        row = lax.broadcasted_iota(jnp.int32, x.shape, 0)
        prev = jnp.where(row == 0, 0.0, _shift_rows(x, 1))
        nxt = jnp.where(row == rows - 1, 0.0, _shift_rows(ct, -1))
        dx_ref[...] = (ct * (1.0 - mu_v) + nxt * mu_v).astype(BF16)
        dmu_ref[...] = jnp.sum(ct * (prev - x), axis=0, keepdims=True)

    col = pl.BlockSpec((rows, SHIFT_TILE), lambda c: (0, c))
    msp = pl.BlockSpec((1, SHIFT_TILE), lambda c: (0, c))
    return pl.pallas_call(
        body, name="shift_bwd", grid=(width // SHIFT_TILE,),
        in_specs=[col, msp, col], out_specs=[col, msp],
        out_shape=[jax.ShapeDtypeStruct((rows, width), BF16), jax.ShapeDtypeStruct((1, width), F32)],
        compiler_params=_cp("parallel"),
    )(zb, mu, dzf)


def _neumann_inverse(p):
    heads = range(len(p))
    eye = (lax.broadcasted_iota(jnp.int32, (CHUNK, CHUNK), 0)
           == lax.broadcasted_iota(jnp.int32, (CHUNK, CHUNK), 1)).astype(F32)
    tinv = [eye + p[h] for h in heads]
    for _ in range(5):
        p = [_dot(p[h], p[h], DN_PRECISION) for h in heads]
        tinv = [tinv[h] + _dot(tinv[h], p[h], DN_PRECISION) for h in heads]
    return tinv


@jax.custom_vjp
def _unit_lower_inverse(p):
    return _neumann_inverse(p)


def _unit_lower_inverse_fwd(p):
    tinv = _neumann_inverse(p)
    return tinv, tinv


def _unit_lower_inverse_bwd(tinv, ct):
    heads = range(len(tinv))
    left = [_dot_tn(tinv[h], ct[h], DN_PRECISION) for h in heads]
    return ([_dot_nt(left[h], tinv[h], DN_PRECISION) for h in heads],)


_unit_lower_inverse.defvjp(_unit_lower_inverse_fwd, _unit_lower_inverse_bwd)


@jax.custom_vjp
def _known_inverse(p, tinv):
    return tinv


_known_inverse.defvjp(lambda p, tinv: (tinv, tinv),
                      lambda tinv, ct: (_unit_lower_inverse_bwd(tinv, ct)[0], [jnp.zeros_like(t) for t in tinv]))


def _dn_chunk(q, k, v, beta, g, state, saved_tinv=None):
    heads = range(len(q))
    ri = lax.broadcasted_iota(jnp.int32, (CHUNK, CHUNK), 0)
    ci = lax.broadcasted_iota(jnp.int32, (CHUNK, CHUNK), 1)
    eye = (ri == ci).astype(F32)
    incl = ri >= ci
    last = lax.broadcasted_iota(jnp.int32, (CHUNK, 1), 0) == CHUNK - 1
    g_row = [jnp.sum(g[h] * eye, axis=0, keepdims=True) for h in heads]
    gc = [jnp.sum(jnp.where(incl, g_row[h], 0.0), axis=1, keepdims=True) for h in heads]
    gc_row = [jnp.sum(gc[h] * eye, axis=0, keepdims=True) for h in heads]
    decay = [jnp.where(incl, jnp.exp(jnp.where(incl, gc[h] - gc_row[h], 0.0)), 0.0) for h in heads]
    kb = [k[h] * beta[h] for h in heads]
    vb = [v[h] * beta[h] for h in heads]
    p = [-jnp.where(ri > ci, _dot_nt(kb[h], k[h]) * decay[h], 0.0) for h in heads]
    tinv = _unit_lower_inverse(p) if saved_tinv is None else _known_inverse(p, saved_tinv)
    eg = [jnp.exp(gc[h]) for h in heads]
    u = [_dot(tinv[h], vb[h]) for h in heads]
    wk = [_dot(tinv[h], kb[h] * eg[h]) for h in heads]
    attn = [_dot_nt(q[h], k[h]) * decay[h] for h in heads]
    g_last = [jnp.sum(jnp.where(last, gc[h], 0.0), axis=0, keepdims=True) for h in heads]
    k_tail = [k[h] * jnp.exp(g_last[h] - gc[h]) for h in heads]
    v_new = [u[h] - _dot(wk[h], state[h]) for h in heads]
    o = [_dot(q[h] * eg[h], state[h]) + _dot(attn[h], v_new[h]) for h in heads]
    new = [state[h] * jnp.exp(g_last[h]) + _dot_tn(k_tail[h], v_new[h]) for h in heads]
    return (o, new, tinv) if saved_tinv is None else (o, new)


def _bg_cols(bg, h, heads):
    lane = lax.broadcasted_iota(jnp.int32, bg.shape, 1)
    beta = jnp.sum(jnp.where(lane == h, bg, 0.0), axis=1, keepdims=True)
    g = jnp.sum(jnp.where(lane == heads + h, bg, 0.0), axis=1, keepdims=True)
    return beta, g


def _dn_fwd(qkv, bg):
    rows = qkv.shape[0]
    heads = qkv.shape[1] // (3 * HEAD_A)
    n = rows // CHUNK
    hp, groups = heads, 1

    def body(q_ref, k_ref, v_ref, bg_ref, o_ref, hist_ref, tinv_ref, s_ref):
        c, grp = pl.program_id(0), pl.program_id(1)

        @pl.when(c == 0)
        def _():
            for i in range(hp):
                s_ref[grp * hp + i] = jnp.zeros((HEAD_A, HEAD_A), F32)

        bg_v = bg_ref[...]
        cols = [slice(i * HEAD_A, (i + 1) * HEAD_A) for i in range(hp)]
        state = [s_ref[grp * hp + i] for i in range(hp)]
        beta_g = [_bg_cols(bg_v, grp * hp + i, heads) for i in range(hp)]
        o, new, tinv = _dn_chunk([q_ref[:, c_] for c_ in cols], [k_ref[:, c_] for c_ in cols],
                                 [v_ref[:, c_] for c_ in cols], [b for b, _ in beta_g], [g for _, g in beta_g], state)
        for i in range(hp):
            hist_ref[0, i] = state[i]
            tinv_ref[0, i] = tinv[i]
            o_ref[:, cols[i]] = o[i]
            s_ref[grp * hp + i] = new[i]

    def part(p):
        return pl.BlockSpec((CHUNK, hp * HEAD_A), lambda c, grp: (c, p * groups + grp))

    return pl.pallas_call(
        body, name="deltanet_fwd", grid=(n, groups),
        in_specs=[part(0), part(1), part(2), pl.BlockSpec((CHUNK, 128), lambda c, grp: (c, 0))],
        out_specs=[part(0), pl.BlockSpec((1, hp, HEAD_A, HEAD_A), lambda c, grp: (c, grp, 0, 0)),
                   pl.BlockSpec((1, hp, CHUNK, CHUNK), lambda c, grp: (c, grp, 0, 0))],
        out_shape=[jax.ShapeDtypeStruct((rows, heads * HEAD_A), F32),
                   jax.ShapeDtypeStruct((n, heads, HEAD_A, HEAD_A), F32),
                   jax.ShapeDtypeStruct((n, heads, CHUNK, CHUNK), F32)],
        scratch_shapes=[pltpu.VMEM((heads, HEAD_A, HEAD_A), F32)],
        compiler_params=_cp("arbitrary", "arbitrary"),
    )(qkv, qkv, qkv, bg)


def _dn_bwd(qkv, bg, hist, tinv_hist, do):
    rows = qkv.shape[0]
    heads = qkv.shape[1] // (3 * HEAD_A)
    n = rows // CHUNK
    hp, groups = heads, 1

    def body(q_ref, k_ref, v_ref, bg_ref, hist_ref, tinv_ref, do_ref, dqkv_ref, dbg_ref, ds_ref):
        c, grp = pl.program_id(0), pl.program_id(1)

        @pl.when(c == 0)
        def _():
            for i in range(hp):
                ds_ref[grp * hp + i] = jnp.zeros((HEAD_A, HEAD_A), F32)

        bg_v = bg_ref[...]
        lane = lax.broadcasted_iota(jnp.int32, (CHUNK, 128), 1)
        cols = [slice(i * HEAD_A, (i + 1) * HEAD_A) for i in range(hp)]
        beta_g = [_bg_cols(bg_v, grp * hp + i, heads) for i in range(hp)]
        _, vjp = jax.vjp(_dn_chunk, [q_ref[:, c_] for c_ in cols], [k_ref[:, c_] for c_ in cols],
                         [v_ref[:, c_] for c_ in cols], [b for b, _ in beta_g], [g for _, g in beta_g],
                         [hist_ref[0, i] for i in range(hp)], [tinv_ref[0, i] for i in range(hp)])
        dq, dk, dv, dbeta, dg, ds, _ = vjp(([do_ref[:, c_] for c_ in cols], [ds_ref[grp * hp + i] for i in range(hp)]))
        dbg = jnp.zeros((CHUNK, 128), F32)
        for i in range(hp):
            h = grp * hp + i
            for p, part_grad in enumerate((dq, dk, dv)):
                dqkv_ref[:, pl.ds((p * heads + i) * HEAD_A, HEAD_A)] = part_grad[i]
            ds_ref[h] = ds[i]
            dbg = dbg + jnp.where(lane == h, dbeta[i], 0.0) + jnp.where(lane == heads + h, dg[i], 0.0)

        @pl.when(grp == 0)
        def _():
            dbg_ref[...] = jnp.zeros_like(dbg_ref)

        dbg_ref[...] += dbg

    def part(p):
        return pl.BlockSpec((CHUNK, hp * HEAD_A), lambda c, grp: (n - 1 - c, p * groups + grp))

    return pl.pallas_call(
        body, name="deltanet_bwd", grid=(n, groups),
        in_specs=[part(0), part(1), part(2), pl.BlockSpec((CHUNK, 128), lambda c, grp: (n - 1 - c, 0)),
                  pl.BlockSpec((1, hp, HEAD_A, HEAD_A), lambda c, grp: (n - 1 - c, grp, 0, 0)),
                  pl.BlockSpec((1, hp, CHUNK, CHUNK), lambda c, grp: (n - 1 - c, grp, 0, 0)), part(0)],
        out_specs=[pl.BlockSpec((CHUNK, 3 * heads * HEAD_A), lambda c, grp: (n - 1 - c, 0)),
                   pl.BlockSpec((CHUNK, 128), lambda c, grp: (n - 1 - c, 0))],
        out_shape=[jax.ShapeDtypeStruct(qkv.shape, F32), jax.ShapeDtypeStruct((rows, 128), F32)],
        scratch_shapes=[pltpu.VMEM((heads, HEAD_A, HEAD_A), F32)],
        compiler_params=_cp("arbitrary", "arbitrary"),
    )(qkv, qkv, qkv, bg, hist, tinv_hist, do)


def _head_mask(heads, width):
    return (lax.broadcasted_iota(jnp.int32, (heads, width), 0)
            == lax.broadcasted_iota(jnp.int32, (heads, width), 1) // HEAD_B)


def _masked_rows(mask, row):
    return jnp.where(mask, row, 0.0).astype(BF16)


def _rwkv_fwd(r, w, k, v, a, b):
    rows, width = r.shape
    heads = width // HEAD_B
    ts = SCAN_STEPS

    def body(r_ref, w_ref, k_ref, v_ref, a_ref, b_ref, y_ref, hist_ref, s_ref):
        @pl.when(pl.program_id(0) == 0)
        def _():
            s_ref[...] = jnp.zeros_like(s_ref)

        mask = _head_mask(heads, width)
        onehot = mask.astype(BF16)
        onehot2 = jnp.concatenate([onehot, onehot], axis=0)
        bd = _block_diag_ones()

        spread_v = [_dot_tn(jnp.concatenate(_hi_lo(v_ref[j]), axis=0), onehot2) for j in range(ts)]
        s = s_ref[...]
        ys = []
        for j in range(ts):
            row = pl.ds(j, 1)
            hist_ref[j] = s
            (sa,) = _segsum_many([((s * a_ref[row, :]).astype(BF16),)], bd)
            s = s * w_ref[row, :] + sa * b_ref[row, :] + spread_v[j] * k_ref[row, :]
            ys.append(_dot_nt(_masked_rows(mask, r_ref[row, :]), s.astype(BF16)))
        for j in range(ts):
            y_ref[j] = ys[j]
        s_ref[...] = s

    blk = pl.BlockSpec((ts, width), lambda i: (i, 0))
    blk3 = pl.BlockSpec((ts, heads, HEAD_B), lambda i: (i, 0, 0))
    return pl.pallas_call(
        body, name="rwkv_fwd", grid=(rows // ts,),
        in_specs=[blk, blk, blk, blk3, blk, blk],
        out_specs=[blk3, pl.BlockSpec((ts, HEAD_B, width), lambda i: (i, 0, 0)),
                   pl.BlockSpec((HEAD_B, width), lambda i: (0, 0))],
        out_shape=[jax.ShapeDtypeStruct((rows, heads, HEAD_B), F32), jax.ShapeDtypeStruct((rows, HEAD_B, width), F32),
                   jax.ShapeDtypeStruct((HEAD_B, width), F32)],
        compiler_params=_cp("arbitrary"),
    )(r, w, k, v, a, b)


def _rwkv_bwd(r, w, k, v, a, b, hist, last, dy):
    rows, width = r.shape
    heads = width // HEAD_B
    ts = SCAN_STEPS
    nb = rows // ts

    def body(r_ref, w_ref, k_ref, v_ref, a_ref, b_ref, hist_ref, last_ref, dy_ref,
             dr_ref, dw_ref, dk_ref, dv_ref, da_ref, db_ref, g_ref, after_ref):
        @pl.when(pl.program_id(0) == 0)
        def _():
            g_ref[...] = jnp.zeros_like(g_ref)
            after_ref[...] = last_ref[...]

        mask = _head_mask(heads, width)
        onehot = mask.astype(BF16)
        bd = _block_diag_ones()

        def own_lanes(x):
            return jnp.sum(jnp.where(mask, x, 0.0), axis=0, keepdims=True)

        def colsum(x):
            return jnp.sum(x, axis=0, keepdims=True)

        dy_m = [dy_ref[j].astype(BF16) for j in range(ts)]
        spread_dy = [_dot_tn(dy_m[j], onehot) for j in range(ts)]
        state_after = [hist_ref[j + 1] if j < ts - 1 else after_ref[...] for j in range(ts)]
        dr = [own_lanes(_dot(dy_m[j], state_after[j].astype(BF16))) for j in range(ts)]
        sa_m = [_dot_nt(_masked_rows(mask, a_ref[pl.ds(j, 1), :]), hist_ref[j].astype(BF16)) for j in range(ts)]
        g = g_ref[...]
        dw, dk, db, da, dv = {}, {}, {}, {}, {}
        for j in reversed(range(ts)):
            row = pl.ds(j, 1)
            sp = hist_ref[j]
            g = g + spread_dy[j] * r_ref[row, :]
            (dsa,) = _segsum_many([((g * b_ref[row, :]).astype(BF16),)], bd)
            g_b = g.astype(BF16)
            both = _dot(jnp.concatenate([v_ref[j].astype(BF16), sa_m[j].astype(BF16)], axis=0), g_b)
            dk[j], db[j] = own_lanes(both[:heads]), own_lanes(both[heads:])
            dv[j] = _dot_nt(_masked_rows(mask, k_ref[row, :]), g_b)
            dw[j] = colsum(g * sp)
            da[j] = colsum(sp * dsa)
            g = g * w_ref[row, :] + dsa * a_ref[row, :]
        g_ref[...] = g
        after_ref[...] = hist_ref[0]
        for j in range(ts):
            dv_ref[j] = dv[j]
            for ref, vals in ((dr_ref, dr), (dw_ref, dw), (dk_ref, dk), (da_ref, da), (db_ref, db)):
                ref[pl.ds(j, 1), :] = vals[j]

    blk = pl.BlockSpec((ts, width), lambda i: (nb - 1 - i, 0))
    blk3 = pl.BlockSpec((ts, heads, HEAD_B), lambda i: (nb - 1 - i, 0, 0))
    state = pl.BlockSpec((HEAD_B, width), lambda i: (0, 0))
    return pl.pallas_call(
        body, name="rwkv_bwd", grid=(nb,),
        in_specs=[blk, blk, blk, blk3, blk, blk, pl.BlockSpec((ts, HEAD_B, width), lambda i: (nb - 1 - i, 0, 0)),
                  state, blk3],
        out_specs=[blk, blk, blk, blk3, blk, blk],
        out_shape=[jax.ShapeDtypeStruct((rows, width), F32)] * 3 + [jax.ShapeDtypeStruct((rows, heads, HEAD_B), F32)]
        + [jax.ShapeDtypeStruct((rows, width), F32)] * 2,
        scratch_shapes=[pltpu.VMEM((HEAD_B, width), F32), pltpu.VMEM((HEAD_B, width), F32)],
        compiler_params=_cp("arbitrary"),
    )(r, w, k, v, a, b, hist, last, dy)


def _live(row0, shape):
    return (row0 + lax.broadcasted_iota(jnp.int32, shape, 0)) >= PAD


def _norm_fn(row0, h, gain):
    return (_rms(h, gain),)


def _norm_res_fn(row0, h, gain):
    return _rms(h, gain), h


def _make_bg_fn(heads):
    def fn(row0, x, log_rate, dt_bias):
        lane = lax.broadcasted_iota(jnp.int32, x.shape, 1)
        beta = _sigmoid(x)
        g = -jnp.exp(log_rate) * _softplus(x + dt_bias)
        out = jnp.where(lane < heads, beta, jnp.where(lane < 2 * heads, g, 0.0))
        return (jnp.where(_live(row0, x.shape), out, 0.0),)
    return fn


def _b_pre_fn(row0, zf, w0, w_up, a0, a_up, g_up, k_k, k_a):
    d = w0.shape[1]
    r, k, v = zf[:, :d], zf[:, d:2 * d], zf[:, 2 * d:3 * d]
    lo = zf[:, 3 * d:3 * d + 128]
    lg = zf[:, 3 * d + 128:3 * d + LORA_PAD]
    lane = lax.broadcasted_iota(jnp.int32, lo.shape, 1)
    lw = _dot(jnp.where(lane < LORA_W, jnp.tanh(lo), 0.0), w_up)
    la = _dot(jnp.where(lane >= LORA_W, lo, 0.0), a_up)
    lane_g = lax.broadcasted_iota(jnp.int32, lg.shape, 1)
    gate = _dot(jnp.where(lane_g < LORA_G, _sigmoid(lg), 0.0), g_up)
    decay = jnp.exp(-jnp.exp(-_softplus(-(w0 + lw)) - 0.5))
    a = _sigmoid(a0 + la)
    kx = k * k_k
    kk = kx * lax.rsqrt(_segsum64(kx * kx) + 1e-6)
    k2 = k * (1.0 + (a - 1.0) * k_a)
    return r, decay, k2, v, -kk, kk * a, gate


def _post_fn(row0, o, zg, y, r, k2, v, gate, out_gain, r_k, ln_g, ln_b):
    d = o.shape[1]
    az, ga, gb = zg[:, :d], zg[:, d:2 * d], zg[:, 2 * d:]
    heads = d // HEAD_A
    parts = []
    for h in range(heads):
        oh = o[:, h * HEAD_A:(h + 1) * HEAD_A]
        parts.append(oh * lax.rsqrt(jnp.mean(oh * oh, axis=-1, keepdims=True) + EPS) * out_gain)
    o_a = jnp.concatenate(parts, axis=1) * _silu(az)
    mean = _segsum64(y) * (1.0 / HEAD_B)
    yc = y - mean
    var = _segsum64(yc * yc) * (1.0 / HEAD_B)
    yn = yc * lax.rsqrt(var + GN_EPS) * ln_g + ln_b
    o_b = (yn + _segsum64(r * k2 * r_k) * v) * gate
    return (_sigmoid(ga) * o_a + _sigmoid(gb) * o_b,)


def _loss(h3, target, gain, tb):
    rows, d = h3.shape

    def body(h_ref, t_ref, g_ref, dh_ref, dg_ref, l_ref):
        i = pl.program_id(0)
        live = (i * tb + lax.broadcasted_iota(jnp.int32, (tb, 1), 0)) >= CHUNK
        tgt = t_ref[...]

        def f(h, g):
            err = _rms(h, g) - tgt
            return 0.5 * jnp.sum(jnp.where(live, jnp.mean(err * err, axis=-1, keepdims=True), 0.0))

        val, vjp = jax.vjp(f, h_ref[...], g_ref[...])
        dh, dg = vjp(jnp.ones((), F32))
        dh_ref[...] = dh

        @pl.when(i == 0)
        def _():
            dg_ref[...] = jnp.zeros_like(dg_ref)
            l_ref[...] = jnp.zeros_like(l_ref)

        dg_ref[...] += dg
        l_ref[...] += jnp.full((1, 128), val, F32)

    blk = pl.BlockSpec((tb, d), lambda i: (i, 0))
    return pl.pallas_call(
        body, name="loss", grid=(rows // tb,),
        in_specs=[blk, blk, pl.BlockSpec((1, d), lambda i: (0, 0))],
        out_specs=[blk, pl.BlockSpec((1, d), lambda i: (0, 0)), pl.BlockSpec((1, 128), lambda i: (0, 0))],
        out_shape=[jax.ShapeDtypeStruct((rows, d), F32), jax.ShapeDtypeStruct((1, d), F32),
                   jax.ShapeDtypeStruct((1, 128), F32)],
        compiler_params=_cp("arbitrary"),
    )(h3, target, gain)


def _adamw_math(w, g, m, v):
    m2 = ADAM_B1 * m + (1.0 - ADAM_B1) * g
    v2 = ADAM_B2 * v + (1.0 - ADAM_B2) * (g * g)
    m_hat = m2 / (1.0 - ADAM_B1 ** ADAM_STEP)
    v_hat = v2 / (1.0 - ADAM_B2 ** ADAM_STEP)
    return -ADAM_LR * (m_hat / (jnp.sqrt(v_hat) + ADAM_EPS) + ADAM_WD * w), m2, v2


def _adamw(name, own, landed, w, m, v):
    rows, cols = w.shape
    if rows % 16 == 0:
        rb = _tb(rows, 128)
        grid, blk = (rows // rb,), pl.BlockSpec((rb, cols), lambda i: (i, 0))
        landed_blk = pl.BlockSpec((N_DEV - 1, rb, cols), lambda i: (0, i, 0))
    else:
        grid, blk = (cols // 128,), pl.BlockSpec((rows, 128), lambda i: (0, i))
        landed_blk = pl.BlockSpec((N_DEV - 1, rows, 128), lambda i: (0, 0, i))

    def body(o_ref, s_ref, w_ref, m_ref, v_ref, g_ref, d_ref, m2_ref, v2_ref):
        g = o_ref[...].astype(F32)
        for peer in range(N_DEV - 1):
            g = g + s_ref[peer].astype(F32)
        g_ref[...] = g
        d_ref[...], m2_ref[...], v2_ref[...] = _adamw_math(w_ref[...], g, m_ref[...], v_ref[...])

    return pl.pallas_call(
        body, name=name, grid=grid, in_specs=[blk, landed_blk, blk, blk, blk],
        out_specs=[blk] * 4, out_shape=[jax.ShapeDtypeStruct((rows, cols), F32)] * 4,
        compiler_params=_cp("parallel"),
    )(own, landed, w, m, v)


def _sum_slabs(name, slabs, rb):
    _, rows, cols = slabs.shape

    def body(s_ref, o_ref):
        g = s_ref[0]
        for dev in range(1, N_DEV):
            g = g + s_ref[dev]
        o_ref[...] = g

    return pl.pallas_call(
        body, name=name, grid=(rows // rb,),
        in_specs=[pl.BlockSpec((N_DEV, rb, cols), lambda i: (0, i, 0))],
        out_specs=pl.BlockSpec((rb, cols), lambda i: (i, 0)),
        out_shape=jax.ShapeDtypeStruct((rows, cols), F32), compiler_params=_cp("parallel"),
    )(slabs)


def _adamw_small(w, g, m, v):
    def body(w_ref, g_ref, m_ref, v_ref, d_ref, m2_ref, v2_ref):
        d_ref[...], m2_ref[...], v2_ref[...] = _adamw_math(w_ref[...], g_ref[...], m_ref[...], v_ref[...])

    return pl.pallas_call(body, name="adamw_small", out_shape=[jax.ShapeDtypeStruct(w.shape, F32)] * 3)(w, g, m, v)


def _place():
    return lax.axis_index("x"), lax.axis_index("y"), lax.axis_index("c")


def _index(p):
    return 4 * p[0] + 2 * p[1] + p[2]


def _all_gather(name, xs):
    n = len(xs)

    def body(*refs):
        x_refs, o_refs = refs[:n], refs[n:2 * n]
        send_sems, recv_sems, local_sems = refs[2 * n:]
        x, y, c = _place()
        me, sibling = (x, y, c), (x, y, 1 - c)
        chips = [(1 - x, y), (x, 1 - y), (1 - x, 1 - y)]

        def copy(i, k, block, to, src=None):
            dst = o_refs[i].at[_index(block)]
            return pltpu.make_async_remote_copy(src_ref=dst if src is None else src, dst_ref=dst,
                                                send_sem=send_sems.at[i, k], recv_sem=recv_sems.at[i, k],
                                                device_id=to, device_id_type=MESH_ID)

        mine = [pltpu.make_async_copy(x_refs[i], o_refs[i].at[_index(me)], local_sems.at[i]) for i in range(n)]
        for cp in mine:
            cp.start()
        first = []
        for i in range(n):
            first.append(copy(i, 0, me, sibling, src=x_refs[i]))
            first += [copy(i, 1 + j, me, (*chip, c), src=x_refs[i]) for j, chip in enumerate(chips)]
        for cp in first:
            cp.start()
        passed = []
        for j, chip in enumerate(chips):
            for i in range(n):
                copy(i, 1 + j, (*chip, c), me).wait_recv()
                cp = copy(i, 4 + j, (*chip, c), sibling)
                cp.start()
                passed.append(cp)
        for i in range(n):
            copy(i, 0, sibling, me).wait_recv()
            for j, chip in enumerate(chips):
                copy(i, 4 + j, (*chip, 1 - c), me).wait_recv()
        for cp in first + passed:
            cp.wait_send()
        for cp in mine:
            cp.wait()

    return pl.pallas_call(
        body, name=name, in_specs=[ANY] * n, out_specs=[ANY] * n,
        out_shape=[jax.ShapeDtypeStruct((N_DEV,) + x.shape, x.dtype) for x in xs],
        scratch_shapes=[pltpu.SemaphoreType.DMA((n, 7)), pltpu.SemaphoreType.DMA((n, 7)), pltpu.SemaphoreType.DMA((n,))],
    )(*xs)


def _exchange_start(name, xs, after=None, gather=False):
    n = len(xs)
    copies = n * (N_DEV - 1)
    extra = [] if after is None else [after]

    def body(*refs):
        x_refs, land_refs = refs[:n], refs[n:2 * n]
        sems = refs[2 * n + len(extra):2 * n + len(extra) + 2 * copies]
        token = refs[-1]
        for i, k, peer in _exchange_copies(n):
            _exchange_copy(x_refs, land_refs, sems, i, k, peer, gather).start()
        token[...] = jnp.zeros_like(token)

    lands = [lax.empty((N_DEV,) + x.shape if gather else (N_DEV - 1,) + x.shape[1:], x.dtype) for x in xs]
    out = pl.pallas_call(
        body, name=name,
        out_shape=(*[pltpu.SemaphoreType.DMA(())] * (2 * copies), *[pltpu.HBM(x.shape, x.dtype) for x in xs],
                   *[pltpu.HBM(l.shape, l.dtype) for l in lands], jax.ShapeDtypeStruct((8, 128), F32)),
        in_specs=[HBM_SPEC] * (2 * n) + [ANY] * len(extra),
        out_specs=(*[SEM_SPEC] * (2 * copies), *[HBM_SPEC] * (2 * n), pl.BlockSpec(memory_space=pltpu.VMEM)),
        input_output_aliases={i: 2 * copies + i for i in range(2 * n)},
        compiler_params=pltpu.CompilerParams(has_side_effects=pltpu.SideEffectType.DATAFLOW_SIDE_EFFECTING),
    )(*[pltpu.with_memory_space_constraint(a, pltpu.HBM) for a in list(xs) + lands], *extra)
    sems, rest = list(out[:2 * copies]), out[2 * copies:]
    return sems, list(rest[:n]), list(rest[n:2 * n]), rest[-1]


def _exchange_copies(n):
    x, y, c = _place()
    for k in range(1, N_DEV):
        peer = ((1 - x) if k & 4 else x, (1 - y) if k & 2 else y, (1 - c) if k & 1 else c)
        for i in range(n):
            yield i, k - 1, peer


def _exchange_copy(x_refs, land_refs, sems, i, k, peer, gather, arriving=False):
    copies = len(sems) // 2
    which = i * (N_DEV - 1) + k
    src = x_refs[i] if gather else x_refs[i].at[_index(peer)]
    dst = land_refs[i].at[_index(peer if arriving else _place())] if gather else land_refs[i].at[k]
    return pltpu.make_async_remote_copy(src_ref=src, dst_ref=dst, send_sem=sems[which], recv_sem=sems[copies + which],
                                        device_id=peer, device_id_type=MESH_ID)


def _exchange_wait(name, sems, xs, lands, after, gather=False):
    n = len(xs)

    def body(*refs):
        x_refs, land_refs = refs[:n], refs[n:2 * n]
        sem_refs = refs[2 * n:2 * n + len(sems)]
        for i, k, peer in _exchange_copies(n):
            _exchange_copy(x_refs, land_refs, sem_refs, i, k, peer, gather).wait_send()
            _exchange_copy(x_refs, land_refs, sem_refs, i, k, peer, gather, arriving=True).wait_recv()

    out = pl.pallas_call(
        body, name=name,
        out_shape=(*[pltpu.HBM(x.shape, x.dtype) for x in xs], *[pltpu.HBM(l.shape, l.dtype) for l in lands]),
        in_specs=[HBM_SPEC] * (2 * n) + [SEM_SPEC] * len(sems) + [ANY], out_specs=tuple([HBM_SPEC] * (2 * n)),
        input_output_aliases={i: i for i in range(2 * n)},
        compiler_params=pltpu.CompilerParams(has_side_effects=pltpu.SideEffectType.DATAFLOW_SIDE_EFFECTING),
    )(*xs, *lands, *sems, after)
    return list(out[:n]), list(out[n:])


def _pack(arrays):
    flat = jnp.concatenate([a.reshape(-1) for a in arrays])
    pad = (-flat.shape[0]) % 1024
    return jnp.pad(flat, (0, pad)).reshape(-1, 128)


def _unpack(packed, shapes):
    flat = packed.reshape(-1)
    out, pos = [], 0
    for s in shapes:
        size = 1
        for dim in s:
            size *= dim
        out.append(flat[pos:pos + size].reshape(s))
        pos += size
    return out


def _cols_from_slabs(stack):
    return jnp.transpose(stack, (1, 0, 2)).reshape(stack.shape[1], -1)


def kernel(x, meta_tokens, ffn1_norm, ffn1_w_gu, ffn1_w_down, mix_norm, w_in, a_conv_w, a_log_rate, a_dt_bias, a_out_norm, b_shift_mu, b_w0, b_w_up, b_a0, b_a_up, b_g_up, b_k_k, b_k_a, b_r_k, b_ln_gain, b_ln_bias, w_out, ffn2_norm, ffn2_w_gu, ffn2_w_down, final_norm, loss_target, m_meta_tokens, m_ffn1_norm, m_ffn1_w_gu, m_ffn1_w_down, m_mix_norm, m_w_in, m_a_conv_w, m_a_log_rate, m_a_dt_bias, m_a_out_norm, m_b_shift_mu, m_b_w0, m_b_w_up, m_b_a0, m_b_a_up, m_b_g_up, m_b_k_k, m_b_k_a, m_b_r_k, m_b_ln_gain, m_b_ln_bias, m_w_out, m_ffn2_norm, m_ffn2_w_gu, m_ffn2_w_down, m_final_norm, v_meta_tokens, v_ffn1_norm, v_ffn1_w_gu, v_ffn1_w_down, v_mix_norm, v_w_in, v_a_conv_w, v_a_log_rate, v_a_dt_bias, v_a_out_norm, v_b_shift_mu, v_b_w0, v_b_w_up, v_b_a0, v_b_a_up, v_b_g_up, v_b_k_k, v_b_k_a, v_b_r_k, v_b_ln_gain, v_b_ln_bias, v_w_out, v_ffn2_norm, v_ffn2_w_gu, v_ffn2_w_down, v_final_norm):
    names = ['meta_tokens', 'ffn1_norm', 'ffn1_w_gu', 'ffn1_w_down', 'mix_norm', 'w_in', 'a_conv_w', 'a_log_rate',
             'a_dt_bias', 'a_out_norm', 'b_shift_mu', 'b_w0', 'b_w_up', 'b_a0', 'b_a_up', 'b_g_up', 'b_k_k', 'b_k_a',
             'b_r_k', 'b_ln_gain', 'b_ln_bias', 'w_out', 'ffn2_norm', 'ffn2_w_gu', 'ffn2_w_down', 'final_norm']
    env = dict(locals())
    wts = {k: env[k] for k in names}
    mom_m = {k: env['m_' + k] for k in names}
    mom_v = {k: env['v_' + k] for k in names}
    big = ['ffn1_w_gu', 'ffn1_w_down', 'w_in', 'w_out', 'ffn2_w_gu', 'ffn2_w_down']
    col_sharded = {'ffn1_w_gu', 'w_in', 'ffn2_w_gu'}
    shard_of = lambda tree, k: tree[k][0].T if k in col_sharded else tree[k][0]
    small_sharded = ['meta_tokens', 'a_conv_w', 'b_w_up', 'b_a_up', 'b_g_up']
    replicated = [k for k in names if k not in big and k not in small_sharded]

    seq, d = x.shape[1], x.shape[2]
    rows = PAD + N_META + seq
    heads_a = d // HEAD_A
    tb_mm = _tb(rows, 416)
    tb_vjp = _tb(rows, 208)
    tb_dw = _tb(rows, 2080)
    tb_ffn = _tb(rows, 832)
    me = _index(_place())

    local_bf = {k: shard_of(wts, k).astype(BF16) for k in big}
    gu1, down1, meta = _all_gather("gather_ffn1", [local_bf['ffn1_w_gu'], local_bf['ffn1_w_down'], wts['meta_tokens']])
    small_rest = small_sharded[1:]
    late_keys = ['w_out', 'ffn2_w_gu', 'ffn2_w_down']
    gather_mid = _exchange_start("gather_start_mid", [local_bf['w_in'], _pack([wts[k][0] for k in small_rest])],
                                 after=gu1, gather=True)
    gather_late = _exchange_start("gather_start_late", [local_bf[k] for k in late_keys], after=gather_mid[-1],
                                  gather=True)

    def finish_gather(tag, started, after):
        sems, mine, lands, _ = started
        mine, lands = _exchange_wait("gather_wait_" + tag, sems, mine, lands, after, gather=True)
        return [lax.dynamic_update_index_in_dim(land, own[None], me, 0) for land, own in zip(lands, mine)]

    full = {'ffn1_w_gu': gu1, 'ffn1_w_down': down1.reshape(-1, d), 'meta_tokens': _cols_from_slabs(meta)}
    for k in replicated:
        full[k] = wts[k].reshape(1, -1)

    h0 = jnp.concatenate([jnp.zeros((PAD, d), F32) + gather_late[-1][:1, :1], full['meta_tokens'], x[0]], axis=0)
    h1 = _ffn_fwd("ffn1_fwd", h0, full['ffn1_norm'], full['ffn1_w_gu'], full['ffn1_w_down'], tb_ffn)
    (u,) = _tok_fwd("mix_norm_fwd", _norm_fn, [h1], [full['mix_norm']], [(d, BF16)], tb_mm)

    win_stack, small_stack = finish_gather("mid", gather_mid, u)
    full['w_in'] = win_stack.reshape(-1, d)
    small_flat, pos = small_stack.reshape(N_DEV, -1), 0
    for k in small_rest:
        shape = wts[k][0].shape
        full[k] = _cols_from_slabs(small_flat[:, pos:pos + shape[0] * shape[1]].reshape((N_DEV,) + shape))
        pos += shape[0] * shape[1]

    win = full['w_in']
    n_b = 3 * d + LORA_W + LORA_A + LORA_G
    off_beta, off_b = 4 * d, 4 * d + 2 * heads_a
    off_ga = off_b + n_b
    b_width = 3 * d + LORA_PAD
    zrows = lambda r: jnp.zeros((r, d), BF16)
    w_qkv = win[:3 * d]
    w_zg = jnp.concatenate([win[3 * d:4 * d], win[off_ga:off_ga + 2 * d]], axis=0)
    w_b = jnp.concatenate([win[off_b:off_b + n_b], zrows(b_width - n_b)], axis=0)
    w_bg = jnp.concatenate([win[off_beta:off_beta + 2 * heads_a], zrows(128 - 2 * heads_a)], axis=0)

    def lanes(vec, start, width):
        return jnp.pad(vec.reshape(1, -1), ((0, 0), (start, width - start - vec.size)))

    log_rate = lanes(wts['a_log_rate'], heads_a, 128)
    dt_bias = lanes(wts['a_dt_bias'], heads_a, 128)
    mu = lanes(wts['b_shift_mu'], 0, b_width)
    w_up = jnp.pad(full['b_w_up'], ((0, 128 - LORA_W), (0, 0)))
    a_up = jnp.pad(full['b_a_up'], ((LORA_W, 0), (0, 0)))
    g_up = jnp.pad(full['b_g_up'], ((0, 256 - LORA_G), (0, 0)))
    b_pars = [full['b_w0'], w_up, full['b_a0'], a_up, g_up, full['b_k_k'], full['b_k_a']]
    post_pars = [full['a_out_norm'], full['b_r_k'], full['b_ln_gain'], full['b_ln_bias']]
    bg_fn = _make_bg_fn(heads_a)

    z_qkv = _mm("in_qkv", u, w_qkv, trans_b=True, tb=tb_mm, tn=_col_tile(3 * d, 1536))
    z_zg = _mm("in_zg", u, w_zg, trans_b=True, tb=tb_mm, tn=_col_tile(3 * d, 1536))
    z_b = _mm("in_b", u, w_b, trans_b=True, tb=tb_mm, tn=_col_tile(b_width, 1536))
    z_bg = _mm("in_bg", u, w_bg, trans_b=True, tb=tb_mm, tn=128)
    qkv = _a_pre_fwd(z_qkv, full['a_conv_w'])
    (bg,) = _tok_fwd("bg_fwd", bg_fn, [z_bg], [log_rate, dt_bias], [(128, F32)], tb_mm)
    o_dn, dn_hist, dn_tinv = _dn_fwd(qkv, bg)
    zf = _shift_fwd(z_b, mu)
    rr, ww, kk2, vv, av, bv, gate = _tok_fwd("b_pre_fwd", _b_pre_fn, [zf], b_pars, [(d, F32)] * 7, tb_vjp)
    per_head = lambda t: t.reshape(rows, d // HEAD_B, HEAD_B)
    y_heads, b_hist, b_last = _rwkv_fwd(rr, ww, kk2, per_head(vv), av, bv)
    y_b = y_heads.reshape(rows, d)
    w_out_stack, full['ffn2_w_gu'], down2 = finish_gather("late", gather_late, y_heads)
    full['w_out'], full['ffn2_w_down'] = w_out_stack.reshape(-1, d), down2.reshape(-1, d)
    post_toks = [o_dn, z_zg, y_b, rr, kk2, vv, gate]
    (merged,) = _tok_fwd("post_fwd", _post_fn, post_toks, post_pars, [(d, BF16)], tb_vjp)
    h2 = _mm("out_proj", merged, full['w_out'], add=h1, tb=tb_mm, tn=d)
    h3 = _ffn_fwd("ffn2_fwd", h2, full['ffn2_norm'], full['ffn2_w_gu'], full['ffn2_w_down'], tb_ffn)

    target = jnp.pad(loss_target[0], ((CHUNK, 0), (0, 0)))
    dh3, g_final, loss_part = _loss(h3, target, full['final_norm'].reshape(1, d), tb_vjp)

    def ffn_backward(tag, h, dout, key_norm, key_gu, key_down):
        dh, dh_bf, dgain, xn, act, dgate, dup, dhalf = _ffn_bwd(tag + "_bwd", h, full[key_norm], dout, full[key_gu],
                                                                full[key_down], tb_mm)
        fc = dgate.shape[2]
        d_gu = jnp.concatenate([_mm_tn_from_slabs(tag + "_dw_gate", dgate, xn, tk=tb_dw).reshape(-1, fc, d),
                                _mm_tn_from_slabs(tag + "_dw_up", dup, xn, tk=tb_dw).reshape(-1, fc, d)], axis=0)
        d_down = _mm_tn_from_slabs(tag + "_dw_down", act, dhalf, tk=tb_dw).reshape(N_DEV, -1, d)
        return dh, dh_bf, dgain, d_gu, d_down

    dh2, dh2_bf, g_ffn2_norm, g_ffn2_gu, g_ffn2_down = ffn_backward("ffn2", h2, dh3, 'ffn2_norm', 'ffn2_w_gu',
                                                                    'ffn2_w_down')
    g_w_out = _mm_tn("dw_out", merged, dh2_bf, tm=d, tn=d, tk=tb_dw).reshape(N_DEV, -1, d)

    def start_exchange(tag, keys, slabs, after=None):
        sems, kept, lands, token = _exchange_start("exchange_start_" + tag, slabs, after)
        return (tag, keys, sems, kept, lands), token

    ex_ffn2, token_ffn2 = start_exchange("ffn2", ['ffn2_w_gu', 'ffn2_w_down', 'w_out'], [g_ffn2_gu, g_ffn2_down, g_w_out])
    dmerged = _mm("d_merged", dh2_bf, full['w_out'], trans_b=True, after=token_ffn2, tb=tb_mm, tn=d)
    post_grads = _tok_bwd("post_bwd", _post_fn, post_toks, post_pars, [[dmerged]], list(range(7)), tb_vjp,
                          [F32, BF16] + [F32] * 5)
    do_dn, dz_zg, dy_b, dr1, dk1, dv1, dgate = post_grads[:7]
    g_out_norm, g_r_k, g_ln_g, g_ln_b = post_grads[7:]
    dr2, dw2, dk2, dv_heads, da2, db2 = _rwkv_bwd(rr, ww, kk2, per_head(vv), av, bv, b_hist, b_last, per_head(dy_b))
    dv2 = dv_heads.reshape(rows, d)
    b_grads = _tok_bwd("b_pre_bwd", _b_pre_fn, [zf], b_pars,
                       [[dr1, dr2], [dw2], [dk1, dk2], [dv1, dv2], [da2], [db2], [dgate]], [0], tb_vjp)
    dzf = b_grads[0]
    g_w0, g_w_up, g_a0, g_a_up, g_g_up, g_k_k, g_k_a = b_grads[1:]
    dz_b, g_mu = _shift_bwd(z_b, mu, dzf)
    dqkv, dbg = _dn_bwd(qkv, bg, dn_hist, dn_tinv, do_dn)
    dz_qkv, g_conv = _a_pre_bwd(z_qkv, full['a_conv_w'], dqkv)
    dz_bg, g_log_rate, g_dt_bias = _tok_bwd("bg_bwd", bg_fn, [z_bg], [log_rate, dt_bias], [[dbg]], [0], tb_mm, [BF16])

    du = None
    g_w_in_parts = []
    for tag, dz, wpiece in (("qkv", dz_qkv, w_qkv), ("zg", dz_zg, w_zg), ("b", dz_b, w_b), ("bg", dz_bg, w_bg)):
        du = _mm("du_" + tag, dz, wpiece, add=du, tb=tb_mm, tn=d)
        g_w_in_parts.append(_mm_tn("dw_in_" + tag, dz, u, tm=_col_tile(dz.shape[1], 1536), tn=d, tk=tb_dw))
    gp_qkv, gp_zg, gp_b, gp_bg = g_w_in_parts
    g_w_in = jnp.concatenate([gp_qkv, gp_zg[:d], gp_bg[:2 * heads_a], gp_b[:n_b], gp_zg[d:]],
                             axis=0).reshape(N_DEV, -1, d)
    ex_w_in, token_w_in = start_exchange("w_in", ['w_in'], [g_w_in])
    dh1, g_mix_norm = _tok_bwd("mix_norm_bwd", _norm_res_fn, [h1], [full['mix_norm']], [[du], [dh2]], [0], tb_vjp,
                               after=token_w_in)
    dh0, _, g_ffn1_norm, g_ffn1_gu, g_ffn1_down = ffn_backward("ffn1", h0, dh1, 'ffn1_norm', 'ffn1_w_gu', 'ffn1_w_down')

    small_full = {
        'meta_tokens': dh0[PAD:CHUNK], 'ffn1_norm': g_ffn1_norm, 'mix_norm': g_mix_norm, 'a_conv_w': g_conv,
        'a_log_rate': g_log_rate[:, heads_a:2 * heads_a], 'a_dt_bias': g_dt_bias[:, heads_a:2 * heads_a],
        'a_out_norm': g_out_norm, 'b_shift_mu': g_mu[:, :n_b], 'b_w0': g_w0, 'b_w_up': g_w_up[:LORA_W],
        'b_a0': g_a0, 'b_a_up': g_a_up[LORA_W:], 'b_g_up': g_g_up[:LORA_G], 'b_k_k': g_k_k, 'b_k_a': g_k_a,
        'b_r_k': g_r_k, 'b_ln_gain': g_ln_g, 'b_ln_bias': g_ln_b, 'ffn2_norm': g_ffn2_norm, 'final_norm': g_final,
    }
    small_names = list(small_full)
    packed = _pack([small_full[k] for k in small_names] + [loss_part[:, :1]])
    (all_parts,) = _all_gather("gather_small_grads", [packed])
    ex_ffn1, _ = start_exchange("ffn1", ['ffn1_w_gu', 'ffn1_w_down'], [g_ffn1_gu, g_ffn1_down], after=all_parts)
    summed = _sum_slabs("sum_small_grads", all_parts, packed.shape[0])
    pieces = _unpack(summed, [small_full[k].shape for k in small_names] + [(1, 1)])
    small_grad = dict(zip(small_names, pieces[:-1]))
    loss = pieces[-1].reshape(())

    grads, deltas, new_m, new_v = {}, {}, {}, {}
    local_small = {}
    for k in small_names:
        g = small_grad[k]
        if k in small_sharded:
            width = wts[k].shape[-1]
            g = lax.dynamic_slice_in_dim(g, me * width, width, axis=1)
        local_small[k] = g.reshape(wts[k].shape)
    pk = lambda tree: _pack([tree[k] for k in small_names])
    dl_s, m_s, v_s = _adamw_small(pk(wts), pk(local_small), pk(mom_m), pk(mom_v))
    shapes = [wts[k].shape for k in small_names]
    for k, dl, m2, v2 in zip(small_names, _unpack(dl_s, shapes), _unpack(m_s, shapes), _unpack(v_s, shapes)):
        grads[k], deltas[k], new_m[k], new_v[k] = local_small[k], dl, m2, v2

    done = dl_s
    for tag, keys, sems, kept, lands in (ex_ffn2, ex_w_in, ex_ffn1):
        kept, lands = _exchange_wait("exchange_wait_" + tag, sems, kept, lands, done)
        for k, slabs, landed in zip(keys, kept, lands):
            own = lax.dynamic_index_in_dim(slabs, me, axis=0, keepdims=False)
            res = _adamw("adamw_" + k, own, landed, shard_of(wts, k), shard_of(mom_m, k), shard_of(mom_v, k))
            done = res[1]
            res = [(t.T if k in col_sharded else t)[None] for t in res]
            grads[k], deltas[k], new_m[k], new_v[k] = res

    grad_x = dh0[CHUNK:][None]
    return (loss, grad_x, *[grads[k] for k in names], *[deltas[k] for k in names],
            *[new_m[k] for k in names], *[new_v[k] for k in names])
```

```python
import functools

import jax
import jax.numpy as jnp
from jax import lax
from jax.experimental import pallas as pl
from jax.experimental.pallas import tpu as pltpu

F32 = jnp.float32
BF16 = jnp.bfloat16
N_DEV = 8
N_META = 16
CHUNK = 64
PAD = CHUNK - N_META
HEAD_A = 128
HEAD_B = 64
LORA_W, LORA_A, LORA_G = 64, 64, 160
LORA_PAD = 384
EPS = 1e-6
GN_EPS = HEAD_B * 1e-5
ADAM_LR, ADAM_B1, ADAM_B2, ADAM_EPS, ADAM_WD, ADAM_STEP = 0.001, 0.9, 0.999, 1e-08, 0.01, 10
SCAN_STEPS = 16
MXU_WIDTH = 256
VMEM_LIMIT = 56 * 1024 * 1024
DN_PRECISION = lax.Precision.HIGH
MESH_ID = pl.DeviceIdType.MESH
ANY = pl.BlockSpec(memory_space=pl.ANY)
HBM_SPEC = pl.BlockSpec(memory_space=pltpu.HBM)
SEM_SPEC = pl.BlockSpec(memory_space=pltpu.SEMAPHORE)


def _cp(*sem):
    return pltpu.CompilerParams(dimension_semantics=sem, vmem_limit_bytes=VMEM_LIMIT)


def _tb(t, target):
    best = 16
    for d in range(16, target + 1, 16):
        if t % d == 0:
            best = d
    return best


def _sigmoid(x):
    return 1.0 / (1.0 + jnp.exp(-x))


def _silu(x):
    return x * _sigmoid(x)


def _softplus(x):
    return jnp.maximum(x, 0.0) + jnp.log(1.0 + jnp.exp(-jnp.abs(x)))


def _dot_nt(a, b, precision=None):
    return lax.dot_general(a, b, (((1,), (1,)), ((), ())), preferred_element_type=F32, precision=precision)


def _dot_tn(a, b, precision=None):
    return lax.dot_general(a, b, (((0,), (0,)), ((), ())), preferred_element_type=F32, precision=precision)


def _dot(a, b, precision=None):
    return jnp.dot(a, b, preferred_element_type=F32, precision=precision)


def _block_diag_ones():
    i = lax.broadcasted_iota(jnp.int32, (MXU_WIDTH, MXU_WIDTH), 0) // HEAD_B
    j = lax.broadcasted_iota(jnp.int32, (MXU_WIDTH, MXU_WIDTH), 1) // HEAD_B
    return (i == j).astype(BF16)


def _hi_lo(x):
    hi = x.astype(BF16)
    return hi, (x - hi.astype(F32)).astype(BF16)


def _segsum_many(xs, bd):
    groups = [x if isinstance(x, tuple) else _hi_lo(x) for x in xs]
    rows = groups[0][0].shape[0]
    stacked = jnp.concatenate([p for grp in groups for p in grp], axis=0)
    out = jnp.concatenate([_dot(stacked[:, s:s + MXU_WIDTH], bd) for s in range(0, stacked.shape[1], MXU_WIDTH)], axis=1)
    res, pos = [], 0
    for grp in groups:
        acc = out[pos:pos + rows]
        for j in range(1, len(grp)):
            acc = acc + out[pos + j * rows:pos + (j + 1) * rows]
        res.append(acc)
        pos += len(grp) * rows
    return res


def _segsum_impl(x):
    return _segsum_many([x], _block_diag_ones())[0]


@jax.custom_vjp
def _segsum64(x):
    return _segsum_impl(x)


_segsum64.defvjp(lambda x: (_segsum_impl(x), None), lambda _, ct: (_segsum_impl(ct),))


def _tok(t):
    return t if isinstance(t, tuple) else (t, t.shape[1], 0)


def _tok_spec(tb, width, colblk):
    return pl.BlockSpec((tb, width), lambda i: (i, colblk))


def _par_spec(p):
    return pl.BlockSpec(p.shape, lambda i: (0, 0))


def _tok_fwd(name, fn, toks, pars, outs, tb):
    toks = [_tok(t) for t in toks]
    rows = toks[0][0].shape[0]
    n_in = len(toks) + len(pars)

    def body(*refs):
        row0 = pl.program_id(0) * tb
        res = fn(row0, *[r[...] for r in refs[:n_in]])
        for r, o in zip(refs[n_in:], res):
            r[...] = o.astype(r.dtype)

    return pl.pallas_call(
        body, name=name, grid=(rows // tb,),
        in_specs=[_tok_spec(tb, w, c) for _, w, c in toks] + [_par_spec(p) for p in pars],
        out_specs=[_tok_spec(tb, w, 0) for w, _ in outs],
        out_shape=[jax.ShapeDtypeStruct((rows, w), dt) for w, dt in outs],
        compiler_params=_cp("parallel"),
    )(*[a for a, _, _ in toks], *pars)


def _tok_bwd(name, fn, toks, pars, cts, want, tb, want_dtypes=None, after=None):
    toks = [_tok(t) for t in toks]
    want_dtypes = want_dtypes or [F32] * len(want)
    cts = [[_tok(c) for c in group] for group in cts]
    flat_cts = [c for group in cts for c in group]
    rows = toks[0][0].shape[0]
    n_tok, n_par, n_ct = len(toks), len(pars), len(flat_cts)
    extra = [] if after is None else [after]

    def body(*refs):
        i = pl.program_id(0)
        row0 = i * tb
        prim = [r[...].astype(F32) for r in refs[:n_tok + n_par]]
        ct_refs = list(refs[n_tok + n_par:n_tok + n_par + n_ct])
        out_refs = refs[n_tok + n_par + n_ct + len(extra):]
        res, vjp = jax.vjp(lambda *a: fn(row0, *a), *prim)
        ct = []
        for group, o in zip(cts, res):
            acc = None
            for _ in group:
                v = ct_refs.pop(0)[...].astype(F32)
                acc = v if acc is None else acc + v
            ct.append(acc.astype(o.dtype))
        grads = vjp(tuple(ct))
        for r, k in zip(out_refs[:len(want)], want):
            r[...] = grads[k].astype(r.dtype)

        @pl.when(i == 0)
        def _():
            for r in out_refs[len(want):]:
                r[...] = jnp.zeros_like(r)

        for r, g in zip(out_refs[len(want):], grads[n_tok:]):
            r[...] += g

    return pl.pallas_call(
        body, name=name, grid=(rows // tb,),
        in_specs=[_tok_spec(tb, w, c) for _, w, c in toks] + [_par_spec(p) for p in pars]
        + [_tok_spec(tb, w, c) for _, w, c in flat_cts] + [ANY] * len(extra),
        out_specs=[_tok_spec(tb, toks[k][1], 0) for k in want] + [_par_spec(p) for p in pars],
        out_shape=[jax.ShapeDtypeStruct((rows, toks[k][1]), dt) for k, dt in zip(want, want_dtypes)]
        + [jax.ShapeDtypeStruct(p.shape, F32) for p in pars],
        compiler_params=_cp("arbitrary"),
    )(*[a for a, _, _ in toks], *pars, *[a for a, _, _ in flat_cts], *extra)


def _mm(name, a, b, *, trans_b=False, add=None, after=None, tb, tn):
    rows, k = a.shape
    n = b.shape[0] if trans_b else b.shape[1]

    def body(*refs):
        a_ref, b_ref = refs[:2]
        o_ref = refs[-1]
        acc = _dot_nt(a_ref[...], b_ref[...]) if trans_b else _dot(a_ref[...], b_ref[...])
        if add is not None:
            acc = acc + refs[2][...]
        o_ref[...] = acc

    in_specs = [pl.BlockSpec((tb, k), lambda i, j: (i, 0)),
                pl.BlockSpec((tn, k), lambda i, j: (j, 0)) if trans_b else pl.BlockSpec((k, tn), lambda i, j: (0, j))]
    args = [a, b]
    if add is not None:
        in_specs.append(pl.BlockSpec((tb, tn), lambda i, j: (i, j)))
        args.append(add)
    if after is not None:
        in_specs.append(ANY)
        args.append(after)
    return pl.pallas_call(
        body, name=name, grid=(rows // tb, n // tn), in_specs=in_specs,
        out_specs=pl.BlockSpec((tb, tn), lambda i, j: (i, j)),
        out_shape=jax.ShapeDtypeStruct((rows, n), F32),
        compiler_params=_cp("parallel", "parallel"),
    )(*args)


def _mm_tn_call(name, grid, a, b, a_spec, b_spec, o_spec, acc_shape, out_shape):
    last = len(grid) - 1

    def body(a_ref, b_ref, o_ref, acc_ref):
        k = pl.program_id(last)

        @pl.when(k == 0)
        def _():
            acc_ref[...] = jnp.zeros_like(acc_ref)

        a_blk = a_ref[0] if len(a_ref.shape) == 3 else a_ref[...]
        b_blk = b_ref[0] if len(b_ref.shape) == 3 else b_ref[...]
        acc_ref[...] += _dot_tn(a_blk, b_blk)

        @pl.when(k == grid[last] - 1)
        def _():
            if len(o_ref.shape) == 3:
                o_ref[0] = acc_ref[...].astype(o_ref.dtype)
            else:
                o_ref[...] = acc_ref[...].astype(o_ref.dtype)

    return pl.pallas_call(
        body, name=name, grid=grid, in_specs=[a_spec, b_spec], out_specs=o_spec,
        out_shape=jax.ShapeDtypeStruct(out_shape, BF16), scratch_shapes=[pltpu.VMEM(acc_shape, F32)],
        compiler_params=_cp(*(["parallel"] * last + ["arbitrary"])),
    )(a, b)


def _mm_tn(name, a, b, *, tm, tn, tk):
    rows, m = a.shape
    n = b.shape[1]
    return _mm_tn_call(name, (m // tm, n // tn, rows // tk), a, b,
                       pl.BlockSpec((tk, tm), lambda i, j, k: (k, i)), pl.BlockSpec((tk, tn), lambda i, j, k: (k, j)),
                       pl.BlockSpec((tm, tn), lambda i, j, k: (i, j)), (tm, tn), (m, n))


def _mm_tn_from_slabs(name, a3, b, *, tk):
    s, rows, c = a3.shape
    n = b.shape[1]
    return _mm_tn_call(name, (s, rows // tk), a3, b,
                       pl.BlockSpec((1, tk, c), lambda i, k: (i, k, 0)), pl.BlockSpec((tk, n), lambda i, k: (k, 0)),
                       pl.BlockSpec((c, n), lambda i, k: (i, 0)), (c, n), (s * c, n))


def _col_tile(n, target):
    if n <= target:
        return n
    best = 128
    for d in range(128, target + 1, 128):
        if n % d == 0:
            best = d
    return best


def _rms(x, gain):
    return x * lax.rsqrt(jnp.mean(x * x, axis=-1, keepdims=True) + EPS) * gain


def _ffn_specs(d, fc, nj):
    return [pl.BlockSpec((1, fc, d), lambda i, j: (j, 0, 0)), pl.BlockSpec((1, fc, d), lambda i, j: (nj + j, 0, 0)),
            pl.BlockSpec((fc, d), lambda i, j: (j, 0))]


def _ffn_fwd(name, h, gain, wgu, wd, tb):
    rows, d = h.shape
    nj = wgu.shape[0] // 2
    fc = wgu.shape[1]

    def body(h_ref, g_ref, wg_ref, wu_ref, wd_ref, o_ref, xn_s, acc_s):
        j = pl.program_id(1)

        @pl.when(j == 0)
        def _():
            xn_s[...] = _rms(h_ref[...], g_ref[...]).astype(BF16)
            acc_s[...] = jnp.zeros_like(acc_s)

        wg, wu, wdn = wg_ref[0], wu_ref[0], wd_ref[...]
        for half in range(2):
            rs = pl.ds(half * (tb // 2), tb // 2)
            xn = xn_s[rs, :]
            gate = _dot_nt(xn, wg)
            up = _dot_nt(xn, wu)
            acc_s[rs, :] += _dot((_silu(gate) * up).astype(BF16), wdn)

        @pl.when(j == nj - 1)
        def _():
            o_ref[...] = h_ref[...] + 0.5 * acc_s[...]

    return pl.pallas_call(
        body, name=name, grid=(rows // tb, nj),
        in_specs=[pl.BlockSpec((tb, d), lambda i, j: (i, 0)), pl.BlockSpec((1, d), lambda i, j: (0, 0))]
        + _ffn_specs(d, fc, nj),
        out_specs=pl.BlockSpec((tb, d), lambda i, j: (i, 0)),
        out_shape=jax.ShapeDtypeStruct((rows, d), F32),
        scratch_shapes=[pltpu.VMEM((tb, d), BF16), pltpu.VMEM((tb, d), F32)],
        compiler_params=_cp("parallel", "arbitrary"),
    )(h, gain, wgu, wgu, wd)


def _ffn_bwd(name, h, gain, dout, wgu, wd, tb):
    rows, d = h.shape
    nj = wgu.shape[0] // 2
    fc = wgu.shape[1]

    def body(h_ref, g_ref, do_ref, wg_ref, wu_ref, wd_ref,
             dh_ref, dhb_ref, dg_ref, xn_ref, act_ref, dgate_ref, dup_ref, dhalf_ref, dxn_s):
        i, j = pl.program_id(0), pl.program_id(1)

        @pl.when(j == 0)
        def _():
            xn_ref[...] = _rms(h_ref[...], g_ref[...]).astype(BF16)
            dhalf_ref[...] = (0.5 * do_ref[...]).astype(BF16)
            dxn_s[...] = jnp.zeros_like(dxn_s)

        wg, wu, wdn = wg_ref[0], wu_ref[0], wd_ref[...]
        for half in range(2):
            rs = pl.ds(half * (tb // 2), tb // 2)
            xn = xn_ref[rs, :]
            gate = _dot_nt(xn, wg)
            up = _dot_nt(xn, wu)
            sg = _sigmoid(gate)
            dact = _dot_nt(dhalf_ref[rs, :], wdn)
            act_ref[0, rs, :] = (gate * sg * up).astype(BF16)
            dgate = (dact * up * (sg * (1.0 + gate * (1.0 - sg)))).astype(BF16)
            dup = (dact * gate * sg).astype(BF16)
            dgate_ref[0, rs, :] = dgate
            dup_ref[0, rs, :] = dup
            dxn_s[rs, :] += _dot(dgate, wg) + _dot(dup, wu)

        @pl.when((i == 0) & (j == 0))
        def _():
            dg_ref[...] = jnp.zeros_like(dg_ref)

        @pl.when(j == nj - 1)
        def _():
            x = h_ref[...]
            r = lax.rsqrt(jnp.mean(x * x, axis=-1, keepdims=True) + EPS)
            dxn = dxn_s[...]
            dyg = dxn * g_ref[...]
            dh = do_ref[...] + r * dyg - x * (r * r * r) * jnp.mean(dyg * x, axis=-1, keepdims=True)
            dh_ref[...] = dh
            dhb_ref[...] = dh.astype(BF16)
            dg_ref[...] += jnp.sum(dxn * x * r, axis=0, keepdims=True)

    row_d = pl.BlockSpec((tb, d), lambda i, j: (i, 0))
    slab = pl.BlockSpec((1, tb, fc), lambda i, j: (j, i, 0))
    hidden = jax.ShapeDtypeStruct((nj, rows, fc), BF16)
    return pl.pallas_call(
        body, name=name, grid=(rows // tb, nj),
        in_specs=[row_d, pl.BlockSpec((1, d), lambda i, j: (0, 0)), row_d] + _ffn_specs(d, fc, nj),
        out_specs=[row_d, row_d, pl.BlockSpec((1, d), lambda i, j: (0, 0)), row_d, slab, slab, slab, row_d],
        out_shape=[jax.ShapeDtypeStruct((rows, d), F32), jax.ShapeDtypeStruct((rows, d), BF16),
                   jax.ShapeDtypeStruct((1, d), F32), jax.ShapeDtypeStruct((rows, d), BF16),
                   hidden, hidden, hidden, jax.ShapeDtypeStruct((rows, d), BF16)],
        scratch_shapes=[pltpu.VMEM((tb, d), F32)],
        compiler_params=_cp("arbitrary", "arbitrary"),
    )(h, gain, dout, wgu, wgu, wd)


def _shift_rows(x, s):
    return pltpu.roll(x, s % x.shape[0], 0)


def _a_post(c, which):
    s = _silu(c)
    n = s * lax.rsqrt(jnp.sum(s * s, axis=-1, keepdims=True) + 1e-6)
    scale = jnp.where(which == 0, HEAD_A ** -0.5, 1.0)
    return jnp.where(which == 2, s, n * scale)


def _conv(x, w):
    return x * w[3:4] + _shift_rows(x, 1) * w[2:3] + _shift_rows(x, 2) * w[1:2] + _shift_rows(x, 3) * w[0:1]


def _a_pre_fwd(zqkv, conv_w):
    rows, width = zqkv.shape
    heads = width // (3 * HEAD_A)

    def body(x_ref, w_ref, o_ref):
        which = pl.program_id(0) // heads
        live = lax.broadcasted_iota(jnp.int32, (rows, HEAD_A), 0) >= PAD
        o_ref[...] = jnp.where(live, _a_post(_conv(x_ref[...], w_ref[...]), which), 0.0)

    return pl.pallas_call(
        body, name="a_pre_fwd", grid=(width // HEAD_A,),
        in_specs=[pl.BlockSpec((rows, HEAD_A), lambda c: (0, c)), pl.BlockSpec((4, HEAD_A), lambda c: (0, c))],
        out_specs=pl.BlockSpec((rows, HEAD_A), lambda c: (0, c)),
        out_shape=jax.ShapeDtypeStruct((rows, width), F32),
        compiler_params=_cp("parallel"),
    )(zqkv, conv_w)


def _a_pre_bwd(zqkv, conv_w, dqkv):
    rows, width = zqkv.shape
    heads = width // (3 * HEAD_A)

    def body(x_ref, w_ref, ct_ref, dx_ref, dw_ref):
        which = pl.program_id(0) // heads
        live = lax.broadcasted_iota(jnp.int32, (rows, HEAD_A), 0) >= PAD
        x, w = x_ref[...], w_ref[...]
        _, vjp = jax.vjp(lambda c: _a_post(c, which), _conv(x, w))
        (dc,) = vjp(jnp.where(live, ct_ref[...], 0.0))
        dc = jnp.where(live, dc, 0.0)
        dx_ref[...] = (dc * w[3:4] + _shift_rows(dc, -1) * w[2:3] + _shift_rows(dc, -2) * w[1:2]
                       + _shift_rows(dc, -3) * w[0:1]).astype(BF16)
        dw_ref[...] = jnp.concatenate(
            [jnp.sum(dc * (_shift_rows(x, 3 - j) if j < 3 else x), axis=0, keepdims=True) for j in range(4)], axis=0)

    col = pl.BlockSpec((rows, HEAD_A), lambda c: (0, c))
    wsp = pl.BlockSpec((4, HEAD_A), lambda c: (0, c))
    return pl.pallas_call(
        body, name="a_pre_bwd", grid=(width // HEAD_A,),
        in_specs=[col, wsp, col], out_specs=[col, wsp],
        out_shape=[jax.ShapeDtypeStruct((rows, width), BF16), jax.ShapeDtypeStruct((4, width), F32)],
        compiler_params=_cp("parallel"),
    )(zqkv, conv_w, dqkv)


SHIFT_TILE = 384


def _shift_fwd(zb, mu):
    rows, width = zb.shape

    def body(x_ref, mu_ref, o_ref):
        x = x_ref[...]
        first = lax.broadcasted_iota(jnp.int32, x.shape, 0) == 0
        prev = jnp.where(first, 0.0, _shift_rows(x, 1))
        o_ref[...] = x + (prev - x) * mu_ref[...]

    col = pl.BlockSpec((rows, SHIFT_TILE), lambda c: (0, c))
    return pl.pallas_call(
        body, name="shift_fwd", grid=(width // SHIFT_TILE,),
        in_specs=[col, pl.BlockSpec((1, SHIFT_TILE), lambda c: (0, c))], out_specs=col,
        out_shape=jax.ShapeDtypeStruct((rows, width), F32), compiler_params=_cp("parallel"),
    )(zb, mu)


def _shift_bwd(zb, mu, dzf):
    rows, width = zb.shape

    def body(x_ref, mu_ref, ct_ref, dx_ref, dmu_ref):
        x, ct, mu_v = x_ref[...], ct_ref[...], mu_ref[...]
        row = lax.broadcasted_iota(jnp.int32, x.shape, 0)
        prev = jnp.where(row == 0, 0.0, _shift_rows(x, 1))
        nxt = jnp.where(row == rows - 1, 0.0, _shift_rows(ct, -1))
        dx_ref[...] = (ct * (1.0 - mu_v) + nxt * mu_v).astype(BF16)
        dmu_ref[...] = jnp.sum(ct * (prev - x), axis=0, keepdims=True)

    col = pl.BlockSpec((rows, SHIFT_TILE), lambda c: (0, c))
    msp = pl.BlockSpec((1, SHIFT_TILE), lambda c: (0, c))
    return pl.pallas_call(
        body, name="shift_bwd", grid=(width // SHIFT_TILE,),
        in_specs=[col, msp, col], out_specs=[col, msp],
        out_shape=[jax.ShapeDtypeStruct((rows, width), BF16), jax.ShapeDtypeStruct((1, width), F32)],
        compiler_params=_cp("parallel"),
    )(zb, mu, dzf)


def _neumann_inverse(p):
    heads = range(len(p))
    eye = (lax.broadcasted_iota(jnp.int32, (CHUNK, CHUNK), 0)
           == lax.broadcasted_iota(jnp.int32, (CHUNK, CHUNK), 1)).astype(F32)
    tinv = [eye + p[h] for h in heads]
    for _ in range(5):
        p = [_dot(p[h], p[h], DN_PRECISION) for h in heads]
        tinv = [tinv[h] + _dot(tinv[h], p[h], DN_PRECISION) for h in heads]
    return tinv


@jax.custom_vjp
def _unit_lower_inverse(p):
    return _neumann_inverse(p)


def _unit_lower_inverse_fwd(p):
    tinv = _neumann_inverse(p)
    return tinv, tinv


def _unit_lower_inverse_bwd(tinv, ct):
    heads = range(len(tinv))
    left = [_dot_tn(tinv[h], ct[h], DN_PRECISION) for h in heads]
    return ([_dot_nt(left[h], tinv[h], DN_PRECISION) for h in heads],)


_unit_lower_inverse.defvjp(_unit_lower_inverse_fwd, _unit_lower_inverse_bwd)


@jax.custom_vjp
def _known_inverse(p, tinv):
    return tinv


_known_inverse.defvjp(lambda p, tinv: (tinv, tinv),
                      lambda tinv, ct: (_unit_lower_inverse_bwd(tinv, ct)[0], [jnp.zeros_like(t) for t in tinv]))


def _dn_chunk(q, k, v, beta, g, state, saved_tinv=None):
    heads = range(len(q))
    ri = lax.broadcasted_iota(jnp.int32, (CHUNK, CHUNK), 0)
    ci = lax.broadcasted_iota(jnp.int32, (CHUNK, CHUNK), 1)
    eye = (ri == ci).astype(F32)
    incl = ri >= ci
    last = lax.broadcasted_iota(jnp.int32, (CHUNK, 1), 0) == CHUNK - 1
    g_row = [jnp.sum(g[h] * eye, axis=0, keepdims=True) for h in heads]
    gc = [jnp.sum(jnp.where(incl, g_row[h], 0.0), axis=1, keepdims=True) for h in heads]
    gc_row = [jnp.sum(gc[h] * eye, axis=0, keepdims=True) for h in heads]
    decay = [jnp.where(incl, jnp.exp(jnp.where(incl, gc[h] - gc_row[h], 0.0)), 0.0) for h in heads]
    kb = [k[h] * beta[h] for h in heads]
    vb = [v[h] * beta[h] for h in heads]
    p = [-jnp.where(ri > ci, _dot_nt(kb[h], k[h]) * decay[h], 0.0) for h in heads]
    tinv = _unit_lower_inverse(p) if saved_tinv is None else _known_inverse(p, saved_tinv)
    eg = [jnp.exp(gc[h]) for h in heads]
    u = [_dot(tinv[h], vb[h]) for h in heads]
    wk = [_dot(tinv[h], kb[h] * eg[h]) for h in heads]
    attn = [_dot_nt(q[h], k[h]) * decay[h] for h in heads]
    g_last = [jnp.sum(jnp.where(last, gc[h], 0.0), axis=0, keepdims=True) for h in heads]
    k_tail = [k[h] * jnp.exp(g_last[h] - gc[h]) for h in heads]
    v_new = [u[h] - _dot(wk[h], state[h]) for h in heads]
    o = [_dot(q[h] * eg[h], state[h]) + _dot(attn[h], v_new[h]) for h in heads]
    new = [state[h] * jnp.exp(g_last[h]) + _dot_tn(k_tail[h], v_new[h]) for h in heads]
    return (o, new, tinv) if saved_tinv is None else (o, new)


def _bg_cols(bg, h, heads):
    lane = lax.broadcasted_iota(jnp.int32, bg.shape, 1)
    beta = jnp.sum(jnp.where(lane == h, bg, 0.0), axis=1, keepdims=True)
    g = jnp.sum(jnp.where(lane == heads + h, bg, 0.0), axis=1, keepdims=True)
    return beta, g


def _dn_fwd(qkv, bg):
    rows = qkv.shape[0]
    heads = qkv.shape[1] // (3 * HEAD_A)
    n = rows // CHUNK
    hp, groups = heads, 1

    def body(q_ref, k_ref, v_ref, bg_ref, o_ref, hist_ref, tinv_ref, s_ref):
        c, grp = pl.program_id(0), pl.program_id(1)

        @pl.when(c == 0)
        def _():
            for i in range(hp):
                s_ref[grp * hp + i] = jnp.zeros((HEAD_A, HEAD_A), F32)

        bg_v = bg_ref[...]
        cols = [slice(i * HEAD_A, (i + 1) * HEAD_A) for i in range(hp)]
        state = [s_ref[grp * hp + i] for i in range(hp)]
        beta_g = [_bg_cols(bg_v, grp * hp + i, heads) for i in range(hp)]
        o, new, tinv = _dn_chunk([q_ref[:, c_] for c_ in cols], [k_ref[:, c_] for c_ in cols],
                                 [v_ref[:, c_] for c_ in cols], [b for b, _ in beta_g], [g for _, g in beta_g], state)
        for i in range(hp):
            hist_ref[0, i] = state[i]
            tinv_ref[0, i] = tinv[i]
            o_ref[:, cols[i]] = o[i]
            s_ref[grp * hp + i] = new[i]

    def part(p):
        return pl.BlockSpec((CHUNK, hp * HEAD_A), lambda c, grp: (c, p * groups + grp))

    return pl.pallas_call(
        body, name="deltanet_fwd", grid=(n, groups),
        in_specs=[part(0), part(1), part(2), pl.BlockSpec((CHUNK, 128), lambda c, grp: (c, 0))],
        out_specs=[part(0), pl.BlockSpec((1, hp, HEAD_A, HEAD_A), lambda c, grp: (c, grp, 0, 0)),
                   pl.BlockSpec((1, hp, CHUNK, CHUNK), lambda c, grp: (c, grp, 0, 0))],
        out_shape=[jax.ShapeDtypeStruct((rows, heads * HEAD_A), F32),
                   jax.ShapeDtypeStruct((n, heads, HEAD_A, HEAD_A), F32),
                   jax.ShapeDtypeStruct((n, heads, CHUNK, CHUNK), F32)],
        scratch_shapes=[pltpu.VMEM((heads, HEAD_A, HEAD_A), F32)],
        compiler_params=_cp("arbitrary", "arbitrary"),
    )(qkv, qkv, qkv, bg)


def _dn_bwd(qkv, bg, hist, tinv_hist, do):
    rows = qkv.shape[0]
    heads = qkv.shape[1] // (3 * HEAD_A)
    n = rows // CHUNK
    hp, groups = heads, 1

    def body(q_ref, k_ref, v_ref, bg_ref, hist_ref, tinv_ref, do_ref, dqkv_ref, dbg_ref, ds_ref):
        c, grp = pl.program_id(0), pl.program_id(1)

        @pl.when(c == 0)
        def _():
            for i in range(hp):
                ds_ref[grp * hp + i] = jnp.zeros((HEAD_A, HEAD_A), F32)

        bg_v = bg_ref[...]
        lane = lax.broadcasted_iota(jnp.int32, (CHUNK, 128), 1)
        cols = [slice(i * HEAD_A, (i + 1) * HEAD_A) for i in range(hp)]
        beta_g = [_bg_cols(bg_v, grp * hp + i, heads) for i in range(hp)]
        _, vjp = jax.vjp(_dn_chunk, [q_ref[:, c_] for c_ in cols], [k_ref[:, c_] for c_ in cols],
                         [v_ref[:, c_] for c_ in cols], [b for b, _ in beta_g], [g for _, g in beta_g],
                         [hist_ref[0, i] for i in range(hp)], [tinv_ref[0, i] for i in range(hp)])
        dq, dk, dv, dbeta, dg, ds, _ = vjp(([do_ref[:, c_] for c_ in cols], [ds_ref[grp * hp + i] for i in range(hp)]))
        dbg = jnp.zeros((CHUNK, 128), F32)
        for i in range(hp):
            h = grp * hp + i
            for p, part_grad in enumerate((dq, dk, dv)):
                dqkv_ref[:, pl.ds((p * heads + i) * HEAD_A, HEAD_A)] = part_grad[i]
            ds_ref[h] = ds[i]
            dbg = dbg + jnp.where(lane == h, dbeta[i], 0.0) + jnp.where(lane == heads + h, dg[i], 0.0)

        @pl.when(grp == 0)
        def _():
            dbg_ref[...] = jnp.zeros_like(dbg_ref)

        dbg_ref[...] += dbg

    def part(p):
        return pl.BlockSpec((CHUNK, hp * HEAD_A), lambda c, grp: (n - 1 - c, p * groups + grp))

    return pl.pallas_call(
        body, name="deltanet_bwd", grid=(n, groups),
        in_specs=[part(0), part(1), part(2), pl.BlockSpec((CHUNK, 128), lambda c, grp: (n - 1 - c, 0)),
                  pl.BlockSpec((1, hp, HEAD_A, HEAD_A), lambda c, grp: (n - 1 - c, grp, 0, 0)),
                  pl.BlockSpec((1, hp, CHUNK, CHUNK), lambda c, grp: (n - 1 - c, grp, 0, 0)), part(0)],
        out_specs=[pl.BlockSpec((CHUNK, 3 * heads * HEAD_A), lambda c, grp: (n - 1 - c, 0)),
                   pl.BlockSpec((CHUNK, 128), lambda c, grp: (n - 1 - c, 0))],
        out_shape=[jax.ShapeDtypeStruct(qkv.shape, F32), jax.ShapeDtypeStruct((rows, 128), F32)],
        scratch_shapes=[pltpu.VMEM((heads, HEAD_A, HEAD_A), F32)],
        compiler_params=_cp("arbitrary", "arbitrary"),
    )(qkv, qkv, qkv, bg, hist, tinv_hist, do)


def _head_mask(heads, width):
    return (lax.broadcasted_iota(jnp.int32, (heads, width), 0)
            == lax.broadcasted_iota(jnp.int32, (heads, width), 1) // HEAD_B)


def _masked_rows(mask, row):
    return jnp.where(mask, row, 0.0).astype(BF16)


def _rwkv_fwd(r, w, k, v, a, b):
    rows, width = r.shape
    heads = width // HEAD_B
    ts = SCAN_STEPS

    def body(r_ref, w_ref, k_ref, v_ref, a_ref, b_ref, y_ref, hist_ref, s_ref):
        @pl.when(pl.program_id(0) == 0)
        def _():
            s_ref[...] = jnp.zeros_like(s_ref)

        mask = _head_mask(heads, width)
        onehot = mask.astype(BF16)
        onehot2 = jnp.concatenate([onehot, onehot], axis=0)
        bd = _block_diag_ones()

        spread_v = [_dot_tn(jnp.concatenate(_hi_lo(v_ref[j]), axis=0), onehot2) for j in range(ts)]
        a_next = pltpu.roll(a_ref[...], ts - 1, 0)
        b_dot_a, k_dot_a = _segsum_many([b_ref[...] * a_next, k_ref[...] * a_next], bd)
        s = s_ref[...]
        ys = []
        for j in range(0, ts, 2):
            row, nxt = pl.ds(j, 1), pl.ds(j + 1, 1)
            hist_ref[j] = s
            sa, base = _segsum_many([((s * a_ref[row, :]).astype(BF16),),
                                     ((s * (w_ref[row, :] * a_ref[nxt, :])).astype(BF16),)], bd)
            sa_next = base + sa * b_dot_a[j:j + 1] + spread_v[j] * k_dot_a[j:j + 1]
            s = s * w_ref[row, :] + sa * b_ref[row, :] + spread_v[j] * k_ref[row, :]
            hist_ref[j + 1] = s
            ys.append(_dot_nt(_masked_rows(mask, r_ref[row, :]), s.astype(BF16)))
            s = s * w_ref[nxt, :] + sa_next * b_ref[nxt, :] + spread_v[j + 1] * k_ref[nxt, :]
            ys.append(_dot_nt(_masked_rows(mask, r_ref[nxt, :]), s.astype(BF16)))
        for j in range(ts):
            y_ref[j] = ys[j]
        s_ref[...] = s

    blk = pl.BlockSpec((ts, width), lambda i: (i, 0))
    blk3 = pl.BlockSpec((ts, heads, HEAD_B), lambda i: (i, 0, 0))
    return pl.pallas_call(
        body, name="rwkv_fwd", grid=(rows // ts,),
        in_specs=[blk, blk, blk, blk3, blk, blk],
        out_specs=[blk3, pl.BlockSpec((ts, HEAD_B, width), lambda i: (i, 0, 0)),
                   pl.BlockSpec((HEAD_B, width), lambda i: (0, 0))],
        out_shape=[jax.ShapeDtypeStruct((rows, heads, HEAD_B), F32), jax.ShapeDtypeStruct((rows, HEAD_B, width), F32),
                   jax.ShapeDtypeStruct((HEAD_B, width), F32)],
        compiler_params=_cp("arbitrary"),
    )(r, w, k, v, a, b)


def _rwkv_bwd(r, w, k, v, a, b, hist, last, dy):
    rows, width = r.shape
    heads = width // HEAD_B
    ts = SCAN_STEPS
    nb = rows // ts

    def body(r_ref, w_ref, k_ref, v_ref, a_ref, b_ref, hist_ref, last_ref, dy_ref,
             dr_ref, dw_ref, dk_ref, dv_ref, da_ref, db_ref, g_ref, after_ref):
        @pl.when(pl.program_id(0) == 0)
        def _():
            g_ref[...] = jnp.zeros_like(g_ref)
            after_ref[...] = last_ref[...]

        mask = _head_mask(heads, width)
        onehot = mask.astype(BF16)
        bd = _block_diag_ones()

        def own_lanes(x):
            return jnp.sum(jnp.where(mask, x, 0.0), axis=0, keepdims=True)

        def colsum(x):
            return jnp.sum(x, axis=0, keepdims=True)

        dy_m = [dy_ref[j].astype(BF16) for j in range(ts)]
        spread_dy = [_dot_tn(dy_m[j], onehot) for j in range(ts)]
        state_after = [hist_ref[j + 1] if j < ts - 1 else after_ref[...] for j in range(ts)]
        dr = [own_lanes(_dot(dy_m[j], state_after[j].astype(BF16))) for j in range(ts)]
        sa_m = [_dot_nt(_masked_rows(mask, a_ref[pl.ds(j, 1), :]), hist_ref[j].astype(BF16)) for j in range(ts)]
        g = g_ref[...]
        dw, dk, db, da, dv = {}, {}, {}, {}, {}
        for j in reversed(range(ts)):
            row = pl.ds(j, 1)
            sp = hist_ref[j]
            g = g + spread_dy[j] * r_ref[row, :]
            (dsa,) = _segsum_many([((g * b_ref[row, :]).astype(BF16),)], bd)
            g_b = g.astype(BF16)
            both = _dot(jnp.concatenate([v_ref[j].astype(BF16), sa_m[j].astype(BF16)], axis=0), g_b)
            dk[j], db[j] = own_lanes(both[:heads]), own_lanes(both[heads:])
            dv[j] = _dot_nt(_masked_rows(mask, k_ref[row, :]), g_b)
            dw[j] = colsum(g * sp)
            da[j] = colsum(sp * dsa)
            g = g * w_ref[row, :] + dsa * a_ref[row, :]
        g_ref[...] = g
        after_ref[...] = hist_ref[0]
        for j in range(ts):
            dv_ref[j] = dv[j]
            for ref, vals in ((dr_ref, dr), (dw_ref, dw), (dk_ref, dk), (da_ref, da), (db_ref, db)):
                ref[pl.ds(j, 1), :] = vals[j]

    blk = pl.BlockSpec((ts, width), lambda i: (nb - 1 - i, 0))
    blk3 = pl.BlockSpec((ts, heads, HEAD_B), lambda i: (nb - 1 - i, 0, 0))
    state = pl.BlockSpec((HEAD_B, width), lambda i: (0, 0))
    return pl.pallas_call(
        body, name="rwkv_bwd", grid=(nb,),
        in_specs=[blk, blk, blk, blk3, blk, blk, pl.BlockSpec((ts, HEAD_B, width), lambda i: (nb - 1 - i, 0, 0)),
                  state, blk3],
        out_specs=[blk, blk, blk, blk3, blk, blk],
        out_shape=[jax.ShapeDtypeStruct((rows, width), F32)] * 3 + [jax.ShapeDtypeStruct((rows, heads, HEAD_B), F32)]
        + [jax.ShapeDtypeStruct((rows, width), F32)] * 2,
        scratch_shapes=[pltpu.VMEM((HEAD_B, width), F32), pltpu.VMEM((HEAD_B, width), F32)],
        compiler_params=_cp("arbitrary"),
    )(r, w, k, v, a, b, hist, last, dy)


def _live(row0, shape):
    return (row0 + lax.broadcasted_iota(jnp.int32, shape, 0)) >= PAD


def _norm_fn(row0, h, gain):
    return (_rms(h, gain),)


def _norm_res_fn(row0, h, gain):
    return _rms(h, gain), h


def _make_bg_fn(heads):
    def fn(row0, x, log_rate, dt_bias):
        lane = lax.broadcasted_iota(jnp.int32, x.shape, 1)
        beta = _sigmoid(x)
        g = -jnp.exp(log_rate) * _softplus(x + dt_bias)
        out = jnp.where(lane < heads, beta, jnp.where(lane < 2 * heads, g, 0.0))
        return (jnp.where(_live(row0, x.shape), out, 0.0),)
    return fn


def _b_pre_fn(row0, zf, w0, w_up, a0, a_up, g_up, k_k, k_a):
    d = w0.shape[1]
    r, k, v = zf[:, :d], zf[:, d:2 * d], zf[:, 2 * d:3 * d]
    lo = zf[:, 3 * d:3 * d + 128]
    lg = zf[:, 3 * d + 128:3 * d + LORA_PAD]
    lane = lax.broadcasted_iota(jnp.int32, lo.shape, 1)
    lw = _dot(jnp.where(lane < LORA_W, jnp.tanh(lo), 0.0), w_up)
    la = _dot(jnp.where(lane >= LORA_W, lo, 0.0), a_up)
    lane_g = lax.broadcasted_iota(jnp.int32, lg.shape, 1)
    gate = _dot(jnp.where(lane_g < LORA_G, _sigmoid(lg), 0.0), g_up)
    decay = jnp.exp(-jnp.exp(-_softplus(-(w0 + lw)) - 0.5))
    a = _sigmoid(a0 + la)
    kx = k * k_k
    kk = kx * lax.rsqrt(_segsum64(kx * kx) + 1e-6)
    k2 = k * (1.0 + (a - 1.0) * k_a)
    return r, decay, k2, v, -kk, kk * a, gate


def _post_fn(row0, o, zg, y, r, k2, v, gate, out_gain, r_k, ln_g, ln_b):
    d = o.shape[1]
    az, ga, gb = zg[:, :d], zg[:, d:2 * d], zg[:, 2 * d:]
    heads = d // HEAD_A
    parts = []
    for h in range(heads):
        oh = o[:, h * HEAD_A:(h + 1) * HEAD_A]
        parts.append(oh * lax.rsqrt(jnp.mean(oh * oh, axis=-1, keepdims=True) + EPS) * out_gain)
    o_a = jnp.concatenate(parts, axis=1) * _silu(az)
    mean = _segsum64(y) * (1.0 / HEAD_B)
    yc = y - mean
    var = _segsum64(yc * yc) * (1.0 / HEAD_B)
    yn = yc * lax.rsqrt(var + GN_EPS) * ln_g + ln_b
    o_b = (yn + _segsum64(r * k2 * r_k) * v) * gate
    return (_sigmoid(ga) * o_a + _sigmoid(gb) * o_b,)


def _loss(h3, target, gain, tb):
    rows, d = h3.shape

    def body(h_ref, t_ref, g_ref, dh_ref, dg_ref, l_ref):
        i = pl.program_id(0)
        live = (i * tb + lax.broadcasted_iota(jnp.int32, (tb, 1), 0)) >= CHUNK
        tgt = t_ref[...]

        def f(h, g):
            err = _rms(h, g) - tgt
            return 0.5 * jnp.sum(jnp.where(live, jnp.mean(err * err, axis=-1, keepdims=True), 0.0))

        val, vjp = jax.vjp(f, h_ref[...], g_ref[...])
        dh, dg = vjp(jnp.ones((), F32))
        dh_ref[...] = dh

        @pl.when(i == 0)
        def _():
            dg_ref[...] = jnp.zeros_like(dg_ref)
            l_ref[...] = jnp.zeros_like(l_ref)

        dg_ref[...] += dg
        l_ref[...] += jnp.full((1, 128), val, F32)

    blk = pl.BlockSpec((tb, d), lambda i: (i, 0))
    return pl.pallas_call(
        body, name="loss", grid=(rows // tb,),
        in_specs=[blk, blk, pl.BlockSpec((1, d), lambda i: (0, 0))],
        out_specs=[blk, pl.BlockSpec((1, d), lambda i: (0, 0)), pl.BlockSpec((1, 128), lambda i: (0, 0))],
        out_shape=[jax.ShapeDtypeStruct((rows, d), F32), jax.ShapeDtypeStruct((1, d), F32),
                   jax.ShapeDtypeStruct((1, 128), F32)],
        compiler_params=_cp("arbitrary"),
    )(h3, target, gain)


def _adamw_math(w, g, m, v):
    m2 = ADAM_B1 * m + (1.0 - ADAM_B1) * g
    v2 = ADAM_B2 * v + (1.0 - ADAM_B2) * (g * g)
    m_hat = m2 / (1.0 - ADAM_B1 ** ADAM_STEP)
    v_hat = v2 / (1.0 - ADAM_B2 ** ADAM_STEP)
    return -ADAM_LR * (m_hat / (jnp.sqrt(v_hat) + ADAM_EPS) + ADAM_WD * w), m2, v2


def _adamw(name, own, landed, w, m, v):
    rows, cols = w.shape
    if rows % 16 == 0:
        rb = _tb(rows, 128)
        grid, blk = (rows // rb,), pl.BlockSpec((rb, cols), lambda i: (i, 0))
        landed_blk = pl.BlockSpec((N_DEV - 1, rb, cols), lambda i: (0, i, 0))
    else:
        grid, blk = (cols // 128,), pl.BlockSpec((rows, 128), lambda i: (0, i))
        landed_blk = pl.BlockSpec((N_DEV - 1, rows, 128), lambda i: (0, 0, i))

    def body(o_ref, s_ref, w_ref, m_ref, v_ref, g_ref, d_ref, m2_ref, v2_ref):
        g = o_ref[...].astype(F32)
        for peer in range(N_DEV - 1):
            g = g + s_ref[peer].astype(F32)
        g_ref[...] = g
        d_ref[...], m2_ref[...], v2_ref[...] = _adamw_math(w_ref[...], g, m_ref[...], v_ref[...])

    return pl.pallas_call(
        body, name=name, grid=grid, in_specs=[blk, landed_blk, blk, blk, blk],
        out_specs=[blk] * 4, out_shape=[jax.ShapeDtypeStruct((rows, cols), F32)] * 4,
        compiler_params=_cp("parallel"),
    )(own, landed, w, m, v)


def _sum_slabs(name, slabs, rb):
    _, rows, cols = slabs.shape

    def body(s_ref, o_ref):
        g = s_ref[0]
        for dev in range(1, N_DEV):
            g = g + s_ref[dev]
        o_ref[...] = g

    return pl.pallas_call(
        body, name=name, grid=(rows // rb,),
        in_specs=[pl.BlockSpec((N_DEV, rb, cols), lambda i: (0, i, 0))],
        out_specs=pl.BlockSpec((rb, cols), lambda i: (i, 0)),
        out_shape=jax.ShapeDtypeStruct((rows, cols), F32), compiler_params=_cp("parallel"),
    )(slabs)


def _adamw_small(w, g, m, v):
    def body(w_ref, g_ref, m_ref, v_ref, d_ref, m2_ref, v2_ref):
        d_ref[...], m2_ref[...], v2_ref[...] = _adamw_math(w_ref[...], g_ref[...], m_ref[...], v_ref[...])

    return pl.pallas_call(body, name="adamw_small", out_shape=[jax.ShapeDtypeStruct(w.shape, F32)] * 3)(w, g, m, v)


def _place():
    return lax.axis_index("x"), lax.axis_index("y"), lax.axis_index("c")


def _index(p):
    return 4 * p[0] + 2 * p[1] + p[2]


def _all_gather(name, xs):
    n = len(xs)

    def body(*refs):
        x_refs, o_refs = refs[:n], refs[n:2 * n]
        send_sems, recv_sems, local_sems = refs[2 * n:]
        x, y, c = _place()
        me, sibling = (x, y, c), (x, y, 1 - c)
        chips = [(1 - x, y), (x, 1 - y), (1 - x, 1 - y)]

        def copy(i, k, block, to, src=None):
            dst = o_refs[i].at[_index(block)]
            return pltpu.make_async_remote_copy(src_ref=dst if src is None else src, dst_ref=dst,
                                                send_sem=send_sems.at[i, k], recv_sem=recv_sems.at[i, k],
                                                device_id=to, device_id_type=MESH_ID)

        mine = [pltpu.make_async_copy(x_refs[i], o_refs[i].at[_index(me)], local_sems.at[i]) for i in range(n)]
        for cp in mine:
            cp.start()
        first = []
        for i in range(n):
            first.append(copy(i, 0, me, sibling, src=x_refs[i]))
            first += [copy(i, 1 + j, me, (*chip, c), src=x_refs[i]) for j, chip in enumerate(chips)]
        for cp in first:
            cp.start()
        passed = []
        for j, chip in enumerate(chips):
            for i in range(n):
                copy(i, 1 + j, (*chip, c), me).wait_recv()
                cp = copy(i, 4 + j, (*chip, c), sibling)
                cp.start()
                passed.append(cp)
        for i in range(n):
            copy(i, 0, sibling, me).wait_recv()
            for j, chip in enumerate(chips):
                copy(i, 4 + j, (*chip, 1 - c), me).wait_recv()
        for cp in first + passed:
            cp.wait_send()
        for cp in mine:
            cp.wait()

    return pl.pallas_call(
        body, name=name, in_specs=[ANY] * n, out_specs=[ANY] * n,
        out_shape=[jax.ShapeDtypeStruct((N_DEV,) + x.shape, x.dtype) for x in xs],
        scratch_shapes=[pltpu.SemaphoreType.DMA((n, 7)), pltpu.SemaphoreType.DMA((n, 7)), pltpu.SemaphoreType.DMA((n,))],
    )(*xs)


def _exchange_start(name, xs, after=None, gather=False):
    n = len(xs)
    copies = n * (N_DEV - 1)
    extra = [] if after is None else [after]

    def body(*refs):
        x_refs, land_refs = refs[:n], refs[n:2 * n]
        sems = refs[2 * n + len(extra):2 * n + len(extra) + 2 * copies]
        token = refs[-1]
        for i, k, peer in _exchange_copies(n):
            _exchange_copy(x_refs, land_refs, sems, i, k, peer, gather).start()
        token[...] = jnp.zeros_like(token)

    lands = [lax.empty((N_DEV,) + x.shape if gather else (N_DEV - 1,) + x.shape[1:], x.dtype) for x in xs]
    out = pl.pallas_call(
        body, name=name,
        out_shape=(*[pltpu.SemaphoreType.DMA(())] * (2 * copies), *[pltpu.HBM(x.shape, x.dtype) for x in xs],
                   *[pltpu.HBM(l.shape, l.dtype) for l in lands], jax.ShapeDtypeStruct((8, 128), F32)),
        in_specs=[HBM_SPEC] * (2 * n) + [ANY] * len(extra),
        out_specs=(*[SEM_SPEC] * (2 * copies), *[HBM_SPEC] * (2 * n), pl.BlockSpec(memory_space=pltpu.VMEM)),
        input_output_aliases={i: 2 * copies + i for i in range(2 * n)},
        compiler_params=pltpu.CompilerParams(has_side_effects=pltpu.SideEffectType.DATAFLOW_SIDE_EFFECTING),
    )(*[pltpu.with_memory_space_constraint(a, pltpu.HBM) for a in list(xs) + lands], *extra)
    sems, rest = list(out[:2 * copies]), out[2 * copies:]
    return sems, list(rest[:n]), list(rest[n:2 * n]), rest[-1]


def _exchange_copies(n):
    x, y, c = _place()
    for k in range(1, N_DEV):
        peer = ((1 - x) if k & 4 else x, (1 - y) if k & 2 else y, (1 - c) if k & 1 else c)
        for i in range(n):
            yield i, k - 1, peer


def _exchange_copy(x_refs, land_refs, sems, i, k, peer, gather, arriving=False):
    copies = len(sems) // 2
    which = i * (N_DEV - 1) + k
    src = x_refs[i] if gather else x_refs[i].at[_index(peer)]
    dst = land_refs[i].at[_index(peer if arriving else _place())] if gather else land_refs[i].at[k]
    return pltpu.make_async_remote_copy(src_ref=src, dst_ref=dst, send_sem=sems[which], recv_sem=sems[copies + which],
                                        device_id=peer, device_id_type=MESH_ID)


def _exchange_wait(name, sems, xs, lands, after, gather=False):
    n = len(xs)

    def body(*refs):
        x_refs, land_refs = refs[:n], refs[n:2 * n]
        sem_refs = refs[2 * n:2 * n + len(sems)]
        for i, k, peer in _exchange_copies(n):
            _exchange_copy(x_refs, land_refs, sem_refs, i, k, peer, gather).wait_send()
            _exchange_copy(x_refs, land_refs, sem_refs, i, k, peer, gather, arriving=True).wait_recv()

    out = pl.pallas_call(
        body, name=name,
        out_shape=(*[pltpu.HBM(x.shape, x.dtype) for x in xs], *[pltpu.HBM(l.shape, l.dtype) for l in lands]),
        in_specs=[HBM_SPEC] * (2 * n) + [SEM_SPEC] * len(sems) + [ANY], out_specs=tuple([HBM_SPEC] * (2 * n)),
        input_output_aliases={i: i for i in range(2 * n)},
        compiler_params=pltpu.CompilerParams(has_side_effects=pltpu.SideEffectType.DATAFLOW_SIDE_EFFECTING),
    )(*xs, *lands, *sems, after)
    return list(out[:n]), list(out[n:])


def _pack(arrays):
    flat = jnp.concatenate([a.reshape(-1) for a in arrays])
    pad = (-flat.shape[0]) % 1024
    return jnp.pad(flat, (0, pad)).reshape(-1, 128)


def _unpack(packed, shapes):
    flat = packed.reshape(-1)
    out, pos = [], 0
    for s in shapes:
        size = 1
        for dim in s:
            size *= dim
        out.append(flat[pos:pos + size].reshape(s))
        pos += size
    return out


def _cols_from_slabs(stack):
    return jnp.transpose(stack, (1, 0, 2)).reshape(stack.shape[1], -1)


def kernel(x, meta_tokens, ffn1_norm, ffn1_w_gu, ffn1_w_down, mix_norm, w_in, a_conv_w, a_log_rate, a_dt_bias, a_out_norm, b_shift_mu, b_w0, b_w_up, b_a0, b_a_up, b_g_up, b_k_k, b_k_a, b_r_k, b_ln_gain, b_ln_bias, w_out, ffn2_norm, ffn2_w_gu, ffn2_w_down, final_norm, loss_target, m_meta_tokens, m_ffn1_norm, m_ffn1_w_gu, m_ffn1_w_down, m_mix_norm, m_w_in, m_a_conv_w, m_a_log_rate, m_a_dt_bias, m_a_out_norm, m_b_shift_mu, m_b_w0, m_b_w_up, m_b_a0, m_b_a_up, m_b_g_up, m_b_k_k, m_b_k_a, m_b_r_k, m_b_ln_gain, m_b_ln_bias, m_w_out, m_ffn2_norm, m_ffn2_w_gu, m_ffn2_w_down, m_final_norm, v_meta_tokens, v_ffn1_norm, v_ffn1_w_gu, v_ffn1_w_down, v_mix_norm, v_w_in, v_a_conv_w, v_a_log_rate, v_a_dt_bias, v_a_out_norm, v_b_shift_mu, v_b_w0, v_b_w_up, v_b_a0, v_b_a_up, v_b_g_up, v_b_k_k, v_b_k_a, v_b_r_k, v_b_ln_gain, v_b_ln_bias, v_w_out, v_ffn2_norm, v_ffn2_w_gu, v_ffn2_w_down, v_final_norm):
    names = ['meta_tokens', 'ffn1_norm', 'ffn1_w_gu', 'ffn1_w_down', 'mix_norm', 'w_in', 'a_conv_w', 'a_log_rate',
             'a_dt_bias', 'a_out_norm', 'b_shift_mu', 'b_w0', 'b_w_up', 'b_a0', 'b_a_up', 'b_g_up', 'b_k_k', 'b_k_a',
             'b_r_k', 'b_ln_gain', 'b_ln_bias', 'w_out', 'ffn2_norm', 'ffn2_w_gu', 'ffn2_w_down', 'final_norm']
    env = dict(locals())
    wts = {k: env[k] for k in names}
    mom_m = {k: env['m_' + k] for k in names}
    mom_v = {k: env['v_' + k] for k in names}
    big = ['ffn1_w_gu', 'ffn1_w_down', 'w_in', 'w_out', 'ffn2_w_gu', 'ffn2_w_down']
    col_sharded = {'ffn1_w_gu', 'w_in', 'ffn2_w_gu'}
    shard_of = lambda tree, k: tree[k][0].T if k in col_sharded else tree[k][0]
    small_sharded = ['meta_tokens', 'a_conv_w', 'b_w_up', 'b_a_up', 'b_g_up']
    replicated = [k for k in names if k not in big and k not in small_sharded]

    seq, d = x.shape[1], x.shape[2]
    rows = PAD + N_META + seq
    heads_a = d // HEAD_A
    tb_mm = _tb(rows, 416)
    tb_vjp = _tb(rows, 208)
    tb_dw = _tb(rows, 2080)
    tb_ffn = _tb(rows, 832)
    me = _index(_place())

    local_bf = {k: shard_of(wts, k).astype(BF16) for k in big}
    gu1, down1, meta = _all_gather("gather_ffn1", [local_bf['ffn1_w_gu'], local_bf['ffn1_w_down'], wts['meta_tokens']])
    small_rest = small_sharded[1:]
    late_keys = ['w_out', 'ffn2_w_gu', 'ffn2_w_down']
    gather_mid = _exchange_start("gather_start_mid", [local_bf['w_in'], _pack([wts[k][0] for k in small_rest])],
                                 after=gu1, gather=True)
    gather_late = _exchange_start("gather_start_late", [local_bf[k] for k in late_keys], after=gather_mid[-1],
                                  gather=True)

    def finish_gather(tag, started, after):
        sems, mine, lands, _ = started
        mine, lands = _exchange_wait("gather_wait_" + tag, sems, mine, lands, after, gather=True)
        return [lax.dynamic_update_index_in_dim(land, own[None], me, 0) for land, own in zip(lands, mine)]

    full = {'ffn1_w_gu': gu1, 'ffn1_w_down': down1.reshape(-1, d), 'meta_tokens': _cols_from_slabs(meta)}
    for k in replicated:
        full[k] = wts[k].reshape(1, -1)

    h0 = jnp.concatenate([jnp.zeros((PAD, d), F32) + gather_late[-1][:1, :1], full['meta_tokens'], x[0]], axis=0)
    h1 = _ffn_fwd("ffn1_fwd", h0, full['ffn1_norm'], full['ffn1_w_gu'], full['ffn1_w_down'], tb_ffn)
    (u,) = _tok_fwd("mix_norm_fwd", _norm_fn, [h1], [full['mix_norm']], [(d, BF16)], tb_mm)

    win_stack, small_stack = finish_gather("mid", gather_mid, u)
    full['w_in'] = win_stack.reshape(-1, d)
    small_flat, pos = small_stack.reshape(N_DEV, -1), 0
    for k in small_rest:
        shape = wts[k][0].shape
        full[k] = _cols_from_slabs(small_flat[:, pos:pos + shape[0] * shape[1]].reshape((N_DEV,) + shape))
        pos += shape[0] * shape[1]

    win = full['w_in']
    n_b = 3 * d + LORA_W + LORA_A + LORA_G
    off_beta, off_b = 4 * d, 4 * d + 2 * heads_a
    off_ga = off_b + n_b
    b_width = 3 * d + LORA_PAD
    zrows = lambda r: jnp.zeros((r, d), BF16)
    w_qkv = win[:3 * d]
    w_zg = jnp.concatenate([win[3 * d:4 * d], win[off_ga:off_ga + 2 * d]], axis=0)
    w_b = jnp.concatenate([win[off_b:off_b + n_b], zrows(b_width - n_b)], axis=0)
    w_bg = jnp.concatenate([win[off_beta:off_beta + 2 * heads_a], zrows(128 - 2 * heads_a)], axis=0)

    def lanes(vec, start, width):
        return jnp.pad(vec.reshape(1, -1), ((0, 0), (start, width - start - vec.size)))

    log_rate = lanes(wts['a_log_rate'], heads_a, 128)
    dt_bias = lanes(wts['a_dt_bias'], heads_a, 128)
    mu = lanes(wts['b_shift_mu'], 0, b_width)
    w_up = jnp.pad(full['b_w_up'], ((0, 128 - LORA_W), (0, 0)))
    a_up = jnp.pad(full['b_a_up'], ((LORA_W, 0), (0, 0)))
    g_up = jnp.pad(full['b_g_up'], ((0, 256 - LORA_G), (0, 0)))
    b_pars = [full['b_w0'], w_up, full['b_a0'], a_up, g_up, full['b_k_k'], full['b_k_a']]
    post_pars = [full['a_out_norm'], full['b_r_k'], full['b_ln_gain'], full['b_ln_bias']]
    bg_fn = _make_bg_fn(heads_a)

    z_qkv = _mm("in_qkv", u, w_qkv, trans_b=True, tb=tb_mm, tn=_col_tile(3 * d, 1536))
    z_zg = _mm("in_zg", u, w_zg, trans_b=True, tb=tb_mm, tn=_col_tile(3 * d, 1536))
    z_b = _mm("in_b", u, w_b, trans_b=True, tb=tb_mm, tn=_col_tile(b_width, 1536))
    z_bg = _mm("in_bg", u, w_bg, trans_b=True, tb=tb_mm, tn=128)
    qkv = _a_pre_fwd(z_qkv, full['a_conv_w'])
    (bg,) = _tok_fwd("bg_fwd", bg_fn, [z_bg], [log_rate, dt_bias], [(128, F32)], tb_mm)
    o_dn, dn_hist, dn_tinv = _dn_fwd(qkv, bg)
    zf = _shift_fwd(z_b, mu)
    rr, ww, kk2, vv, av, bv, gate = _tok_fwd("b_pre_fwd", _b_pre_fn, [zf], b_pars, [(d, F32)] * 7, tb_vjp)
    per_head = lambda t: t.reshape(rows, d // HEAD_B, HEAD_B)
    y_heads, b_hist, b_last = _rwkv_fwd(rr, ww, kk2, per_head(vv), av, bv)
    y_b = y_heads.reshape(rows, d)
    w_out_stack, full['ffn2_w_gu'], down2 = finish_gather("late", gather_late, y_heads)
    full['w_out'], full['ffn2_w_down'] = w_out_stack.reshape(-1, d), down2.reshape(-1, d)
    post_toks = [o_dn, z_zg, y_b, rr, kk2, vv, gate]
    (merged,) = _tok_fwd("post_fwd", _post_fn, post_toks, post_pars, [(d, BF16)], tb_vjp)
    h2 = _mm("out_proj", merged, full['w_out'], add=h1, tb=tb_mm, tn=d)
    h3 = _ffn_fwd("ffn2_fwd", h2, full['ffn2_norm'], full['ffn2_w_gu'], full['ffn2_w_down'], tb_ffn)

    target = jnp.pad(loss_target[0], ((CHUNK, 0), (0, 0)))
    dh3, g_final, loss_part = _loss(h3, target, full['final_norm'].reshape(1, d), tb_vjp)

    def ffn_backward(tag, h, dout, key_norm, key_gu, key_down):
        dh, dh_bf, dgain, xn, act, dgate, dup, dhalf = _ffn_bwd(tag + "_bwd", h, full[key_norm], dout, full[key_gu],
                                                                full[key_down], tb_mm)
        fc = dgate.shape[2]
        d_gu = jnp.concatenate([_mm_tn_from_slabs(tag + "_dw_gate", dgate, xn, tk=tb_dw).reshape(-1, fc, d),
                                _mm_tn_from_slabs(tag + "_dw_up", dup, xn, tk=tb_dw).reshape(-1, fc, d)], axis=0)
        d_down = _mm_tn_from_slabs(tag + "_dw_down", act, dhalf, tk=tb_dw).reshape(N_DEV, -1, d)
        return dh, dh_bf, dgain, d_gu, d_down

    dh2, dh2_bf, g_ffn2_norm, g_ffn2_gu, g_ffn2_down = ffn_backward("ffn2", h2, dh3, 'ffn2_norm', 'ffn2_w_gu',
                                                                    'ffn2_w_down')
    g_w_out = _mm_tn("dw_out", merged, dh2_bf, tm=d, tn=d, tk=tb_dw).reshape(N_DEV, -1, d)

    def start_exchange(tag, keys, slabs, after=None):
        sems, kept, lands, token = _exchange_start("exchange_start_" + tag, slabs, after)
        return (tag, keys, sems, kept, lands), token

    ex_ffn2, token_ffn2 = start_exchange("ffn2", ['ffn2_w_gu', 'ffn2_w_down', 'w_out'], [g_ffn2_gu, g_ffn2_down, g_w_out])
    dmerged = _mm("d_merged", dh2_bf, full['w_out'], trans_b=True, after=token_ffn2, tb=tb_mm, tn=d)
    post_grads = _tok_bwd("post_bwd", _post_fn, post_toks, post_pars, [[dmerged]], list(range(7)), tb_vjp,
                          [F32, BF16] + [F32] * 5)
    do_dn, dz_zg, dy_b, dr1, dk1, dv1, dgate = post_grads[:7]
    g_out_norm, g_r_k, g_ln_g, g_ln_b = post_grads[7:]
    dr2, dw2, dk2, dv_heads, da2, db2 = _rwkv_bwd(rr, ww, kk2, per_head(vv), av, bv, b_hist, b_last, per_head(dy_b))
    dv2 = dv_heads.reshape(rows, d)
    b_grads = _tok_bwd("b_pre_bwd", _b_pre_fn, [zf], b_pars,
                       [[dr1, dr2], [dw2], [dk1, dk2], [dv1, dv2], [da2], [db2], [dgate]], [0], tb_vjp)
    dzf = b_grads[0]
    g_w0, g_w_up, g_a0, g_a_up, g_g_up, g_k_k, g_k_a = b_grads[1:]
    dz_b, g_mu = _shift_bwd(z_b, mu, dzf)
    dqkv, dbg = _dn_bwd(qkv, bg, dn_hist, dn_tinv, do_dn)
    dz_qkv, g_conv = _a_pre_bwd(z_qkv, full['a_conv_w'], dqkv)
    dz_bg, g_log_rate, g_dt_bias = _tok_bwd("bg_bwd", bg_fn, [z_bg], [log_rate, dt_bias], [[dbg]], [0], tb_mm, [BF16])

    du = None
    g_w_in_parts = []
    for tag, dz, wpiece in (("qkv", dz_qkv, w_qkv), ("zg", dz_zg, w_zg), ("b", dz_b, w_b), ("bg", dz_bg, w_bg)):
        du = _mm("du_" + tag, dz, wpiece, add=du, tb=tb_mm, tn=d)
        g_w_in_parts.append(_mm_tn("dw_in_" + tag, dz, u, tm=_col_tile(dz.shape[1], 1536), tn=d, tk=tb_dw))
    gp_qkv, gp_zg, gp_b, gp_bg = g_w_in_parts
    g_w_in = jnp.concatenate([gp_qkv, gp_zg[:d], gp_bg[:2 * heads_a], gp_b[:n_b], gp_zg[d:]],
                             axis=0).reshape(N_DEV, -1, d)
    ex_w_in, token_w_in = start_exchange("w_in", ['w_in'], [g_w_in])
    dh1, g_mix_norm = _tok_bwd("mix_norm_bwd", _norm_res_fn, [h1], [full['mix_norm']], [[du], [dh2]], [0], tb_vjp,
                               after=token_w_in)
    dh0, _, g_ffn1_norm, g_ffn1_gu, g_ffn1_down = ffn_backward("ffn1", h0, dh1, 'ffn1_norm', 'ffn1_w_gu', 'ffn1_w_down')

    small_full = {
        'meta_tokens': dh0[PAD:CHUNK], 'ffn1_norm': g_ffn1_norm, 'mix_norm': g_mix_norm, 'a_conv_w': g_conv,
        'a_log_rate': g_log_rate[:, heads_a:2 * heads_a], 'a_dt_bias': g_dt_bias[:, heads_a:2 * heads_a],
        'a_out_norm': g_out_norm, 'b_shift_mu': g_mu[:, :n_b], 'b_w0': g_w0, 'b_w_up': g_w_up[:LORA_W],
        'b_a0': g_a0, 'b_a_up': g_a_up[LORA_W:], 'b_g_up': g_g_up[:LORA_G], 'b_k_k': g_k_k, 'b_k_a': g_k_a,
        'b_r_k': g_r_k, 'b_ln_gain': g_ln_g, 'b_ln_bias': g_ln_b, 'ffn2_norm': g_ffn2_norm, 'final_norm': g_final,
    }
    small_names = list(small_full)
    packed = _pack([small_full[k] for k in small_names] + [loss_part[:, :1]])
    (all_parts,) = _all_gather("gather_small_grads", [packed])
    ex_ffn1, _ = start_exchange("ffn1", ['ffn1_w_gu', 'ffn1_w_down'], [g_ffn1_gu, g_ffn1_down], after=all_parts)
    summed = _sum_slabs("sum_small_grads", all_parts, packed.shape[0])
    pieces = _unpack(summed, [small_full[k].shape for k in small_names] + [(1, 1)])
    small_grad = dict(zip(small_names, pieces[:-1]))
    loss = pieces[-1].reshape(())

    grads, deltas, new_m, new_v = {}, {}, {}, {}
    local_small = {}
    for k in small_names:
        g = small_grad[k]
        if k in small_sharded:
            width = wts[k].shape[-1]
            g = lax.dynamic_slice_in_dim(g, me * width, width, axis=1)
        local_small[k] = g.reshape(wts[k].shape)
    pk = lambda tree: _pack([tree[k] for k in small_names])
    dl_s, m_s, v_s = _adamw_small(pk(wts), pk(local_small), pk(mom_m), pk(mom_v))
    shapes = [wts[k].shape for k in small_names]
    for k, dl, m2, v2 in zip(small_names, _unpack(dl_s, shapes), _unpack(m_s, shapes), _unpack(v_s, shapes)):
        grads[k], deltas[k], new_m[k], new_v[k] = local_small[k], dl, m2, v2

    done = dl_s
    for tag, keys, sems, kept, lands in (ex_ffn2, ex_w_in, ex_ffn1):
        kept, lands = _exchange_wait("exchange_wait_" + tag, sems, kept, lands, done)
        for k, slabs, landed in zip(keys, kept, lands):
            own = lax.dynamic_index_in_dim(slabs, me, axis=0, keepdims=False)
            res = _adamw("adamw_" + k, own, landed, shard_of(wts, k), shard_of(mom_m, k), shard_of(mom_v, k))
            done = res[1]
            res = [(t.T if k in col_sharded else t)[None] for t in res]
            grads[k], deltas[k], new_m[k], new_v[k] = res

    grad_x = dh0[CHUNK:][None]
    return (loss, grad_x, *[grads[k] for k in names], *[deltas[k] for k in names],
            *[new_m[k] for k in names], *[new_v[k] for k in names])
```

```python
import functools

import jax
import jax.numpy as jnp
from jax import lax
from jax.experimental import pallas as pl
from jax.experimental.pallas import tpu as pltpu

F32 = jnp.float32
BF16 = jnp.bfloat16
N_DEV = 8
N_META = 16
CHUNK = 64
PAD = CHUNK - N_META
HEAD_A = 128
HEAD_B = 64
LORA_W, LORA_A, LORA_G = 64, 64, 160
LORA_PAD = 384
EPS = 1e-6
GN_EPS = HEAD_B * 1e-5
ADAM_LR, ADAM_B1, ADAM_B2, ADAM_EPS, ADAM_WD, ADAM_STEP = 0.001, 0.9, 0.999, 1e-08, 0.01, 10
SCAN_STEPS = 32
MXU_WIDTH = 256
VMEM_LIMIT = 56 * 1024 * 1024
DN_PRECISION = lax.Precision.HIGH
MESH_ID = pl.DeviceIdType.MESH
ANY = pl.BlockSpec(memory_space=pl.ANY)
HBM_SPEC = pl.BlockSpec(memory_space=pltpu.HBM)
SEM_SPEC = pl.BlockSpec(memory_space=pltpu.SEMAPHORE)


def _cp(*sem):
    return pltpu.CompilerParams(dimension_semantics=sem, vmem_limit_bytes=VMEM_LIMIT)


def _tb(t, target):
    best = 16
    for d in range(16, target + 1, 16):
        if t % d == 0:
            best = d
    return best


def _sigmoid(x):
    return 1.0 / (1.0 + jnp.exp(-x))


def _silu(x):
    return x * _sigmoid(x)


def _softplus(x):
    return jnp.maximum(x, 0.0) + jnp.log(1.0 + jnp.exp(-jnp.abs(x)))


def _dot_nt(a, b, precision=None):
    return lax.dot_general(a, b, (((1,), (1,)), ((), ())), preferred_element_type=F32, precision=precision)


def _dot_tn(a, b, precision=None):
    return lax.dot_general(a, b, (((0,), (0,)), ((), ())), preferred_element_type=F32, precision=precision)


def _dot(a, b, precision=None):
    return jnp.dot(a, b, preferred_element_type=F32, precision=precision)


def _block_diag_ones():
    i = lax.broadcasted_iota(jnp.int32, (MXU_WIDTH, MXU_WIDTH), 0) // HEAD_B
    j = lax.broadcasted_iota(jnp.int32, (MXU_WIDTH, MXU_WIDTH), 1) // HEAD_B
    return (i == j).astype(BF16)


def _hi_lo(x):
    hi = x.astype(BF16)
    return hi, (x - hi.astype(F32)).astype(BF16)


def _segsum_many(xs, bd):
    groups = [x if isinstance(x, tuple) else _hi_lo(x) for x in xs]
    rows = groups[0][0].shape[0]
    stacked = jnp.concatenate([p for grp in groups for p in grp], axis=0)
    out = jnp.concatenate([_dot(stacked[:, s:s + MXU_WIDTH], bd) for s in range(0, stacked.shape[1], MXU_WIDTH)], axis=1)
    res, pos = [], 0
    for grp in groups:
        acc = out[pos:pos + rows]
        for j in range(1, len(grp)):
            acc = acc + out[pos + j * rows:pos + (j + 1) * rows]
        res.append(acc)
        pos += len(grp) * rows
    return res


def _segsum_impl(x):
    return _segsum_many([x], _block_diag_ones())[0]


@jax.custom_vjp
def _segsum64(x):
    return _segsum_impl(x)


_segsum64.defvjp(lambda x: (_segsum_impl(x), None), lambda _, ct: (_segsum_impl(ct),))


def _tok(t):
    return t if isinstance(t, tuple) else (t, t.shape[1], 0)


def _tok_spec(tb, width, colblk):
    return pl.BlockSpec((tb, width), lambda i: (i, colblk))


def _par_spec(p):
    return pl.BlockSpec(p.shape, lambda i: (0, 0))


def _tok_fwd(name, fn, toks, pars, outs, tb):
    toks = [_tok(t) for t in toks]
    rows = toks[0][0].shape[0]
    n_in = len(toks) + len(pars)

    def body(*refs):
        row0 = pl.program_id(0) * tb
        res = fn(row0, *[r[...] for r in refs[:n_in]])
        for r, o in zip(refs[n_in:], res):
            r[...] = o.astype(r.dtype)

    return pl.pallas_call(
        body, name=name, grid=(rows // tb,),
        in_specs=[_tok_spec(tb, w, c) for _, w, c in toks] + [_par_spec(p) for p in pars],
        out_specs=[_tok_spec(tb, w, 0) for w, _ in outs],
        out_shape=[jax.ShapeDtypeStruct((rows, w), dt) for w, dt in outs],
        compiler_params=_cp("parallel"),
    )(*[a for a, _, _ in toks], *pars)


def _tok_bwd(name, fn, toks, pars, cts, want, tb, want_dtypes=None, after=None):
    toks = [_tok(t) for t in toks]
    want_dtypes = want_dtypes or [F32] * len(want)
    cts = [[_tok(c) for c in group] for group in cts]
    flat_cts = [c for group in cts for c in group]
    rows = toks[0][0].shape[0]
    n_tok, n_par, n_ct = len(toks), len(pars), len(flat_cts)
    extra = [] if after is None else [after]

    def body(*refs):
        i = pl.program_id(0)
        row0 = i * tb
        prim = [r[...].astype(F32) for r in refs[:n_tok + n_par]]
        ct_refs = list(refs[n_tok + n_par:n_tok + n_par + n_ct])
        out_refs = refs[n_tok + n_par + n_ct + len(extra):]
        res, vjp = jax.vjp(lambda *a: fn(row0, *a), *prim)
        ct = []
        for group, o in zip(cts, res):
            acc = None
            for _ in group:
                v = ct_refs.pop(0)[...].astype(F32)
                acc = v if acc is None else acc + v
            ct.append(acc.astype(o.dtype))
        grads = vjp(tuple(ct))
        for r, k in zip(out_refs[:len(want)], want):
            r[...] = grads[k].astype(r.dtype)

        @pl.when(i == 0)
        def _():
            for r in out_refs[len(want):]:
                r[...] = jnp.zeros_like(r)

        for r, g in zip(out_refs[len(want):], grads[n_tok:]):
            r[...] += g

    return pl.pallas_call(
        body, name=name, grid=(rows // tb,),
        in_specs=[_tok_spec(tb, w, c) for _, w, c in toks] + [_par_spec(p) for p in pars]
        + [_tok_spec(tb, w, c) for _, w, c in flat_cts] + [ANY] * len(extra),
        out_specs=[_tok_spec(tb, toks[k][1], 0) for k in want] + [_par_spec(p) for p in pars],
        out_shape=[jax.ShapeDtypeStruct((rows, toks[k][1]), dt) for k, dt in zip(want, want_dtypes)]
        + [jax.ShapeDtypeStruct(p.shape, F32) for p in pars],
        compiler_params=_cp("arbitrary"),
    )(*[a for a, _, _ in toks], *pars, *[a for a, _, _ in flat_cts], *extra)


def _mm(name, a, b, *, trans_b=False, add=None, after=None, tb, tn):
    rows, k = a.shape
    n = b.shape[0] if trans_b else b.shape[1]

    def body(*refs):
        a_ref, b_ref = refs[:2]
        o_ref = refs[-1]
        acc = _dot_nt(a_ref[...], b_ref[...]) if trans_b else _dot(a_ref[...], b_ref[...])
        if add is not None:
            acc = acc + refs[2][...]
        o_ref[...] = acc

    in_specs = [pl.BlockSpec((tb, k), lambda i, j: (i, 0)),
                pl.BlockSpec((tn, k), lambda i, j: (j, 0)) if trans_b else pl.BlockSpec((k, tn), lambda i, j: (0, j))]
    args = [a, b]
    if add is not None:
        in_specs.append(pl.BlockSpec((tb, tn), lambda i, j: (i, j)))
        args.append(add)
    if after is not None:
        in_specs.append(ANY)
        args.append(after)
    return pl.pallas_call(
        body, name=name, grid=(rows // tb, n // tn), in_specs=in_specs,
        out_specs=pl.BlockSpec((tb, tn), lambda i, j: (i, j)),
        out_shape=jax.ShapeDtypeStruct((rows, n), F32),
        compiler_params=_cp("parallel", "parallel"),
    )(*args)


def _mm_tn_call(name, grid, a, b, a_spec, b_spec, o_spec, acc_shape, out_shape):
    last = len(grid) - 1

    def body(a_ref, b_ref, o_ref, acc_ref):
        k = pl.program_id(last)

        @pl.when(k == 0)
        def _():
            acc_ref[...] = jnp.zeros_like(acc_ref)

        a_blk = a_ref[0] if len(a_ref.shape) == 3 else a_ref[...]
        b_blk = b_ref[0] if len(b_ref.shape) == 3 else b_ref[...]
        acc_ref[...] += _dot_tn(a_blk, b_blk)

        @pl.when(k == grid[last] - 1)
        def _():
            if len(o_ref.shape) == 3:
                o_ref[0] = acc_ref[...].astype(o_ref.dtype)
            else:
                o_ref[...] = acc_ref[...].astype(o_ref.dtype)

    return pl.pallas_call(
        body, name=name, grid=grid, in_specs=[a_spec, b_spec], out_specs=o_spec,
        out_shape=jax.ShapeDtypeStruct(out_shape, BF16), scratch_shapes=[pltpu.VMEM(acc_shape, F32)],
        compiler_params=_cp(*(["parallel"] * last + ["arbitrary"])),
    )(a, b)


def _mm_tn(name, a, b, *, tm, tn, tk):
    rows, m = a.shape
    n = b.shape[1]
    return _mm_tn_call(name, (m // tm, n // tn, rows // tk), a, b,
                       pl.BlockSpec((tk, tm), lambda i, j, k: (k, i)), pl.BlockSpec((tk, tn), lambda i, j, k: (k, j)),
                       pl.BlockSpec((tm, tn), lambda i, j, k: (i, j)), (tm, tn), (m, n))


def _mm_tn_from_slabs(name, a3, b, *, tk):
    s, rows, c = a3.shape
    n = b.shape[1]
    return _mm_tn_call(name, (s, rows // tk), a3, b,
                       pl.BlockSpec((1, tk, c), lambda i, k: (i, k, 0)), pl.BlockSpec((tk, n), lambda i, k: (k, 0)),
                       pl.BlockSpec((c, n), lambda i, k: (i, 0)), (c, n), (s * c, n))


def _col_tile(n, target):
    if n <= target:
        return n
    best = 128
    for d in range(128, target + 1, 128):
        if n % d == 0:
            best = d
    return best


def _rms(x, gain):
    return x * lax.rsqrt(jnp.mean(x * x, axis=-1, keepdims=True) + EPS) * gain


def _ffn_specs(d, fc, nj):
    return [pl.BlockSpec((1, fc, d), lambda i, j: (j, 0, 0)), pl.BlockSpec((1, fc, d), lambda i, j: (nj + j, 0, 0)),
            pl.BlockSpec((fc, d), lambda i, j: (j, 0))]


def _ffn_fwd(name, h, gain, wgu, wd, tb):
    rows, d = h.shape
    nj = wgu.shape[0] // 2
    fc = wgu.shape[1]

    def body(h_ref, g_ref, wg_ref, wu_ref, wd_ref, o_ref, xn_s, acc_s):
        j = pl.program_id(1)

        @pl.when(j == 0)
        def _():
            xn_s[...] = _rms(h_ref[...], g_ref[...]).astype(BF16)
            acc_s[...] = jnp.zeros_like(acc_s)

        wg, wu, wdn = wg_ref[0], wu_ref[0], wd_ref[...]
        for half in range(2):
            rs = pl.ds(half * (tb // 2), tb // 2)
            xn = xn_s[rs, :]
            gate = _dot_nt(xn, wg)
            up = _dot_nt(xn, wu)
            acc_s[rs, :] += _dot((_silu(gate) * up).astype(BF16), wdn)

        @pl.when(j == nj - 1)
        def _():
            o_ref[...] = h_ref[...] + 0.5 * acc_s[...]

    return pl.pallas_call(
        body, name=name, grid=(rows // tb, nj),
        in_specs=[pl.BlockSpec((tb, d), lambda i, j: (i, 0)), pl.BlockSpec((1, d), lambda i, j: (0, 0))]
        + _ffn_specs(d, fc, nj),
        out_specs=pl.BlockSpec((tb, d), lambda i, j: (i, 0)),
        out_shape=jax.ShapeDtypeStruct((rows, d), F32),
        scratch_shapes=[pltpu.VMEM((tb, d), BF16), pltpu.VMEM((tb, d), F32)],
        compiler_params=_cp("parallel", "arbitrary"),
    )(h, gain, wgu, wgu, wd)


def _ffn_bwd(name, h, gain, dout, wgu, wd, tb):
    rows, d = h.shape
    nj = wgu.shape[0] // 2
    fc = wgu.shape[1]

    def body(h_ref, g_ref, do_ref, wg_ref, wu_ref, wd_ref,
             dh_ref, dhb_ref, dg_ref, xn_ref, act_ref, dgate_ref, dup_ref, dhalf_ref, dxn_s):
        i, j = pl.program_id(0), pl.program_id(1)

        @pl.when(j == 0)
        def _():
            xn_ref[...] = _rms(h_ref[...], g_ref[...]).astype(BF16)
            dhalf_ref[...] = (0.5 * do_ref[...]).astype(BF16)
            dxn_s[...] = jnp.zeros_like(dxn_s)

        wg, wu, wdn = wg_ref[0], wu_ref[0], wd_ref[...]
        for half in range(2):
            rs = pl.ds(half * (tb // 2), tb // 2)
            xn = xn_ref[rs, :]
            gate = _dot_nt(xn, wg)
            up = _dot_nt(xn, wu)
            sg = _sigmoid(gate)
            dact = _dot_nt(dhalf_ref[rs, :], wdn)
            act_ref[0, rs, :] = (gate * sg * up).astype(BF16)
            dgate = (dact * up * (sg * (1.0 + gate * (1.0 - sg)))).astype(BF16)
            dup = (dact * gate * sg).astype(BF16)
            dgate_ref[0, rs, :] = dgate
            dup_ref[0, rs, :] = dup
            dxn_s[rs, :] += _dot(dgate, wg) + _dot(dup, wu)

        @pl.when((i == 0) & (j == 0))
        def _():
            dg_ref[...] = jnp.zeros_like(dg_ref)

        @pl.when(j == nj - 1)
        def _():
            x = h_ref[...]
            r = lax.rsqrt(jnp.mean(x * x, axis=-1, keepdims=True) + EPS)
            dxn = dxn_s[...]
            dyg = dxn * g_ref[...]
            dh = do_ref[...] + r * dyg - x * (r * r * r) * jnp.mean(dyg * x, axis=-1, keepdims=True)
            dh_ref[...] = dh
            dhb_ref[...] = dh.astype(BF16)
            dg_ref[...] += jnp.sum(dxn * x * r, axis=0, keepdims=True)

    row_d = pl.BlockSpec((tb, d), lambda i, j: (i, 0))
    slab = pl.BlockSpec((1, tb, fc), lambda i, j: (j, i, 0))
    hidden = jax.ShapeDtypeStruct((nj, rows, fc), BF16)
    return pl.pallas_call(
        body, name=name, grid=(rows // tb, nj),
        in_specs=[row_d, pl.BlockSpec((1, d), lambda i, j: (0, 0)), row_d] + _ffn_specs(d, fc, nj),
        out_specs=[row_d, row_d, pl.BlockSpec((1, d), lambda i, j: (0, 0)), row_d, slab, slab, slab, row_d],
        out_shape=[jax.ShapeDtypeStruct((rows, d), F32), jax.ShapeDtypeStruct((rows, d), BF16),
                   jax.ShapeDtypeStruct((1, d), F32), jax.ShapeDtypeStruct((rows, d), BF16),
                   hidden, hidden, hidden, jax.ShapeDtypeStruct((rows, d), BF16)],
        scratch_shapes=[pltpu.VMEM((tb, d), F32)],
        compiler_params=_cp("arbitrary", "arbitrary"),
    )(h, gain, dout, wgu, wgu, wd)


def _shift_rows(x, s):
    return pltpu.roll(x, s % x.shape[0], 0)


def _a_post(c, which):
    s = _silu(c)
    n = s * lax.rsqrt(jnp.sum(s * s, axis=-1, keepdims=True) + 1e-6)
    scale = jnp.where(which == 0, HEAD_A ** -0.5, 1.0)
    return jnp.where(which == 2, s, n * scale)


def _conv(x, w):
    return x * w[3:4] + _shift_rows(x, 1) * w[2:3] + _shift_rows(x, 2) * w[1:2] + _shift_rows(x, 3) * w[0:1]


def _a_pre_fwd(zqkv, conv_w):
    rows, width = zqkv.shape
    heads = width // (3 * HEAD_A)

    def body(x_ref, w_ref, o_ref):
        which = pl.program_id(0) // heads
        live = lax.broadcasted_iota(jnp.int32, (rows, HEAD_A), 0) >= PAD
        o_ref[...] = jnp.where(live, _a_post(_conv(x_ref[...], w_ref[...]), which), 0.0)

    return pl.pallas_call(
        body, name="a_pre_fwd", grid=(width // HEAD_A,),
        in_specs=[pl.BlockSpec((rows, HEAD_A), lambda c: (0, c)), pl.BlockSpec((4, HEAD_A), lambda c: (0, c))],
        out_specs=pl.BlockSpec((rows, HEAD_A), lambda c: (0, c)),
        out_shape=jax.ShapeDtypeStruct((rows, width), F32),
        compiler_params=_cp("parallel"),
    )(zqkv, conv_w)


def _a_pre_bwd(zqkv, conv_w, dqkv):
    rows, width = zqkv.shape
    heads = width // (3 * HEAD_A)

    def body(x_ref, w_ref, ct_ref, dx_ref, dw_ref):
        which = pl.program_id(0) // heads
        live = lax.broadcasted_iota(jnp.int32, (rows, HEAD_A), 0) >= PAD
        x, w = x_ref[...], w_ref[...]
        _, vjp = jax.vjp(lambda c: _a_post(c, which), _conv(x, w))
        (dc,) = vjp(jnp.where(live, ct_ref[...], 0.0))
        dc = jnp.where(live, dc, 0.0)
        dx_ref[...] = (dc * w[3:4] + _shift_rows(dc, -1) * w[2:3] + _shift_rows(dc, -2) * w[1:2]
                       + _shift_rows(dc, -3) * w[0:1]).astype(BF16)
        dw_ref[...] = jnp.concatenate(
            [jnp.sum(dc * (_shift_rows(x, 3 - j) if j < 3 else x), axis=0, keepdims=True) for j in range(4)], axis=0)

    col = pl.BlockSpec((rows, HEAD_A), lambda c: (0, c))
    wsp = pl.BlockSpec((4, HEAD_A), lambda c: (0, c))
    return pl.pallas_call(
        body, name="a_pre_bwd", grid=(width // HEAD_A,),
        in_specs=[col, wsp, col], out_specs=[col, wsp],
        out_shape=[jax.ShapeDtypeStruct((rows, width), BF16), jax.ShapeDtypeStruct((4, width), F32)],
        compiler_params=_cp("parallel"),
    )(zqkv, conv_w, dqkv)


SHIFT_TILE = 384


def _shift_fwd(zb, mu):
    rows, width = zb.shape

    def body(x_ref, mu_ref, o_ref):
        x = x_ref[...]
        first = lax.broadcasted_iota(jnp.int32, x.shape, 0) == 0
        prev = jnp.where(first, 0.0, _shift_rows(x, 1))
        o_ref[...] = x + (prev - x) * mu_ref[...]

    col = pl.BlockSpec((rows, SHIFT_TILE), lambda c: (0, c))
    return pl.pallas_call(
        body, name="shift_fwd", grid=(width // SHIFT_TILE,),
        in_specs=[col, pl.BlockSpec((1, SHIFT_TILE), lambda c: (0, c))], out_specs=col,
        out_shape=jax.ShapeDtypeStruct((rows, width), F32), compiler_params=_cp("parallel"),
    )(zb, mu)


def _shift_bwd(zb, mu, dzf):
    rows, width = zb.shape

    def body(x_ref, mu_ref, ct_ref, dx_ref, dmu_ref):
        x, ct, mu_v = x_ref[...], ct_ref[...], mu_ref[...]
        row = lax.broadcasted_iota(jnp.int32, x.shape, 0)
        prev = jnp.where(row == 0, 0.0, _shift_rows(x, 1))
        nxt = jnp.where(row == rows - 1, 0.0, _shift_rows(ct, -1))
        dx_ref[...] = (ct * (1.0 - mu_v) + nxt * mu_v).astype(BF16)
        dmu_ref[...] = jnp.sum(ct * (prev - x), axis=0, keepdims=True)

    col = pl.BlockSpec((rows, SHIFT_TILE), lambda c: (0, c))
    msp = pl.BlockSpec((1, SHIFT_TILE), lambda c: (0, c))
    return pl.pallas_call(
        body, name="shift_bwd", grid=(width // SHIFT_TILE,),
        in_specs=[col, msp, col], out_specs=[col, msp],
        out_shape=[jax.ShapeDtypeStruct((rows, width), BF16), jax.ShapeDtypeStruct((1, width), F32)],
        compiler_params=_cp("parallel"),
    )(zb, mu, dzf)


def _neumann_inverse(p):
    heads = range(len(p))
    eye = (lax.broadcasted_iota(jnp.int32, (CHUNK, CHUNK), 0)
           == lax.broadcasted_iota(jnp.int32, (CHUNK, CHUNK), 1)).astype(F32)
    tinv = [eye + p[h] for h in heads]
    for _ in range(5):
        p = [_dot(p[h], p[h], DN_PRECISION) for h in heads]
        tinv = [tinv[h] + _dot(tinv[h], p[h], DN_PRECISION) for h in heads]
    return tinv


@jax.custom_vjp
def _unit_lower_inverse(p):
    return _neumann_inverse(p)


def _unit_lower_inverse_fwd(p):
    tinv = _neumann_inverse(p)
    return tinv, tinv


def _unit_lower_inverse_bwd(tinv, ct):
    heads = range(len(tinv))
    left = [_dot_tn(tinv[h], ct[h], DN_PRECISION) for h in heads]
    return ([_dot_nt(left[h], tinv[h], DN_PRECISION) for h in heads],)


_unit_lower_inverse.defvjp(_unit_lower_inverse_fwd, _unit_lower_inverse_bwd)


@jax.custom_vjp
def _known_inverse(p, tinv):
    return tinv


_known_inverse.defvjp(lambda p, tinv: (tinv, tinv),
                      lambda tinv, ct: (_unit_lower_inverse_bwd(tinv, ct)[0], [jnp.zeros_like(t) for t in tinv]))


def _dn_chunk(q, k, v, beta, g, state, saved_tinv=None):
    heads = range(len(q))
    ri = lax.broadcasted_iota(jnp.int32, (CHUNK, CHUNK), 0)
    ci = lax.broadcasted_iota(jnp.int32, (CHUNK, CHUNK), 1)
    eye = (ri == ci).astype(F32)
    incl = ri >= ci
    last = lax.broadcasted_iota(jnp.int32, (CHUNK, 1), 0) == CHUNK - 1
    g_row = [jnp.sum(g[h] * eye, axis=0, keepdims=True) for h in heads]
    gc = [jnp.sum(jnp.where(incl, g_row[h], 0.0), axis=1, keepdims=True) for h in heads]
    gc_row = [jnp.sum(gc[h] * eye, axis=0, keepdims=True) for h in heads]
    decay = [jnp.where(incl, jnp.exp(jnp.where(incl, gc[h] - gc_row[h], 0.0)), 0.0) for h in heads]
    kb = [k[h] * beta[h] for h in heads]
    vb = [v[h] * beta[h] for h in heads]
    p = [-jnp.where(ri > ci, _dot_nt(kb[h], k[h]) * decay[h], 0.0) for h in heads]
    tinv = _unit_lower_inverse(p) if saved_tinv is None else _known_inverse(p, saved_tinv)
    eg = [jnp.exp(gc[h]) for h in heads]
    u = [_dot(tinv[h], vb[h]) for h in heads]
    wk = [_dot(tinv[h], kb[h] * eg[h]) for h in heads]
    attn = [_dot_nt(q[h], k[h]) * decay[h] for h in heads]
    g_last = [jnp.sum(jnp.where(last, gc[h], 0.0), axis=0, keepdims=True) for h in heads]
    k_tail = [k[h] * jnp.exp(g_last[h] - gc[h]) for h in heads]
    v_new = [u[h] - _dot(wk[h], state[h]) for h in heads]
    o = [_dot(q[h] * eg[h], state[h]) + _dot(attn[h], v_new[h]) for h in heads]
    new = [state[h] * jnp.exp(g_last[h]) + _dot_tn(k_tail[h], v_new[h]) for h in heads]
    return (o, new, tinv) if saved_tinv is None else (o, new)


def _bg_cols(bg, h, heads):
    lane = lax.broadcasted_iota(jnp.int32, bg.shape, 1)
    beta = jnp.sum(jnp.where(lane == h, bg, 0.0), axis=1, keepdims=True)
    g = jnp.sum(jnp.where(lane == heads + h, bg, 0.0), axis=1, keepdims=True)
    return beta, g


def _dn_fwd(qkv, bg):
    rows = qkv.shape[0]
    heads = qkv.shape[1] // (3 * HEAD_A)
    n = rows // CHUNK
    hp, groups = heads, 1

    def body(q_ref, k_ref, v_ref, bg_ref, o_ref, hist_ref, tinv_ref, s_ref):
        c, grp = pl.program_id(0), pl.program_id(1)

        @pl.when(c == 0)
        def _():
            for i in range(hp):
                s_ref[grp * hp + i] = jnp.zeros((HEAD_A, HEAD_A), F32)

        bg_v = bg_ref[...]
        cols = [slice(i * HEAD_A, (i + 1) * HEAD_A) for i in range(hp)]
        state = [s_ref[grp * hp + i] for i in range(hp)]
        beta_g = [_bg_cols(bg_v, grp * hp + i, heads) for i in range(hp)]
        o, new, tinv = _dn_chunk([q_ref[:, c_] for c_ in cols], [k_ref[:, c_] for c_ in cols],
                                 [v_ref[:, c_] for c_ in cols], [b for b, _ in beta_g], [g for _, g in beta_g], state)
        for i in range(hp):
            hist_ref[0, i] = state[i]
            tinv_ref[0, i] = tinv[i]
            o_ref[:, cols[i]] = o[i]
            s_ref[grp * hp + i] = new[i]

    def part(p):
        return pl.BlockSpec((CHUNK, hp * HEAD_A), lambda c, grp: (c, p * groups + grp))

    return pl.pallas_call(
        body, name="deltanet_fwd", grid=(n, groups),
        in_specs=[part(0), part(1), part(2), pl.BlockSpec((CHUNK, 128), lambda c, grp: (c, 0))],
        out_specs=[part(0), pl.BlockSpec((1, hp, HEAD_A, HEAD_A), lambda c, grp: (c, grp, 0, 0)),
                   pl.BlockSpec((1, hp, CHUNK, CHUNK), lambda c, grp: (c, grp, 0, 0))],
        out_shape=[jax.ShapeDtypeStruct((rows, heads * HEAD_A), F32),
                   jax.ShapeDtypeStruct((n, heads, HEAD_A, HEAD_A), F32),
                   jax.ShapeDtypeStruct((n, heads, CHUNK, CHUNK), F32)],
        scratch_shapes=[pltpu.VMEM((heads, HEAD_A, HEAD_A), F32)],
        compiler_params=_cp("arbitrary", "arbitrary"),
    )(qkv, qkv, qkv, bg)


def _dn_bwd(qkv, bg, hist, tinv_hist, do):
    rows = qkv.shape[0]
    heads = qkv.shape[1] // (3 * HEAD_A)
    n = rows // CHUNK
    hp, groups = heads, 1

    def body(q_ref, k_ref, v_ref, bg_ref, hist_ref, tinv_ref, do_ref, dqkv_ref, dbg_ref, ds_ref):
        c, grp = pl.program_id(0), pl.program_id(1)

        @pl.when(c == 0)
        def _():
            for i in range(hp):
                ds_ref[grp * hp + i] = jnp.zeros((HEAD_A, HEAD_A), F32)

        bg_v = bg_ref[...]
        lane = lax.broadcasted_iota(jnp.int32, (CHUNK, 128), 1)
        cols = [slice(i * HEAD_A, (i + 1) * HEAD_A) for i in range(hp)]
        beta_g = [_bg_cols(bg_v, grp * hp + i, heads) for i in range(hp)]
        _, vjp = jax.vjp(_dn_chunk, [q_ref[:, c_] for c_ in cols], [k_ref[:, c_] for c_ in cols],
                         [v_ref[:, c_] for c_ in cols], [b for b, _ in beta_g], [g for _, g in beta_g],
                         [hist_ref[0, i] for i in range(hp)], [tinv_ref[0, i] for i in range(hp)])
        dq, dk, dv, dbeta, dg, ds, _ = vjp(([do_ref[:, c_] for c_ in cols], [ds_ref[grp * hp + i] for i in range(hp)]))
        dbg = jnp.zeros((CHUNK, 128), F32)
        for i in range(hp):
            h = grp * hp + i
            for p, part_grad in enumerate((dq, dk, dv)):
                dqkv_ref[:, pl.ds((p * heads + i) * HEAD_A, HEAD_A)] = part_grad[i]
            ds_ref[h] = ds[i]
            dbg = dbg + jnp.where(lane == h, dbeta[i], 0.0) + jnp.where(lane == heads + h, dg[i], 0.0)

        @pl.when(grp == 0)
        def _():
            dbg_ref[...] = jnp.zeros_like(dbg_ref)

        dbg_ref[...] += dbg

    def part(p):
        return pl.BlockSpec((CHUNK, hp * HEAD_A), lambda c, grp: (n - 1 - c, p * groups + grp))

    return pl.pallas_call(
        body, name="deltanet_bwd", grid=(n, groups),
        in_specs=[part(0), part(1), part(2), pl.BlockSpec((CHUNK, 128), lambda c, grp: (n - 1 - c, 0)),
                  pl.BlockSpec((1, hp, HEAD_A, HEAD_A), lambda c, grp: (n - 1 - c, grp, 0, 0)),
                  pl.BlockSpec((1, hp, CHUNK, CHUNK), lambda c, grp: (n - 1 - c, grp, 0, 0)), part(0)],
        out_specs=[pl.BlockSpec((CHUNK, 3 * heads * HEAD_A), lambda c, grp: (n - 1 - c, 0)),
                   pl.BlockSpec((CHUNK, 128), lambda c, grp: (n - 1 - c, 0))],
        out_shape=[jax.ShapeDtypeStruct(qkv.shape, F32), jax.ShapeDtypeStruct((rows, 128), F32)],
        scratch_shapes=[pltpu.VMEM((heads, HEAD_A, HEAD_A), F32)],
        compiler_params=_cp("arbitrary", "arbitrary"),
    )(qkv, qkv, qkv, bg, hist, tinv_hist, do)


def _head_mask(heads, width):
    return (lax.broadcasted_iota(jnp.int32, (heads, width), 0)
            == lax.broadcasted_iota(jnp.int32, (heads, width), 1) // HEAD_B)


def _masked_rows(mask, row):
    return jnp.where(mask, row, 0.0).astype(BF16)


def _rwkv_fwd(r, w, k, v, a, b):
    rows, width = r.shape
    heads = width // HEAD_B
    ts = SCAN_STEPS

    def body(r_ref, w_ref, k_ref, v_ref, a_ref, b_ref, y_ref, hist_ref, s_ref):
        @pl.when(pl.program_id(0) == 0)
        def _():
            s_ref[...] = jnp.zeros_like(s_ref)

        mask = _head_mask(heads, width)
        onehot = mask.astype(BF16)
        onehot2 = jnp.concatenate([onehot, onehot], axis=0)
        bd = _block_diag_ones()

        spread_v = [_dot_tn(jnp.concatenate(_hi_lo(v_ref[j]), axis=0), onehot2) for j in range(ts)]
        a_next = pltpu.roll(a_ref[...], ts - 1, 0)
        b_dot_a, k_dot_a = _segsum_many([b_ref[...] * a_next, k_ref[...] * a_next], bd)
        s = s_ref[...]
        ys = []
        for j in range(0, ts, 2):
            row, nxt = pl.ds(j, 1), pl.ds(j + 1, 1)
            hist_ref[j] = s
            sa, base = _segsum_many([((s * a_ref[row, :]).astype(BF16),),
                                     ((s * (w_ref[row, :] * a_ref[nxt, :])).astype(BF16),)], bd)
            sa_next = base + sa * b_dot_a[j:j + 1] + spread_v[j] * k_dot_a[j:j + 1]
            s = s * w_ref[row, :] + sa * b_ref[row, :] + spread_v[j] * k_ref[row, :]
            hist_ref[j + 1] = s
            ys.append(_dot_nt(_masked_rows(mask, r_ref[row, :]), s.astype(BF16)))
            s = s * w_ref[nxt, :] + sa_next * b_ref[nxt, :] + spread_v[j + 1] * k_ref[nxt, :]
            ys.append(_dot_nt(_masked_rows(mask, r_ref[nxt, :]), s.astype(BF16)))
        for j in range(ts):
            y_ref[j] = ys[j]
        s_ref[...] = s

    blk = pl.BlockSpec((ts, width), lambda i: (i, 0))
    blk3 = pl.BlockSpec((ts, heads, HEAD_B), lambda i: (i, 0, 0))
    return pl.pallas_call(
        body, name="rwkv_fwd", grid=(rows // ts,),
        in_specs=[blk, blk, blk, blk3, blk, blk],
        out_specs=[blk3, pl.BlockSpec((ts, HEAD_B, width), lambda i: (i, 0, 0)),
                   pl.BlockSpec((HEAD_B, width), lambda i: (0, 0))],
        out_shape=[jax.ShapeDtypeStruct((rows, heads, HEAD_B), F32), jax.ShapeDtypeStruct((rows, HEAD_B, width), F32),
                   jax.ShapeDtypeStruct((HEAD_B, width), F32)],
        compiler_params=_cp("arbitrary"),
    )(r, w, k, v, a, b)


def _rwkv_bwd(r, w, k, v, a, b, hist, last, dy):
    rows, width = r.shape
    heads = width // HEAD_B
    ts = SCAN_STEPS
    nb = rows // ts

    def body(r_ref, w_ref, k_ref, v_ref, a_ref, b_ref, hist_ref, last_ref, dy_ref,
             dr_ref, dw_ref, dk_ref, dv_ref, da_ref, db_ref, g_ref, after_ref):
        @pl.when(pl.program_id(0) == 0)
        def _():
            g_ref[...] = jnp.zeros_like(g_ref)
            after_ref[...] = last_ref[...]

        mask = _head_mask(heads, width)
        onehot = mask.astype(BF16)
        bd = _block_diag_ones()

        def own_lanes(x):
            return jnp.sum(jnp.where(mask, x, 0.0), axis=0, keepdims=True)

        def colsum(x):
            return jnp.sum(x, axis=0, keepdims=True)

        dy_m = [dy_ref[j].astype(BF16) for j in range(ts)]
        spread_dy = [_dot_tn(dy_m[j], onehot) for j in range(ts)]
        state_after = [hist_ref[j + 1] if j < ts - 1 else after_ref[...] for j in range(ts)]
        dr = [own_lanes(_dot(dy_m[j], state_after[j].astype(BF16))) for j in range(ts)]
        sa_m = [_dot_nt(_masked_rows(mask, a_ref[pl.ds(j, 1), :]), hist_ref[j].astype(BF16)) for j in range(ts)]
        g = g_ref[...]
        dw, dk, db, da, dv = {}, {}, {}, {}, {}
        for j in reversed(range(ts)):
            row = pl.ds(j, 1)
            sp = hist_ref[j]
            g = g + spread_dy[j] * r_ref[row, :]
            (dsa,) = _segsum_many([((g * b_ref[row, :]).astype(BF16),)], bd)
            g_b = g.astype(BF16)
            both = _dot(jnp.concatenate([v_ref[j].astype(BF16), sa_m[j].astype(BF16)], axis=0), g_b)
            dk[j], db[j] = own_lanes(both[:heads]), own_lanes(both[heads:])
            dv[j] = _dot_nt(_masked_rows(mask, k_ref[row, :]), g_b)
            dw[j] = colsum(g * sp)
            da[j] = colsum(sp * dsa)
            g = g * w_ref[row, :] + dsa * a_ref[row, :]
        g_ref[...] = g
        after_ref[...] = hist_ref[0]
        for j in range(ts):
            dv_ref[j] = dv[j]
            for ref, vals in ((dr_ref, dr), (dw_ref, dw), (dk_ref, dk), (da_ref, da), (db_ref, db)):
                ref[pl.ds(j, 1), :] = vals[j]

    blk = pl.BlockSpec((ts, width), lambda i: (nb - 1 - i, 0))
    blk3 = pl.BlockSpec((ts, heads, HEAD_B), lambda i: (nb - 1 - i, 0, 0))
    state = pl.BlockSpec((HEAD_B, width), lambda i: (0, 0))
    return pl.pallas_call(
        body, name="rwkv_bwd", grid=(nb,),
        in_specs=[blk, blk, blk, blk3, blk, blk, pl.BlockSpec((ts, HEAD_B, width), lambda i: (nb - 1 - i, 0, 0)),
                  state, blk3],
        out_specs=[blk, blk, blk, blk3, blk, blk],
        out_shape=[jax.ShapeDtypeStruct((rows, width), F32)] * 3 + [jax.ShapeDtypeStruct((rows, heads, HEAD_B), F32)]
        + [jax.ShapeDtypeStruct((rows, width), F32)] * 2,
        scratch_shapes=[pltpu.VMEM((HEAD_B, width), F32), pltpu.VMEM((HEAD_B, width), F32)],
        compiler_params=_cp("arbitrary"),
    )(r, w, k, v, a, b, hist, last, dy)


def _live(row0, shape):
    return (row0 + lax.broadcasted_iota(jnp.int32, shape, 0)) >= PAD


def _norm_fn(row0, h, gain):
    return (_rms(h, gain),)


def _norm_res_fn(row0, h, gain):
    return _rms(h, gain), h


def _make_bg_fn(heads):
    def fn(row0, x, log_rate, dt_bias):
        lane = lax.broadcasted_iota(jnp.int32, x.shape, 1)
        beta = _sigmoid(x)
        g = -jnp.exp(log_rate) * _softplus(x + dt_bias)
        out = jnp.where(lane < heads, beta, jnp.where(lane < 2 * heads, g, 0.0))
        return (jnp.where(_live(row0, x.shape), out, 0.0),)
    return fn


def _b_pre_fn(row0, zf, w0, w_up, a0, a_up, g_up, k_k, k_a):
    d = w0.shape[1]
    r, k, v = zf[:, :d], zf[:, d:2 * d], zf[:, 2 * d:3 * d]
    lo = zf[:, 3 * d:3 * d + 128]
    lg = zf[:, 3 * d + 128:3 * d + LORA_PAD]
    lane = lax.broadcasted_iota(jnp.int32, lo.shape, 1)
    lw = _dot(jnp.where(lane < LORA_W, jnp.tanh(lo), 0.0), w_up)
    la = _dot(jnp.where(lane >= LORA_W, lo, 0.0), a_up)
    lane_g = lax.broadcasted_iota(jnp.int32, lg.shape, 1)
    gate = _dot(jnp.where(lane_g < LORA_G, _sigmoid(lg), 0.0), g_up)
    decay = jnp.exp(-jnp.exp(-_softplus(-(w0 + lw)) - 0.5))
    a = _sigmoid(a0 + la)
    kx = k * k_k
    kk = kx * lax.rsqrt(_segsum64(kx * kx) + 1e-6)
    k2 = k * (1.0 + (a - 1.0) * k_a)
    return r, decay, k2, v, -kk, kk * a, gate


def _post_fn(row0, o, zg, y, r, k2, v, gate, out_gain, r_k, ln_g, ln_b):
    d = o.shape[1]
    az, ga, gb = zg[:, :d], zg[:, d:2 * d], zg[:, 2 * d:]
    heads = d // HEAD_A
    parts = []
    for h in range(heads):
        oh = o[:, h * HEAD_A:(h + 1) * HEAD_A]
        parts.append(oh * lax.rsqrt(jnp.mean(oh * oh, axis=-1, keepdims=True) + EPS) * out_gain)
    o_a = jnp.concatenate(parts, axis=1) * _silu(az)
    mean = _segsum64(y) * (1.0 / HEAD_B)
    yc = y - mean
    var = _segsum64(yc * yc) * (1.0 / HEAD_B)
    yn = yc * lax.rsqrt(var + GN_EPS) * ln_g + ln_b
    o_b = (yn + _segsum64(r * k2 * r_k) * v) * gate
    return (_sigmoid(ga) * o_a + _sigmoid(gb) * o_b,)


def _loss(h3, target, gain, tb):
    rows, d = h3.shape

    def body(h_ref, t_ref, g_ref, dh_ref, dg_ref, l_ref):
        i = pl.program_id(0)
        live = (i * tb + lax.broadcasted_iota(jnp.int32, (tb, 1), 0)) >= CHUNK
        tgt = t_ref[...]

        def f(h, g):
            err = _rms(h, g) - tgt
            return 0.5 * jnp.sum(jnp.where(live, jnp.mean(err * err, axis=-1, keepdims=True), 0.0))

        val, vjp = jax.vjp(f, h_ref[...], g_ref[...])
        dh, dg = vjp(jnp.ones((), F32))
        dh_ref[...] = dh

        @pl.when(i == 0)
        def _():
            dg_ref[...] = jnp.zeros_like(dg_ref)
            l_ref[...] = jnp.zeros_like(l_ref)

        dg_ref[...] += dg
        l_ref[...] += jnp.full((1, 128), val, F32)

    blk = pl.BlockSpec((tb, d), lambda i: (i, 0))
    return pl.pallas_call(
        body, name="loss", grid=(rows // tb,),
        in_specs=[blk, blk, pl.BlockSpec((1, d), lambda i: (0, 0))],
        out_specs=[blk, pl.BlockSpec((1, d), lambda i: (0, 0)), pl.BlockSpec((1, 128), lambda i: (0, 0))],
        out_shape=[jax.ShapeDtypeStruct((rows, d), F32), jax.ShapeDtypeStruct((1, d), F32),
                   jax.ShapeDtypeStruct((1, 128), F32)],
        compiler_params=_cp("arbitrary"),
    )(h3, target, gain)


def _adamw_math(w, g, m, v):
    m2 = ADAM_B1 * m + (1.0 - ADAM_B1) * g
    v2 = ADAM_B2 * v + (1.0 - ADAM_B2) * (g * g)
    m_hat = m2 / (1.0 - ADAM_B1 ** ADAM_STEP)
    v_hat = v2 / (1.0 - ADAM_B2 ** ADAM_STEP)
    return -ADAM_LR * (m_hat / (jnp.sqrt(v_hat) + ADAM_EPS) + ADAM_WD * w), m2, v2


def _adamw(name, own, landed, w, m, v):
    rows, cols = w.shape
    if rows % 16 == 0:
        rb = _tb(rows, 128)
        grid, blk = (rows // rb,), pl.BlockSpec((rb, cols), lambda i: (i, 0))
        landed_blk = pl.BlockSpec((N_DEV - 1, rb, cols), lambda i: (0, i, 0))
    else:
        grid, blk = (cols // 128,), pl.BlockSpec((rows, 128), lambda i: (0, i))
        landed_blk = pl.BlockSpec((N_DEV - 1, rows, 128), lambda i: (0, 0, i))

    def body(o_ref, s_ref, w_ref, m_ref, v_ref, g_ref, d_ref, m2_ref, v2_ref):
        g = o_ref[...].astype(F32)
        for peer in range(N_DEV - 1):
            g = g + s_ref[peer].astype(F32)
        g_ref[...] = g
        d_ref[...], m2_ref[...], v2_ref[...] = _adamw_math(w_ref[...], g, m_ref[...], v_ref[...])

    return pl.pallas_call(
        body, name=name, grid=grid, in_specs=[blk, landed_blk, blk, blk, blk],
        out_specs=[blk] * 4, out_shape=[jax.ShapeDtypeStruct((rows, cols), F32)] * 4,
        compiler_params=_cp("parallel"),
    )(own, landed, w, m, v)


def _sum_slabs(name, slabs, rb):
    _, rows, cols = slabs.shape

    def body(s_ref, o_ref):
        g = s_ref[0]
        for dev in range(1, N_DEV):
            g = g + s_ref[dev]
        o_ref[...] = g

    return pl.pallas_call(
        body, name=name, grid=(rows // rb,),
        in_specs=[pl.BlockSpec((N_DEV, rb, cols), lambda i: (0, i, 0))],
        out_specs=pl.BlockSpec((rb, cols), lambda i: (i, 0)),
        out_shape=jax.ShapeDtypeStruct((rows, cols), F32), compiler_params=_cp("parallel"),
    )(slabs)


def _adamw_small(w, g, m, v):
    def body(w_ref, g_ref, m_ref, v_ref, d_ref, m2_ref, v2_ref):
        d_ref[...], m2_ref[...], v2_ref[...] = _adamw_math(w_ref[...], g_ref[...], m_ref[...], v_ref[...])

    return pl.pallas_call(body, name="adamw_small", out_shape=[jax.ShapeDtypeStruct(w.shape, F32)] * 3)(w, g, m, v)


def _place():
    return lax.axis_index("x"), lax.axis_index("y"), lax.axis_index("c")


def _index(p):
    return 4 * p[0] + 2 * p[1] + p[2]


def _all_gather(name, xs):
    n = len(xs)

    def body(*refs):
        x_refs, o_refs = refs[:n], refs[n:2 * n]
        send_sems, recv_sems, local_sems = refs[2 * n:]
        x, y, c = _place()
        me, sibling = (x, y, c), (x, y, 1 - c)
        chips = [(1 - x, y), (x, 1 - y), (1 - x, 1 - y)]

        def copy(i, k, block, to, src=None):
            dst = o_refs[i].at[_index(block)]
            return pltpu.make_async_remote_copy(src_ref=dst if src is None else src, dst_ref=dst,
                                                send_sem=send_sems.at[i, k], recv_sem=recv_sems.at[i, k],
                                                device_id=to, device_id_type=MESH_ID)

        mine = [pltpu.make_async_copy(x_refs[i], o_refs[i].at[_index(me)], local_sems.at[i]) for i in range(n)]
        for cp in mine:
            cp.start()
        first = []
        for i in range(n):
            first.append(copy(i, 0, me, sibling, src=x_refs[i]))
            first += [copy(i, 1 + j, me, (*chip, c), src=x_refs[i]) for j, chip in enumerate(chips)]
        for cp in first:
            cp.start()
        passed = []
        for j, chip in enumerate(chips):
            for i in range(n):
                copy(i, 1 + j, (*chip, c), me).wait_recv()
                cp = copy(i, 4 + j, (*chip, c), sibling)
                cp.start()
                passed.append(cp)
        for i in range(n):
            copy(i, 0, sibling, me).wait_recv()
            for j, chip in enumerate(chips):
                copy(i, 4 + j, (*chip, 1 - c), me).wait_recv()
        for cp in first + passed:
            cp.wait_send()
        for cp in mine:
            cp.wait()

    return pl.pallas_call(
        body, name=name, in_specs=[ANY] * n, out_specs=[ANY] * n,
        out_shape=[jax.ShapeDtypeStruct((N_DEV,) + x.shape, x.dtype) for x in xs],
        scratch_shapes=[pltpu.SemaphoreType.DMA((n, 7)), pltpu.SemaphoreType.DMA((n, 7)), pltpu.SemaphoreType.DMA((n,))],
    )(*xs)


def _exchange_start(name, xs, after=None, gather=False):
    n = len(xs)
    copies = n * (N_DEV - 1)
    extra = [] if after is None else [after]

    def body(*refs):
        x_refs, land_refs = refs[:n], refs[n:2 * n]
        sems = refs[2 * n + len(extra):2 * n + len(extra) + 2 * copies]
        token = refs[-1]
        for i, k, peer in _exchange_copies(n):
            _exchange_copy(x_refs, land_refs, sems, i, k, peer, gather).start()
        token[...] = jnp.zeros_like(token)

    lands = [lax.empty((N_DEV,) + x.shape if gather else (N_DEV - 1,) + x.shape[1:], x.dtype) for x in xs]
    out = pl.pallas_call(
        body, name=name,
        out_shape=(*[pltpu.SemaphoreType.DMA(())] * (2 * copies), *[pltpu.HBM(x.shape, x.dtype) for x in xs],
                   *[pltpu.HBM(l.shape, l.dtype) for l in lands], jax.ShapeDtypeStruct((8, 128), F32)),
        in_specs=[HBM_SPEC] * (2 * n) + [ANY] * len(extra),
        out_specs=(*[SEM_SPEC] * (2 * copies), *[HBM_SPEC] * (2 * n), pl.BlockSpec(memory_space=pltpu.VMEM)),
        input_output_aliases={i: 2 * copies + i for i in range(2 * n)},
        compiler_params=pltpu.CompilerParams(has_side_effects=pltpu.SideEffectType.DATAFLOW_SIDE_EFFECTING),
    )(*[pltpu.with_memory_space_constraint(a, pltpu.HBM) for a in list(xs) + lands], *extra)
    sems, rest = list(out[:2 * copies]), out[2 * copies:]
    return sems, list(rest[:n]), list(rest[n:2 * n]), rest[-1]


def _exchange_copies(n):
    x, y, c = _place()
    for k in range(1, N_DEV):
        peer = ((1 - x) if k & 4 else x, (1 - y) if k & 2 else y, (1 - c) if k & 1 else c)
        for i in range(n):
            yield i, k - 1, peer


def _exchange_copy(x_refs, land_refs, sems, i, k, peer, gather, arriving=False):
    copies = len(sems) // 2
    which = i * (N_DEV - 1) + k
    src = x_refs[i] if gather else x_refs[i].at[_index(peer)]
    dst = land_refs[i].at[_index(peer if arriving else _place())] if gather else land_refs[i].at[k]
    return pltpu.make_async_remote_copy(src_ref=src, dst_ref=dst, send_sem=sems[which], recv_sem=sems[copies + which],
                                        device_id=peer, device_id_type=MESH_ID)


def _exchange_wait(name, sems, xs, lands, after, gather=False):
    n = len(xs)

    def body(*refs):
        x_refs, land_refs = refs[:n], refs[n:2 * n]
        sem_refs = refs[2 * n:2 * n + len(sems)]
        for i, k, peer in _exchange_copies(n):
            _exchange_copy(x_refs, land_refs, sem_refs, i, k, peer, gather).wait_send()
            _exchange_copy(x_refs, land_refs, sem_refs, i, k, peer, gather, arriving=True).wait_recv()

    out = pl.pallas_call(
        body, name=name,
        out_shape=(*[pltpu.HBM(x.shape, x.dtype) for x in xs], *[pltpu.HBM(l.shape, l.dtype) for l in lands]),
        in_specs=[HBM_SPEC] * (2 * n) + [SEM_SPEC] * len(sems) + [ANY], out_specs=tuple([HBM_SPEC] * (2 * n)),
        input_output_aliases={i: i for i in range(2 * n)},
        compiler_params=pltpu.CompilerParams(has_side_effects=pltpu.SideEffectType.DATAFLOW_SIDE_EFFECTING),
    )(*xs, *lands, *sems, after)
    return list(out[:n]), list(out[n:])


def _pack(arrays):
    flat = jnp.concatenate([a.reshape(-1) for a in arrays])
    pad = (-flat.shape[0]) % 1024
    return jnp.pad(flat, (0, pad)).reshape(-1, 128)


def _unpack(packed, shapes):
    flat = packed.reshape(-1)
    out, pos = [], 0
    for s in shapes:
        size = 1
        for dim in s:
            size *= dim
        out.append(flat[pos:pos + size].reshape(s))
        pos += size
    return out


def _cols_from_slabs(stack):
    return jnp.transpose(stack, (1, 0, 2)).reshape(stack.shape[1], -1)


def kernel(x, meta_tokens, ffn1_norm, ffn1_w_gu, ffn1_w_down, mix_norm, w_in, a_conv_w, a_log_rate, a_dt_bias, a_out_norm, b_shift_mu, b_w0, b_w_up, b_a0, b_a_up, b_g_up, b_k_k, b_k_a, b_r_k, b_ln_gain, b_ln_bias, w_out, ffn2_norm, ffn2_w_gu, ffn2_w_down, final_norm, loss_target, m_meta_tokens, m_ffn1_norm, m_ffn1_w_gu, m_ffn1_w_down, m_mix_norm, m_w_in, m_a_conv_w, m_a_log_rate, m_a_dt_bias, m_a_out_norm, m_b_shift_mu, m_b_w0, m_b_w_up, m_b_a0, m_b_a_up, m_b_g_up, m_b_k_k, m_b_k_a, m_b_r_k, m_b_ln_gain, m_b_ln_bias, m_w_out, m_ffn2_norm, m_ffn2_w_gu, m_ffn2_w_down, m_final_norm, v_meta_tokens, v_ffn1_norm, v_ffn1_w_gu, v_ffn1_w_down, v_mix_norm, v_w_in, v_a_conv_w, v_a_log_rate, v_a_dt_bias, v_a_out_norm, v_b_shift_mu, v_b_w0, v_b_w_up, v_b_a0, v_b_a_up, v_b_g_up, v_b_k_k, v_b_k_a, v_b_r_k, v_b_ln_gain, v_b_ln_bias, v_w_out, v_ffn2_norm, v_ffn2_w_gu, v_ffn2_w_down, v_final_norm):
    names = ['meta_tokens', 'ffn1_norm', 'ffn1_w_gu', 'ffn1_w_down', 'mix_norm', 'w_in', 'a_conv_w', 'a_log_rate',
             'a_dt_bias', 'a_out_norm', 'b_shift_mu', 'b_w0', 'b_w_up', 'b_a0', 'b_a_up', 'b_g_up', 'b_k_k', 'b_k_a',
             'b_r_k', 'b_ln_gain', 'b_ln_bias', 'w_out', 'ffn2_norm', 'ffn2_w_gu', 'ffn2_w_down', 'final_norm']
    env = dict(locals())
    wts = {k: env[k] for k in names}
    mom_m = {k: env['m_' + k] for k in names}
    mom_v = {k: env['v_' + k] for k in names}
    big = ['ffn1_w_gu', 'ffn1_w_down', 'w_in', 'w_out', 'ffn2_w_gu', 'ffn2_w_down']
    col_sharded = {'ffn1_w_gu', 'w_in', 'ffn2_w_gu'}
    shard_of = lambda tree, k: tree[k][0].T if k in col_sharded else tree[k][0]
    small_sharded = ['meta_tokens', 'a_conv_w', 'b_w_up', 'b_a_up', 'b_g_up']
    replicated = [k for k in names if k not in big and k not in small_sharded]

    seq, d = x.shape[1], x.shape[2]
    rows = PAD + N_META + seq
    heads_a = d // HEAD_A
    tb_mm = _tb(rows, 416)
    tb_vjp = _tb(rows, 208)
    tb_dw = _tb(rows, 2080)
    tb_ffn = _tb(rows, 832)
    me = _index(_place())

    local_bf = {k: shard_of(wts, k).astype(BF16) for k in big}
    gu1, down1, meta = _all_gather("gather_ffn1", [local_bf['ffn1_w_gu'], local_bf['ffn1_w_down'], wts['meta_tokens']])
    small_rest = small_sharded[1:]
    late_keys = ['w_out', 'ffn2_w_gu', 'ffn2_w_down']
    gather_mid = _exchange_start("gather_start_mid", [local_bf['w_in'], _pack([wts[k][0] for k in small_rest])],
                                 after=gu1, gather=True)
    gather_late = _exchange_start("gather_start_late", [local_bf[k] for k in late_keys], after=gather_mid[-1],
                                  gather=True)

    def finish_gather(tag, started, after):
        sems, mine, lands, _ = started
        mine, lands = _exchange_wait("gather_wait_" + tag, sems, mine, lands, after, gather=True)
        return [lax.dynamic_update_index_in_dim(land, own[None], me, 0) for land, own in zip(lands, mine)]

    full = {'ffn1_w_gu': gu1, 'ffn1_w_down': down1.reshape(-1, d), 'meta_tokens': _cols_from_slabs(meta)}
    for k in replicated:
        full[k] = wts[k].reshape(1, -1)

    h0 = jnp.concatenate([jnp.zeros((PAD, d), F32) + gather_late[-1][:1, :1], full['meta_tokens'], x[0]], axis=0)
    h1 = _ffn_fwd("ffn1_fwd", h0, full['ffn1_norm'], full['ffn1_w_gu'], full['ffn1_w_down'], tb_ffn)
    (u,) = _tok_fwd("mix_norm_fwd", _norm_fn, [h1], [full['mix_norm']], [(d, BF16)], tb_mm)

    win_stack, small_stack = finish_gather("mid", gather_mid, u)
    full['w_in'] = win_stack.reshape(-1, d)
    small_flat, pos = small_stack.reshape(N_DEV, -1), 0
    for k in small_rest:
        shape = wts[k][0].shape
        full[k] = _cols_from_slabs(small_flat[:, pos:pos + shape[0] * shape[1]].reshape((N_DEV,) + shape))
        pos += shape[0] * shape[1]

    win = full['w_in']
    n_b = 3 * d + LORA_W + LORA_A + LORA_G
    off_beta, off_b = 4 * d, 4 * d + 2 * heads_a
    off_ga = off_b + n_b
    b_width = 3 * d + LORA_PAD
    zrows = lambda r: jnp.zeros((r, d), BF16)
    w_qkv = win[:3 * d]
    w_zg = jnp.concatenate([win[3 * d:4 * d], win[off_ga:off_ga + 2 * d]], axis=0)
    w_b = jnp.concatenate([win[off_b:off_b + n_b], zrows(b_width - n_b)], axis=0)
    w_bg = jnp.concatenate([win[off_beta:off_beta + 2 * heads_a], zrows(128 - 2 * heads_a)], axis=0)

    def lanes(vec, start, width):
        return jnp.pad(vec.reshape(1, -1), ((0, 0), (start, width - start - vec.size)))

    log_rate = lanes(wts['a_log_rate'], heads_a, 128)
    dt_bias = lanes(wts['a_dt_bias'], heads_a, 128)
    mu = lanes(wts['b_shift_mu'], 0, b_width)
    w_up = jnp.pad(full['b_w_up'], ((0, 128 - LORA_W), (0, 0)))
    a_up = jnp.pad(full['b_a_up'], ((LORA_W, 0), (0, 0)))
    g_up = jnp.pad(full['b_g_up'], ((0, 256 - LORA_G), (0, 0)))
    b_pars = [full['b_w0'], w_up, full['b_a0'], a_up, g_up, full['b_k_k'], full['b_k_a']]
    post_pars = [full['a_out_norm'], full['b_r_k'], full['b_ln_gain'], full['b_ln_bias']]
    bg_fn = _make_bg_fn(heads_a)

    z_qkv = _mm("in_qkv", u, w_qkv, trans_b=True, tb=tb_mm, tn=_col_tile(3 * d, 1536))
    z_zg = _mm("in_zg", u, w_zg, trans_b=True, tb=tb_mm, tn=_col_tile(3 * d, 1536))
    z_b = _mm("in_b", u, w_b, trans_b=True, tb=tb_mm, tn=_col_tile(b_width, 1536))
    z_bg = _mm("in_bg", u, w_bg, trans_b=True, tb=tb_mm, tn=128)
    qkv = _a_pre_fwd(z_qkv, full['a_conv_w'])
    (bg,) = _tok_fwd("bg_fwd", bg_fn, [z_bg], [log_rate, dt_bias], [(128, F32)], tb_mm)
    o_dn, dn_hist, dn_tinv = _dn_fwd(qkv, bg)
    zf = _shift_fwd(z_b, mu)
    rr, ww, kk2, vv, av, bv, gate = _tok_fwd("b_pre_fwd", _b_pre_fn, [zf], b_pars, [(d, F32)] * 7, tb_vjp)
    per_head = lambda t: t.reshape(rows, d // HEAD_B, HEAD_B)
    y_heads, b_hist, b_last = _rwkv_fwd(rr, ww, kk2, per_head(vv), av, bv)
    y_b = y_heads.reshape(rows, d)
    w_out_stack, full['ffn2_w_gu'], down2 = finish_gather("late", gather_late, y_heads)
    full['w_out'], full['ffn2_w_down'] = w_out_stack.reshape(-1, d), down2.reshape(-1, d)
    post_toks = [o_dn, z_zg, y_b, rr, kk2, vv, gate]
    (merged,) = _tok_fwd("post_fwd", _post_fn, post_toks, post_pars, [(d, BF16)], tb_vjp)
    h2 = _mm("out_proj", merged, full['w_out'], add=h1, tb=tb_mm, tn=d)
    h3 = _ffn_fwd("ffn2_fwd", h2, full['ffn2_norm'], full['ffn2_w_gu'], full['ffn2_w_down'], tb_ffn)

    target = jnp.pad(loss_target[0], ((CHUNK, 0), (0, 0)))
    dh3, g_final, loss_part = _loss(h3, target, full['final_norm'].reshape(1, d), tb_vjp)

    def ffn_backward(tag, h, dout, key_norm, key_gu, key_down):
        dh, dh_bf, dgain, xn, act, dgate, dup, dhalf = _ffn_bwd(tag + "_bwd", h, full[key_norm], dout, full[key_gu],
                                                                full[key_down], tb_mm)
        fc = dgate.shape[2]
        d_gu = jnp.concatenate([_mm_tn_from_slabs(tag + "_dw_gate", dgate, xn, tk=tb_dw).reshape(-1, fc, d),
                                _mm_tn_from_slabs(tag + "_dw_up", dup, xn, tk=tb_dw).reshape(-1, fc, d)], axis=0)
        d_down = _mm_tn_from_slabs(tag + "_dw_down", act, dhalf, tk=tb_dw).reshape(N_DEV, -1, d)
        return dh, dh_bf, dgain, d_gu, d_down

    dh2, dh2_bf, g_ffn2_norm, g_ffn2_gu, g_ffn2_down = ffn_backward("ffn2", h2, dh3, 'ffn2_norm', 'ffn2_w_gu',
                                                                    'ffn2_w_down')
    g_w_out = _mm_tn("dw_out", merged, dh2_bf, tm=d, tn=d, tk=tb_dw).reshape(N_DEV, -1, d)

    def start_exchange(tag, keys, slabs, after=None):
        sems, kept, lands, token = _exchange_start("exchange_start_" + tag, slabs, after)
        return (tag, keys, sems, kept, lands), token

    ex_ffn2, token_ffn2 = start_exchange("ffn2", ['ffn2_w_gu', 'ffn2_w_down', 'w_out'], [g_ffn2_gu, g_ffn2_down, g_w_out])
    dmerged = _mm("d_merged", dh2_bf, full['w_out'], trans_b=True, after=token_ffn2, tb=tb_mm, tn=d)
    post_grads = _tok_bwd("post_bwd", _post_fn, post_toks, post_pars, [[dmerged]], list(range(7)), tb_vjp,
                          [F32, BF16] + [F32] * 5)
    do_dn, dz_zg, dy_b, dr1, dk1, dv1, dgate = post_grads[:7]
    g_out_norm, g_r_k, g_ln_g, g_ln_b = post_grads[7:]
    dr2, dw2, dk2, dv_heads, da2, db2 = _rwkv_bwd(rr, ww, kk2, per_head(vv), av, bv, b_hist, b_last, per_head(dy_b))
    dv2 = dv_heads.reshape(rows, d)
    b_grads = _tok_bwd("b_pre_bwd", _b_pre_fn, [zf], b_pars,
                       [[dr1, dr2], [dw2], [dk1, dk2], [dv1, dv2], [da2], [db2], [dgate]], [0], tb_vjp)
    dzf = b_grads[0]
    g_w0, g_w_up, g_a0, g_a_up, g_g_up, g_k_k, g_k_a = b_grads[1:]
    dz_b, g_mu = _shift_bwd(z_b, mu, dzf)
    dqkv, dbg = _dn_bwd(qkv, bg, dn_hist, dn_tinv, do_dn)
    dz_qkv, g_conv = _a_pre_bwd(z_qkv, full['a_conv_w'], dqkv)
    dz_bg, g_log_rate, g_dt_bias = _tok_bwd("bg_bwd", bg_fn, [z_bg], [log_rate, dt_bias], [[dbg]], [0], tb_mm, [BF16])

    du = None
    g_w_in_parts = []
    for tag, dz, wpiece in (("qkv", dz_qkv, w_qkv), ("zg", dz_zg, w_zg), ("b", dz_b, w_b), ("bg", dz_bg, w_bg)):
        du = _mm("du_" + tag, dz, wpiece, add=du, tb=tb_mm, tn=d)
        g_w_in_parts.append(_mm_tn("dw_in_" + tag, dz, u, tm=_col_tile(dz.shape[1], 1536), tn=d, tk=tb_dw))
    gp_qkv, gp_zg, gp_b, gp_bg = g_w_in_parts
    g_w_in = jnp.concatenate([gp_qkv, gp_zg[:d], gp_bg[:2 * heads_a], gp_b[:n_b], gp_zg[d:]],
                             axis=0).reshape(N_DEV, -1, d)
    ex_w_in, token_w_in = start_exchange("w_in", ['w_in'], [g_w_in])
    dh1, g_mix_norm = _tok_bwd("mix_norm_bwd", _norm_res_fn, [h1], [full['mix_norm']], [[du], [dh2]], [0], tb_vjp,
                               after=token_w_in)
    dh0, _, g_ffn1_norm, g_ffn1_gu, g_ffn1_down = ffn_backward("ffn1", h0, dh1, 'ffn1_norm', 'ffn1_w_gu', 'ffn1_w_down')

    small_full = {
        'meta_tokens': dh0[PAD:CHUNK], 'ffn1_norm': g_ffn1_norm, 'mix_norm': g_mix_norm, 'a_conv_w': g_conv,
        'a_log_rate': g_log_rate[:, heads_a:2 * heads_a], 'a_dt_bias': g_dt_bias[:, heads_a:2 * heads_a],
        'a_out_norm': g_out_norm, 'b_shift_mu': g_mu[:, :n_b], 'b_w0': g_w0, 'b_w_up': g_w_up[:LORA_W],
        'b_a0': g_a0, 'b_a_up': g_a_up[LORA_W:], 'b_g_up': g_g_up[:LORA_G], 'b_k_k': g_k_k, 'b_k_a': g_k_a,
        'b_r_k': g_r_k, 'b_ln_gain': g_ln_g, 'b_ln_bias': g_ln_b, 'ffn2_norm': g_ffn2_norm, 'final_norm': g_final,
    }
    small_names = list(small_full)
    packed = _pack([small_full[k] for k in small_names] + [loss_part[:, :1]])
    (all_parts,) = _all_gather("gather_small_grads", [packed])
    ex_ffn1, _ = start_exchange("ffn1", ['ffn1_w_gu', 'ffn1_w_down'], [g_ffn1_gu, g_ffn1_down], after=all_parts)
    summed = _sum_slabs("sum_small_grads", all_parts, packed.shape[0])
    pieces = _unpack(summed, [small_full[k].shape for k in small_names] + [(1, 1)])
    small_grad = dict(zip(small_names, pieces[:-1]))
    loss = pieces[-1].reshape(())

    grads, deltas, new_m, new_v = {}, {}, {}, {}
    local_small = {}
    for k in small_names:
        g = small_grad[k]
        if k in small_sharded:
            width = wts[k].shape[-1]
            g = lax.dynamic_slice_in_dim(g, me * width, width, axis=1)
        local_small[k] = g.reshape(wts[k].shape)
    pk = lambda tree: _pack([tree[k] for k in small_names])
    dl_s, m_s, v_s = _adamw_small(pk(wts), pk(local_small), pk(mom_m), pk(mom_v))
    shapes = [wts[k].shape for k in small_names]
    for k, dl, m2, v2 in zip(small_names, _unpack(dl_s, shapes), _unpack(m_s, shapes), _unpack(v_s, shapes)):
        grads[k], deltas[k], new_m[k], new_v[k] = local_small[k], dl, m2, v2

    done = dl_s
    for tag, keys, sems, kept, lands in (ex_ffn2, ex_w_in, ex_ffn1):
        kept, lands = _exchange_wait("exchange_wait_" + tag, sems, kept, lands, done)
        for k, slabs, landed in zip(keys, kept, lands):
            own = lax.dynamic_index_in_dim(slabs, me, axis=0, keepdims=False)
            res = _adamw("adamw_" + k, own, landed, shard_of(wts, k), shard_of(mom_m, k), shard_of(mom_v, k))
            done = res[1]
            res = [(t.T if k in col_sharded else t)[None] for t in res]
            grads[k], deltas[k], new_m[k], new_v[k] = res

    grad_x = dh0[CHUNK:][None]
    return (loss, grad_x, *[grads[k] for k in names], *[deltas[k] for k in names],
            *[new_m[k] for k in names], *[new_v[k] for k in names])
```

```python
import functools

import jax
import jax.numpy as jnp
from jax import lax
from jax.experimental import pallas as pl
from jax.experimental.pallas import tpu as pltpu

F32 = jnp.float32
BF16 = jnp.bfloat16
N_DEV = 8
N_META = 16
CHUNK = 64
PAD = CHUNK - N_META
HEAD_A = 128
HEAD_B = 64
LORA_W, LORA_A, LORA_G = 64, 64, 160
LORA_PAD = 384
EPS = 1e-6
GN_EPS = HEAD_B * 1e-5
ADAM_LR, ADAM_B1, ADAM_B2, ADAM_EPS, ADAM_WD, ADAM_STEP = 0.001, 0.9, 0.999, 1e-08, 0.01, 10
SCAN_STEPS = 32
MXU_WIDTH = 256
VMEM_LIMIT = 56 * 1024 * 1024
DN_PRECISION = lax.Precision.HIGH
MESH_ID = pl.DeviceIdType.MESH
ANY = pl.BlockSpec(memory_space=pl.ANY)
HBM_SPEC = pl.BlockSpec(memory_space=pltpu.HBM)
SEM_SPEC = pl.BlockSpec(memory_space=pltpu.SEMAPHORE)


def _cp(*sem):
    return pltpu.CompilerParams(dimension_semantics=sem, vmem_limit_bytes=VMEM_LIMIT)


def _tb(t, target):
    best = 16
    for d in range(16, target + 1, 16):
        if t % d == 0:
            best = d
    return best


def _sigmoid(x):
    return 1.0 / (1.0 + jnp.exp(-x))


def _silu(x):
    return x * _sigmoid(x)


def _softplus(x):
    return jnp.maximum(x, 0.0) + jnp.log(1.0 + jnp.exp(-jnp.abs(x)))


def _dot_nt(a, b, precision=None):
    return lax.dot_general(a, b, (((1,), (1,)), ((), ())), preferred_element_type=F32, precision=precision)


def _dot_tn(a, b, precision=None):
    return lax.dot_general(a, b, (((0,), (0,)), ((), ())), preferred_element_type=F32, precision=precision)


def _dot(a, b, precision=None):
    return jnp.dot(a, b, preferred_element_type=F32, precision=precision)


def _block_diag_ones():
    i = lax.broadcasted_iota(jnp.int32, (MXU_WIDTH, MXU_WIDTH), 0) // HEAD_B
    j = lax.broadcasted_iota(jnp.int32, (MXU_WIDTH, MXU_WIDTH), 1) // HEAD_B
    return (i == j).astype(BF16)


def _hi_lo(x):
    hi = x.astype(BF16)
    return hi, (x - hi.astype(F32)).astype(BF16)


def _segsum_many(xs, bd):
    groups = [x if isinstance(x, tuple) else _hi_lo(x) for x in xs]
    rows = groups[0][0].shape[0]
    stacked = jnp.concatenate([p for grp in groups for p in grp], axis=0)
    out = jnp.concatenate([_dot(stacked[:, s:s + MXU_WIDTH], bd) for s in range(0, stacked.shape[1], MXU_WIDTH)], axis=1)
    res, pos = [], 0
    for grp in groups:
        acc = out[pos:pos + rows]
        for j in range(1, len(grp)):
            acc = acc + out[pos + j * rows:pos + (j + 1) * rows]
        res.append(acc)
        pos += len(grp) * rows
    return res


def _segsum_impl(x):
    return _segsum_many([x], _block_diag_ones())[0]


@jax.custom_vjp
def _segsum64(x):
    return _segsum_impl(x)


_segsum64.defvjp(lambda x: (_segsum_impl(x), None), lambda _, ct: (_segsum_impl(ct),))


def _tok(t):
    return t if isinstance(t, tuple) else (t, t.shape[1], 0)


def _tok_spec(tb, width, colblk):
    return pl.BlockSpec((tb, width), lambda i: (i, colblk))


def _par_spec(p):
    return pl.BlockSpec(p.shape, lambda i: (0, 0))


def _tok_fwd(name, fn, toks, pars, outs, tb):
    toks = [_tok(t) for t in toks]
    rows = toks[0][0].shape[0]
    n_in = len(toks) + len(pars)

    def body(*refs):
        row0 = pl.program_id(0) * tb
        res = fn(row0, *[r[...] for r in refs[:n_in]])
        for r, o in zip(refs[n_in:], res):
            r[...] = o.astype(r.dtype)

    return pl.pallas_call(
        body, name=name, grid=(rows // tb,),
        in_specs=[_tok_spec(tb, w, c) for _, w, c in toks] + [_par_spec(p) for p in pars],
        out_specs=[_tok_spec(tb, w, 0) for w, _ in outs],
        out_shape=[jax.ShapeDtypeStruct((rows, w), dt) for w, dt in outs],
        compiler_params=_cp("parallel"),
    )(*[a for a, _, _ in toks], *pars)


def _tok_bwd(name, fn, toks, pars, cts, want, tb, want_dtypes=None, after=None):
    toks = [_tok(t) for t in toks]
    want_dtypes = want_dtypes or [F32] * len(want)
    cts = [[_tok(c) for c in group] for group in cts]
    flat_cts = [c for group in cts for c in group]
    rows = toks[0][0].shape[0]
    n_tok, n_par, n_ct = len(toks), len(pars), len(flat_cts)
    extra = [] if after is None else [after]

    def body(*refs):
        i = pl.program_id(0)
        row0 = i * tb
        prim = [r[...].astype(F32) for r in refs[:n_tok + n_par]]
        ct_refs = list(refs[n_tok + n_par:n_tok + n_par + n_ct])
        out_refs = refs[n_tok + n_par + n_ct + len(extra):]
        res, vjp = jax.vjp(lambda *a: fn(row0, *a), *prim)
        ct = []
        for group, o in zip(cts, res):
            acc = None
            for _ in group:
                v = ct_refs.pop(0)[...].astype(F32)
                acc = v if acc is None else acc + v
            ct.append(acc.astype(o.dtype))
        grads = vjp(tuple(ct))
        for r, k in zip(out_refs[:len(want)], want):
            r[...] = grads[k].astype(r.dtype)

        @pl.when(i == 0)
        def _():
            for r in out_refs[len(want):]:
                r[...] = jnp.zeros_like(r)

        for r, g in zip(out_refs[len(want):], grads[n_tok:]):
            r[...] += g

    return pl.pallas_call(
        body, name=name, grid=(rows // tb,),
        in_specs=[_tok_spec(tb, w, c) for _, w, c in toks] + [_par_spec(p) for p in pars]
        + [_tok_spec(tb, w, c) for _, w, c in flat_cts] + [ANY] * len(extra),
        out_specs=[_tok_spec(tb, toks[k][1], 0) for k in want] + [_par_spec(p) for p in pars],
        out_shape=[jax.ShapeDtypeStruct((rows, toks[k][1]), dt) for k, dt in zip(want, want_dtypes)]
        + [jax.ShapeDtypeStruct(p.shape, F32) for p in pars],
        compiler_params=_cp("arbitrary"),
    )(*[a for a, _, _ in toks], *pars, *[a for a, _, _ in flat_cts], *extra)


def _mm(name, a, b, *, trans_b=False, add=None, after=None, tb, tn):
    rows, k = a.shape
    n = b.shape[0] if trans_b else b.shape[1]

    def body(*refs):
        a_ref, b_ref = refs[:2]
        o_ref = refs[-1]
        acc = _dot_nt(a_ref[...], b_ref[...]) if trans_b else _dot(a_ref[...], b_ref[...])
        if add is not None:
            acc = acc + refs[2][...]
        o_ref[...] = acc

    in_specs = [pl.BlockSpec((tb, k), lambda i, j: (i, 0)),
                pl.BlockSpec((tn, k), lambda i, j: (j, 0)) if trans_b else pl.BlockSpec((k, tn), lambda i, j: (0, j))]
    args = [a, b]
    if add is not None:
        in_specs.append(pl.BlockSpec((tb, tn), lambda i, j: (i, j)))
        args.append(add)
    if after is not None:
        in_specs.append(ANY)
        args.append(after)
    return pl.pallas_call(
        body, name=name, grid=(rows // tb, n // tn), in_specs=in_specs,
        out_specs=pl.BlockSpec((tb, tn), lambda i, j: (i, j)),
        out_shape=jax.ShapeDtypeStruct((rows, n), F32),
        compiler_params=_cp("parallel", "parallel"),
    )(*args)


def _mm_tn_call(name, grid, a, b, a_spec, b_spec, o_spec, acc_shape, out_shape):
    last = len(grid) - 1

    def body(a_ref, b_ref, o_ref, acc_ref):
        k = pl.program_id(last)

        @pl.when(k == 0)
        def _():
            acc_ref[...] = jnp.zeros_like(acc_ref)

        a_blk = a_ref[0] if len(a_ref.shape) == 3 else a_ref[...]
        b_blk = b_ref[0] if len(b_ref.shape) == 3 else b_ref[...]
        acc_ref[...] += _dot_tn(a_blk, b_blk)

        @pl.when(k == grid[last] - 1)
        def _():
            if len(o_ref.shape) == 3:
                o_ref[0] = acc_ref[...].astype(o_ref.dtype)
            else:
                o_ref[...] = acc_ref[...].astype(o_ref.dtype)

    return pl.pallas_call(
        body, name=name, grid=grid, in_specs=[a_spec, b_spec], out_specs=o_spec,
        out_shape=jax.ShapeDtypeStruct(out_shape, BF16), scratch_shapes=[pltpu.VMEM(acc_shape, F32)],
        compiler_params=_cp(*(["parallel"] * last + ["arbitrary"])),
    )(a, b)


def _mm_tn(name, a, b, *, tm, tn, tk):
    rows, m = a.shape
    n = b.shape[1]
    return _mm_tn_call(name, (m // tm, n // tn, rows // tk), a, b,
                       pl.BlockSpec((tk, tm), lambda i, j, k: (k, i)), pl.BlockSpec((tk, tn), lambda i, j, k: (k, j)),
                       pl.BlockSpec((tm, tn), lambda i, j, k: (i, j)), (tm, tn), (m, n))


def _mm_tn_from_slabs(name, a3, b, *, tk):
    s, rows, c = a3.shape
    n = b.shape[1]
    return _mm_tn_call(name, (s, rows // tk), a3, b,
                       pl.BlockSpec((1, tk, c), lambda i, k: (i, k, 0)), pl.BlockSpec((tk, n), lambda i, k: (k, 0)),
                       pl.BlockSpec((c, n), lambda i, k: (i, 0)), (c, n), (s * c, n))


def _col_tile(n, target):
    if n <= target:
        return n
    best = 128
    for d in range(128, target + 1, 128):
        if n % d == 0:
            best = d
    return best


def _rms(x, gain):
    return x * lax.rsqrt(jnp.mean(x * x, axis=-1, keepdims=True) + EPS) * gain


def _ffn_specs(d, fc, nj):
    return [pl.BlockSpec((1, fc, d), lambda i, j: (j, 0, 0)), pl.BlockSpec((1, fc, d), lambda i, j: (nj + j, 0, 0)),
            pl.BlockSpec((fc, d), lambda i, j: (j, 0))]


def _ffn_fwd(name, h, gain, wgu, wd, tb):
    rows, d = h.shape
    nj = wgu.shape[0] // 2
    fc = wgu.shape[1]

    def body(h_ref, g_ref, wg_ref, wu_ref, wd_ref, o_ref, xn_s, acc_s):
        j = pl.program_id(1)

        @pl.when(j == 0)
        def _():
            xn_s[...] = _rms(h_ref[...], g_ref[...]).astype(BF16)
            acc_s[...] = jnp.zeros_like(acc_s)

        wg, wu, wdn = wg_ref[0], wu_ref[0], wd_ref[...]
        for half in range(2):
            rs = pl.ds(half * (tb // 2), tb // 2)
            xn = xn_s[rs, :]
            gate = _dot_nt(xn, wg)
            up = _dot_nt(xn, wu)
            acc_s[rs, :] += _dot((_silu(gate) * up).astype(BF16), wdn)

        @pl.when(j == nj - 1)
        def _():
            o_ref[...] = h_ref[...] + 0.5 * acc_s[...]

    return pl.pallas_call(
        body, name=name, grid=(rows // tb, nj),
        in_specs=[pl.BlockSpec((tb, d), lambda i, j: (i, 0)), pl.BlockSpec((1, d), lambda i, j: (0, 0))]
        + _ffn_specs(d, fc, nj),
        out_specs=pl.BlockSpec((tb, d), lambda i, j: (i, 0)),
        out_shape=jax.ShapeDtypeStruct((rows, d), F32),
        scratch_shapes=[pltpu.VMEM((tb, d), BF16), pltpu.VMEM((tb, d), F32)],
        compiler_params=_cp("parallel", "arbitrary"),
    )(h, gain, wgu, wgu, wd)


def _ffn_bwd(name, h, gain, dout, wgu, wd, tb):
    rows, d = h.shape
    nj = wgu.shape[0] // 2
    fc = wgu.shape[1]

    def body(h_ref, g_ref, do_ref, wg_ref, wu_ref, wd_ref,
             dh_ref, dhb_ref, dg_ref, xn_ref, act_ref, dgate_ref, dup_ref, dhalf_ref, dxn_s):
        i, j = pl.program_id(0), pl.program_id(1)

        @pl.when(j == 0)
        def _():
            xn_ref[...] = _rms(h_ref[...], g_ref[...]).astype(BF16)
            dhalf_ref[...] = (0.5 * do_ref[...]).astype(BF16)
            dxn_s[...] = jnp.zeros_like(dxn_s)

        wg, wu, wdn = wg_ref[0], wu_ref[0], wd_ref[...]
        for half in range(2):
            rs = pl.ds(half * (tb // 2), tb // 2)
            xn = xn_ref[rs, :]
            gate = _dot_nt(xn, wg)
            up = _dot_nt(xn, wu)
            sg = _sigmoid(gate)
            dact = _dot_nt(dhalf_ref[rs, :], wdn)
            act_ref[0, rs, :] = (gate * sg * up).astype(BF16)
            dgate = (dact * up * (sg * (1.0 + gate * (1.0 - sg)))).astype(BF16)
            dup = (dact * gate * sg).astype(BF16)
            dgate_ref[0, rs, :] = dgate
            dup_ref[0, rs, :] = dup
            dxn_s[rs, :] += _dot(dgate, wg) + _dot(dup, wu)

        @pl.when((i == 0) & (j == 0))
        def _():
            dg_ref[...] = jnp.zeros_like(dg_ref)

        @pl.when(j == nj - 1)
        def _():
            x = h_ref[...]
            r = lax.rsqrt(jnp.mean(x * x, axis=-1, keepdims=True) + EPS)
            dxn = dxn_s[...]
            dyg = dxn * g_ref[...]
            dh = do_ref[...] + r * dyg - x * (r * r * r) * jnp.mean(dyg * x, axis=-1, keepdims=True)
            dh_ref[...] = dh
            dhb_ref[...] = dh.astype(BF16)
            dg_ref[...] += jnp.sum(dxn * x * r, axis=0, keepdims=True)

    row_d = pl.BlockSpec((tb, d), lambda i, j: (i, 0))
    slab = pl.BlockSpec((1, tb, fc), lambda i, j: (j, i, 0))
    hidden = jax.ShapeDtypeStruct((nj, rows, fc), BF16)
    return pl.pallas_call(
        body, name=name, grid=(rows // tb, nj),
        in_specs=[row_d, pl.BlockSpec((1, d), lambda i, j: (0, 0)), row_d] + _ffn_specs(d, fc, nj),
        out_specs=[row_d, row_d, pl.BlockSpec((1, d), lambda i, j: (0, 0)), row_d, slab, slab, slab, row_d],
        out_shape=[jax.ShapeDtypeStruct((rows, d), F32), jax.ShapeDtypeStruct((rows, d), BF16),
                   jax.ShapeDtypeStruct((1, d), F32), jax.ShapeDtypeStruct((rows, d), BF16),
                   hidden, hidden, hidden, jax.ShapeDtypeStruct((rows, d), BF16)],
        scratch_shapes=[pltpu.VMEM((tb, d), F32)],
        compiler_params=_cp("arbitrary", "arbitrary"),
    )(h, gain, dout, wgu, wgu, wd)


def _shift_rows(x, s):
    return pltpu.roll(x, s % x.shape[0], 0)


def _a_post(c, which):
    s = _silu(c)
    n = s * lax.rsqrt(jnp.sum(s * s, axis=-1, keepdims=True) + 1e-6)
    scale = jnp.where(which == 0, HEAD_A ** -0.5, 1.0)
    return jnp.where(which == 2, s, n * scale)


def _conv(x, w):
    return x * w[3:4] + _shift_rows(x, 1) * w[2:3] + _shift_rows(x, 2) * w[1:2] + _shift_rows(x, 3) * w[0:1]


def _a_pre_fwd(zqkv, conv_w):
    rows, width = zqkv.shape
    heads = width // (3 * HEAD_A)

    def body(x_ref, w_ref, o_ref):
        which = pl.program_id(0) // heads
        live = lax.broadcasted_iota(jnp.int32, (rows, HEAD_A), 0) >= PAD
        o_ref[...] = jnp.where(live, _a_post(_conv(x_ref[...], w_ref[...]), which), 0.0)

    return pl.pallas_call(
        body, name="a_pre_fwd", grid=(width // HEAD_A,),
        in_specs=[pl.BlockSpec((rows, HEAD_A), lambda c: (0, c)), pl.BlockSpec((4, HEAD_A), lambda c: (0, c))],
        out_specs=pl.BlockSpec((rows, HEAD_A), lambda c: (0, c)),
        out_shape=jax.ShapeDtypeStruct((rows, width), F32),
        compiler_params=_cp("parallel"),
    )(zqkv, conv_w)


def _a_pre_bwd(zqkv, conv_w, dqkv):
    rows, width = zqkv.shape
    heads = width // (3 * HEAD_A)

    def body(x_ref, w_ref, ct_ref, dx_ref, dw_ref):
        which = pl.program_id(0) // heads
        live = lax.broadcasted_iota(jnp.int32, (rows, HEAD_A), 0) >= PAD
        x, w = x_ref[...], w_ref[...]
        _, vjp = jax.vjp(lambda c: _a_post(c, which), _conv(x, w))
        (dc,) = vjp(jnp.where(live, ct_ref[...], 0.0))
        dc = jnp.where(live, dc, 0.0)
        dx_ref[...] = (dc * w[3:4] + _shift_rows(dc, -1) * w[2:3] + _shift_rows(dc, -2) * w[1:2]
                       + _shift_rows(dc, -3) * w[0:1]).astype(BF16)
        dw_ref[...] = jnp.concatenate(
            [jnp.sum(dc * (_shift_rows(x, 3 - j) if j < 3 else x), axis=0, keepdims=True) for j in range(4)], axis=0)

    col = pl.BlockSpec((rows, HEAD_A), lambda c: (0, c))
    wsp = pl.BlockSpec((4, HEAD_A), lambda c: (0, c))
    return pl.pallas_call(
        body, name="a_pre_bwd", grid=(width // HEAD_A,),
        in_specs=[col, wsp, col], out_specs=[col, wsp],
        out_shape=[jax.ShapeDtypeStruct((rows, width), BF16), jax.ShapeDtypeStruct((4, width), F32)],
        compiler_params=_cp("parallel"),
    )(zqkv, conv_w, dqkv)


SHIFT_TILE = 384


def _shift_fwd(zb, mu):
    rows, width = zb.shape

    def body(x_ref, mu_ref, o_ref):
        x = x_ref[...]
        first = lax.broadcasted_iota(jnp.int32, x.shape, 0) == 0
        prev = jnp.where(first, 0.0, _shift_rows(x, 1))
        o_ref[...] = x + (prev - x) * mu_ref[...]

    col = pl.BlockSpec((rows, SHIFT_TILE), lambda c: (0, c))
    return pl.pallas_call(
        body, name="shift_fwd", grid=(width // SHIFT_TILE,),
        in_specs=[col, pl.BlockSpec((1, SHIFT_TILE), lambda c: (0, c))], out_specs=col,
        out_shape=jax.ShapeDtypeStruct((rows, width), F32), compiler_params=_cp("parallel"),
    )(zb, mu)


def _shift_bwd(zb, mu, dzf):
    rows, width = zb.shape

    def body(x_ref, mu_ref, ct_ref, dx_ref, dmu_ref):
        x, ct, mu_v = x_ref[...], ct_ref[...], mu_ref[...]
        row = lax.broadcasted_iota(jnp.int32, x.shape, 0)
        prev = jnp.where(row == 0, 0.0, _shift_rows(x, 1))
        nxt = jnp.where(row == rows - 1, 0.0, _shift_rows(ct, -1))
        dx_ref[...] = (ct * (1.0 - mu_v) + nxt * mu_v).astype(BF16)
        dmu_ref[...] = jnp.sum(ct * (prev - x), axis=0, keepdims=True)

    col = pl.BlockSpec((rows, SHIFT_TILE), lambda c: (0, c))
    msp = pl.BlockSpec((1, SHIFT_TILE), lambda c: (0, c))
    return pl.pallas_call(
        body, name="shift_bwd", grid=(width // SHIFT_TILE,),
        in_specs=[col, msp, col], out_specs=[col, msp],
        out_shape=[jax.ShapeDtypeStruct((rows, width), BF16), jax.ShapeDtypeStruct((1, width), F32)],
        compiler_params=_cp("parallel"),
    )(zb, mu, dzf)


def _neumann_inverse(p):
    heads = range(len(p))
    eye = (lax.broadcasted_iota(jnp.int32, (CHUNK, CHUNK), 0)
           == lax.broadcasted_iota(jnp.int32, (CHUNK, CHUNK), 1)).astype(F32)
    tinv = [eye + p[h] for h in heads]
    for _ in range(5):
        p = [_dot(p[h], p[h], DN_PRECISION) for h in heads]
        tinv = [tinv[h] + _dot(tinv[h], p[h], DN_PRECISION) for h in heads]
    return tinv


@jax.custom_vjp
def _unit_lower_inverse(p):
    return _neumann_inverse(p)


def _unit_lower_inverse_fwd(p):
    tinv = _neumann_inverse(p)
    return tinv, tinv


def _unit_lower_inverse_bwd(tinv, ct):
    heads = range(len(tinv))
    left = [_dot_tn(tinv[h], ct[h], DN_PRECISION) for h in heads]
    return ([_dot_nt(left[h], tinv[h], DN_PRECISION) for h in heads],)


_unit_lower_inverse.defvjp(_unit_lower_inverse_fwd, _unit_lower_inverse_bwd)


@jax.custom_vjp
def _known_inverse(p, tinv):
    return tinv


_known_inverse.defvjp(lambda p, tinv: (tinv, tinv),
                      lambda tinv, ct: (_unit_lower_inverse_bwd(tinv, ct)[0], [jnp.zeros_like(t) for t in tinv]))


def _dn_chunk(q, k, v, beta, g, state, saved_tinv=None):
    heads = range(len(q))
    ri = lax.broadcasted_iota(jnp.int32, (CHUNK, CHUNK), 0)
    ci = lax.broadcasted_iota(jnp.int32, (CHUNK, CHUNK), 1)
    eye = (ri == ci).astype(F32)
    incl = ri >= ci
    last = lax.broadcasted_iota(jnp.int32, (CHUNK, 1), 0) == CHUNK - 1
    g_row = [jnp.sum(g[h] * eye, axis=0, keepdims=True) for h in heads]
    gc = [jnp.sum(jnp.where(incl, g_row[h], 0.0), axis=1, keepdims=True) for h in heads]
    gc_row = [jnp.sum(gc[h] * eye, axis=0, keepdims=True) for h in heads]
    decay = [jnp.where(incl, jnp.exp(jnp.where(incl, gc[h] - gc_row[h], 0.0)), 0.0) for h in heads]
    kb = [k[h] * beta[h] for h in heads]
    vb = [v[h] * beta[h] for h in heads]
    p = [-jnp.where(ri > ci, _dot_nt(kb[h], k[h]) * decay[h], 0.0) for h in heads]
    tinv = _unit_lower_inverse(p) if saved_tinv is None else _known_inverse(p, saved_tinv)
    eg = [jnp.exp(gc[h]) for h in heads]
    u = [_dot(tinv[h], vb[h]) for h in heads]
    wk = [_dot(tinv[h], kb[h] * eg[h]) for h in heads]
    attn = [_dot_nt(q[h], k[h]) * decay[h] for h in heads]
    g_last = [jnp.sum(jnp.where(last, gc[h], 0.0), axis=0, keepdims=True) for h in heads]
    k_tail = [k[h] * jnp.exp(g_last[h] - gc[h]) for h in heads]
    v_new = [u[h] - _dot(wk[h], state[h]) for h in heads]
    o = [_dot(q[h] * eg[h], state[h]) + _dot(attn[h], v_new[h]) for h in heads]
    new = [state[h] * jnp.exp(g_last[h]) + _dot_tn(k_tail[h], v_new[h]) for h in heads]
    return (o, new, tinv) if saved_tinv is None else (o, new)


def _bg_cols(bg, h, heads):
    lane = lax.broadcasted_iota(jnp.int32, bg.shape, 1)
    beta = jnp.sum(jnp.where(lane == h, bg, 0.0), axis=1, keepdims=True)
    g = jnp.sum(jnp.where(lane == heads + h, bg, 0.0), axis=1, keepdims=True)
    return beta, g


def _dn_fwd(qkv, bg):
    rows = qkv.shape[0]
    heads = qkv.shape[1] // (3 * HEAD_A)
    n = rows // CHUNK
    hp, groups = heads, 1

    def body(q_ref, k_ref, v_ref, bg_ref, o_ref, hist_ref, tinv_ref, s_ref):
        c, grp = pl.program_id(0), pl.program_id(1)

        @pl.when(c == 0)
        def _():
            for i in range(hp):
                s_ref[grp * hp + i] = jnp.zeros((HEAD_A, HEAD_A), F32)

        bg_v = bg_ref[...]
        cols = [slice(i * HEAD_A, (i + 1) * HEAD_A) for i in range(hp)]
        state = [s_ref[grp * hp + i] for i in range(hp)]
        beta_g = [_bg_cols(bg_v, grp * hp + i, heads) for i in range(hp)]
        o, new, tinv = _dn_chunk([q_ref[:, c_] for c_ in cols], [k_ref[:, c_] for c_ in cols],
                                 [v_ref[:, c_] for c_ in cols], [b for b, _ in beta_g], [g for _, g in beta_g], state)
        for i in range(hp):
            hist_ref[0, i] = state[i]
            tinv_ref[0, i] = tinv[i]
            o_ref[:, cols[i]] = o[i]
            s_ref[grp * hp + i] = new[i]

    def part(p):
        return pl.BlockSpec((CHUNK, hp * HEAD_A), lambda c, grp: (c, p * groups + grp))

    return pl.pallas_call(
        body, name="deltanet_fwd", grid=(n, groups),
        in_specs=[part(0), part(1), part(2), pl.BlockSpec((CHUNK, 128), lambda c, grp: (c, 0))],
        out_specs=[part(0), pl.BlockSpec((1, hp, HEAD_A, HEAD_A), lambda c, grp: (c, grp, 0, 0)),
                   pl.BlockSpec((1, hp, CHUNK, CHUNK), lambda c, grp: (c, grp, 0, 0))],
        out_shape=[jax.ShapeDtypeStruct((rows, heads * HEAD_A), F32),
                   jax.ShapeDtypeStruct((n, heads, HEAD_A, HEAD_A), F32),
                   jax.ShapeDtypeStruct((n, heads, CHUNK, CHUNK), F32)],
        scratch_shapes=[pltpu.VMEM((heads, HEAD_A, HEAD_A), F32)],
        compiler_params=_cp("arbitrary", "arbitrary"),
    )(qkv, qkv, qkv, bg)


def _dn_bwd(qkv, bg, hist, tinv_hist, do):
    rows = qkv.shape[0]
    heads = qkv.shape[1] // (3 * HEAD_A)
    n = rows // CHUNK
    hp, groups = heads, 1

    def body(q_ref, k_ref, v_ref, bg_ref, hist_ref, tinv_ref, do_ref, dqkv_ref, dbg_ref, ds_ref):
        c, grp = pl.program_id(0), pl.program_id(1)

        @pl.when(c == 0)
        def _():
            for i in range(hp):
                ds_ref[grp * hp + i] = jnp.zeros((HEAD_A, HEAD_A), F32)

        bg_v = bg_ref[...]
        lane = lax.broadcasted_iota(jnp.int32, (CHUNK, 128), 1)
        cols = [slice(i * HEAD_A, (i + 1) * HEAD_A) for i in range(hp)]
        beta_g = [_bg_cols(bg_v, grp * hp + i, heads) for i in range(hp)]
        _, vjp = jax.vjp(_dn_chunk, [q_ref[:, c_] for c_ in cols], [k_ref[:, c_] for c_ in cols],
                         [v_ref[:, c_] for c_ in cols], [b for b, _ in beta_g], [g for _, g in beta_g],
                         [hist_ref[0, i] for i in range(hp)], [tinv_ref[0, i] for i in range(hp)])
        dq, dk, dv, dbeta, dg, ds, _ = vjp(([do_ref[:, c_] for c_ in cols], [ds_ref[grp * hp + i] for i in range(hp)]))
        dbg = jnp.zeros((CHUNK, 128), F32)
        for i in range(hp):
            h = grp * hp + i
            for p, part_grad in enumerate((dq, dk, dv)):
                dqkv_ref[:, pl.ds((p * heads + i) * HEAD_A, HEAD_A)] = part_grad[i]
            ds_ref[h] = ds[i]
            dbg = dbg + jnp.where(lane == h, dbeta[i], 0.0) + jnp.where(lane == heads + h, dg[i], 0.0)

        @pl.when(grp == 0)
        def _():
            dbg_ref[...] = jnp.zeros_like(dbg_ref)

        dbg_ref[...] += dbg

    def part(p):
        return pl.BlockSpec((CHUNK, hp * HEAD_A), lambda c, grp: (n - 1 - c, p * groups + grp))

    return pl.pallas_call(
        body, name="deltanet_bwd", grid=(n, groups),
        in_specs=[part(0), part(1), part(2), pl.BlockSpec((CHUNK, 128), lambda c, grp: (n - 1 - c, 0)),
                  pl.BlockSpec((1, hp, HEAD_A, HEAD_A), lambda c, grp: (n - 1 - c, grp, 0, 0)),
                  pl.BlockSpec((1, hp, CHUNK, CHUNK), lambda c, grp: (n - 1 - c, grp, 0, 0)), part(0)],
        out_specs=[pl.BlockSpec((CHUNK, 3 * heads * HEAD_A), lambda c, grp: (n - 1 - c, 0)),
                   pl.BlockSpec((CHUNK, 128), lambda c, grp: (n - 1 - c, 0))],
        out_shape=[jax.ShapeDtypeStruct(qkv.shape, F32), jax.ShapeDtypeStruct((rows, 128), F32)],
        scratch_shapes=[pltpu.VMEM((heads, HEAD_A, HEAD_A), F32)],
        compiler_params=_cp("arbitrary", "arbitrary"),
    )(qkv, qkv, qkv, bg, hist, tinv_hist, do)


def _head_mask(heads, width):
    return (lax.broadcasted_iota(jnp.int32, (heads, width), 0)
            == lax.broadcasted_iota(jnp.int32, (heads, width), 1) // HEAD_B)


def _masked_rows(mask, row):
    return jnp.where(mask, row, 0.0).astype(BF16)


def _rwkv_fwd(r, w, k, v, a, b):
    rows, width = r.shape
    heads = width // HEAD_B
    ts = SCAN_STEPS

    def body(r_ref, w_ref, k_ref, v_ref, a_ref, b_ref, y_ref, hist_ref, s_ref):
        @pl.when(pl.program_id(0) == 0)
        def _():
            s_ref[...] = jnp.zeros_like(s_ref)

        mask = _head_mask(heads, width)
        onehot = mask.astype(BF16)
        onehot2 = jnp.concatenate([onehot, onehot], axis=0)
        bd = _block_diag_ones()

        spread_v = [_dot_tn(jnp.concatenate(_hi_lo(v_ref[j]), axis=0), onehot2) for j in range(ts)]
        a_next = pltpu.roll(a_ref[...], ts - 1, 0)
        b_dot_a, k_dot_a = _segsum_many([b_ref[...] * a_next, k_ref[...] * a_next], bd)
        s = s_ref[...]
        ys = []
        for j in range(0, ts, 2):
            row, nxt = pl.ds(j, 1), pl.ds(j + 1, 1)
            hist_ref[j] = s
            sa, base = _segsum_many([((s * a_ref[row, :]).astype(BF16),),
                                     ((s * (w_ref[row, :] * a_ref[nxt, :])).astype(BF16),)], bd)
            sa_next = base + sa * b_dot_a[j:j + 1] + spread_v[j] * k_dot_a[j:j + 1]
            s = s * w_ref[row, :] + sa * b_ref[row, :] + spread_v[j] * k_ref[row, :]
            hist_ref[j + 1] = s
            ys.append(_dot_nt(_masked_rows(mask, r_ref[row, :]), s.astype(BF16)))
            s = s * w_ref[nxt, :] + sa_next * b_ref[nxt, :] + spread_v[j + 1] * k_ref[nxt, :]
            ys.append(_dot_nt(_masked_rows(mask, r_ref[nxt, :]), s.astype(BF16)))
        for j in range(ts):
            y_ref[j] = ys[j]
        s_ref[...] = s

    blk = pl.BlockSpec((ts, width), lambda i: (i, 0))
    blk3 = pl.BlockSpec((ts, heads, HEAD_B), lambda i: (i, 0, 0))
    return pl.pallas_call(
        body, name="rwkv_fwd", grid=(rows // ts,),
        in_specs=[blk, blk, blk, blk3, blk, blk],
        out_specs=[blk3, pl.BlockSpec((ts, HEAD_B, width), lambda i: (i, 0, 0)),
                   pl.BlockSpec((HEAD_B, width), lambda i: (0, 0))],
        out_shape=[jax.ShapeDtypeStruct((rows, heads, HEAD_B), F32), jax.ShapeDtypeStruct((rows, HEAD_B, width), F32),
                   jax.ShapeDtypeStruct((HEAD_B, width), F32)],
        compiler_params=_cp("arbitrary"),
    )(r, w, k, v, a, b)


def _rwkv_bwd(r, w, k, v, a, b, hist, last, dy):
    rows, width = r.shape
    heads = width // HEAD_B
    ts = SCAN_STEPS
    nb = rows // ts

    def body(r_ref, w_ref, k_ref, v_ref, a_ref, b_ref, hist_ref, last_ref, dy_ref,
             dr_ref, dw_ref, dk_ref, dv_ref, da_ref, db_ref, g_ref, after_ref):
        @pl.when(pl.program_id(0) == 0)
        def _():
            g_ref[...] = jnp.zeros_like(g_ref)
            after_ref[...] = last_ref[...]

        mask = _head_mask(heads, width)
        onehot = mask.astype(BF16)
        bd = _block_diag_ones()

        def own_lanes(x):
            return jnp.sum(jnp.where(mask, x, 0.0), axis=0, keepdims=True)

        def colsum(x):
            return jnp.sum(x, axis=0, keepdims=True)

        dy_m = [dy_ref[j].astype(BF16) for j in range(ts)]
        spread_dy = [_dot_tn(dy_m[j], onehot) for j in range(ts)]
        state_after = [hist_ref[j + 1] if j < ts - 1 else after_ref[...] for j in range(ts)]
        dr = [own_lanes(_dot(dy_m[j], state_after[j].astype(BF16))) for j in range(ts)]
        sa_m = [_dot_nt(_masked_rows(mask, a_ref[pl.ds(j, 1), :]), hist_ref[j].astype(BF16)) for j in range(ts)]
        g = g_ref[...]
        dw, dk, db, da, dv = {}, {}, {}, {}, {}
        for j in reversed(range(ts)):
            row = pl.ds(j, 1)
            sp = hist_ref[j]
            g = g + spread_dy[j] * r_ref[row, :]
            (dsa,) = _segsum_many([((g * b_ref[row, :]).astype(BF16),)], bd)
            g_b = g.astype(BF16)
            both = _dot(jnp.concatenate([v_ref[j].astype(BF16), sa_m[j].astype(BF16)], axis=0), g_b)
            dk[j], db[j] = own_lanes(both[:heads]), own_lanes(both[heads:])
            dv[j] = _dot_nt(_masked_rows(mask, k_ref[row, :]), g_b)
            dw[j] = colsum(g * sp)
            da[j] = colsum(sp * dsa)
            g = g * w_ref[row, :] + dsa * a_ref[row, :]
        g_ref[...] = g
        after_ref[...] = hist_ref[0]
        for j in range(ts):
            dv_ref[j] = dv[j]
            for ref, vals in ((dr_ref, dr), (dw_ref, dw), (dk_ref, dk), (da_ref, da), (db_ref, db)):
                ref[pl.ds(j, 1), :] = vals[j]

    blk = pl.BlockSpec((ts, width), lambda i: (nb - 1 - i, 0))
    blk3 = pl.BlockSpec((ts, heads, HEAD_B), lambda i: (nb - 1 - i, 0, 0))
    state = pl.BlockSpec((HEAD_B, width), lambda i: (0, 0))
    return pl.pallas_call(
        body, name="rwkv_bwd", grid=(nb,),
        in_specs=[blk, blk, blk, blk3, blk, blk, pl.BlockSpec((ts, HEAD_B, width), lambda i: (nb - 1 - i, 0, 0)),
                  state, blk3],
        out_specs=[blk, blk, blk, blk3, blk, blk],
        out_shape=[jax.ShapeDtypeStruct((rows, width), F32)] * 3 + [jax.ShapeDtypeStruct((rows, heads, HEAD_B), F32)]
        + [jax.ShapeDtypeStruct((rows, width), F32)] * 2,
        scratch_shapes=[pltpu.VMEM((HEAD_B, width), F32), pltpu.VMEM((HEAD_B, width), F32)],
        compiler_params=_cp("arbitrary"),
    )(r, w, k, v, a, b, hist, last, dy)


def _live(row0, shape):
    return (row0 + lax.broadcasted_iota(jnp.int32, shape, 0)) >= PAD


def _norm_fn(row0, h, gain):
    return (_rms(h, gain),)


def _norm_res_fn(row0, h, gain):
    return _rms(h, gain), h


def _make_bg_fn(heads):
    def fn(row0, x, log_rate, dt_bias):
        lane = lax.broadcasted_iota(jnp.int32, x.shape, 1)
        beta = _sigmoid(x)
        g = -jnp.exp(log_rate) * _softplus(x + dt_bias)
        out = jnp.where(lane < heads, beta, jnp.where(lane < 2 * heads, g, 0.0))
        return (jnp.where(_live(row0, x.shape), out, 0.0),)
    return fn


def _b_pre_fn(row0, zf, w0, w_up, a0, a_up, g_up, k_k, k_a):
    d = w0.shape[1]
    r, k, v = zf[:, :d], zf[:, d:2 * d], zf[:, 2 * d:3 * d]
    lo = zf[:, 3 * d:3 * d + 128]
    lg = zf[:, 3 * d + 128:3 * d + LORA_PAD]
    lane = lax.broadcasted_iota(jnp.int32, lo.shape, 1)
    lw = _dot(jnp.where(lane < LORA_W, jnp.tanh(lo), 0.0), w_up)
    la = _dot(jnp.where(lane >= LORA_W, lo, 0.0), a_up)
    lane_g = lax.broadcasted_iota(jnp.int32, lg.shape, 1)
    gate = _dot(jnp.where(lane_g < LORA_G, _sigmoid(lg), 0.0), g_up)
    decay = jnp.exp(-jnp.exp(-_softplus(-(w0 + lw)) - 0.5))
    a = _sigmoid(a0 + la)
    kx = k * k_k
    kk = kx * lax.rsqrt(_segsum64(kx * kx) + 1e-6)
    k2 = k * (1.0 + (a - 1.0) * k_a)
    return r, decay, k2, v, -kk, kk * a, gate


def _post_fn(row0, o, zg, y, r, k2, v, gate, out_gain, r_k, ln_g, ln_b):
    d = o.shape[1]
    az, ga, gb = zg[:, :d], zg[:, d:2 * d], zg[:, 2 * d:]
    heads = d // HEAD_A
    parts = []
    for h in range(heads):
        oh = o[:, h * HEAD_A:(h + 1) * HEAD_A]
        parts.append(oh * lax.rsqrt(jnp.mean(oh * oh, axis=-1, keepdims=True) + EPS) * out_gain)
    o_a = jnp.concatenate(parts, axis=1) * _silu(az)
    mean = _segsum64(y) * (1.0 / HEAD_B)
    yc = y - mean
    var = _segsum64(yc * yc) * (1.0 / HEAD_B)
    yn = yc * lax.rsqrt(var + GN_EPS) * ln_g + ln_b
    o_b = (yn + _segsum64(r * k2 * r_k) * v) * gate
    return (_sigmoid(ga) * o_a + _sigmoid(gb) * o_b,)


def _loss(h3, target, gain, tb):
    rows, d = h3.shape

    def body(h_ref, t_ref, g_ref, dh_ref, dg_ref, l_ref):
        i = pl.program_id(0)
        live = (i * tb + lax.broadcasted_iota(jnp.int32, (tb, 1), 0)) >= CHUNK
        tgt = t_ref[...]

        def f(h, g):
            err = _rms(h, g) - tgt
            return 0.5 * jnp.sum(jnp.where(live, jnp.mean(err * err, axis=-1, keepdims=True), 0.0))

        val, vjp = jax.vjp(f, h_ref[...], g_ref[...])
        dh, dg = vjp(jnp.ones((), F32))
        dh_ref[...] = dh

        @pl.when(i == 0)
        def _():
            dg_ref[...] = jnp.zeros_like(dg_ref)
            l_ref[...] = jnp.zeros_like(l_ref)

        dg_ref[...] += dg
        l_ref[...] += jnp.full((1, 128), val, F32)

    blk = pl.BlockSpec((tb, d), lambda i: (i, 0))
    return pl.pallas_call(
        body, name="loss", grid=(rows // tb,),
        in_specs=[blk, blk, pl.BlockSpec((1, d), lambda i: (0, 0))],
        out_specs=[blk, pl.BlockSpec((1, d), lambda i: (0, 0)), pl.BlockSpec((1, 128), lambda i: (0, 0))],
        out_shape=[jax.ShapeDtypeStruct((rows, d), F32), jax.ShapeDtypeStruct((1, d), F32),
                   jax.ShapeDtypeStruct((1, 128), F32)],
        compiler_params=_cp("arbitrary"),
    )(h3, target, gain)


def _adamw_math(w, g, m, v):
    m2 = ADAM_B1 * m + (1.0 - ADAM_B1) * g
    v2 = ADAM_B2 * v + (1.0 - ADAM_B2) * (g * g)
    m_hat = m2 / (1.0 - ADAM_B1 ** ADAM_STEP)
    v_hat = v2 / (1.0 - ADAM_B2 ** ADAM_STEP)
    return -ADAM_LR * (m_hat / (jnp.sqrt(v_hat) + ADAM_EPS) + ADAM_WD * w), m2, v2


def _adamw(name, own, landed, w, m, v):
    rows, cols = w.shape
    if rows % 16 == 0:
        rb = _tb(rows, 128)
        grid, blk = (rows // rb,), pl.BlockSpec((rb, cols), lambda i: (i, 0))
        landed_blk = pl.BlockSpec((N_DEV - 1, rb, cols), lambda i: (0, i, 0))
    else:
        grid, blk = (cols // 128,), pl.BlockSpec((rows, 128), lambda i: (0, i))
        landed_blk = pl.BlockSpec((N_DEV - 1, rows, 128), lambda i: (0, 0, i))

    def body(o_ref, s_ref, w_ref, m_ref, v_ref, g_ref, d_ref, m2_ref, v2_ref):
        g = o_ref[...].astype(F32)
        for peer in range(N_DEV - 1):
            g = g + s_ref[peer].astype(F32)
        g_ref[...] = g
        d_ref[...], m2_ref[...], v2_ref[...] = _adamw_math(w_ref[...], g, m_ref[...], v_ref[...])

    return pl.pallas_call(
        body, name=name, grid=grid, in_specs=[blk, landed_blk, blk, blk, blk],
        out_specs=[blk] * 4, out_shape=[jax.ShapeDtypeStruct((rows, cols), F32)] * 4,
        compiler_params=_cp("parallel"),
    )(own, landed, w, m, v)


def _sum_slabs(name, slabs, rb):
    _, rows, cols = slabs.shape

    def body(s_ref, o_ref):
        g = s_ref[0]
        for dev in range(1, N_DEV):
            g = g + s_ref[dev]
        o_ref[...] = g

    return pl.pallas_call(
        body, name=name, grid=(rows // rb,),
        in_specs=[pl.BlockSpec((N_DEV, rb, cols), lambda i: (0, i, 0))],
        out_specs=pl.BlockSpec((rb, cols), lambda i: (i, 0)),
        out_shape=jax.ShapeDtypeStruct((rows, cols), F32), compiler_params=_cp("parallel"),
    )(slabs)


def _adamw_small(w, g, m, v):
    def body(w_ref, g_ref, m_ref, v_ref, d_ref, m2_ref, v2_ref):
        d_ref[...], m2_ref[...], v2_ref[...] = _adamw_math(w_ref[...], g_ref[...], m_ref[...], v_ref[...])

    return pl.pallas_call(body, name="adamw_small", out_shape=[jax.ShapeDtypeStruct(w.shape, F32)] * 3)(w, g, m, v)


def _place():
    return lax.axis_index("x"), lax.axis_index("y"), lax.axis_index("c")


def _index(p):
    return 4 * p[0] + 2 * p[1] + p[2]


def _all_gather(name, xs):
    n = len(xs)

    def body(*refs):
        x_refs, o_refs = refs[:n], refs[n:2 * n]
        send_sems, recv_sems, local_sems = refs[2 * n:]
        x, y, c = _place()
        me, sibling = (x, y, c), (x, y, 1 - c)
        chips = [(1 - x, y), (x, 1 - y), (1 - x, 1 - y)]

        def copy(i, k, block, to, src=None):
            dst = o_refs[i].at[_index(block)]
            return pltpu.make_async_remote_copy(src_ref=dst if src is None else src, dst_ref=dst,
                                                send_sem=send_sems.at[i, k], recv_sem=recv_sems.at[i, k],
                                                device_id=to, device_id_type=MESH_ID)

        mine = [pltpu.make_async_copy(x_refs[i], o_refs[i].at[_index(me)], local_sems.at[i]) for i in range(n)]
        for cp in mine:
            cp.start()
        first = []
        for i in range(n):
            first.append(copy(i, 0, me, sibling, src=x_refs[i]))
            first += [copy(i, 1 + j, me, (*chip, c), src=x_refs[i]) for j, chip in enumerate(chips)]
        for cp in first:
            cp.start()
        passed = []
        for j, chip in enumerate(chips):
            for i in range(n):
                copy(i, 1 + j, (*chip, c), me).wait_recv()
                cp = copy(i, 4 + j, (*chip, c), sibling)
                cp.start()
                passed.append(cp)
        for i in range(n):
            copy(i, 0, sibling, me).wait_recv()
            for j, chip in enumerate(chips):
                copy(i, 4 + j, (*chip, 1 - c), me).wait_recv()
        for cp in first + passed:
            cp.wait_send()
        for cp in mine:
            cp.wait()

    return pl.pallas_call(
        body, name=name, in_specs=[ANY] * n, out_specs=[ANY] * n,
        out_shape=[jax.ShapeDtypeStruct((N_DEV,) + x.shape, x.dtype) for x in xs],
        scratch_shapes=[pltpu.SemaphoreType.DMA((n, 7)), pltpu.SemaphoreType.DMA((n, 7)), pltpu.SemaphoreType.DMA((n,))],
    )(*xs)


def _exchange_start(name, xs, after=None, gather=False):
    n = len(xs)
    copies = n * (N_DEV - 1)
    extra = [] if after is None else [after]

    def body(*refs):
        x_refs, land_refs = refs[:n], refs[n:2 * n]
        sems = refs[2 * n + len(extra):2 * n + len(extra) + 2 * copies]
        token = refs[-1]
        for i, k, peer in _exchange_copies(n):
            _exchange_copy(x_refs, land_refs, sems, i, k, peer, gather).start()
        token[...] = jnp.zeros_like(token)

    lands = [lax.empty((N_DEV,) + x.shape if gather else (N_DEV - 1,) + x.shape[1:], x.dtype) for x in xs]
    out = pl.pallas_call(
        body, name=name,
        out_shape=(*[pltpu.SemaphoreType.DMA(())] * (2 * copies), *[pltpu.HBM(x.shape, x.dtype) for x in xs],
                   *[pltpu.HBM(l.shape, l.dtype) for l in lands], jax.ShapeDtypeStruct((8, 128), F32)),
        in_specs=[HBM_SPEC] * (2 * n) + [ANY] * len(extra),
        out_specs=(*[SEM_SPEC] * (2 * copies), *[HBM_SPEC] * (2 * n), pl.BlockSpec(memory_space=pltpu.VMEM)),
        input_output_aliases={i: 2 * copies + i for i in range(2 * n)},
        compiler_params=pltpu.CompilerParams(has_side_effects=pltpu.SideEffectType.DATAFLOW_SIDE_EFFECTING),
    )(*[pltpu.with_memory_space_constraint(a, pltpu.HBM) for a in list(xs) + lands], *extra)
    sems, rest = list(out[:2 * copies]), out[2 * copies:]
    return sems, list(rest[:n]), list(rest[n:2 * n]), rest[-1]


def _exchange_copies(n):
    x, y, c = _place()
    for k in range(1, N_DEV):
        peer = ((1 - x) if k & 4 else x, (1 - y) if k & 2 else y, (1 - c) if k & 1 else c)
        for i in range(n):
            yield i, k - 1, peer


def _exchange_copy(x_refs, land_refs, sems, i, k, peer, gather, arriving=False):
    copies = len(sems) // 2
    which = i * (N_DEV - 1) + k
    src = x_refs[i] if gather else x_refs[i].at[_index(peer)]
    dst = land_refs[i].at[_index(peer if arriving else _place())] if gather else land_refs[i].at[k]
    return pltpu.make_async_remote_copy(src_ref=src, dst_ref=dst, send_sem=sems[which], recv_sem=sems[copies + which],
                                        device_id=peer, device_id_type=MESH_ID)


def _exchange_wait(name, sems, xs, lands, after, gather=False):
    n = len(xs)

    def body(*refs):
        x_refs, land_refs = refs[:n], refs[n:2 * n]
        sem_refs = refs[2 * n:2 * n + len(sems)]
        for i, k, peer in _exchange_copies(n):
            _exchange_copy(x_refs, land_refs, sem_refs, i, k, peer, gather).wait_send()
            _exchange_copy(x_refs, land_refs, sem_refs, i, k, peer, gather, arriving=True).wait_recv()

    out = pl.pallas_call(
        body, name=name,
        out_shape=(*[pltpu.HBM(x.shape, x.dtype) for x in xs], *[pltpu.HBM(l.shape, l.dtype) for l in lands]),
        in_specs=[HBM_SPEC] * (2 * n) + [SEM_SPEC] * len(sems) + [ANY], out_specs=tuple([HBM_SPEC] * (2 * n)),
        input_output_aliases={i: i for i in range(2 * n)},
        compiler_params=pltpu.CompilerParams(has_side_effects=pltpu.SideEffectType.DATAFLOW_SIDE_EFFECTING),
    )(*xs, *lands, *sems, after)
    return list(out[:n]), list(out[n:])


def _pack(arrays):
    flat = jnp.concatenate([a.reshape(-1) for a in arrays])
    pad = (-flat.shape[0]) % 1024
    return jnp.pad(flat, (0, pad)).reshape(-1, 128)


def _unpack(packed, shapes):
    flat = packed.reshape(-1)
    out, pos = [], 0
    for s in shapes:
        size = 1
        for dim in s:
            size *= dim
        out.append(flat[pos:pos + size].reshape(s))
        pos += size
    return out


def _cols_from_slabs(stack):
    return jnp.transpose(stack, (1, 0, 2)).reshape(stack.shape[1], -1)


def kernel(x, meta_tokens, ffn1_norm, ffn1_w_gu, ffn1_w_down, mix_norm, w_in, a_conv_w, a_log_rate, a_dt_bias, a_out_norm, b_shift_mu, b_w0, b_w_up, b_a0, b_a_up, b_g_up, b_k_k, b_k_a, b_r_k, b_ln_gain, b_ln_bias, w_out, ffn2_norm, ffn2_w_gu, ffn2_w_down, final_norm, loss_target, m_meta_tokens, m_ffn1_norm, m_ffn1_w_gu, m_ffn1_w_down, m_mix_norm, m_w_in, m_a_conv_w, m_a_log_rate, m_a_dt_bias, m_a_out_norm, m_b_shift_mu, m_b_w0, m_b_w_up, m_b_a0, m_b_a_up, m_b_g_up, m_b_k_k, m_b_k_a, m_b_r_k, m_b_ln_gain, m_b_ln_bias, m_w_out, m_ffn2_norm, m_ffn2_w_gu, m_ffn2_w_down, m_final_norm, v_meta_tokens, v_ffn1_norm, v_ffn1_w_gu, v_ffn1_w_down, v_mix_norm, v_w_in, v_a_conv_w, v_a_log_rate, v_a_dt_bias, v_a_out_norm, v_b_shift_mu, v_b_w0, v_b_w_up, v_b_a0, v_b_a_up, v_b_g_up, v_b_k_k, v_b_k_a, v_b_r_k, v_b_ln_gain, v_b_ln_bias, v_w_out, v_ffn2_norm, v_ffn2_w_gu, v_ffn2_w_down, v_final_norm):
    names = ['meta_tokens', 'ffn1_norm', 'ffn1_w_gu', 'ffn1_w_down', 'mix_norm', 'w_in', 'a_conv_w', 'a_log_rate',
             'a_dt_bias', 'a_out_norm', 'b_shift_mu', 'b_w0', 'b_w_up', 'b_a0', 'b_a_up', 'b_g_up', 'b_k_k', 'b_k_a',
             'b_r_k', 'b_ln_gain', 'b_ln_bias', 'w_out', 'ffn2_norm', 'ffn2_w_gu', 'ffn2_w_down', 'final_norm']
    env = dict(locals())
    wts = {k: env[k] for k in names}
    mom_m = {k: env['m_' + k] for k in names}
    mom_v = {k: env['v_' + k] for k in names}
    big = ['ffn1_w_gu', 'ffn1_w_down', 'w_in', 'w_out', 'ffn2_w_gu', 'ffn2_w_down']
    col_sharded = {'ffn1_w_gu', 'w_in', 'ffn2_w_gu'}
    shard_of = lambda tree, k: tree[k][0].T if k in col_sharded else tree[k][0]
    small_sharded = ['meta_tokens', 'a_conv_w', 'b_w_up', 'b_a_up', 'b_g_up']
    replicated = [k for k in names if k not in big and k not in small_sharded]

    seq, d = x.shape[1], x.shape[2]
    rows = PAD + N_META + seq
    heads_a = d // HEAD_A
    tb_mm = _tb(rows, 416)
    tb_vjp = _tb(rows, 208)
    tb_dw = _tb(rows, 2080)
    tb_ffn = _tb(rows, 832)
    me = _index(_place())

    local_bf = {k: shard_of(wts, k).astype(BF16) for k in big}
    gu1, down1, meta = _all_gather("gather_ffn1", [local_bf['ffn1_w_gu'], local_bf['ffn1_w_down'], wts['meta_tokens']])
    small_rest = small_sharded[1:]
    late_keys = ['w_out', 'ffn2_w_gu', 'ffn2_w_down']
    gather_mid = _exchange_start("gather_start_mid", [local_bf['w_in'], _pack([wts[k][0] for k in small_rest])],
                                 after=gu1, gather=True)
    gather_late = _exchange_start("gather_start_late", [local_bf[k] for k in late_keys], after=gather_mid[-1],
                                  gather=True)

    def finish_gather(tag, started, after):
        sems, mine, lands, _ = started
        mine, lands = _exchange_wait("gather_wait_" + tag, sems, mine, lands, after, gather=True)
        return [lax.dynamic_update_index_in_dim(land, own[None], me, 0) for land, own in zip(lands, mine)]

    full = {'ffn1_w_gu': gu1, 'ffn1_w_down': down1.reshape(-1, d), 'meta_tokens': _cols_from_slabs(meta)}
    for k in replicated:
        full[k] = wts[k].reshape(1, -1)

    h0 = jnp.concatenate([jnp.zeros((PAD, d), F32) + gather_late[-1][:1, :1], full['meta_tokens'], x[0]], axis=0)
    h1 = _ffn_fwd("ffn1_fwd", h0, full['ffn1_norm'], full['ffn1_w_gu'], full['ffn1_w_down'], tb_ffn)
    (u,) = _tok_fwd("mix_norm_fwd", _norm_fn, [h1], [full['mix_norm']], [(d, BF16)], tb_mm)

    win_stack, small_stack = finish_gather("mid", gather_mid, u)
    full['w_in'] = win_stack.reshape(-1, d)
    small_flat, pos = small_stack.reshape(N_DEV, -1), 0
    for k in small_rest:
        shape = wts[k][0].shape
        full[k] = _cols_from_slabs(small_flat[:, pos:pos + shape[0] * shape[1]].reshape((N_DEV,) + shape))
        pos += shape[0] * shape[1]

    win = full['w_in']
    n_b = 3 * d + LORA_W + LORA_A + LORA_G
    off_beta, off_b = 4 * d, 4 * d + 2 * heads_a
    off_ga = off_b + n_b
    b_width = 3 * d + LORA_PAD
    zrows = lambda r: jnp.zeros((r, d), BF16)
    w_qkv = win[:3 * d]
    w_zg = jnp.concatenate([win[3 * d:4 * d], win[off_ga:off_ga + 2 * d]], axis=0)
    w_b = jnp.concatenate([win[off_b:off_b + n_b], zrows(b_width - n_b)], axis=0)
    w_bg = jnp.concatenate([win[off_beta:off_beta + 2 * heads_a], zrows(128 - 2 * heads_a)], axis=0)

    def lanes(vec, start, width):
        return jnp.pad(vec.reshape(1, -1), ((0, 0), (start, width - start - vec.size)))

    log_rate = lanes(wts['a_log_rate'], heads_a, 128)
    dt_bias = lanes(wts['a_dt_bias'], heads_a, 128)
    mu = lanes(wts['b_shift_mu'], 0, b_width)
    w_up = jnp.pad(full['b_w_up'], ((0, 128 - LORA_W), (0, 0)))
    a_up = jnp.pad(full['b_a_up'], ((LORA_W, 0), (0, 0)))
    g_up = jnp.pad(full['b_g_up'], ((0, 256 - LORA_G), (0, 0)))
    b_pars = [full['b_w0'], w_up, full['b_a0'], a_up, g_up, full['b_k_k'], full['b_k_a']]
    post_pars = [full['a_out_norm'], full['b_r_k'], full['b_ln_gain'], full['b_ln_bias']]
    bg_fn = _make_bg_fn(heads_a)

    z_qkv = _mm("in_qkv", u, w_qkv, trans_b=True, tb=tb_mm, tn=_col_tile(3 * d, 1536))
    z_zg = _mm("in_zg", u, w_zg, trans_b=True, tb=tb_mm, tn=_col_tile(3 * d, 1536))
    z_b = _mm("in_b", u, w_b, trans_b=True, tb=tb_mm, tn=_col_tile(b_width, 1536))
    z_bg = _mm("in_bg", u, w_bg, trans_b=True, tb=tb_mm, tn=128)
    qkv = _a_pre_fwd(z_qkv, full['a_conv_w'])
    (bg,) = _tok_fwd("bg_fwd", bg_fn, [z_bg], [log_rate, dt_bias], [(128, F32)], tb_mm)
    o_dn, dn_hist, dn_tinv = _dn_fwd(qkv, bg)
    zf = _shift_fwd(z_b, mu)
    rr, ww, kk2, vv, av, bv, gate = _tok_fwd("b_pre_fwd", _b_pre_fn, [zf], b_pars, [(d, F32)] * 7, tb_vjp)
    per_head = lambda t: t.reshape(rows, d // HEAD_B, HEAD_B)
    y_heads, b_hist, b_last = _rwkv_fwd(rr, ww, kk2, per_head(vv), av, bv)
    y_b = y_heads.reshape(rows, d)
    w_out_stack, full['ffn2_w_gu'], down2 = finish_gather("late", gather_late, y_heads)
    full['w_out'], full['ffn2_w_down'] = w_out_stack.reshape(-1, d), down2.reshape(-1, d)
    post_toks = [o_dn, z_zg, y_b, rr, kk2, vv, gate]
    (merged,) = _tok_fwd("post_fwd", _post_fn, post_toks, post_pars, [(d, BF16)], tb_vjp)
    h2 = _mm("out_proj", merged, full['w_out'], add=h1, tb=tb_mm, tn=d)
    h3 = _ffn_fwd("ffn2_fwd", h2, full['ffn2_norm'], full['ffn2_w_gu'], full['ffn2_w_down'], tb_ffn)

    target = jnp.pad(loss_target[0], ((CHUNK, 0), (0, 0)))
    dh3, g_final, loss_part = _loss(h3, target, full['final_norm'].reshape(1, d), tb_vjp)

    def ffn_backward(tag, h, dout, key_norm, key_gu, key_down):
        dh, dh_bf, dgain, xn, act, dgate, dup, dhalf = _ffn_bwd(tag + "_bwd", h, full[key_norm], dout, full[key_gu],
                                                                full[key_down], tb_mm)
        fc = dgate.shape[2]
        d_gu = jnp.concatenate([_mm_tn_from_slabs(tag + "_dw_gate", dgate, xn, tk=tb_dw).reshape(-1, fc, d),
                                _mm_tn_from_slabs(tag + "_dw_up", dup, xn, tk=tb_dw).reshape(-1, fc, d)], axis=0)
        d_down = _mm_tn_from_slabs(tag + "_dw_down", act, dhalf, tk=tb_dw).reshape(N_DEV, -1, d)
        return dh, dh_bf, dgain, d_gu, d_down

    dh2, dh2_bf, g_ffn2_norm, g_ffn2_gu, g_ffn2_down = ffn_backward("ffn2", h2, dh3, 'ffn2_norm', 'ffn2_w_gu',
                                                                    'ffn2_w_down')
    g_w_out = _mm_tn("dw_out", merged, dh2_bf, tm=d, tn=d, tk=tb_dw).reshape(N_DEV, -1, d)

    def start_exchange(tag, keys, slabs, after=None):
        sems, kept, lands, token = _exchange_start("exchange_start_" + tag, slabs, after)
        return (tag, keys, sems, kept, lands), token

    ex_ffn2, token_ffn2 = start_exchange("ffn2", ['ffn2_w_gu', 'ffn2_w_down', 'w_out'], [g_ffn2_gu, g_ffn2_down, g_w_out])
    dmerged = _mm("d_merged", dh2_bf, full['w_out'], trans_b=True, after=token_ffn2, tb=tb_mm, tn=d)
    post_grads = _tok_bwd("post_bwd", _post_fn, post_toks, post_pars, [[dmerged]], list(range(7)), tb_vjp,
                          [F32, BF16] + [F32] * 5)
    do_dn, dz_zg, dy_b, dr1, dk1, dv1, dgate = post_grads[:7]
    g_out_norm, g_r_k, g_ln_g, g_ln_b = post_grads[7:]
    dr2, dw2, dk2, dv_heads, da2, db2 = _rwkv_bwd(rr, ww, kk2, per_head(vv), av, bv, b_hist, b_last, per_head(dy_b))
    dv2 = dv_heads.reshape(rows, d)
    b_grads = _tok_bwd("b_pre_bwd", _b_pre_fn, [zf], b_pars,
                       [[dr1, dr2], [dw2], [dk1, dk2], [dv1, dv2], [da2], [db2], [dgate]], [0], tb_vjp)
    dzf = b_grads[0]
    g_w0, g_w_up, g_a0, g_a_up, g_g_up, g_k_k, g_k_a = b_grads[1:]
    dz_b, g_mu = _shift_bwd(z_b, mu, dzf)
    dqkv, dbg = _dn_bwd(qkv, bg, dn_hist, dn_tinv, do_dn)
    dz_qkv, g_conv = _a_pre_bwd(z_qkv, full['a_conv_w'], dqkv)
    dz_bg, g_log_rate, g_dt_bias = _tok_bwd("bg_bwd", bg_fn, [z_bg], [log_rate, dt_bias], [[dbg]], [0], tb_mm, [BF16])

    small_early = {
        'a_conv_w': g_conv, 'a_log_rate': g_log_rate[:, heads_a:2 * heads_a],
        'a_dt_bias': g_dt_bias[:, heads_a:2 * heads_a], 'a_out_norm': g_out_norm, 'b_shift_mu': g_mu[:, :n_b],
        'b_w0': g_w0, 'b_w_up': g_w_up[:LORA_W], 'b_a0': g_a0, 'b_a_up': g_a_up[LORA_W:], 'b_g_up': g_g_up[:LORA_G],
        'b_k_k': g_k_k, 'b_k_a': g_k_a, 'b_r_k': g_r_k, 'b_ln_gain': g_ln_g, 'b_ln_bias': g_ln_b,
        'ffn2_norm': g_ffn2_norm, 'final_norm': g_final,
    }
    gather_small = _exchange_start("gather_start_small", [_pack(list(small_early.values()))], gather=True)

    du = None
    g_w_in_parts = []
    for tag, dz, wpiece in (("qkv", dz_qkv, w_qkv), ("zg", dz_zg, w_zg), ("b", dz_b, w_b), ("bg", dz_bg, w_bg)):
        du = _mm("du_" + tag, dz, wpiece, add=du, after=gather_small[-1] if du is None else None, tb=tb_mm, tn=d)
        g_w_in_parts.append(_mm_tn("dw_in_" + tag, dz, u, tm=_col_tile(dz.shape[1], 1536), tn=d, tk=tb_dw))
    gp_qkv, gp_zg, gp_b, gp_bg = g_w_in_parts
    g_w_in = jnp.concatenate([gp_qkv, gp_zg[:d], gp_bg[:2 * heads_a], gp_b[:n_b], gp_zg[d:]],
                             axis=0).reshape(N_DEV, -1, d)
    ex_w_in, token_w_in = start_exchange("w_in", ['w_in'], [g_w_in])
    dh1, g_mix_norm = _tok_bwd("mix_norm_bwd", _norm_res_fn, [h1], [full['mix_norm']], [[du], [dh2]], [0], tb_vjp,
                               after=token_w_in)
    dh0, _, g_ffn1_norm, g_ffn1_gu, g_ffn1_down = ffn_backward("ffn1", h0, dh1, 'ffn1_norm', 'ffn1_w_gu', 'ffn1_w_down')

    small_late = {'meta_tokens': dh0[PAD:CHUNK], 'ffn1_norm': g_ffn1_norm, 'mix_norm': g_mix_norm}
    packed_late = _pack(list(small_late.values()) + [loss_part[:, :1]])
    (late_parts,) = _all_gather("gather_small_grads", [packed_late])
    ex_ffn1, _ = start_exchange("ffn1", ['ffn1_w_gu', 'ffn1_w_down'], [g_ffn1_gu, g_ffn1_down], after=late_parts)
    (early_parts,) = finish_gather("small", gather_small, late_parts)
    pieces = (_unpack(_sum_slabs("sum_small_early", early_parts, early_parts.shape[1]),
                      [g.shape for g in small_early.values()])
              + _unpack(_sum_slabs("sum_small_grads", late_parts, packed_late.shape[0]),
                        [g.shape for g in small_late.values()] + [(1, 1)]))
    small_names = list(small_early) + list(small_late)
    small_grad = dict(zip(small_names, pieces[:-1]))
    loss = pieces[-1].reshape(())

    grads, deltas, new_m, new_v = {}, {}, {}, {}
    local_small = {}
    for k in small_names:
        g = small_grad[k]
        if k in small_sharded:
            width = wts[k].shape[-1]
            g = lax.dynamic_slice_in_dim(g, me * width, width, axis=1)
        local_small[k] = g.reshape(wts[k].shape)
    pk = lambda tree: _pack([tree[k] for k in small_names])
    dl_s, m_s, v_s = _adamw_small(pk(wts), pk(local_small), pk(mom_m), pk(mom_v))
    shapes = [wts[k].shape for k in small_names]
    for k, dl, m2, v2 in zip(small_names, _unpack(dl_s, shapes), _unpack(m_s, shapes), _unpack(v_s, shapes)):
        grads[k], deltas[k], new_m[k], new_v[k] = local_small[k], dl, m2, v2

    done = dl_s
    for tag, keys, sems, kept, lands in (ex_ffn2, ex_w_in, ex_ffn1):
        kept, lands = _exchange_wait("exchange_wait_" + tag, sems, kept, lands, done)
        for k, slabs, landed in zip(keys, kept, lands):
            own = lax.dynamic_index_in_dim(slabs, me, axis=0, keepdims=False)
            res = _adamw("adamw_" + k, own, landed, shard_of(wts, k), shard_of(mom_m, k), shard_of(mom_v, k))
            done = res[1]
            res = [(t.T if k in col_sharded else t)[None] for t in res]
            grads[k], deltas[k], new_m[k], new_v[k] = res

    grad_x = dh0[CHUNK:][None]
    return (loss, grad_x, *[grads[k] for k in names], *[deltas[k] for k in names],
            *[new_m[k] for k in names], *[new_v[k] for k in names])
```

```python
import functools

import jax
import jax.numpy as jnp
from jax import lax
from jax.experimental import pallas as pl
from jax.experimental.pallas import tpu as pltpu

F32 = jnp.float32
BF16 = jnp.bfloat16
N_DEV = 8
N_META = 16
CHUNK = 64
PAD = CHUNK - N_META
HEAD_A = 128
HEAD_B = 64
LORA_W, LORA_A, LORA_G = 64, 64, 160
LORA_PAD = 384
EPS = 1e-6
GN_EPS = HEAD_B * 1e-5
ADAM_LR, ADAM_B1, ADAM_B2, ADAM_EPS, ADAM_WD, ADAM_STEP = 0.001, 0.9, 0.999, 1e-08, 0.01, 10
SCAN_STEPS = 32
MXU_WIDTH = 256
VMEM_LIMIT = 56 * 1024 * 1024
DN_PRECISION = lax.Precision.HIGH
MESH_ID = pl.DeviceIdType.MESH
ANY = pl.BlockSpec(memory_space=pl.ANY)
HBM_SPEC = pl.BlockSpec(memory_space=pltpu.HBM)
SEM_SPEC = pl.BlockSpec(memory_space=pltpu.SEMAPHORE)


def _cp(*sem):
    return pltpu.CompilerParams(dimension_semantics=sem, vmem_limit_bytes=VMEM_LIMIT)


def _tb(t, target):
    best = 16
    for d in range(16, target + 1, 16):
        if t % d == 0:
            best = d
    return best


def _sigmoid(x):
    return 1.0 / (1.0 + jnp.exp(-x))


def _silu(x):
    return x * _sigmoid(x)


def _softplus(x):
    return jnp.maximum(x, 0.0) + jnp.log(1.0 + jnp.exp(-jnp.abs(x)))


def _dot_nt(a, b, precision=None):
    return lax.dot_general(a, b, (((1,), (1,)), ((), ())), preferred_element_type=F32, precision=precision)


def _dot_tn(a, b, precision=None):
    return lax.dot_general(a, b, (((0,), (0,)), ((), ())), preferred_element_type=F32, precision=precision)


def _dot(a, b, precision=None):
    return jnp.dot(a, b, preferred_element_type=F32, precision=precision)


def _block_diag_ones():
    i = lax.broadcasted_iota(jnp.int32, (MXU_WIDTH, MXU_WIDTH), 0) // HEAD_B
    j = lax.broadcasted_iota(jnp.int32, (MXU_WIDTH, MXU_WIDTH), 1) // HEAD_B
    return (i == j).astype(BF16)


def _hi_lo(x):
    hi = x.astype(BF16)
    return hi, (x - hi.astype(F32)).astype(BF16)


def _segsum_many(xs, bd):
    groups = [x if isinstance(x, tuple) else _hi_lo(x) for x in xs]
    rows = groups[0][0].shape[0]
    stacked = jnp.concatenate([p for grp in groups for p in grp], axis=0)
    out = jnp.concatenate([_dot(stacked[:, s:s + MXU_WIDTH], bd) for s in range(0, stacked.shape[1], MXU_WIDTH)], axis=1)
    res, pos = [], 0
    for grp in groups:
        acc = out[pos:pos + rows]
        for j in range(1, len(grp)):
            acc = acc + out[pos + j * rows:pos + (j + 1) * rows]
        res.append(acc)
        pos += len(grp) * rows
    return res


def _segsum_impl(x):
    return _segsum_many([x], _block_diag_ones())[0]


@jax.custom_vjp
def _segsum64(x):
    return _segsum_impl(x)


_segsum64.defvjp(lambda x: (_segsum_impl(x), None), lambda _, ct: (_segsum_impl(ct),))


def _tok(t):
    return t if isinstance(t, tuple) else (t, t.shape[1], 0)


def _tok_spec(tb, width, colblk):
    return pl.BlockSpec((tb, width), lambda i: (i, colblk))


def _par_spec(p):
    return pl.BlockSpec(p.shape, lambda i: (0, 0))


def _tok_fwd(name, fn, toks, pars, outs, tb):
    toks = [_tok(t) for t in toks]
    rows = toks[0][0].shape[0]
    n_in = len(toks) + len(pars)

    def body(*refs):
        row0 = pl.program_id(0) * tb
        res = fn(row0, *[r[...] for r in refs[:n_in]])
        for r, o in zip(refs[n_in:], res):
            r[...] = o.astype(r.dtype)

    return pl.pallas_call(
        body, name=name, grid=(rows // tb,),
        in_specs=[_tok_spec(tb, w, c) for _, w, c in toks] + [_par_spec(p) for p in pars],
        out_specs=[_tok_spec(tb, w, 0) for w, _ in outs],
        out_shape=[jax.ShapeDtypeStruct((rows, w), dt) for w, dt in outs],
        compiler_params=_cp("parallel"),
    )(*[a for a, _, _ in toks], *pars)


def _tok_bwd(name, fn, toks, pars, cts, want, tb, want_dtypes=None, after=None):
    toks = [_tok(t) for t in toks]
    want_dtypes = want_dtypes or [F32] * len(want)
    cts = [[_tok(c) for c in group] for group in cts]
    flat_cts = [c for group in cts for c in group]
    rows = toks[0][0].shape[0]
    n_tok, n_par, n_ct = len(toks), len(pars), len(flat_cts)
    extra = [] if after is None else [after]

    def body(*refs):
        i = pl.program_id(0)
        row0 = i * tb
        prim = [r[...].astype(F32) for r in refs[:n_tok + n_par]]
        ct_refs = list(refs[n_tok + n_par:n_tok + n_par + n_ct])
        out_refs = refs[n_tok + n_par + n_ct + len(extra):]
        res, vjp = jax.vjp(lambda *a: fn(row0, *a), *prim)
        ct = []
        for group, o in zip(cts, res):
            acc = None
            for _ in group:
                v = ct_refs.pop(0)[...].astype(F32)
                acc = v if acc is None else acc + v
            ct.append(acc.astype(o.dtype))
        grads = vjp(tuple(ct))
        for r, k in zip(out_refs[:len(want)], want):
            r[...] = grads[k].astype(r.dtype)

        @pl.when(i == 0)
        def _():
            for r in out_refs[len(want):]:
                r[...] = jnp.zeros_like(r)

        for r, g in zip(out_refs[len(want):], grads[n_tok:]):
            r[...] += g

    return pl.pallas_call(
        body, name=name, grid=(rows // tb,),
        in_specs=[_tok_spec(tb, w, c) for _, w, c in toks] + [_par_spec(p) for p in pars]
        + [_tok_spec(tb, w, c) for _, w, c in flat_cts] + [ANY] * len(extra),
        out_specs=[_tok_spec(tb, toks[k][1], 0) for k in want] + [_par_spec(p) for p in pars],
        out_shape=[jax.ShapeDtypeStruct((rows, toks[k][1]), dt) for k, dt in zip(want, want_dtypes)]
        + [jax.ShapeDtypeStruct(p.shape, F32) for p in pars],
        compiler_params=_cp("arbitrary"),
    )(*[a for a, _, _ in toks], *pars, *[a for a, _, _ in flat_cts], *extra)


def _mm(name, a, b, *, trans_b=False, add=None, after=None, tb, tn):
    rows, k = a.shape
    n = b.shape[0] if trans_b else b.shape[1]

    def body(*refs):
        a_ref, b_ref = refs[:2]
        o_ref = refs[-1]
        acc = _dot_nt(a_ref[...], b_ref[...]) if trans_b else _dot(a_ref[...], b_ref[...])
        if add is not None:
            acc = acc + refs[2][...]
        o_ref[...] = acc

    in_specs = [pl.BlockSpec((tb, k), lambda i, j: (i, 0)),
                pl.BlockSpec((tn, k), lambda i, j: (j, 0)) if trans_b else pl.BlockSpec((k, tn), lambda i, j: (0, j))]
    args = [a, b]
    if add is not None:
        in_specs.append(pl.BlockSpec((tb, tn), lambda i, j: (i, j)))
        args.append(add)
    if after is not None:
        in_specs.append(ANY)
        args.append(after)
    return pl.pallas_call(
        body, name=name, grid=(rows // tb, n // tn), in_specs=in_specs,
        out_specs=pl.BlockSpec((tb, tn), lambda i, j: (i, j)),
        out_shape=jax.ShapeDtypeStruct((rows, n), F32),
        compiler_params=_cp("parallel", "parallel"),
    )(*args)


def _mm_tn_call(name, grid, a, b, a_spec, b_spec, o_spec, acc_shape, out_shape, after=None):
    last = len(grid) - 1
    extra = [] if after is None else [after]

    def body(a_ref, b_ref, *rest):
        o_ref, acc_ref = rest[len(extra):]
        k = pl.program_id(last)

        @pl.when(k == 0)
        def _():
            acc_ref[...] = jnp.zeros_like(acc_ref)

        a_blk = a_ref[0] if len(a_ref.shape) == 3 else a_ref[...]
        b_blk = b_ref[0] if len(b_ref.shape) == 3 else b_ref[...]
        acc_ref[...] += _dot_tn(a_blk, b_blk)

        @pl.when(k == grid[last] - 1)
        def _():
            if len(o_ref.shape) == 3:
                o_ref[0] = acc_ref[...].astype(o_ref.dtype)
            else:
                o_ref[...] = acc_ref[...].astype(o_ref.dtype)

    return pl.pallas_call(
        body, name=name, grid=grid, in_specs=[a_spec, b_spec] + [ANY] * len(extra), out_specs=o_spec,
        out_shape=jax.ShapeDtypeStruct(out_shape, BF16), scratch_shapes=[pltpu.VMEM(acc_shape, F32)],
        compiler_params=_cp(*(["parallel"] * last + ["arbitrary"])),
    )(a, b, *extra)


def _mm_tn(name, a, b, *, tm, tn, tk):
    rows, m = a.shape
    n = b.shape[1]
    return _mm_tn_call(name, (m // tm, n // tn, rows // tk), a, b,
                       pl.BlockSpec((tk, tm), lambda i, j, k: (k, i)), pl.BlockSpec((tk, tn), lambda i, j, k: (k, j)),
                       pl.BlockSpec((tm, tn), lambda i, j, k: (i, j)), (tm, tn), (m, n))


def _mm_tn_from_slabs(name, a3, b, *, tk, after=None):
    s, rows, c = a3.shape
    n = b.shape[1]
    return _mm_tn_call(name, (s, rows // tk), a3, b,
                       pl.BlockSpec((1, tk, c), lambda i, k: (i, k, 0)), pl.BlockSpec((tk, n), lambda i, k: (k, 0)),
                       pl.BlockSpec((c, n), lambda i, k: (i, 0)), (c, n), (s * c, n), after)


def _col_tile(n, target):
    if n <= target:
        return n
    best = 128
    for d in range(128, target + 1, 128):
        if n % d == 0:
            best = d
    return best


def _rms(x, gain):
    return x * lax.rsqrt(jnp.mean(x * x, axis=-1, keepdims=True) + EPS) * gain


def _ffn_specs(d, fc, nj):
    return [pl.BlockSpec((1, fc, d), lambda i, j: (j, 0, 0)), pl.BlockSpec((1, fc, d), lambda i, j: (nj + j, 0, 0)),
            pl.BlockSpec((fc, d), lambda i, j: (j, 0))]


def _ffn_fwd(name, h, gain, wgu, wd, tb):
    rows, d = h.shape
    nj = wgu.shape[0] // 2
    fc = wgu.shape[1]

    def body(h_ref, g_ref, wg_ref, wu_ref, wd_ref, o_ref, xn_s, acc_s):
        j = pl.program_id(1)

        @pl.when(j == 0)
        def _():
            xn_s[...] = _rms(h_ref[...], g_ref[...]).astype(BF16)
            acc_s[...] = jnp.zeros_like(acc_s)

        wg, wu, wdn = wg_ref[0], wu_ref[0], wd_ref[...]
        for half in range(2):
            rs = pl.ds(half * (tb // 2), tb // 2)
            xn = xn_s[rs, :]
            gate = _dot_nt(xn, wg)
            up = _dot_nt(xn, wu)
            acc_s[rs, :] += _dot((_silu(gate) * up).astype(BF16), wdn)

        @pl.when(j == nj - 1)
        def _():
            o_ref[...] = h_ref[...] + 0.5 * acc_s[...]

    return pl.pallas_call(
        body, name=name, grid=(rows // tb, nj),
        in_specs=[pl.BlockSpec((tb, d), lambda i, j: (i, 0)), pl.BlockSpec((1, d), lambda i, j: (0, 0))]
        + _ffn_specs(d, fc, nj),
        out_specs=pl.BlockSpec((tb, d), lambda i, j: (i, 0)),
        out_shape=jax.ShapeDtypeStruct((rows, d), F32),
        scratch_shapes=[pltpu.VMEM((tb, d), BF16), pltpu.VMEM((tb, d), F32)],
        compiler_params=_cp("parallel", "arbitrary"),
    )(h, gain, wgu, wgu, wd)


def _ffn_bwd(name, h, gain, dout, wgu, wd, tb):
    rows, d = h.shape
    nj = wgu.shape[0] // 2
    fc = wgu.shape[1]

    def body(h_ref, g_ref, do_ref, wg_ref, wu_ref, wd_ref,
             dh_ref, dhb_ref, dg_ref, xn_ref, act_ref, dgate_ref, dup_ref, dhalf_ref, dxn_s):
        i, j = pl.program_id(0), pl.program_id(1)

        @pl.when(j == 0)
        def _():
            xn_ref[...] = _rms(h_ref[...], g_ref[...]).astype(BF16)
            dhalf_ref[...] = (0.5 * do_ref[...]).astype(BF16)
            dxn_s[...] = jnp.zeros_like(dxn_s)

        wg, wu, wdn = wg_ref[0], wu_ref[0], wd_ref[...]
        for half in range(2):
            rs = pl.ds(half * (tb // 2), tb // 2)
            xn = xn_ref[rs, :]
            gate = _dot_nt(xn, wg)
            up = _dot_nt(xn, wu)
            sg = _sigmoid(gate)
            dact = _dot_nt(dhalf_ref[rs, :], wdn)
            act_ref[0, rs, :] = (gate * sg * up).astype(BF16)
            dgate = (dact * up * (sg * (1.0 + gate * (1.0 - sg)))).astype(BF16)
            dup = (dact * gate * sg).astype(BF16)
            dgate_ref[0, rs, :] = dgate
            dup_ref[0, rs, :] = dup
            dxn_s[rs, :] += _dot(dgate, wg) + _dot(dup, wu)

        @pl.when((i == 0) & (j == 0))
        def _():
            dg_ref[...] = jnp.zeros_like(dg_ref)

        @pl.when(j == nj - 1)
        def _():
            x = h_ref[...]
            r = lax.rsqrt(jnp.mean(x * x, axis=-1, keepdims=True) + EPS)
            dxn = dxn_s[...]
            dyg = dxn * g_ref[...]
            dh = do_ref[...] + r * dyg - x * (r * r * r) * jnp.mean(dyg * x, axis=-1, keepdims=True)
            dh_ref[...] = dh
            dhb_ref[...] = dh.astype(BF16)
            dg_ref[...] += jnp.sum(dxn * x * r, axis=0, keepdims=True)

    row_d = pl.BlockSpec((tb, d), lambda i, j: (i, 0))
    slab = pl.BlockSpec((1, tb, fc), lambda i, j: (j, i, 0))
    hidden = jax.ShapeDtypeStruct((nj, rows, fc), BF16)
    return pl.pallas_call(
        body, name=name, grid=(rows // tb, nj),
        in_specs=[row_d, pl.BlockSpec((1, d), lambda i, j: (0, 0)), row_d] + _ffn_specs(d, fc, nj),
        out_specs=[row_d, row_d, pl.BlockSpec((1, d), lambda i, j: (0, 0)), row_d, slab, slab, slab, row_d],
        out_shape=[jax.ShapeDtypeStruct((rows, d), F32), jax.ShapeDtypeStruct((rows, d), BF16),
                   jax.ShapeDtypeStruct((1, d), F32), jax.ShapeDtypeStruct((rows, d), BF16),
                   hidden, hidden, hidden, jax.ShapeDtypeStruct((rows, d), BF16)],
        scratch_shapes=[pltpu.VMEM((tb, d), F32)],
        compiler_params=_cp("arbitrary", "arbitrary"),
    )(h, gain, dout, wgu, wgu, wd)


def _shift_rows(x, s):
    return pltpu.roll(x, s % x.shape[0], 0)


def _a_post(c, which):
    s = _silu(c)
    n = s * lax.rsqrt(jnp.sum(s * s, axis=-1, keepdims=True) + 1e-6)
    scale = jnp.where(which == 0, HEAD_A ** -0.5, 1.0)
    return jnp.where(which == 2, s, n * scale)


def _conv(x, w):
    return x * w[3:4] + _shift_rows(x, 1) * w[2:3] + _shift_rows(x, 2) * w[1:2] + _shift_rows(x, 3) * w[0:1]


def _a_pre_fwd(zqkv, conv_w):
    rows, width = zqkv.shape
    heads = width // (3 * HEAD_A)

    def body(x_ref, w_ref, o_ref):
        which = pl.program_id(0) // heads
        live = lax.broadcasted_iota(jnp.int32, (rows, HEAD_A), 0) >= PAD
        o_ref[...] = jnp.where(live, _a_post(_conv(x_ref[...], w_ref[...]), which), 0.0)

    return pl.pallas_call(
        body, name="a_pre_fwd", grid=(width // HEAD_A,),
        in_specs=[pl.BlockSpec((rows, HEAD_A), lambda c: (0, c)), pl.BlockSpec((4, HEAD_A), lambda c: (0, c))],
        out_specs=pl.BlockSpec((rows, HEAD_A), lambda c: (0, c)),
        out_shape=jax.ShapeDtypeStruct((rows, width), F32),
        compiler_params=_cp("parallel"),
    )(zqkv, conv_w)


def _a_pre_bwd(zqkv, conv_w, dqkv):
    rows, width = zqkv.shape
    heads = width // (3 * HEAD_A)

    def body(x_ref, w_ref, ct_ref, dx_ref, dw_ref):
        which = pl.program_id(0) // heads
        live = lax.broadcasted_iota(jnp.int32, (rows, HEAD_A), 0) >= PAD
        x, w = x_ref[...], w_ref[...]
        _, vjp = jax.vjp(lambda c: _a_post(c, which), _conv(x, w))
        (dc,) = vjp(jnp.where(live, ct_ref[...], 0.0))
        dc = jnp.where(live, dc, 0.0)
        dx_ref[...] = (dc * w[3:4] + _shift_rows(dc, -1) * w[2:3] + _shift_rows(dc, -2) * w[1:2]
                       + _shift_rows(dc, -3) * w[0:1]).astype(BF16)
        dw_ref[...] = jnp.concatenate(
            [jnp.sum(dc * (_shift_rows(x, 3 - j) if j < 3 else x), axis=0, keepdims=True) for j in range(4)], axis=0)

    col = pl.BlockSpec((rows, HEAD_A), lambda c: (0, c))
    wsp = pl.BlockSpec((4, HEAD_A), lambda c: (0, c))
    return pl.pallas_call(
        body, name="a_pre_bwd", grid=(width // HEAD_A,),
        in_specs=[col, wsp, col], out_specs=[col, wsp],
        out_shape=[jax.ShapeDtypeStruct((rows, width), BF16), jax.ShapeDtypeStruct((4, width), F32)],
        compiler_params=_cp("parallel"),
    )(zqkv, conv_w, dqkv)


SHIFT_TILE = 384


def _shift_fwd(zb, mu):
    rows, width = zb.shape

    def body(x_ref, mu_ref, o_ref):
        x = x_ref[...]
        first = lax.broadcasted_iota(jnp.int32, x.shape, 0) == 0
        prev = jnp.where(first, 0.0, _shift_rows(x, 1))
        o_ref[...] = x + (prev - x) * mu_ref[...]

    col = pl.BlockSpec((rows, SHIFT_TILE), lambda c: (0, c))
    return pl.pallas_call(
        body, name="shift_fwd", grid=(width // SHIFT_TILE,),
        in_specs=[col, pl.BlockSpec((1, SHIFT_TILE), lambda c: (0, c))], out_specs=col,
        out_shape=jax.ShapeDtypeStruct((rows, width), F32), compiler_params=_cp("parallel"),
    )(zb, mu)


def _shift_bwd(zb, mu, dzf):
    rows, width = zb.shape

    def body(x_ref, mu_ref, ct_ref, dx_ref, dmu_ref):
        x, ct, mu_v = x_ref[...], ct_ref[...], mu_ref[...]
        row = lax.broadcasted_iota(jnp.int32, x.shape, 0)
        prev = jnp.where(row == 0, 0.0, _shift_rows(x, 1))
        nxt = jnp.where(row == rows - 1, 0.0, _shift_rows(ct, -1))
        dx_ref[...] = (ct * (1.0 - mu_v) + nxt * mu_v).astype(BF16)
        dmu_ref[...] = jnp.sum(ct * (prev - x), axis=0, keepdims=True)

    col = pl.BlockSpec((rows, SHIFT_TILE), lambda c: (0, c))
    msp = pl.BlockSpec((1, SHIFT_TILE), lambda c: (0, c))
    return pl.pallas_call(
        body, name="shift_bwd", grid=(width // SHIFT_TILE,),
        in_specs=[col, msp, col], out_specs=[col, msp],
        out_shape=[jax.ShapeDtypeStruct((rows, width), BF16), jax.ShapeDtypeStruct((1, width), F32)],
        compiler_params=_cp("parallel"),
    )(zb, mu, dzf)


def _neumann_inverse(p):
    heads = range(len(p))
    eye = (lax.broadcasted_iota(jnp.int32, (CHUNK, CHUNK), 0)
           == lax.broadcasted_iota(jnp.int32, (CHUNK, CHUNK), 1)).astype(F32)
    tinv = [eye + p[h] for h in heads]
    for _ in range(5):
        p = [_dot(p[h], p[h], DN_PRECISION) for h in heads]
        tinv = [tinv[h] + _dot(tinv[h], p[h], DN_PRECISION) for h in heads]
    return tinv


@jax.custom_vjp
def _unit_lower_inverse(p):
    return _neumann_inverse(p)


def _unit_lower_inverse_fwd(p):
    tinv = _neumann_inverse(p)
    return tinv, tinv


def _unit_lower_inverse_bwd(tinv, ct):
    heads = range(len(tinv))
    left = [_dot_tn(tinv[h], ct[h], DN_PRECISION) for h in heads]
    return ([_dot_nt(left[h], tinv[h], DN_PRECISION) for h in heads],)


_unit_lower_inverse.defvjp(_unit_lower_inverse_fwd, _unit_lower_inverse_bwd)


@jax.custom_vjp
def _known_inverse(p, tinv):
    return tinv


_known_inverse.defvjp(lambda p, tinv: (tinv, tinv),
                      lambda tinv, ct: (_unit_lower_inverse_bwd(tinv, ct)[0], [jnp.zeros_like(t) for t in tinv]))


def _dn_chunk(q, k, v, beta, g, state, saved_tinv=None):
    heads = range(len(q))
    ri = lax.broadcasted_iota(jnp.int32, (CHUNK, CHUNK), 0)
    ci = lax.broadcasted_iota(jnp.int32, (CHUNK, CHUNK), 1)
    eye = (ri == ci).astype(F32)
    incl = ri >= ci
    last = lax.broadcasted_iota(jnp.int32, (CHUNK, 1), 0) == CHUNK - 1
    g_row = [jnp.sum(g[h] * eye, axis=0, keepdims=True) for h in heads]
    gc = [jnp.sum(jnp.where(incl, g_row[h], 0.0), axis=1, keepdims=True) for h in heads]
    gc_row = [jnp.sum(gc[h] * eye, axis=0, keepdims=True) for h in heads]
    decay = [jnp.where(incl, jnp.exp(jnp.where(incl, gc[h] - gc_row[h], 0.0)), 0.0) for h in heads]
    kb = [k[h] * beta[h] for h in heads]
    vb = [v[h] * beta[h] for h in heads]
    p = [-jnp.where(ri > ci, _dot_nt(kb[h], k[h]) * decay[h], 0.0) for h in heads]
    tinv = _unit_lower_inverse(p) if saved_tinv is None else _known_inverse(p, saved_tinv)
    eg = [jnp.exp(gc[h]) for h in heads]
    u = [_dot(tinv[h], vb[h]) for h in heads]
    wk = [_dot(tinv[h], kb[h] * eg[h]) for h in heads]
    attn = [_dot_nt(q[h], k[h]) * decay[h] for h in heads]
    g_last = [jnp.sum(jnp.where(last, gc[h], 0.0), axis=0, keepdims=True) for h in heads]
    k_tail = [k[h] * jnp.exp(g_last[h] - gc[h]) for h in heads]
    v_new = [u[h] - _dot(wk[h], state[h]) for h in heads]
    o = [_dot(q[h] * eg[h], state[h]) + _dot(attn[h], v_new[h]) for h in heads]
    new = [state[h] * jnp.exp(g_last[h]) + _dot_tn(k_tail[h], v_new[h]) for h in heads]
    return (o, new, tinv) if saved_tinv is None else (o, new)


def _bg_cols(bg, h, heads):
    lane = lax.broadcasted_iota(jnp.int32, bg.shape, 1)
    beta = jnp.sum(jnp.where(lane == h, bg, 0.0), axis=1, keepdims=True)
    g = jnp.sum(jnp.where(lane == heads + h, bg, 0.0), axis=1, keepdims=True)
    return beta, g


def _dn_fwd(qkv, bg):
    rows = qkv.shape[0]
    heads = qkv.shape[1] // (3 * HEAD_A)
    n = rows // CHUNK
    hp, groups = heads, 1

    def body(q_ref, k_ref, v_ref, bg_ref, o_ref, hist_ref, tinv_ref, s_ref):
        c, grp = pl.program_id(0), pl.program_id(1)

        @pl.when(c == 0)
        def _():
            for i in range(hp):
                s_ref[grp * hp + i] = jnp.zeros((HEAD_A, HEAD_A), F32)

        bg_v = bg_ref[...]
        cols = [slice(i * HEAD_A, (i + 1) * HEAD_A) for i in range(hp)]
        state = [s_ref[grp * hp + i] for i in range(hp)]
        beta_g = [_bg_cols(bg_v, grp * hp + i, heads) for i in range(hp)]
        o, new, tinv = _dn_chunk([q_ref[:, c_] for c_ in cols], [k_ref[:, c_] for c_ in cols],
                                 [v_ref[:, c_] for c_ in cols], [b for b, _ in beta_g], [g for _, g in beta_g], state)
        for i in range(hp):
            hist_ref[0, i] = state[i]
            tinv_ref[0, i] = tinv[i]
            o_ref[:, cols[i]] = o[i]
            s_ref[grp * hp + i] = new[i]

    def part(p):
        return pl.BlockSpec((CHUNK, hp * HEAD_A), lambda c, grp: (c, p * groups + grp))

    return pl.pallas_call(
        body, name="deltanet_fwd", grid=(n, groups),
        in_specs=[part(0), part(1), part(2), pl.BlockSpec((CHUNK, 128), lambda c, grp: (c, 0))],
        out_specs=[part(0), pl.BlockSpec((1, hp, HEAD_A, HEAD_A), lambda c, grp: (c, grp, 0, 0)),
                   pl.BlockSpec((1, hp, CHUNK, CHUNK), lambda c, grp: (c, grp, 0, 0))],
        out_shape=[jax.ShapeDtypeStruct((rows, heads * HEAD_A), F32),
                   jax.ShapeDtypeStruct((n, heads, HEAD_A, HEAD_A), F32),
                   jax.ShapeDtypeStruct((n, heads, CHUNK, CHUNK), F32)],
        scratch_shapes=[pltpu.VMEM((heads, HEAD_A, HEAD_A), F32)],
        compiler_params=_cp("arbitrary", "arbitrary"),
    )(qkv, qkv, qkv, bg)


def _dn_bwd(qkv, bg, hist, tinv_hist, do):
    rows = qkv.shape[0]
    heads = qkv.shape[1] // (3 * HEAD_A)
    n = rows // CHUNK
    hp, groups = heads, 1

    def body(q_ref, k_ref, v_ref, bg_ref, hist_ref, tinv_ref, do_ref, dqkv_ref, dbg_ref, ds_ref):
        c, grp = pl.program_id(0), pl.program_id(1)

        @pl.when(c == 0)
        def _():
            for i in range(hp):
                ds_ref[grp * hp + i] = jnp.zeros((HEAD_A, HEAD_A), F32)

        bg_v = bg_ref[...]
        lane = lax.broadcasted_iota(jnp.int32, (CHUNK, 128), 1)
        cols = [slice(i * HEAD_A, (i + 1) * HEAD_A) for i in range(hp)]
        beta_g = [_bg_cols(bg_v, grp * hp + i, heads) for i in range(hp)]
        _, vjp = jax.vjp(_dn_chunk, [q_ref[:, c_] for c_ in cols], [k_ref[:, c_] for c_ in cols],
                         [v_ref[:, c_] for c_ in cols], [b for b, _ in beta_g], [g for _, g in beta_g],
                         [hist_ref[0, i] for i in range(hp)], [tinv_ref[0, i] for i in range(hp)])
        dq, dk, dv, dbeta, dg, ds, _ = vjp(([do_ref[:, c_] for c_ in cols], [ds_ref[grp * hp + i] for i in range(hp)]))
        dbg = jnp.zeros((CHUNK, 128), F32)
        for i in range(hp):
            h = grp * hp + i
            for p, part_grad in enumerate((dq, dk, dv)):
                dqkv_ref[:, pl.ds((p * heads + i) * HEAD_A, HEAD_A)] = part_grad[i]
            ds_ref[h] = ds[i]
            dbg = dbg + jnp.where(lane == h, dbeta[i], 0.0) + jnp.where(lane == heads + h, dg[i], 0.0)

        @pl.when(grp == 0)
        def _():
            dbg_ref[...] = jnp.zeros_like(dbg_ref)

        dbg_ref[...] += dbg

    def part(p):
        return pl.BlockSpec((CHUNK, hp * HEAD_A), lambda c, grp: (n - 1 - c, p * groups + grp))

    return pl.pallas_call(
        body, name="deltanet_bwd", grid=(n, groups),
        in_specs=[part(0), part(1), part(2), pl.BlockSpec((CHUNK, 128), lambda c, grp: (n - 1 - c, 0)),
                  pl.BlockSpec((1, hp, HEAD_A, HEAD_A), lambda c, grp: (n - 1 - c, grp, 0, 0)),
                  pl.BlockSpec((1, hp, CHUNK, CHUNK), lambda c, grp: (n - 1 - c, grp, 0, 0)), part(0)],
        out_specs=[pl.BlockSpec((CHUNK, 3 * heads * HEAD_A), lambda c, grp: (n - 1 - c, 0)),
                   pl.BlockSpec((CHUNK, 128), lambda c, grp: (n - 1 - c, 0))],
        out_shape=[jax.ShapeDtypeStruct(qkv.shape, F32), jax.ShapeDtypeStruct((rows, 128), F32)],
        scratch_shapes=[pltpu.VMEM((heads, HEAD_A, HEAD_A), F32)],
        compiler_params=_cp("arbitrary", "arbitrary"),
    )(qkv, qkv, qkv, bg, hist, tinv_hist, do)


def _head_mask(heads, width):
    return (lax.broadcasted_iota(jnp.int32, (heads, width), 0)
            == lax.broadcasted_iota(jnp.int32, (heads, width), 1) // HEAD_B)


def _masked_rows(mask, row):
    return jnp.where(mask, row, 0.0).astype(BF16)


def _rwkv_fwd(r, w, k, v, a, b):
    rows, width = r.shape
    heads = width // HEAD_B
    ts = SCAN_STEPS

    def body(r_ref, w_ref, k_ref, v_ref, a_ref, b_ref, y_ref, hist_ref, s_ref):
        @pl.when(pl.program_id(0) == 0)
        def _():
            s_ref[...] = jnp.zeros_like(s_ref)

        mask = _head_mask(heads, width)
        onehot = mask.astype(BF16)
        onehot2 = jnp.concatenate([onehot, onehot], axis=0)
        bd = _block_diag_ones()

        spread_v = [_dot_tn(jnp.concatenate(_hi_lo(v_ref[j]), axis=0), onehot2) for j in range(ts)]
        a_next = pltpu.roll(a_ref[...], ts - 1, 0)
        b_dot_a, k_dot_a = _segsum_many([b_ref[...] * a_next, k_ref[...] * a_next], bd)
        s = s_ref[...]
        ys = []
        for j in range(0, ts, 2):
            row, nxt = pl.ds(j, 1), pl.ds(j + 1, 1)
            hist_ref[j] = s
            sa, base = _segsum_many([((s * a_ref[row, :]).astype(BF16),),
                                     ((s * (w_ref[row, :] * a_ref[nxt, :])).astype(BF16),)], bd)
            sa_next = base + sa * b_dot_a[j:j + 1] + spread_v[j] * k_dot_a[j:j + 1]
            s = s * w_ref[row, :] + sa * b_ref[row, :] + spread_v[j] * k_ref[row, :]
            hist_ref[j + 1] = s
            ys.append(_dot_nt(_masked_rows(mask, r_ref[row, :]), s.astype(BF16)))
            s = s * w_ref[nxt, :] + sa_next * b_ref[nxt, :] + spread_v[j + 1] * k_ref[nxt, :]
            ys.append(_dot_nt(_masked_rows(mask, r_ref[nxt, :]), s.astype(BF16)))
        for j in range(ts):
            y_ref[j] = ys[j]
        s_ref[...] = s

    blk = pl.BlockSpec((ts, width), lambda i: (i, 0))
    blk3 = pl.BlockSpec((ts, heads, HEAD_B), lambda i: (i, 0, 0))
    return pl.pallas_call(
        body, name="rwkv_fwd", grid=(rows // ts,),
        in_specs=[blk, blk, blk, blk3, blk, blk],
        out_specs=[blk3, pl.BlockSpec((ts, HEAD_B, width), lambda i: (i, 0, 0)),
                   pl.BlockSpec((HEAD_B, width), lambda i: (0, 0))],
        out_shape=[jax.ShapeDtypeStruct((rows, heads, HEAD_B), F32), jax.ShapeDtypeStruct((rows, HEAD_B, width), F32),
                   jax.ShapeDtypeStruct((HEAD_B, width), F32)],
        compiler_params=_cp("arbitrary"),
    )(r, w, k, v, a, b)


def _rwkv_bwd(r, w, k, v, a, b, hist, last, dy):
    rows, width = r.shape
    heads = width // HEAD_B
    ts = SCAN_STEPS
    nb = rows // ts

    def body(r_ref, w_ref, k_ref, v_ref, a_ref, b_ref, hist_ref, last_ref, dy_ref,
             dr_ref, dw_ref, dk_ref, dv_ref, da_ref, db_ref, g_ref, after_ref):
        @pl.when(pl.program_id(0) == 0)
        def _():
            g_ref[...] = jnp.zeros_like(g_ref)
            after_ref[...] = last_ref[...]

        mask = _head_mask(heads, width)
        onehot = mask.astype(BF16)
        bd = _block_diag_ones()

        def own_lanes(x):
            return jnp.sum(jnp.where(mask, x, 0.0), axis=0, keepdims=True)

        def colsum(x):
            return jnp.sum(x, axis=0, keepdims=True)

        dy_m = [dy_ref[j].astype(BF16) for j in range(ts)]
        spread_dy = [_dot_tn(dy_m[j], onehot) for j in range(ts)]
        state_after = [hist_ref[j + 1] if j < ts - 1 else after_ref[...] for j in range(ts)]
        dr = [own_lanes(_dot(dy_m[j], state_after[j].astype(BF16))) for j in range(ts)]
        sa_m = [_dot_nt(_masked_rows(mask, a_ref[pl.ds(j, 1), :]), hist_ref[j].astype(BF16)) for j in range(ts)]
        g = g_ref[...]
        dw, dk, db, da, dv = {}, {}, {}, {}, {}
        for j in reversed(range(ts)):
            row = pl.ds(j, 1)
            sp = hist_ref[j]
            g = g + spread_dy[j] * r_ref[row, :]
            (dsa,) = _segsum_many([((g * b_ref[row, :]).astype(BF16),)], bd)
            g_b = g.astype(BF16)
            both = _dot(jnp.concatenate([v_ref[j].astype(BF16), sa_m[j].astype(BF16)], axis=0), g_b)
            dk[j], db[j] = own_lanes(both[:heads]), own_lanes(both[heads:])
            dv[j] = _dot_nt(_masked_rows(mask, k_ref[row, :]), g_b)
            dw[j] = colsum(g * sp)
            da[j] = colsum(sp * dsa)
            g = g * w_ref[row, :] + dsa * a_ref[row, :]
        g_ref[...] = g
        after_ref[...] = hist_ref[0]
        for j in range(ts):
            dv_ref[j] = dv[j]
            for ref, vals in ((dr_ref, dr), (dw_ref, dw), (dk_ref, dk), (da_ref, da), (db_ref, db)):
                ref[pl.ds(j, 1), :] = vals[j]

    blk = pl.BlockSpec((ts, width), lambda i: (nb - 1 - i, 0))
    blk3 = pl.BlockSpec((ts, heads, HEAD_B), lambda i: (nb - 1 - i, 0, 0))
    state = pl.BlockSpec((HEAD_B, width), lambda i: (0, 0))
    return pl.pallas_call(
        body, name="rwkv_bwd", grid=(nb,),
        in_specs=[blk, blk, blk, blk3, blk, blk, pl.BlockSpec((ts, HEAD_B, width), lambda i: (nb - 1 - i, 0, 0)),
                  state, blk3],
        out_specs=[blk, blk, blk, blk3, blk, blk],
        out_shape=[jax.ShapeDtypeStruct((rows, width), F32)] * 3 + [jax.ShapeDtypeStruct((rows, heads, HEAD_B), F32)]
        + [jax.ShapeDtypeStruct((rows, width), F32)] * 2,
        scratch_shapes=[pltpu.VMEM((HEAD_B, width), F32), pltpu.VMEM((HEAD_B, width), F32)],
        compiler_params=_cp("arbitrary"),
    )(r, w, k, v, a, b, hist, last, dy)


def _live(row0, shape):
    return (row0 + lax.broadcasted_iota(jnp.int32, shape, 0)) >= PAD


def _norm_fn(row0, h, gain):
    return (_rms(h, gain),)


def _norm_res_fn(row0, h, gain):
    return _rms(h, gain), h


def _make_bg_fn(heads):
    def fn(row0, x, log_rate, dt_bias):
        lane = lax.broadcasted_iota(jnp.int32, x.shape, 1)
        beta = _sigmoid(x)
        g = -jnp.exp(log_rate) * _softplus(x + dt_bias)
        out = jnp.where(lane < heads, beta, jnp.where(lane < 2 * heads, g, 0.0))
        return (jnp.where(_live(row0, x.shape), out, 0.0),)
    return fn


def _b_pre_fn(row0, zf, w0, w_up, a0, a_up, g_up, k_k, k_a):
    d = w0.shape[1]
    r, k, v = zf[:, :d], zf[:, d:2 * d], zf[:, 2 * d:3 * d]
    lo = zf[:, 3 * d:3 * d + 128]
    lg = zf[:, 3 * d + 128:3 * d + LORA_PAD]
    lane = lax.broadcasted_iota(jnp.int32, lo.shape, 1)
    lw = _dot(jnp.where(lane < LORA_W, jnp.tanh(lo), 0.0), w_up)
    la = _dot(jnp.where(lane >= LORA_W, lo, 0.0), a_up)
    lane_g = lax.broadcasted_iota(jnp.int32, lg.shape, 1)
    gate = _dot(jnp.where(lane_g < LORA_G, _sigmoid(lg), 0.0), g_up)
    decay = jnp.exp(-jnp.exp(-_softplus(-(w0 + lw)) - 0.5))
    a = _sigmoid(a0 + la)
    kx = k * k_k
    kk = kx * lax.rsqrt(_segsum64(kx * kx) + 1e-6)
    k2 = k * (1.0 + (a - 1.0) * k_a)
    return r, decay, k2, v, -kk, kk * a, gate


def _post_fn(row0, o, zg, y, r, k2, v, gate, out_gain, r_k, ln_g, ln_b):
    d = o.shape[1]
    az, ga, gb = zg[:, :d], zg[:, d:2 * d], zg[:, 2 * d:]
    heads = d // HEAD_A
    parts = []
    for h in range(heads):
        oh = o[:, h * HEAD_A:(h + 1) * HEAD_A]
        parts.append(oh * lax.rsqrt(jnp.mean(oh * oh, axis=-1, keepdims=True) + EPS) * out_gain)
    o_a = jnp.concatenate(parts, axis=1) * _silu(az)
    mean = _segsum64(y) * (1.0 / HEAD_B)
    yc = y - mean
    var = _segsum64(yc * yc) * (1.0 / HEAD_B)
    yn = yc * lax.rsqrt(var + GN_EPS) * ln_g + ln_b
    o_b = (yn + _segsum64(r * k2 * r_k) * v) * gate
    return (_sigmoid(ga) * o_a + _sigmoid(gb) * o_b,)


def _loss(h3, target, gain, tb):
    rows, d = h3.shape

    def body(h_ref, t_ref, g_ref, dh_ref, dg_ref, l_ref):
        i = pl.program_id(0)
        live = (i * tb + lax.broadcasted_iota(jnp.int32, (tb, 1), 0)) >= CHUNK
        tgt = t_ref[...]

        def f(h, g):
            err = _rms(h, g) - tgt
            return 0.5 * jnp.sum(jnp.where(live, jnp.mean(err * err, axis=-1, keepdims=True), 0.0))

        val, vjp = jax.vjp(f, h_ref[...], g_ref[...])
        dh, dg = vjp(jnp.ones((), F32))
        dh_ref[...] = dh

        @pl.when(i == 0)
        def _():
            dg_ref[...] = jnp.zeros_like(dg_ref)
            l_ref[...] = jnp.zeros_like(l_ref)

        dg_ref[...] += dg
        l_ref[...] += jnp.full((1, 128), val, F32)

    blk = pl.BlockSpec((tb, d), lambda i: (i, 0))
    return pl.pallas_call(
        body, name="loss", grid=(rows // tb,),
        in_specs=[blk, blk, pl.BlockSpec((1, d), lambda i: (0, 0))],
        out_specs=[blk, pl.BlockSpec((1, d), lambda i: (0, 0)), pl.BlockSpec((1, 128), lambda i: (0, 0))],
        out_shape=[jax.ShapeDtypeStruct((rows, d), F32), jax.ShapeDtypeStruct((1, d), F32),
                   jax.ShapeDtypeStruct((1, 128), F32)],
        compiler_params=_cp("arbitrary"),
    )(h3, target, gain)


def _adamw_math(w, g, m, v):
    m2 = ADAM_B1 * m + (1.0 - ADAM_B1) * g
    v2 = ADAM_B2 * v + (1.0 - ADAM_B2) * (g * g)
    m_hat = m2 / (1.0 - ADAM_B1 ** ADAM_STEP)
    v_hat = v2 / (1.0 - ADAM_B2 ** ADAM_STEP)
    return -ADAM_LR * (m_hat / (jnp.sqrt(v_hat) + ADAM_EPS) + ADAM_WD * w), m2, v2


def _adamw(name, own, landed, w, m, v):
    rows, cols = w.shape
    if rows % 16 == 0:
        rb = _tb(rows, 128)
        grid, blk = (rows // rb,), pl.BlockSpec((rb, cols), lambda i: (i, 0))
        landed_blk = pl.BlockSpec((N_DEV - 1, rb, cols), lambda i: (0, i, 0))
    else:
        grid, blk = (cols // 128,), pl.BlockSpec((rows, 128), lambda i: (0, i))
        landed_blk = pl.BlockSpec((N_DEV - 1, rows, 128), lambda i: (0, 0, i))

    def body(o_ref, s_ref, w_ref, m_ref, v_ref, g_ref, d_ref, m2_ref, v2_ref):
        g = o_ref[...].astype(F32)
        for peer in range(N_DEV - 1):
            g = g + s_ref[peer].astype(F32)
        g_ref[...] = g
        d_ref[...], m2_ref[...], v2_ref[...] = _adamw_math(w_ref[...], g, m_ref[...], v_ref[...])

    return pl.pallas_call(
        body, name=name, grid=grid, in_specs=[blk, landed_blk, blk, blk, blk],
        out_specs=[blk] * 4, out_shape=[jax.ShapeDtypeStruct((rows, cols), F32)] * 4,
        compiler_params=_cp("parallel"),
    )(own, landed, w, m, v)


def _sum_slabs(name, slabs, rb):
    _, rows, cols = slabs.shape

    def body(s_ref, o_ref):
        g = s_ref[0]
        for dev in range(1, N_DEV):
            g = g + s_ref[dev]
        o_ref[...] = g

    return pl.pallas_call(
        body, name=name, grid=(rows // rb,),
        in_specs=[pl.BlockSpec((N_DEV, rb, cols), lambda i: (0, i, 0))],
        out_specs=pl.BlockSpec((rb, cols), lambda i: (i, 0)),
        out_shape=jax.ShapeDtypeStruct((rows, cols), F32), compiler_params=_cp("parallel"),
    )(slabs)


def _adamw_small(w, g, m, v):
    def body(w_ref, g_ref, m_ref, v_ref, d_ref, m2_ref, v2_ref):
        d_ref[...], m2_ref[...], v2_ref[...] = _adamw_math(w_ref[...], g_ref[...], m_ref[...], v_ref[...])

    return pl.pallas_call(body, name="adamw_small", out_shape=[jax.ShapeDtypeStruct(w.shape, F32)] * 3)(w, g, m, v)


def _place():
    return lax.axis_index("x"), lax.axis_index("y"), lax.axis_index("c")


def _index(p):
    return 4 * p[0] + 2 * p[1] + p[2]


def _all_gather(name, xs):
    n = len(xs)

    def body(*refs):
        x_refs, o_refs = refs[:n], refs[n:2 * n]
        send_sems, recv_sems, local_sems = refs[2 * n:]
        x, y, c = _place()
        me, sibling = (x, y, c), (x, y, 1 - c)
        chips = [(1 - x, y), (x, 1 - y), (1 - x, 1 - y)]

        def copy(i, k, block, to, src=None):
            dst = o_refs[i].at[_index(block)]
            return pltpu.make_async_remote_copy(src_ref=dst if src is None else src, dst_ref=dst,
                                                send_sem=send_sems.at[i, k], recv_sem=recv_sems.at[i, k],
                                                device_id=to, device_id_type=MESH_ID)

        mine = [pltpu.make_async_copy(x_refs[i], o_refs[i].at[_index(me)], local_sems.at[i]) for i in range(n)]
        for cp in mine:
            cp.start()
        first = []
        for i in range(n):
            first.append(copy(i, 0, me, sibling, src=x_refs[i]))
            first += [copy(i, 1 + j, me, (*chip, c), src=x_refs[i]) for j, chip in enumerate(chips)]
        for cp in first:
            cp.start()
        passed = []
        for j, chip in enumerate(chips):
            for i in range(n):
                copy(i, 1 + j, (*chip, c), me).wait_recv()
                cp = copy(i, 4 + j, (*chip, c), sibling)
                cp.start()
                passed.append(cp)
        for i in range(n):
            copy(i, 0, sibling, me).wait_recv()
            for j, chip in enumerate(chips):
                copy(i, 4 + j, (*chip, 1 - c), me).wait_recv()
        for cp in first + passed:
            cp.wait_send()
        for cp in mine:
            cp.wait()

    return pl.pallas_call(
        body, name=name, in_specs=[ANY] * n, out_specs=[ANY] * n,
        out_shape=[jax.ShapeDtypeStruct((N_DEV,) + x.shape, x.dtype) for x in xs],
        scratch_shapes=[pltpu.SemaphoreType.DMA((n, 7)), pltpu.SemaphoreType.DMA((n, 7)), pltpu.SemaphoreType.DMA((n,))],
    )(*xs)


def _exchange_start(name, xs, after=None, gather=False):
    n = len(xs)
    copies = n * (N_DEV - 1)
    extra = [] if after is None else [after]

    def body(*refs):
        x_refs, land_refs = refs[:n], refs[n:2 * n]
        sems = refs[2 * n + len(extra):2 * n + len(extra) + 2 * copies]
        token = refs[-1]
        for i, k, peer in _exchange_copies(n):
            _exchange_copy(x_refs, land_refs, sems, i, k, peer, gather).start()
        token[...] = jnp.zeros_like(token)

    lands = [lax.empty((N_DEV,) + x.shape if gather else (N_DEV - 1,) + x.shape[1:], x.dtype) for x in xs]
    out = pl.pallas_call(
        body, name=name,
        out_shape=(*[pltpu.SemaphoreType.DMA(())] * (2 * copies), *[pltpu.HBM(x.shape, x.dtype) for x in xs],
                   *[pltpu.HBM(l.shape, l.dtype) for l in lands], jax.ShapeDtypeStruct((8, 128), F32)),
        in_specs=[HBM_SPEC] * (2 * n) + [ANY] * len(extra),
        out_specs=(*[SEM_SPEC] * (2 * copies), *[HBM_SPEC] * (2 * n), pl.BlockSpec(memory_space=pltpu.VMEM)),
        input_output_aliases={i: 2 * copies + i for i in range(2 * n)},
        compiler_params=pltpu.CompilerParams(has_side_effects=pltpu.SideEffectType.DATAFLOW_SIDE_EFFECTING),
    )(*[pltpu.with_memory_space_constraint(a, pltpu.HBM) for a in list(xs) + lands], *extra)
    sems, rest = list(out[:2 * copies]), out[2 * copies:]
    return sems, list(rest[:n]), list(rest[n:2 * n]), rest[-1]


def _exchange_copies(n):
    x, y, c = _place()
    for k in range(1, N_DEV):
        peer = ((1 - x) if k & 4 else x, (1 - y) if k & 2 else y, (1 - c) if k & 1 else c)
        for i in range(n):
            yield i, k - 1, peer


def _exchange_copy(x_refs, land_refs, sems, i, k, peer, gather, arriving=False):
    copies = len(sems) // 2
    which = i * (N_DEV - 1) + k
    src = x_refs[i] if gather else x_refs[i].at[_index(peer)]
    dst = land_refs[i].at[_index(peer if arriving else _place())] if gather else land_refs[i].at[k]
    return pltpu.make_async_remote_copy(src_ref=src, dst_ref=dst, send_sem=sems[which], recv_sem=sems[copies + which],
                                        device_id=peer, device_id_type=MESH_ID)


def _exchange_wait(name, sems, xs, lands, after, gather=False):
    n = len(xs)

    def body(*refs):
        x_refs, land_refs = refs[:n], refs[n:2 * n]
        sem_refs = refs[2 * n:2 * n + len(sems)]
        for i, k, peer in _exchange_copies(n):
            _exchange_copy(x_refs, land_refs, sem_refs, i, k, peer, gather).wait_send()
            _exchange_copy(x_refs, land_refs, sem_refs, i, k, peer, gather, arriving=True).wait_recv()

    out = pl.pallas_call(
        body, name=name,
        out_shape=(*[pltpu.HBM(x.shape, x.dtype) for x in xs], *[pltpu.HBM(l.shape, l.dtype) for l in lands]),
        in_specs=[HBM_SPEC] * (2 * n) + [SEM_SPEC] * len(sems) + [ANY], out_specs=tuple([HBM_SPEC] * (2 * n)),
        input_output_aliases={i: i for i in range(2 * n)},
        compiler_params=pltpu.CompilerParams(has_side_effects=pltpu.SideEffectType.DATAFLOW_SIDE_EFFECTING),
    )(*xs, *lands, *sems, after)
    return list(out[:n]), list(out[n:])


def _pack(arrays):
    flat = jnp.concatenate([a.reshape(-1) for a in arrays])
    pad = (-flat.shape[0]) % 1024
    return jnp.pad(flat, (0, pad)).reshape(-1, 128)


def _unpack(packed, shapes):
    flat = packed.reshape(-1)
    out, pos = [], 0
    for s in shapes:
        size = 1
        for dim in s:
            size *= dim
        out.append(flat[pos:pos + size].reshape(s))
        pos += size
    return out


def _cols_from_slabs(stack):
    return jnp.transpose(stack, (1, 0, 2)).reshape(stack.shape[1], -1)


def kernel(x, meta_tokens, ffn1_norm, ffn1_w_gu, ffn1_w_down, mix_norm, w_in, a_conv_w, a_log_rate, a_dt_bias, a_out_norm, b_shift_mu, b_w0, b_w_up, b_a0, b_a_up, b_g_up, b_k_k, b_k_a, b_r_k, b_ln_gain, b_ln_bias, w_out, ffn2_norm, ffn2_w_gu, ffn2_w_down, final_norm, loss_target, m_meta_tokens, m_ffn1_norm, m_ffn1_w_gu, m_ffn1_w_down, m_mix_norm, m_w_in, m_a_conv_w, m_a_log_rate, m_a_dt_bias, m_a_out_norm, m_b_shift_mu, m_b_w0, m_b_w_up, m_b_a0, m_b_a_up, m_b_g_up, m_b_k_k, m_b_k_a, m_b_r_k, m_b_ln_gain, m_b_ln_bias, m_w_out, m_ffn2_norm, m_ffn2_w_gu, m_ffn2_w_down, m_final_norm, v_meta_tokens, v_ffn1_norm, v_ffn1_w_gu, v_ffn1_w_down, v_mix_norm, v_w_in, v_a_conv_w, v_a_log_rate, v_a_dt_bias, v_a_out_norm, v_b_shift_mu, v_b_w0, v_b_w_up, v_b_a0, v_b_a_up, v_b_g_up, v_b_k_k, v_b_k_a, v_b_r_k, v_b_ln_gain, v_b_ln_bias, v_w_out, v_ffn2_norm, v_ffn2_w_gu, v_ffn2_w_down, v_final_norm):
    names = ['meta_tokens', 'ffn1_norm', 'ffn1_w_gu', 'ffn1_w_down', 'mix_norm', 'w_in', 'a_conv_w', 'a_log_rate',
             'a_dt_bias', 'a_out_norm', 'b_shift_mu', 'b_w0', 'b_w_up', 'b_a0', 'b_a_up', 'b_g_up', 'b_k_k', 'b_k_a',
             'b_r_k', 'b_ln_gain', 'b_ln_bias', 'w_out', 'ffn2_norm', 'ffn2_w_gu', 'ffn2_w_down', 'final_norm']
    env = dict(locals())
    wts = {k: env[k] for k in names}
    mom_m = {k: env['m_' + k] for k in names}
    mom_v = {k: env['v_' + k] for k in names}
    big = ['ffn1_w_gu', 'ffn1_w_down', 'w_in', 'w_out', 'ffn2_w_gu', 'ffn2_w_down']
    col_sharded = {'ffn1_w_gu', 'w_in', 'ffn2_w_gu'}
    shard_of = lambda tree, k: tree[k][0].T if k in col_sharded else tree[k][0]
    small_sharded = ['meta_tokens', 'a_conv_w', 'b_w_up', 'b_a_up', 'b_g_up']
    replicated = [k for k in names if k not in big and k not in small_sharded]

    seq, d = x.shape[1], x.shape[2]
    rows = PAD + N_META + seq
    heads_a = d // HEAD_A
    tb_mm = _tb(rows, 416)
    tb_vjp = _tb(rows, 208)
    tb_dw = _tb(rows, 2080)
    tb_ffn = _tb(rows, 832)
    me = _index(_place())

    local_bf = {k: shard_of(wts, k).astype(BF16) for k in big}
    gu1, down1, meta = _all_gather("gather_ffn1", [local_bf['ffn1_w_gu'], local_bf['ffn1_w_down'], wts['meta_tokens']])
    small_rest = small_sharded[1:]
    late_keys = ['w_out', 'ffn2_w_gu', 'ffn2_w_down']
    gather_mid = _exchange_start("gather_start_mid", [local_bf['w_in'], _pack([wts[k][0] for k in small_rest])],
                                 after=gu1, gather=True)
    gather_late = _exchange_start("gather_start_late", [local_bf[k] for k in late_keys], after=gather_mid[-1],
                                  gather=True)

    def finish_gather(tag, started, after):
        sems, mine, lands, _ = started
        mine, lands = _exchange_wait("gather_wait_" + tag, sems, mine, lands, after, gather=True)
        return [lax.dynamic_update_index_in_dim(land, own[None], me, 0) for land, own in zip(lands, mine)]

    full = {'ffn1_w_gu': gu1, 'ffn1_w_down': down1.reshape(-1, d), 'meta_tokens': _cols_from_slabs(meta)}
    for k in replicated:
        full[k] = wts[k].reshape(1, -1)

    h0 = jnp.concatenate([jnp.zeros((PAD, d), F32) + gather_late[-1][:1, :1], full['meta_tokens'], x[0]], axis=0)
    h1 = _ffn_fwd("ffn1_fwd", h0, full['ffn1_norm'], full['ffn1_w_gu'], full['ffn1_w_down'], tb_ffn)
    (u,) = _tok_fwd("mix_norm_fwd", _norm_fn, [h1], [full['mix_norm']], [(d, BF16)], tb_mm)

    win_stack, small_stack = finish_gather("mid", gather_mid, u)
    full['w_in'] = win_stack.reshape(-1, d)
    small_flat, pos = small_stack.reshape(N_DEV, -1), 0
    for k in small_rest:
        shape = wts[k][0].shape
        full[k] = _cols_from_slabs(small_flat[:, pos:pos + shape[0] * shape[1]].reshape((N_DEV,) + shape))
        pos += shape[0] * shape[1]

    win = full['w_in']
    n_b = 3 * d + LORA_W + LORA_A + LORA_G
    off_beta, off_b = 4 * d, 4 * d + 2 * heads_a
    off_ga = off_b + n_b
    b_width = 3 * d + LORA_PAD
    zrows = lambda r: jnp.zeros((r, d), BF16)
    w_qkv = win[:3 * d]
    w_zg = jnp.concatenate([win[3 * d:4 * d], win[off_ga:off_ga + 2 * d]], axis=0)
    w_b = jnp.concatenate([win[off_b:off_b + n_b], zrows(b_width - n_b)], axis=0)
    w_bg = jnp.concatenate([win[off_beta:off_beta + 2 * heads_a], zrows(128 - 2 * heads_a)], axis=0)

    def lanes(vec, start, width):
        return jnp.pad(vec.reshape(1, -1), ((0, 0), (start, width - start - vec.size)))

    log_rate = lanes(wts['a_log_rate'], heads_a, 128)
    dt_bias = lanes(wts['a_dt_bias'], heads_a, 128)
    mu = lanes(wts['b_shift_mu'], 0, b_width)
    w_up = jnp.pad(full['b_w_up'], ((0, 128 - LORA_W), (0, 0)))
    a_up = jnp.pad(full['b_a_up'], ((LORA_W, 0), (0, 0)))
    g_up = jnp.pad(full['b_g_up'], ((0, 256 - LORA_G), (0, 0)))
    b_pars = [full['b_w0'], w_up, full['b_a0'], a_up, g_up, full['b_k_k'], full['b_k_a']]
    post_pars = [full['a_out_norm'], full['b_r_k'], full['b_ln_gain'], full['b_ln_bias']]
    bg_fn = _make_bg_fn(heads_a)

    z_qkv = _mm("in_qkv", u, w_qkv, trans_b=True, tb=tb_mm, tn=_col_tile(3 * d, 1536))
    z_zg = _mm("in_zg", u, w_zg, trans_b=True, tb=tb_mm, tn=_col_tile(3 * d, 1536))
    z_b = _mm("in_b", u, w_b, trans_b=True, tb=tb_mm, tn=_col_tile(b_width, 1536))
    z_bg = _mm("in_bg", u, w_bg, trans_b=True, tb=tb_mm, tn=128)
    qkv = _a_pre_fwd(z_qkv, full['a_conv_w'])
    (bg,) = _tok_fwd("bg_fwd", bg_fn, [z_bg], [log_rate, dt_bias], [(128, F32)], tb_mm)
    o_dn, dn_hist, dn_tinv = _dn_fwd(qkv, bg)
    zf = _shift_fwd(z_b, mu)
    rr, ww, kk2, vv, av, bv, gate = _tok_fwd("b_pre_fwd", _b_pre_fn, [zf], b_pars, [(d, F32)] * 7, tb_vjp)
    per_head = lambda t: t.reshape(rows, d // HEAD_B, HEAD_B)
    y_heads, b_hist, b_last = _rwkv_fwd(rr, ww, kk2, per_head(vv), av, bv)
    y_b = y_heads.reshape(rows, d)
    w_out_stack, full['ffn2_w_gu'], down2 = finish_gather("late", gather_late, y_heads)
    full['w_out'], full['ffn2_w_down'] = w_out_stack.reshape(-1, d), down2.reshape(-1, d)
    post_toks = [o_dn, z_zg, y_b, rr, kk2, vv, gate]
    (merged,) = _tok_fwd("post_fwd", _post_fn, post_toks, post_pars, [(d, BF16)], tb_vjp)
    h2 = _mm("out_proj", merged, full['w_out'], add=h1, tb=tb_mm, tn=d)
    h3 = _ffn_fwd("ffn2_fwd", h2, full['ffn2_norm'], full['ffn2_w_gu'], full['ffn2_w_down'], tb_ffn)

    target = jnp.pad(loss_target[0], ((CHUNK, 0), (0, 0)))
    dh3, g_final, loss_part = _loss(h3, target, full['final_norm'].reshape(1, d), tb_vjp)

    def ffn_backward(tag, h, dout, key_norm, key_gu, key_down, once_down=None):
        dh, dh_bf, dgain, xn, act, dgate, dup, dhalf = _ffn_bwd(tag + "_bwd", h, full[key_norm], dout, full[key_gu],
                                                                full[key_down], tb_mm)
        fc = dgate.shape[2]
        d_down = _mm_tn_from_slabs(tag + "_dw_down", act, dhalf, tk=tb_dw).reshape(N_DEV, -1, d)
        token = once_down(dh, dgain, d_down) if once_down else None
        d_gu = jnp.concatenate(
            [_mm_tn_from_slabs(tag + "_dw_gate", dgate, xn, tk=tb_dw, after=token).reshape(-1, fc, d),
             _mm_tn_from_slabs(tag + "_dw_up", dup, xn, tk=tb_dw).reshape(-1, fc, d)], axis=0)
        return dh, dh_bf, dgain, d_gu, d_down

    dh2, dh2_bf, g_ffn2_norm, g_ffn2_gu, g_ffn2_down = ffn_backward("ffn2", h2, dh3, 'ffn2_norm', 'ffn2_w_gu',
                                                                    'ffn2_w_down')
    g_w_out = _mm_tn("dw_out", merged, dh2_bf, tm=d, tn=d, tk=tb_dw).reshape(N_DEV, -1, d)

    def start_exchange(tag, keys, slabs, after=None):
        sems, kept, lands, token = _exchange_start("exchange_start_" + tag, slabs, after)
        return (tag, keys, sems, kept, lands), token

    ex_ffn2, token_ffn2 = start_exchange("ffn2", ['ffn2_w_gu', 'ffn2_w_down', 'w_out'], [g_ffn2_gu, g_ffn2_down, g_w_out])
    dmerged = _mm("d_merged", dh2_bf, full['w_out'], trans_b=True, after=token_ffn2, tb=tb_mm, tn=d)
    post_grads = _tok_bwd("post_bwd", _post_fn, post_toks, post_pars, [[dmerged]], list(range(7)), tb_vjp,
                          [F32, BF16] + [F32] * 5)
    do_dn, dz_zg, dy_b, dr1, dk1, dv1, dgate = post_grads[:7]
    g_out_norm, g_r_k, g_ln_g, g_ln_b = post_grads[7:]
    dr2, dw2, dk2, dv_heads, da2, db2 = _rwkv_bwd(rr, ww, kk2, per_head(vv), av, bv, b_hist, b_last, per_head(dy_b))
    dv2 = dv_heads.reshape(rows, d)
    b_grads = _tok_bwd("b_pre_bwd", _b_pre_fn, [zf], b_pars,
                       [[dr1, dr2], [dw2], [dk1, dk2], [dv1, dv2], [da2], [db2], [dgate]], [0], tb_vjp)
    dzf = b_grads[0]
    g_w0, g_w_up, g_a0, g_a_up, g_g_up, g_k_k, g_k_a = b_grads[1:]
    dz_b, g_mu = _shift_bwd(z_b, mu, dzf)
    dqkv, dbg = _dn_bwd(qkv, bg, dn_hist, dn_tinv, do_dn)
    dz_qkv, g_conv = _a_pre_bwd(z_qkv, full['a_conv_w'], dqkv)
    dz_bg, g_log_rate, g_dt_bias = _tok_bwd("bg_bwd", bg_fn, [z_bg], [log_rate, dt_bias], [[dbg]], [0], tb_mm, [BF16])

    small_early = {
        'a_conv_w': g_conv, 'a_log_rate': g_log_rate[:, heads_a:2 * heads_a],
        'a_dt_bias': g_dt_bias[:, heads_a:2 * heads_a], 'a_out_norm': g_out_norm, 'b_shift_mu': g_mu[:, :n_b],
        'b_w0': g_w0, 'b_w_up': g_w_up[:LORA_W], 'b_a0': g_a0, 'b_a_up': g_a_up[LORA_W:], 'b_g_up': g_g_up[:LORA_G],
        'b_k_k': g_k_k, 'b_k_a': g_k_a, 'b_r_k': g_r_k, 'b_ln_gain': g_ln_g, 'b_ln_bias': g_ln_b,
        'ffn2_norm': g_ffn2_norm, 'final_norm': g_final,
    }
    gather_small = _exchange_start("gather_start_small", [_pack(list(small_early.values()))], gather=True)

    du = None
    g_w_in_parts = []
    for tag, dz, wpiece in (("qkv", dz_qkv, w_qkv), ("zg", dz_zg, w_zg), ("b", dz_b, w_b), ("bg", dz_bg, w_bg)):
        du = _mm("du_" + tag, dz, wpiece, add=du, after=gather_small[-1] if du is None else None, tb=tb_mm, tn=d)
        g_w_in_parts.append(_mm_tn("dw_in_" + tag, dz, u, tm=_col_tile(dz.shape[1], 1536), tn=d, tk=tb_dw))
    gp_qkv, gp_zg, gp_b, gp_bg = g_w_in_parts
    g_w_in = jnp.concatenate([gp_qkv, gp_zg[:d], gp_bg[:2 * heads_a], gp_b[:n_b], gp_zg[d:]],
                             axis=0).reshape(N_DEV, -1, d)
    ex_w_in, token_w_in = start_exchange("w_in", ['w_in'], [g_w_in])
    dh1, g_mix_norm = _tok_bwd("mix_norm_bwd", _norm_res_fn, [h1], [full['mix_norm']], [[du], [dh2]], [0], tb_vjp,
                               after=token_w_in)
    tail = {}

    def once_ffn1_down(dh0, g_ffn1_norm, g_ffn1_down):
        tail['late'] = {'meta_tokens': dh0[PAD:CHUNK], 'ffn1_norm': g_ffn1_norm, 'mix_norm': g_mix_norm}
        tail['packed'] = _pack(list(tail['late'].values()) + [loss_part[:, :1]])
        (tail['parts'],) = _all_gather("gather_small_grads", [tail['packed']])
        tail['ex_down'], token = start_exchange("ffn1_down", ['ffn1_w_down'], [g_ffn1_down], after=tail['parts'])
        return token

    dh0, _, _, g_ffn1_gu, _ = ffn_backward("ffn1", h0, dh1, 'ffn1_norm', 'ffn1_w_gu', 'ffn1_w_down', once_ffn1_down)
    ex_ffn1_gu, _ = start_exchange("ffn1_gu", ['ffn1_w_gu'], [g_ffn1_gu])
    small_late, packed_late, late_parts = tail['late'], tail['packed'], tail['parts']
    (early_parts,) = finish_gather("small", gather_small, late_parts)
    pieces = (_unpack(_sum_slabs("sum_small_early", early_parts, early_parts.shape[1]),
                      [g.shape for g in small_early.values()])
              + _unpack(_sum_slabs("sum_small_grads", late_parts, packed_late.shape[0]),
                        [g.shape for g in small_late.values()] + [(1, 1)]))
    small_names = list(small_early) + list(small_late)
    small_grad = dict(zip(small_names, pieces[:-1]))
    loss = pieces[-1].reshape(())

    grads, deltas, new_m, new_v = {}, {}, {}, {}
    local_small = {}
    for k in small_names:
        g = small_grad[k]
        if k in small_sharded:
            width = wts[k].shape[-1]
            g = lax.dynamic_slice_in_dim(g, me * width, width, axis=1)
        local_small[k] = g.reshape(wts[k].shape)
    pk = lambda tree: _pack([tree[k] for k in small_names])
    dl_s, m_s, v_s = _adamw_small(pk(wts), pk(local_small), pk(mom_m), pk(mom_v))
    shapes = [wts[k].shape for k in small_names]
    for k, dl, m2, v2 in zip(small_names, _unpack(dl_s, shapes), _unpack(m_s, shapes), _unpack(v_s, shapes)):
        grads[k], deltas[k], new_m[k], new_v[k] = local_small[k], dl, m2, v2

    done = dl_s
    for tag, keys, sems, kept, lands in (ex_ffn2, ex_w_in, tail['ex_down'], ex_ffn1_gu):
        kept, lands = _exchange_wait("exchange_wait_" + tag, sems, kept, lands, done)
        for k, slabs, landed in zip(keys, kept, lands):
            own = lax.dynamic_index_in_dim(slabs, me, axis=0, keepdims=False)
            res = _adamw("adamw_" + k, own, landed, shard_of(wts, k), shard_of(mom_m, k), shard_of(mom_v, k))
            done = res[1]
            res = [(t.T if k in col_sharded else t)[None] for t in res]
            grads[k], deltas[k], new_m[k], new_v[k] = res

    grad_x = dh0[CHUNK:][None]
    return (loss, grad_x, *[grads[k] for k in names], *[deltas[k] for k in names],
            *[new_m[k] for k in names], *[new_v[k] for k in names])
```

```python
import functools

import jax
import jax.numpy as jnp
from jax import lax
from jax.experimental import pallas as pl
from jax.experimental.pallas import tpu as pltpu

F32 = jnp.float32
BF16 = jnp.bfloat16
N_DEV = 8
N_META = 16
CHUNK = 64
PAD = CHUNK - N_META
HEAD_A = 128
HEAD_B = 64
LORA_W, LORA_A, LORA_G = 64, 64, 160
LORA_PAD = 384
EPS = 1e-6
GN_EPS = HEAD_B * 1e-5
ADAM_LR, ADAM_B1, ADAM_B2, ADAM_EPS, ADAM_WD, ADAM_STEP = 0.001, 0.9, 0.999, 1e-08, 0.01, 10
SCAN_STEPS = 32
MXU_WIDTH = 256
VMEM_LIMIT = 56 * 1024 * 1024
DN_PRECISION = lax.Precision.HIGH
MESH_ID = pl.DeviceIdType.MESH
ANY = pl.BlockSpec(memory_space=pl.ANY)
HBM_SPEC = pl.BlockSpec(memory_space=pltpu.HBM)
SEM_SPEC = pl.BlockSpec(memory_space=pltpu.SEMAPHORE)


def _cp(*sem):
    return pltpu.CompilerParams(dimension_semantics=sem, vmem_limit_bytes=VMEM_LIMIT)


def _tb(t, target):
    best = 16
    for d in range(16, target + 1, 16):
        if t % d == 0:
            best = d
    return best


def _sigmoid(x):
    return 1.0 / (1.0 + jnp.exp(-x))


def _silu(x):
    return x * _sigmoid(x)


def _softplus(x):
    return jnp.maximum(x, 0.0) + jnp.log(1.0 + jnp.exp(-jnp.abs(x)))


def _dot_nt(a, b, precision=None):
    return lax.dot_general(a, b, (((1,), (1,)), ((), ())), preferred_element_type=F32, precision=precision)


def _dot_tn(a, b, precision=None):
    return lax.dot_general(a, b, (((0,), (0,)), ((), ())), preferred_element_type=F32, precision=precision)


def _dot(a, b, precision=None):
    return jnp.dot(a, b, preferred_element_type=F32, precision=precision)


def _block_diag_ones():
    i = lax.broadcasted_iota(jnp.int32, (MXU_WIDTH, MXU_WIDTH), 0) // HEAD_B
    j = lax.broadcasted_iota(jnp.int32, (MXU_WIDTH, MXU_WIDTH), 1) // HEAD_B
    return (i == j).astype(BF16)


def _hi_lo(x):
    hi = x.astype(BF16)
    return hi, (x - hi.astype(F32)).astype(BF16)


def _segsum_many(xs, bd):
    groups = [x if isinstance(x, tuple) else _hi_lo(x) for x in xs]
    rows = groups[0][0].shape[0]
    stacked = jnp.concatenate([p for grp in groups for p in grp], axis=0)
    out = jnp.concatenate([_dot(stacked[:, s:s + MXU_WIDTH], bd) for s in range(0, stacked.shape[1], MXU_WIDTH)], axis=1)
    res, pos = [], 0
    for grp in groups:
        acc = out[pos:pos + rows]
        for j in range(1, len(grp)):
            acc = acc + out[pos + j * rows:pos + (j + 1) * rows]
        res.append(acc)
        pos += len(grp) * rows
    return res


def _segsum_impl(x):
    return _segsum_many([x], _block_diag_ones())[0]


@jax.custom_vjp
def _segsum64(x):
    return _segsum_impl(x)


_segsum64.defvjp(lambda x: (_segsum_impl(x), None), lambda _, ct: (_segsum_impl(ct),))


def _tok(t):
    return t if isinstance(t, tuple) else (t, t.shape[1], 0)


def _tok_spec(tb, width, colblk):
    return pl.BlockSpec((tb, width), lambda i: (i, colblk))


def _par_spec(p):
    return pl.BlockSpec(p.shape, lambda i: (0, 0))


def _tok_fwd(name, fn, toks, pars, outs, tb):
    toks = [_tok(t) for t in toks]
    rows = toks[0][0].shape[0]
    n_in = len(toks) + len(pars)

    def body(*refs):
        row0 = pl.program_id(0) * tb
        res = fn(row0, *[r[...] for r in refs[:n_in]])
        for r, o in zip(refs[n_in:], res):
            r[...] = o.astype(r.dtype)

    return pl.pallas_call(
        body, name=name, grid=(rows // tb,),
        in_specs=[_tok_spec(tb, w, c) for _, w, c in toks] + [_par_spec(p) for p in pars],
        out_specs=[_tok_spec(tb, w, 0) for w, _ in outs],
        out_shape=[jax.ShapeDtypeStruct((rows, w), dt) for w, dt in outs],
        compiler_params=_cp("parallel"),
    )(*[a for a, _, _ in toks], *pars)


def _tok_bwd(name, fn, toks, pars, cts, want, tb, want_dtypes=None, after=None):
    toks = [_tok(t) for t in toks]
    want_dtypes = want_dtypes or [F32] * len(want)
    cts = [[_tok(c) for c in group] for group in cts]
    flat_cts = [c for group in cts for c in group]
    rows = toks[0][0].shape[0]
    n_tok, n_par, n_ct = len(toks), len(pars), len(flat_cts)
    extra = [] if after is None else [after]

    def body(*refs):
        i = pl.program_id(0)
        row0 = i * tb
        prim = [r[...].astype(F32) for r in refs[:n_tok + n_par]]
        ct_refs = list(refs[n_tok + n_par:n_tok + n_par + n_ct])
        out_refs = refs[n_tok + n_par + n_ct + len(extra):]
        res, vjp = jax.vjp(lambda *a: fn(row0, *a), *prim)
        ct = []
        for group, o in zip(cts, res):
            acc = None
            for _ in group:
                v = ct_refs.pop(0)[...].astype(F32)
                acc = v if acc is None else acc + v
            ct.append(acc.astype(o.dtype))
        grads = vjp(tuple(ct))
        for r, k in zip(out_refs[:len(want)], want):
            r[...] = grads[k].astype(r.dtype)

        @pl.when(i == 0)
        def _():
            for r in out_refs[len(want):]:
                r[...] = jnp.zeros_like(r)

        for r, g in zip(out_refs[len(want):], grads[n_tok:]):
            r[...] += g

    return pl.pallas_call(
        body, name=name, grid=(rows // tb,),
        in_specs=[_tok_spec(tb, w, c) for _, w, c in toks] + [_par_spec(p) for p in pars]
        + [_tok_spec(tb, w, c) for _, w, c in flat_cts] + [ANY] * len(extra),
        out_specs=[_tok_spec(tb, toks[k][1], 0) for k in want] + [_par_spec(p) for p in pars],
        out_shape=[jax.ShapeDtypeStruct((rows, toks[k][1]), dt) for k, dt in zip(want, want_dtypes)]
        + [jax.ShapeDtypeStruct(p.shape, F32) for p in pars],
        compiler_params=_cp("arbitrary"),
    )(*[a for a, _, _ in toks], *pars, *[a for a, _, _ in flat_cts], *extra)


def _mm(name, a, b, *, trans_b=False, add=None, after=None, tb, tn):
    rows, k = a.shape
    n = b.shape[0] if trans_b else b.shape[1]

    def body(*refs):
        a_ref, b_ref = refs[:2]
        o_ref = refs[-1]
        acc = _dot_nt(a_ref[...], b_ref[...]) if trans_b else _dot(a_ref[...], b_ref[...])
        if add is not None:
            acc = acc + refs[2][...]
        o_ref[...] = acc

    in_specs = [pl.BlockSpec((tb, k), lambda i, j: (i, 0)),
                pl.BlockSpec((tn, k), lambda i, j: (j, 0)) if trans_b else pl.BlockSpec((k, tn), lambda i, j: (0, j))]
    args = [a, b]
    if add is not None:
        in_specs.append(pl.BlockSpec((tb, tn), lambda i, j: (i, j)))
        args.append(add)
    if after is not None:
        in_specs.append(ANY)
        args.append(after)
    return pl.pallas_call(
        body, name=name, grid=(rows // tb, n // tn), in_specs=in_specs,
        out_specs=pl.BlockSpec((tb, tn), lambda i, j: (i, j)),
        out_shape=jax.ShapeDtypeStruct((rows, n), F32),
        compiler_params=_cp("parallel", "parallel"),
    )(*args)


def _mm_tn_call(name, grid, a, b, a_spec, b_spec, o_spec, acc_shape, out_shape, after=None):
    last = len(grid) - 1
    extra = [] if after is None else [after]

    def body(a_ref, b_ref, *rest):
        o_ref, acc_ref = rest[len(extra):]
        k = pl.program_id(last)

        @pl.when(k == 0)
        def _():
            acc_ref[...] = jnp.zeros_like(acc_ref)

        a_blk = a_ref[0] if len(a_ref.shape) == 3 else a_ref[...]
        b_blk = b_ref[0] if len(b_ref.shape) == 3 else b_ref[...]
        acc_ref[...] += _dot_tn(a_blk, b_blk)

        @pl.when(k == grid[last] - 1)
        def _():
            if len(o_ref.shape) == 3:
                o_ref[0] = acc_ref[...].astype(o_ref.dtype)
            else:
                o_ref[...] = acc_ref[...].astype(o_ref.dtype)

    return pl.pallas_call(
        body, name=name, grid=grid, in_specs=[a_spec, b_spec] + [ANY] * len(extra), out_specs=o_spec,
        out_shape=jax.ShapeDtypeStruct(out_shape, BF16), scratch_shapes=[pltpu.VMEM(acc_shape, F32)],
        compiler_params=_cp(*(["parallel"] * last + ["arbitrary"])),
    )(a, b, *extra)


def _mm_tn(name, a, b, *, tm, tn, tk):
    rows, m = a.shape
    n = b.shape[1]
    return _mm_tn_call(name, (m // tm, n // tn, rows // tk), a, b,
                       pl.BlockSpec((tk, tm), lambda i, j, k: (k, i)), pl.BlockSpec((tk, tn), lambda i, j, k: (k, j)),
                       pl.BlockSpec((tm, tn), lambda i, j, k: (i, j)), (tm, tn), (m, n))


def _mm_tn_from_slabs(name, a3, b, *, tk, after=None):
    s, rows, c = a3.shape
    n = b.shape[1]
    return _mm_tn_call(name, (s, rows // tk), a3, b,
                       pl.BlockSpec((1, tk, c), lambda i, k: (i, k, 0)), pl.BlockSpec((tk, n), lambda i, k: (k, 0)),
                       pl.BlockSpec((c, n), lambda i, k: (i, 0)), (c, n), (s * c, n), after)


def _col_tile(n, target):
    if n <= target:
        return n
    best = 128
    for d in range(128, target + 1, 128):
        if n % d == 0:
            best = d
    return best


def _rms(x, gain):
    return x * lax.rsqrt(jnp.mean(x * x, axis=-1, keepdims=True) + EPS) * gain


def _ffn_specs(d, fc, nj):
    return [pl.BlockSpec((1, fc, d), lambda i, j: (j, 0, 0)), pl.BlockSpec((1, fc, d), lambda i, j: (nj + j, 0, 0)),
            pl.BlockSpec((fc, d), lambda i, j: (j, 0))]


def _ffn_fwd(name, h, gain, wgu, wd, tb):
    rows, d = h.shape
    nj = wgu.shape[0] // 2
    fc = wgu.shape[1]

    def body(h_ref, g_ref, wg_ref, wu_ref, wd_ref, o_ref, xn_s, acc_s):
        j = pl.program_id(1)

        @pl.when(j == 0)
        def _():
            xn_s[...] = _rms(h_ref[...], g_ref[...]).astype(BF16)
            acc_s[...] = jnp.zeros_like(acc_s)

        wg, wu, wdn = wg_ref[0], wu_ref[0], wd_ref[...]
        for half in range(2):
            rs = pl.ds(half * (tb // 2), tb // 2)
            xn = xn_s[rs, :]
            gate = _dot_nt(xn, wg)
            up = _dot_nt(xn, wu)
            acc_s[rs, :] += _dot((_silu(gate) * up).astype(BF16), wdn)

        @pl.when(j == nj - 1)
        def _():
            o_ref[...] = h_ref[...] + 0.5 * acc_s[...]

    return pl.pallas_call(
        body, name=name, grid=(rows // tb, nj),
        in_specs=[pl.BlockSpec((tb, d), lambda i, j: (i, 0)), pl.BlockSpec((1, d), lambda i, j: (0, 0))]
        + _ffn_specs(d, fc, nj),
        out_specs=pl.BlockSpec((tb, d), lambda i, j: (i, 0)),
        out_shape=jax.ShapeDtypeStruct((rows, d), F32),
        scratch_shapes=[pltpu.VMEM((tb, d), BF16), pltpu.VMEM((tb, d), F32)],
        compiler_params=_cp("parallel", "arbitrary"),
    )(h, gain, wgu, wgu, wd)


def _ffn_bwd(name, h, gain, dout, wgu, wd, tb):
    rows, d = h.shape
    nj = wgu.shape[0] // 2
    fc = wgu.shape[1]

    def body(h_ref, g_ref, do_ref, wg_ref, wu_ref, wd_ref,
             dh_ref, dhb_ref, dg_ref, xn_ref, act_ref, dgate_ref, dup_ref, dhalf_ref, dxn_s):
        i, j = pl.program_id(0), pl.program_id(1)

        @pl.when(j == 0)
        def _():
            xn_ref[...] = _rms(h_ref[...], g_ref[...]).astype(BF16)
            dhalf_ref[...] = (0.5 * do_ref[...]).astype(BF16)
            dxn_s[...] = jnp.zeros_like(dxn_s)

        wg, wu, wdn = wg_ref[0], wu_ref[0], wd_ref[...]
        for half in range(2):
            rs = pl.ds(half * (tb // 2), tb // 2)
            xn = xn_ref[rs, :]
            gate = _dot_nt(xn, wg)
            up = _dot_nt(xn, wu)
            sg = _sigmoid(gate)
            dact = _dot_nt(dhalf_ref[rs, :], wdn)
            act_ref[0, rs, :] = (gate * sg * up).astype(BF16)
            dgate = (dact * up * (sg * (1.0 + gate * (1.0 - sg)))).astype(BF16)
            dup = (dact * gate * sg).astype(BF16)
            dgate_ref[0, rs, :] = dgate
            dup_ref[0, rs, :] = dup
            dxn_s[rs, :] += _dot(dgate, wg) + _dot(dup, wu)

        @pl.when((i == 0) & (j == 0))
        def _():
            dg_ref[...] = jnp.zeros_like(dg_ref)

        @pl.when(j == nj - 1)
        def _():
            x = h_ref[...]
            r = lax.rsqrt(jnp.mean(x * x, axis=-1, keepdims=True) + EPS)
            dxn = dxn_s[...]
            dyg = dxn * g_ref[...]
            dh = do_ref[...] + r * dyg - x * (r * r * r) * jnp.mean(dyg * x, axis=-1, keepdims=True)
            dh_ref[...] = dh
            dhb_ref[...] = dh.astype(BF16)
            dg_ref[...] += jnp.sum(dxn * x * r, axis=0, keepdims=True)

    row_d = pl.BlockSpec((tb, d), lambda i, j: (i, 0))
    slab = pl.BlockSpec((1, tb, fc), lambda i, j: (j, i, 0))
    hidden = jax.ShapeDtypeStruct((nj, rows, fc), BF16)
    return pl.pallas_call(
        body, name=name, grid=(rows // tb, nj),
        in_specs=[row_d, pl.BlockSpec((1, d), lambda i, j: (0, 0)), row_d] + _ffn_specs(d, fc, nj),
        out_specs=[row_d, row_d, pl.BlockSpec((1, d), lambda i, j: (0, 0)), row_d, slab, slab, slab, row_d],
        out_shape=[jax.ShapeDtypeStruct((rows, d), F32), jax.ShapeDtypeStruct((rows, d), BF16),
                   jax.ShapeDtypeStruct((1, d), F32), jax.ShapeDtypeStruct((rows, d), BF16),
                   hidden, hidden, hidden, jax.ShapeDtypeStruct((rows, d), BF16)],
        scratch_shapes=[pltpu.VMEM((tb, d), F32)],
        compiler_params=_cp("arbitrary", "arbitrary"),
    )(h, gain, dout, wgu, wgu, wd)


def _shift_rows(x, s):
    return pltpu.roll(x, s % x.shape[0], 0)


def _a_post(c, which):
    s = _silu(c)
    n = s * lax.rsqrt(jnp.sum(s * s, axis=-1, keepdims=True) + 1e-6)
    scale = jnp.where(which == 0, HEAD_A ** -0.5, 1.0)
    return jnp.where(which == 2, s, n * scale)


def _conv(x, w):
    return x * w[3:4] + _shift_rows(x, 1) * w[2:3] + _shift_rows(x, 2) * w[1:2] + _shift_rows(x, 3) * w[0:1]


def _a_pre_fwd(zqkv, conv_w):
    rows, width = zqkv.shape
    heads = width // (3 * HEAD_A)

    def body(x_ref, w_ref, o_ref):
        which = pl.program_id(0) // heads
        live = lax.broadcasted_iota(jnp.int32, (rows, HEAD_A), 0) >= PAD
        o_ref[...] = jnp.where(live, _a_post(_conv(x_ref[...], w_ref[...]), which), 0.0)

    return pl.pallas_call(
        body, name="a_pre_fwd", grid=(width // HEAD_A,),
        in_specs=[pl.BlockSpec((rows, HEAD_A), lambda c: (0, c)), pl.BlockSpec((4, HEAD_A), lambda c: (0, c))],
        out_specs=pl.BlockSpec((rows, HEAD_A), lambda c: (0, c)),
        out_shape=jax.ShapeDtypeStruct((rows, width), F32),
        compiler_params=_cp("parallel"),
    )(zqkv, conv_w)


def _a_pre_bwd(zqkv, conv_w, dqkv):
    rows, width = zqkv.shape
    heads = width // (3 * HEAD_A)

    def body(x_ref, w_ref, ct_ref, dx_ref, dw_ref):
        which = pl.program_id(0) // heads
        live = lax.broadcasted_iota(jnp.int32, (rows, HEAD_A), 0) >= PAD
        x, w = x_ref[...], w_ref[...]
        _, vjp = jax.vjp(lambda c: _a_post(c, which), _conv(x, w))
        (dc,) = vjp(jnp.where(live, ct_ref[...], 0.0))
        dc = jnp.where(live, dc, 0.0)
        dx_ref[...] = (dc * w[3:4] + _shift_rows(dc, -1) * w[2:3] + _shift_rows(dc, -2) * w[1:2]
                       + _shift_rows(dc, -3) * w[0:1]).astype(BF16)
        dw_ref[...] = jnp.concatenate(
            [jnp.sum(dc * (_shift_rows(x, 3 - j) if j < 3 else x), axis=0, keepdims=True) for j in range(4)], axis=0)

    col = pl.BlockSpec((rows, HEAD_A), lambda c: (0, c))
    wsp = pl.BlockSpec((4, HEAD_A), lambda c: (0, c))
    return pl.pallas_call(
        body, name="a_pre_bwd", grid=(width // HEAD_A,),
        in_specs=[col, wsp, col], out_specs=[col, wsp],
        out_shape=[jax.ShapeDtypeStruct((rows, width), BF16), jax.ShapeDtypeStruct((4, width), F32)],
        compiler_params=_cp("parallel"),
    )(zqkv, conv_w, dqkv)


SHIFT_TILE = 384


def _shift_fwd(zb, mu):
    rows, width = zb.shape

    def body(x_ref, mu_ref, o_ref):
        x = x_ref[...]
        first = lax.broadcasted_iota(jnp.int32, x.shape, 0) == 0
        prev = jnp.where(first, 0.0, _shift_rows(x, 1))
        o_ref[...] = x + (prev - x) * mu_ref[...]

    col = pl.BlockSpec((rows, SHIFT_TILE), lambda c: (0, c))
    return pl.pallas_call(
        body, name="shift_fwd", grid=(width // SHIFT_TILE,),
        in_specs=[col, pl.BlockSpec((1, SHIFT_TILE), lambda c: (0, c))], out_specs=col,
        out_shape=jax.ShapeDtypeStruct((rows, width), F32), compiler_params=_cp("parallel"),
    )(zb, mu)


def _shift_bwd(zb, mu, dzf):
    rows, width = zb.shape

    def body(x_ref, mu_ref, ct_ref, dx_ref, dmu_ref):
        x, ct, mu_v = x_ref[...], ct_ref[...], mu_ref[...]
        row = lax.broadcasted_iota(jnp.int32, x.shape, 0)
        prev = jnp.where(row == 0, 0.0, _shift_rows(x, 1))
        nxt = jnp.where(row == rows - 1, 0.0, _shift_rows(ct, -1))
        dx_ref[...] = (ct * (1.0 - mu_v) + nxt * mu_v).astype(BF16)
        dmu_ref[...] = jnp.sum(ct * (prev - x), axis=0, keepdims=True)

    col = pl.BlockSpec((rows, SHIFT_TILE), lambda c: (0, c))
    msp = pl.BlockSpec((1, SHIFT_TILE), lambda c: (0, c))
    return pl.pallas_call(
        body, name="shift_bwd", grid=(width // SHIFT_TILE,),
        in_specs=[col, msp, col], out_specs=[col, msp],
        out_shape=[jax.ShapeDtypeStruct((rows, width), BF16), jax.ShapeDtypeStruct((1, width), F32)],
        compiler_params=_cp("parallel"),
    )(zb, mu, dzf)


def _neumann_inverse(p):
    heads = range(len(p))
    eye = (lax.broadcasted_iota(jnp.int32, (CHUNK, CHUNK), 0)
           == lax.broadcasted_iota(jnp.int32, (CHUNK, CHUNK), 1)).astype(F32)
    tinv = [eye + p[h] for h in heads]
    for _ in range(5):
        p = [_dot(p[h], p[h], DN_PRECISION) for h in heads]
        tinv = [tinv[h] + _dot(tinv[h], p[h], DN_PRECISION) for h in heads]
    return tinv


@jax.custom_vjp
def _unit_lower_inverse(p):
    return _neumann_inverse(p)


def _unit_lower_inverse_fwd(p):
    tinv = _neumann_inverse(p)
    return tinv, tinv


def _unit_lower_inverse_bwd(tinv, ct):
    heads = range(len(tinv))
    left = [_dot_tn(tinv[h], ct[h], DN_PRECISION) for h in heads]
    return ([_dot_nt(left[h], tinv[h], DN_PRECISION) for h in heads],)


_unit_lower_inverse.defvjp(_unit_lower_inverse_fwd, _unit_lower_inverse_bwd)


@jax.custom_vjp
def _known_inverse(p, tinv):
    return tinv


_known_inverse.defvjp(lambda p, tinv: (tinv, tinv),
                      lambda tinv, ct: (_unit_lower_inverse_bwd(tinv, ct)[0], [jnp.zeros_like(t) for t in tinv]))


def _dn_chunk(q, k, v, beta, g, state, saved_tinv=None):
    heads = range(len(q))
    ri = lax.broadcasted_iota(jnp.int32, (CHUNK, CHUNK), 0)
    ci = lax.broadcasted_iota(jnp.int32, (CHUNK, CHUNK), 1)
    eye = (ri == ci).astype(F32)
    incl = ri >= ci
    last = lax.broadcasted_iota(jnp.int32, (CHUNK, 1), 0) == CHUNK - 1
    g_row = [jnp.sum(g[h] * eye, axis=0, keepdims=True) for h in heads]
    gc = [jnp.sum(jnp.where(incl, g_row[h], 0.0), axis=1, keepdims=True) for h in heads]
    gc_row = [jnp.sum(gc[h] * eye, axis=0, keepdims=True) for h in heads]
    decay = [jnp.where(incl, jnp.exp(jnp.where(incl, gc[h] - gc_row[h], 0.0)), 0.0) for h in heads]
    kb = [k[h] * beta[h] for h in heads]
    vb = [v[h] * beta[h] for h in heads]
    p = [-jnp.where(ri > ci, _dot_nt(kb[h], k[h]) * decay[h], 0.0) for h in heads]
    tinv = _unit_lower_inverse(p) if saved_tinv is None else _known_inverse(p, saved_tinv)
    eg = [jnp.exp(gc[h]) for h in heads]
    u = [_dot(tinv[h], vb[h]) for h in heads]
    wk = [_dot(tinv[h], kb[h] * eg[h]) for h in heads]
    attn = [_dot_nt(q[h], k[h]) * decay[h] for h in heads]
    g_last = [jnp.sum(jnp.where(last, gc[h], 0.0), axis=0, keepdims=True) for h in heads]
    k_tail = [k[h] * jnp.exp(g_last[h] - gc[h]) for h in heads]
    v_new = [u[h] - _dot(wk[h], state[h]) for h in heads]
    o = [_dot(q[h] * eg[h], state[h]) + _dot(attn[h], v_new[h]) for h in heads]
    new = [state[h] * jnp.exp(g_last[h]) + _dot_tn(k_tail[h], v_new[h]) for h in heads]
    return (o, new, tinv) if saved_tinv is None else (o, new)


def _bg_cols(bg, h, heads):
    lane = lax.broadcasted_iota(jnp.int32, bg.shape, 1)
    beta = jnp.sum(jnp.where(lane == h, bg, 0.0), axis=1, keepdims=True)
    g = jnp.sum(jnp.where(lane == heads + h, bg, 0.0), axis=1, keepdims=True)
    return beta, g


def _dn_fwd(qkv, bg):
    rows = qkv.shape[0]
    heads = qkv.shape[1] // (3 * HEAD_A)
    n = rows // CHUNK
    hp, groups = heads, 1

    def body(q_ref, k_ref, v_ref, bg_ref, o_ref, hist_ref, tinv_ref, s_ref):
        c, grp = pl.program_id(0), pl.program_id(1)

        @pl.when(c == 0)
        def _():
            for i in range(hp):
                s_ref[grp * hp + i] = jnp.zeros((HEAD_A, HEAD_A), F32)

        bg_v = bg_ref[...]
        cols = [slice(i * HEAD_A, (i + 1) * HEAD_A) for i in range(hp)]
        state = [s_ref[grp * hp + i] for i in range(hp)]
        beta_g = [_bg_cols(bg_v, grp * hp + i, heads) for i in range(hp)]
        o, new, tinv = _dn_chunk([q_ref[:, c_] for c_ in cols], [k_ref[:, c_] for c_ in cols],
                                 [v_ref[:, c_] for c_ in cols], [b for b, _ in beta_g], [g for _, g in beta_g], state)
        for i in range(hp):
            hist_ref[0, i] = state[i]
            tinv_ref[0, i] = tinv[i]
            o_ref[:, cols[i]] = o[i]
            s_ref[grp * hp + i] = new[i]

    def part(p):
        return pl.BlockSpec((CHUNK, hp * HEAD_A), lambda c, grp: (c, p * groups + grp))

    return pl.pallas_call(
        body, name="deltanet_fwd", grid=(n, groups),
        in_specs=[part(0), part(1), part(2), pl.BlockSpec((CHUNK, 128), lambda c, grp: (c, 0))],
        out_specs=[part(0), pl.BlockSpec((1, hp, HEAD_A, HEAD_A), lambda c, grp: (c, grp, 0, 0)),
                   pl.BlockSpec((1, hp, CHUNK, CHUNK), lambda c, grp: (c, grp, 0, 0))],
        out_shape=[jax.ShapeDtypeStruct((rows, heads * HEAD_A), F32),
                   jax.ShapeDtypeStruct((n, heads, HEAD_A, HEAD_A), F32),
                   jax.ShapeDtypeStruct((n, heads, CHUNK, CHUNK), F32)],
        scratch_shapes=[pltpu.VMEM((heads, HEAD_A, HEAD_A), F32)],
        compiler_params=_cp("arbitrary", "arbitrary"),
    )(qkv, qkv, qkv, bg)


def _dn_bwd(qkv, bg, hist, tinv_hist, do):
    rows = qkv.shape[0]
    heads = qkv.shape[1] // (3 * HEAD_A)
    n = rows // CHUNK
    hp, groups = heads, 1

    def body(q_ref, k_ref, v_ref, bg_ref, hist_ref, tinv_ref, do_ref, dqkv_ref, dbg_ref, ds_ref):
        c, grp = pl.program_id(0), pl.program_id(1)

        @pl.when(c == 0)
        def _():
            for i in range(hp):
                ds_ref[grp * hp + i] = jnp.zeros((HEAD_A, HEAD_A), F32)

        bg_v = bg_ref[...]
        lane = lax.broadcasted_iota(jnp.int32, (CHUNK, 128), 1)
        cols = [slice(i * HEAD_A, (i + 1) * HEAD_A) for i in range(hp)]
        beta_g = [_bg_cols(bg_v, grp * hp + i, heads) for i in range(hp)]
        _, vjp = jax.vjp(_dn_chunk, [q_ref[:, c_] for c_ in cols], [k_ref[:, c_] for c_ in cols],
                         [v_ref[:, c_] for c_ in cols], [b for b, _ in beta_g], [g for _, g in beta_g],
                         [hist_ref[0, i] for i in range(hp)], [tinv_ref[0, i] for i in range(hp)])
        dq, dk, dv, dbeta, dg, ds, _ = vjp(([do_ref[:, c_] for c_ in cols], [ds_ref[grp * hp + i] for i in range(hp)]))
        dbg = jnp.zeros((CHUNK, 128), F32)
        for i in range(hp):
            h = grp * hp + i
            for p, part_grad in enumerate((dq, dk, dv)):
                dqkv_ref[:, pl.ds((p * heads + i) * HEAD_A, HEAD_A)] = part_grad[i]
            ds_ref[h] = ds[i]
            dbg = dbg + jnp.where(lane == h, dbeta[i], 0.0) + jnp.where(lane == heads + h, dg[i], 0.0)

        @pl.when(grp == 0)
        def _():
            dbg_ref[...] = jnp.zeros_like(dbg_ref)

        dbg_ref[...] += dbg

    def part(p):
        return pl.BlockSpec((CHUNK, hp * HEAD_A), lambda c, grp: (n - 1 - c, p * groups + grp))

    return pl.pallas_call(
        body, name="deltanet_bwd", grid=(n, groups),
        in_specs=[part(0), part(1), part(2), pl.BlockSpec((CHUNK, 128), lambda c, grp: (n - 1 - c, 0)),
                  pl.BlockSpec((1, hp, HEAD_A, HEAD_A), lambda c, grp: (n - 1 - c, grp, 0, 0)),
                  pl.BlockSpec((1, hp, CHUNK, CHUNK), lambda c, grp: (n - 1 - c, grp, 0, 0)), part(0)],
        out_specs=[pl.BlockSpec((CHUNK, 3 * heads * HEAD_A), lambda c, grp: (n - 1 - c, 0)),
                   pl.BlockSpec((CHUNK, 128), lambda c, grp: (n - 1 - c, 0))],
        out_shape=[jax.ShapeDtypeStruct(qkv.shape, F32), jax.ShapeDtypeStruct((rows, 128), F32)],
        scratch_shapes=[pltpu.VMEM((heads, HEAD_A, HEAD_A), F32)],
        compiler_params=_cp("arbitrary", "arbitrary"),
    )(qkv, qkv, qkv, bg, hist, tinv_hist, do)


def _head_mask(heads, width):
    return (lax.broadcasted_iota(jnp.int32, (heads, width), 0)
            == lax.broadcasted_iota(jnp.int32, (heads, width), 1) // HEAD_B)


def _masked_rows(mask, row):
    return jnp.where(mask, row, 0.0).astype(BF16)


def _rwkv_fwd(r, w, k, v, a, b):
    rows, width = r.shape
    heads = width // HEAD_B
    ts = SCAN_STEPS

    def body(r_ref, w_ref, k_ref, v_ref, a_ref, b_ref, y_ref, hist_ref, s_ref):
        @pl.when(pl.program_id(0) == 0)
        def _():
            s_ref[...] = jnp.zeros_like(s_ref)

        mask = _head_mask(heads, width)
        onehot = mask.astype(BF16)
        onehot2 = jnp.concatenate([onehot, onehot], axis=0)
        bd = _block_diag_ones()

        spread_v = [_dot_tn(jnp.concatenate(_hi_lo(v_ref[j]), axis=0), onehot2) for j in range(ts)]
        a_next = pltpu.roll(a_ref[...], ts - 1, 0)
        b_dot_a, k_dot_a = _segsum_many([b_ref[...] * a_next, k_ref[...] * a_next], bd)
        s = s_ref[...]
        ys = []
        for j in range(0, ts, 2):
            row, nxt = pl.ds(j, 1), pl.ds(j + 1, 1)
            hist_ref[j] = s
            sa, base = _segsum_many([((s * a_ref[row, :]).astype(BF16),),
                                     ((s * (w_ref[row, :] * a_ref[nxt, :])).astype(BF16),)], bd)
            sa_next = base + sa * b_dot_a[j:j + 1] + spread_v[j] * k_dot_a[j:j + 1]
            s = s * w_ref[row, :] + sa * b_ref[row, :] + spread_v[j] * k_ref[row, :]
            hist_ref[j + 1] = s
            ys.append(_dot_nt(_masked_rows(mask, r_ref[row, :]), s.astype(BF16)))
            s = s * w_ref[nxt, :] + sa_next * b_ref[nxt, :] + spread_v[j + 1] * k_ref[nxt, :]
            ys.append(_dot_nt(_masked_rows(mask, r_ref[nxt, :]), s.astype(BF16)))
        for j in range(ts):
            y_ref[j] = ys[j]
        s_ref[...] = s

    blk = pl.BlockSpec((ts, width), lambda i: (i, 0))
    blk3 = pl.BlockSpec((ts, heads, HEAD_B), lambda i: (i, 0, 0))
    return pl.pallas_call(
        body, name="rwkv_fwd", grid=(rows // ts,),
        in_specs=[blk, blk, blk, blk3, blk, blk],
        out_specs=[blk3, pl.BlockSpec((ts, HEAD_B, width), lambda i: (i, 0, 0)),
                   pl.BlockSpec((HEAD_B, width), lambda i: (0, 0))],
        out_shape=[jax.ShapeDtypeStruct((rows, heads, HEAD_B), F32), jax.ShapeDtypeStruct((rows, HEAD_B, width), F32),
                   jax.ShapeDtypeStruct((HEAD_B, width), F32)],
        compiler_params=_cp("arbitrary"),
    )(r, w, k, v, a, b)


def _rwkv_bwd(r, w, k, v, a, b, hist, last, dy):
    rows, width = r.shape
    heads = width // HEAD_B
    ts = SCAN_STEPS
    nb = rows // ts

    def body(r_ref, w_ref, k_ref, v_ref, a_ref, b_ref, hist_ref, last_ref, dy_ref,
             dr_ref, dw_ref, dk_ref, dv_ref, da_ref, db_ref, g_ref, after_ref):
        @pl.when(pl.program_id(0) == 0)
        def _():
            g_ref[...] = jnp.zeros_like(g_ref)
            after_ref[...] = last_ref[...]

        mask = _head_mask(heads, width)
        onehot = mask.astype(BF16)
        bd = _block_diag_ones()

        def own_lanes(x):
            return jnp.sum(jnp.where(mask, x, 0.0), axis=0, keepdims=True)

        def colsum(x):
            return jnp.sum(x, axis=0, keepdims=True)

        dy_m = [dy_ref[j].astype(BF16) for j in range(ts)]
        spread_dy = [_dot_tn(dy_m[j], onehot) for j in range(ts)]
        state_after = [hist_ref[j + 1] if j < ts - 1 else after_ref[...] for j in range(ts)]
        dr = [own_lanes(_dot(dy_m[j], state_after[j].astype(BF16))) for j in range(ts)]
        sa_m = [_dot_nt(_masked_rows(mask, a_ref[pl.ds(j, 1), :]), hist_ref[j].astype(BF16)) for j in range(ts)]
        g = g_ref[...]
        dw, dk, db, da, dv = {}, {}, {}, {}, {}
        for j in reversed(range(ts)):
            row = pl.ds(j, 1)
            sp = hist_ref[j]
            g = g + spread_dy[j] * r_ref[row, :]
            (dsa,) = _segsum_many([((g * b_ref[row, :]).astype(BF16),)], bd)
            g_b = g.astype(BF16)
            both = _dot(jnp.concatenate([v_ref[j].astype(BF16), sa_m[j].astype(BF16)], axis=0), g_b)
            dk[j], db[j] = own_lanes(both[:heads]), own_lanes(both[heads:])
            dv[j] = _dot_nt(_masked_rows(mask, k_ref[row, :]), g_b)
            dw[j] = colsum(g * sp)
            da[j] = colsum(sp * dsa)
            g = g * w_ref[row, :] + dsa * a_ref[row, :]
        g_ref[...] = g
        after_ref[...] = hist_ref[0]
        for j in range(ts):
            dv_ref[j] = dv[j]
            for ref, vals in ((dr_ref, dr), (dw_ref, dw), (dk_ref, dk), (da_ref, da), (db_ref, db)):
                ref[pl.ds(j, 1), :] = vals[j]

    blk = pl.BlockSpec((ts, width), lambda i: (nb - 1 - i, 0))
    blk3 = pl.BlockSpec((ts, heads, HEAD_B), lambda i: (nb - 1 - i, 0, 0))
    state = pl.BlockSpec((HEAD_B, width), lambda i: (0, 0))
    return pl.pallas_call(
        body, name="rwkv_bwd", grid=(nb,),
        in_specs=[blk, blk, blk, blk3, blk, blk, pl.BlockSpec((ts, HEAD_B, width), lambda i: (nb - 1 - i, 0, 0)),
                  state, blk3],
        out_specs=[blk, blk, blk, blk3, blk, blk],
        out_shape=[jax.ShapeDtypeStruct((rows, width), F32)] * 3 + [jax.ShapeDtypeStruct((rows, heads, HEAD_B), F32)]
        + [jax.ShapeDtypeStruct((rows, width), F32)] * 2,
        scratch_shapes=[pltpu.VMEM((HEAD_B, width), F32), pltpu.VMEM((HEAD_B, width), F32)],
        compiler_params=_cp("arbitrary"),
    )(r, w, k, v, a, b, hist, last, dy)


def _live(row0, shape):
    return (row0 + lax.broadcasted_iota(jnp.int32, shape, 0)) >= PAD


def _norm_fn(row0, h, gain):
    return (_rms(h, gain),)


def _norm_res_fn(row0, h, gain):
    return _rms(h, gain), h


def _make_bg_fn(heads):
    def fn(row0, x, log_rate, dt_bias):
        lane = lax.broadcasted_iota(jnp.int32, x.shape, 1)
        beta = _sigmoid(x)
        g = -jnp.exp(log_rate) * _softplus(x + dt_bias)
        out = jnp.where(lane < heads, beta, jnp.where(lane < 2 * heads, g, 0.0))
        return (jnp.where(_live(row0, x.shape), out, 0.0),)
    return fn


def _b_pre_fn(row0, zf, w0, w_up, a0, a_up, g_up, k_k, k_a):
    d = w0.shape[1]
    r, k, v = zf[:, :d], zf[:, d:2 * d], zf[:, 2 * d:3 * d]
    lo = zf[:, 3 * d:3 * d + 128]
    lg = zf[:, 3 * d + 128:3 * d + LORA_PAD]
    lane = lax.broadcasted_iota(jnp.int32, lo.shape, 1)
    lw = _dot(jnp.where(lane < LORA_W, jnp.tanh(lo), 0.0), w_up)
    la = _dot(jnp.where(lane >= LORA_W, lo, 0.0), a_up)
    lane_g = lax.broadcasted_iota(jnp.int32, lg.shape, 1)
    gate = _dot(jnp.where(lane_g < LORA_G, _sigmoid(lg), 0.0), g_up)
    decay = jnp.exp(-jnp.exp(-_softplus(-(w0 + lw)) - 0.5))
    a = _sigmoid(a0 + la)
    kx = k * k_k
    kk = kx * lax.rsqrt(_segsum64(kx * kx) + 1e-6)
    k2 = k * (1.0 + (a - 1.0) * k_a)
    return r, decay, k2, v, -kk, kk * a, gate


def _post_fn(row0, o, zg, y, r, k2, v, gate, out_gain, r_k, ln_g, ln_b):
    d = o.shape[1]
    az, ga, gb = zg[:, :d], zg[:, d:2 * d], zg[:, 2 * d:]
    heads = d // HEAD_A
    parts = []
    for h in range(heads):
        oh = o[:, h * HEAD_A:(h + 1) * HEAD_A]
        parts.append(oh * lax.rsqrt(jnp.mean(oh * oh, axis=-1, keepdims=True) + EPS) * out_gain)
    o_a = jnp.concatenate(parts, axis=1) * _silu(az)
    mean = _segsum64(y) * (1.0 / HEAD_B)
    yc = y - mean
    var = _segsum64(yc * yc) * (1.0 / HEAD_B)
    yn = yc * lax.rsqrt(var + GN_EPS) * ln_g + ln_b
    o_b = (yn + _segsum64(r * k2 * r_k) * v) * gate
    return (_sigmoid(ga) * o_a + _sigmoid(gb) * o_b,)


def _loss(h3, target, gain, tb):
    rows, d = h3.shape

    def body(h_ref, t_ref, g_ref, dh_ref, dg_ref, l_ref):
        i = pl.program_id(0)
        live = (i * tb + lax.broadcasted_iota(jnp.int32, (tb, 1), 0)) >= CHUNK
        tgt = t_ref[...]

        def f(h, g):
            err = _rms(h, g) - tgt
            return 0.5 * jnp.sum(jnp.where(live, jnp.mean(err * err, axis=-1, keepdims=True), 0.0))

        val, vjp = jax.vjp(f, h_ref[...], g_ref[...])
        dh, dg = vjp(jnp.ones((), F32))
        dh_ref[...] = dh

        @pl.when(i == 0)
        def _():
            dg_ref[...] = jnp.zeros_like(dg_ref)
            l_ref[...] = jnp.zeros_like(l_ref)

        dg_ref[...] += dg
        l_ref[...] += jnp.full((1, 128), val, F32)

    blk = pl.BlockSpec((tb, d), lambda i: (i, 0))
    return pl.pallas_call(
        body, name="loss", grid=(rows // tb,),
        in_specs=[blk, blk, pl.BlockSpec((1, d), lambda i: (0, 0))],
        out_specs=[blk, pl.BlockSpec((1, d), lambda i: (0, 0)), pl.BlockSpec((1, 128), lambda i: (0, 0))],
        out_shape=[jax.ShapeDtypeStruct((rows, d), F32), jax.ShapeDtypeStruct((1, d), F32),
                   jax.ShapeDtypeStruct((1, 128), F32)],
        compiler_params=_cp("arbitrary"),
    )(h3, target, gain)


def _adamw_math(w, g, m, v):
    m2 = ADAM_B1 * m + (1.0 - ADAM_B1) * g
    v2 = ADAM_B2 * v + (1.0 - ADAM_B2) * (g * g)
    m_hat = m2 / (1.0 - ADAM_B1 ** ADAM_STEP)
    v_hat = v2 / (1.0 - ADAM_B2 ** ADAM_STEP)
    return -ADAM_LR * (m_hat / (jnp.sqrt(v_hat) + ADAM_EPS) + ADAM_WD * w), m2, v2


def _adamw(name, own, landed, w, m, v):
    rows, cols = w.shape
    if rows % 16 == 0:
        rb = _tb(rows, 128)
        grid, blk = (rows // rb,), pl.BlockSpec((rb, cols), lambda i: (i, 0))
        landed_blk = pl.BlockSpec((N_DEV - 1, rb, cols), lambda i: (0, i, 0))
    else:
        grid, blk = (cols // 128,), pl.BlockSpec((rows, 128), lambda i: (0, i))
        landed_blk = pl.BlockSpec((N_DEV - 1, rows, 128), lambda i: (0, 0, i))

    def body(o_ref, s_ref, w_ref, m_ref, v_ref, g_ref, d_ref, m2_ref, v2_ref):
        g = o_ref[...].astype(F32)
        for peer in range(N_DEV - 1):
            g = g + s_ref[peer].astype(F32)
        g_ref[...] = g
        d_ref[...], m2_ref[...], v2_ref[...] = _adamw_math(w_ref[...], g, m_ref[...], v_ref[...])

    return pl.pallas_call(
        body, name=name, grid=grid, in_specs=[blk, landed_blk, blk, blk, blk],
        out_specs=[blk] * 4, out_shape=[jax.ShapeDtypeStruct((rows, cols), F32)] * 4,
        compiler_params=_cp("parallel"),
    )(own, landed, w, m, v)


def _sum_slabs(name, slabs, rb):
    _, rows, cols = slabs.shape

    def body(s_ref, o_ref):
        g = s_ref[0]
        for dev in range(1, N_DEV):
            g = g + s_ref[dev]
        o_ref[...] = g

    return pl.pallas_call(
        body, name=name, grid=(rows // rb,),
        in_specs=[pl.BlockSpec((N_DEV, rb, cols), lambda i: (0, i, 0))],
        out_specs=pl.BlockSpec((rb, cols), lambda i: (i, 0)),
        out_shape=jax.ShapeDtypeStruct((rows, cols), F32), compiler_params=_cp("parallel"),
    )(slabs)


def _adamw_small(w, g, m, v):
    def body(w_ref, g_ref, m_ref, v_ref, d_ref, m2_ref, v2_ref):
        d_ref[...], m2_ref[...], v2_ref[...] = _adamw_math(w_ref[...], g_ref[...], m_ref[...], v_ref[...])

    return pl.pallas_call(body, name="adamw_small", out_shape=[jax.ShapeDtypeStruct(w.shape, F32)] * 3)(w, g, m, v)


def _place():
    return lax.axis_index("x"), lax.axis_index("y"), lax.axis_index("c")


def _index(p):
    return 4 * p[0] + 2 * p[1] + p[2]


def _all_gather(name, xs):
    n = len(xs)

    def body(*refs):
        x_refs, o_refs = refs[:n], refs[n:2 * n]
        send_sems, recv_sems, local_sems = refs[2 * n:]
        x, y, c = _place()
        me, sibling = (x, y, c), (x, y, 1 - c)
        chips = [(1 - x, y), (x, 1 - y), (1 - x, 1 - y)]

        def copy(i, k, block, to, src=None):
            dst = o_refs[i].at[_index(block)]
            return pltpu.make_async_remote_copy(src_ref=dst if src is None else src, dst_ref=dst,
                                                send_sem=send_sems.at[i, k], recv_sem=recv_sems.at[i, k],
                                                device_id=to, device_id_type=MESH_ID)

        mine = [pltpu.make_async_copy(x_refs[i], o_refs[i].at[_index(me)], local_sems.at[i]) for i in range(n)]
        for cp in mine:
            cp.start()
        first = []
        for i in range(n):
            first.append(copy(i, 0, me, sibling, src=x_refs[i]))
            first += [copy(i, 1 + j, me, (*chip, c), src=x_refs[i]) for j, chip in enumerate(chips)]
        for cp in first:
            cp.start()
        passed = []
        for j, chip in enumerate(chips):
            for i in range(n):
                copy(i, 1 + j, (*chip, c), me).wait_recv()
                cp = copy(i, 4 + j, (*chip, c), sibling)
                cp.start()
                passed.append(cp)
        for i in range(n):
            copy(i, 0, sibling, me).wait_recv()
            for j, chip in enumerate(chips):
                copy(i, 4 + j, (*chip, 1 - c), me).wait_recv()
        for cp in first + passed:
            cp.wait_send()
        for cp in mine:
            cp.wait()

    return pl.pallas_call(
        body, name=name, in_specs=[ANY] * n, out_specs=[ANY] * n,
        out_shape=[jax.ShapeDtypeStruct((N_DEV,) + x.shape, x.dtype) for x in xs],
        scratch_shapes=[pltpu.SemaphoreType.DMA((n, 7)), pltpu.SemaphoreType.DMA((n, 7)), pltpu.SemaphoreType.DMA((n,))],
    )(*xs)


def _exchange_start(name, xs, after=None, gather=False):
    n = len(xs)
    copies = n * (N_DEV - 1)
    extra = [] if after is None else [after]

    def body(*refs):
        x_refs, land_refs = refs[:n], refs[n:2 * n]
        sems = refs[2 * n + len(extra):2 * n + len(extra) + 2 * copies]
        token = refs[-1]
        for i, k, peer in _exchange_copies(n):
            _exchange_copy(x_refs, land_refs, sems, i, k, peer, gather).start()
        token[...] = jnp.zeros_like(token)

    lands = [lax.empty((N_DEV,) + x.shape if gather else (N_DEV - 1,) + x.shape[1:], x.dtype) for x in xs]
    out = pl.pallas_call(
        body, name=name,
        out_shape=(*[pltpu.SemaphoreType.DMA(())] * (2 * copies), *[pltpu.HBM(x.shape, x.dtype) for x in xs],
                   *[pltpu.HBM(l.shape, l.dtype) for l in lands], jax.ShapeDtypeStruct((8, 128), F32)),
        in_specs=[HBM_SPEC] * (2 * n) + [ANY] * len(extra),
        out_specs=(*[SEM_SPEC] * (2 * copies), *[HBM_SPEC] * (2 * n), pl.BlockSpec(memory_space=pltpu.VMEM)),
        input_output_aliases={i: 2 * copies + i for i in range(2 * n)},
        compiler_params=pltpu.CompilerParams(has_side_effects=pltpu.SideEffectType.DATAFLOW_SIDE_EFFECTING),
    )(*[pltpu.with_memory_space_constraint(a, pltpu.HBM) for a in list(xs) + lands], *extra)
    sems, rest = list(out[:2 * copies]), out[2 * copies:]
    return sems, list(rest[:n]), list(rest[n:2 * n]), rest[-1]


def _exchange_copies(n):
    x, y, c = _place()
    for k in range(1, N_DEV):
        peer = ((1 - x) if k & 4 else x, (1 - y) if k & 2 else y, (1 - c) if k & 1 else c)
        for i in range(n):
            yield i, k - 1, peer


def _exchange_copy(x_refs, land_refs, sems, i, k, peer, gather, arriving=False):
    copies = len(sems) // 2
    which = i * (N_DEV - 1) + k
    src = x_refs[i] if gather else x_refs[i].at[_index(peer)]
    dst = land_refs[i].at[_index(peer if arriving else _place())] if gather else land_refs[i].at[k]
    return pltpu.make_async_remote_copy(src_ref=src, dst_ref=dst, send_sem=sems[which], recv_sem=sems[copies + which],
                                        device_id=peer, device_id_type=MESH_ID)


def _exchange_wait(name, sems, xs, lands, after, gather=False):
    n = len(xs)

    def body(*refs):
        x_refs, land_refs = refs[:n], refs[n:2 * n]
        sem_refs = refs[2 * n:2 * n + len(sems)]
        for i, k, peer in _exchange_copies(n):
            _exchange_copy(x_refs, land_refs, sem_refs, i, k, peer, gather).wait_send()
            _exchange_copy(x_refs, land_refs, sem_refs, i, k, peer, gather, arriving=True).wait_recv()

    out = pl.pallas_call(
        body, name=name,
        out_shape=(*[pltpu.HBM(x.shape, x.dtype) for x in xs], *[pltpu.HBM(l.shape, l.dtype) for l in lands]),
        in_specs=[HBM_SPEC] * (2 * n) + [SEM_SPEC] * len(sems) + [ANY], out_specs=tuple([HBM_SPEC] * (2 * n)),
        input_output_aliases={i: i for i in range(2 * n)},
        compiler_params=pltpu.CompilerParams(has_side_effects=pltpu.SideEffectType.DATAFLOW_SIDE_EFFECTING),
    )(*xs, *lands, *sems, after)
    return list(out[:n]), list(out[n:])


def _pack(arrays):
    flat = jnp.concatenate([a.reshape(-1) for a in arrays])
    pad = (-flat.shape[0]) % 1024
    return jnp.pad(flat, (0, pad)).reshape(-1, 128)


def _unpack(packed, shapes):
    flat = packed.reshape(-1)
    out, pos = [], 0
    for s in shapes:
        size = 1
        for dim in s:
            size *= dim
        out.append(flat[pos:pos + size].reshape(s))
        pos += size
    return out


def _cols_from_slabs(stack):
    return jnp.transpose(stack, (1, 0, 2)).reshape(stack.shape[1], -1)


def kernel(x, meta_tokens, ffn1_norm, ffn1_w_gu, ffn1_w_down, mix_norm, w_in, a_conv_w, a_log_rate, a_dt_bias, a_out_norm, b_shift_mu, b_w0, b_w_up, b_a0, b_a_up, b_g_up, b_k_k, b_k_a, b_r_k, b_ln_gain, b_ln_bias, w_out, ffn2_norm, ffn2_w_gu, ffn2_w_down, final_norm, loss_target, m_meta_tokens, m_ffn1_norm, m_ffn1_w_gu, m_ffn1_w_down, m_mix_norm, m_w_in, m_a_conv_w, m_a_log_rate, m_a_dt_bias, m_a_out_norm, m_b_shift_mu, m_b_w0, m_b_w_up, m_b_a0, m_b_a_up, m_b_g_up, m_b_k_k, m_b_k_a, m_b_r_k, m_b_ln_gain, m_b_ln_bias, m_w_out, m_ffn2_norm, m_ffn2_w_gu, m_ffn2_w_down, m_final_norm, v_meta_tokens, v_ffn1_norm, v_ffn1_w_gu, v_ffn1_w_down, v_mix_norm, v_w_in, v_a_conv_w, v_a_log_rate, v_a_dt_bias, v_a_out_norm, v_b_shift_mu, v_b_w0, v_b_w_up, v_b_a0, v_b_a_up, v_b_g_up, v_b_k_k, v_b_k_a, v_b_r_k, v_b_ln_gain, v_b_ln_bias, v_w_out, v_ffn2_norm, v_ffn2_w_gu, v_ffn2_w_down, v_final_norm):
    names = ['meta_tokens', 'ffn1_norm', 'ffn1_w_gu', 'ffn1_w_down', 'mix_norm', 'w_in', 'a_conv_w', 'a_log_rate',
             'a_dt_bias', 'a_out_norm', 'b_shift_mu', 'b_w0', 'b_w_up', 'b_a0', 'b_a_up', 'b_g_up', 'b_k_k', 'b_k_a',
             'b_r_k', 'b_ln_gain', 'b_ln_bias', 'w_out', 'ffn2_norm', 'ffn2_w_gu', 'ffn2_w_down', 'final_norm']
    env = dict(locals())
    wts = {k: env[k] for k in names}
    mom_m = {k: env['m_' + k] for k in names}
    mom_v = {k: env['v_' + k] for k in names}
    big = ['ffn1_w_gu', 'ffn1_w_down', 'w_in', 'w_out', 'ffn2_w_gu', 'ffn2_w_down']
    col_sharded = {'ffn1_w_gu', 'w_in', 'ffn2_w_gu'}
    shard_of = lambda tree, k: tree[k][0].T if k in col_sharded else tree[k][0]
    small_sharded = ['meta_tokens', 'a_conv_w', 'b_w_up', 'b_a_up', 'b_g_up']
    replicated = [k for k in names if k not in big and k not in small_sharded]

    seq, d = x.shape[1], x.shape[2]
    rows = PAD + N_META + seq
    heads_a = d // HEAD_A
    tb_mm = _tb(rows, 416)
    tb_vjp = _tb(rows, 208)
    tb_dw = _tb(rows, 2080)
    tb_ffn = _tb(rows, 832)
    me = _index(_place())

    local_bf = {k: shard_of(wts, k).astype(BF16) for k in big}
    gu1, down1, meta = _all_gather("gather_ffn1", [local_bf['ffn1_w_gu'], local_bf['ffn1_w_down'], wts['meta_tokens']])
    small_rest = small_sharded[1:]
    late_keys = ['w_out', 'ffn2_w_gu', 'ffn2_w_down']
    gather_mid = _exchange_start("gather_start_mid", [local_bf['w_in'], _pack([wts[k][0] for k in small_rest])],
                                 after=gu1, gather=True)

    def finish_gather(tag, started, after):
        sems, mine, lands, _ = started
        mine, lands = _exchange_wait("gather_wait_" + tag, sems, mine, lands, after, gather=True)
        return [lax.dynamic_update_index_in_dim(land, own[None], me, 0) for land, own in zip(lands, mine)]

    full = {'ffn1_w_gu': gu1, 'ffn1_w_down': down1.reshape(-1, d), 'meta_tokens': _cols_from_slabs(meta)}
    for k in replicated:
        full[k] = wts[k].reshape(1, -1)

    h0 = jnp.concatenate([jnp.zeros((PAD, d), F32) + gather_mid[-1][:1, :1], full['meta_tokens'], x[0]], axis=0)
    h1 = _ffn_fwd("ffn1_fwd", h0, full['ffn1_norm'], full['ffn1_w_gu'], full['ffn1_w_down'], tb_ffn)
    (u,) = _tok_fwd("mix_norm_fwd", _norm_fn, [h1], [full['mix_norm']], [(d, BF16)], tb_mm)

    win_stack, small_stack = finish_gather("mid", gather_mid, u)
    gather_late = _exchange_start("gather_start_late", [local_bf[k] for k in late_keys], after=win_stack, gather=True)
    full['w_in'] = win_stack.reshape(-1, d)
    small_flat, pos = small_stack.reshape(N_DEV, -1), 0
    for k in small_rest:
        shape = wts[k][0].shape
        full[k] = _cols_from_slabs(small_flat[:, pos:pos + shape[0] * shape[1]].reshape((N_DEV,) + shape))
        pos += shape[0] * shape[1]

    win = full['w_in']
    n_b = 3 * d + LORA_W + LORA_A + LORA_G
    off_beta, off_b = 4 * d, 4 * d + 2 * heads_a
    off_ga = off_b + n_b
    b_width = 3 * d + LORA_PAD
    zrows = lambda r: jnp.zeros((r, d), BF16)
    w_qkv = win[:3 * d]
    w_zg = jnp.concatenate([win[3 * d:4 * d], win[off_ga:off_ga + 2 * d]], axis=0)
    w_b = jnp.concatenate([win[off_b:off_b + n_b], zrows(b_width - n_b)], axis=0)
    w_bg = jnp.concatenate([win[off_beta:off_beta + 2 * heads_a], zrows(128 - 2 * heads_a)], axis=0)

    def lanes(vec, start, width):
        return jnp.pad(vec.reshape(1, -1), ((0, 0), (start, width - start - vec.size)))

    log_rate = lanes(wts['a_log_rate'], heads_a, 128)
    dt_bias = lanes(wts['a_dt_bias'], heads_a, 128)
    mu = lanes(wts['b_shift_mu'], 0, b_width)
    w_up = jnp.pad(full['b_w_up'], ((0, 128 - LORA_W), (0, 0)))
    a_up = jnp.pad(full['b_a_up'], ((LORA_W, 0), (0, 0)))
    g_up = jnp.pad(full['b_g_up'], ((0, 256 - LORA_G), (0, 0)))
    b_pars = [full['b_w0'], w_up, full['b_a0'], a_up, g_up, full['b_k_k'], full['b_k_a']]
    post_pars = [full['a_out_norm'], full['b_r_k'], full['b_ln_gain'], full['b_ln_bias']]
    bg_fn = _make_bg_fn(heads_a)

    z_qkv = _mm("in_qkv", u, w_qkv, trans_b=True, after=gather_late[-1], tb=tb_mm, tn=_col_tile(3 * d, 1536))
    z_zg = _mm("in_zg", u, w_zg, trans_b=True, tb=tb_mm, tn=_col_tile(3 * d, 1536))
    z_b = _mm("in_b", u, w_b, trans_b=True, tb=tb_mm, tn=_col_tile(b_width, 1536))
    z_bg = _mm("in_bg", u, w_bg, trans_b=True, tb=tb_mm, tn=128)
    qkv = _a_pre_fwd(z_qkv, full['a_conv_w'])
    (bg,) = _tok_fwd("bg_fwd", bg_fn, [z_bg], [log_rate, dt_bias], [(128, F32)], tb_mm)
    o_dn, dn_hist, dn_tinv = _dn_fwd(qkv, bg)
    zf = _shift_fwd(z_b, mu)
    rr, ww, kk2, vv, av, bv, gate = _tok_fwd("b_pre_fwd", _b_pre_fn, [zf], b_pars, [(d, F32)] * 7, tb_vjp)
    per_head = lambda t: t.reshape(rows, d // HEAD_B, HEAD_B)
    y_heads, b_hist, b_last = _rwkv_fwd(rr, ww, kk2, per_head(vv), av, bv)
    y_b = y_heads.reshape(rows, d)
    w_out_stack, full['ffn2_w_gu'], down2 = finish_gather("late", gather_late, y_heads)
    full['w_out'], full['ffn2_w_down'] = w_out_stack.reshape(-1, d), down2.reshape(-1, d)
    post_toks = [o_dn, z_zg, y_b, rr, kk2, vv, gate]
    (merged,) = _tok_fwd("post_fwd", _post_fn, post_toks, post_pars, [(d, BF16)], tb_vjp)
    h2 = _mm("out_proj", merged, full['w_out'], add=h1, tb=tb_mm, tn=d)
    h3 = _ffn_fwd("ffn2_fwd", h2, full['ffn2_norm'], full['ffn2_w_gu'], full['ffn2_w_down'], tb_ffn)

    target = jnp.pad(loss_target[0], ((CHUNK, 0), (0, 0)))
    dh3, g_final, loss_part = _loss(h3, target, full['final_norm'].reshape(1, d), tb_vjp)

    def ffn_backward(tag, h, dout, key_norm, key_gu, key_down, once_down=None):
        dh, dh_bf, dgain, xn, act, dgate, dup, dhalf = _ffn_bwd(tag + "_bwd", h, full[key_norm], dout, full[key_gu],
                                                                full[key_down], tb_mm)
        fc = dgate.shape[2]
        d_down = _mm_tn_from_slabs(tag + "_dw_down", act, dhalf, tk=tb_dw).reshape(N_DEV, -1, d)
        token = once_down(dh, dgain, d_down) if once_down else None
        d_gu = jnp.concatenate(
            [_mm_tn_from_slabs(tag + "_dw_gate", dgate, xn, tk=tb_dw, after=token).reshape(-1, fc, d),
             _mm_tn_from_slabs(tag + "_dw_up", dup, xn, tk=tb_dw).reshape(-1, fc, d)], axis=0)
        return dh, dh_bf, dgain, d_gu, d_down

    dh2, dh2_bf, g_ffn2_norm, g_ffn2_gu, g_ffn2_down = ffn_backward("ffn2", h2, dh3, 'ffn2_norm', 'ffn2_w_gu',
                                                                    'ffn2_w_down')
    g_w_out = _mm_tn("dw_out", merged, dh2_bf, tm=d, tn=d, tk=tb_dw).reshape(N_DEV, -1, d)

    def start_exchange(tag, keys, slabs, after=None):
        sems, kept, lands, token = _exchange_start("exchange_start_" + tag, slabs, after)
        return (tag, keys, sems, kept, lands), token

    ex_ffn2, token_ffn2 = start_exchange("ffn2", ['ffn2_w_gu', 'ffn2_w_down', 'w_out'], [g_ffn2_gu, g_ffn2_down, g_w_out])
    dmerged = _mm("d_merged", dh2_bf, full['w_out'], trans_b=True, after=token_ffn2, tb=tb_mm, tn=d)
    post_grads = _tok_bwd("post_bwd", _post_fn, post_toks, post_pars, [[dmerged]], list(range(7)), tb_vjp,
                          [F32, BF16] + [F32] * 5)
    do_dn, dz_zg, dy_b, dr1, dk1, dv1, dgate = post_grads[:7]
    g_out_norm, g_r_k, g_ln_g, g_ln_b = post_grads[7:]
    dr2, dw2, dk2, dv_heads, da2, db2 = _rwkv_bwd(rr, ww, kk2, per_head(vv), av, bv, b_hist, b_last, per_head(dy_b))
    dv2 = dv_heads.reshape(rows, d)
    b_grads = _tok_bwd("b_pre_bwd", _b_pre_fn, [zf], b_pars,
                       [[dr1, dr2], [dw2], [dk1, dk2], [dv1, dv2], [da2], [db2], [dgate]], [0], tb_vjp)
    dzf = b_grads[0]
    g_w0, g_w_up, g_a0, g_a_up, g_g_up, g_k_k, g_k_a = b_grads[1:]
    dz_b, g_mu = _shift_bwd(z_b, mu, dzf)
    dqkv, dbg = _dn_bwd(qkv, bg, dn_hist, dn_tinv, do_dn)
    dz_qkv, g_conv = _a_pre_bwd(z_qkv, full['a_conv_w'], dqkv)
    dz_bg, g_log_rate, g_dt_bias = _tok_bwd("bg_bwd", bg_fn, [z_bg], [log_rate, dt_bias], [[dbg]], [0], tb_mm, [BF16])

    small_early = {
        'a_conv_w': g_conv, 'a_log_rate': g_log_rate[:, heads_a:2 * heads_a],
        'a_dt_bias': g_dt_bias[:, heads_a:2 * heads_a], 'a_out_norm': g_out_norm, 'b_shift_mu': g_mu[:, :n_b],
        'b_w0': g_w0, 'b_w_up': g_w_up[:LORA_W], 'b_a0': g_a0, 'b_a_up': g_a_up[LORA_W:], 'b_g_up': g_g_up[:LORA_G],
        'b_k_k': g_k_k, 'b_k_a': g_k_a, 'b_r_k': g_r_k, 'b_ln_gain': g_ln_g, 'b_ln_bias': g_ln_b,
        'ffn2_norm': g_ffn2_norm, 'final_norm': g_final,
    }
    gather_small = _exchange_start("gather_start_small", [_pack(list(small_early.values()))], gather=True)

    du = None
    g_w_in_parts = []
    for tag, dz, wpiece in (("qkv", dz_qkv, w_qkv), ("zg", dz_zg, w_zg), ("b", dz_b, w_b), ("bg", dz_bg, w_bg)):
        du = _mm("du_" + tag, dz, wpiece, add=du, after=gather_small[-1] if du is None else None, tb=tb_mm, tn=d)
        g_w_in_parts.append(_mm_tn("dw_in_" + tag, dz, u, tm=_col_tile(dz.shape[1], 1536), tn=d, tk=tb_dw))
    gp_qkv, gp_zg, gp_b, gp_bg = g_w_in_parts
    g_w_in = jnp.concatenate([gp_qkv, gp_zg[:d], gp_bg[:2 * heads_a], gp_b[:n_b], gp_zg[d:]],
                             axis=0).reshape(N_DEV, -1, d)
    ex_w_in, token_w_in = start_exchange("w_in", ['w_in'], [g_w_in])
    dh1, g_mix_norm = _tok_bwd("mix_norm_bwd", _norm_res_fn, [h1], [full['mix_norm']], [[du], [dh2]], [0], tb_vjp,
                               after=token_w_in)
    tail = {}

    def once_ffn1_down(dh0, g_ffn1_norm, g_ffn1_down):
        tail['late'] = {'meta_tokens': dh0[PAD:CHUNK], 'ffn1_norm': g_ffn1_norm, 'mix_norm': g_mix_norm}
        tail['packed'] = _pack(list(tail['late'].values()) + [loss_part[:, :1]])
        (tail['parts'],) = _all_gather("gather_small_grads", [tail['packed']])
        tail['ex_down'], token = start_exchange("ffn1_down", ['ffn1_w_down'], [g_ffn1_down], after=tail['parts'])
        return token

    dh0, _, _, g_ffn1_gu, _ = ffn_backward("ffn1", h0, dh1, 'ffn1_norm', 'ffn1_w_gu', 'ffn1_w_down', once_ffn1_down)
    ex_ffn1_gu, _ = start_exchange("ffn1_gu", ['ffn1_w_gu'], [g_ffn1_gu])
    small_late, packed_late, late_parts = tail['late'], tail['packed'], tail['parts']
    (early_parts,) = finish_gather("small", gather_small, late_parts)
    pieces = (_unpack(_sum_slabs("sum_small_early", early_parts, early_parts.shape[1]),
                      [g.shape for g in small_early.values()])
              + _unpack(_sum_slabs("sum_small_grads", late_parts, packed_late.shape[0]),
                        [g.shape for g in small_late.values()] + [(1, 1)]))
    small_names = list(small_early) + list(small_late)
    small_grad = dict(zip(small_names, pieces[:-1]))
    loss = pieces[-1].reshape(())

    grads, deltas, new_m, new_v = {}, {}, {}, {}
    local_small = {}
    for k in small_names:
        g = small_grad[k]
        if k in small_sharded:
            width = wts[k].shape[-1]
            g = lax.dynamic_slice_in_dim(g, me * width, width, axis=1)
        local_small[k] = g.reshape(wts[k].shape)
    pk = lambda tree: _pack([tree[k] for k in small_names])
    dl_s, m_s, v_s = _adamw_small(pk(wts), pk(local_small), pk(mom_m), pk(mom_v))
    shapes = [wts[k].shape for k in small_names]
    for k, dl, m2, v2 in zip(small_names, _unpack(dl_s, shapes), _unpack(m_s, shapes), _unpack(v_s, shapes)):
        grads[k], deltas[k], new_m[k], new_v[k] = local_small[k], dl, m2, v2

    done = dl_s
    for tag, keys, sems, kept, lands in (ex_ffn2, ex_w_in, tail['ex_down'], ex_ffn1_gu):
        kept, lands = _exchange_wait("exchange_wait_" + tag, sems, kept, lands, done)
        for k, slabs, landed in zip(keys, kept, lands):
            own = lax.dynamic_index_in_dim(slabs, me, axis=0, keepdims=False)
            res = _adamw("adamw_" + k, own, landed, shard_of(wts, k), shard_of(mom_m, k), shard_of(mom_v, k))
            done = res[1]
            res = [(t.T if k in col_sharded else t)[None] for t in res]
            grads[k], deltas[k], new_m[k], new_v[k] = res

    grad_x = dh0[CHUNK:][None]
    return (loss, grad_x, *[grads[k] for k in names], *[deltas[k] for k in names],
            *[new_m[k] for k in names], *[new_v[k] for k in names])
```

```python
import functools

import jax
import jax.numpy as jnp
from jax import lax
from jax.experimental import pallas as pl
from jax.experimental.pallas import tpu as pltpu

F32 = jnp.float32
BF16 = jnp.bfloat16
N_DEV = 8
N_META = 16
CHUNK = 64
PAD = CHUNK - N_META
HEAD_A = 128
HEAD_B = 64
LORA_W, LORA_A, LORA_G = 64, 64, 160
LORA_PAD = 384
EPS = 1e-6
GN_EPS = HEAD_B * 1e-5
ADAM_LR, ADAM_B1, ADAM_B2, ADAM_EPS, ADAM_WD, ADAM_STEP = 0.001, 0.9, 0.999, 1e-08, 0.01, 10
SCAN_STEPS = 32
MXU_WIDTH = 256
VMEM_LIMIT = 56 * 1024 * 1024
DN_PRECISION = lax.Precision.HIGH
MESH_ID = pl.DeviceIdType.MESH
ANY = pl.BlockSpec(memory_space=pl.ANY)
HBM_SPEC = pl.BlockSpec(memory_space=pltpu.HBM)
SEM_SPEC = pl.BlockSpec(memory_space=pltpu.SEMAPHORE)


def _cp(*sem):
    return pltpu.CompilerParams(dimension_semantics=sem, vmem_limit_bytes=VMEM_LIMIT)


def _tb(t, target):
    best = 16
    for d in range(16, target + 1, 16):
        if t % d == 0:
            best = d
    return best


def _sigmoid(x):
    return 1.0 / (1.0 + jnp.exp(-x))


def _silu(x):
    return x * _sigmoid(x)


def _softplus(x):
    return jnp.maximum(x, 0.0) + jnp.log(1.0 + jnp.exp(-jnp.abs(x)))


def _dot_nt(a, b, precision=None):
    return lax.dot_general(a, b, (((1,), (1,)), ((), ())), preferred_element_type=F32, precision=precision)


def _dot_tn(a, b, precision=None):
    return lax.dot_general(a, b, (((0,), (0,)), ((), ())), preferred_element_type=F32, precision=precision)


def _dot(a, b, precision=None):
    return jnp.dot(a, b, preferred_element_type=F32, precision=precision)


def _block_diag_ones():
    i = lax.broadcasted_iota(jnp.int32, (MXU_WIDTH, MXU_WIDTH), 0) // HEAD_B
    j = lax.broadcasted_iota(jnp.int32, (MXU_WIDTH, MXU_WIDTH), 1) // HEAD_B
    return (i == j).astype(BF16)


def _hi_lo(x):
    hi = x.astype(BF16)
    return hi, (x - hi.astype(F32)).astype(BF16)


def _segsum_many(xs, bd):
    groups = [x if isinstance(x, tuple) else _hi_lo(x) for x in xs]
    rows = groups[0][0].shape[0]
    stacked = jnp.concatenate([p for grp in groups for p in grp], axis=0)
    out = jnp.concatenate([_dot(stacked[:, s:s + MXU_WIDTH], bd) for s in range(0, stacked.shape[1], MXU_WIDTH)], axis=1)
    res, pos = [], 0
    for grp in groups:
        acc = out[pos:pos + rows]
        for j in range(1, len(grp)):
            acc = acc + out[pos + j * rows:pos + (j + 1) * rows]
        res.append(acc)
        pos += len(grp) * rows
    return res


def _segsum_impl(x):
    return _segsum_many([x], _block_diag_ones())[0]


@jax.custom_vjp
def _segsum64(x):
    return _segsum_impl(x)


_segsum64.defvjp(lambda x: (_segsum_impl(x), None), lambda _, ct: (_segsum_impl(ct),))


def _tok(t):
    return t if isinstance(t, tuple) else (t, t.shape[1], 0)


def _tok_spec(tb, width, colblk):
    return pl.BlockSpec((tb, width), lambda i: (i, colblk))


def _par_spec(p):
    return pl.BlockSpec(p.shape, lambda i: (0, 0))


def _tok_fwd(name, fn, toks, pars, outs, tb):
    toks = [_tok(t) for t in toks]
    rows = toks[0][0].shape[0]
    n_in = len(toks) + len(pars)

    def body(*refs):
        row0 = pl.program_id(0) * tb
        res = fn(row0, *[r[...] for r in refs[:n_in]])
        for r, o in zip(refs[n_in:], res):
            r[...] = o.astype(r.dtype)

    return pl.pallas_call(
        body, name=name, grid=(rows // tb,),
        in_specs=[_tok_spec(tb, w, c) for _, w, c in toks] + [_par_spec(p) for p in pars],
        out_specs=[_tok_spec(tb, w, 0) for w, _ in outs],
        out_shape=[jax.ShapeDtypeStruct((rows, w), dt) for w, dt in outs],
        compiler_params=_cp("parallel"),
    )(*[a for a, _, _ in toks], *pars)


def _tok_bwd(name, fn, toks, pars, cts, want, tb, want_dtypes=None, after=None):
    toks = [_tok(t) for t in toks]
    want_dtypes = want_dtypes or [F32] * len(want)
    cts = [[_tok(c) for c in group] for group in cts]
    flat_cts = [c for group in cts for c in group]
    rows = toks[0][0].shape[0]
    n_tok, n_par, n_ct = len(toks), len(pars), len(flat_cts)
    extra = [] if after is None else [after]

    def body(*refs):
        i = pl.program_id(0)
        row0 = i * tb
        prim = [r[...].astype(F32) for r in refs[:n_tok + n_par]]
        ct_refs = list(refs[n_tok + n_par:n_tok + n_par + n_ct])
        out_refs = refs[n_tok + n_par + n_ct + len(extra):]
        res, vjp = jax.vjp(lambda *a: fn(row0, *a), *prim)
        ct = []
        for group, o in zip(cts, res):
            acc = None
            for _ in group:
                v = ct_refs.pop(0)[...].astype(F32)
                acc = v if acc is None else acc + v
            ct.append(acc.astype(o.dtype))
        grads = vjp(tuple(ct))
        for r, k in zip(out_refs[:len(want)], want):
            r[...] = grads[k].astype(r.dtype)

        @pl.when(i == 0)
        def _():
            for r in out_refs[len(want):]:
                r[...] = jnp.zeros_like(r)

        for r, g in zip(out_refs[len(want):], grads[n_tok:]):
            r[...] += g

    return pl.pallas_call(
        body, name=name, grid=(rows // tb,),
        in_specs=[_tok_spec(tb, w, c) for _, w, c in toks] + [_par_spec(p) for p in pars]
        + [_tok_spec(tb, w, c) for _, w, c in flat_cts] + [ANY] * len(extra),
        out_specs=[_tok_spec(tb, toks[k][1], 0) for k in want] + [_par_spec(p) for p in pars],
        out_shape=[jax.ShapeDtypeStruct((rows, toks[k][1]), dt) for k, dt in zip(want, want_dtypes)]
        + [jax.ShapeDtypeStruct(p.shape, F32) for p in pars],
        compiler_params=_cp("arbitrary"),
    )(*[a for a, _, _ in toks], *pars, *[a for a, _, _ in flat_cts], *extra)


def _mm(name, a, b, *, trans_b=False, add=None, after=None, tb, tn):
    rows, k = a.shape
    n = b.shape[0] if trans_b else b.shape[1]

    def body(*refs):
        a_ref, b_ref = refs[:2]
        o_ref = refs[-1]
        acc = _dot_nt(a_ref[...], b_ref[...]) if trans_b else _dot(a_ref[...], b_ref[...])
        if add is not None:
            acc = acc + refs[2][...]
        o_ref[...] = acc

    in_specs = [pl.BlockSpec((tb, k), lambda i, j: (i, 0)),
                pl.BlockSpec((tn, k), lambda i, j: (j, 0)) if trans_b else pl.BlockSpec((k, tn), lambda i, j: (0, j))]
    args = [a, b]
    if add is not None:
        in_specs.append(pl.BlockSpec((tb, tn), lambda i, j: (i, j)))
        args.append(add)
    if after is not None:
        in_specs.append(ANY)
        args.append(after)
    return pl.pallas_call(
        body, name=name, grid=(rows // tb, n // tn), in_specs=in_specs,
        out_specs=pl.BlockSpec((tb, tn), lambda i, j: (i, j)),
        out_shape=jax.ShapeDtypeStruct((rows, n), F32),
        compiler_params=_cp("parallel", "parallel"),
    )(*args)


def _mm_tn_call(name, grid, a, b, a_spec, b_spec, o_spec, acc_shape, out_shape, after=None):
    last = len(grid) - 1
    extra = [] if after is None else [after]

    def body(a_ref, b_ref, *rest):
        o_ref, acc_ref = rest[len(extra):]
        k = pl.program_id(last)

        @pl.when(k == 0)
        def _():
            acc_ref[...] = jnp.zeros_like(acc_ref)

        a_blk = a_ref[0] if len(a_ref.shape) == 3 else a_ref[...]
        b_blk = b_ref[0] if len(b_ref.shape) == 3 else b_ref[...]
        acc_ref[...] += _dot_tn(a_blk, b_blk)

        @pl.when(k == grid[last] - 1)
        def _():
            if len(o_ref.shape) == 3:
                o_ref[0] = acc_ref[...].astype(o_ref.dtype)
            else:
                o_ref[...] = acc_ref[...].astype(o_ref.dtype)

    return pl.pallas_call(
        body, name=name, grid=grid, in_specs=[a_spec, b_spec] + [ANY] * len(extra), out_specs=o_spec,
        out_shape=jax.ShapeDtypeStruct(out_shape, BF16), scratch_shapes=[pltpu.VMEM(acc_shape, F32)],
        compiler_params=_cp(*(["parallel"] * last + ["arbitrary"])),
    )(a, b, *extra)


def _mm_tn(name, a, b, *, tm, tn, tk):
    rows, m = a.shape
    n = b.shape[1]
    return _mm_tn_call(name, (m // tm, n // tn, rows // tk), a, b,
                       pl.BlockSpec((tk, tm), lambda i, j, k: (k, i)), pl.BlockSpec((tk, tn), lambda i, j, k: (k, j)),
                       pl.BlockSpec((tm, tn), lambda i, j, k: (i, j)), (tm, tn), (m, n))


def _mm_tn_from_slabs(name, a3, b, *, tk, after=None):
    s, rows, c = a3.shape
    n = b.shape[1]
    return _mm_tn_call(name, (s, rows // tk), a3, b,
                       pl.BlockSpec((1, tk, c), lambda i, k: (i, k, 0)), pl.BlockSpec((tk, n), lambda i, k: (k, 0)),
                       pl.BlockSpec((c, n), lambda i, k: (i, 0)), (c, n), (s * c, n), after)


def _col_tile(n, target):
    if n <= target:
        return n
    best = 128
    for d in range(128, target + 1, 128):
        if n % d == 0:
            best = d
    return best


def _rms(x, gain):
    return x * lax.rsqrt(jnp.mean(x * x, axis=-1, keepdims=True) + EPS) * gain


def _ffn_specs(d, fc, nj):
    return [pl.BlockSpec((1, fc, d), lambda i, j: (j, 0, 0)), pl.BlockSpec((1, fc, d), lambda i, j: (nj + j, 0, 0)),
            pl.BlockSpec((fc, d), lambda i, j: (j, 0))]


def _ffn_fwd(name, h, gain, wgu, wd, tb):
    rows, d = h.shape
    nj = wgu.shape[0] // 2
    fc = wgu.shape[1]

    def body(h_ref, g_ref, wg_ref, wu_ref, wd_ref, o_ref, xn_s, acc_s):
        j = pl.program_id(1)

        @pl.when(j == 0)
        def _():
            xn_s[...] = _rms(h_ref[...], g_ref[...]).astype(BF16)
            acc_s[...] = jnp.zeros_like(acc_s)

        wg, wu, wdn = wg_ref[0], wu_ref[0], wd_ref[...]
        for half in range(2):
            rs = pl.ds(half * (tb // 2), tb // 2)
            xn = xn_s[rs, :]
            gate = _dot_nt(xn, wg)
            up = _dot_nt(xn, wu)
            acc_s[rs, :] += _dot((_silu(gate) * up).astype(BF16), wdn)

        @pl.when(j == nj - 1)
        def _():
            o_ref[...] = h_ref[...] + 0.5 * acc_s[...]

    return pl.pallas_call(
        body, name=name, grid=(rows // tb, nj),
        in_specs=[pl.BlockSpec((tb, d), lambda i, j: (i, 0)), pl.BlockSpec((1, d), lambda i, j: (0, 0))]
        + _ffn_specs(d, fc, nj),
        out_specs=pl.BlockSpec((tb, d), lambda i, j: (i, 0)),
        out_shape=jax.ShapeDtypeStruct((rows, d), F32),
        scratch_shapes=[pltpu.VMEM((tb, d), BF16), pltpu.VMEM((tb, d), F32)],
        compiler_params=_cp("parallel", "arbitrary"),
    )(h, gain, wgu, wgu, wd)


def _ffn_fwd_loss(name, h, gain, wgu, wd, target, final_gain, tb):
    rows, d = h.shape
    nj = wgu.shape[0] // 2
    fc = wgu.shape[1]

    def body(h_ref, g_ref, wg_ref, wu_ref, wd_ref, t_ref, fg_ref, dh_ref, dg_ref, l_ref, xn_s, acc_s):
        i, j = pl.program_id(0), pl.program_id(1)

        @pl.when(j == 0)
        def _():
            xn_s[...] = _rms(h_ref[...], g_ref[...]).astype(BF16)
            acc_s[...] = jnp.zeros_like(acc_s)

        wg, wu, wdn = wg_ref[0], wu_ref[0], wd_ref[...]
        for half in range(2):
            rs = pl.ds(half * (tb // 2), tb // 2)
            xn = xn_s[rs, :]
            gate = _dot_nt(xn, wg)
            up = _dot_nt(xn, wu)
            acc_s[rs, :] += _dot((_silu(gate) * up).astype(BF16), wdn)

        @pl.when((i == 0) & (j == 0))
        def _():
            dg_ref[...] = jnp.zeros_like(dg_ref)
            l_ref[...] = jnp.zeros_like(l_ref)

        @pl.when(j == nj - 1)
        def _():
            live = (i * tb + lax.broadcasted_iota(jnp.int32, (tb, 1), 0)) >= CHUNK
            tgt = t_ref[...]

            def head(out, fg):
                err = _rms(out, fg) - tgt
                return 0.5 * jnp.sum(jnp.where(live, jnp.mean(err * err, axis=-1, keepdims=True), 0.0))

            val, vjp = jax.vjp(head, h_ref[...] + 0.5 * acc_s[...], fg_ref[...])
            dh, dg = vjp(jnp.ones((), F32))
            dh_ref[...] = dh
            dg_ref[...] += dg
            l_ref[...] += jnp.full((1, 128), val, F32)

    row_d = pl.BlockSpec((tb, d), lambda i, j: (i, 0))
    vec_d = pl.BlockSpec((1, d), lambda i, j: (0, 0))
    return pl.pallas_call(
        body, name=name, grid=(rows // tb, nj),
        in_specs=[row_d, vec_d] + _ffn_specs(d, fc, nj) + [row_d, vec_d],
        out_specs=[row_d, vec_d, pl.BlockSpec((1, 128), lambda i, j: (0, 0))],
        out_shape=[jax.ShapeDtypeStruct((rows, d), F32), jax.ShapeDtypeStruct((1, d), F32),
                   jax.ShapeDtypeStruct((1, 128), F32)],
        scratch_shapes=[pltpu.VMEM((tb, d), BF16), pltpu.VMEM((tb, d), F32)],
        compiler_params=_cp("arbitrary", "arbitrary"),
    )(h, gain, wgu, wgu, wd, target, final_gain)


def _ffn_bwd(name, h, gain, dout, wgu, wd, tb):
    rows, d = h.shape
    nj = wgu.shape[0] // 2
    fc = wgu.shape[1]

    def body(h_ref, g_ref, do_ref, wg_ref, wu_ref, wd_ref,
             dh_ref, dhb_ref, dg_ref, xn_ref, act_ref, dgate_ref, dup_ref, dhalf_ref, dxn_s):
        i, j = pl.program_id(0), pl.program_id(1)

        @pl.when(j == 0)
        def _():
            xn_ref[...] = _rms(h_ref[...], g_ref[...]).astype(BF16)
            dhalf_ref[...] = (0.5 * do_ref[...]).astype(BF16)
            dxn_s[...] = jnp.zeros_like(dxn_s)

        wg, wu, wdn = wg_ref[0], wu_ref[0], wd_ref[...]
        for half in range(2):
            rs = pl.ds(half * (tb // 2), tb // 2)
            xn = xn_ref[rs, :]
            gate = _dot_nt(xn, wg)
            up = _dot_nt(xn, wu)
            sg = _sigmoid(gate)
            dact = _dot_nt(dhalf_ref[rs, :], wdn)
            act_ref[0, rs, :] = (gate * sg * up).astype(BF16)
            dgate = (dact * up * (sg * (1.0 + gate * (1.0 - sg)))).astype(BF16)
            dup = (dact * gate * sg).astype(BF16)
            dgate_ref[0, rs, :] = dgate
            dup_ref[0, rs, :] = dup
            dxn_s[rs, :] += _dot(dgate, wg) + _dot(dup, wu)

        @pl.when((i == 0) & (j == 0))
        def _():
            dg_ref[...] = jnp.zeros_like(dg_ref)

        @pl.when(j == nj - 1)
        def _():
            x = h_ref[...]
            r = lax.rsqrt(jnp.mean(x * x, axis=-1, keepdims=True) + EPS)
            dxn = dxn_s[...]
            dyg = dxn * g_ref[...]
            dh = do_ref[...] + r * dyg - x * (r * r * r) * jnp.mean(dyg * x, axis=-1, keepdims=True)
            dh_ref[...] = dh
            dhb_ref[...] = dh.astype(BF16)
            dg_ref[...] += jnp.sum(dxn * x * r, axis=0, keepdims=True)

    row_d = pl.BlockSpec((tb, d), lambda i, j: (i, 0))
    slab = pl.BlockSpec((1, tb, fc), lambda i, j: (j, i, 0))
    hidden = jax.ShapeDtypeStruct((nj, rows, fc), BF16)
    return pl.pallas_call(
        body, name=name, grid=(rows // tb, nj),
        in_specs=[row_d, pl.BlockSpec((1, d), lambda i, j: (0, 0)), row_d] + _ffn_specs(d, fc, nj),
        out_specs=[row_d, row_d, pl.BlockSpec((1, d), lambda i, j: (0, 0)), row_d, slab, slab, slab, row_d],
        out_shape=[jax.ShapeDtypeStruct((rows, d), F32), jax.ShapeDtypeStruct((rows, d), BF16),
                   jax.ShapeDtypeStruct((1, d), F32), jax.ShapeDtypeStruct((rows, d), BF16),
                   hidden, hidden, hidden, jax.ShapeDtypeStruct((rows, d), BF16)],
        scratch_shapes=[pltpu.VMEM((tb, d), F32)],
        compiler_params=_cp("arbitrary", "arbitrary"),
    )(h, gain, dout, wgu, wgu, wd)


def _shift_rows(x, s):
    return pltpu.roll(x, s % x.shape[0], 0)


def _a_post(c, which):
    s = _silu(c)
    n = s * lax.rsqrt(jnp.sum(s * s, axis=-1, keepdims=True) + 1e-6)
    scale = jnp.where(which == 0, HEAD_A ** -0.5, 1.0)
    return jnp.where(which == 2, s, n * scale)


def _conv(x, w):
    return x * w[3:4] + _shift_rows(x, 1) * w[2:3] + _shift_rows(x, 2) * w[1:2] + _shift_rows(x, 3) * w[0:1]


def _a_pre_fwd(zqkv, conv_w):
    rows, width = zqkv.shape
    heads = width // (3 * HEAD_A)

    def body(x_ref, w_ref, o_ref):
        which = pl.program_id(0) // heads
        live = lax.broadcasted_iota(jnp.int32, (rows, HEAD_A), 0) >= PAD
        o_ref[...] = jnp.where(live, _a_post(_conv(x_ref[...], w_ref[...]), which), 0.0)

    return pl.pallas_call(
        body, name="a_pre_fwd", grid=(width // HEAD_A,),
        in_specs=[pl.BlockSpec((rows, HEAD_A), lambda c: (0, c)), pl.BlockSpec((4, HEAD_A), lambda c: (0, c))],
        out_specs=pl.BlockSpec((rows, HEAD_A), lambda c: (0, c)),
        out_shape=jax.ShapeDtypeStruct((rows, width), F32),
        compiler_params=_cp("parallel"),
    )(zqkv, conv_w)


def _a_pre_bwd(zqkv, conv_w, dqkv):
    rows, width = zqkv.shape
    heads = width // (3 * HEAD_A)

    def body(x_ref, w_ref, ct_ref, dx_ref, dw_ref):
        which = pl.program_id(0) // heads
        live = lax.broadcasted_iota(jnp.int32, (rows, HEAD_A), 0) >= PAD
        x, w = x_ref[...], w_ref[...]
        _, vjp = jax.vjp(lambda c: _a_post(c, which), _conv(x, w))
        (dc,) = vjp(jnp.where(live, ct_ref[...], 0.0))
        dc = jnp.where(live, dc, 0.0)
        dx_ref[...] = (dc * w[3:4] + _shift_rows(dc, -1) * w[2:3] + _shift_rows(dc, -2) * w[1:2]
                       + _shift_rows(dc, -3) * w[0:1]).astype(BF16)
        dw_ref[...] = jnp.concatenate(
            [jnp.sum(dc * (_shift_rows(x, 3 - j) if j < 3 else x), axis=0, keepdims=True) for j in range(4)], axis=0)

    col = pl.BlockSpec((rows, HEAD_A), lambda c: (0, c))
    wsp = pl.BlockSpec((4, HEAD_A), lambda c: (0, c))
    return pl.pallas_call(
        body, name="a_pre_bwd", grid=(width // HEAD_A,),
        in_specs=[col, wsp, col], out_specs=[col, wsp],
        out_shape=[jax.ShapeDtypeStruct((rows, width), BF16), jax.ShapeDtypeStruct((4, width), F32)],
        compiler_params=_cp("parallel"),
    )(zqkv, conv_w, dqkv)


SHIFT_TILE = 384


def _shift_fwd(zb, mu):
    rows, width = zb.shape

    def body(x_ref, mu_ref, o_ref):
        x = x_ref[...]
        first = lax.broadcasted_iota(jnp.int32, x.shape, 0) == 0
        prev = jnp.where(first, 0.0, _shift_rows(x, 1))
        o_ref[...] = x + (prev - x) * mu_ref[...]

    col = pl.BlockSpec((rows, SHIFT_TILE), lambda c: (0, c))
    return pl.pallas_call(
        body, name="shift_fwd", grid=(width // SHIFT_TILE,),
        in_specs=[col, pl.BlockSpec((1, SHIFT_TILE), lambda c: (0, c))], out_specs=col,
        out_shape=jax.ShapeDtypeStruct((rows, width), F32), compiler_params=_cp("parallel"),
    )(zb, mu)


def _shift_bwd(zb, mu, dzf):
    rows, width = zb.shape

    def body(x_ref, mu_ref, ct_ref, dx_ref, dmu_ref):
        x, ct, mu_v = x_ref[...], ct_ref[...], mu_ref[...]
        row = lax.broadcasted_iota(jnp.int32, x.shape, 0)
        prev = jnp.where(row == 0, 0.0, _shift_rows(x, 1))
        nxt = jnp.where(row == rows - 1, 0.0, _shift_rows(ct, -1))
        dx_ref[...] = (ct * (1.0 - mu_v) + nxt * mu_v).astype(BF16)
        dmu_ref[...] = jnp.sum(ct * (prev - x), axis=0, keepdims=True)

    col = pl.BlockSpec((rows, SHIFT_TILE), lambda c: (0, c))
    msp = pl.BlockSpec((1, SHIFT_TILE), lambda c: (0, c))
    return pl.pallas_call(
        body, name="shift_bwd", grid=(width // SHIFT_TILE,),
        in_specs=[col, msp, col], out_specs=[col, msp],
        out_shape=[jax.ShapeDtypeStruct((rows, width), BF16), jax.ShapeDtypeStruct((1, width), F32)],
        compiler_params=_cp("parallel"),
    )(zb, mu, dzf)


def _neumann_inverse(p):
    heads = range(len(p))
    eye = (lax.broadcasted_iota(jnp.int32, (CHUNK, CHUNK), 0)
           == lax.broadcasted_iota(jnp.int32, (CHUNK, CHUNK), 1)).astype(F32)
    tinv = [eye + p[h] for h in heads]
    for _ in range(5):
        p = [_dot(p[h], p[h], DN_PRECISION) for h in heads]
        tinv = [tinv[h] + _dot(tinv[h], p[h], DN_PRECISION) for h in heads]
    return tinv


@jax.custom_vjp
def _unit_lower_inverse(p):
    return _neumann_inverse(p)


def _unit_lower_inverse_fwd(p):
    tinv = _neumann_inverse(p)
    return tinv, tinv


def _unit_lower_inverse_bwd(tinv, ct):
    heads = range(len(tinv))
    left = [_dot_tn(tinv[h], ct[h], DN_PRECISION) for h in heads]
    return ([_dot_nt(left[h], tinv[h], DN_PRECISION) for h in heads],)


_unit_lower_inverse.defvjp(_unit_lower_inverse_fwd, _unit_lower_inverse_bwd)


@jax.custom_vjp
def _known_inverse(p, tinv):
    return tinv


_known_inverse.defvjp(lambda p, tinv: (tinv, tinv),
                      lambda tinv, ct: (_unit_lower_inverse_bwd(tinv, ct)[0], [jnp.zeros_like(t) for t in tinv]))


def _dn_chunk(q, k, v, beta, g, state, saved_tinv=None):
    heads = range(len(q))
    ri = lax.broadcasted_iota(jnp.int32, (CHUNK, CHUNK), 0)
    ci = lax.broadcasted_iota(jnp.int32, (CHUNK, CHUNK), 1)
    eye = (ri == ci).astype(F32)
    incl = ri >= ci
    last = lax.broadcasted_iota(jnp.int32, (CHUNK, 1), 0) == CHUNK - 1
    g_row = [jnp.sum(g[h] * eye, axis=0, keepdims=True) for h in heads]
    gc = [jnp.sum(jnp.where(incl, g_row[h], 0.0), axis=1, keepdims=True) for h in heads]
    gc_row = [jnp.sum(gc[h] * eye, axis=0, keepdims=True) for h in heads]
    decay = [jnp.where(incl, jnp.exp(jnp.where(incl, gc[h] - gc_row[h], 0.0)), 0.0) for h in heads]
    kb = [k[h] * beta[h] for h in heads]
    vb = [v[h] * beta[h] for h in heads]
    p = [-jnp.where(ri > ci, _dot_nt(kb[h], k[h]) * decay[h], 0.0) for h in heads]
    tinv = _unit_lower_inverse(p) if saved_tinv is None else _known_inverse(p, saved_tinv)
    eg = [jnp.exp(gc[h]) for h in heads]
    u = [_dot(tinv[h], vb[h]) for h in heads]
    wk = [_dot(tinv[h], kb[h] * eg[h]) for h in heads]
    attn = [_dot_nt(q[h], k[h]) * decay[h] for h in heads]
    g_last = [jnp.sum(jnp.where(last, gc[h], 0.0), axis=0, keepdims=True) for h in heads]
    k_tail = [k[h] * jnp.exp(g_last[h] - gc[h]) for h in heads]
    v_new = [u[h] - _dot(wk[h], state[h]) for h in heads]
    o = [_dot(q[h] * eg[h], state[h]) + _dot(attn[h], v_new[h]) for h in heads]
    new = [state[h] * jnp.exp(g_last[h]) + _dot_tn(k_tail[h], v_new[h]) for h in heads]
    return (o, new, tinv) if saved_tinv is None else (o, new)


def _bg_cols(bg, h, heads):
    lane = lax.broadcasted_iota(jnp.int32, bg.shape, 1)
    beta = jnp.sum(jnp.where(lane == h, bg, 0.0), axis=1, keepdims=True)
    g = jnp.sum(jnp.where(lane == heads + h, bg, 0.0), axis=1, keepdims=True)
    return beta, g


def _dn_fwd(qkv, bg):
    rows = qkv.shape[0]
    heads = qkv.shape[1] // (3 * HEAD_A)
    n = rows // CHUNK
    hp, groups = heads, 1

    def body(q_ref, k_ref, v_ref, bg_ref, o_ref, hist_ref, tinv_ref, s_ref):
        c, grp = pl.program_id(0), pl.program_id(1)

        @pl.when(c == 0)
        def _():
            for i in range(hp):
                s_ref[grp * hp + i] = jnp.zeros((HEAD_A, HEAD_A), F32)

        bg_v = bg_ref[...]
        cols = [slice(i * HEAD_A, (i + 1) * HEAD_A) for i in range(hp)]
        state = [s_ref[grp * hp + i] for i in range(hp)]
        beta_g = [_bg_cols(bg_v, grp * hp + i, heads) for i in range(hp)]
        o, new, tinv = _dn_chunk([q_ref[:, c_] for c_ in cols], [k_ref[:, c_] for c_ in cols],
                                 [v_ref[:, c_] for c_ in cols], [b for b, _ in beta_g], [g for _, g in beta_g], state)
        for i in range(hp):
            hist_ref[0, i] = state[i]
            tinv_ref[0, i] = tinv[i]
            o_ref[:, cols[i]] = o[i]
            s_ref[grp * hp + i] = new[i]

    def part(p):
        return pl.BlockSpec((CHUNK, hp * HEAD_A), lambda c, grp: (c, p * groups + grp))

    return pl.pallas_call(
        body, name="deltanet_fwd", grid=(n, groups),
        in_specs=[part(0), part(1), part(2), pl.BlockSpec((CHUNK, 128), lambda c, grp: (c, 0))],
        out_specs=[part(0), pl.BlockSpec((1, hp, HEAD_A, HEAD_A), lambda c, grp: (c, grp, 0, 0)),
                   pl.BlockSpec((1, hp, CHUNK, CHUNK), lambda c, grp: (c, grp, 0, 0))],
        out_shape=[jax.ShapeDtypeStruct((rows, heads * HEAD_A), F32),
                   jax.ShapeDtypeStruct((n, heads, HEAD_A, HEAD_A), F32),
                   jax.ShapeDtypeStruct((n, heads, CHUNK, CHUNK), F32)],
        scratch_shapes=[pltpu.VMEM((heads, HEAD_A, HEAD_A), F32)],
        compiler_params=_cp("arbitrary", "arbitrary"),
    )(qkv, qkv, qkv, bg)


def _dn_bwd(qkv, bg, hist, tinv_hist, do):
    rows = qkv.shape[0]
    heads = qkv.shape[1] // (3 * HEAD_A)
    n = rows // CHUNK
    hp, groups = heads, 1

    def body(q_ref, k_ref, v_ref, bg_ref, hist_ref, tinv_ref, do_ref, dqkv_ref, dbg_ref, ds_ref):
        c, grp = pl.program_id(0), pl.program_id(1)

        @pl.when(c == 0)
        def _():
            for i in range(hp):
                ds_ref[grp * hp + i] = jnp.zeros((HEAD_A, HEAD_A), F32)

        bg_v = bg_ref[...]
        lane = lax.broadcasted_iota(jnp.int32, (CHUNK, 128), 1)
        cols = [slice(i * HEAD_A, (i + 1) * HEAD_A) for i in range(hp)]
        beta_g = [_bg_cols(bg_v, grp * hp + i, heads) for i in range(hp)]
        _, vjp = jax.vjp(_dn_chunk, [q_ref[:, c_] for c_ in cols], [k_ref[:, c_] for c_ in cols],
                         [v_ref[:, c_] for c_ in cols], [b for b, _ in beta_g], [g for _, g in beta_g],
                         [hist_ref[0, i] for i in range(hp)], [tinv_ref[0, i] for i in range(hp)])
        dq, dk, dv, dbeta, dg, ds, _ = vjp(([do_ref[:, c_] for c_ in cols], [ds_ref[grp * hp + i] for i in range(hp)]))
        dbg = jnp.zeros((CHUNK, 128), F32)
        for i in range(hp):
            h = grp * hp + i
            for p, part_grad in enumerate((dq, dk, dv)):
                dqkv_ref[:, pl.ds((p * heads + i) * HEAD_A, HEAD_A)] = part_grad[i]
            ds_ref[h] = ds[i]
            dbg = dbg + jnp.where(lane == h, dbeta[i], 0.0) + jnp.where(lane == heads + h, dg[i], 0.0)

        @pl.when(grp == 0)
        def _():
            dbg_ref[...] = jnp.zeros_like(dbg_ref)

        dbg_ref[...] += dbg

    def part(p):
        return pl.BlockSpec((CHUNK, hp * HEAD_A), lambda c, grp: (n - 1 - c, p * groups + grp))

    return pl.pallas_call(
        body, name="deltanet_bwd", grid=(n, groups),
        in_specs=[part(0), part(1), part(2), pl.BlockSpec((CHUNK, 128), lambda c, grp: (n - 1 - c, 0)),
                  pl.BlockSpec((1, hp, HEAD_A, HEAD_A), lambda c, grp: (n - 1 - c, grp, 0, 0)),
                  pl.BlockSpec((1, hp, CHUNK, CHUNK), lambda c, grp: (n - 1 - c, grp, 0, 0)), part(0)],
        out_specs=[pl.BlockSpec((CHUNK, 3 * heads * HEAD_A), lambda c, grp: (n - 1 - c, 0)),
                   pl.BlockSpec((CHUNK, 128), lambda c, grp: (n - 1 - c, 0))],
        out_shape=[jax.ShapeDtypeStruct(qkv.shape, F32), jax.ShapeDtypeStruct((rows, 128), F32)],
        scratch_shapes=[pltpu.VMEM((heads, HEAD_A, HEAD_A), F32)],
        compiler_params=_cp("arbitrary", "arbitrary"),
    )(qkv, qkv, qkv, bg, hist, tinv_hist, do)


def _head_mask(heads, width):
    return (lax.broadcasted_iota(jnp.int32, (heads, width), 0)
            == lax.broadcasted_iota(jnp.int32, (heads, width), 1) // HEAD_B)


def _masked_rows(mask, row):
    return jnp.where(mask, row, 0.0).astype(BF16)


def _rwkv_fwd(r, w, k, v, a, b):
    rows, width = r.shape
    heads = width // HEAD_B
    ts = SCAN_STEPS

    def body(r_ref, w_ref, k_ref, v_ref, a_ref, b_ref, y_ref, hist_ref, s_ref):
        @pl.when(pl.program_id(0) == 0)
        def _():
            s_ref[...] = jnp.zeros_like(s_ref)

        mask = _head_mask(heads, width)
        onehot = mask.astype(BF16)
        onehot2 = jnp.concatenate([onehot, onehot], axis=0)
        bd = _block_diag_ones()

        spread_v = [_dot_tn(jnp.concatenate(_hi_lo(v_ref[j]), axis=0), onehot2) for j in range(ts)]
        a_next = pltpu.roll(a_ref[...], ts - 1, 0)
        b_dot_a, k_dot_a = _segsum_many([b_ref[...] * a_next, k_ref[...] * a_next], bd)
        s = s_ref[...]
        ys = []
        for j in range(0, ts, 2):
            row, nxt = pl.ds(j, 1), pl.ds(j + 1, 1)
            hist_ref[j] = s
            sa, base = _segsum_many([((s * a_ref[row, :]).astype(BF16),),
                                     ((s * (w_ref[row, :] * a_ref[nxt, :])).astype(BF16),)], bd)
            sa_next = base + sa * b_dot_a[j:j + 1] + spread_v[j] * k_dot_a[j:j + 1]
            s = s * w_ref[row, :] + sa * b_ref[row, :] + spread_v[j] * k_ref[row, :]
            hist_ref[j + 1] = s
            ys.append(_dot_nt(_masked_rows(mask, r_ref[row, :]), s.astype(BF16)))
            s = s * w_ref[nxt, :] + sa_next * b_ref[nxt, :] + spread_v[j + 1] * k_ref[nxt, :]
            ys.append(_dot_nt(_masked_rows(mask, r_ref[nxt, :]), s.astype(BF16)))
        for j in range(ts):
            y_ref[j] = ys[j]
        s_ref[...] = s

    blk = pl.BlockSpec((ts, width), lambda i: (i, 0))
    blk3 = pl.BlockSpec((ts, heads, HEAD_B), lambda i: (i, 0, 0))
    return pl.pallas_call(
        body, name="rwkv_fwd", grid=(rows // ts,),
        in_specs=[blk, blk, blk, blk3, blk, blk],
        out_specs=[blk3, pl.BlockSpec((ts, HEAD_B, width), lambda i: (i, 0, 0)),
                   pl.BlockSpec((HEAD_B, width), lambda i: (0, 0))],
        out_shape=[jax.ShapeDtypeStruct((rows, heads, HEAD_B), F32), jax.ShapeDtypeStruct((rows, HEAD_B, width), F32),
                   jax.ShapeDtypeStruct((HEAD_B, width), F32)],
        compiler_params=_cp("arbitrary"),
    )(r, w, k, v, a, b)


def _rwkv_bwd(r, w, k, v, a, b, hist, last, dy):
    rows, width = r.shape
    heads = width // HEAD_B
    ts = SCAN_STEPS
    nb = rows // ts

    def body(r_ref, w_ref, k_ref, v_ref, a_ref, b_ref, hist_ref, last_ref, dy_ref,
             dr_ref, dw_ref, dk_ref, dv_ref, da_ref, db_ref, g_ref, after_ref):
        @pl.when(pl.program_id(0) == 0)
        def _():
            g_ref[...] = jnp.zeros_like(g_ref)
            after_ref[...] = last_ref[...]

        mask = _head_mask(heads, width)
        onehot = mask.astype(BF16)
        bd = _block_diag_ones()

        def own_lanes(x):
            return jnp.sum(jnp.where(mask, x, 0.0), axis=0, keepdims=True)

        def colsum(x):
            return jnp.sum(x, axis=0, keepdims=True)

        dy_m = [dy_ref[j].astype(BF16) for j in range(ts)]
        spread_dy = [_dot_tn(dy_m[j], onehot) for j in range(ts)]
        state_after = [hist_ref[j + 1] if j < ts - 1 else after_ref[...] for j in range(ts)]
        dr = [own_lanes(_dot(dy_m[j], state_after[j].astype(BF16))) for j in range(ts)]
        sa_m = [_dot_nt(_masked_rows(mask, a_ref[pl.ds(j, 1), :]), hist_ref[j].astype(BF16)) for j in range(ts)]
        g = g_ref[...]
        dw, dk, db, da, dv = {}, {}, {}, {}, {}
        for j in reversed(range(ts)):
            row = pl.ds(j, 1)
            sp = hist_ref[j]
            g = g + spread_dy[j] * r_ref[row, :]
            (dsa,) = _segsum_many([((g * b_ref[row, :]).astype(BF16),)], bd)
            g_b = g.astype(BF16)
            both = _dot(jnp.concatenate([v_ref[j].astype(BF16), sa_m[j].astype(BF16)], axis=0), g_b)
            dk[j], db[j] = own_lanes(both[:heads]), own_lanes(both[heads:])
            dv[j] = _dot_nt(_masked_rows(mask, k_ref[row, :]), g_b)
            dw[j] = colsum(g * sp)
            da[j] = colsum(sp * dsa)
            g = g * w_ref[row, :] + dsa * a_ref[row, :]
        g_ref[...] = g
        after_ref[...] = hist_ref[0]
        for j in range(ts):
            dv_ref[j] = dv[j]
            for ref, vals in ((dr_ref, dr), (dw_ref, dw), (dk_ref, dk), (da_ref, da), (db_ref, db)):
                ref[pl.ds(j, 1), :] = vals[j]

    blk = pl.BlockSpec((ts, width), lambda i: (nb - 1 - i, 0))
    blk3 = pl.BlockSpec((ts, heads, HEAD_B), lambda i: (nb - 1 - i, 0, 0))
    state = pl.BlockSpec((HEAD_B, width), lambda i: (0, 0))
    return pl.pallas_call(
        body, name="rwkv_bwd", grid=(nb,),
        in_specs=[blk, blk, blk, blk3, blk, blk, pl.BlockSpec((ts, HEAD_B, width), lambda i: (nb - 1 - i, 0, 0)),
                  state, blk3],
        out_specs=[blk, blk, blk, blk3, blk, blk],
        out_shape=[jax.ShapeDtypeStruct((rows, width), F32)] * 3 + [jax.ShapeDtypeStruct((rows, heads, HEAD_B), F32)]
        + [jax.ShapeDtypeStruct((rows, width), F32)] * 2,
        scratch_shapes=[pltpu.VMEM((HEAD_B, width), F32), pltpu.VMEM((HEAD_B, width), F32)],
        compiler_params=_cp("arbitrary"),
    )(r, w, k, v, a, b, hist, last, dy)


def _live(row0, shape):
    return (row0 + lax.broadcasted_iota(jnp.int32, shape, 0)) >= PAD


def _norm_fn(row0, h, gain):
    return (_rms(h, gain),)


def _norm_res_fn(row0, h, gain):
    return _rms(h, gain), h


def _make_bg_fn(heads):
    def fn(row0, x, log_rate, dt_bias):
        lane = lax.broadcasted_iota(jnp.int32, x.shape, 1)
        beta = _sigmoid(x)
        g = -jnp.exp(log_rate) * _softplus(x + dt_bias)
        out = jnp.where(lane < heads, beta, jnp.where(lane < 2 * heads, g, 0.0))
        return (jnp.where(_live(row0, x.shape), out, 0.0),)
    return fn


def _b_pre_fn(row0, zf, w0, w_up, a0, a_up, g_up, k_k, k_a):
    d = w0.shape[1]
    r, k, v = zf[:, :d], zf[:, d:2 * d], zf[:, 2 * d:3 * d]
    lo = zf[:, 3 * d:3 * d + 128]
    lg = zf[:, 3 * d + 128:3 * d + LORA_PAD]
    lane = lax.broadcasted_iota(jnp.int32, lo.shape, 1)
    lw = _dot(jnp.where(lane < LORA_W, jnp.tanh(lo), 0.0), w_up)
    la = _dot(jnp.where(lane >= LORA_W, lo, 0.0), a_up)
    lane_g = lax.broadcasted_iota(jnp.int32, lg.shape, 1)
    gate = _dot(jnp.where(lane_g < LORA_G, _sigmoid(lg), 0.0), g_up)
    decay = jnp.exp(-jnp.exp(-_softplus(-(w0 + lw)) - 0.5))
    a = _sigmoid(a0 + la)
    kx = k * k_k
    kk = kx * lax.rsqrt(_segsum64(kx * kx) + 1e-6)
    k2 = k * (1.0 + (a - 1.0) * k_a)
    return r, decay, k2, v, -kk, kk * a, gate


def _post_fn(row0, o, zg, y, r, k2, v, gate, out_gain, r_k, ln_g, ln_b):
    d = o.shape[1]
    az, ga, gb = zg[:, :d], zg[:, d:2 * d], zg[:, 2 * d:]
    heads = d // HEAD_A
    parts = []
    for h in range(heads):
        oh = o[:, h * HEAD_A:(h + 1) * HEAD_A]
        parts.append(oh * lax.rsqrt(jnp.mean(oh * oh, axis=-1, keepdims=True) + EPS) * out_gain)
    o_a = jnp.concatenate(parts, axis=1) * _silu(az)
    mean = _segsum64(y) * (1.0 / HEAD_B)
    yc = y - mean
    var = _segsum64(yc * yc) * (1.0 / HEAD_B)
    yn = yc * lax.rsqrt(var + GN_EPS) * ln_g + ln_b
    o_b = (yn + _segsum64(r * k2 * r_k) * v) * gate
    return (_sigmoid(ga) * o_a + _sigmoid(gb) * o_b,)


def _loss(h3, target, gain, tb):
    rows, d = h3.shape

    def body(h_ref, t_ref, g_ref, dh_ref, dg_ref, l_ref):
        i = pl.program_id(0)
        live = (i * tb + lax.broadcasted_iota(jnp.int32, (tb, 1), 0)) >= CHUNK
        tgt = t_ref[...]

        def f(h, g):
            err = _rms(h, g) - tgt
            return 0.5 * jnp.sum(jnp.where(live, jnp.mean(err * err, axis=-1, keepdims=True), 0.0))

        val, vjp = jax.vjp(f, h_ref[...], g_ref[...])
        dh, dg = vjp(jnp.ones((), F32))
        dh_ref[...] = dh

        @pl.when(i == 0)
        def _():
            dg_ref[...] = jnp.zeros_like(dg_ref)
            l_ref[...] = jnp.zeros_like(l_ref)

        dg_ref[...] += dg
        l_ref[...] += jnp.full((1, 128), val, F32)

    blk = pl.BlockSpec((tb, d), lambda i: (i, 0))
    return pl.pallas_call(
        body, name="loss", grid=(rows // tb,),
        in_specs=[blk, blk, pl.BlockSpec((1, d), lambda i: (0, 0))],
        out_specs=[blk, pl.BlockSpec((1, d), lambda i: (0, 0)), pl.BlockSpec((1, 128), lambda i: (0, 0))],
        out_shape=[jax.ShapeDtypeStruct((rows, d), F32), jax.ShapeDtypeStruct((1, d), F32),
                   jax.ShapeDtypeStruct((1, 128), F32)],
        compiler_params=_cp("arbitrary"),
    )(h3, target, gain)


def _adamw_math(w, g, m, v):
    m2 = ADAM_B1 * m + (1.0 - ADAM_B1) * g
    v2 = ADAM_B2 * v + (1.0 - ADAM_B2) * (g * g)
    m_hat = m2 / (1.0 - ADAM_B1 ** ADAM_STEP)
    v_hat = v2 / (1.0 - ADAM_B2 ** ADAM_STEP)
    return -ADAM_LR * (m_hat / (jnp.sqrt(v_hat) + ADAM_EPS) + ADAM_WD * w), m2, v2


def _adamw(name, own, landed, w, m, v):
    rows, cols = w.shape
    if rows % 16 == 0:
        rb = _tb(rows, 128)
        grid, blk = (rows // rb,), pl.BlockSpec((rb, cols), lambda i: (i, 0))
        landed_blk = pl.BlockSpec((N_DEV - 1, rb, cols), lambda i: (0, i, 0))
    else:
        grid, blk = (cols // 128,), pl.BlockSpec((rows, 128), lambda i: (0, i))
        landed_blk = pl.BlockSpec((N_DEV - 1, rows, 128), lambda i: (0, 0, i))

    def body(o_ref, s_ref, w_ref, m_ref, v_ref, g_ref, d_ref, m2_ref, v2_ref):
        g = o_ref[...].astype(F32)
        for peer in range(N_DEV - 1):
            g = g + s_ref[peer].astype(F32)
        g_ref[...] = g
        d_ref[...], m2_ref[...], v2_ref[...] = _adamw_math(w_ref[...], g, m_ref[...], v_ref[...])

    return pl.pallas_call(
        body, name=name, grid=grid, in_specs=[blk, landed_blk, blk, blk, blk],
        out_specs=[blk] * 4, out_shape=[jax.ShapeDtypeStruct((rows, cols), F32)] * 4,
        compiler_params=_cp("parallel"),
    )(own, landed, w, m, v)


def _sum_slabs(name, slabs, rb):
    _, rows, cols = slabs.shape

    def body(s_ref, o_ref):
        g = s_ref[0]
        for dev in range(1, N_DEV):
            g = g + s_ref[dev]
        o_ref[...] = g

    return pl.pallas_call(
        body, name=name, grid=(rows // rb,),
        in_specs=[pl.BlockSpec((N_DEV, rb, cols), lambda i: (0, i, 0))],
        out_specs=pl.BlockSpec((rb, cols), lambda i: (i, 0)),
        out_shape=jax.ShapeDtypeStruct((rows, cols), F32), compiler_params=_cp("parallel"),
    )(slabs)


def _adamw_small(w, g, m, v):
    def body(w_ref, g_ref, m_ref, v_ref, d_ref, m2_ref, v2_ref):
        d_ref[...], m2_ref[...], v2_ref[...] = _adamw_math(w_ref[...], g_ref[...], m_ref[...], v_ref[...])

    return pl.pallas_call(body, name="adamw_small", out_shape=[jax.ShapeDtypeStruct(w.shape, F32)] * 3)(w, g, m, v)


def _place():
    return lax.axis_index("x"), lax.axis_index("y"), lax.axis_index("c")


def _index(p):
    return 4 * p[0] + 2 * p[1] + p[2]


def _all_gather(name, xs):
    n = len(xs)

    def body(*refs):
        x_refs, o_refs = refs[:n], refs[n:2 * n]
        send_sems, recv_sems, local_sems = refs[2 * n:]
        x, y, c = _place()
        me, sibling = (x, y, c), (x, y, 1 - c)
        chips = [(1 - x, y), (x, 1 - y), (1 - x, 1 - y)]

        def copy(i, k, block, to, src=None):
            dst = o_refs[i].at[_index(block)]
            return pltpu.make_async_remote_copy(src_ref=dst if src is None else src, dst_ref=dst,
                                                send_sem=send_sems.at[i, k], recv_sem=recv_sems.at[i, k],
                                                device_id=to, device_id_type=MESH_ID)

        mine = [pltpu.make_async_copy(x_refs[i], o_refs[i].at[_index(me)], local_sems.at[i]) for i in range(n)]
        for cp in mine:
            cp.start()
        first = []
        for i in range(n):
            first.append(copy(i, 0, me, sibling, src=x_refs[i]))
            first += [copy(i, 1 + j, me, (*chip, c), src=x_refs[i]) for j, chip in enumerate(chips)]
        for cp in first:
            cp.start()
        passed = []
        for j, chip in enumerate(chips):
            for i in range(n):
                copy(i, 1 + j, (*chip, c), me).wait_recv()
                cp = copy(i, 4 + j, (*chip, c), sibling)
                cp.start()
                passed.append(cp)
        for i in range(n):
            copy(i, 0, sibling, me).wait_recv()
            for j, chip in enumerate(chips):
                copy(i, 4 + j, (*chip, 1 - c), me).wait_recv()
        for cp in first + passed:
            cp.wait_send()
        for cp in mine:
            cp.wait()

    return pl.pallas_call(
        body, name=name, in_specs=[ANY] * n, out_specs=[ANY] * n,
        out_shape=[jax.ShapeDtypeStruct((N_DEV,) + x.shape, x.dtype) for x in xs],
        scratch_shapes=[pltpu.SemaphoreType.DMA((n, 7)), pltpu.SemaphoreType.DMA((n, 7)), pltpu.SemaphoreType.DMA((n,))],
    )(*xs)


def _exchange_start(name, xs, after=None, gather=False):
    n = len(xs)
    copies = n * (N_DEV - 1)
    extra = [] if after is None else [after]

    def body(*refs):
        x_refs, land_refs = refs[:n], refs[n:2 * n]
        sems = refs[2 * n + len(extra):2 * n + len(extra) + 2 * copies]
        token = refs[-1]
        for i, k, peer in _exchange_copies(n):
            _exchange_copy(x_refs, land_refs, sems, i, k, peer, gather).start()
        token[...] = jnp.zeros_like(token)

    lands = [lax.empty((N_DEV,) + x.shape if gather else (N_DEV - 1,) + x.shape[1:], x.dtype) for x in xs]
    out = pl.pallas_call(
        body, name=name,
        out_shape=(*[pltpu.SemaphoreType.DMA(())] * (2 * copies), *[pltpu.HBM(x.shape, x.dtype) for x in xs],
                   *[pltpu.HBM(l.shape, l.dtype) for l in lands], jax.ShapeDtypeStruct((8, 128), F32)),
        in_specs=[HBM_SPEC] * (2 * n) + [ANY] * len(extra),
        out_specs=(*[SEM_SPEC] * (2 * copies), *[HBM_SPEC] * (2 * n), pl.BlockSpec(memory_space=pltpu.VMEM)),
        input_output_aliases={i: 2 * copies + i for i in range(2 * n)},
        compiler_params=pltpu.CompilerParams(has_side_effects=pltpu.SideEffectType.DATAFLOW_SIDE_EFFECTING),
    )(*[pltpu.with_memory_space_constraint(a, pltpu.HBM) for a in list(xs) + lands], *extra)
    sems, rest = list(out[:2 * copies]), out[2 * copies:]
    return sems, list(rest[:n]), list(rest[n:2 * n]), rest[-1]


def _exchange_copies(n):
    x, y, c = _place()
    for k in range(1, N_DEV):
        peer = ((1 - x) if k & 4 else x, (1 - y) if k & 2 else y, (1 - c) if k & 1 else c)
        for i in range(n):
            yield i, k - 1, peer


def _exchange_copy(x_refs, land_refs, sems, i, k, peer, gather, arriving=False):
    copies = len(sems) // 2
    which = i * (N_DEV - 1) + k
    src = x_refs[i] if gather else x_refs[i].at[_index(peer)]
    dst = land_refs[i].at[_index(peer if arriving else _place())] if gather else land_refs[i].at[k]
    return pltpu.make_async_remote_copy(src_ref=src, dst_ref=dst, send_sem=sems[which], recv_sem=sems[copies + which],
                                        device_id=peer, device_id_type=MESH_ID)


def _exchange_wait(name, sems, xs, lands, after, gather=False):
    n = len(xs)

    def body(*refs):
        x_refs, land_refs = refs[:n], refs[n:2 * n]
        sem_refs = refs[2 * n:2 * n + len(sems)]
        for i, k, peer in _exchange_copies(n):
            _exchange_copy(x_refs, land_refs, sem_refs, i, k, peer, gather).wait_send()
            _exchange_copy(x_refs, land_refs, sem_refs, i, k, peer, gather, arriving=True).wait_recv()

    out = pl.pallas_call(
        body, name=name,
        out_shape=(*[pltpu.HBM(x.shape, x.dtype) for x in xs], *[pltpu.HBM(l.shape, l.dtype) for l in lands]),
        in_specs=[HBM_SPEC] * (2 * n) + [SEM_SPEC] * len(sems) + [ANY], out_specs=tuple([HBM_SPEC] * (2 * n)),
        input_output_aliases={i: i for i in range(2 * n)},
        compiler_params=pltpu.CompilerParams(has_side_effects=pltpu.SideEffectType.DATAFLOW_SIDE_EFFECTING),
    )(*xs, *lands, *sems, after)
    return list(out[:n]), list(out[n:])


def _pack(arrays):
    flat = jnp.concatenate([a.reshape(-1) for a in arrays])
    pad = (-flat.shape[0]) % 1024
    return jnp.pad(flat, (0, pad)).reshape(-1, 128)


def _unpack(packed, shapes):
    flat = packed.reshape(-1)
    out, pos = [], 0
    for s in shapes:
        size = 1
        for dim in s:
            size *= dim
        out.append(flat[pos:pos + size].reshape(s))
        pos += size
    return out


def _cols_from_slabs(stack):
    return jnp.transpose(stack, (1, 0, 2)).reshape(stack.shape[1], -1)


def kernel(x, meta_tokens, ffn1_norm, ffn1_w_gu, ffn1_w_down, mix_norm, w_in, a_conv_w, a_log_rate, a_dt_bias, a_out_norm, b_shift_mu, b_w0, b_w_up, b_a0, b_a_up, b_g_up, b_k_k, b_k_a, b_r_k, b_ln_gain, b_ln_bias, w_out, ffn2_norm, ffn2_w_gu, ffn2_w_down, final_norm, loss_target, m_meta_tokens, m_ffn1_norm, m_ffn1_w_gu, m_ffn1_w_down, m_mix_norm, m_w_in, m_a_conv_w, m_a_log_rate, m_a_dt_bias, m_a_out_norm, m_b_shift_mu, m_b_w0, m_b_w_up, m_b_a0, m_b_a_up, m_b_g_up, m_b_k_k, m_b_k_a, m_b_r_k, m_b_ln_gain, m_b_ln_bias, m_w_out, m_ffn2_norm, m_ffn2_w_gu, m_ffn2_w_down, m_final_norm, v_meta_tokens, v_ffn1_norm, v_ffn1_w_gu, v_ffn1_w_down, v_mix_norm, v_w_in, v_a_conv_w, v_a_log_rate, v_a_dt_bias, v_a_out_norm, v_b_shift_mu, v_b_w0, v_b_w_up, v_b_a0, v_b_a_up, v_b_g_up, v_b_k_k, v_b_k_a, v_b_r_k, v_b_ln_gain, v_b_ln_bias, v_w_out, v_ffn2_norm, v_ffn2_w_gu, v_ffn2_w_down, v_final_norm):
    names = ['meta_tokens', 'ffn1_norm', 'ffn1_w_gu', 'ffn1_w_down', 'mix_norm', 'w_in', 'a_conv_w', 'a_log_rate',
             'a_dt_bias', 'a_out_norm', 'b_shift_mu', 'b_w0', 'b_w_up', 'b_a0', 'b_a_up', 'b_g_up', 'b_k_k', 'b_k_a',
             'b_r_k', 'b_ln_gain', 'b_ln_bias', 'w_out', 'ffn2_norm', 'ffn2_w_gu', 'ffn2_w_down', 'final_norm']
    env = dict(locals())
    wts = {k: env[k] for k in names}
    mom_m = {k: env['m_' + k] for k in names}
    mom_v = {k: env['v_' + k] for k in names}
    big = ['ffn1_w_gu', 'ffn1_w_down', 'w_in', 'w_out', 'ffn2_w_gu', 'ffn2_w_down']
    col_sharded = {'ffn1_w_gu', 'w_in', 'ffn2_w_gu'}
    shard_of = lambda tree, k: tree[k][0].T if k in col_sharded else tree[k][0]
    small_sharded = ['meta_tokens', 'a_conv_w', 'b_w_up', 'b_a_up', 'b_g_up']
    replicated = [k for k in names if k not in big and k not in small_sharded]

    seq, d = x.shape[1], x.shape[2]
    rows = PAD + N_META + seq
    heads_a = d // HEAD_A
    tb_mm = _tb(rows, 416)
    tb_vjp = _tb(rows, 208)
    tb_dw = _tb(rows, 2080)
    tb_ffn = _tb(rows, 832)
    me = _index(_place())

    local_bf = {k: shard_of(wts, k).astype(BF16) for k in big}
    gu1, down1, meta = _all_gather("gather_ffn1", [local_bf['ffn1_w_gu'], local_bf['ffn1_w_down'], wts['meta_tokens']])
    small_rest = small_sharded[1:]
    late_keys = ['w_out', 'ffn2_w_gu', 'ffn2_w_down']
    gather_mid = _exchange_start("gather_start_mid", [local_bf['w_in'], _pack([wts[k][0] for k in small_rest])],
                                 after=gu1, gather=True)
    gather_late = _exchange_start("gather_start_late", [local_bf[k] for k in late_keys], after=gather_mid[-1],
                                  gather=True)

    def finish_gather(tag, started, after):
        sems, mine, lands, _ = started
        mine, lands = _exchange_wait("gather_wait_" + tag, sems, mine, lands, after, gather=True)
        return [lax.dynamic_update_index_in_dim(land, own[None], me, 0) for land, own in zip(lands, mine)]

    full = {'ffn1_w_gu': gu1, 'ffn1_w_down': down1.reshape(-1, d), 'meta_tokens': _cols_from_slabs(meta)}
    for k in replicated:
        full[k] = wts[k].reshape(1, -1)

    h0 = jnp.concatenate([jnp.zeros((PAD, d), F32) + gather_late[-1][:1, :1], full['meta_tokens'], x[0]], axis=0)
    h1 = _ffn_fwd("ffn1_fwd", h0, full['ffn1_norm'], full['ffn1_w_gu'], full['ffn1_w_down'], tb_ffn)
    (u,) = _tok_fwd("mix_norm_fwd", _norm_fn, [h1], [full['mix_norm']], [(d, BF16)], tb_mm)

    win_stack, small_stack = finish_gather("mid", gather_mid, u)
    full['w_in'] = win_stack.reshape(-1, d)
    small_flat, pos = small_stack.reshape(N_DEV, -1), 0
    for k in small_rest:
        shape = wts[k][0].shape
        full[k] = _cols_from_slabs(small_flat[:, pos:pos + shape[0] * shape[1]].reshape((N_DEV,) + shape))
        pos += shape[0] * shape[1]

    win = full['w_in']
    n_b = 3 * d + LORA_W + LORA_A + LORA_G
    off_beta, off_b = 4 * d, 4 * d + 2 * heads_a
    off_ga = off_b + n_b
    b_width = 3 * d + LORA_PAD
    zrows = lambda r: jnp.zeros((r, d), BF16)
    w_qkv = win[:3 * d]
    w_zg = jnp.concatenate([win[3 * d:4 * d], win[off_ga:off_ga + 2 * d]], axis=0)
    w_b = jnp.concatenate([win[off_b:off_b + n_b], zrows(b_width - n_b)], axis=0)
    w_bg = jnp.concatenate([win[off_beta:off_beta + 2 * heads_a], zrows(128 - 2 * heads_a)], axis=0)

    def lanes(vec, start, width):
        return jnp.pad(vec.reshape(1, -1), ((0, 0), (start, width - start - vec.size)))

    log_rate = lanes(wts['a_log_rate'], heads_a, 128)
    dt_bias = lanes(wts['a_dt_bias'], heads_a, 128)
    mu = lanes(wts['b_shift_mu'], 0, b_width)
    w_up = jnp.pad(full['b_w_up'], ((0, 128 - LORA_W), (0, 0)))
    a_up = jnp.pad(full['b_a_up'], ((LORA_W, 0), (0, 0)))
    g_up = jnp.pad(full['b_g_up'], ((0, 256 - LORA_G), (0, 0)))
    b_pars = [full['b_w0'], w_up, full['b_a0'], a_up, g_up, full['b_k_k'], full['b_k_a']]
    post_pars = [full['a_out_norm'], full['b_r_k'], full['b_ln_gain'], full['b_ln_bias']]
    bg_fn = _make_bg_fn(heads_a)

    z_qkv = _mm("in_qkv", u, w_qkv, trans_b=True, tb=tb_mm, tn=_col_tile(3 * d, 1536))
    z_zg = _mm("in_zg", u, w_zg, trans_b=True, tb=tb_mm, tn=_col_tile(3 * d, 1536))
    z_b = _mm("in_b", u, w_b, trans_b=True, tb=tb_mm, tn=_col_tile(b_width, 1536))
    z_bg = _mm("in_bg", u, w_bg, trans_b=True, tb=tb_mm, tn=128)
    qkv = _a_pre_fwd(z_qkv, full['a_conv_w'])
    (bg,) = _tok_fwd("bg_fwd", bg_fn, [z_bg], [log_rate, dt_bias], [(128, F32)], tb_mm)
    o_dn, dn_hist, dn_tinv = _dn_fwd(qkv, bg)
    zf = _shift_fwd(z_b, mu)
    rr, ww, kk2, vv, av, bv, gate = _tok_fwd("b_pre_fwd", _b_pre_fn, [zf], b_pars, [(d, F32)] * 7, tb_vjp)
    per_head = lambda t: t.reshape(rows, d // HEAD_B, HEAD_B)
    y_heads, b_hist, b_last = _rwkv_fwd(rr, ww, kk2, per_head(vv), av, bv)
    y_b = y_heads.reshape(rows, d)
    w_out_stack, full['ffn2_w_gu'], down2 = finish_gather("late", gather_late, y_heads)
    full['w_out'], full['ffn2_w_down'] = w_out_stack.reshape(-1, d), down2.reshape(-1, d)
    post_toks = [o_dn, z_zg, y_b, rr, kk2, vv, gate]
    (merged,) = _tok_fwd("post_fwd", _post_fn, post_toks, post_pars, [(d, BF16)], tb_vjp)
    h2 = _mm("out_proj", merged, full['w_out'], add=h1, tb=tb_mm, tn=d)

    target = jnp.pad(loss_target[0], ((CHUNK, 0), (0, 0)))
    dh3, g_final, loss_part = _ffn_fwd_loss("ffn2_fwd_loss", h2, full['ffn2_norm'], full['ffn2_w_gu'],
                                            full['ffn2_w_down'], target, full['final_norm'].reshape(1, d), tb_mm)

    def ffn_backward(tag, h, dout, key_norm, key_gu, key_down, once_down=None):
        dh, dh_bf, dgain, xn, act, dgate, dup, dhalf = _ffn_bwd(tag + "_bwd", h, full[key_norm], dout, full[key_gu],
                                                                full[key_down], tb_mm)
        fc = dgate.shape[2]
        d_down = _mm_tn_from_slabs(tag + "_dw_down", act, dhalf, tk=tb_dw).reshape(N_DEV, -1, d)
        token = once_down(dh, dgain, d_down) if once_down else None
        d_gu = jnp.concatenate(
            [_mm_tn_from_slabs(tag + "_dw_gate", dgate, xn, tk=tb_dw, after=token).reshape(-1, fc, d),
             _mm_tn_from_slabs(tag + "_dw_up", dup, xn, tk=tb_dw).reshape(-1, fc, d)], axis=0)
        return dh, dh_bf, dgain, d_gu, d_down

    dh2, dh2_bf, g_ffn2_norm, g_ffn2_gu, g_ffn2_down = ffn_backward("ffn2", h2, dh3, 'ffn2_norm', 'ffn2_w_gu',
                                                                    'ffn2_w_down')
    g_w_out = _mm_tn("dw_out", merged, dh2_bf, tm=d, tn=d, tk=tb_dw).reshape(N_DEV, -1, d)

    def start_exchange(tag, keys, slabs, after=None):
        sems, kept, lands, token = _exchange_start("exchange_start_" + tag, slabs, after)
        return (tag, keys, sems, kept, lands), token

    ex_ffn2, token_ffn2 = start_exchange("ffn2", ['ffn2_w_gu', 'ffn2_w_down', 'w_out'], [g_ffn2_gu, g_ffn2_down, g_w_out])
    dmerged = _mm("d_merged", dh2_bf, full['w_out'], trans_b=True, after=token_ffn2, tb=tb_mm, tn=d)
    post_grads = _tok_bwd("post_bwd", _post_fn, post_toks, post_pars, [[dmerged]], list(range(7)), tb_vjp,
                          [F32, BF16] + [F32] * 5)
    do_dn, dz_zg, dy_b, dr1, dk1, dv1, dgate = post_grads[:7]
    g_out_norm, g_r_k, g_ln_g, g_ln_b = post_grads[7:]
    dr2, dw2, dk2, dv_heads, da2, db2 = _rwkv_bwd(rr, ww, kk2, per_head(vv), av, bv, b_hist, b_last, per_head(dy_b))
    dv2 = dv_heads.reshape(rows, d)
    b_grads = _tok_bwd("b_pre_bwd", _b_pre_fn, [zf], b_pars,
                       [[dr1, dr2], [dw2], [dk1, dk2], [dv1, dv2], [da2], [db2], [dgate]], [0], tb_vjp)
    dzf = b_grads[0]
    g_w0, g_w_up, g_a0, g_a_up, g_g_up, g_k_k, g_k_a = b_grads[1:]
    dz_b, g_mu = _shift_bwd(z_b, mu, dzf)
    dqkv, dbg = _dn_bwd(qkv, bg, dn_hist, dn_tinv, do_dn)
    dz_qkv, g_conv = _a_pre_bwd(z_qkv, full['a_conv_w'], dqkv)
    dz_bg, g_log_rate, g_dt_bias = _tok_bwd("bg_bwd", bg_fn, [z_bg], [log_rate, dt_bias], [[dbg]], [0], tb_mm, [BF16])

    small_early = {
        'a_conv_w': g_conv, 'a_log_rate': g_log_rate[:, heads_a:2 * heads_a],
        'a_dt_bias': g_dt_bias[:, heads_a:2 * heads_a], 'a_out_norm': g_out_norm, 'b_shift_mu': g_mu[:, :n_b],
        'b_w0': g_w0, 'b_w_up': g_w_up[:LORA_W], 'b_a0': g_a0, 'b_a_up': g_a_up[LORA_W:], 'b_g_up': g_g_up[:LORA_G],
        'b_k_k': g_k_k, 'b_k_a': g_k_a, 'b_r_k': g_r_k, 'b_ln_gain': g_ln_g, 'b_ln_bias': g_ln_b,
        'ffn2_norm': g_ffn2_norm, 'final_norm': g_final,
    }
    gather_small = _exchange_start("gather_start_small", [_pack(list(small_early.values()))], gather=True)

    du = None
    g_w_in_parts = []
    for tag, dz, wpiece in (("qkv", dz_qkv, w_qkv), ("zg", dz_zg, w_zg), ("b", dz_b, w_b), ("bg", dz_bg, w_bg)):
        du = _mm("du_" + tag, dz, wpiece, add=du, after=gather_small[-1] if du is None else None, tb=tb_mm, tn=d)
        g_w_in_parts.append(_mm_tn("dw_in_" + tag, dz, u, tm=_col_tile(dz.shape[1], 1536), tn=d, tk=tb_dw))
    gp_qkv, gp_zg, gp_b, gp_bg = g_w_in_parts
    g_w_in = jnp.concatenate([gp_qkv, gp_zg[:d], gp_bg[:2 * heads_a], gp_b[:n_b], gp_zg[d:]],
                             axis=0).reshape(N_DEV, -1, d)
    ex_w_in, token_w_in = start_exchange("w_in", ['w_in'], [g_w_in])
    dh1, g_mix_norm = _tok_bwd("mix_norm_bwd", _norm_res_fn, [h1], [full['mix_norm']], [[du], [dh2]], [0], tb_vjp,
                               after=token_w_in)
    tail = {}

    def once_ffn1_down(dh0, g_ffn1_norm, g_ffn1_down):
        tail['late'] = {'meta_tokens': dh0[PAD:CHUNK], 'ffn1_norm': g_ffn1_norm, 'mix_norm': g_mix_norm}
        tail['packed'] = _pack(list(tail['late'].values()) + [loss_part[:, :1]])
        (tail['parts'],) = _all_gather("gather_small_grads", [tail['packed']])
        tail['ex_down'], token = start_exchange("ffn1_down", ['ffn1_w_down'], [g_ffn1_down], after=tail['parts'])
        return token

    dh0, _, _, g_ffn1_gu, _ = ffn_backward("ffn1", h0, dh1, 'ffn1_norm', 'ffn1_w_gu', 'ffn1_w_down', once_ffn1_down)
    ex_ffn1_gu, _ = start_exchange("ffn1_gu", ['ffn1_w_gu'], [g_ffn1_gu])
    small_late, packed_late, late_parts = tail['late'], tail['packed'], tail['parts']
    (early_parts,) = finish_gather("small", gather_small, late_parts)
    pieces = (_unpack(_sum_slabs("sum_small_early", early_parts, early_parts.shape[1]),
                      [g.shape for g in small_early.values()])
              + _unpack(_sum_slabs("sum_small_grads", late_parts, packed_late.shape[0]),
                        [g.shape for g in small_late.values()] + [(1, 1)]))
    small_names = list(small_early) + list(small_late)
    small_grad = dict(zip(small_names, pieces[:-1]))
    loss = pieces[-1].reshape(())

    grads, deltas, new_m, new_v = {}, {}, {}, {}
    local_small = {}
    for k in small_names:
        g = small_grad[k]
        if k in small_sharded:
            width = wts[k].shape[-1]
            g = lax.dynamic_slice_in_dim(g, me * width, width, axis=1)
        local_small[k] = g.reshape(wts[k].shape)
    pk = lambda tree: _pack([tree[k] for k in small_names])
    dl_s, m_s, v_s = _adamw_small(pk(wts), pk(local_small), pk(mom_m), pk(mom_v))
    shapes = [wts[k].shape for k in small_names]
    for k, dl, m2, v2 in zip(small_names, _unpack(dl_s, shapes), _unpack(m_s, shapes), _unpack(v_s, shapes)):
        grads[k], deltas[k], new_m[k], new_v[k] = local_small[k], dl, m2, v2

    done = dl_s
    for tag, keys, sems, kept, lands in (ex_ffn2, ex_w_in, tail['ex_down'], ex_ffn1_gu):
        kept, lands = _exchange_wait("exchange_wait_" + tag, sems, kept, lands, done)
        for k, slabs, landed in zip(keys, kept, lands):
            own = lax.dynamic_index_in_dim(slabs, me, axis=0, keepdims=False)
            res = _adamw("adamw_" + k, own, landed, shard_of(wts, k), shard_of(mom_m, k), shard_of(mom_v, k))
            done = res[1]
            res = [(t.T if k in col_sharded else t)[None] for t in res]
            grads[k], deltas[k], new_m[k], new_v[k] = res

    grad_x = dh0[CHUNK:][None]
    return (loss, grad_x, *[grads[k] for k in names], *[deltas[k] for k in names],
            *[new_m[k] for k in names], *[new_v[k] for k in names])
```

```python
import functools

import jax
import jax.numpy as jnp
from jax import lax
from jax.experimental import pallas as pl
from jax.experimental.pallas import tpu as pltpu

F32 = jnp.float32
BF16 = jnp.bfloat16
N_DEV = 8
N_META = 16
CHUNK = 64
PAD = CHUNK - N_META
HEAD_A = 128
HEAD_B = 64
LORA_W, LORA_A, LORA_G = 64, 64, 160
LORA_PAD = 384
EPS = 1e-6
GN_EPS = HEAD_B * 1e-5
ADAM_LR, ADAM_B1, ADAM_B2, ADAM_EPS, ADAM_WD, ADAM_STEP = 0.001, 0.9, 0.999, 1e-08, 0.01, 10
SCAN_STEPS = 32
MXU_WIDTH = 256
VMEM_LIMIT = 56 * 1024 * 1024
DN_PRECISION = lax.Precision.HIGH
MESH_ID = pl.DeviceIdType.MESH
ANY = pl.BlockSpec(memory_space=pl.ANY)
HBM_SPEC = pl.BlockSpec(memory_space=pltpu.HBM)
SEM_SPEC = pl.BlockSpec(memory_space=pltpu.SEMAPHORE)


def _cp(*sem):
    return pltpu.CompilerParams(dimension_semantics=sem, vmem_limit_bytes=VMEM_LIMIT)


def _tb(t, target):
    best = 16
    for d in range(16, target + 1, 16):
        if t % d == 0:
            best = d
    return best


def _sigmoid(x):
    return 1.0 / (1.0 + jnp.exp(-x))


def _silu(x):
    return x * _sigmoid(x)


def _softplus(x):
    return jnp.maximum(x, 0.0) + jnp.log(1.0 + jnp.exp(-jnp.abs(x)))


def _dot_nt(a, b, precision=None):
    return lax.dot_general(a, b, (((1,), (1,)), ((), ())), preferred_element_type=F32, precision=precision)


def _dot_tn(a, b, precision=None):
    return lax.dot_general(a, b, (((0,), (0,)), ((), ())), preferred_element_type=F32, precision=precision)


def _dot(a, b, precision=None):
    return jnp.dot(a, b, preferred_element_type=F32, precision=precision)


def _block_diag_ones():
    i = lax.broadcasted_iota(jnp.int32, (MXU_WIDTH, MXU_WIDTH), 0) // HEAD_B
    j = lax.broadcasted_iota(jnp.int32, (MXU_WIDTH, MXU_WIDTH), 1) // HEAD_B
    return (i == j).astype(BF16)


def _hi_lo(x):
    hi = x.astype(BF16)
    return hi, (x - hi.astype(F32)).astype(BF16)


def _segsum_many(xs, bd):
    groups = [x if isinstance(x, tuple) else _hi_lo(x) for x in xs]
    rows = groups[0][0].shape[0]
    stacked = jnp.concatenate([p for grp in groups for p in grp], axis=0)
    out = jnp.concatenate([_dot(stacked[:, s:s + MXU_WIDTH], bd) for s in range(0, stacked.shape[1], MXU_WIDTH)], axis=1)
    res, pos = [], 0
    for grp in groups:
        acc = out[pos:pos + rows]
        for j in range(1, len(grp)):
            acc = acc + out[pos + j * rows:pos + (j + 1) * rows]
        res.append(acc)
        pos += len(grp) * rows
    return res


def _segsum_impl(x):
    return _segsum_many([x], _block_diag_ones())[0]


@jax.custom_vjp
def _segsum64(x):
    return _segsum_impl(x)


_segsum64.defvjp(lambda x: (_segsum_impl(x), None), lambda _, ct: (_segsum_impl(ct),))


def _tok(t):
    return t if isinstance(t, tuple) else (t, t.shape[1], 0)


def _tok_spec(tb, width, colblk):
    return pl.BlockSpec((tb, width), lambda i: (i, colblk))


def _par_spec(p):
    return pl.BlockSpec(p.shape, lambda i: (0, 0))


def _tok_fwd(name, fn, toks, pars, outs, tb):
    toks = [_tok(t) for t in toks]
    rows = toks[0][0].shape[0]
    n_in = len(toks) + len(pars)

    def body(*refs):
        row0 = pl.program_id(0) * tb
        res = fn(row0, *[r[...] for r in refs[:n_in]])
        for r, o in zip(refs[n_in:], res):
            r[...] = o.astype(r.dtype)

    return pl.pallas_call(
        body, name=name, grid=(rows // tb,),
        in_specs=[_tok_spec(tb, w, c) for _, w, c in toks] + [_par_spec(p) for p in pars],
        out_specs=[_tok_spec(tb, w, 0) for w, _ in outs],
        out_shape=[jax.ShapeDtypeStruct((rows, w), dt) for w, dt in outs],
        compiler_params=_cp("parallel"),
    )(*[a for a, _, _ in toks], *pars)


def _tok_bwd(name, fn, toks, pars, cts, want, tb, want_dtypes=None, after=None):
    toks = [_tok(t) for t in toks]
    want_dtypes = want_dtypes or [F32] * len(want)
    cts = [[_tok(c) for c in group] for group in cts]
    flat_cts = [c for group in cts for c in group]
    rows = toks[0][0].shape[0]
    n_tok, n_par, n_ct = len(toks), len(pars), len(flat_cts)
    extra = [] if after is None else [after]

    def body(*refs):
        i = pl.program_id(0)
        row0 = i * tb
        prim = [r[...].astype(F32) for r in refs[:n_tok + n_par]]
        ct_refs = list(refs[n_tok + n_par:n_tok + n_par + n_ct])
        out_refs = refs[n_tok + n_par + n_ct + len(extra):]
        res, vjp = jax.vjp(lambda *a: fn(row0, *a), *prim)
        ct = []
        for group, o in zip(cts, res):
            acc = None
            for _ in group:
                v = ct_refs.pop(0)[...].astype(F32)
                acc = v if acc is None else acc + v
            ct.append(acc.astype(o.dtype))
        grads = vjp(tuple(ct))
        for r, k in zip(out_refs[:len(want)], want):
            r[...] = grads[k].astype(r.dtype)

        @pl.when(i == 0)
        def _():
            for r in out_refs[len(want):]:
                r[...] = jnp.zeros_like(r)

        for r, g in zip(out_refs[len(want):], grads[n_tok:]):
            r[...] += g

    return pl.pallas_call(
        body, name=name, grid=(rows // tb,),
        in_specs=[_tok_spec(tb, w, c) for _, w, c in toks] + [_par_spec(p) for p in pars]
        + [_tok_spec(tb, w, c) for _, w, c in flat_cts] + [ANY] * len(extra),
        out_specs=[_tok_spec(tb, toks[k][1], 0) for k in want] + [_par_spec(p) for p in pars],
        out_shape=[jax.ShapeDtypeStruct((rows, toks[k][1]), dt) for k, dt in zip(want, want_dtypes)]
        + [jax.ShapeDtypeStruct(p.shape, F32) for p in pars],
        compiler_params=_cp("arbitrary"),
    )(*[a for a, _, _ in toks], *pars, *[a for a, _, _ in flat_cts], *extra)


def _mm(name, a, b, *, trans_b=False, add=None, after=None, tb, tn):
    rows, k = a.shape
    n = b.shape[0] if trans_b else b.shape[1]

    def body(*refs):
        a_ref, b_ref = refs[:2]
        o_ref = refs[-1]
        acc = _dot_nt(a_ref[...], b_ref[...]) if trans_b else _dot(a_ref[...], b_ref[...])
        if add is not None:
            acc = acc + refs[2][...]
        o_ref[...] = acc

    in_specs = [pl.BlockSpec((tb, k), lambda i, j: (i, 0)),
                pl.BlockSpec((tn, k), lambda i, j: (j, 0)) if trans_b else pl.BlockSpec((k, tn), lambda i, j: (0, j))]
    args = [a, b]
    if add is not None:
        in_specs.append(pl.BlockSpec((tb, tn), lambda i, j: (i, j)))
        args.append(add)
    if after is not None:
        in_specs.append(ANY)
        args.append(after)
    return pl.pallas_call(
        body, name=name, grid=(rows // tb, n // tn), in_specs=in_specs,
        out_specs=pl.BlockSpec((tb, tn), lambda i, j: (i, j)),
        out_shape=jax.ShapeDtypeStruct((rows, n), F32),
        compiler_params=_cp("parallel", "parallel"),
    )(*args)


def _mm_tn_call(name, grid, a, b, a_spec, b_spec, o_spec, acc_shape, out_shape, after=None):
    last = len(grid) - 1
    extra = [] if after is None else [after]

    def body(a_ref, b_ref, *rest):
        o_ref, acc_ref = rest[len(extra):]
        k = pl.program_id(last)

        @pl.when(k == 0)
        def _():
            acc_ref[...] = jnp.zeros_like(acc_ref)

        a_blk = a_ref[0] if len(a_ref.shape) == 3 else a_ref[...]
        b_blk = b_ref[0] if len(b_ref.shape) == 3 else b_ref[...]
        acc_ref[...] += _dot_tn(a_blk, b_blk)

        @pl.when(k == grid[last] - 1)
        def _():
            if len(o_ref.shape) == 3:
                o_ref[0] = acc_ref[...].astype(o_ref.dtype)
            else:
                o_ref[...] = acc_ref[...].astype(o_ref.dtype)

    return pl.pallas_call(
        body, name=name, grid=grid, in_specs=[a_spec, b_spec] + [ANY] * len(extra), out_specs=o_spec,
        out_shape=jax.ShapeDtypeStruct(out_shape, BF16), scratch_shapes=[pltpu.VMEM(acc_shape, F32)],
        compiler_params=_cp(*(["parallel"] * last + ["arbitrary"])),
    )(a, b, *extra)


def _mm_tn(name, a, b, *, tm, tn, tk):
    rows, m = a.shape
    n = b.shape[1]
    return _mm_tn_call(name, (m // tm, n // tn, rows // tk), a, b,
                       pl.BlockSpec((tk, tm), lambda i, j, k: (k, i)), pl.BlockSpec((tk, tn), lambda i, j, k: (k, j)),
                       pl.BlockSpec((tm, tn), lambda i, j, k: (i, j)), (tm, tn), (m, n))


def _mm_tn_from_slabs(name, a3, b, *, tk, after=None):
    s, rows, c = a3.shape
    n = b.shape[1]
    return _mm_tn_call(name, (s, rows // tk), a3, b,
                       pl.BlockSpec((1, tk, c), lambda i, k: (i, k, 0)), pl.BlockSpec((tk, n), lambda i, k: (k, 0)),
                       pl.BlockSpec((c, n), lambda i, k: (i, 0)), (c, n), (s * c, n), after)


def _col_tile(n, target):
    if n <= target:
        return n
    best = 128
    for d in range(128, target + 1, 128):
        if n % d == 0:
            best = d
    return best


def _rms(x, gain):
    return x * lax.rsqrt(jnp.mean(x * x, axis=-1, keepdims=True) + EPS) * gain


def _ffn_specs(d, fc, nj):
    return [pl.BlockSpec((1, fc, d), lambda i, j: (j, 0, 0)), pl.BlockSpec((1, fc, d), lambda i, j: (nj + j, 0, 0)),
            pl.BlockSpec((fc, d), lambda i, j: (j, 0))]


def _ffn_fwd(name, h, gain, wgu, wd, tb):
    rows, d = h.shape
    nj = wgu.shape[0] // 2
    fc = wgu.shape[1]

    def body(h_ref, g_ref, wg_ref, wu_ref, wd_ref, o_ref, xn_s, acc_s):
        j = pl.program_id(1)

        @pl.when(j == 0)
        def _():
            xn_s[...] = _rms(h_ref[...], g_ref[...]).astype(BF16)
            acc_s[...] = jnp.zeros_like(acc_s)

        wg, wu, wdn = wg_ref[0], wu_ref[0], wd_ref[...]
        for half in range(2):
            rs = pl.ds(half * (tb // 2), tb // 2)
            xn = xn_s[rs, :]
            gate = _dot_nt(xn, wg)
            up = _dot_nt(xn, wu)
            acc_s[rs, :] += _dot((_silu(gate) * up).astype(BF16), wdn)

        @pl.when(j == nj - 1)
        def _():
            o_ref[...] = h_ref[...] + 0.5 * acc_s[...]

    return pl.pallas_call(
        body, name=name, grid=(rows // tb, nj),
        in_specs=[pl.BlockSpec((tb, d), lambda i, j: (i, 0)), pl.BlockSpec((1, d), lambda i, j: (0, 0))]
        + _ffn_specs(d, fc, nj),
        out_specs=pl.BlockSpec((tb, d), lambda i, j: (i, 0)),
        out_shape=jax.ShapeDtypeStruct((rows, d), F32),
        scratch_shapes=[pltpu.VMEM((tb, d), BF16), pltpu.VMEM((tb, d), F32)],
        compiler_params=_cp("parallel", "arbitrary"),
    )(h, gain, wgu, wgu, wd)


def _ffn_bwd(name, h, gain, dout, wgu, wd, tb):
    rows, d = h.shape
    nj = wgu.shape[0] // 2
    fc = wgu.shape[1]

    def body(h_ref, g_ref, do_ref, wg_ref, wu_ref, wd_ref,
             dh_ref, dhb_ref, dg_ref, xn_ref, act_ref, dgate_ref, dup_ref, dhalf_ref, dxn_s):
        i, j = pl.program_id(0), pl.program_id(1)

        @pl.when(j == 0)
        def _():
            xn_ref[...] = _rms(h_ref[...], g_ref[...]).astype(BF16)
            dhalf_ref[...] = (0.5 * do_ref[...]).astype(BF16)
            dxn_s[...] = jnp.zeros_like(dxn_s)

        wg, wu, wdn = wg_ref[0], wu_ref[0], wd_ref[...]
        for half in range(2):
            rs = pl.ds(half * (tb // 2), tb // 2)
            xn = xn_ref[rs, :]
            gate = _dot_nt(xn, wg)
            up = _dot_nt(xn, wu)
            sg = _sigmoid(gate)
            dact = _dot_nt(dhalf_ref[rs, :], wdn)
            act_ref[0, rs, :] = (gate * sg * up).astype(BF16)
            dgate = (dact * up * (sg * (1.0 + gate * (1.0 - sg)))).astype(BF16)
            dup = (dact * gate * sg).astype(BF16)
            dgate_ref[0, rs, :] = dgate
            dup_ref[0, rs, :] = dup
            dxn_s[rs, :] += _dot(dgate, wg) + _dot(dup, wu)

        @pl.when((i == 0) & (j == 0))
        def _():
            dg_ref[...] = jnp.zeros_like(dg_ref)

        @pl.when(j == nj - 1)
        def _():
            x = h_ref[...]
            r = lax.rsqrt(jnp.mean(x * x, axis=-1, keepdims=True) + EPS)
            dxn = dxn_s[...]
            dyg = dxn * g_ref[...]
            dh = do_ref[...] + r * dyg - x * (r * r * r) * jnp.mean(dyg * x, axis=-1, keepdims=True)
            dh_ref[...] = dh
            dhb_ref[...] = dh.astype(BF16)
            dg_ref[...] += jnp.sum(dxn * x * r, axis=0, keepdims=True)

    row_d = pl.BlockSpec((tb, d), lambda i, j: (i, 0))
    slab = pl.BlockSpec((1, tb, fc), lambda i, j: (j, i, 0))
    hidden = jax.ShapeDtypeStruct((nj, rows, fc), BF16)
    return pl.pallas_call(
        body, name=name, grid=(rows // tb, nj),
        in_specs=[row_d, pl.BlockSpec((1, d), lambda i, j: (0, 0)), row_d] + _ffn_specs(d, fc, nj),
        out_specs=[row_d, row_d, pl.BlockSpec((1, d), lambda i, j: (0, 0)), row_d, slab, slab, slab, row_d],
        out_shape=[jax.ShapeDtypeStruct((rows, d), F32), jax.ShapeDtypeStruct((rows, d), BF16),
                   jax.ShapeDtypeStruct((1, d), F32), jax.ShapeDtypeStruct((rows, d), BF16),
                   hidden, hidden, hidden, jax.ShapeDtypeStruct((rows, d), BF16)],
        scratch_shapes=[pltpu.VMEM((tb, d), F32)],
        compiler_params=_cp("arbitrary", "arbitrary"),
    )(h, gain, dout, wgu, wgu, wd)


def _shift_rows(x, s):
    return pltpu.roll(x, s % x.shape[0], 0)


def _a_post(c, which):
    s = _silu(c)
    n = s * lax.rsqrt(jnp.sum(s * s, axis=-1, keepdims=True) + 1e-6)
    scale = jnp.where(which == 0, HEAD_A ** -0.5, 1.0)
    return jnp.where(which == 2, s, n * scale)


def _conv(x, w):
    return x * w[3:4] + _shift_rows(x, 1) * w[2:3] + _shift_rows(x, 2) * w[1:2] + _shift_rows(x, 3) * w[0:1]


def _a_pre_fwd(zqkv, conv_w):
    rows, width = zqkv.shape
    heads = width // (3 * HEAD_A)

    def body(x_ref, w_ref, o_ref):
        which = pl.program_id(0) // heads
        live = lax.broadcasted_iota(jnp.int32, (rows, HEAD_A), 0) >= PAD
        o_ref[...] = jnp.where(live, _a_post(_conv(x_ref[...], w_ref[...]), which), 0.0)

    return pl.pallas_call(
        body, name="a_pre_fwd", grid=(width // HEAD_A,),
        in_specs=[pl.BlockSpec((rows, HEAD_A), lambda c: (0, c)), pl.BlockSpec((4, HEAD_A), lambda c: (0, c))],
        out_specs=pl.BlockSpec((rows, HEAD_A), lambda c: (0, c)),
        out_shape=jax.ShapeDtypeStruct((rows, width), F32),
        compiler_params=_cp("parallel"),
    )(zqkv, conv_w)


def _a_pre_bwd(zqkv, conv_w, dqkv):
    rows, width = zqkv.shape
    heads = width // (3 * HEAD_A)

    def body(x_ref, w_ref, ct_ref, dx_ref, dw_ref):
        which = pl.program_id(0) // heads
        live = lax.broadcasted_iota(jnp.int32, (rows, HEAD_A), 0) >= PAD
        x, w = x_ref[...], w_ref[...]
        _, vjp = jax.vjp(lambda c: _a_post(c, which), _conv(x, w))
        (dc,) = vjp(jnp.where(live, ct_ref[...], 0.0))
        dc = jnp.where(live, dc, 0.0)
        dx_ref[...] = (dc * w[3:4] + _shift_rows(dc, -1) * w[2:3] + _shift_rows(dc, -2) * w[1:2]
                       + _shift_rows(dc, -3) * w[0:1]).astype(BF16)
        dw_ref[...] = jnp.concatenate(
            [jnp.sum(dc * (_shift_rows(x, 3 - j) if j < 3 else x), axis=0, keepdims=True) for j in range(4)], axis=0)

    col = pl.BlockSpec((rows, HEAD_A), lambda c: (0, c))
    wsp = pl.BlockSpec((4, HEAD_A), lambda c: (0, c))
    return pl.pallas_call(
        body, name="a_pre_bwd", grid=(width // HEAD_A,),
        in_specs=[col, wsp, col], out_specs=[col, wsp],
        out_shape=[jax.ShapeDtypeStruct((rows, width), BF16), jax.ShapeDtypeStruct((4, width), F32)],
        compiler_params=_cp("parallel"),
    )(zqkv, conv_w, dqkv)


SHIFT_TILE = 384


def _shift_fwd(zb, mu):
    rows, width = zb.shape

    def body(x_ref, mu_ref, o_ref):
        x = x_ref[...]
        first = lax.broadcasted_iota(jnp.int32, x.shape, 0) == 0
        prev = jnp.where(first, 0.0, _shift_rows(x, 1))
        o_ref[...] = x + (prev - x) * mu_ref[...]

    col = pl.BlockSpec((rows, SHIFT_TILE), lambda c: (0, c))
    return pl.pallas_call(
        body, name="shift_fwd", grid=(width // SHIFT_TILE,),
        in_specs=[col, pl.BlockSpec((1, SHIFT_TILE), lambda c: (0, c))], out_specs=col,
        out_shape=jax.ShapeDtypeStruct((rows, width), F32), compiler_params=_cp("parallel"),
    )(zb, mu)


def _shift_bwd(zb, mu, dzf):
    rows, width = zb.shape

    def body(x_ref, mu_ref, ct_ref, dx_ref, dmu_ref):
        x, ct, mu_v = x_ref[...], ct_ref[...], mu_ref[...]
        row = lax.broadcasted_iota(jnp.int32, x.shape, 0)
        prev = jnp.where(row == 0, 0.0, _shift_rows(x, 1))
        nxt = jnp.where(row == rows - 1, 0.0, _shift_rows(ct, -1))
        dx_ref[...] = (ct * (1.0 - mu_v) + nxt * mu_v).astype(BF16)
        dmu_ref[...] = jnp.sum(ct * (prev - x), axis=0, keepdims=True)

    col = pl.BlockSpec((rows, SHIFT_TILE), lambda c: (0, c))
    msp = pl.BlockSpec((1, SHIFT_TILE), lambda c: (0, c))
    return pl.pallas_call(
        body, name="shift_bwd", grid=(width // SHIFT_TILE,),
        in_specs=[col, msp, col], out_specs=[col, msp],
        out_shape=[jax.ShapeDtypeStruct((rows, width), BF16), jax.ShapeDtypeStruct((1, width), F32)],
        compiler_params=_cp("parallel"),
    )(zb, mu, dzf)


def _neumann_inverse(p):
    heads = range(len(p))
    eye = (lax.broadcasted_iota(jnp.int32, (CHUNK, CHUNK), 0)
           == lax.broadcasted_iota(jnp.int32, (CHUNK, CHUNK), 1)).astype(F32)
    tinv = [eye + p[h] for h in heads]
    for _ in range(5):
        p = [_dot(p[h], p[h], DN_PRECISION) for h in heads]
        tinv = [tinv[h] + _dot(tinv[h], p[h], DN_PRECISION) for h in heads]
    return tinv


@jax.custom_vjp
def _unit_lower_inverse(p):
    return _neumann_inverse(p)


def _unit_lower_inverse_fwd(p):
    tinv = _neumann_inverse(p)
    return tinv, tinv


def _unit_lower_inverse_bwd(tinv, ct):
    heads = range(len(tinv))
    left = [_dot_tn(tinv[h], ct[h], DN_PRECISION) for h in heads]
    return ([_dot_nt(left[h], tinv[h], DN_PRECISION) for h in heads],)


_unit_lower_inverse.defvjp(_unit_lower_inverse_fwd, _unit_lower_inverse_bwd)


@jax.custom_vjp
def _known_inverse(p, tinv):
    return tinv


_known_inverse.defvjp(lambda p, tinv: (tinv, tinv),
                      lambda tinv, ct: (_unit_lower_inverse_bwd(tinv, ct)[0], [jnp.zeros_like(t) for t in tinv]))


def _dn_chunk(q, k, v, beta, g, state, saved_tinv=None):
    heads = range(len(q))
    ri = lax.broadcasted_iota(jnp.int32, (CHUNK, CHUNK), 0)
    ci = lax.broadcasted_iota(jnp.int32, (CHUNK, CHUNK), 1)
    eye = (ri == ci).astype(F32)
    incl = ri >= ci
    last = lax.broadcasted_iota(jnp.int32, (CHUNK, 1), 0) == CHUNK - 1
    g_row = [jnp.sum(g[h] * eye, axis=0, keepdims=True) for h in heads]
    gc = [jnp.sum(jnp.where(incl, g_row[h], 0.0), axis=1, keepdims=True) for h in heads]
    gc_row = [jnp.sum(gc[h] * eye, axis=0, keepdims=True) for h in heads]
    decay = [jnp.where(incl, jnp.exp(jnp.where(incl, gc[h] - gc_row[h], 0.0)), 0.0) for h in heads]
    kb = [k[h] * beta[h] for h in heads]
    vb = [v[h] * beta[h] for h in heads]
    p = [-jnp.where(ri > ci, _dot_nt(kb[h], k[h]) * decay[h], 0.0) for h in heads]
    tinv = _unit_lower_inverse(p) if saved_tinv is None else _known_inverse(p, saved_tinv)
    eg = [jnp.exp(gc[h]) for h in heads]
    u = [_dot(tinv[h], vb[h]) for h in heads]
    wk = [_dot(tinv[h], kb[h] * eg[h]) for h in heads]
    attn = [_dot_nt(q[h], k[h]) * decay[h] for h in heads]
    g_last = [jnp.sum(jnp.where(last, gc[h], 0.0), axis=0, keepdims=True) for h in heads]
    k_tail = [k[h] * jnp.exp(g_last[h] - gc[h]) for h in heads]
    v_new = [u[h] - _dot(wk[h], state[h]) for h in heads]
    o = [_dot(q[h] * eg[h], state[h]) + _dot(attn[h], v_new[h]) for h in heads]
    new = [state[h] * jnp.exp(g_last[h]) + _dot_tn(k_tail[h], v_new[h]) for h in heads]
    return (o, new, tinv) if saved_tinv is None else (o, new)


def _bg_cols(bg, h, heads):
    lane = lax.broadcasted_iota(jnp.int32, bg.shape, 1)
    beta = jnp.sum(jnp.where(lane == h, bg, 0.0), axis=1, keepdims=True)
    g = jnp.sum(jnp.where(lane == heads + h, bg, 0.0), axis=1, keepdims=True)
    return beta, g


def _dn_fwd(qkv, bg):
    rows = qkv.shape[0]
    heads = qkv.shape[1] // (3 * HEAD_A)
    n = rows // CHUNK
    hp, groups = heads, 1

    def body(q_ref, k_ref, v_ref, bg_ref, o_ref, hist_ref, tinv_ref, s_ref):
        c, grp = pl.program_id(0), pl.program_id(1)

        @pl.when(c == 0)
        def _():
            for i in range(hp):
                s_ref[grp * hp + i] = jnp.zeros((HEAD_A, HEAD_A), F32)

        bg_v = bg_ref[...]
        cols = [slice(i * HEAD_A, (i + 1) * HEAD_A) for i in range(hp)]
        state = [s_ref[grp * hp + i] for i in range(hp)]
        beta_g = [_bg_cols(bg_v, grp * hp + i, heads) for i in range(hp)]
        o, new, tinv = _dn_chunk([q_ref[:, c_] for c_ in cols], [k_ref[:, c_] for c_ in cols],
                                 [v_ref[:, c_] for c_ in cols], [b for b, _ in beta_g], [g for _, g in beta_g], state)
        for i in range(hp):
            hist_ref[0, i] = state[i]
            tinv_ref[0, i] = tinv[i]
            o_ref[:, cols[i]] = o[i]
            s_ref[grp * hp + i] = new[i]

    def part(p):
        return pl.BlockSpec((CHUNK, hp * HEAD_A), lambda c, grp: (c, p * groups + grp))

    return pl.pallas_call(
        body, name="deltanet_fwd", grid=(n, groups),
        in_specs=[part(0), part(1), part(2), pl.BlockSpec((CHUNK, 128), lambda c, grp: (c, 0))],
        out_specs=[part(0), pl.BlockSpec((1, hp, HEAD_A, HEAD_A), lambda c, grp: (c, grp, 0, 0)),
                   pl.BlockSpec((1, hp, CHUNK, CHUNK), lambda c, grp: (c, grp, 0, 0))],
        out_shape=[jax.ShapeDtypeStruct((rows, heads * HEAD_A), F32),
                   jax.ShapeDtypeStruct((n, heads, HEAD_A, HEAD_A), F32),
                   jax.ShapeDtypeStruct((n, heads, CHUNK, CHUNK), F32)],
        scratch_shapes=[pltpu.VMEM((heads, HEAD_A, HEAD_A), F32)],
        compiler_params=_cp("arbitrary", "arbitrary"),
    )(qkv, qkv, qkv, bg)


def _dn_bwd(qkv, bg, hist, tinv_hist, do):
    rows = qkv.shape[0]
    heads = qkv.shape[1] // (3 * HEAD_A)
    n = rows // CHUNK
    hp, groups = heads, 1

    def body(q_ref, k_ref, v_ref, bg_ref, hist_ref, tinv_ref, do_ref, dqkv_ref, dbg_ref, ds_ref):
        c, grp = pl.program_id(0), pl.program_id(1)

        @pl.when(c == 0)
        def _():
            for i in range(hp):
                ds_ref[grp * hp + i] = jnp.zeros((HEAD_A, HEAD_A), F32)

        bg_v = bg_ref[...]
        lane = lax.broadcasted_iota(jnp.int32, (CHUNK, 128), 1)
        cols = [slice(i * HEAD_A, (i + 1) * HEAD_A) for i in range(hp)]
        beta_g = [_bg_cols(bg_v, grp * hp + i, heads) for i in range(hp)]
        _, vjp = jax.vjp(_dn_chunk, [q_ref[:, c_] for c_ in cols], [k_ref[:, c_] for c_ in cols],
                         [v_ref[:, c_] for c_ in cols], [b for b, _ in beta_g], [g for _, g in beta_g],
                         [hist_ref[0, i] for i in range(hp)], [tinv_ref[0, i] for i in range(hp)])
        dq, dk, dv, dbeta, dg, ds, _ = vjp(([do_ref[:, c_] for c_ in cols], [ds_ref[grp * hp + i] for i in range(hp)]))
        dbg = jnp.zeros((CHUNK, 128), F32)
        for i in range(hp):
            h = grp * hp + i
            for p, part_grad in enumerate((dq, dk, dv)):
                dqkv_ref[:, pl.ds((p * heads + i) * HEAD_A, HEAD_A)] = part_grad[i]
            ds_ref[h] = ds[i]
            dbg = dbg + jnp.where(lane == h, dbeta[i], 0.0) + jnp.where(lane == heads + h, dg[i], 0.0)

        @pl.when(grp == 0)
        def _():
            dbg_ref[...] = jnp.zeros_like(dbg_ref)

        dbg_ref[...] += dbg

    def part(p):
        return pl.BlockSpec((CHUNK, hp * HEAD_A), lambda c, grp: (n - 1 - c, p * groups + grp))

    return pl.pallas_call(
        body, name="deltanet_bwd", grid=(n, groups),
        in_specs=[part(0), part(1), part(2), pl.BlockSpec((CHUNK, 128), lambda c, grp: (n - 1 - c, 0)),
                  pl.BlockSpec((1, hp, HEAD_A, HEAD_A), lambda c, grp: (n - 1 - c, grp, 0, 0)),
                  pl.BlockSpec((1, hp, CHUNK, CHUNK), lambda c, grp: (n - 1 - c, grp, 0, 0)), part(0)],
        out_specs=[pl.BlockSpec((CHUNK, 3 * heads * HEAD_A), lambda c, grp: (n - 1 - c, 0)),
                   pl.BlockSpec((CHUNK, 128), lambda c, grp: (n - 1 - c, 0))],
        out_shape=[jax.ShapeDtypeStruct(qkv.shape, F32), jax.ShapeDtypeStruct((rows, 128), F32)],
        scratch_shapes=[pltpu.VMEM((heads, HEAD_A, HEAD_A), F32)],
        compiler_params=_cp("arbitrary", "arbitrary"),
    )(qkv, qkv, qkv, bg, hist, tinv_hist, do)


def _head_mask(heads, width):
    return (lax.broadcasted_iota(jnp.int32, (heads, width), 0)
            == lax.broadcasted_iota(jnp.int32, (heads, width), 1) // HEAD_B)


def _masked_rows(mask, row):
    return jnp.where(mask, row, 0.0).astype(BF16)


def _rwkv_fwd(r, w, k, v, a, b):
    rows, width = r.shape
    heads = width // HEAD_B
    ts = SCAN_STEPS

    def body(r_ref, w_ref, k_ref, v_ref, a_ref, b_ref, y_ref, hist_ref, s_ref):
        @pl.when(pl.program_id(0) == 0)
        def _():
            s_ref[...] = jnp.zeros_like(s_ref)

        mask = _head_mask(heads, width)
        onehot = mask.astype(BF16)
        onehot2 = jnp.concatenate([onehot, onehot], axis=0)
        bd = _block_diag_ones()

        spread_v = [_dot_tn(jnp.concatenate(_hi_lo(v_ref[j]), axis=0), onehot2) for j in range(ts)]
        a_next = pltpu.roll(a_ref[...], ts - 1, 0)
        b_dot_a, k_dot_a = _segsum_many([b_ref[...] * a_next, k_ref[...] * a_next], bd)
        s = s_ref[...]
        ys = []
        for j in range(0, ts, 2):
            row, nxt = pl.ds(j, 1), pl.ds(j + 1, 1)
            hist_ref[j] = s
            sa, base = _segsum_many([((s * a_ref[row, :]).astype(BF16),),
                                     ((s * (w_ref[row, :] * a_ref[nxt, :])).astype(BF16),)], bd)
            sa_next = base + sa * b_dot_a[j:j + 1] + spread_v[j] * k_dot_a[j:j + 1]
            s = s * w_ref[row, :] + sa * b_ref[row, :] + spread_v[j] * k_ref[row, :]
            hist_ref[j + 1] = s
            ys.append(_dot_nt(_masked_rows(mask, r_ref[row, :]), s.astype(BF16)))
            s = s * w_ref[nxt, :] + sa_next * b_ref[nxt, :] + spread_v[j + 1] * k_ref[nxt, :]
            ys.append(_dot_nt(_masked_rows(mask, r_ref[nxt, :]), s.astype(BF16)))
        for j in range(ts):
            y_ref[j] = ys[j]
        s_ref[...] = s

    blk = pl.BlockSpec((ts, width), lambda i: (i, 0))
    blk3 = pl.BlockSpec((ts, heads, HEAD_B), lambda i: (i, 0, 0))
    return pl.pallas_call(
        body, name="rwkv_fwd", grid=(rows // ts,),
        in_specs=[blk, blk, blk, blk3, blk, blk],
        out_specs=[blk3, pl.BlockSpec((ts, HEAD_B, width), lambda i: (i, 0, 0)),
                   pl.BlockSpec((HEAD_B, width), lambda i: (0, 0))],
        out_shape=[jax.ShapeDtypeStruct((rows, heads, HEAD_B), F32), jax.ShapeDtypeStruct((rows, HEAD_B, width), F32),
                   jax.ShapeDtypeStruct((HEAD_B, width), F32)],
        compiler_params=_cp("arbitrary"),
    )(r, w, k, v, a, b)


def _rwkv_bwd(r, w, k, v, a, b, hist, last, dy):
    rows, width = r.shape
    heads = width // HEAD_B
    ts = SCAN_STEPS
    nb = rows // ts

    def body(r_ref, w_ref, k_ref, v_ref, a_ref, b_ref, hist_ref, last_ref, dy_ref,
             dr_ref, dw_ref, dk_ref, dv_ref, da_ref, db_ref, g_ref, after_ref):
        @pl.when(pl.program_id(0) == 0)
        def _():
            g_ref[...] = jnp.zeros_like(g_ref)
            after_ref[...] = last_ref[...]

        mask = _head_mask(heads, width)
        onehot = mask.astype(BF16)
        bd = _block_diag_ones()

        def own_lanes(x):
            return jnp.sum(jnp.where(mask, x, 0.0), axis=0, keepdims=True)

        def colsum(x):
            return jnp.sum(x, axis=0, keepdims=True)

        dy_m = [dy_ref[j].astype(BF16) for j in range(ts)]
        spread_dy = [_dot_tn(dy_m[j], onehot) for j in range(ts)]
        state_after = [hist_ref[j + 1] if j < ts - 1 else after_ref[...] for j in range(ts)]
        dr = [own_lanes(_dot(dy_m[j], state_after[j].astype(BF16))) for j in range(ts)]
        sa_m = [_dot_nt(_masked_rows(mask, a_ref[pl.ds(j, 1), :]), hist_ref[j].astype(BF16)) for j in range(ts)]
        g = g_ref[...]
        dw, dk, db, da, dv = {}, {}, {}, {}, {}
        for j in reversed(range(ts)):
            row = pl.ds(j, 1)
            sp = hist_ref[j]
            g = g + spread_dy[j] * r_ref[row, :]
            (dsa,) = _segsum_many([((g * b_ref[row, :]).astype(BF16),)], bd)
            g_b = g.astype(BF16)
            both = _dot(jnp.concatenate([v_ref[j].astype(BF16), sa_m[j].astype(BF16)], axis=0), g_b)
            dk[j], db[j] = own_lanes(both[:heads]), own_lanes(both[heads:])
            dv[j] = _dot_nt(_masked_rows(mask, k_ref[row, :]), g_b)
            dw[j] = colsum(g * sp)
            da[j] = colsum(sp * dsa)
            g = g * w_ref[row, :] + dsa * a_ref[row, :]
        g_ref[...] = g
        after_ref[...] = hist_ref[0]
        for j in range(ts):
            dv_ref[j] = dv[j]
            for ref, vals in ((dr_ref, dr), (dw_ref, dw), (dk_ref, dk), (da_ref, da), (db_ref, db)):
                ref[pl.ds(j, 1), :] = vals[j]

    blk = pl.BlockSpec((ts, width), lambda i: (nb - 1 - i, 0))
    blk3 = pl.BlockSpec((ts, heads, HEAD_B), lambda i: (nb - 1 - i, 0, 0))
    state = pl.BlockSpec((HEAD_B, width), lambda i: (0, 0))
    return pl.pallas_call(
        body, name="rwkv_bwd", grid=(nb,),
        in_specs=[blk, blk, blk, blk3, blk, blk, pl.BlockSpec((ts, HEAD_B, width), lambda i: (nb - 1 - i, 0, 0)),
                  state, blk3],
        out_specs=[blk, blk, blk, blk3, blk, blk],
        out_shape=[jax.ShapeDtypeStruct((rows, width), F32)] * 3 + [jax.ShapeDtypeStruct((rows, heads, HEAD_B), F32)]
        + [jax.ShapeDtypeStruct((rows, width), F32)] * 2,
        scratch_shapes=[pltpu.VMEM((HEAD_B, width), F32), pltpu.VMEM((HEAD_B, width), F32)],
        compiler_params=_cp("arbitrary"),
    )(r, w, k, v, a, b, hist, last, dy)


def _live(row0, shape):
    return (row0 + lax.broadcasted_iota(jnp.int32, shape, 0)) >= PAD


def _norm_fn(row0, h, gain):
    return (_rms(h, gain),)


def _norm_res_fn(row0, h, gain):
    return _rms(h, gain), h


def _make_bg_fn(heads):
    def fn(row0, x, log_rate, dt_bias):
        lane = lax.broadcasted_iota(jnp.int32, x.shape, 1)
        beta = _sigmoid(x)
        g = -jnp.exp(log_rate) * _softplus(x + dt_bias)
        out = jnp.where(lane < heads, beta, jnp.where(lane < 2 * heads, g, 0.0))
        return (jnp.where(_live(row0, x.shape), out, 0.0),)
    return fn


def _b_pre_fn(row0, zf, w0, w_up, a0, a_up, g_up, k_k, k_a):
    d = w0.shape[1]
    r, k, v = zf[:, :d], zf[:, d:2 * d], zf[:, 2 * d:3 * d]
    lo = zf[:, 3 * d:3 * d + 128]
    lg = zf[:, 3 * d + 128:3 * d + LORA_PAD]
    lane = lax.broadcasted_iota(jnp.int32, lo.shape, 1)
    lw = _dot(jnp.where(lane < LORA_W, jnp.tanh(lo), 0.0), w_up)
    la = _dot(jnp.where(lane >= LORA_W, lo, 0.0), a_up)
    lane_g = lax.broadcasted_iota(jnp.int32, lg.shape, 1)
    gate = _dot(jnp.where(lane_g < LORA_G, _sigmoid(lg), 0.0), g_up)
    decay = jnp.exp(-jnp.exp(-_softplus(-(w0 + lw)) - 0.5))
    a = _sigmoid(a0 + la)
    kx = k * k_k
    kk = kx * lax.rsqrt(_segsum64(kx * kx) + 1e-6)
    k2 = k * (1.0 + (a - 1.0) * k_a)
    return r, decay, k2, v, -kk, kk * a, gate


def _post_fn(row0, o, zg, y, r, k2, v, gate, out_gain, r_k, ln_g, ln_b):
    d = o.shape[1]
    az, ga, gb = zg[:, :d], zg[:, d:2 * d], zg[:, 2 * d:]
    heads = d // HEAD_A
    parts = []
    for h in range(heads):
        oh = o[:, h * HEAD_A:(h + 1) * HEAD_A]
        parts.append(oh * lax.rsqrt(jnp.mean(oh * oh, axis=-1, keepdims=True) + EPS) * out_gain)
    o_a = jnp.concatenate(parts, axis=1) * _silu(az)
    mean = _segsum64(y) * (1.0 / HEAD_B)
    yc = y - mean
    var = _segsum64(yc * yc) * (1.0 / HEAD_B)
    yn = yc * lax.rsqrt(var + GN_EPS) * ln_g + ln_b
    o_b = (yn + _segsum64(r * k2 * r_k) * v) * gate
    return (_sigmoid(ga) * o_a + _sigmoid(gb) * o_b,)


def _loss(h3, target, gain, tb):
    rows, d = h3.shape

    def body(h_ref, t_ref, g_ref, dh_ref, dg_ref, l_ref):
        i = pl.program_id(0)
        live = (i * tb + lax.broadcasted_iota(jnp.int32, (tb, 1), 0)) >= CHUNK
        tgt = t_ref[...]

        def f(h, g):
            err = _rms(h, g) - tgt
            return 0.5 * jnp.sum(jnp.where(live, jnp.mean(err * err, axis=-1, keepdims=True), 0.0))

        val, vjp = jax.vjp(f, h_ref[...], g_ref[...])
        dh, dg = vjp(jnp.ones((), F32))
        dh_ref[...] = dh

        @pl.when(i == 0)
        def _():
            dg_ref[...] = jnp.zeros_like(dg_ref)
            l_ref[...] = jnp.zeros_like(l_ref)

        dg_ref[...] += dg
        l_ref[...] += jnp.full((1, 128), val, F32)

    blk = pl.BlockSpec((tb, d), lambda i: (i, 0))
    return pl.pallas_call(
        body, name="loss", grid=(rows // tb,),
        in_specs=[blk, blk, pl.BlockSpec((1, d), lambda i: (0, 0))],
        out_specs=[blk, pl.BlockSpec((1, d), lambda i: (0, 0)), pl.BlockSpec((1, 128), lambda i: (0, 0))],
        out_shape=[jax.ShapeDtypeStruct((rows, d), F32), jax.ShapeDtypeStruct((1, d), F32),
                   jax.ShapeDtypeStruct((1, 128), F32)],
        compiler_params=_cp("arbitrary"),
    )(h3, target, gain)


def _adamw_math(w, g, m, v):
    m2 = ADAM_B1 * m + (1.0 - ADAM_B1) * g
    v2 = ADAM_B2 * v + (1.0 - ADAM_B2) * (g * g)
    m_hat = m2 / (1.0 - ADAM_B1 ** ADAM_STEP)
    v_hat = v2 / (1.0 - ADAM_B2 ** ADAM_STEP)
    return -ADAM_LR * (m_hat / (jnp.sqrt(v_hat) + ADAM_EPS) + ADAM_WD * w), m2, v2


def _adamw(name, own, landed, w, m, v):
    rows, cols = w.shape
    if rows % 16 == 0:
        rb = _tb(rows, 128)
        grid, blk = (rows // rb,), pl.BlockSpec((rb, cols), lambda i: (i, 0))
        landed_blk = pl.BlockSpec((N_DEV - 1, rb, cols), lambda i: (0, i, 0))
    else:
        grid, blk = (cols // 128,), pl.BlockSpec((rows, 128), lambda i: (0, i))
        landed_blk = pl.BlockSpec((N_DEV - 1, rows, 128), lambda i: (0, 0, i))

    def body(o_ref, s_ref, w_ref, m_ref, v_ref, g_ref, d_ref, m2_ref, v2_ref):
        g = o_ref[...].astype(F32)
        for peer in range(N_DEV - 1):
            g = g + s_ref[peer].astype(F32)
        g_ref[...] = g
        d_ref[...], m2_ref[...], v2_ref[...] = _adamw_math(w_ref[...], g, m_ref[...], v_ref[...])

    return pl.pallas_call(
        body, name=name, grid=grid, in_specs=[blk, landed_blk, blk, blk, blk],
        out_specs=[blk] * 4, out_shape=[jax.ShapeDtypeStruct((rows, cols), F32)] * 4,
        compiler_params=_cp("parallel"),
    )(own, landed, w, m, v)


def _sum_slabs(name, slabs, rb):
    _, rows, cols = slabs.shape

    def body(s_ref, o_ref):
        g = s_ref[0]
        for dev in range(1, N_DEV):
            g = g + s_ref[dev]
        o_ref[...] = g

    return pl.pallas_call(
        body, name=name, grid=(rows // rb,),
        in_specs=[pl.BlockSpec((N_DEV, rb, cols), lambda i: (0, i, 0))],
        out_specs=pl.BlockSpec((rb, cols), lambda i: (i, 0)),
        out_shape=jax.ShapeDtypeStruct((rows, cols), F32), compiler_params=_cp("parallel"),
    )(slabs)


def _adamw_small(w, g, m, v):
    def body(w_ref, g_ref, m_ref, v_ref, d_ref, m2_ref, v2_ref):
        d_ref[...], m2_ref[...], v2_ref[...] = _adamw_math(w_ref[...], g_ref[...], m_ref[...], v_ref[...])

    return pl.pallas_call(body, name="adamw_small", out_shape=[jax.ShapeDtypeStruct(w.shape, F32)] * 3)(w, g, m, v)


def _place():
    return lax.axis_index("x"), lax.axis_index("y"), lax.axis_index("c")


def _index(p):
    return 4 * p[0] + 2 * p[1] + p[2]


def _all_gather(name, xs):
    n = len(xs)

    def body(*refs):
        x_refs, o_refs = refs[:n], refs[n:2 * n]
        send_sems, recv_sems, local_sems = refs[2 * n:]
        x, y, c = _place()
        me, sibling = (x, y, c), (x, y, 1 - c)
        chips = [(1 - x, y), (x, 1 - y), (1 - x, 1 - y)]

        def copy(i, k, block, to, src=None):
            dst = o_refs[i].at[_index(block)]
            return pltpu.make_async_remote_copy(src_ref=dst if src is None else src, dst_ref=dst,
                                                send_sem=send_sems.at[i, k], recv_sem=recv_sems.at[i, k],
                                                device_id=to, device_id_type=MESH_ID)

        mine = [pltpu.make_async_copy(x_refs[i], o_refs[i].at[_index(me)], local_sems.at[i]) for i in range(n)]
        for cp in mine:
            cp.start()
        first = []
        for i in range(n):
            first.append(copy(i, 0, me, sibling, src=x_refs[i]))
            first += [copy(i, 1 + j, me, (*chip, c), src=x_refs[i]) for j, chip in enumerate(chips)]
        for cp in first:
            cp.start()
        passed = []
        for j, chip in enumerate(chips):
            for i in range(n):
                copy(i, 1 + j, (*chip, c), me).wait_recv()
                cp = copy(i, 4 + j, (*chip, c), sibling)
                cp.start()
                passed.append(cp)
        for i in range(n):
            copy(i, 0, sibling, me).wait_recv()
            for j, chip in enumerate(chips):
                copy(i, 4 + j, (*chip, 1 - c), me).wait_recv()
        for cp in first + passed:
            cp.wait_send()
        for cp in mine:
            cp.wait()

    return pl.pallas_call(
        body, name=name, in_specs=[ANY] * n, out_specs=[ANY] * n,
        out_shape=[jax.ShapeDtypeStruct((N_DEV,) + x.shape, x.dtype) for x in xs],
        scratch_shapes=[pltpu.SemaphoreType.DMA((n, 7)), pltpu.SemaphoreType.DMA((n, 7)), pltpu.SemaphoreType.DMA((n,))],
    )(*xs)


def _exchange_start(name, xs, after=None, gather=False):
    n = len(xs)
    copies = n * (N_DEV - 1)
    extra = [] if after is None else [after]

    def body(*refs):
        x_refs, land_refs = refs[:n], refs[n:2 * n]
        sems = refs[2 * n + len(extra):2 * n + len(extra) + 2 * copies]
        token = refs[-1]
        for i, k, peer in _exchange_copies(n):
            _exchange_copy(x_refs, land_refs, sems, i, k, peer, gather).start()
        token[...] = jnp.zeros_like(token)

    lands = [lax.empty((N_DEV,) + x.shape if gather else (N_DEV - 1,) + x.shape[1:], x.dtype) for x in xs]
    out = pl.pallas_call(
        body, name=name,
        out_shape=(*[pltpu.SemaphoreType.DMA(())] * (2 * copies), *[pltpu.HBM(x.shape, x.dtype) for x in xs],
                   *[pltpu.HBM(l.shape, l.dtype) for l in lands], jax.ShapeDtypeStruct((8, 128), F32)),
        in_specs=[HBM_SPEC] * (2 * n) + [ANY] * len(extra),
        out_specs=(*[SEM_SPEC] * (2 * copies), *[HBM_SPEC] * (2 * n), pl.BlockSpec(memory_space=pltpu.VMEM)),
        input_output_aliases={i: 2 * copies + i for i in range(2 * n)},
        compiler_params=pltpu.CompilerParams(has_side_effects=pltpu.SideEffectType.DATAFLOW_SIDE_EFFECTING),
    )(*[pltpu.with_memory_space_constraint(a, pltpu.HBM) for a in list(xs) + lands], *extra)
    sems, rest = list(out[:2 * copies]), out[2 * copies:]
    return sems, list(rest[:n]), list(rest[n:2 * n]), rest[-1]


def _exchange_copies(n):
    x, y, c = _place()
    for k in range(1, N_DEV):
        peer = ((1 - x) if k & 4 else x, (1 - y) if k & 2 else y, (1 - c) if k & 1 else c)
        for i in range(n):
            yield i, k - 1, peer


def _exchange_copy(x_refs, land_refs, sems, i, k, peer, gather, arriving=False):
    copies = len(sems) // 2
    which = i * (N_DEV - 1) + k
    src = x_refs[i] if gather else x_refs[i].at[_index(peer)]
    dst = land_refs[i].at[_index(peer if arriving else _place())] if gather else land_refs[i].at[k]
    return pltpu.make_async_remote_copy(src_ref=src, dst_ref=dst, send_sem=sems[which], recv_sem=sems[copies + which],
                                        device_id=peer, device_id_type=MESH_ID)


def _exchange_wait(name, sems, xs, lands, after, gather=False):
    n = len(xs)

    def body(*refs):
        x_refs, land_refs = refs[:n], refs[n:2 * n]
        sem_refs = refs[2 * n:2 * n + len(sems)]
        for i, k, peer in _exchange_copies(n):
            _exchange_copy(x_refs, land_refs, sem_refs, i, k, peer, gather).wait_send()
            _exchange_copy(x_refs, land_refs, sem_refs, i, k, peer, gather, arriving=True).wait_recv()

    out = pl.pallas_call(
        body, name=name,
        out_shape=(*[pltpu.HBM(x.shape, x.dtype) for x in xs], *[pltpu.HBM(l.shape, l.dtype) for l in lands]),
        in_specs=[HBM_SPEC] * (2 * n) + [SEM_SPEC] * len(sems) + [ANY], out_specs=tuple([HBM_SPEC] * (2 * n)),
        input_output_aliases={i: i for i in range(2 * n)},
        compiler_params=pltpu.CompilerParams(has_side_effects=pltpu.SideEffectType.DATAFLOW_SIDE_EFFECTING),
    )(*xs, *lands, *sems, after)
    return list(out[:n]), list(out[n:])


def _pack(arrays):
    flat = jnp.concatenate([a.reshape(-1) for a in arrays])
    pad = (-flat.shape[0]) % 1024
    return jnp.pad(flat, (0, pad)).reshape(-1, 128)


def _unpack(packed, shapes):
    flat = packed.reshape(-1)
    out, pos = [], 0
    for s in shapes:
        size = 1
        for dim in s:
            size *= dim
        out.append(flat[pos:pos + size].reshape(s))
        pos += size
    return out


def _cols_from_slabs(stack):
    return jnp.transpose(stack, (1, 0, 2)).reshape(stack.shape[1], -1)


def kernel(x, meta_tokens, ffn1_norm, ffn1_w_gu, ffn1_w_down, mix_norm, w_in, a_conv_w, a_log_rate, a_dt_bias, a_out_norm, b_shift_mu, b_w0, b_w_up, b_a0, b_a_up, b_g_up, b_k_k, b_k_a, b_r_k, b_ln_gain, b_ln_bias, w_out, ffn2_norm, ffn2_w_gu, ffn2_w_down, final_norm, loss_target, m_meta_tokens, m_ffn1_norm, m_ffn1_w_gu, m_ffn1_w_down, m_mix_norm, m_w_in, m_a_conv_w, m_a_log_rate, m_a_dt_bias, m_a_out_norm, m_b_shift_mu, m_b_w0, m_b_w_up, m_b_a0, m_b_a_up, m_b_g_up, m_b_k_k, m_b_k_a, m_b_r_k, m_b_ln_gain, m_b_ln_bias, m_w_out, m_ffn2_norm, m_ffn2_w_gu, m_ffn2_w_down, m_final_norm, v_meta_tokens, v_ffn1_norm, v_ffn1_w_gu, v_ffn1_w_down, v_mix_norm, v_w_in, v_a_conv_w, v_a_log_rate, v_a_dt_bias, v_a_out_norm, v_b_shift_mu, v_b_w0, v_b_w_up, v_b_a0, v_b_a_up, v_b_g_up, v_b_k_k, v_b_k_a, v_b_r_k, v_b_ln_gain, v_b_ln_bias, v_w_out, v_ffn2_norm, v_ffn2_w_gu, v_ffn2_w_down, v_final_norm):
    names = ['meta_tokens', 'ffn1_norm', 'ffn1_w_gu', 'ffn1_w_down', 'mix_norm', 'w_in', 'a_conv_w', 'a_log_rate',
             'a_dt_bias', 'a_out_norm', 'b_shift_mu', 'b_w0', 'b_w_up', 'b_a0', 'b_a_up', 'b_g_up', 'b_k_k', 'b_k_a',
             'b_r_k', 'b_ln_gain', 'b_ln_bias', 'w_out', 'ffn2_norm', 'ffn2_w_gu', 'ffn2_w_down', 'final_norm']
    env = dict(locals())
    wts = {k: env[k] for k in names}
    mom_m = {k: env['m_' + k] for k in names}
    mom_v = {k: env['v_' + k] for k in names}
    big = ['ffn1_w_gu', 'ffn1_w_down', 'w_in', 'w_out', 'ffn2_w_gu', 'ffn2_w_down']
    col_sharded = {'ffn1_w_gu', 'w_in', 'ffn2_w_gu'}
    shard_of = lambda tree, k: tree[k][0].T if k in col_sharded else tree[k][0]
    small_sharded = ['meta_tokens', 'a_conv_w', 'b_w_up', 'b_a_up', 'b_g_up']
    replicated = [k for k in names if k not in big and k not in small_sharded]

    seq, d = x.shape[1], x.shape[2]
    rows = PAD + N_META + seq
    heads_a = d // HEAD_A
    tb_mm = _tb(rows, 416)
    tb_vjp = _tb(rows, 208)
    tb_dw = _tb(rows, 2080)
    tb_ffn = _tb(rows, 832)
    me = _index(_place())

    local_bf = {k: shard_of(wts, k).astype(BF16) for k in big}
    gu1, down1, meta = _all_gather("gather_ffn1", [local_bf['ffn1_w_gu'], local_bf['ffn1_w_down'], wts['meta_tokens']])
    small_rest = small_sharded[1:]
    late_keys = ['w_out', 'ffn2_w_gu', 'ffn2_w_down']
    gather_mid = _exchange_start("gather_start_mid", [local_bf['w_in'], _pack([wts[k][0] for k in small_rest])],
                                 after=gu1, gather=True)
    gather_late = _exchange_start("gather_start_late", [local_bf[k] for k in late_keys], after=gather_mid[-1],
                                  gather=True)

    def finish_gather(tag, started, after):
        sems, mine, lands, _ = started
        mine, lands = _exchange_wait("gather_wait_" + tag, sems, mine, lands, after, gather=True)
        return [lax.dynamic_update_index_in_dim(land, own[None], me, 0) for land, own in zip(lands, mine)]

    full = {'ffn1_w_gu': gu1, 'ffn1_w_down': down1.reshape(-1, d), 'meta_tokens': _cols_from_slabs(meta)}
    for k in replicated:
        full[k] = wts[k].reshape(1, -1)

    h0 = jnp.concatenate([jnp.zeros((PAD, d), F32) + gather_late[-1][:1, :1], full['meta_tokens'], x[0]], axis=0)
    h1 = _ffn_fwd("ffn1_fwd", h0, full['ffn1_norm'], full['ffn1_w_gu'], full['ffn1_w_down'], tb_ffn)
    (u,) = _tok_fwd("mix_norm_fwd", _norm_fn, [h1], [full['mix_norm']], [(d, BF16)], tb_mm)

    win_stack, small_stack = finish_gather("mid", gather_mid, u)
    full['w_in'] = win_stack.reshape(-1, d)
    small_flat, pos = small_stack.reshape(N_DEV, -1), 0
    for k in small_rest:
        shape = wts[k][0].shape
        full[k] = _cols_from_slabs(small_flat[:, pos:pos + shape[0] * shape[1]].reshape((N_DEV,) + shape))
        pos += shape[0] * shape[1]

    win = full['w_in']
    n_b = 3 * d + LORA_W + LORA_A + LORA_G
    off_beta, off_b = 4 * d, 4 * d + 2 * heads_a
    off_ga = off_b + n_b
    b_width = 3 * d + LORA_PAD
    zrows = lambda r: jnp.zeros((r, d), BF16)
    w_qkv = win[:3 * d]
    w_zg = jnp.concatenate([win[3 * d:4 * d], win[off_ga:off_ga + 2 * d]], axis=0)
    w_b = jnp.concatenate([win[off_b:off_b + n_b], zrows(b_width - n_b)], axis=0)
    w_bg = jnp.concatenate([win[off_beta:off_beta + 2 * heads_a], zrows(128 - 2 * heads_a)], axis=0)

    def lanes(vec, start, width):
        return jnp.pad(vec.reshape(1, -1), ((0, 0), (start, width - start - vec.size)))

    log_rate = lanes(wts['a_log_rate'], heads_a, 128)
    dt_bias = lanes(wts['a_dt_bias'], heads_a, 128)
    mu = lanes(wts['b_shift_mu'], 0, b_width)
    w_up = jnp.pad(full['b_w_up'], ((0, 128 - LORA_W), (0, 0)))
    a_up = jnp.pad(full['b_a_up'], ((LORA_W, 0), (0, 0)))
    g_up = jnp.pad(full['b_g_up'], ((0, 256 - LORA_G), (0, 0)))
    b_pars = [full['b_w0'], w_up, full['b_a0'], a_up, g_up, full['b_k_k'], full['b_k_a']]
    post_pars = [full['a_out_norm'], full['b_r_k'], full['b_ln_gain'], full['b_ln_bias']]
    bg_fn = _make_bg_fn(heads_a)

    z_qkv = _mm("in_qkv", u, w_qkv, trans_b=True, tb=tb_ffn, tn=_col_tile(3 * d, 1536))
    z_zg = _mm("in_zg", u, w_zg, trans_b=True, tb=tb_ffn, tn=_col_tile(3 * d, 1536))
    z_b = _mm("in_b", u, w_b, trans_b=True, tb=tb_ffn, tn=_col_tile(b_width, 1536))
    z_bg = _mm("in_bg", u, w_bg, trans_b=True, tb=tb_ffn, tn=128)
    qkv = _a_pre_fwd(z_qkv, full['a_conv_w'])
    (bg,) = _tok_fwd("bg_fwd", bg_fn, [z_bg], [log_rate, dt_bias], [(128, F32)], tb_mm)
    o_dn, dn_hist, dn_tinv = _dn_fwd(qkv, bg)
    zf = _shift_fwd(z_b, mu)
    rr, ww, kk2, vv, av, bv, gate = _tok_fwd("b_pre_fwd", _b_pre_fn, [zf], b_pars, [(d, F32)] * 7, tb_vjp)
    per_head = lambda t: t.reshape(rows, d // HEAD_B, HEAD_B)
    y_heads, b_hist, b_last = _rwkv_fwd(rr, ww, kk2, per_head(vv), av, bv)
    y_b = y_heads.reshape(rows, d)
    w_out_stack, full['ffn2_w_gu'], down2 = finish_gather("late", gather_late, y_heads)
    full['w_out'], full['ffn2_w_down'] = w_out_stack.reshape(-1, d), down2.reshape(-1, d)
    post_toks = [o_dn, z_zg, y_b, rr, kk2, vv, gate]
    (merged,) = _tok_fwd("post_fwd", _post_fn, post_toks, post_pars, [(d, BF16)], tb_vjp)
    h2 = _mm("out_proj", merged, full['w_out'], add=h1, tb=tb_ffn, tn=d)
    h3 = _ffn_fwd("ffn2_fwd", h2, full['ffn2_norm'], full['ffn2_w_gu'], full['ffn2_w_down'], tb_ffn)

    target = jnp.pad(loss_target[0], ((CHUNK, 0), (0, 0)))
    dh3, g_final, loss_part = _loss(h3, target, full['final_norm'].reshape(1, d), tb_vjp)

    def ffn_backward(tag, h, dout, key_norm, key_gu, key_down, once_down=None):
        dh, dh_bf, dgain, xn, act, dgate, dup, dhalf = _ffn_bwd(tag + "_bwd", h, full[key_norm], dout, full[key_gu],
                                                                full[key_down], tb_mm)
        fc = dgate.shape[2]
        d_down = _mm_tn_from_slabs(tag + "_dw_down", act, dhalf, tk=tb_dw).reshape(N_DEV, -1, d)
        token = once_down(dh, dgain, d_down) if once_down else None
        d_gu = jnp.concatenate(
            [_mm_tn_from_slabs(tag + "_dw_gate", dgate, xn, tk=tb_dw, after=token).reshape(-1, fc, d),
             _mm_tn_from_slabs(tag + "_dw_up", dup, xn, tk=tb_dw).reshape(-1, fc, d)], axis=0)
        return dh, dh_bf, dgain, d_gu, d_down

    dh2, dh2_bf, g_ffn2_norm, g_ffn2_gu, g_ffn2_down = ffn_backward("ffn2", h2, dh3, 'ffn2_norm', 'ffn2_w_gu',
                                                                    'ffn2_w_down')
    g_w_out = _mm_tn("dw_out", merged, dh2_bf, tm=d, tn=d, tk=tb_dw).reshape(N_DEV, -1, d)

    def start_exchange(tag, keys, slabs, after=None):
        sems, kept, lands, token = _exchange_start("exchange_start_" + tag, slabs, after)
        return (tag, keys, sems, kept, lands), token

    ex_ffn2, token_ffn2 = start_exchange("ffn2", ['ffn2_w_gu', 'ffn2_w_down', 'w_out'], [g_ffn2_gu, g_ffn2_down, g_w_out])
    dmerged = _mm("d_merged", dh2_bf, full['w_out'], trans_b=True, after=token_ffn2, tb=tb_ffn, tn=d)
    post_grads = _tok_bwd("post_bwd", _post_fn, post_toks, post_pars, [[dmerged]], list(range(7)), tb_vjp,
                          [F32, BF16] + [F32] * 5)
    do_dn, dz_zg, dy_b, dr1, dk1, dv1, dgate = post_grads[:7]
    g_out_norm, g_r_k, g_ln_g, g_ln_b = post_grads[7:]
    dr2, dw2, dk2, dv_heads, da2, db2 = _rwkv_bwd(rr, ww, kk2, per_head(vv), av, bv, b_hist, b_last, per_head(dy_b))
    dv2 = dv_heads.reshape(rows, d)
    b_grads = _tok_bwd("b_pre_bwd", _b_pre_fn, [zf], b_pars,
                       [[dr1, dr2], [dw2], [dk1, dk2], [dv1, dv2], [da2], [db2], [dgate]], [0], tb_vjp)
    dzf = b_grads[0]
    g_w0, g_w_up, g_a0, g_a_up, g_g_up, g_k_k, g_k_a = b_grads[1:]
    dz_b, g_mu = _shift_bwd(z_b, mu, dzf)
    dqkv, dbg = _dn_bwd(qkv, bg, dn_hist, dn_tinv, do_dn)
    dz_qkv, g_conv = _a_pre_bwd(z_qkv, full['a_conv_w'], dqkv)
    dz_bg, g_log_rate, g_dt_bias = _tok_bwd("bg_bwd", bg_fn, [z_bg], [log_rate, dt_bias], [[dbg]], [0], tb_mm, [BF16])

    small_early = {
        'a_conv_w': g_conv, 'a_log_rate': g_log_rate[:, heads_a:2 * heads_a],
        'a_dt_bias': g_dt_bias[:, heads_a:2 * heads_a], 'a_out_norm': g_out_norm, 'b_shift_mu': g_mu[:, :n_b],
        'b_w0': g_w0, 'b_w_up': g_w_up[:LORA_W], 'b_a0': g_a0, 'b_a_up': g_a_up[LORA_W:], 'b_g_up': g_g_up[:LORA_G],
        'b_k_k': g_k_k, 'b_k_a': g_k_a, 'b_r_k': g_r_k, 'b_ln_gain': g_ln_g, 'b_ln_bias': g_ln_b,
        'ffn2_norm': g_ffn2_norm, 'final_norm': g_final,
    }
    gather_small = _exchange_start("gather_start_small", [_pack(list(small_early.values()))], gather=True)

    du = None
    g_w_in_parts = []
    for tag, dz, wpiece in (("qkv", dz_qkv, w_qkv), ("zg", dz_zg, w_zg), ("b", dz_b, w_b), ("bg", dz_bg, w_bg)):
        du = _mm("du_" + tag, dz, wpiece, add=du, after=gather_small[-1] if du is None else None, tb=tb_ffn, tn=d)
        g_w_in_parts.append(_mm_tn("dw_in_" + tag, dz, u, tm=_col_tile(dz.shape[1], 1536), tn=d, tk=tb_dw))
    gp_qkv, gp_zg, gp_b, gp_bg = g_w_in_parts
    g_w_in = jnp.concatenate([gp_qkv, gp_zg[:d], gp_bg[:2 * heads_a], gp_b[:n_b], gp_zg[d:]],
                             axis=0).reshape(N_DEV, -1, d)
    ex_w_in, token_w_in = start_exchange("w_in", ['w_in'], [g_w_in])
    dh1, g_mix_norm = _tok_bwd("mix_norm_bwd", _norm_res_fn, [h1], [full['mix_norm']], [[du], [dh2]], [0], tb_vjp,
                               after=token_w_in)
    tail = {}

    def once_ffn1_down(dh0, g_ffn1_norm, g_ffn1_down):
        tail['late'] = {'meta_tokens': dh0[PAD:CHUNK], 'ffn1_norm': g_ffn1_norm, 'mix_norm': g_mix_norm}
        tail['packed'] = _pack(list(tail['late'].values()) + [loss_part[:, :1]])
        (tail['parts'],) = _all_gather("gather_small_grads", [tail['packed']])
        tail['ex_down'], token = start_exchange("ffn1_down", ['ffn1_w_down'], [g_ffn1_down], after=tail['parts'])
        return token

    dh0, _, _, g_ffn1_gu, _ = ffn_backward("ffn1", h0, dh1, 'ffn1_norm', 'ffn1_w_gu', 'ffn1_w_down', once_ffn1_down)
    ex_ffn1_gu, _ = start_exchange("ffn1_gu", ['ffn1_w_gu'], [g_ffn1_gu])
    small_late, packed_late, late_parts = tail['late'], tail['packed'], tail['parts']
    (early_parts,) = finish_gather("small", gather_small, late_parts)
    pieces = (_unpack(_sum_slabs("sum_small_early", early_parts, early_parts.shape[1]),
                      [g.shape for g in small_early.values()])
              + _unpack(_sum_slabs("sum_small_grads", late_parts, packed_late.shape[0]),
                        [g.shape for g in small_late.values()] + [(1, 1)]))
    small_names = list(small_early) + list(small_late)
    small_grad = dict(zip(small_names, pieces[:-1]))
    loss = pieces[-1].reshape(())

    grads, deltas, new_m, new_v = {}, {}, {}, {}
    local_small = {}
    for k in small_names:
        g = small_grad[k]
        if k in small_sharded:
            width = wts[k].shape[-1]
            g = lax.dynamic_slice_in_dim(g, me * width, width, axis=1)
        local_small[k] = g.reshape(wts[k].shape)
    pk = lambda tree: _pack([tree[k] for k in small_names])
    dl_s, m_s, v_s = _adamw_small(pk(wts), pk(local_small), pk(mom_m), pk(mom_v))
    shapes = [wts[k].shape for k in small_names]
    for k, dl, m2, v2 in zip(small_names, _unpack(dl_s, shapes), _unpack(m_s, shapes), _unpack(v_s, shapes)):
        grads[k], deltas[k], new_m[k], new_v[k] = local_small[k], dl, m2, v2

    done = dl_s
    for tag, keys, sems, kept, lands in (ex_ffn2, ex_w_in, tail['ex_down'], ex_ffn1_gu):
        kept, lands = _exchange_wait("exchange_wait_" + tag, sems, kept, lands, done)
        for k, slabs, landed in zip(keys, kept, lands):
            own = lax.dynamic_index_in_dim(slabs, me, axis=0, keepdims=False)
            res = _adamw("adamw_" + k, own, landed, shard_of(wts, k), shard_of(mom_m, k), shard_of(mom_v, k))
            done = res[1]
            res = [(t.T if k in col_sharded else t)[None] for t in res]
            grads[k], deltas[k], new_m[k], new_v[k] = res

    grad_x = dh0[CHUNK:][None]
    return (loss, grad_x, *[grads[k] for k in names], *[deltas[k] for k in names],
            *[new_m[k] for k in names], *[new_v[k] for k in names])
```

```python
import functools

import jax
import jax.numpy as jnp
from jax import lax
from jax.experimental import pallas as pl
from jax.experimental.pallas import tpu as pltpu

F32 = jnp.float32
BF16 = jnp.bfloat16
N_DEV = 8
N_META = 16
CHUNK = 64
PAD = CHUNK - N_META
HEAD_A = 128
HEAD_B = 64
LORA_W, LORA_A, LORA_G = 64, 64, 160
LORA_PAD = 384
EPS = 1e-6
GN_EPS = HEAD_B * 1e-5
ADAM_LR, ADAM_B1, ADAM_B2, ADAM_EPS, ADAM_WD, ADAM_STEP = 0.001, 0.9, 0.999, 1e-08, 0.01, 10
SCAN_STEPS = 32
MXU_WIDTH = 256
VMEM_LIMIT = 56 * 1024 * 1024
DN_PRECISION = lax.Precision.HIGH
MESH_ID = pl.DeviceIdType.MESH
ANY = pl.BlockSpec(memory_space=pl.ANY)
HBM_SPEC = pl.BlockSpec(memory_space=pltpu.HBM)
SEM_SPEC = pl.BlockSpec(memory_space=pltpu.SEMAPHORE)


def _cp(*sem):
    return pltpu.CompilerParams(dimension_semantics=sem, vmem_limit_bytes=VMEM_LIMIT)


def _tb(t, target):
    best = 16
    for d in range(16, target + 1, 16):
        if t % d == 0:
            best = d
    return best


def _sigmoid(x):
    return 1.0 / (1.0 + jnp.exp(-x))


def _silu(x):
    return x * _sigmoid(x)


def _softplus(x):
    return jnp.maximum(x, 0.0) + jnp.log(1.0 + jnp.exp(-jnp.abs(x)))


def _dot_nt(a, b, precision=None):
    return lax.dot_general(a, b, (((1,), (1,)), ((), ())), preferred_element_type=F32, precision=precision)


def _dot_tn(a, b, precision=None):
    return lax.dot_general(a, b, (((0,), (0,)), ((), ())), preferred_element_type=F32, precision=precision)


def _dot(a, b, precision=None):
    return jnp.dot(a, b, preferred_element_type=F32, precision=precision)


def _block_diag_ones():
    i = lax.broadcasted_iota(jnp.int32, (MXU_WIDTH, MXU_WIDTH), 0) // HEAD_B
    j = lax.broadcasted_iota(jnp.int32, (MXU_WIDTH, MXU_WIDTH), 1) // HEAD_B
    return (i == j).astype(BF16)


def _hi_lo(x):
    hi = x.astype(BF16)
    return hi, (x - hi.astype(F32)).astype(BF16)


def _segsum_many(xs, bd):
    groups = [x if isinstance(x, tuple) else _hi_lo(x) for x in xs]
    rows = groups[0][0].shape[0]
    stacked = jnp.concatenate([p for grp in groups for p in grp], axis=0)
    out = jnp.concatenate([_dot(stacked[:, s:s + MXU_WIDTH], bd) for s in range(0, stacked.shape[1], MXU_WIDTH)], axis=1)
    res, pos = [], 0
    for grp in groups:
        acc = out[pos:pos + rows]
        for j in range(1, len(grp)):
            acc = acc + out[pos + j * rows:pos + (j + 1) * rows]
        res.append(acc)
        pos += len(grp) * rows
    return res


def _segsum_impl(x):
    return _segsum_many([x], _block_diag_ones())[0]


@jax.custom_vjp
def _segsum64(x):
    return _segsum_impl(x)


_segsum64.defvjp(lambda x: (_segsum_impl(x), None), lambda _, ct: (_segsum_impl(ct),))


def _tok(t):
    return t if isinstance(t, tuple) else (t, t.shape[1], 0)


def _tok_spec(tb, width, colblk):
    return pl.BlockSpec((tb, width), lambda i: (i, colblk))


def _par_spec(p):
    return pl.BlockSpec(p.shape, lambda i: (0, 0))


def _tok_fwd(name, fn, toks, pars, outs, tb):
    toks = [_tok(t) for t in toks]
    rows = toks[0][0].shape[0]
    n_in = len(toks) + len(pars)

    def body(*refs):
        row0 = pl.program_id(0) * tb
        res = fn(row0, *[r[...] for r in refs[:n_in]])
        for r, o in zip(refs[n_in:], res):
            r[...] = o.astype(r.dtype)

    return pl.pallas_call(
        body, name=name, grid=(rows // tb,),
        in_specs=[_tok_spec(tb, w, c) for _, w, c in toks] + [_par_spec(p) for p in pars],
        out_specs=[_tok_spec(tb, w, 0) for w, _ in outs],
        out_shape=[jax.ShapeDtypeStruct((rows, w), dt) for w, dt in outs],
        compiler_params=_cp("parallel"),
    )(*[a for a, _, _ in toks], *pars)


def _tok_bwd(name, fn, toks, pars, cts, want, tb, want_dtypes=None, after=None):
    toks = [_tok(t) for t in toks]
    want_dtypes = want_dtypes or [F32] * len(want)
    cts = [[_tok(c) for c in group] for group in cts]
    flat_cts = [c for group in cts for c in group]
    rows = toks[0][0].shape[0]
    n_tok, n_par, n_ct = len(toks), len(pars), len(flat_cts)
    extra = [] if after is None else [after]

    def body(*refs):
        i = pl.program_id(0)
        row0 = i * tb
        prim = [r[...].astype(F32) for r in refs[:n_tok + n_par]]
        ct_refs = list(refs[n_tok + n_par:n_tok + n_par + n_ct])
        out_refs = refs[n_tok + n_par + n_ct + len(extra):]
        res, vjp = jax.vjp(lambda *a: fn(row0, *a), *prim)
        ct = []
        for group, o in zip(cts, res):
            acc = None
            for _ in group:
                v = ct_refs.pop(0)[...].astype(F32)
                acc = v if acc is None else acc + v
            ct.append(acc.astype(o.dtype))
        grads = vjp(tuple(ct))
        for r, k in zip(out_refs[:len(want)], want):
            r[...] = grads[k].astype(r.dtype)

        @pl.when(i == 0)
        def _():
            for r in out_refs[len(want):]:
                r[...] = jnp.zeros_like(r)

        for r, g in zip(out_refs[len(want):], grads[n_tok:]):
            r[...] += g

    return pl.pallas_call(
        body, name=name, grid=(rows // tb,),
        in_specs=[_tok_spec(tb, w, c) for _, w, c in toks] + [_par_spec(p) for p in pars]
        + [_tok_spec(tb, w, c) for _, w, c in flat_cts] + [ANY] * len(extra),
        out_specs=[_tok_spec(tb, toks[k][1], 0) for k in want] + [_par_spec(p) for p in pars],
        out_shape=[jax.ShapeDtypeStruct((rows, toks[k][1]), dt) for k, dt in zip(want, want_dtypes)]
        + [jax.ShapeDtypeStruct(p.shape, F32) for p in pars],
        compiler_params=_cp("arbitrary"),
    )(*[a for a, _, _ in toks], *pars, *[a for a, _, _ in flat_cts], *extra)


def _mm(name, a, b, *, trans_b=False, add=None, after=None, tb, tn):
    rows, k = a.shape
    n = b.shape[0] if trans_b else b.shape[1]

    def body(*refs):
        a_ref, b_ref = refs[:2]
        o_ref = refs[-1]
        acc = _dot_nt(a_ref[...], b_ref[...]) if trans_b else _dot(a_ref[...], b_ref[...])
        if add is not None:
            acc = acc + refs[2][...]
        o_ref[...] = acc

    in_specs = [pl.BlockSpec((tb, k), lambda i, j: (i, 0)),
                pl.BlockSpec((tn, k), lambda i, j: (j, 0)) if trans_b else pl.BlockSpec((k, tn), lambda i, j: (0, j))]
    args = [a, b]
    if add is not None:
        in_specs.append(pl.BlockSpec((tb, tn), lambda i, j: (i, j)))
        args.append(add)
    if after is not None:
        in_specs.append(ANY)
        args.append(after)
    return pl.pallas_call(
        body, name=name, grid=(rows // tb, n // tn), in_specs=in_specs,
        out_specs=pl.BlockSpec((tb, tn), lambda i, j: (i, j)),
        out_shape=jax.ShapeDtypeStruct((rows, n), F32),
        compiler_params=_cp("parallel", "parallel"),
    )(*args)


def _mm_tn_call(name, grid, a, b, a_spec, b_spec, o_spec, acc_shape, out_shape, after=None):
    last = len(grid) - 1
    extra = [] if after is None else [after]

    def body(a_ref, b_ref, *rest):
        o_ref, acc_ref = rest[len(extra):]
        k = pl.program_id(last)

        @pl.when(k == 0)
        def _():
            acc_ref[...] = jnp.zeros_like(acc_ref)

        a_blk = a_ref[0] if len(a_ref.shape) == 3 else a_ref[...]
        b_blk = b_ref[0] if len(b_ref.shape) == 3 else b_ref[...]
        acc_ref[...] += _dot_tn(a_blk, b_blk)

        @pl.when(k == grid[last] - 1)
        def _():
            if len(o_ref.shape) == 3:
                o_ref[0] = acc_ref[...].astype(o_ref.dtype)
            else:
                o_ref[...] = acc_ref[...].astype(o_ref.dtype)

    return pl.pallas_call(
        body, name=name, grid=grid, in_specs=[a_spec, b_spec] + [ANY] * len(extra), out_specs=o_spec,
        out_shape=jax.ShapeDtypeStruct(out_shape, BF16), scratch_shapes=[pltpu.VMEM(acc_shape, F32)],
        compiler_params=_cp(*(["parallel"] * last + ["arbitrary"])),
    )(a, b, *extra)


def _mm_tn(name, a, b, *, tm, tn, tk):
    rows, m = a.shape
    n = b.shape[1]
    return _mm_tn_call(name, (m // tm, n // tn, rows // tk), a, b,
                       pl.BlockSpec((tk, tm), lambda i, j, k: (k, i)), pl.BlockSpec((tk, tn), lambda i, j, k: (k, j)),
                       pl.BlockSpec((tm, tn), lambda i, j, k: (i, j)), (tm, tn), (m, n))


def _mm_tn_from_slabs(name, a3, b, *, tk, after=None):
    s, rows, c = a3.shape
    n = b.shape[1]
    return _mm_tn_call(name, (s, rows // tk), a3, b,
                       pl.BlockSpec((1, tk, c), lambda i, k: (i, k, 0)), pl.BlockSpec((tk, n), lambda i, k: (k, 0)),
                       pl.BlockSpec((c, n), lambda i, k: (i, 0)), (c, n), (s * c, n), after)


def _col_tile(n, target):
    if n <= target:
        return n
    best = 128
    for d in range(128, target + 1, 128):
        if n % d == 0:
            best = d
    return best


def _rms(x, gain):
    return x * lax.rsqrt(jnp.mean(x * x, axis=-1, keepdims=True) + EPS) * gain


def _ffn_specs(d, fc, nj):
    return [pl.BlockSpec((1, fc, d), lambda i, j: (j, 0, 0)), pl.BlockSpec((1, fc, d), lambda i, j: (nj + j, 0, 0)),
            pl.BlockSpec((fc, d), lambda i, j: (j, 0))]


def _ffn_fwd(name, h, gain, wgu, wd, tb):
    rows, d = h.shape
    nj = wgu.shape[0] // 2
    fc = wgu.shape[1]

    def body(h_ref, g_ref, wg_ref, wu_ref, wd_ref, o_ref, xn_s, acc_s):
        j = pl.program_id(1)

        @pl.when(j == 0)
        def _():
            xn_s[...] = _rms(h_ref[...], g_ref[...]).astype(BF16)
            acc_s[...] = jnp.zeros_like(acc_s)

        wg, wu, wdn = wg_ref[0], wu_ref[0], wd_ref[...]
        for half in range(2):
            rs = pl.ds(half * (tb // 2), tb // 2)
            xn = xn_s[rs, :]
            gate = _dot_nt(xn, wg)
            up = _dot_nt(xn, wu)
            acc_s[rs, :] += _dot((_silu(gate) * up).astype(BF16), wdn)

        @pl.when(j == nj - 1)
        def _():
            o_ref[...] = h_ref[...] + 0.5 * acc_s[...]

    return pl.pallas_call(
        body, name=name, grid=(rows // tb, nj),
        in_specs=[pl.BlockSpec((tb, d), lambda i, j: (i, 0)), pl.BlockSpec((1, d), lambda i, j: (0, 0))]
        + _ffn_specs(d, fc, nj),
        out_specs=pl.BlockSpec((tb, d), lambda i, j: (i, 0)),
        out_shape=jax.ShapeDtypeStruct((rows, d), F32),
        scratch_shapes=[pltpu.VMEM((tb, d), BF16), pltpu.VMEM((tb, d), F32)],
        compiler_params=_cp("parallel", "arbitrary"),
    )(h, gain, wgu, wgu, wd)


def _ffn_bwd(name, h, gain, dout, wgu, wd, tb):
    rows, d = h.shape
    nj = wgu.shape[0] // 2
    fc = wgu.shape[1]

    def body(h_ref, g_ref, do_ref, wg_ref, wu_ref, wd_ref,
             dh_ref, dhb_ref, dg_ref, xn_ref, act_ref, dgate_ref, dup_ref, dhalf_ref, dxn_s):
        i, j = pl.program_id(0), pl.program_id(1)

        @pl.when(j == 0)
        def _():
            xn_ref[...] = _rms(h_ref[...], g_ref[...]).astype(BF16)
            dhalf_ref[...] = (0.5 * do_ref[...]).astype(BF16)
            dxn_s[...] = jnp.zeros_like(dxn_s)

        wg, wu, wdn = wg_ref[0], wu_ref[0], wd_ref[...]
        for half in range(2):
            rs = pl.ds(half * (tb // 2), tb // 2)
            xn = xn_ref[rs, :]
            gate = _dot_nt(xn, wg)
            up = _dot_nt(xn, wu)
            sg = _sigmoid(gate)
            dact = _dot_nt(dhalf_ref[rs, :], wdn)
            act_ref[0, rs, :] = (gate * sg * up).astype(BF16)
            dgate = (dact * up * (sg * (1.0 + gate * (1.0 - sg)))).astype(BF16)
            dup = (dact * gate * sg).astype(BF16)
            dgate_ref[0, rs, :] = dgate
            dup_ref[0, rs, :] = dup
            dxn_s[rs, :] += _dot(dgate, wg) + _dot(dup, wu)

        @pl.when((i == 0) & (j == 0))
        def _():
            dg_ref[...] = jnp.zeros_like(dg_ref)

        @pl.when(j == nj - 1)
        def _():
            x = h_ref[...]
            r = lax.rsqrt(jnp.mean(x * x, axis=-1, keepdims=True) + EPS)
            dxn = dxn_s[...]
            dyg = dxn * g_ref[...]
            dh = do_ref[...] + r * dyg - x * (r * r * r) * jnp.mean(dyg * x, axis=-1, keepdims=True)
            dh_ref[...] = dh
            dhb_ref[...] = dh.astype(BF16)
            dg_ref[...] += jnp.sum(dxn * x * r, axis=0, keepdims=True)

    row_d = pl.BlockSpec((tb, d), lambda i, j: (i, 0))
    slab = pl.BlockSpec((1, tb, fc), lambda i, j: (j, i, 0))
    hidden = jax.ShapeDtypeStruct((nj, rows, fc), BF16)
    return pl.pallas_call(
        body, name=name, grid=(rows // tb, nj),
        in_specs=[row_d, pl.BlockSpec((1, d), lambda i, j: (0, 0)), row_d] + _ffn_specs(d, fc, nj),
        out_specs=[row_d, row_d, pl.BlockSpec((1, d), lambda i, j: (0, 0)), row_d, slab, slab, slab, row_d],
        out_shape=[jax.ShapeDtypeStruct((rows, d), F32), jax.ShapeDtypeStruct((rows, d), BF16),
                   jax.ShapeDtypeStruct((1, d), F32), jax.ShapeDtypeStruct((rows, d), BF16),
                   hidden, hidden, hidden, jax.ShapeDtypeStruct((rows, d), BF16)],
        scratch_shapes=[pltpu.VMEM((tb, d), F32)],
        compiler_params=_cp("arbitrary", "arbitrary"),
    )(h, gain, dout, wgu, wgu, wd)


def _shift_rows(x, s):
    return pltpu.roll(x, s % x.shape[0], 0)


def _a_post(c, which):
    s = _silu(c)
    n = s * lax.rsqrt(jnp.sum(s * s, axis=-1, keepdims=True) + 1e-6)
    scale = jnp.where(which == 0, HEAD_A ** -0.5, 1.0)
    return jnp.where(which == 2, s, n * scale)


def _conv(x, w):
    return x * w[3:4] + _shift_rows(x, 1) * w[2:3] + _shift_rows(x, 2) * w[1:2] + _shift_rows(x, 3) * w[0:1]


def _a_pre_fwd(zqkv, conv_w):
    rows, width = zqkv.shape
    heads = width // (3 * HEAD_A)

    def body(x_ref, w_ref, o_ref):
        which = pl.program_id(0) // heads
        live = lax.broadcasted_iota(jnp.int32, (rows, HEAD_A), 0) >= PAD
        o_ref[...] = jnp.where(live, _a_post(_conv(x_ref[...], w_ref[...]), which), 0.0)

    return pl.pallas_call(
        body, name="a_pre_fwd", grid=(width // HEAD_A,),
        in_specs=[pl.BlockSpec((rows, HEAD_A), lambda c: (0, c)), pl.BlockSpec((4, HEAD_A), lambda c: (0, c))],
        out_specs=pl.BlockSpec((rows, HEAD_A), lambda c: (0, c)),
        out_shape=jax.ShapeDtypeStruct((rows, width), F32),
        compiler_params=_cp("parallel"),
    )(zqkv, conv_w)


def _a_pre_bwd(zqkv, conv_w, dqkv):
    rows, width = zqkv.shape
    heads = width // (3 * HEAD_A)

    def body(x_ref, w_ref, ct_ref, dx_ref, dw_ref):
        which = pl.program_id(0) // heads
        live = lax.broadcasted_iota(jnp.int32, (rows, HEAD_A), 0) >= PAD
        x, w = x_ref[...], w_ref[...]
        _, vjp = jax.vjp(lambda c: _a_post(c, which), _conv(x, w))
        (dc,) = vjp(jnp.where(live, ct_ref[...], 0.0))
        dc = jnp.where(live, dc, 0.0)
        dx_ref[...] = (dc * w[3:4] + _shift_rows(dc, -1) * w[2:3] + _shift_rows(dc, -2) * w[1:2]
                       + _shift_rows(dc, -3) * w[0:1]).astype(BF16)
        dw_ref[...] = jnp.concatenate(
            [jnp.sum(dc * (_shift_rows(x, 3 - j) if j < 3 else x), axis=0, keepdims=True) for j in range(4)], axis=0)

    col = pl.BlockSpec((rows, HEAD_A), lambda c: (0, c))
    wsp = pl.BlockSpec((4, HEAD_A), lambda c: (0, c))
    return pl.pallas_call(
        body, name="a_pre_bwd", grid=(width // HEAD_A,),
        in_specs=[col, wsp, col], out_specs=[col, wsp],
        out_shape=[jax.ShapeDtypeStruct((rows, width), BF16), jax.ShapeDtypeStruct((4, width), F32)],
        compiler_params=_cp("parallel"),
    )(zqkv, conv_w, dqkv)


SHIFT_TILE = 384


def _shift_fwd(zb, mu):
    rows, width = zb.shape

    def body(x_ref, mu_ref, o_ref):
        x = x_ref[...]
        first = lax.broadcasted_iota(jnp.int32, x.shape, 0) == 0
        prev = jnp.where(first, 0.0, _shift_rows(x, 1))
        o_ref[...] = x + (prev - x) * mu_ref[...]

    col = pl.BlockSpec((rows, SHIFT_TILE), lambda c: (0, c))
    return pl.pallas_call(
        body, name="shift_fwd", grid=(width // SHIFT_TILE,),
        in_specs=[col, pl.BlockSpec((1, SHIFT_TILE), lambda c: (0, c))], out_specs=col,
        out_shape=jax.ShapeDtypeStruct((rows, width), F32), compiler_params=_cp("parallel"),
    )(zb, mu)


def _shift_bwd(zb, mu, dzf):
    rows, width = zb.shape

    def body(x_ref, mu_ref, ct_ref, dx_ref, dmu_ref):
        x, ct, mu_v = x_ref[...], ct_ref[...], mu_ref[...]
        row = lax.broadcasted_iota(jnp.int32, x.shape, 0)
        prev = jnp.where(row == 0, 0.0, _shift_rows(x, 1))
        nxt = jnp.where(row == rows - 1, 0.0, _shift_rows(ct, -1))
        dx_ref[...] = (ct * (1.0 - mu_v) + nxt * mu_v).astype(BF16)
        dmu_ref[...] = jnp.sum(ct * (prev - x), axis=0, keepdims=True)

    col = pl.BlockSpec((rows, SHIFT_TILE), lambda c: (0, c))
    msp = pl.BlockSpec((1, SHIFT_TILE), lambda c: (0, c))
    return pl.pallas_call(
        body, name="shift_bwd", grid=(width // SHIFT_TILE,),
        in_specs=[col, msp, col], out_specs=[col, msp],
        out_shape=[jax.ShapeDtypeStruct((rows, width), BF16), jax.ShapeDtypeStruct((1, width), F32)],
        compiler_params=_cp("parallel"),
    )(zb, mu, dzf)


def _neumann_inverse(p):
    heads = range(len(p))
    eye = (lax.broadcasted_iota(jnp.int32, (CHUNK, CHUNK), 0)
           == lax.broadcasted_iota(jnp.int32, (CHUNK, CHUNK), 1)).astype(F32)
    tinv = [eye + p[h] for h in heads]
    for _ in range(5):
        p = [_dot(p[h], p[h], DN_PRECISION) for h in heads]
        tinv = [tinv[h] + _dot(tinv[h], p[h], DN_PRECISION) for h in heads]
    return tinv


@jax.custom_vjp
def _unit_lower_inverse(p):
    return _neumann_inverse(p)


def _unit_lower_inverse_fwd(p):
    tinv = _neumann_inverse(p)
    return tinv, tinv


def _unit_lower_inverse_bwd(tinv, ct):
    heads = range(len(tinv))
    left = [_dot_tn(tinv[h], ct[h], DN_PRECISION) for h in heads]
    return ([_dot_nt(left[h], tinv[h], DN_PRECISION) for h in heads],)


_unit_lower_inverse.defvjp(_unit_lower_inverse_fwd, _unit_lower_inverse_bwd)


@jax.custom_vjp
def _known_inverse(p, tinv):
    return tinv


_known_inverse.defvjp(lambda p, tinv: (tinv, tinv),
                      lambda tinv, ct: (_unit_lower_inverse_bwd(tinv, ct)[0], [jnp.zeros_like(t) for t in tinv]))


def _dn_chunk(q, k, v, beta, g, state, saved_tinv=None):
    heads = range(len(q))
    ri = lax.broadcasted_iota(jnp.int32, (CHUNK, CHUNK), 0)
    ci = lax.broadcasted_iota(jnp.int32, (CHUNK, CHUNK), 1)
    eye = (ri == ci).astype(F32)
    incl = ri >= ci
    last = lax.broadcasted_iota(jnp.int32, (CHUNK, 1), 0) == CHUNK - 1
    g_row = [jnp.sum(g[h] * eye, axis=0, keepdims=True) for h in heads]
    gc = [jnp.sum(jnp.where(incl, g_row[h], 0.0), axis=1, keepdims=True) for h in heads]
    gc_row = [jnp.sum(gc[h] * eye, axis=0, keepdims=True) for h in heads]
    decay = [jnp.where(incl, jnp.exp(jnp.where(incl, gc[h] - gc_row[h], 0.0)), 0.0) for h in heads]
    kb = [k[h] * beta[h] for h in heads]
    vb = [v[h] * beta[h] for h in heads]
    p = [-jnp.where(ri > ci, _dot_nt(kb[h], k[h]) * decay[h], 0.0) for h in heads]
    tinv = _unit_lower_inverse(p) if saved_tinv is None else _known_inverse(p, saved_tinv)
    eg = [jnp.exp(gc[h]) for h in heads]
    u = [_dot(tinv[h], vb[h]) for h in heads]
    wk = [_dot(tinv[h], kb[h] * eg[h]) for h in heads]
    attn = [_dot_nt(q[h], k[h]) * decay[h] for h in heads]
    g_last = [jnp.sum(jnp.where(last, gc[h], 0.0), axis=0, keepdims=True) for h in heads]
    k_tail = [k[h] * jnp.exp(g_last[h] - gc[h]) for h in heads]
    v_new = [u[h] - _dot(wk[h], state[h]) for h in heads]
    o = [_dot(q[h] * eg[h], state[h]) + _dot(attn[h], v_new[h]) for h in heads]
    new = [state[h] * jnp.exp(g_last[h]) + _dot_tn(k_tail[h], v_new[h]) for h in heads]
    return (o, new, tinv) if saved_tinv is None else (o, new)


def _bg_cols(bg, h, heads):
    lane = lax.broadcasted_iota(jnp.int32, bg.shape, 1)
    beta = jnp.sum(jnp.where(lane == h, bg, 0.0), axis=1, keepdims=True)
    g = jnp.sum(jnp.where(lane == heads + h, bg, 0.0), axis=1, keepdims=True)
    return beta, g


def _dn_fwd(qkv, bg):
    rows = qkv.shape[0]
    heads = qkv.shape[1] // (3 * HEAD_A)
    n = rows // CHUNK
    hp, groups = heads, 1

    def body(q_ref, k_ref, v_ref, bg_ref, o_ref, hist_ref, tinv_ref, s_ref):
        c, grp = pl.program_id(0), pl.program_id(1)

        @pl.when(c == 0)
        def _():
            for i in range(hp):
                s_ref[grp * hp + i] = jnp.zeros((HEAD_A, HEAD_A), F32)

        bg_v = bg_ref[...]
        cols = [slice(i * HEAD_A, (i + 1) * HEAD_A) for i in range(hp)]
        state = [s_ref[grp * hp + i] for i in range(hp)]
        beta_g = [_bg_cols(bg_v, grp * hp + i, heads) for i in range(hp)]
        o, new, tinv = _dn_chunk([q_ref[:, c_] for c_ in cols], [k_ref[:, c_] for c_ in cols],
                                 [v_ref[:, c_] for c_ in cols], [b for b, _ in beta_g], [g for _, g in beta_g], state)
        for i in range(hp):
            hist_ref[0, i] = state[i]
            tinv_ref[0, i] = tinv[i]
            o_ref[:, cols[i]] = o[i]
            s_ref[grp * hp + i] = new[i]

    def part(p):
        return pl.BlockSpec((CHUNK, hp * HEAD_A), lambda c, grp: (c, p * groups + grp))

    return pl.pallas_call(
        body, name="deltanet_fwd", grid=(n, groups),
        in_specs=[part(0), part(1), part(2), pl.BlockSpec((CHUNK, 128), lambda c, grp: (c, 0))],
        out_specs=[part(0), pl.BlockSpec((1, hp, HEAD_A, HEAD_A), lambda c, grp: (c, grp, 0, 0)),
                   pl.BlockSpec((1, hp, CHUNK, CHUNK), lambda c, grp: (c, grp, 0, 0))],
        out_shape=[jax.ShapeDtypeStruct((rows, heads * HEAD_A), F32),
                   jax.ShapeDtypeStruct((n, heads, HEAD_A, HEAD_A), F32),
                   jax.ShapeDtypeStruct((n, heads, CHUNK, CHUNK), F32)],
        scratch_shapes=[pltpu.VMEM((heads, HEAD_A, HEAD_A), F32)],
        compiler_params=_cp("arbitrary", "arbitrary"),
    )(qkv, qkv, qkv, bg)


def _dn_bwd(qkv, bg, hist, tinv_hist, do):
    rows = qkv.shape[0]
    heads = qkv.shape[1] // (3 * HEAD_A)
    n = rows // CHUNK
    hp, groups = heads, 1

    def body(q_ref, k_ref, v_ref, bg_ref, hist_ref, tinv_ref, do_ref, dqkv_ref, dbg_ref, ds_ref):
        c, grp = pl.program_id(0), pl.program_id(1)

        @pl.when(c == 0)
        def _():
            for i in range(hp):
                ds_ref[grp * hp + i] = jnp.zeros((HEAD_A, HEAD_A), F32)

        bg_v = bg_ref[...]
        lane = lax.broadcasted_iota(jnp.int32, (CHUNK, 128), 1)
        cols = [slice(i * HEAD_A, (i + 1) * HEAD_A) for i in range(hp)]
        beta_g = [_bg_cols(bg_v, grp * hp + i, heads) for i in range(hp)]
        _, vjp = jax.vjp(_dn_chunk, [q_ref[:, c_] for c_ in cols], [k_ref[:, c_] for c_ in cols],
                         [v_ref[:, c_] for c_ in cols], [b for b, _ in beta_g], [g for _, g in beta_g],
                         [hist_ref[0, i] for i in range(hp)], [tinv_ref[0, i] for i in range(hp)])
        dq, dk, dv, dbeta, dg, ds, _ = vjp(([do_ref[:, c_] for c_ in cols], [ds_ref[grp * hp + i] for i in range(hp)]))
        dbg = jnp.zeros((CHUNK, 128), F32)
        for i in range(hp):
            h = grp * hp + i
            for p, part_grad in enumerate((dq, dk, dv)):
                dqkv_ref[:, pl.ds((p * heads + i) * HEAD_A, HEAD_A)] = part_grad[i]
            ds_ref[h] = ds[i]
            dbg = dbg + jnp.where(lane == h, dbeta[i], 0.0) + jnp.where(lane == heads + h, dg[i], 0.0)

        @pl.when(grp == 0)
        def _():
            dbg_ref[...] = jnp.zeros_like(dbg_ref)

        dbg_ref[...] += dbg

    def part(p):
        return pl.BlockSpec((CHUNK, hp * HEAD_A), lambda c, grp: (n - 1 - c, p * groups + grp))

    return pl.pallas_call(
        body, name="deltanet_bwd", grid=(n, groups),
        in_specs=[part(0), part(1), part(2), pl.BlockSpec((CHUNK, 128), lambda c, grp: (n - 1 - c, 0)),
                  pl.BlockSpec((1, hp, HEAD_A, HEAD_A), lambda c, grp: (n - 1 - c, grp, 0, 0)),
                  pl.BlockSpec((1, hp, CHUNK, CHUNK), lambda c, grp: (n - 1 - c, grp, 0, 0)), part(0)],
        out_specs=[pl.BlockSpec((CHUNK, 3 * heads * HEAD_A), lambda c, grp: (n - 1 - c, 0)),
                   pl.BlockSpec((CHUNK, 128), lambda c, grp: (n - 1 - c, 0))],
        out_shape=[jax.ShapeDtypeStruct(qkv.shape, F32), jax.ShapeDtypeStruct((rows, 128), F32)],
        scratch_shapes=[pltpu.VMEM((heads, HEAD_A, HEAD_A), F32)],
        compiler_params=_cp("arbitrary", "arbitrary"),
    )(qkv, qkv, qkv, bg, hist, tinv_hist, do)


def _head_mask(heads, width):
    return (lax.broadcasted_iota(jnp.int32, (heads, width), 0)
            == lax.broadcasted_iota(jnp.int32, (heads, width), 1) // HEAD_B)


def _masked_rows(mask, row):
    return jnp.where(mask, row, 0.0).astype(BF16)


def _rwkv_fwd(r, w, k, v, a, b):
    rows, width = r.shape
    heads = width // HEAD_B
    ts = SCAN_STEPS

    def body(r_ref, w_ref, k_ref, v_ref, a_ref, b_ref, y_ref, hist_ref, s_ref):
        @pl.when(pl.program_id(0) == 0)
        def _():
            s_ref[...] = jnp.zeros_like(s_ref)

        mask = _head_mask(heads, width)
        onehot = mask.astype(BF16)
        onehot2 = jnp.concatenate([onehot, onehot], axis=0)
        bd = _block_diag_ones()

        spread_v = [_dot_tn(jnp.concatenate(_hi_lo(v_ref[j]), axis=0), onehot2) for j in range(ts)]
        a_next = pltpu.roll(a_ref[...], ts - 1, 0)
        b_dot_a, k_dot_a = _segsum_many([b_ref[...] * a_next, k_ref[...] * a_next], bd)
        s = s_ref[...]
        ys = []
        for j in range(0, ts, 2):
            row, nxt = pl.ds(j, 1), pl.ds(j + 1, 1)
            hist_ref[j] = s
            sa, base = _segsum_many([((s * a_ref[row, :]).astype(BF16),),
                                     ((s * (w_ref[row, :] * a_ref[nxt, :])).astype(BF16),)], bd)
            sa_next = base + sa * b_dot_a[j:j + 1] + spread_v[j] * k_dot_a[j:j + 1]
            s = s * w_ref[row, :] + sa * b_ref[row, :] + spread_v[j] * k_ref[row, :]
            hist_ref[j + 1] = s
            ys.append(_dot_nt(_masked_rows(mask, r_ref[row, :]), s.astype(BF16)))
            s = s * w_ref[nxt, :] + sa_next * b_ref[nxt, :] + spread_v[j + 1] * k_ref[nxt, :]
            ys.append(_dot_nt(_masked_rows(mask, r_ref[nxt, :]), s.astype(BF16)))
        for j in range(ts):
            y_ref[j] = ys[j]
        s_ref[...] = s

    blk = pl.BlockSpec((ts, width), lambda i: (i, 0))
    blk3 = pl.BlockSpec((ts, heads, HEAD_B), lambda i: (i, 0, 0))
    return pl.pallas_call(
        body, name="rwkv_fwd", grid=(rows // ts,),
        in_specs=[blk, blk, blk, blk3, blk, blk],
        out_specs=[blk3, pl.BlockSpec((ts, HEAD_B, width), lambda i: (i, 0, 0)),
                   pl.BlockSpec((HEAD_B, width), lambda i: (0, 0))],
        out_shape=[jax.ShapeDtypeStruct((rows, heads, HEAD_B), F32), jax.ShapeDtypeStruct((rows, HEAD_B, width), F32),
                   jax.ShapeDtypeStruct((HEAD_B, width), F32)],
        compiler_params=_cp("arbitrary"),
    )(r, w, k, v, a, b)


def _rwkv_bwd(r, w, k, v, a, b, hist, last, dy):
    rows, width = r.shape
    heads = width // HEAD_B
    ts = SCAN_STEPS
    nb = rows // ts

    def body(r_ref, w_ref, k_ref, v_ref, a_ref, b_ref, hist_ref, last_ref, dy_ref,
             dr_ref, dw_ref, dk_ref, dv_ref, da_ref, db_ref, g_ref, after_ref):
        @pl.when(pl.program_id(0) == 0)
        def _():
            g_ref[...] = jnp.zeros_like(g_ref)
            after_ref[...] = last_ref[...]

        mask = _head_mask(heads, width)
        onehot = mask.astype(BF16)
        bd = _block_diag_ones()

        def own_lanes(x):
            return jnp.sum(jnp.where(mask, x, 0.0), axis=0, keepdims=True)

        def colsum(x):
            return jnp.sum(x, axis=0, keepdims=True)

        dy_m = [dy_ref[j].astype(BF16) for j in range(ts)]
        spread_dy = [_dot_tn(dy_m[j], onehot) for j in range(ts)]
        state_after = [hist_ref[j + 1] if j < ts - 1 else after_ref[...] for j in range(ts)]
        dr = [own_lanes(_dot(dy_m[j], state_after[j].astype(BF16))) for j in range(ts)]
        sa_m = [_dot_nt(_masked_rows(mask, a_ref[pl.ds(j, 1), :]), hist_ref[j].astype(BF16)) for j in range(ts)]
        g = g_ref[...]
        dw, dk, db, da, dv = {}, {}, {}, {}, {}
        for j in reversed(range(ts)):
            row = pl.ds(j, 1)
            sp = hist_ref[j]
            g = g + spread_dy[j] * r_ref[row, :]
            (dsa,) = _segsum_many([((g * b_ref[row, :]).astype(BF16),)], bd)
            g_b = g.astype(BF16)
            both = _dot(jnp.concatenate([v_ref[j].astype(BF16), sa_m[j].astype(BF16)], axis=0), g_b)
            dk[j], db[j] = own_lanes(both[:heads]), own_lanes(both[heads:])
            dv[j] = _dot_nt(_masked_rows(mask, k_ref[row, :]), g_b)
            dw[j] = colsum(g * sp)
            da[j] = colsum(sp * dsa)
            g = g * w_ref[row, :] + dsa * a_ref[row, :]
        g_ref[...] = g
        after_ref[...] = hist_ref[0]
        for j in range(ts):
            dv_ref[j] = dv[j]
            for ref, vals in ((dr_ref, dr), (dw_ref, dw), (dk_ref, dk), (da_ref, da), (db_ref, db)):
                ref[pl.ds(j, 1), :] = vals[j]

    blk = pl.BlockSpec((ts, width), lambda i: (nb - 1 - i, 0))
    blk3 = pl.BlockSpec((ts, heads, HEAD_B), lambda i: (nb - 1 - i, 0, 0))
    state = pl.BlockSpec((HEAD_B, width), lambda i: (0, 0))
    return pl.pallas_call(
        body, name="rwkv_bwd", grid=(nb,),
        in_specs=[blk, blk, blk, blk3, blk, blk, pl.BlockSpec((ts, HEAD_B, width), lambda i: (nb - 1 - i, 0, 0)),
                  state, blk3],
        out_specs=[blk, blk, blk, blk3, blk, blk],
        out_shape=[jax.ShapeDtypeStruct((rows, width), F32)] * 3 + [jax.ShapeDtypeStruct((rows, heads, HEAD_B), F32)]
        + [jax.ShapeDtypeStruct((rows, width), F32)] * 2,
        scratch_shapes=[pltpu.VMEM((HEAD_B, width), F32), pltpu.VMEM((HEAD_B, width), F32)],
        compiler_params=_cp("arbitrary"),
    )(r, w, k, v, a, b, hist, last, dy)


def _live(row0, shape):
    return (row0 + lax.broadcasted_iota(jnp.int32, shape, 0)) >= PAD


def _norm_fn(row0, h, gain):
    return (_rms(h, gain),)


def _norm_res_fn(row0, h, gain):
    return _rms(h, gain), h


def _make_bg_fn(heads):
    def fn(row0, x, log_rate, dt_bias):
        lane = lax.broadcasted_iota(jnp.int32, x.shape, 1)
        beta = _sigmoid(x)
        g = -jnp.exp(log_rate) * _softplus(x + dt_bias)
        out = jnp.where(lane < heads, beta, jnp.where(lane < 2 * heads, g, 0.0))
        return (jnp.where(_live(row0, x.shape), out, 0.0),)
    return fn


def _b_pre_fn(row0, zf, w0, w_up, a0, a_up, g_up, k_k, k_a):
    d = w0.shape[1]
    r, k, v = zf[:, :d], zf[:, d:2 * d], zf[:, 2 * d:3 * d]
    lo = zf[:, 3 * d:3 * d + 128]
    lg = zf[:, 3 * d + 128:3 * d + LORA_PAD]
    lane = lax.broadcasted_iota(jnp.int32, lo.shape, 1)
    lw = _dot(jnp.where(lane < LORA_W, jnp.tanh(lo), 0.0), w_up)
    la = _dot(jnp.where(lane >= LORA_W, lo, 0.0), a_up)
    lane_g = lax.broadcasted_iota(jnp.int32, lg.shape, 1)
    gate = _dot(jnp.where(lane_g < LORA_G, _sigmoid(lg), 0.0), g_up)
    decay = jnp.exp(-jnp.exp(-_softplus(-(w0 + lw)) - 0.5))
    a = _sigmoid(a0 + la)
    kx = k * k_k
    kk = kx * lax.rsqrt(_segsum64(kx * kx) + 1e-6)
    k2 = k * (1.0 + (a - 1.0) * k_a)
    return r, decay, k2, v, -kk, kk * a, gate


def _post_fn(row0, o, zg, y, r, k2, v, gate, out_gain, r_k, ln_g, ln_b):
    d = o.shape[1]
    az, ga, gb = zg[:, :d], zg[:, d:2 * d], zg[:, 2 * d:]
    heads = d // HEAD_A
    parts = []
    for h in range(heads):
        oh = o[:, h * HEAD_A:(h + 1) * HEAD_A]
        parts.append(oh * lax.rsqrt(jnp.mean(oh * oh, axis=-1, keepdims=True) + EPS) * out_gain)
    o_a = jnp.concatenate(parts, axis=1) * _silu(az)
    mean = _segsum64(y) * (1.0 / HEAD_B)
    yc = y - mean
    var = _segsum64(yc * yc) * (1.0 / HEAD_B)
    yn = yc * lax.rsqrt(var + GN_EPS) * ln_g + ln_b
    o_b = (yn + _segsum64(r * k2 * r_k) * v) * gate
    return (_sigmoid(ga) * o_a + _sigmoid(gb) * o_b,)


def _loss(h3, target, gain, tb):
    rows, d = h3.shape

    def body(h_ref, t_ref, g_ref, dh_ref, dg_ref, l_ref):
        i = pl.program_id(0)
        live = (i * tb + lax.broadcasted_iota(jnp.int32, (tb, 1), 0)) >= CHUNK
        tgt = t_ref[...]

        def f(h, g):
            err = _rms(h, g) - tgt
            return 0.5 * jnp.sum(jnp.where(live, jnp.mean(err * err, axis=-1, keepdims=True), 0.0))

        val, vjp = jax.vjp(f, h_ref[...], g_ref[...])
        dh, dg = vjp(jnp.ones((), F32))
        dh_ref[...] = dh

        @pl.when(i == 0)
        def _():
            dg_ref[...] = jnp.zeros_like(dg_ref)
            l_ref[...] = jnp.zeros_like(l_ref)

        dg_ref[...] += dg
        l_ref[...] += jnp.full((1, 128), val, F32)

    blk = pl.BlockSpec((tb, d), lambda i: (i, 0))
    return pl.pallas_call(
        body, name="loss", grid=(rows // tb,),
        in_specs=[blk, blk, pl.BlockSpec((1, d), lambda i: (0, 0))],
        out_specs=[blk, pl.BlockSpec((1, d), lambda i: (0, 0)), pl.BlockSpec((1, 128), lambda i: (0, 0))],
        out_shape=[jax.ShapeDtypeStruct((rows, d), F32), jax.ShapeDtypeStruct((1, d), F32),
                   jax.ShapeDtypeStruct((1, 128), F32)],
        compiler_params=_cp("arbitrary"),
    )(h3, target, gain)


def _adamw_math(w, g, m, v):
    m2 = ADAM_B1 * m + (1.0 - ADAM_B1) * g
    v2 = ADAM_B2 * v + (1.0 - ADAM_B2) * (g * g)
    m_hat = m2 / (1.0 - ADAM_B1 ** ADAM_STEP)
    v_hat = v2 / (1.0 - ADAM_B2 ** ADAM_STEP)
    return -ADAM_LR * (m_hat / (jnp.sqrt(v_hat) + ADAM_EPS) + ADAM_WD * w), m2, v2


def _adamw(name, own, landed, w, m, v):
    rows, cols = w.shape
    if rows % 16 == 0:
        rb = _tb(rows, 128)
        grid, blk = (rows // rb,), pl.BlockSpec((rb, cols), lambda i: (i, 0))
        landed_blk = pl.BlockSpec((N_DEV - 1, rb, cols), lambda i: (0, i, 0))
    else:
        grid, blk = (cols // 128,), pl.BlockSpec((rows, 128), lambda i: (0, i))
        landed_blk = pl.BlockSpec((N_DEV - 1, rows, 128), lambda i: (0, 0, i))

    def body(o_ref, s_ref, w_ref, m_ref, v_ref, g_ref, d_ref, m2_ref, v2_ref):
        g = o_ref[...].astype(F32)
        for peer in range(N_DEV - 1):
            g = g + s_ref[peer].astype(F32)
        g_ref[...] = g
        d_ref[...], m2_ref[...], v2_ref[...] = _adamw_math(w_ref[...], g, m_ref[...], v_ref[...])

    return pl.pallas_call(
        body, name=name, grid=grid, in_specs=[blk, landed_blk, blk, blk, blk],
        out_specs=[blk] * 4, out_shape=[jax.ShapeDtypeStruct((rows, cols), F32)] * 4,
        compiler_params=_cp("parallel"),
    )(own, landed, w, m, v)


def _sum_slabs(name, slabs, rb):
    _, rows, cols = slabs.shape

    def body(s_ref, o_ref):
        g = s_ref[0]
        for dev in range(1, N_DEV):
            g = g + s_ref[dev]
        o_ref[...] = g

    return pl.pallas_call(
        body, name=name, grid=(rows // rb,),
        in_specs=[pl.BlockSpec((N_DEV, rb, cols), lambda i: (0, i, 0))],
        out_specs=pl.BlockSpec((rb, cols), lambda i: (i, 0)),
        out_shape=jax.ShapeDtypeStruct((rows, cols), F32), compiler_params=_cp("parallel"),
    )(slabs)


def _adamw_small(w, g, m, v):
    def body(w_ref, g_ref, m_ref, v_ref, d_ref, m2_ref, v2_ref):
        d_ref[...], m2_ref[...], v2_ref[...] = _adamw_math(w_ref[...], g_ref[...], m_ref[...], v_ref[...])

    return pl.pallas_call(body, name="adamw_small", out_shape=[jax.ShapeDtypeStruct(w.shape, F32)] * 3)(w, g, m, v)


def _place():
    return lax.axis_index("x"), lax.axis_index("y"), lax.axis_index("c")


def _index(p):
    return 4 * p[0] + 2 * p[1] + p[2]


def _all_gather(name, xs):
    n = len(xs)

    def body(*refs):
        x_refs, o_refs = refs[:n], refs[n:2 * n]
        send_sems, recv_sems, local_sems = refs[2 * n:]
        x, y, c = _place()
        me, sibling = (x, y, c), (x, y, 1 - c)
        chips = [(1 - x, y), (x, 1 - y), (1 - x, 1 - y)]

        def copy(i, k, block, to, src=None):
            dst = o_refs[i].at[_index(block)]
            return pltpu.make_async_remote_copy(src_ref=dst if src is None else src, dst_ref=dst,
                                                send_sem=send_sems.at[i, k], recv_sem=recv_sems.at[i, k],
                                                device_id=to, device_id_type=MESH_ID)

        mine = [pltpu.make_async_copy(x_refs[i], o_refs[i].at[_index(me)], local_sems.at[i]) for i in range(n)]
        for cp in mine:
            cp.start()
        first = []
        for i in range(n):
            first.append(copy(i, 0, me, sibling, src=x_refs[i]))
            first += [copy(i, 1 + j, me, (*chip, c), src=x_refs[i]) for j, chip in enumerate(chips)]
        for cp in first:
            cp.start()
        passed = []
        for j, chip in enumerate(chips):
            for i in range(n):
                copy(i, 1 + j, (*chip, c), me).wait_recv()
                cp = copy(i, 4 + j, (*chip, c), sibling)
                cp.start()
                passed.append(cp)
        for i in range(n):
            copy(i, 0, sibling, me).wait_recv()
            for j, chip in enumerate(chips):
                copy(i, 4 + j, (*chip, 1 - c), me).wait_recv()
        for cp in first + passed:
            cp.wait_send()
        for cp in mine:
            cp.wait()

    return pl.pallas_call(
        body, name=name, in_specs=[ANY] * n, out_specs=[ANY] * n,
        out_shape=[jax.ShapeDtypeStruct((N_DEV,) + x.shape, x.dtype) for x in xs],
        scratch_shapes=[pltpu.SemaphoreType.DMA((n, 7)), pltpu.SemaphoreType.DMA((n, 7)), pltpu.SemaphoreType.DMA((n,))],
    )(*xs)


def _exchange_start(name, xs, after=None, gather=False):
    n = len(xs)
    copies = n * (N_DEV - 1)
    extra = [] if after is None else [after]

    def body(*refs):
        x_refs, land_refs = refs[:n], refs[n:2 * n]
        sems = refs[2 * n + len(extra):2 * n + len(extra) + 2 * copies]
        token = refs[-1]
        for i, k, peer in _exchange_copies(n):
            _exchange_copy(x_refs, land_refs, sems, i, k, peer, gather).start()
        token[...] = jnp.zeros_like(token)

    lands = [lax.empty((N_DEV,) + x.shape if gather else (N_DEV - 1,) + x.shape[1:], x.dtype) for x in xs]
    out = pl.pallas_call(
        body, name=name,
        out_shape=(*[pltpu.SemaphoreType.DMA(())] * (2 * copies), *[pltpu.HBM(x.shape, x.dtype) for x in xs],
                   *[pltpu.HBM(l.shape, l.dtype) for l in lands], jax.ShapeDtypeStruct((8, 128), F32)),
        in_specs=[HBM_SPEC] * (2 * n) + [ANY] * len(extra),
        out_specs=(*[SEM_SPEC] * (2 * copies), *[HBM_SPEC] * (2 * n), pl.BlockSpec(memory_space=pltpu.VMEM)),
        input_output_aliases={i: 2 * copies + i for i in range(2 * n)},
        compiler_params=pltpu.CompilerParams(has_side_effects=pltpu.SideEffectType.DATAFLOW_SIDE_EFFECTING),
    )(*[pltpu.with_memory_space_constraint(a, pltpu.HBM) for a in list(xs) + lands], *extra)
    sems, rest = list(out[:2 * copies]), out[2 * copies:]
    return sems, list(rest[:n]), list(rest[n:2 * n]), rest[-1]


def _exchange_copies(n):
    x, y, c = _place()
    for k in range(1, N_DEV):
        peer = ((1 - x) if k & 4 else x, (1 - y) if k & 2 else y, (1 - c) if k & 1 else c)
        for i in range(n):
            yield i, k - 1, peer


def _exchange_copy(x_refs, land_refs, sems, i, k, peer, gather, arriving=False):
    copies = len(sems) // 2
    which = i * (N_DEV - 1) + k
    src = x_refs[i] if gather else x_refs[i].at[_index(peer)]
    dst = land_refs[i].at[_index(peer if arriving else _place())] if gather else land_refs[i].at[k]
    return pltpu.make_async_remote_copy(src_ref=src, dst_ref=dst, send_sem=sems[which], recv_sem=sems[copies + which],
                                        device_id=peer, device_id_type=MESH_ID)


def _exchange_wait(name, sems, xs, lands, after, gather=False):
    n = len(xs)

    def body(*refs):
        x_refs, land_refs = refs[:n], refs[n:2 * n]
        sem_refs = refs[2 * n:2 * n + len(sems)]
        for i, k, peer in _exchange_copies(n):
            _exchange_copy(x_refs, land_refs, sem_refs, i, k, peer, gather).wait_send()
            _exchange_copy(x_refs, land_refs, sem_refs, i, k, peer, gather, arriving=True).wait_recv()

    out = pl.pallas_call(
        body, name=name,
        out_shape=(*[pltpu.HBM(x.shape, x.dtype) for x in xs], *[pltpu.HBM(l.shape, l.dtype) for l in lands]),
        in_specs=[HBM_SPEC] * (2 * n) + [SEM_SPEC] * len(sems) + [ANY], out_specs=tuple([HBM_SPEC] * (2 * n)),
        input_output_aliases={i: i for i in range(2 * n)},
        compiler_params=pltpu.CompilerParams(has_side_effects=pltpu.SideEffectType.DATAFLOW_SIDE_EFFECTING),
    )(*xs, *lands, *sems, after)
    return list(out[:n]), list(out[n:])


def _pack(arrays):
    flat = jnp.concatenate([a.reshape(-1) for a in arrays])
    pad = (-flat.shape[0]) % 1024
    return jnp.pad(flat, (0, pad)).reshape(-1, 128)


def _unpack(packed, shapes):
    flat = packed.reshape(-1)
    out, pos = [], 0
    for s in shapes:
        size = 1
        for dim in s:
            size *= dim
        out.append(flat[pos:pos + size].reshape(s))
        pos += size
    return out


def _cols_from_slabs(stack):
    return jnp.transpose(stack, (1, 0, 2)).reshape(stack.shape[1], -1)


def kernel(x, meta_tokens, ffn1_norm, ffn1_w_gu, ffn1_w_down, mix_norm, w_in, a_conv_w, a_log_rate, a_dt_bias, a_out_norm, b_shift_mu, b_w0, b_w_up, b_a0, b_a_up, b_g_up, b_k_k, b_k_a, b_r_k, b_ln_gain, b_ln_bias, w_out, ffn2_norm, ffn2_w_gu, ffn2_w_down, final_norm, loss_target, m_meta_tokens, m_ffn1_norm, m_ffn1_w_gu, m_ffn1_w_down, m_mix_norm, m_w_in, m_a_conv_w, m_a_log_rate, m_a_dt_bias, m_a_out_norm, m_b_shift_mu, m_b_w0, m_b_w_up, m_b_a0, m_b_a_up, m_b_g_up, m_b_k_k, m_b_k_a, m_b_r_k, m_b_ln_gain, m_b_ln_bias, m_w_out, m_ffn2_norm, m_ffn2_w_gu, m_ffn2_w_down, m_final_norm, v_meta_tokens, v_ffn1_norm, v_ffn1_w_gu, v_ffn1_w_down, v_mix_norm, v_w_in, v_a_conv_w, v_a_log_rate, v_a_dt_bias, v_a_out_norm, v_b_shift_mu, v_b_w0, v_b_w_up, v_b_a0, v_b_a_up, v_b_g_up, v_b_k_k, v_b_k_a, v_b_r_k, v_b_ln_gain, v_b_ln_bias, v_w_out, v_ffn2_norm, v_ffn2_w_gu, v_ffn2_w_down, v_final_norm):
    names = ['meta_tokens', 'ffn1_norm', 'ffn1_w_gu', 'ffn1_w_down', 'mix_norm', 'w_in', 'a_conv_w', 'a_log_rate',
             'a_dt_bias', 'a_out_norm', 'b_shift_mu', 'b_w0', 'b_w_up', 'b_a0', 'b_a_up', 'b_g_up', 'b_k_k', 'b_k_a',
             'b_r_k', 'b_ln_gain', 'b_ln_bias', 'w_out', 'ffn2_norm', 'ffn2_w_gu', 'ffn2_w_down', 'final_norm']
    env = dict(locals())
    wts = {k: env[k] for k in names}
    mom_m = {k: env['m_' + k] for k in names}
    mom_v = {k: env['v_' + k] for k in names}
    big = ['ffn1_w_gu', 'ffn1_w_down', 'w_in', 'w_out', 'ffn2_w_gu', 'ffn2_w_down']
    col_sharded = {'ffn1_w_gu', 'w_in', 'ffn2_w_gu'}
    shard_of = lambda tree, k: tree[k][0].T if k in col_sharded else tree[k][0]
    small_sharded = ['meta_tokens', 'a_conv_w', 'b_w_up', 'b_a_up', 'b_g_up']
    replicated = [k for k in names if k not in big and k not in small_sharded]

    seq, d = x.shape[1], x.shape[2]
    rows = PAD + N_META + seq
    heads_a = d // HEAD_A
    tb_mm = _tb(rows, 416)
    tb_vjp = _tb(rows, 208)
    tb_dw = _tb(rows, 2080)
    tb_ffn = _tb(rows, 832)
    me = _index(_place())

    local_bf = {k: shard_of(wts, k).astype(BF16) for k in big}
    gu1, down1, meta = _all_gather("gather_ffn1", [local_bf['ffn1_w_gu'], local_bf['ffn1_w_down'], wts['meta_tokens']])
    small_rest = small_sharded[1:]
    late_keys = ['w_out', 'ffn2_w_gu', 'ffn2_w_down']
    gather_mid = _exchange_start("gather_start_mid", [local_bf['w_in'], _pack([wts[k][0] for k in small_rest])],
                                 after=gu1, gather=True)
    gather_late = _exchange_start("gather_start_late", [local_bf[k] for k in late_keys], after=gather_mid[-1],
                                  gather=True)

    def finish_gather(tag, started, after):
        sems, mine, lands, _ = started
        mine, lands = _exchange_wait("gather_wait_" + tag, sems, mine, lands, after, gather=True)
        return [lax.dynamic_update_index_in_dim(land, own[None], me, 0) for land, own in zip(lands, mine)]

    full = {'ffn1_w_gu': gu1, 'ffn1_w_down': down1.reshape(-1, d), 'meta_tokens': _cols_from_slabs(meta)}
    for k in replicated:
        full[k] = wts[k].reshape(1, -1)

    h0 = jnp.concatenate([jnp.zeros((PAD, d), F32) + gather_late[-1][:1, :1], full['meta_tokens'], x[0]], axis=0)
    h1 = _ffn_fwd("ffn1_fwd", h0, full['ffn1_norm'], full['ffn1_w_gu'], full['ffn1_w_down'], tb_ffn)
    (u,) = _tok_fwd("mix_norm_fwd", _norm_fn, [h1], [full['mix_norm']], [(d, BF16)], tb_mm)

    win_stack, small_stack = finish_gather("mid", gather_mid, u)
    full['w_in'] = win_stack.reshape(-1, d)
    small_flat, pos = small_stack.reshape(N_DEV, -1), 0
    for k in small_rest:
        shape = wts[k][0].shape
        full[k] = _cols_from_slabs(small_flat[:, pos:pos + shape[0] * shape[1]].reshape((N_DEV,) + shape))
        pos += shape[0] * shape[1]

    win = full['w_in']
    n_b = 3 * d + LORA_W + LORA_A + LORA_G
    off_beta, off_b = 4 * d, 4 * d + 2 * heads_a
    off_ga = off_b + n_b
    b_width = 3 * d + LORA_PAD
    zrows = lambda r: jnp.zeros((r, d), BF16)
    w_qkv = win[:3 * d]
    w_zg = jnp.concatenate([win[3 * d:4 * d], win[off_ga:off_ga + 2 * d]], axis=0)
    w_b = jnp.concatenate([win[off_b:off_b + n_b], zrows(b_width - n_b)], axis=0)
    w_bg = jnp.concatenate([win[off_beta:off_beta + 2 * heads_a], zrows(128 - 2 * heads_a)], axis=0)

    def lanes(vec, start, width):
        return jnp.pad(vec.reshape(1, -1), ((0, 0), (start, width - start - vec.size)))

    log_rate = lanes(wts['a_log_rate'], heads_a, 128)
    dt_bias = lanes(wts['a_dt_bias'], heads_a, 128)
    mu = lanes(wts['b_shift_mu'], 0, b_width)
    w_up = jnp.pad(full['b_w_up'], ((0, 128 - LORA_W), (0, 0)))
    a_up = jnp.pad(full['b_a_up'], ((LORA_W, 0), (0, 0)))
    g_up = jnp.pad(full['b_g_up'], ((0, 256 - LORA_G), (0, 0)))
    b_pars = [full['b_w0'], w_up, full['b_a0'], a_up, g_up, full['b_k_k'], full['b_k_a']]
    post_pars = [full['a_out_norm'], full['b_r_k'], full['b_ln_gain'], full['b_ln_bias']]
    bg_fn = _make_bg_fn(heads_a)

    z_qkv = _mm("in_qkv", u, w_qkv, trans_b=True, tb=tb_ffn, tn=3 * d)
    z_zg = _mm("in_zg", u, w_zg, trans_b=True, tb=tb_ffn, tn=3 * d)
    z_b = _mm("in_b", u, w_b, trans_b=True, tb=tb_ffn, tn=_col_tile(b_width, 1536))
    z_bg = _mm("in_bg", u, w_bg, trans_b=True, tb=tb_ffn, tn=128)
    qkv = _a_pre_fwd(z_qkv, full['a_conv_w'])
    (bg,) = _tok_fwd("bg_fwd", bg_fn, [z_bg], [log_rate, dt_bias], [(128, F32)], tb_mm)
    o_dn, dn_hist, dn_tinv = _dn_fwd(qkv, bg)
    zf = _shift_fwd(z_b, mu)
    rr, ww, kk2, vv, av, bv, gate = _tok_fwd("b_pre_fwd", _b_pre_fn, [zf], b_pars, [(d, F32)] * 7, tb_vjp)
    per_head = lambda t: t.reshape(rows, d // HEAD_B, HEAD_B)
    y_heads, b_hist, b_last = _rwkv_fwd(rr, ww, kk2, per_head(vv), av, bv)
    y_b = y_heads.reshape(rows, d)
    w_out_stack, full['ffn2_w_gu'], down2 = finish_gather("late", gather_late, y_heads)
    full['w_out'], full['ffn2_w_down'] = w_out_stack.reshape(-1, d), down2.reshape(-1, d)
    post_toks = [o_dn, z_zg, y_b, rr, kk2, vv, gate]
    (merged,) = _tok_fwd("post_fwd", _post_fn, post_toks, post_pars, [(d, BF16)], tb_vjp)
    h2 = _mm("out_proj", merged, full['w_out'], add=h1, tb=tb_ffn, tn=d)
    h3 = _ffn_fwd("ffn2_fwd", h2, full['ffn2_norm'], full['ffn2_w_gu'], full['ffn2_w_down'], tb_ffn)

    target = jnp.pad(loss_target[0], ((CHUNK, 0), (0, 0)))
    dh3, g_final, loss_part = _loss(h3, target, full['final_norm'].reshape(1, d), tb_vjp)

    def ffn_backward(tag, h, dout, key_norm, key_gu, key_down, once_down=None):
        dh, dh_bf, dgain, xn, act, dgate, dup, dhalf = _ffn_bwd(tag + "_bwd", h, full[key_norm], dout, full[key_gu],
                                                                full[key_down], tb_mm)
        fc = dgate.shape[2]
        d_down = _mm_tn_from_slabs(tag + "_dw_down", act, dhalf, tk=tb_dw).reshape(N_DEV, -1, d)
        token = once_down(dh, dgain, d_down) if once_down else None
        d_gu = jnp.concatenate(
            [_mm_tn_from_slabs(tag + "_dw_gate", dgate, xn, tk=tb_dw, after=token).reshape(-1, fc, d),
             _mm_tn_from_slabs(tag + "_dw_up", dup, xn, tk=tb_dw).reshape(-1, fc, d)], axis=0)
        return dh, dh_bf, dgain, d_gu, d_down

    dh2, dh2_bf, g_ffn2_norm, g_ffn2_gu, g_ffn2_down = ffn_backward("ffn2", h2, dh3, 'ffn2_norm', 'ffn2_w_gu',
                                                                    'ffn2_w_down')
    g_w_out = _mm_tn("dw_out", merged, dh2_bf, tm=d, tn=d, tk=tb_dw).reshape(N_DEV, -1, d)

    def start_exchange(tag, keys, slabs, after=None):
        sems, kept, lands, token = _exchange_start("exchange_start_" + tag, slabs, after)
        return (tag, keys, sems, kept, lands), token

    ex_ffn2, token_ffn2 = start_exchange("ffn2", ['ffn2_w_gu', 'ffn2_w_down', 'w_out'], [g_ffn2_gu, g_ffn2_down, g_w_out])
    dmerged = _mm("d_merged", dh2_bf, full['w_out'], trans_b=True, after=token_ffn2, tb=tb_ffn, tn=d)
    post_grads = _tok_bwd("post_bwd", _post_fn, post_toks, post_pars, [[dmerged]], list(range(7)), tb_vjp,
                          [F32, BF16] + [F32] * 5)
    do_dn, dz_zg, dy_b, dr1, dk1, dv1, dgate = post_grads[:7]
    g_out_norm, g_r_k, g_ln_g, g_ln_b = post_grads[7:]
    dr2, dw2, dk2, dv_heads, da2, db2 = _rwkv_bwd(rr, ww, kk2, per_head(vv), av, bv, b_hist, b_last, per_head(dy_b))
    dv2 = dv_heads.reshape(rows, d)
    b_grads = _tok_bwd("b_pre_bwd", _b_pre_fn, [zf], b_pars,
                       [[dr1, dr2], [dw2], [dk1, dk2], [dv1, dv2], [da2], [db2], [dgate]], [0], tb_vjp)
    dzf = b_grads[0]
    g_w0, g_w_up, g_a0, g_a_up, g_g_up, g_k_k, g_k_a = b_grads[1:]
    dz_b, g_mu = _shift_bwd(z_b, mu, dzf)
    dqkv, dbg = _dn_bwd(qkv, bg, dn_hist, dn_tinv, do_dn)
    dz_qkv, g_conv = _a_pre_bwd(z_qkv, full['a_conv_w'], dqkv)
    dz_bg, g_log_rate, g_dt_bias = _tok_bwd("bg_bwd", bg_fn, [z_bg], [log_rate, dt_bias], [[dbg]], [0], tb_mm, [BF16])

    small_early = {
        'a_conv_w': g_conv, 'a_log_rate': g_log_rate[:, heads_a:2 * heads_a],
        'a_dt_bias': g_dt_bias[:, heads_a:2 * heads_a], 'a_out_norm': g_out_norm, 'b_shift_mu': g_mu[:, :n_b],
        'b_w0': g_w0, 'b_w_up': g_w_up[:LORA_W], 'b_a0': g_a0, 'b_a_up': g_a_up[LORA_W:], 'b_g_up': g_g_up[:LORA_G],
        'b_k_k': g_k_k, 'b_k_a': g_k_a, 'b_r_k': g_r_k, 'b_ln_gain': g_ln_g, 'b_ln_bias': g_ln_b,
        'ffn2_norm': g_ffn2_norm, 'final_norm': g_final,
    }
    gather_small = _exchange_start("gather_start_small", [_pack(list(small_early.values()))], gather=True)

    du = None
    g_w_in_parts = []
    for tag, dz, wpiece in (("qkv", dz_qkv, w_qkv), ("zg", dz_zg, w_zg), ("b", dz_b, w_b), ("bg", dz_bg, w_bg)):
        du = _mm("du_" + tag, dz, wpiece, add=du, after=gather_small[-1] if du is None else None, tb=tb_ffn, tn=d)
        g_w_in_parts.append(_mm_tn("dw_in_" + tag, dz, u, tm=_col_tile(dz.shape[1], 1536), tn=d, tk=tb_dw))
    gp_qkv, gp_zg, gp_b, gp_bg = g_w_in_parts
    g_w_in = jnp.concatenate([gp_qkv, gp_zg[:d], gp_bg[:2 * heads_a], gp_b[:n_b], gp_zg[d:]],
                             axis=0).reshape(N_DEV, -1, d)
    ex_w_in, token_w_in = start_exchange("w_in", ['w_in'], [g_w_in])
    dh1, g_mix_norm = _tok_bwd("mix_norm_bwd", _norm_res_fn, [h1], [full['mix_norm']], [[du], [dh2]], [0], tb_vjp,
                               after=token_w_in)
    tail = {}

    def once_ffn1_down(dh0, g_ffn1_norm, g_ffn1_down):
        tail['late'] = {'meta_tokens': dh0[PAD:CHUNK], 'ffn1_norm': g_ffn1_norm, 'mix_norm': g_mix_norm}
        tail['packed'] = _pack(list(tail['late'].values()) + [loss_part[:, :1]])
        (tail['parts'],) = _all_gather("gather_small_grads", [tail['packed']])
        tail['ex_down'], token = start_exchange("ffn1_down", ['ffn1_w_down'], [g_ffn1_down], after=tail['parts'])
        return token

    dh0, _, _, g_ffn1_gu, _ = ffn_backward("ffn1", h0, dh1, 'ffn1_norm', 'ffn1_w_gu', 'ffn1_w_down', once_ffn1_down)
    ex_ffn1_gu, _ = start_exchange("ffn1_gu", ['ffn1_w_gu'], [g_ffn1_gu])
    small_late, packed_late, late_parts = tail['late'], tail['packed'], tail['parts']
    (early_parts,) = finish_gather("small", gather_small, late_parts)
    pieces = (_unpack(_sum_slabs("sum_small_early", early_parts, early_parts.shape[1]),
                      [g.shape for g in small_early.values()])
              + _unpack(_sum_slabs("sum_small_grads", late_parts, packed_late.shape[0]),
                        [g.shape for g in small_late.values()] + [(1, 1)]))
    small_names = list(small_early) + list(small_late)
    small_grad = dict(zip(small_names, pieces[:-1]))
    loss = pieces[-1].reshape(())

    grads, deltas, new_m, new_v = {}, {}, {}, {}
    local_small = {}
    for k in small_names:
        g = small_grad[k]
        if k in small_sharded:
            width = wts[k].shape[-1]
            g = lax.dynamic_slice_in_dim(g, me * width, width, axis=1)
        local_small[k] = g.reshape(wts[k].shape)
    pk = lambda tree: _pack([tree[k] for k in small_names])
    dl_s, m_s, v_s = _adamw_small(pk(wts), pk(local_small), pk(mom_m), pk(mom_v))
    shapes = [wts[k].shape for k in small_names]
    for k, dl, m2, v2 in zip(small_names, _unpack(dl_s, shapes), _unpack(m_s, shapes), _unpack(v_s, shapes)):
        grads[k], deltas[k], new_m[k], new_v[k] = local_small[k], dl, m2, v2

    done = dl_s
    for tag, keys, sems, kept, lands in (ex_ffn2, ex_w_in, tail['ex_down'], ex_ffn1_gu):
        kept, lands = _exchange_wait("exchange_wait_" + tag, sems, kept, lands, done)
        for k, slabs, landed in zip(keys, kept, lands):
            own = lax.dynamic_index_in_dim(slabs, me, axis=0, keepdims=False)
            res = _adamw("adamw_" + k, own, landed, shard_of(wts, k), shard_of(mom_m, k), shard_of(mom_v, k))
            done = res[1]
            res = [(t.T if k in col_sharded else t)[None] for t in res]
            grads[k], deltas[k], new_m[k], new_v[k] = res

    grad_x = dh0[CHUNK:][None]
    return (loss, grad_x, *[grads[k] for k in names], *[deltas[k] for k in names],
            *[new_m[k] for k in names], *[new_v[k] for k in names])
```
